```python
import jax, jax.numpy as jnp
from jax import lax
import numpy as np

D_MODEL = 1024
BATCH = 8
SEQ = 8192
DEPTH = 1

HEAD_DIM = 64
N_HEADS_FOX = D_MODEL // (2 * HEAD_DIM)
N_HEADS_SB = D_MODEL // (2 * HEAD_DIM)
D_FOX = N_HEADS_FOX * HEAD_DIM
D_SB = N_HEADS_SB * HEAD_DIM
D_MIX = D_FOX + D_SB
N_IN = 3 * D_FOX + 3 * D_SB + N_HEADS_FOX
BLOCK_Q = 128
D_FF = int(round(8 * D_MODEL / 3 / 64)) * 64
CONV_WIDTH = 3
N_MOD = 6
EPS = 1e-6

kernel_name = 'hybrid_fox_stickbreaking_convffn_adaln'


def rms_norm(x, g):
    xf = x.astype(jnp.float32)
    y = xf * lax.rsqrt(jnp.mean(xf * xf, axis=-1, keepdims=True) + EPS)
    return (y * g.astype(jnp.float32)).astype(x.dtype)


def split_heads(t, n_heads):
    B, S, _ = t.shape
    return t.reshape(B, S, n_heads, HEAD_DIM).transpose(0, 2, 1, 3)


def head_rms_norm(o, g):
    B, H, S, Dh = o.shape
    o = o.transpose(0, 2, 1, 3)
    return rms_norm(o, g.reshape(H, Dh)).reshape(B, S, H * Dh)


def to_blocks(t):
    B, H, S = t.shape[:3]
    nb = S // BLOCK_Q
    t = t.reshape((B, H, nb, BLOCK_Q) + t.shape[3:])
    return jnp.moveaxis(t, 2, 0)


def from_blocks(t):
    nb, B, H, bq, Dh = t.shape
    return jnp.moveaxis(t, 0, 2).reshape(B, H, nb * bq, Dh)


def forgetting_attention(q, k, v, log_f):
    S = q.shape[2]
    scale = HEAD_DIM ** -0.5
    F = jnp.cumsum(log_f.astype(jnp.float32), axis=-1)
    kpos = jnp.arange(S)
    nb = S // BLOCK_Q

    def one_block(args):
        i, q_blk, F_blk = args
        qpos = i * BLOCK_Q + jnp.arange(BLOCK_Q)
        s = jnp.einsum('bhqd,bhkd->bhqk', q_blk, k, preferred_element_type=jnp.float32) * scale
        s = s + F_blk[..., None] - F[:, :, None, :]
        causal = kpos[None, :] <= qpos[:, None]
        s = jnp.where(causal, s, -jnp.inf)
        p = jax.nn.softmax(s, axis=-1)
        return jnp.einsum('bhqk,bhkd->bhqd', p.astype(v.dtype), v)

    out = lax.map(one_block, (jnp.arange(nb), to_blocks(q), to_blocks(F)))
    return from_blocks(out)


def stick_breaking_attention(q, k, v):
    S = q.shape[2]
    scale = HEAD_DIM ** -0.5
    kpos = jnp.arange(S)
    nb = S // BLOCK_Q

    def one_block(args):
        i, q_blk = args
        qpos = i * BLOCK_Q + jnp.arange(BLOCK_Q)
        z = jnp.einsum('bhqd,bhkd->bhqk', q_blk, k, preferred_element_type=jnp.float32) * scale
        strict = kpos[None, :] < qpos[:, None]
        log_one_minus_beta = jnp.where(strict, jax.nn.log_sigmoid(-z), 0.0)
        rest = lax.cumsum(log_one_minus_beta, axis=3, reverse=True) - log_one_minus_beta
        log_a = jax.nn.log_sigmoid(z) + rest
        a = jnp.where(strict, jnp.exp(log_a), 0.0)
        return jnp.einsum('bhqk,bhkd->bhqd', a.astype(v.dtype), v)

    out = lax.map(one_block, (jnp.arange(nb), to_blocks(q)))
    return from_blocks(out)


def causal_depthwise_conv(u, w, b):
    C = u.shape[-1]
    y = lax.conv_general_dilated(
        u, w.astype(u.dtype).reshape(CONV_WIDTH, 1, C),
        window_strides=(1,), padding=[(CONV_WIDTH - 1, 0)],
        dimension_numbers=('NWC', 'WIO', 'NWC'), feature_group_count=C)
    return y + b.astype(u.dtype)


def _fwd_setup_inputs(seed: int = 0) -> dict:
    key = jax.random.key(seed)
    ks = jax.random.split(key, 16)
    L, D = DEPTH, D_MODEL
    f32 = jnp.float32

    def nrm(k, shape, s):
        return jax.random.normal(k, shape, f32) * s

    return {
        'x': nrm(ks[0], (BATCH, SEQ, D), 1.0),
        'c': nrm(ks[1], (BATCH, D), 1.0),
        'w_ada': nrm(ks[2], (L, D, N_MOD * D), D ** -0.5),
        'b_ada': nrm(ks[3], (L, N_MOD * D), 0.02),
        'g_attn': 1.0 + nrm(ks[4], (L, D), 0.02),
        'w_in': nrm(ks[5], (L, D, N_IN), D ** -0.5),
        'b_fgate': 2.0 + nrm(ks[6], (L, N_HEADS_FOX), 0.5),
        'g_out_fox': 1.0 + nrm(ks[7], (L, D_FOX), 0.02),
        'g_out_sb': 1.0 + nrm(ks[8], (L, D_SB), 0.02),
        'w_out': nrm(ks[9], (L, D_MIX, D), D_MIX ** -0.5),
        'g_mlp': 1.0 + nrm(ks[10], (L, D), 0.02),
        'w_up': nrm(ks[11], (L, D, 2 * D_FF), D ** -0.5),
        'conv_w': nrm(ks[12], (L, CONV_WIDTH, 2 * D_FF), CONV_WIDTH ** -0.5),
        'conv_b': nrm(ks[13], (L, 2 * D_FF), 0.02),
        'w_down': nrm(ks[14], (L, D_FF, D), D_FF ** -0.5),
        'g_final': 1.0 + nrm(ks[15], (D,), 0.02),
    }


def _fwd_reference(x, c, w_ada, b_ada, g_attn, w_in, b_fgate, g_out_fox, g_out_sb, w_out,
              g_mlp, w_up, conv_w, conv_b, w_down, g_final):
    sizes = [D_FOX, D_FOX, D_FOX, D_SB, D_SB, D_SB]
    offsets = np.cumsum(sizes).tolist()
    for l in range(DEPTH):
        mod = jax.nn.silu(c) @ w_ada[l] + b_ada[l]
        shift_a, scale_a, gate_a, shift_m, scale_m, gate_m = [
            m[:, None, :] for m in jnp.split(mod, N_MOD, axis=-1)]

        h = rms_norm(x, g_attn[l]) * (1.0 + scale_a) + shift_a
        proj = h @ w_in[l]
        q_f, k_f, v_f, q_s, k_s, v_s, f_logit = jnp.split(proj, offsets, axis=-1)
        log_f = jax.nn.log_sigmoid((f_logit + b_fgate[l]).astype(jnp.float32))
        o_fox = forgetting_attention(split_heads(q_f, N_HEADS_FOX), split_heads(k_f, N_HEADS_FOX),
                                     split_heads(v_f, N_HEADS_FOX), log_f.transpose(0, 2, 1))
        o_sb = stick_breaking_attention(split_heads(q_s, N_HEADS_SB), split_heads(k_s, N_HEADS_SB),
                                        split_heads(v_s, N_HEADS_SB))
        mix = jnp.concatenate([head_rms_norm(o_fox, g_out_fox[l]),
                               head_rms_norm(o_sb, g_out_sb[l])], axis=-1)
        x = x + gate_a * (mix @ w_out[l])

        h = rms_norm(x, g_mlp[l]) * (1.0 + scale_m) + shift_m
        u = causal_depthwise_conv(h @ w_up[l], conv_w[l], conv_b[l])
        u_gate, u_val = jnp.split(u, 2, axis=-1)
        x = x + gate_m * ((jax.nn.silu(u_gate) * u_val) @ w_down[l])
    return rms_norm(x, g_final)


import jax as _jax
import jax.numpy as _jnp

TWIN_FORMAT = 'train_step'
FWD_PARAMS = ['x', 'c', 'w_ada', 'b_ada', 'g_attn', 'w_in', 'b_fgate', 'g_out_fox', 'g_out_sb', 'w_out', 'g_mlp', 'w_up', 'conv_w', 'conv_b', 'w_down', 'g_final']
TWIN_WEIGHTS = ['w_ada', 'b_ada', 'g_attn', 'w_in', 'b_fgate', 'g_out_fox', 'g_out_sb', 'w_out', 'g_mlp', 'w_up', 'conv_w', 'conv_b', 'w_down', 'g_final']
TWIN_DIFF_INPUT = 'x'
TWIN_INPUTS = ['x', 'c', 'w_ada', 'b_ada', 'g_attn', 'w_in', 'b_fgate', 'g_out_fox', 'g_out_sb', 'w_out', 'g_mlp', 'w_up', 'conv_w', 'conv_b', 'w_down', 'g_final', 'loss_target', 'm_w_ada', 'm_b_ada', 'm_g_attn', 'm_w_in', 'm_b_fgate', 'm_g_out_fox', 'm_g_out_sb', 'm_w_out', 'm_g_mlp', 'm_w_up', 'm_conv_w', 'm_conv_b', 'm_w_down', 'm_g_final', 'v_w_ada', 'v_b_ada', 'v_g_attn', 'v_w_in', 'v_b_fgate', 'v_g_out_fox', 'v_g_out_sb', 'v_w_out', 'v_g_mlp', 'v_w_up', 'v_conv_w', 'v_conv_b', 'v_w_down', 'v_g_final']
TWIN_OUTPUTS = ['loss', 'grad_x', 'grad_w_ada', 'grad_b_ada', 'grad_g_attn', 'grad_w_in', 'grad_b_fgate', 'grad_g_out_fox', 'grad_g_out_sb', 'grad_w_out', 'grad_g_mlp', 'grad_w_up', 'grad_conv_w', 'grad_conv_b', 'grad_w_down', 'grad_g_final', 'delta_w_ada', 'delta_b_ada', 'delta_g_attn', 'delta_w_in', 'delta_b_fgate', 'delta_g_out_fox', 'delta_g_out_sb', 'delta_w_out', 'delta_g_mlp', 'delta_w_up', 'delta_conv_w', 'delta_conv_b', 'delta_w_down', 'delta_g_final', 'new_m_w_ada', 'new_m_b_ada', 'new_m_g_attn', 'new_m_w_in', 'new_m_b_fgate', 'new_m_g_out_fox', 'new_m_g_out_sb', 'new_m_w_out', 'new_m_g_mlp', 'new_m_w_up', 'new_m_conv_w', 'new_m_conv_b', 'new_m_w_down', 'new_m_g_final', 'new_v_w_ada', 'new_v_b_ada', 'new_v_g_attn', 'new_v_w_in', 'new_v_b_fgate', 'new_v_g_out_fox', 'new_v_g_out_sb', 'new_v_w_out', 'new_v_g_mlp', 'new_v_w_up', 'new_v_conv_w', 'new_v_conv_b', 'new_v_w_down', 'new_v_g_final']
TWIN_LEAF_KINDS = {'loss': 'loss', 'grad_x': 'grad_x', 'grad_w_ada': 'grad_w', 'grad_b_ada': 'grad_w', 'grad_g_attn': 'grad_w', 'grad_w_in': 'grad_w', 'grad_b_fgate': 'grad_w', 'grad_g_out_fox': 'grad_w', 'grad_g_out_sb': 'grad_w', 'grad_w_out': 'grad_w', 'grad_g_mlp': 'grad_w', 'grad_w_up': 'grad_w', 'grad_conv_w': 'grad_w', 'grad_conv_b': 'grad_w', 'grad_w_down': 'grad_w', 'grad_g_final': 'grad_w', 'delta_w_ada': 'delta_w', 'delta_b_ada': 'delta_w', 'delta_g_attn': 'delta_w', 'delta_w_in': 'delta_w', 'delta_b_fgate': 'delta_w', 'delta_g_out_fox': 'delta_w', 'delta_g_out_sb': 'delta_w', 'delta_w_out': 'delta_w', 'delta_g_mlp': 'delta_w', 'delta_w_up': 'delta_w', 'delta_conv_w': 'delta_w', 'delta_conv_b': 'delta_w', 'delta_w_down': 'delta_w', 'delta_g_final': 'delta_w', 'new_m_w_ada': 'new_m', 'new_m_b_ada': 'new_m', 'new_m_g_attn': 'new_m', 'new_m_w_in': 'new_m', 'new_m_b_fgate': 'new_m', 'new_m_g_out_fox': 'new_m', 'new_m_g_out_sb': 'new_m', 'new_m_w_out': 'new_m', 'new_m_g_mlp': 'new_m', 'new_m_w_up': 'new_m', 'new_m_conv_w': 'new_m', 'new_m_conv_b': 'new_m', 'new_m_w_down': 'new_m', 'new_m_g_final': 'new_m', 'new_v_w_ada': 'new_v', 'new_v_b_ada': 'new_v', 'new_v_g_attn': 'new_v', 'new_v_w_in': 'new_v', 'new_v_b_fgate': 'new_v', 'new_v_g_out_fox': 'new_v', 'new_v_g_out_sb': 'new_v', 'new_v_w_out': 'new_v', 'new_v_g_mlp': 'new_v', 'new_v_w_up': 'new_v', 'new_v_conv_w': 'new_v', 'new_v_conv_b': 'new_v', 'new_v_w_down': 'new_v', 'new_v_g_final': 'new_v'}


def _forward(args):
    return _fwd_reference(*[args[k] for k in FWD_PARAMS])


def _output_shape():
    def fwd():
        inp = _fwd_setup_inputs(0)
        return _fwd_reference(*[inp[k] for k in FWD_PARAMS])
    out = _jax.eval_shape(fwd)
    return out.shape, out.dtype

N_MICROBATCH = 1
ADAM_LR = 0.001
ADAM_B1 = 0.9
ADAM_B2 = 0.999
ADAM_EPS = 1e-08
ADAM_WD = 0.01
ADAM_STEP = 10
PER_EXAMPLE_BATCH_AXIS = {'x': 0, 'c': 0, 'loss_target': 0}
SHARED_INPUTS = []
_WEIGHT_DTYPES = {'w_ada': _jnp.float32, 'b_ada': _jnp.float32, 'g_attn': _jnp.float32, 'w_in': _jnp.float32, 'b_fgate': _jnp.float32, 'g_out_fox': _jnp.float32, 'g_out_sb': _jnp.float32, 'w_out': _jnp.float32, 'g_mlp': _jnp.float32, 'w_up': _jnp.float32, 'conv_w': _jnp.float32, 'conv_b': _jnp.float32, 'w_down': _jnp.float32, 'g_final': _jnp.float32}
MOMENT_SCALE = {'w_ada': 1.446804e-01, 'b_ada': 2.778852e-01, 'g_attn': 1.615620e-01, 'w_in': 1.226975e-01, 'b_fgate': 1.049322e+00, 'g_out_fox': 1.713738e-01, 'g_out_sb': 1.550806e-01, 'w_out': 1.714170e-01, 'g_mlp': 1.636669e-01, 'w_up': 8.131410e-02, 'conv_w': 7.986877e-02, 'conv_b': 6.613278e-02, 'w_down': 1.370508e-01, 'g_final': 6.534837e+01}


def _to_microbatches(a, axis):
    t = _jnp.moveaxis(a, axis, 0)
    t = t.reshape((N_MICROBATCH, t.shape[0] // N_MICROBATCH) + t.shape[1:])
    return _jnp.moveaxis(t, 1, axis + 1)


def setup_inputs(seed: int = 0) -> dict:
    inp = _fwd_setup_inputs(seed)
    key = _jax.random.fold_in(_jax.random.key(seed), 7919)
    shape, _ = _output_shape()
    out = dict(inp)
    out["loss_target"] = _jax.random.normal(_jax.random.fold_in(key, 0), shape, _jnp.float32)
    for i, name in enumerate(TWIN_WEIGHTS):
        w = inp[name].astype(_jnp.float32)
        if MOMENT_SCALE is None:
            s = _jnp.sqrt(_jnp.mean(_jnp.square(w)) + 1e-30)
        else:
            s = MOMENT_SCALE[name]
        km, kv = _jax.random.split(_jax.random.fold_in(key, i + 1))
        out[name] = w
        out["m_" + name] = s * _jax.random.normal(km, w.shape, _jnp.float32)
        out["v_" + name] = (s * s) * _jax.random.uniform(kv, w.shape, _jnp.float32, 0.5, 1.5)
    if N_MICROBATCH > 1:
        for name, axis in PER_EXAMPLE_BATCH_AXIS.items():
            out[name] = _to_microbatches(out[name], axis)
    return {'x': out['x'], 'c': out['c'], 'w_ada': out['w_ada'], 'b_ada': out['b_ada'], 'g_attn': out['g_attn'], 'w_in': out['w_in'], 'b_fgate': out['b_fgate'], 'g_out_fox': out['g_out_fox'], 'g_out_sb': out['g_out_sb'], 'w_out': out['w_out'], 'g_mlp': out['g_mlp'], 'w_up': out['w_up'], 'conv_w': out['conv_w'], 'conv_b': out['conv_b'], 'w_down': out['w_down'], 'g_final': out['g_final'], 'loss_target': out['loss_target'], 'm_w_ada': out['m_w_ada'], 'm_b_ada': out['m_b_ada'], 'm_g_attn': out['m_g_attn'], 'm_w_in': out['m_w_in'], 'm_b_fgate': out['m_b_fgate'], 'm_g_out_fox': out['m_g_out_fox'], 'm_g_out_sb': out['m_g_out_sb'], 'm_w_out': out['m_w_out'], 'm_g_mlp': out['m_g_mlp'], 'm_w_up': out['m_w_up'], 'm_conv_w': out['m_conv_w'], 'm_conv_b': out['m_conv_b'], 'm_w_down': out['m_w_down'], 'm_g_final': out['m_g_final'], 'v_w_ada': out['v_w_ada'], 'v_b_ada': out['v_b_ada'], 'v_g_attn': out['v_g_attn'], 'v_w_in': out['v_w_in'], 'v_b_fgate': out['v_b_fgate'], 'v_g_out_fox': out['v_g_out_fox'], 'v_g_out_sb': out['v_g_out_sb'], 'v_w_out': out['v_w_out'], 'v_g_mlp': out['v_g_mlp'], 'v_w_up': out['v_w_up'], 'v_conv_w': out['v_conv_w'], 'v_conv_b': out['v_conv_b'], 'v_w_down': out['v_w_down'], 'v_g_final': out['v_g_final']}


def _loss(weights, diff, rest, loss_target):
    with _jax.named_scope("forward"):
        args = {**rest, TWIN_DIFF_INPUT: diff, **{k: w.astype(_WEIGHT_DTYPES[k]) for k, w in weights.items()}}
        y = _forward(args)
    with _jax.named_scope("loss_head"):
        err = _jnp.square(y.astype(_jnp.float32) - loss_target)
        return 0.5 * _jnp.sum(_jnp.mean(err, axis=-1)) if err.ndim else 0.5 * err


def _adamw(w, g, m, v):
    m = ADAM_B1 * m + (1.0 - ADAM_B1) * g
    v = ADAM_B2 * v + (1.0 - ADAM_B2) * _jnp.square(g)
    m_hat = m / (1.0 - ADAM_B1 ** ADAM_STEP)
    v_hat = v / (1.0 - ADAM_B2 ** ADAM_STEP)
    delta = -ADAM_LR * (m_hat / (_jnp.sqrt(v_hat) + ADAM_EPS) + ADAM_WD * w)
    return delta, m, v


def reference(x, c, w_ada, b_ada, g_attn, w_in, b_fgate, g_out_fox, g_out_sb, w_out, g_mlp, w_up, conv_w, conv_b, w_down, g_final, loss_target, m_w_ada, m_b_ada, m_g_attn, m_w_in, m_b_fgate, m_g_out_fox, m_g_out_sb, m_w_out, m_g_mlp, m_w_up, m_conv_w, m_conv_b, m_w_down, m_g_final, v_w_ada, v_b_ada, v_g_attn, v_w_in, v_b_fgate, v_g_out_fox, v_g_out_sb, v_w_out, v_g_mlp, v_w_up, v_conv_w, v_conv_b, v_w_down, v_g_final):
    given = dict(x=x, c=c, w_ada=w_ada, b_ada=b_ada, g_attn=g_attn, w_in=w_in, b_fgate=b_fgate, g_out_fox=g_out_fox, g_out_sb=g_out_sb, w_out=w_out, g_mlp=g_mlp, w_up=w_up, conv_w=conv_w, conv_b=conv_b, w_down=w_down, g_final=g_final, loss_target=loss_target, m_w_ada=m_w_ada, m_b_ada=m_b_ada, m_g_attn=m_g_attn, m_w_in=m_w_in, m_b_fgate=m_b_fgate, m_g_out_fox=m_g_out_fox, m_g_out_sb=m_g_out_sb, m_w_out=m_w_out, m_g_mlp=m_g_mlp, m_w_up=m_w_up, m_conv_w=m_conv_w, m_conv_b=m_conv_b, m_w_down=m_w_down, m_g_final=m_g_final, v_w_ada=v_w_ada, v_b_ada=v_b_ada, v_g_attn=v_g_attn, v_w_in=v_w_in, v_b_fgate=v_b_fgate, v_g_out_fox=v_g_out_fox, v_g_out_sb=v_g_out_sb, v_w_out=v_w_out, v_g_mlp=v_g_mlp, v_w_up=v_w_up, v_conv_w=v_conv_w, v_conv_b=v_conv_b, v_w_down=v_w_down, v_g_final=v_g_final)
    weights = {n: given[n] for n in TWIN_WEIGHTS}
    shared = {n: given[n] for n in SHARED_INPUTS}
    per_example = {n: given[n] for n in ['x', 'c']}
    grad_fn = _jax.value_and_grad(_loss, argnums=(0, 1))

    def one_microbatch(ex, loss_target):
        ex = dict(ex)
        diff = ex.pop(TWIN_DIFF_INPUT)
        return grad_fn(weights, diff, {**shared, **ex}, loss_target)

    if N_MICROBATCH == 1:
        loss, (grad_w, grad_x) = one_microbatch(per_example, given["loss_target"])
    else:
        def body(carry, xs):
            loss_sum, grad_sum = carry
            l_k, (gw_k, gx_k) = one_microbatch(xs[0], xs[1])
            with _jax.named_scope("update"):
                return (loss_sum + l_k, _jax.tree.map(_jnp.add, grad_sum, gw_k)), gx_k

        init = (_jnp.zeros((), _jnp.float32), _jax.tree.map(_jnp.zeros_like, weights))
        (loss, grad_w), grad_x = _jax.lax.scan(body, init, (per_example, given["loss_target"]))
    with _jax.named_scope("update"):
        delta_w, new_m, new_v = {}, {}, {}
        for n in TWIN_WEIGHTS:
            delta_w[n], new_m[n], new_v[n] = _adamw(weights[n], grad_w[n], given["m_" + n], given["v_" + n])
    return (loss, grad_x, *[grad_w[n] for n in TWIN_WEIGHTS], *[delta_w[n] for n in TWIN_WEIGHTS],
            *[new_m[n] for n in TWIN_WEIGHTS], *[new_v[n] for n in TWIN_WEIGHTS])
```

```python
import functools

import numpy as np
import jax
import jax.numpy as jnp
from jax import lax
from jax.experimental import pallas as pl
from jax.experimental.pallas import tpu as pltpu

F32 = jnp.float32
BF16 = jnp.bfloat16
MESH = pl.DeviceIdType.MESH

HEAD_DIM = 64
LANES = 128
EPS = 1e-6
NEG = -1e30
ADAM_LR, ADAM_B1, ADAM_B2, ADAM_EPS, ADAM_WD, ADAM_STEP = 0.001, 0.9, 0.999, 1e-08, 0.01, 10
V7X_VMEM_BYTES = 64 * 1024 * 1024
VMEM_LIMIT = V7X_VMEM_BYTES - 12 * 1024 * 1024
NT_DIMS = (((1,), (1,)), ((), ()))
TN_DIMS = (((0,), (0,)), ((), ()))


def _pcall(body, **kw):
    return pl.pallas_call(body, **kw)


def _params(sem=None, **kw):
    return pltpu.CompilerParams(dimension_semantics=sem, vmem_limit_bytes=VMEM_LIMIT, **kw)


def _split_dot(x, m, passes):
    acc = None
    for _ in range(passes):
        part = x.astype(BF16)
        d = jnp.dot(part, m, preferred_element_type=F32)
        acc = d if acc is None else acc + d
        x = x - part.astype(F32)
    return acc


def _tile(n, candidates):
    for t in candidates:
        if n % t == 0:
            return t
    return n


def _rows_tile(rows, row_bytes, budget=2 * 1024 * 1024):
    best = None
    for t in range(8, rows + 1, 8):
        if rows % t == 0 and t * row_bytes <= budget:
            best = t
    return best if best is not None else rows


def _all_gather8(v):
    m_per, n = v.shape

    def body(x_ref, out_ref, send_sems, recv_sems, local_sem):
        x, y, c = lax.axis_index("x"), lax.axis_index("y"), lax.axis_index("c")
        me, sibling = (x, y, c), (x, y, 1 - c)
        chips = [(1 - x, y), (x, 1 - y), (1 - x, 1 - y)]

        def rows(px, py, pc):
            return out_ref.at[pl.ds((4 * px + 2 * py + pc) * m_per, m_per), :]

        def copy(k, block, to, src=None):
            return pltpu.make_async_remote_copy(
                src_ref=rows(*block) if src is None else src, dst_ref=rows(*block),
                send_sem=send_sems.at[k], recv_sem=recv_sems.at[k], device_id=to, device_id_type=MESH)

        mine = pltpu.make_async_copy(x_ref, rows(*me), local_sem)
        mine.start()
        first = [copy(0, me, sibling, src=x_ref)]
        first += [copy(1 + j, me, (*chip, c), src=x_ref) for j, chip in enumerate(chips)]
        for cp in first:
            cp.start()
        passed = [copy(4 + j, (*chip, c), sibling) for j, chip in enumerate(chips)]
        for j, chip in enumerate(chips):
            copy(1 + j, (*chip, c), me).wait_recv()
            passed[j].start()
        copy(0, sibling, me).wait_recv()
        for j, chip in enumerate(chips):
            copy(4 + j, (*chip, 1 - c), me).wait_recv()
        for cp in first + passed:
            cp.wait_send()
        mine.wait()

    return _pcall(
        body, name="all_gather8",
        out_shape=jax.ShapeDtypeStruct((8 * m_per, n), v.dtype),
        in_specs=[pl.BlockSpec(memory_space=pltpu.VMEM)],
        out_specs=pl.BlockSpec(memory_space=pltpu.VMEM),
        scratch_shapes=[pltpu.SemaphoreType.DMA((7,)), pltpu.SemaphoreType.DMA((7,)), pltpu.SemaphoreType.DMA],
        compiler_params=pltpu.CompilerParams(vmem_limit_bytes=VMEM_LIMIT),
    )(v)


def _gather_xy(shards):
    n = len(shards)

    def body(*refs):
        ins, outs = refs[:n], refs[n:2 * n]
        send_sems, recv_sems, local_sems = refs[2 * n:]
        x, y, c = lax.axis_index("x"), lax.axis_index("y"), lax.axis_index("c")
        chips = [(1 - x, y), (x, 1 - y), (1 - x, 1 - y)]
        mine = 2 * x + y
        local, remote = [], []
        for w in range(n):
            cp = pltpu.make_async_copy(ins[w], outs[w].at[mine], local_sems.at[w])
            cp.start()
            local.append(cp)
            for k, (px, py) in enumerate(chips):
                cp = pltpu.make_async_remote_copy(
                    src_ref=ins[w], dst_ref=outs[w].at[mine], send_sem=send_sems.at[3 * w + k],
                    recv_sem=recv_sems.at[3 * w + k], device_id=(px, py, c), device_id_type=MESH)
                cp.start()
                remote.append(cp)
        for cp in remote:
            cp.wait_recv()
        for cp in remote:
            cp.wait_send()
        for cp in local:
            cp.wait()

    hbm = pl.BlockSpec(memory_space=pltpu.HBM)
    return _pcall(
        body, name="gather_xy",
        out_shape=[jax.ShapeDtypeStruct((4,) + s.shape, s.dtype) for s in shards],
        in_specs=[hbm] * n, out_specs=[hbm] * n,
        scratch_shapes=[pltpu.SemaphoreType.DMA((3 * n,)), pltpu.SemaphoreType.DMA((3 * n,)),
                        pltpu.SemaphoreType.DMA((n,))],
        compiler_params=pltpu.CompilerParams(vmem_limit_bytes=VMEM_LIMIT),
    )(*shards)


def _scatter8(pieces):
    n = len(pieces)

    def body(*refs):
        ins, outs = refs[:n], refs[n:2 * n]
        send_sems, recv_sems, local_sems = refs[2 * n:]
        x, y, c = lax.axis_index("x"), lax.axis_index("y"), lax.axis_index("c")
        me = 4 * x + 2 * y + c
        local, remote = [], []
        for w in range(n):
            cp = pltpu.make_async_copy(ins[w].at[me], outs[w].at[me], local_sems.at[w])
            cp.start()
            local.append(cp)
            for f in range(1, 8):
                px = 1 - x if f & 4 else x
                py = 1 - y if f & 2 else y
                pc = 1 - c if f & 1 else c
                cp = pltpu.make_async_remote_copy(
                    src_ref=ins[w].at[4 * px + 2 * py + pc], dst_ref=outs[w].at[me],
                    send_sem=send_sems.at[7 * w + f - 1], recv_sem=recv_sems.at[7 * w + f - 1],
                    device_id=(px, py, pc), device_id_type=MESH)
                cp.start()
                remote.append(cp)
        for cp in remote:
            cp.wait_recv()
        for cp in remote:
            cp.wait_send()
        for cp in local:
            cp.wait()

    hbm = pl.BlockSpec(memory_space=pltpu.HBM)
    return _pcall(
        body, name="scatter8",
        out_shape=[jax.ShapeDtypeStruct(p.shape, p.dtype) for p in pieces],
        in_specs=[hbm] * n, out_specs=[hbm] * n,
        scratch_shapes=[pltpu.SemaphoreType.DMA((7 * n,)), pltpu.SemaphoreType.DMA((7 * n,)),
                        pltpu.SemaphoreType.DMA((n,))],
        compiler_params=pltpu.CompilerParams(vmem_limit_bytes=VMEM_LIMIT),
    )(*pieces)


def _swap_halves(halves):
    n = len(halves)

    def body(*refs):
        ins, outs = refs[:n], refs[n:2 * n]
        send_sems, recv_sems, local_sems = refs[2 * n:]
        x, y, c = lax.axis_index("x"), lax.axis_index("y"), lax.axis_index("c")
        local, remote = [], []
        for w in range(n):
            cp = pltpu.make_async_copy(ins[w], outs[w].at[c], local_sems.at[w])
            cp.start()
            local.append(cp)
            cp = pltpu.make_async_remote_copy(
                src_ref=ins[w], dst_ref=outs[w].at[c], send_sem=send_sems.at[w], recv_sem=recv_sems.at[w],
                device_id=(x, y, 1 - c), device_id_type=MESH)
            cp.start()
            remote.append(cp)
        for cp in remote:
            cp.wait_recv()
        for cp in remote:
            cp.wait_send()
        for cp in local:
            cp.wait()

    hbm = pl.BlockSpec(memory_space=pltpu.HBM)
    return _pcall(
        body, name="swap_halves",
        out_shape=[jax.ShapeDtypeStruct((2,) + h.shape, h.dtype) for h in halves],
        in_specs=[hbm] * n, out_specs=[hbm] * n,
        scratch_shapes=[pltpu.SemaphoreType.DMA((n,)), pltpu.SemaphoreType.DMA((n,)), pltpu.SemaphoreType.DMA((n,))],
        compiler_params=pltpu.CompilerParams(vmem_limit_bytes=VMEM_LIMIT),
    )(*halves)


def _sum_leading(a, name):
    n, r, c = a.shape
    tr = _rows_tile(r, n * c * 4, budget=6 * 1024 * 1024)

    def body(a_ref, o_ref):
        acc = a_ref[0]
        for k in range(1, n):
            acc = acc + a_ref[k]
        o_ref[...] = acc

    return _pcall(
        body, name=name, grid=(r // tr,),
        out_shape=jax.ShapeDtypeStruct((r, c), a.dtype),
        in_specs=[pl.BlockSpec((n, tr, c), lambda i: (0, i, 0))],
        out_specs=pl.BlockSpec((tr, c), lambda i: (i, 0)),
        compiler_params=_params(("arbitrary",)),
    )(a)


def _adamw(w, g, m, v, name):
    r, c = w.shape
    tr = _rows_tile(r, c * 4, budget=1024 * 1024)
    c1 = 1.0 - ADAM_B1 ** ADAM_STEP
    c2 = 1.0 - ADAM_B2 ** ADAM_STEP

    def body(w_ref, g_ref, m_ref, v_ref, d_ref, nm_ref, nv_ref):
        gg = g_ref[...]
        nm = ADAM_B1 * m_ref[...] + (1.0 - ADAM_B1) * gg
        nv = ADAM_B2 * v_ref[...] + (1.0 - ADAM_B2) * (gg * gg)
        m_hat = nm / c1
        v_hat = nv / c2
        d_ref[...] = -ADAM_LR * (m_hat / (jnp.sqrt(v_hat) + ADAM_EPS) + ADAM_WD * w_ref[...])
        nm_ref[...] = nm
        nv_ref[...] = nv

    spec = pl.BlockSpec((tr, c), lambda i: (i, 0))
    return _pcall(
        body, name=name, grid=(r // tr,),
        out_shape=[jax.ShapeDtypeStruct((r, c), F32)] * 3,
        in_specs=[spec] * 4, out_specs=[spec] * 3,
        compiler_params=_params(("arbitrary",)),
    )(w, g, m, v)


def _ada_fwd(c_all, w_shard, b_shard):
    nb, d = c_all.shape
    cols = w_shard.shape[1]

    def body(c_ref, w_ref, b_ref, sc_ref, mod_ref):
        cv = c_ref[...]
        sc = cv * jax.nn.sigmoid(cv)
        sc_ref[...] = sc
        mod_ref[...] = jnp.dot(sc.astype(BF16), w_ref[...].astype(BF16), preferred_element_type=F32) + b_ref[...]

    return _pcall(
        body, name="ada_fwd",
        out_shape=[jax.ShapeDtypeStruct((nb, d), F32), jax.ShapeDtypeStruct((nb, cols), F32)],
        compiler_params=pltpu.CompilerParams(vmem_limit_bytes=VMEM_LIMIT),
    )(c_all, w_shard, b_shard)


def _ada_bwd(sc_t, dmod_cols):
    d, nb = sc_t.shape
    cols = dmod_cols.shape[1]
    tr = _rows_tile(d, cols * 4, budget=1024 * 1024)

    def body(s_ref, m_ref, o_ref):
        s = s_ref[...]
        m = m_ref[...]
        acc = s[:, 0:1] * m[0:1, :]
        for b in range(1, nb):
            acc = acc + s[:, b:b + 1] * m[b:b + 1, :]
        o_ref[...] = acc

    return _pcall(
        body, name="ada_bwd", grid=(d // tr,),
        out_shape=jax.ShapeDtypeStruct((d, cols), F32),
        in_specs=[pl.BlockSpec((tr, nb), lambda i: (i, 0)), pl.BlockSpec((nb, cols), lambda i: (0, 0))],
        out_specs=pl.BlockSpec((tr, cols), lambda i: (i, 0)),
        compiler_params=_params(("arbitrary",)),
    )(sc_t, dmod_cols)


def _log_sigmoid(x):
    return jnp.minimum(x, 0.0) - jnp.log1p(jnp.exp(-jnp.abs(x)))


def _fgate_fwd(fl2d, b_rows, tri_in, tri_blk):
    r = fl2d.shape[0]

    def body(x_ref, b_ref, u_ref, l_ref, f_ref):
        lf = _log_sigmoid(x_ref[...] + b_ref[...])
        c1 = _split_dot(lf, u_ref[...], 3)
        tot = jnp.broadcast_to(c1[:, LANES - 1:LANES], (r, LANES))
        acc = None
        for _ in range(3):
            part = tot.astype(BF16)
            dd = jnp.dot(l_ref[...], part, preferred_element_type=F32)
            acc = dd if acc is None else acc + dd
            tot = tot - part.astype(F32)
        f_ref[...] = c1 + acc

    return _pcall(
        body, name="fgate_fwd", out_shape=jax.ShapeDtypeStruct((r, LANES), F32),
        compiler_params=pltpu.CompilerParams(vmem_limit_bytes=VMEM_LIMIT),
    )(fl2d, b_rows, tri_in, tri_blk)


def _fgate_bwd(fl2d, b_rows, df_query, df_key, tri_in_rev, tri_blk_rev, head_rows):
    r = fl2d.shape[0]
    nhp = head_rows.shape[0]

    def body(x_ref, b_ref, dq_ref, dk_ref, u_ref, l_ref, hr_ref, o_ref, gb_ref):
        c1 = _split_dot(dq_ref[...] + dk_ref[...], u_ref[...], 3)
        tot = jnp.broadcast_to(c1[:, 0:1], (r, LANES))
        acc = None
        for _ in range(3):
            part = tot.astype(BF16)
            dd = jnp.dot(l_ref[...], part, preferred_element_type=F32)
            acc = dd if acc is None else acc + dd
            tot = tot - part.astype(F32)
        x = x_ref[...] + b_ref[...]
        e = jnp.exp(-jnp.abs(x))
        dfl = (c1 + acc) * (jnp.where(x >= 0, e, 1.0) / (1.0 + e))
        o_ref[...] = dfl
        rs = jnp.broadcast_to(jnp.sum(dfl, axis=1, keepdims=True), (r, LANES))
        gb = None
        for _ in range(3):
            part = rs.astype(BF16)
            dd = jnp.dot(hr_ref[...], part, preferred_element_type=F32)
            gb = dd if gb is None else gb + dd
            rs = rs - part.astype(F32)
        gb_ref[...] = gb

    return _pcall(
        body, name="fgate_bwd",
        out_shape=[jax.ShapeDtypeStruct((r, LANES), F32), jax.ShapeDtypeStruct((nhp, LANES), F32)],
        compiler_params=pltpu.CompilerParams(vmem_limit_bytes=VMEM_LIMIT),
    )(fl2d, b_rows, df_query, df_key, tri_in_rev, tri_blk_rev, head_rows)


def _norm_mod(x, g, scale, shift):
    r = lax.rsqrt(jnp.mean(x * x, axis=-1, keepdims=True) + EPS)
    return (x * r * g) * (1.0 + scale) + shift


def _norm_mod_bwd(x, dh, g, scale):
    r = lax.rsqrt(jnp.mean(x * x, axis=-1, keepdims=True) + EPS)
    xn = x * r
    dshift = jnp.sum(dh, axis=0, keepdims=True)
    dscale = jnp.sum(dh * (xn * g), axis=0, keepdims=True)
    dxn_g = dh * (1.0 + scale)
    dg = jnp.sum(dxn_g * xn, axis=0, keepdims=True)
    dxn = dxn_g * g
    dx = r * (dxn - xn * jnp.mean(dxn * xn, axis=-1, keepdims=True))
    return dx, dshift, dscale, dg


def _in_proj_fwd(x, mod8, g_attn, w_qkv, w_f, tm):
    t, d = x.shape
    dg = w_qkv.shape[1] // 6

    def body(x_ref, mod_ref, g_ref, w_ref, wf_ref, qkv_ref, fl_ref, h1_ref, h_sc):
        j = pl.program_id(1)

        @pl.when(j == 0)
        def _():
            h = _norm_mod(x_ref[...], g_ref[...], mod_ref[1:2, :], mod_ref[0:1, :]).astype(BF16)
            h_sc[...] = h
            h1_ref[...] = h
            fl_ref[...] = jnp.dot(h, wf_ref[...], preferred_element_type=F32)

        s = jnp.where((j == 0) | (j == 3), HEAD_DIM ** -0.5, 1.0)
        qkv_ref[...] = (jnp.dot(h_sc[...], w_ref[...], preferred_element_type=F32) * s).astype(BF16)

    return _pcall(
        body, name="in_proj_fwd", grid=(t // tm, 6),
        out_shape=[jax.ShapeDtypeStruct((t, 6 * dg), BF16), jax.ShapeDtypeStruct((t, LANES), F32),
                   jax.ShapeDtypeStruct((t, d), BF16)],
        in_specs=[pl.BlockSpec((tm, d), lambda i, j: (i, 0)), pl.BlockSpec((8, d), lambda i, j: (0, 0)),
                  pl.BlockSpec((1, d), lambda i, j: (0, 0)), pl.BlockSpec((d, dg), lambda i, j: (0, j)),
                  pl.BlockSpec((d, LANES), lambda i, j: (0, 0))],
        out_specs=[pl.BlockSpec((tm, dg), lambda i, j: (i, j)), pl.BlockSpec((tm, LANES), lambda i, j: (i, 0)),
                   pl.BlockSpec((tm, d), lambda i, j: (i, 0))],
        scratch_shapes=[pltpu.VMEM((tm, d), BF16)],
        compiler_params=_params(("arbitrary", "arbitrary")),
    )(x, mod8, g_attn, w_qkv, w_f)


def _head_rstd(o, bd):
    return lax.rsqrt(_split_dot(o * o, bd, 3) * (1.0 / HEAD_DIM) + EPS)


def _attn_out_fwd(x, o_fox, o_sb, g_fox, g_sb, w_out, mod8, bd, tm):
    t, d = x.shape
    dg = o_fox.shape[1]

    def body(x_ref, of_ref, os_ref, gf_ref, gs_ref, w_ref, mod_ref, bd_ref, x2_ref, mix_ref):
        of, osb = of_ref[...], os_ref[...]
        mf = (of * _head_rstd(of, bd_ref[...]) * gf_ref[...]).astype(BF16)
        ms = (osb * _head_rstd(osb, bd_ref[...]) * gs_ref[...]).astype(BF16)
        mix_ref[:, :dg] = mf
        mix_ref[:, dg:] = ms
        y = jnp.dot(mf, w_ref[:dg, :], preferred_element_type=F32) + jnp.dot(ms, w_ref[dg:, :], preferred_element_type=F32)
        x2_ref[...] = x_ref[...] + mod_ref[2:3, :] * y

    row = lambda w: pl.BlockSpec((tm, w), lambda i: (i, 0))
    full = lambda a: pl.BlockSpec(a.shape, lambda i: (0,) * a.ndim)
    return _pcall(
        body, name="attn_out_fwd", grid=(t // tm,),
        out_shape=[jax.ShapeDtypeStruct((t, d), F32), jax.ShapeDtypeStruct((t, 2 * dg), BF16)],
        in_specs=[row(d), row(dg), row(dg), full(g_fox), full(g_sb), full(w_out), full(mod8), full(bd)],
        out_specs=[row(d), row(2 * dg)],
        compiler_params=_params(("arbitrary",)),
    )(x, o_fox, o_sb, g_fox, g_sb, w_out, mod8, bd)


def _attn_out_bwd(dx2, mix, o_fox, o_sb, g_fox, g_sb, w_out, mod8, bd, hsel, tm):
    t, d = dx2.shape
    dg = o_fox.shape[1]

    def body(dx_ref, mix_ref, of_ref, os_ref, gf_ref, gs_ref, w_ref, mod_ref, bd_ref, hs_ref,
             dof_ref, dos_ref, dlt_ref, dxg_ref, part_ref):
        dx = dx_ref[...]
        gate = mod_ref[2:3, :]
        dxg = (dx * gate).astype(BF16)
        dxg_ref[...] = dxg
        mixv = mix_ref[...]
        y = jnp.dot(mixv[:, :dg], w_ref[:dg, :], preferred_element_type=F32)
        y = y + jnp.dot(mixv[:, dg:], w_ref[dg:, :], preferred_element_type=F32)
        part_ref[0] = jnp.zeros((8, d), F32)
        part_ref[0, 0:1, :] = jnp.sum(dx * y, axis=0, keepdims=True)
        for grp, (o_ref, g_ref, do_ref) in enumerate(((of_ref, gf_ref, dof_ref), (os_ref, gs_ref, dos_ref))):
            dmix = lax.dot_general(dxg, w_ref[grp * dg:(grp + 1) * dg, :], NT_DIMS, preferred_element_type=F32)
            o = o_ref[...]
            r = _head_rstd(o, bd_ref[...])
            n = o * r
            part_ref[0, 1:2, grp * dg:(grp + 1) * dg] = jnp.sum(dmix * n, axis=0, keepdims=True)
            dn = dmix * g_ref[...]
            mh = _split_dot(dn * n, bd_ref[...], 3) * (1.0 / HEAD_DIM)
            do = r * (dn - n * mh)
            do_ref[...] = do.astype(BF16)
            if grp == 0:
                dlt_ref[...] = _split_dot(do * o, hs_ref[...], 3)

    row = lambda w: pl.BlockSpec((tm, w), lambda i: (i, 0))
    full = lambda a: pl.BlockSpec(a.shape, lambda i: (0,) * a.ndim)
    nt = t // tm
    return _pcall(
        body, name="attn_out_bwd", grid=(nt,),
        out_shape=[jax.ShapeDtypeStruct((t, dg), BF16), jax.ShapeDtypeStruct((t, dg), BF16),
                   jax.ShapeDtypeStruct((t, LANES), F32), jax.ShapeDtypeStruct((t, d), BF16),
                   jax.ShapeDtypeStruct((nt, 8, d), F32)],
        in_specs=[row(d), row(2 * dg), row(dg), row(dg), full(g_fox), full(g_sb), full(w_out), full(mod8),
                  full(bd), full(hsel)],
        out_specs=[row(dg), row(dg), row(LANES), row(d), pl.BlockSpec((1, 8, d), lambda i: (i, 0, 0))],
        compiler_params=_params(("arbitrary",)),
    )(dx2, mix, o_fox, o_sb, g_fox, g_sb, w_out, mod8, bd, hsel)


def _in_proj_bwd(dparts, dfl, w_qkv, w_f, x, dx2, mod8, g_attn, tm):
    t, d = x.shape
    dg = dparts[0].shape[1]

    def body(*refs):
        d_refs = refs[:6]
        dfl_ref, w_ref, wf_ref, x_ref, dx2_ref, mod_ref, g_ref, gx_ref, dp_ref, dflb_ref, part_ref = refs[6:]
        dh = None
        for k in range(6):
            dk = d_refs[k][...]
            if k in (0, 3):
                dk = dk * HEAD_DIM ** -0.5
            db = dk.astype(BF16)
            dp_ref[:, k * dg:(k + 1) * dg] = db
            term = lax.dot_general(db, w_ref[:, k * dg:(k + 1) * dg], NT_DIMS, preferred_element_type=F32)
            dh = term if dh is None else dh + term
        dfb = dfl_ref[...].astype(BF16)
        dflb_ref[...] = dfb
        dh = dh + lax.dot_general(dfb, wf_ref[...], NT_DIMS, preferred_element_type=F32)
        dx, dshift, dscale, dgn = _norm_mod_bwd(x_ref[...], dh, g_ref[...], mod_ref[1:2, :])
        gx_ref[...] = dx2_ref[...] + dx
        part_ref[0] = jnp.zeros((8, d), F32)
        part_ref[0, 0:1, :] = dshift
        part_ref[0, 1:2, :] = dscale
        part_ref[0, 2:3, :] = dgn

    row = lambda w: pl.BlockSpec((tm, w), lambda i: (i, 0))
    full = lambda a: pl.BlockSpec(a.shape, lambda i: (0,) * a.ndim)
    nt = t // tm
    return _pcall(
        body, name="in_proj_bwd", grid=(nt,),
        out_shape=[jax.ShapeDtypeStruct((t, d), F32), jax.ShapeDtypeStruct((t, 6 * dg), BF16),
                   jax.ShapeDtypeStruct((t, LANES), BF16), jax.ShapeDtypeStruct((nt, 8, d), F32)],
        in_specs=[row(dg)] * 6 + [row(LANES), full(w_qkv), full(w_f), row(d), row(d), full(mod8), full(g_attn)],
        out_specs=[row(d), row(6 * dg), row(LANES), pl.BlockSpec((1, 8, d), lambda i: (i, 0, 0))],
        compiler_params=_params(("arbitrary",)),
    )(*dparts, dfl, w_qkv, w_f, x, dx2, mod8, g_attn)


def _matmul_tn(a, b, name):
    t, m = a.shape
    n = b.shape[1]
    tm_ = _tile(m, (512, 256, 128))
    tn_ = _tile(n, (1024, 512, 256, 128))
    tk = _tile(t, (512, 256, 128))
    nk = t // tk

    def body(a_ref, b_ref, o_ref):
        k = pl.program_id(2)

        @pl.when(k == 0)
        def _():
            o_ref[...] = jnp.zeros_like(o_ref)

        o_ref[...] += lax.dot_general(a_ref[...], b_ref[...], TN_DIMS, preferred_element_type=F32)

    return _pcall(
        body, name=name, grid=(m // tm_, n // tn_, nk),
        out_shape=jax.ShapeDtypeStruct((m, n), F32),
        in_specs=[pl.BlockSpec((tk, tm_), lambda i, j, k: (k, i)), pl.BlockSpec((tk, tn_), lambda i, j, k: (k, j))],
        out_specs=pl.BlockSpec((tm_, tn_), lambda i, j, k: (i, j)),
        compiler_params=_params(("arbitrary", "arbitrary", "arbitrary")),
    )(a, b)


HALO = 16


def _conv_taps(up_ext, cw, lo, rows):
    s1 = pltpu.roll(up_ext, 1, 0)
    s2 = pltpu.roll(up_ext, 2, 0)
    u = cw[2:3, :] * up_ext[lo:lo + rows] + cw[1:2, :] * s1[lo:lo + rows] + cw[0:1, :] * s2[lo:lo + rows] + cw[3:4, :]
    return u, s1, s2


def _ffn_fwd(x2, target, mod8, g_mlp, g_final, wg, wv, cwg, cwv, wd, tm, cf):
    t, d = x2.shape
    dfp = wg.shape[1]
    nt, nc = t // tm, dfp // cf
    hb = tm // HALO

    def body(x_ref, xp_ref, tg_ref, mod_ref, g_ref, gf_ref, wg_ref, wv_ref, cg_ref, cv_ref, wd_ref,
             dx3_ref, h2_ref, part_ref, h_sc, acc_sc):
        i, j = pl.program_id(0), pl.program_id(1)

        @pl.when(j == 0)
        def _():
            xe = jnp.concatenate([xp_ref[...], x_ref[...]], axis=0)
            h = _norm_mod(xe, g_ref[...], mod_ref[4:5, :], mod_ref[3:4, :]).astype(BF16)
            h_sc[...] = h
            h2_ref[...] = h[HALO:]
            acc_sc[...] = jnp.zeros_like(acc_sc)

        rowi = lax.broadcasted_iota(jnp.int32, (tm + HALO, 1), 0)
        keep = (rowi >= HALO) | (i > 0)
        hv = h_sc[...]
        upg = jnp.where(keep, jnp.dot(hv, wg_ref[...], preferred_element_type=F32), 0.0)
        upv = jnp.where(keep, jnp.dot(hv, wv_ref[...], preferred_element_type=F32), 0.0)
        ug, _, _ = _conv_taps(upg, cg_ref[...], HALO, tm)
        uv, _, _ = _conv_taps(upv, cv_ref[...], HALO, tm)
        act = (ug * jax.nn.sigmoid(ug) * uv).astype(BF16)
        acc_sc[...] += jnp.dot(act, wd_ref[...], preferred_element_type=F32)

        @pl.when(j == nc - 1)
        def _():
            y_ffn = acc_sc[...]
            x3 = x_ref[...] + mod_ref[5:6, :] * y_ffn
            r3 = lax.rsqrt(jnp.mean(x3 * x3, axis=-1, keepdims=True) + EPS)
            xn = x3 * r3
            gf = gf_ref[...]
            diff = xn * gf - tg_ref[...]
            dy = diff * (1.0 / d)
            dxn = dy * gf
            dx3 = r3 * (dxn - xn * jnp.mean(dxn * xn, axis=-1, keepdims=True))
            dx3_ref[...] = dx3
            part_ref[0] = jnp.zeros((8, d), F32)
            part_ref[0, 0:1, :] = jnp.sum(dy * xn, axis=0, keepdims=True)
            part_ref[0, 1:2, :] = jnp.sum(dx3 * y_ffn, axis=0, keepdims=True)
            part_ref[0, 2:3, :] = jnp.sum(diff * diff, axis=0, keepdims=True) * (0.5 / d)

    row = lambda w: pl.BlockSpec((tm, w), lambda i, j: (i, 0))
    full = lambda a: pl.BlockSpec(a.shape, lambda i, j: (0,) * a.ndim)
    return _pcall(
        body, name="ffn_fwd", grid=(nt, nc),
        out_shape=[jax.ShapeDtypeStruct((t, d), F32), jax.ShapeDtypeStruct((t, d), BF16),
                   jax.ShapeDtypeStruct((nt, 8, d), F32)],
        in_specs=[row(d), pl.BlockSpec((HALO, d), lambda i, j: (jnp.maximum(i * hb - 1, 0), 0)), row(d),
                  full(mod8), full(g_mlp), full(g_final),
                  pl.BlockSpec((d, cf), lambda i, j: (0, j)), pl.BlockSpec((d, cf), lambda i, j: (0, j)),
                  pl.BlockSpec((8, cf), lambda i, j: (0, j)), pl.BlockSpec((8, cf), lambda i, j: (0, j)),
                  pl.BlockSpec((cf, d), lambda i, j: (j, 0))],
        out_specs=[row(d), row(d), pl.BlockSpec((1, 8, d), lambda i, j: (i, 0, 0))],
        scratch_shapes=[pltpu.VMEM((tm + HALO, d), BF16), pltpu.VMEM((tm, d), F32)],
        compiler_params=_params(("arbitrary", "arbitrary")),
    )(x2, x2, target, mod8, g_mlp, g_final, wg, wv, cwg, cwv, wd)


def _ffn_bwd(x2, dx3, mod8, g_mlp, wg, wv, cwg, cwv, wd, tm, cf):
    t, d = x2.shape
    dfp = wg.shape[1]
    nt, nc = t // tm, dfp // cf
    hb = tm // HALO
    nhb = t // HALO
    ext = tm + 2 * HALO

    def body(x_ref, xp_ref, xn_ref, dx_ref, dxn_ref, mod_ref, g_ref, wg_ref, wv_ref, cg_ref, cv_ref, wd_ref,
             dx2_ref, dug_ref, duv_ref, act_ref, dxg_ref, part_ref, pcg_ref, pcv_ref, h_sc, dg_sc, dh_sc):
        i, j = pl.program_id(0), pl.program_id(1)

        @pl.when(j == 0)
        def _():
            xe = jnp.concatenate([xp_ref[...], x_ref[...], xn_ref[...]], axis=0)
            h_sc[...] = _norm_mod(xe, g_ref[...], mod_ref[4:5, :], mod_ref[3:4, :]).astype(BF16)
            de = (jnp.concatenate([dx_ref[...], dxn_ref[...]], axis=0) * mod_ref[5:6, :]).astype(BF16)
            dg_sc[...] = de
            dxg_ref[...] = de[:tm]
            dh_sc[...] = jnp.zeros_like(dh_sc)

        rowe = lax.broadcasted_iota(jnp.int32, (ext, 1), 0)
        keep_up = (rowe >= HALO) | (i > 0)
        rowu = lax.broadcasted_iota(jnp.int32, (tm + HALO, 1), 0)
        keep_du = (rowu < tm) | (i < nt - 1)
        hv = h_sc[...]
        upg = jnp.where(keep_up, jnp.dot(hv, wg_ref[...], preferred_element_type=F32), 0.0)
        upv = jnp.where(keep_up, jnp.dot(hv, wv_ref[...], preferred_element_type=F32), 0.0)
        cg, cv = cg_ref[...], cv_ref[...]
        ug, g1, g2 = _conv_taps(upg, cg, HALO, tm + HALO)
        uv, v1, v2 = _conv_taps(upv, cv, HALO, tm + HALO)
        dact = lax.dot_general(dg_sc[...], wd_ref[...], NT_DIMS, preferred_element_type=F32)
        sg = jax.nn.sigmoid(ug)
        sil = ug * sg
        act_ref[...] = (sil * uv)[:tm].astype(BF16)
        duv = jnp.where(keep_du, dact * sil, 0.0)
        dug = jnp.where(keep_du, dact * uv * (sg * (1.0 + ug * (1.0 - sg))), 0.0)

        def back(du, cw, up, s1, s2, pc_ref):
            n = tm + HALO
            dup = (cw[2:3, :] * du + cw[1:2, :] * pltpu.roll(du, n - 1, 0) + cw[0:1, :] * pltpu.roll(du, n - 2, 0))[:tm]
            dut = du[:tm]
            pc_ref[0] = jnp.zeros((8, cf), F32)
            pc_ref[0, 0:1, :] = jnp.sum(dut * s2[HALO:HALO + tm], axis=0, keepdims=True)
            pc_ref[0, 1:2, :] = jnp.sum(dut * s1[HALO:HALO + tm], axis=0, keepdims=True)
            pc_ref[0, 2:3, :] = jnp.sum(dut * up[HALO:HALO + tm], axis=0, keepdims=True)
            pc_ref[0, 3:4, :] = jnp.sum(dut, axis=0, keepdims=True)
            return dup.astype(BF16)

        dupg = back(dug, cg, upg, g1, g2, pcg_ref)
        dupv = back(duv, cv, upv, v1, v2, pcv_ref)
        dug_ref[...] = dupg
        duv_ref[...] = dupv
        dh_sc[...] += (lax.dot_general(dupg, wg_ref[...], NT_DIMS, preferred_element_type=F32)
                       + lax.dot_general(dupv, wv_ref[...], NT_DIMS, preferred_element_type=F32))

        @pl.when(j == nc - 1)
        def _():
            dx, dshift, dscale, dgn = _norm_mod_bwd(x_ref[...], dh_sc[...], g_ref[...], mod_ref[4:5, :])
            dx2_ref[...] = dx_ref[...] + dx
            part_ref[0] = jnp.zeros((8, d), F32)
            part_ref[0, 0:1, :] = dshift
            part_ref[0, 1:2, :] = dscale
            part_ref[0, 2:3, :] = dgn

    row = lambda w: pl.BlockSpec((tm, w), lambda i, j: (i, 0))
    prev = pl.BlockSpec((HALO, d), lambda i, j: (jnp.maximum(i * hb - 1, 0), 0))
    nxt = pl.BlockSpec((HALO, d), lambda i, j: (jnp.minimum((i + 1) * hb, nhb - 1), 0))
    full = lambda a: pl.BlockSpec(a.shape, lambda i, j: (0,) * a.ndim)
    chunk = pl.BlockSpec((tm, cf), lambda i, j: (i, j))
    pchunk = pl.BlockSpec((1, 8, cf), lambda i, j: (i, 0, j))
    return _pcall(
        body, name="ffn_bwd", grid=(nt, nc),
        out_shape=[jax.ShapeDtypeStruct((t, d), F32), jax.ShapeDtypeStruct((t, dfp), BF16),
                   jax.ShapeDtypeStruct((t, dfp), BF16), jax.ShapeDtypeStruct((t, dfp), BF16),
                   jax.ShapeDtypeStruct((t, d), BF16), jax.ShapeDtypeStruct((nt, 8, d), F32),
                   jax.ShapeDtypeStruct((nt, 8, dfp), F32), jax.ShapeDtypeStruct((nt, 8, dfp), F32)],
        in_specs=[row(d), prev, nxt, row(d), nxt, full(mod8), full(g_mlp),
                  pl.BlockSpec((d, cf), lambda i, j: (0, j)), pl.BlockSpec((d, cf), lambda i, j: (0, j)),
                  pl.BlockSpec((8, cf), lambda i, j: (0, j)), pl.BlockSpec((8, cf), lambda i, j: (0, j)),
                  pl.BlockSpec((cf, d), lambda i, j: (j, 0))],
        out_specs=[row(d), chunk, chunk, chunk, row(d), pl.BlockSpec((1, 8, d), lambda i, j: (i, 0, 0)), pchunk, pchunk],
        scratch_shapes=[pltpu.VMEM((ext, d), BF16), pltpu.VMEM((tm + HALO, d), BF16), pltpu.VMEM((tm, d), F32)],
        compiler_params=_params(("arbitrary", "arbitrary")),
    )(x2, x2, x2, dx3, dx3, mod8, g_mlp, wg, wv, cwg, cwv, wd)


def _head_masks():
    lane = lax.broadcasted_iota(jnp.int32, (1, LANES), 1)
    in_a = lane < HEAD_DIM
    return in_a, jnp.logical_not(in_a)


def _attn_specs(t, dg, bq, base):
    p = dg // LANES
    q = pl.BlockSpec((bq, LANES), lambda h, i: (i, base * p + h))
    k = pl.BlockSpec((t, LANES), lambda h, i: (0, (base + 1) * p + h))
    v = pl.BlockSpec((t, LANES), lambda h, i: (0, (base + 2) * p + h))
    return q, k, v


def _fox_fwd(qkv, fcol, frow, dg, bq):
    t = qkv.shape[0]
    p, nq, bk = dg // LANES, t // bq, bq
    nh = 2 * p

    def body(q_ref, k_ref, v_ref, ft_ref, fs_ref, o_ref, lse_ref):
        i = pl.program_id(1)
        masks = _head_masks()
        q2 = q_ref[...]
        rowi = lax.broadcasted_iota(jnp.int32, (bq, bk), 0)
        coli = lax.broadcasted_iota(jnp.int32, (bq, bk), 1)
        causal = coli <= rowi
        outs = []
        for a in range(2):
            qa = jnp.where(masks[a], q2, jnp.zeros_like(q2))
            ft = ft_ref[a]

            def tile(j, carry, masked, qa=qa, ft=ft, a=a):
                m, l, acc = carry
                off = pl.multiple_of(j * bk, bk)
                kb = k_ref[pl.ds(off, bk), :]
                vb = v_ref[pl.ds(off, bk), :]
                s = lax.dot_general(qa, kb, NT_DIMS, preferred_element_type=F32)
                s = (s + ft) - fs_ref[a, j]
                if masked:
                    s = jnp.where(causal, s, NEG)
                m_new = jnp.maximum(m, jnp.max(s, axis=1, keepdims=True))
                pr = jnp.exp(s - m_new)
                alpha = jnp.exp(m - m_new)
                l = alpha * l + jnp.sum(pr, axis=1, keepdims=True)
                acc = alpha * acc + jnp.dot(pr.astype(BF16), vb, preferred_element_type=F32)
                return m_new, l, acc

            init = (jnp.full((bq, 1), NEG, F32), jnp.zeros((bq, 1), F32), jnp.zeros((bq, LANES), F32))
            carry = lax.fori_loop(0, i, functools.partial(tile, masked=False), init)
            m, l, acc = tile(i, carry, True)
            outs.append(acc / l)
            lse_ref[a] = m + jnp.log(l)
        o_ref[...] = jnp.where(masks[0], outs[0], outs[1])

    qs, ks, vs = _attn_specs(t, dg, bq, 0)
    col = pl.BlockSpec((2, bq, 1), lambda h, i: (h, i, 0))
    return _pcall(
        body, name="fox_fwd", grid=(p, nq),
        out_shape=[jax.ShapeDtypeStruct((t, dg), F32), jax.ShapeDtypeStruct((nh, t, 1), F32)],
        in_specs=[qs, ks, vs, col, pl.BlockSpec((2, nq, 1, bk), lambda h, i: (h, 0, 0, 0))],
        out_specs=[pl.BlockSpec((bq, LANES), lambda h, i: (i, h)), col],
        compiler_params=_params(("arbitrary", "arbitrary")),
    )(qkv, qkv, qkv, fcol, frow)


def _fox_bwd(qkv, do, fcol, frow, lse, delta, dg, bq):
    t = qkv.shape[0]
    p, nq, bk = dg // LANES, t // bq, bq
    nh = 2 * p

    def body(q_ref, k_ref, v_ref, do_ref, ft_ref, fs_ref, lse_ref, dl_ref, dq_ref, dk_ref, dv_ref, dfs_ref, dft_ref):
        i = pl.program_id(1)

        @pl.when(i == 0)
        def _():
            dk_ref[...] = jnp.zeros_like(dk_ref)
            dv_ref[...] = jnp.zeros_like(dv_ref)
            dfs_ref[...] = jnp.zeros_like(dfs_ref)

        masks = _head_masks()
        q2, do2 = q_ref[...], do_ref[...]
        rowi = lax.broadcasted_iota(jnp.int32, (bq, bk), 0)
        coli = lax.broadcasted_iota(jnp.int32, (bq, bk), 1)
        causal = coli <= rowi
        outs = []
        for a in range(2):
            qa = jnp.where(masks[a], q2, jnp.zeros_like(q2))
            doa = jnp.where(masks[a], do2, jnp.zeros_like(do2))
            ft, lse_a, dl = ft_ref[a], lse_ref[a], dl_ref[a]

            def tile(j, carry, masked, qa=qa, doa=doa, ft=ft, lse_a=lse_a, dl=dl, a=a):
                dq, dft = carry
                off = pl.multiple_of(j * bk, bk)
                kb = k_ref[pl.ds(off, bk), :]
                vb = v_ref[pl.ds(off, bk), :]
                s = lax.dot_general(qa, kb, NT_DIMS, preferred_element_type=F32)
                s = (s + ft) - fs_ref[a, j]
                if masked:
                    s = jnp.where(causal, s, NEG)
                pr = jnp.exp(s - lse_a)
                dp = lax.dot_general(doa, vb, NT_DIMS, preferred_element_type=F32)
                ds = pr * (dp - dl)
                dsb = ds.astype(BF16)
                dk_ref[pl.ds(off, bk), :] += lax.dot_general(dsb, qa, TN_DIMS, preferred_element_type=F32)
                dv_ref[pl.ds(off, bk), :] += lax.dot_general(pr.astype(BF16), doa, TN_DIMS, preferred_element_type=F32)
                dfs_ref[a, j] += -jnp.sum(ds, axis=0, keepdims=True)
                return dq + jnp.dot(dsb, kb, preferred_element_type=F32), dft + jnp.sum(ds, axis=1, keepdims=True)

            init = (jnp.zeros((bq, LANES), F32), jnp.zeros((bq, 1), F32))
            carry = lax.fori_loop(0, i, functools.partial(tile, masked=False), init)
            dq, dft = tile(i, carry, True)
            outs.append(dq)
            dft_ref[a] = dft
        dq_ref[...] = jnp.where(masks[0], outs[0], outs[1])

    qs, ks, vs = _attn_specs(t, dg, bq, 0)
    col = pl.BlockSpec((2, bq, 1), lambda h, i: (h, i, 0))
    rowspec = pl.BlockSpec((2, nq, 1, bk), lambda h, i: (h, 0, 0, 0))
    blk = pl.BlockSpec((bq, LANES), lambda h, i: (i, h))
    acc = pl.BlockSpec((t, LANES), lambda h, i: (0, h))
    return _pcall(
        body, name="fox_bwd", grid=(p, nq),
        out_shape=[jax.ShapeDtypeStruct((t, dg), F32)] * 3 + [jax.ShapeDtypeStruct((nh, nq, 1, bk), F32),
                                                              jax.ShapeDtypeStruct((nh, t, 1), F32)],
        in_specs=[qs, ks, vs, blk, col, rowspec, col, col],
        out_specs=[blk, acc, acc, rowspec, col],
        compiler_params=_params(("arbitrary", "arbitrary")),
    )(qkv, qkv, qkv, do, fcol, frow, lse, delta)


def _softplus_parts(z):
    e = jnp.exp(-jnp.abs(z))
    return jnp.maximum(z, 0.0) + jnp.log1p(e), e


def _sb_fwd(qkv, dg, bq):
    t = qkv.shape[0]
    p, nq, bk = dg // LANES, t // bq, bq
    nh = 2 * p

    def body(q_ref, k_ref, v_ref, o_ref, rt_ref):
        i = pl.program_id(1)
        masks = _head_masks()
        q2 = q_ref[...]
        rowi = lax.broadcasted_iota(jnp.int32, (bq, bk), 0)
        coli = lax.broadcasted_iota(jnp.int32, (bq, bk), 1)
        strict = coli < rowi
        suffix = (rowi >= coli).astype(BF16)
        outs = []
        for a in range(2):
            qa = jnp.where(masks[a], q2, jnp.zeros_like(q2))

            def tile(j, carry, masked, qa=qa):
                rest, acc = carry
                off = pl.multiple_of(j * bk, bk)
                kb = k_ref[pl.ds(off, bk), :]
                vb = v_ref[pl.ds(off, bk), :]
                z = lax.dot_general(qa, kb, NT_DIMS, preferred_element_type=F32)
                sp, _ = _softplus_parts(z)
                if masked:
                    sp = jnp.where(strict, sp, 0.0)
                cs = _split_dot(sp, suffix, 2)
                w = jnp.exp(z - cs - rest)
                if masked:
                    w = jnp.where(strict, w, 0.0)
                acc = acc + jnp.dot(w.astype(BF16), vb, preferred_element_type=F32)
                return rest + cs[:, 0:1], acc

            carry = tile(i, (jnp.zeros((bq, 1), F32), jnp.zeros((bq, LANES), F32)), True)
            rest, acc = lax.fori_loop(0, i, lambda jj, c: tile(i - 1 - jj, c, False), carry)
            outs.append(acc)
            rt_ref[a] = rest
        o_ref[...] = jnp.where(masks[0], outs[0], outs[1])

    qs, ks, vs = _attn_specs(t, dg, bq, 3)
    col = pl.BlockSpec((2, bq, 1), lambda h, i: (h, i, 0))
    return _pcall(
        body, name="sb_fwd", grid=(p, nq),
        out_shape=[jax.ShapeDtypeStruct((t, dg), F32), jax.ShapeDtypeStruct((nh, t, 1), F32)],
        in_specs=[qs, ks, vs],
        out_specs=[pl.BlockSpec((bq, LANES), lambda h, i: (i, h)), col],
        compiler_params=_params(("arbitrary", "arbitrary")),
    )(qkv, qkv, qkv)


def _sb_bwd(qkv, do, rtot, dg, bq):
    t = qkv.shape[0]
    p, nq, bk = dg // LANES, t // bq, bq

    def body(q_ref, k_ref, v_ref, do_ref, rt_ref, dq_ref, dk_ref, dv_ref):
        i = pl.program_id(1)

        @pl.when(i == 0)
        def _():
            dk_ref[...] = jnp.zeros_like(dk_ref)
            dv_ref[...] = jnp.zeros_like(dv_ref)

        masks = _head_masks()
        q2, do2 = q_ref[...], do_ref[...]
        rowi = lax.broadcasted_iota(jnp.int32, (bq, bk), 0)
        coli = lax.broadcasted_iota(jnp.int32, (bq, bk), 1)
        strict = coli < rowi
        prefix = (rowi <= coli).astype(BF16)
        outs = []
        for a in range(2):
            qa = jnp.where(masks[a], q2, jnp.zeros_like(q2))
            doa = jnp.where(masks[a], do2, jnp.zeros_like(do2))
            rt = rt_ref[a]

            def tile(j, carry, masked, qa=qa, doa=doa, rt=rt):
                before, gbefore, dq = carry
                off = pl.multiple_of(j * bk, bk)
                kb = k_ref[pl.ds(off, bk), :]
                vb = v_ref[pl.ds(off, bk), :]
                z = lax.dot_general(qa, kb, NT_DIMS, preferred_element_type=F32)
                sp, e = _softplus_parts(z)
                sig = jnp.where(z >= 0, 1.0, e) / (1.0 + e)
                if masked:
                    sp = jnp.where(strict, sp, 0.0)
                pre = _split_dot(sp, prefix, 2)
                w = jnp.exp(z - rt + before + pre - sp)
                if masked:
                    w = jnp.where(strict, w, 0.0)
                da = lax.dot_general(doa, vb, NT_DIMS, preferred_element_type=F32)
                g = w * da
                preg = _split_dot(g, prefix, 2)
                dz = g * (1.0 - sig) - sig * (gbefore + preg - g)
                if masked:
                    dz = jnp.where(strict, dz, 0.0)
                dzb = dz.astype(BF16)
                dk_ref[pl.ds(off, bk), :] += lax.dot_general(dzb, qa, TN_DIMS, preferred_element_type=F32)
                dv_ref[pl.ds(off, bk), :] += lax.dot_general(w.astype(BF16), doa, TN_DIMS, preferred_element_type=F32)
                dq = dq + jnp.dot(dzb, kb, preferred_element_type=F32)
                return before + pre[:, bk - 1:bk], gbefore + preg[:, bk - 1:bk], dq

            init = (jnp.zeros((bq, 1), F32), jnp.zeros((bq, 1), F32), jnp.zeros((bq, LANES), F32))
            carry = lax.fori_loop(0, i, functools.partial(tile, masked=False), init)
            outs.append(tile(i, carry, True)[2])
        dq_ref[...] = jnp.where(masks[0], outs[0], outs[1])

    qs, ks, vs = _attn_specs(t, dg, bq, 3)
    col = pl.BlockSpec((2, bq, 1), lambda h, i: (h, i, 0))
    blk = pl.BlockSpec((bq, LANES), lambda h, i: (i, h))
    acc = pl.BlockSpec((t, LANES), lambda h, i: (0, h))
    return _pcall(
        body, name="sb_bwd", grid=(p, nq),
        out_shape=[jax.ShapeDtypeStruct((t, dg), F32)] * 3,
        in_specs=[qs, ks, vs, blk, col],
        out_specs=[blk, acc, acc],
        compiler_params=_params(("arbitrary", "arbitrary")),
    )(qkv, qkv, qkv, do, rtot)


def _tri_constants(nh, t):
    nb = t // LANES
    r = nh * nb
    li = np.arange(LANES)
    tri_in = (li[:, None] <= li[None, :])
    ri = np.arange(r)
    same = (ri[:, None] // nb) == (ri[None, :] // nb)
    blk = same & (ri[None, :] < ri[:, None])
    blk_rev = same & (ri[None, :] > ri[:, None])
    head_rows = (np.arange(max(8, nh))[:, None] == (ri[None, :] // nb))
    as_bf16 = lambda a: jnp.asarray(a.astype(np.float32), BF16)
    return as_bf16(tri_in), as_bf16(blk), as_bf16(tri_in.T), as_bf16(blk_rev), as_bf16(head_rows)


def kernel(x, c, w_ada, b_ada, g_attn, w_in, b_fgate, g_out_fox, g_out_sb, w_out, g_mlp, w_up, conv_w, conv_b, w_down, g_final, loss_target, m_w_ada, m_b_ada, m_g_attn, m_w_in, m_b_fgate, m_g_out_fox, m_g_out_sb, m_w_out, m_g_mlp, m_w_up, m_conv_w, m_conv_b, m_w_down, m_g_final, v_w_ada, v_b_ada, v_g_attn, v_w_in, v_b_fgate, v_g_out_fox, v_g_out_sb, v_w_out, v_g_mlp, v_w_up, v_conv_w, v_conv_b, v_w_down, v_g_final):
    t, d = x.shape[1], x.shape[2]
    dg = d // 2
    nh = dg // HEAD_DIM
    n_in = 6 * dg + nh
    dff = w_down.shape[1] * 4
    dfp = -(-dff // 256) * 256
    cf = 256
    tm = _tile(t, (512, 256, 128))
    bq = _tile(t, (256, 128))
    nq = t // bq
    xi, yi, ci = lax.axis_index("x"), lax.axis_index("y"), lax.axis_index("c")
    shard = 2 * xi + yi
    me = 4 * xi + 2 * yi + ci

    x2d, tg2d = x[0], loss_target[0]

    c_all = _all_gather8(jnp.pad(c, ((0, 7), (0, 0)))).reshape(8, 8, d)[:, 0, :]
    ada_cols = w_ada.shape[2]
    b_shard = lax.dynamic_slice(b_ada, (0, shard * ada_cols), (1, ada_cols))
    sc_all, mod_shard = _ada_fwd(c_all, w_ada[0], b_shard)
    mod_all = _all_gather8(mod_shard).reshape(4, 2, 8, ada_cols)
    mod_me = lax.dynamic_index_in_dim(mod_all[:, 0], me, axis=1, keepdims=False)
    mod8 = jnp.pad(mod_me.reshape(6, d), ((0, 2), (0, 0)))

    g_in, g_out, g_up, g_down, g_cw = _gather_xy(
        [w_in[0].astype(BF16), w_out[0].astype(BF16), w_up[0].astype(BF16), w_down[0].astype(BF16), conv_w[0]])
    w_in_full = jnp.transpose(g_in, (1, 0, 2)).reshape(d, n_in)
    w_qkv = w_in_full[:, :6 * dg]
    w_f = jnp.pad(w_in_full[:, 6 * dg:], ((0, 0), (0, LANES - nh)))
    w_out_full = g_out.reshape(2 * dg, d)
    w_up_full = jnp.transpose(g_up, (1, 0, 2)).reshape(d, 2 * dff)
    padc = ((0, 0), (0, dfp - dff))
    wg, wv = jnp.pad(w_up_full[:, :dff], padc), jnp.pad(w_up_full[:, dff:], padc)
    wd = jnp.pad(g_down.reshape(dff, d), ((0, dfp - dff), (0, 0)))
    cw_full = jnp.transpose(g_cw, (1, 0, 2)).reshape(3, 2 * dff)
    cw4 = jnp.concatenate([cw_full, conv_b], axis=0)
    cwg = jnp.pad(cw4[:, :dff], ((0, 4), (0, dfp - dff)))
    cwv = jnp.pad(cw4[:, dff:], ((0, 4), (0, dfp - dff)))

    qkv, fl, h1 = _in_proj_fwd(x2d, mod8, g_attn, w_qkv, w_f, tm)
    tri_in, tri_blk, tri_in_rev, tri_blk_rev, head_rows = _tri_constants(nh, t)
    fl2d = fl[:, :nh].T.reshape(nh * t // LANES, LANES)
    b_rows = jnp.repeat(b_fgate[0], t // LANES)[:, None]
    f2d = _fgate_fwd(fl2d, b_rows, tri_in, tri_blk)
    fcol = f2d.reshape(nh, t, 1)
    frow = f2d.reshape(nh, nq, 1, bq)
    o_fox, lse = _fox_fwd(qkv, fcol, frow, dg, bq)
    o_sb, rtot = _sb_fwd(qkv, dg, bq)
    li = np.arange(dg)
    bd = jnp.asarray((li[:, None] // HEAD_DIM == li[None, :] // HEAD_DIM).astype(np.float32), BF16)
    hsel = jnp.asarray((li[:, None] // HEAD_DIM == np.arange(LANES)[None, :]).astype(np.float32), BF16)
    x2, mix = _attn_out_fwd(x2d, o_fox, o_sb, g_out_fox, g_out_sb, w_out_full, mod8, bd, tm)
    g_final2 = g_final[None, :]
    dx3, h2, part_f = _ffn_fwd(x2, tg2d, mod8, g_mlp, g_final2, wg, wv, cwg, cwv, wd, tm, cf)

    dx2, dupg, dupv, act, dxg3, part_b, pcg, pcv = _ffn_bwd(x2, dx3, mod8, g_mlp, wg, wv, cwg, cwv, wd, tm, cf)
    do_fox, do_sb, delta, dxg2, part_o = _attn_out_bwd(dx2, mix, o_fox, o_sb, g_out_fox, g_out_sb, w_out_full, mod8, bd, hsel, tm)
    dcol = delta[:, :nh].T.reshape(nh, t, 1)
    dq_f, dk_f, dv_f, dfs, dft = _fox_bwd(qkv, do_fox, fcol, frow, lse, dcol, dg, bq)
    dq_s, dk_s, dv_s = _sb_bwd(qkv, do_sb, rtot, dg, bq)
    f2d_shape = (nh * t // LANES, LANES)
    dfl2d, gb8 = _fgate_bwd(fl2d, b_rows, dft.reshape(f2d_shape), dfs.reshape(f2d_shape), tri_in_rev, tri_blk_rev, head_rows)
    dfl = jnp.pad(dfl2d.reshape(nh, t).T, ((0, 0), (0, LANES - nh)))
    grad_x, dproj, dflb, part_i = _in_proj_bwd([dq_f, dk_f, dv_f, dq_s, dk_s, dv_s], dfl, w_qkv, w_f, x2d, dx2, mod8, g_attn, tm)

    gw_qkv = _matmul_tn(h1, dproj, "grad_w_qkv")
    gw_f = _matmul_tn(h1, dflb, "grad_w_f")
    gw_in = jnp.concatenate([gw_qkv, gw_f[:, :nh]], axis=1)
    gw_out = _matmul_tn(mix, dxg2, "grad_w_out")
    gw_upg = _matmul_tn(h2, dupg, "grad_w_up_gate")
    gw_upv = _matmul_tn(h2, dupv, "grad_w_up_val")
    gw_up = jnp.concatenate([gw_upg[:, :dff], gw_upv[:, :dff]], axis=1)
    gw_down = _matmul_tn(act, dxg3, "grad_w_down")[:dff]

    sf = _sum_leading(part_f, "sum_part_ffn_fwd")
    sb_ = _sum_leading(part_b, "sum_part_ffn_bwd")
    so = _sum_leading(part_o, "sum_part_attn_out")
    si = _sum_leading(part_i, "sum_part_in_proj")
    scg = _sum_leading(pcg, "sum_part_conv_gate")
    scv = _sum_leading(pcv, "sum_part_conv_val")
    gb_f = gb8[:nh, 0]
    dmod = jnp.concatenate([si[0], si[1], so[0], sb_[0], sb_[1], sf[1]])
    g_conv_w = jnp.concatenate([scg[0:3, :dff], scv[0:3, :dff]], axis=1).reshape(-1)
    g_conv_b = jnp.concatenate([scg[3, :dff], scv[3, :dff]])
    loss_part = jnp.sum(sf[2])
    fields = [dmod, si[2], gb_f, so[1, :dg], so[1, dg:], sb_[2], g_conv_b, sf[0], g_conv_w, loss_part[None]]
    sizes = [int(f.shape[0]) for f in fields]
    n_pack = sum(sizes)
    lanes_pack = -(-n_pack // (8 * LANES)) * LANES
    pack = jnp.pad(jnp.concatenate(fields), (0, 8 * lanes_pack - n_pack)).reshape(8, lanes_pack)
    gathered = _all_gather8(pack)
    tot = _sum_leading(gathered.reshape(8, 8, lanes_pack), "sum_pack").reshape(-1)
    offs = np.concatenate([[0], np.cumsum(sizes)])
    take = lambda k: tot[int(offs[k]):int(offs[k + 1])]
    g_b_ada, g_g_attn, g_b_fgate, g_g_fox, g_g_sb, g_g_mlp, g_cb, g_g_final, g_cw_full, loss_v = [take(k) for k in range(10)]
    loss = loss_v[0]
    dmod_all = gathered.reshape(8, 8 * lanes_pack)[:, :6 * d]
    dmod_cols = lax.dynamic_slice(dmod_all, (0, shard * ada_cols), (8, ada_cols))
    g_w_ada = _ada_bwd(sc_all.T, dmod_cols)

    def col_pieces(g):
        r, cc = g.shape
        return jnp.transpose(g.reshape(2, r // 2, 4, cc // 4), (2, 0, 1, 3)).reshape(8, r // 2, cc // 4)

    def row_pieces(g):
        r, cc = g.shape
        return g.reshape(8, r // 8, cc)

    recv = _scatter8([col_pieces(gw_in), row_pieces(gw_out), col_pieces(gw_up), row_pieces(gw_down)])
    halves = [_sum_leading(rv, nm) for rv, nm in zip(recv, ("sum_w_in", "sum_w_out", "sum_w_up", "sum_w_down"))]
    swapped = _swap_halves(halves)
    g_w_in, g_w_out, g_w_up, g_w_down = [s.reshape(2 * s.shape[1], s.shape[2]) for s in swapped]
    g_conv_w_shard = lax.dynamic_slice(g_cw_full.reshape(3, 2 * dff), (0, shard * (dff // 2)), (3, dff // 2))

    grads, deltas, new_m, new_v = {}, {}, {}, {}

    def step(name, w, g, m, v):
        shape = w.shape
        as2d = lambda a: a.reshape(-1, shape[-1])
        dl, nm, nv = _adamw(as2d(w), as2d(g), as2d(m), as2d(v), "adamw_" + name)
        grads[name], deltas[name], new_m[name], new_v[name] = g.reshape(shape), dl.reshape(shape), nm.reshape(shape), nv.reshape(shape)

    step("w_ada", w_ada, g_w_ada, m_w_ada, v_w_ada)
    step("w_in", w_in, g_w_in, m_w_in, v_w_in)
    step("w_out", w_out, g_w_out, m_w_out, v_w_out)
    step("w_up", w_up, g_w_up, m_w_up, v_w_up)
    step("conv_w", conv_w, g_conv_w_shard, m_conv_w, v_conv_w)
    step("w_down", w_down, g_w_down, m_w_down, v_w_down)

    small = [("b_ada", b_ada, g_b_ada, m_b_ada, v_b_ada), ("g_attn", g_attn, g_g_attn, m_g_attn, v_g_attn),
             ("b_fgate", b_fgate, g_b_fgate, m_b_fgate, v_b_fgate), ("g_out_fox", g_out_fox, g_g_fox, m_g_out_fox, v_g_out_fox),
             ("g_out_sb", g_out_sb, g_g_sb, m_g_out_sb, v_g_out_sb), ("g_mlp", g_mlp, g_g_mlp, m_g_mlp, v_g_mlp),
             ("conv_b", conv_b, g_cb, m_conv_b, v_conv_b), ("g_final", g_final, g_g_final, m_g_final, v_g_final)]
    ssz = [int(np.prod(s[1].shape)) for s in small]
    n_small = sum(ssz)
    lanes_small = -(-n_small // (8 * LANES)) * LANES
    packs = [jnp.pad(jnp.concatenate([s[k].reshape(-1) for s in small]), (0, 8 * lanes_small - n_small)).reshape(8, lanes_small)
             for k in (1, 2, 3, 4)]
    dl_s, nm_s, nv_s = _adamw(*packs, "adamw_small")
    so_ = np.concatenate([[0], np.cumsum(ssz)])
    for k, s in enumerate(small):
        cut = lambda a: a.reshape(-1)[int(so_[k]):int(so_[k + 1])].reshape(s[1].shape)
        grads[s[0]], deltas[s[0]], new_m[s[0]], new_v[s[0]] = s[2].reshape(s[1].shape), cut(dl_s), cut(nm_s), cut(nv_s)

    order = ["w_ada", "b_ada", "g_attn", "w_in", "b_fgate", "g_out_fox", "g_out_sb", "w_out", "g_mlp", "w_up",
             "conv_w", "conv_b", "w_down", "g_final"]
    return (loss, grad_x[None], *[grads[n] for n in order], *[deltas[n] for n in order],
            *[new_m[n] for n in order], *[new_v[n] for n in order])
```

```python
import functools

import numpy as np
import jax
import jax.numpy as jnp
from jax import lax
from jax.experimental import pallas as pl
from jax.experimental.pallas import tpu as pltpu

F32 = jnp.float32
BF16 = jnp.bfloat16
MESH = pl.DeviceIdType.MESH

HEAD_DIM = 64
LANES = 128
EPS = 1e-6
NEG = -1e30
ADAM_LR, ADAM_B1, ADAM_B2, ADAM_EPS, ADAM_WD, ADAM_STEP = 0.001, 0.9, 0.999, 1e-08, 0.01, 10
V7X_VMEM_BYTES = 64 * 1024 * 1024
VMEM_LIMIT = V7X_VMEM_BYTES - 12 * 1024 * 1024
NT_DIMS = (((1,), (1,)), ((), ()))
TN_DIMS = (((0,), (0,)), ((), ()))


def _pcall(body, **kw):
    return pl.pallas_call(body, **kw)


def _params(sem=None, **kw):
    return pltpu.CompilerParams(dimension_semantics=sem, vmem_limit_bytes=VMEM_LIMIT, **kw)


def _split_dot(x, m, passes):
    acc = None
    for _ in range(passes):
        part = x.astype(BF16)
        d = jnp.dot(part, m, preferred_element_type=F32)
        acc = d if acc is None else acc + d
        x = x - part.astype(F32)
    return acc


def _tile(n, candidates):
    for t in candidates:
        if n % t == 0:
            return t
    return n


def _rows_tile(rows, row_bytes, budget=2 * 1024 * 1024):
    best = None
    for t in range(8, rows + 1, 8):
        if rows % t == 0 and t * row_bytes <= budget:
            best = t
    return best if best is not None else rows


def _all_gather8(v):
    m_per, n = v.shape

    def body(x_ref, out_ref, send_sems, recv_sems, local_sem):
        x, y, c = lax.axis_index("x"), lax.axis_index("y"), lax.axis_index("c")
        me, sibling = (x, y, c), (x, y, 1 - c)
        chips = [(1 - x, y), (x, 1 - y), (1 - x, 1 - y)]

        def rows(px, py, pc):
            return out_ref.at[pl.ds((4 * px + 2 * py + pc) * m_per, m_per), :]

        def copy(k, block, to, src=None):
            return pltpu.make_async_remote_copy(
                src_ref=rows(*block) if src is None else src, dst_ref=rows(*block),
                send_sem=send_sems.at[k], recv_sem=recv_sems.at[k], device_id=to, device_id_type=MESH)

        mine = pltpu.make_async_copy(x_ref, rows(*me), local_sem)
        mine.start()
        first = [copy(0, me, sibling, src=x_ref)]
        first += [copy(1 + j, me, (*chip, c), src=x_ref) for j, chip in enumerate(chips)]
        for cp in first:
            cp.start()
        passed = [copy(4 + j, (*chip, c), sibling) for j, chip in enumerate(chips)]
        for j, chip in enumerate(chips):
            copy(1 + j, (*chip, c), me).wait_recv()
            passed[j].start()
        copy(0, sibling, me).wait_recv()
        for j, chip in enumerate(chips):
            copy(4 + j, (*chip, 1 - c), me).wait_recv()
        for cp in first + passed:
            cp.wait_send()
        mine.wait()

    return _pcall(
        body, name="all_gather8",
        out_shape=jax.ShapeDtypeStruct((8 * m_per, n), v.dtype),
        in_specs=[pl.BlockSpec(memory_space=pltpu.VMEM)],
        out_specs=pl.BlockSpec(memory_space=pltpu.VMEM),
        scratch_shapes=[pltpu.SemaphoreType.DMA((7,)), pltpu.SemaphoreType.DMA((7,)), pltpu.SemaphoreType.DMA],
        compiler_params=pltpu.CompilerParams(vmem_limit_bytes=VMEM_LIMIT),
    )(v)


def _gather_xy(shards):
    n = len(shards)

    def body(*refs):
        ins, outs = refs[:n], refs[n:2 * n]
        send_sems, recv_sems, local_sems = refs[2 * n:]
        x, y, c = lax.axis_index("x"), lax.axis_index("y"), lax.axis_index("c")
        chips = [(1 - x, y), (x, 1 - y), (1 - x, 1 - y)]
        mine = 2 * x + y
        local, remote = [], []
        for w in range(n):
            cp = pltpu.make_async_copy(ins[w], outs[w].at[mine], local_sems.at[w])
            cp.start()
            local.append(cp)
            for k, (px, py) in enumerate(chips):
                cp = pltpu.make_async_remote_copy(
                    src_ref=ins[w], dst_ref=outs[w].at[mine], send_sem=send_sems.at[3 * w + k],
                    recv_sem=recv_sems.at[3 * w + k], device_id=(px, py, c), device_id_type=MESH)
                cp.start()
                remote.append(cp)
        for cp in remote:
            cp.wait_recv()
        for cp in remote:
            cp.wait_send()
        for cp in local:
            cp.wait()

    hbm = pl.BlockSpec(memory_space=pltpu.HBM)
    return _pcall(
        body, name="gather_xy",
        out_shape=[jax.ShapeDtypeStruct((4,) + s.shape, s.dtype) for s in shards],
        in_specs=[hbm] * n, out_specs=[hbm] * n,
        scratch_shapes=[pltpu.SemaphoreType.DMA((3 * n,)), pltpu.SemaphoreType.DMA((3 * n,)),
                        pltpu.SemaphoreType.DMA((n,))],
        compiler_params=pltpu.CompilerParams(vmem_limit_bytes=VMEM_LIMIT),
    )(*shards)


def _scatter8(pieces):
    n = len(pieces)

    def body(*refs):
        ins, outs = refs[:n], refs[n:2 * n]
        send_sems, recv_sems, local_sems = refs[2 * n:]
        x, y, c = lax.axis_index("x"), lax.axis_index("y"), lax.axis_index("c")
        me = 4 * x + 2 * y + c
        local, remote = [], []
        for w in range(n):
            cp = pltpu.make_async_copy(ins[w].at[me], outs[w].at[me], local_sems.at[w])
            cp.start()
            local.append(cp)
            for f in range(1, 8):
                px = 1 - x if f & 4 else x
                py = 1 - y if f & 2 else y
                pc = 1 - c if f & 1 else c
                cp = pltpu.make_async_remote_copy(
                    src_ref=ins[w].at[4 * px + 2 * py + pc], dst_ref=outs[w].at[me],
                    send_sem=send_sems.at[7 * w + f - 1], recv_sem=recv_sems.at[7 * w + f - 1],
                    device_id=(px, py, pc), device_id_type=MESH)
                cp.start()
                remote.append(cp)
        for cp in remote:
            cp.wait_recv()
        for cp in remote:
            cp.wait_send()
        for cp in local:
            cp.wait()

    hbm = pl.BlockSpec(memory_space=pltpu.HBM)
    return _pcall(
        body, name="scatter8",
        out_shape=[jax.ShapeDtypeStruct(p.shape, p.dtype) for p in pieces],
        in_specs=[hbm] * n, out_specs=[hbm] * n,
        scratch_shapes=[pltpu.SemaphoreType.DMA((7 * n,)), pltpu.SemaphoreType.DMA((7 * n,)),
                        pltpu.SemaphoreType.DMA((n,))],
        compiler_params=pltpu.CompilerParams(vmem_limit_bytes=VMEM_LIMIT),
    )(*pieces)


def _swap_halves(halves):
    n = len(halves)

    def body(*refs):
        ins, outs = refs[:n], refs[n:2 * n]
        send_sems, recv_sems, local_sems = refs[2 * n:]
        x, y, c = lax.axis_index("x"), lax.axis_index("y"), lax.axis_index("c")
        local, remote = [], []
        for w in range(n):
            cp = pltpu.make_async_copy(ins[w], outs[w].at[c], local_sems.at[w])
            cp.start()
            local.append(cp)
            cp = pltpu.make_async_remote_copy(
                src_ref=ins[w], dst_ref=outs[w].at[c], send_sem=send_sems.at[w], recv_sem=recv_sems.at[w],
                device_id=(x, y, 1 - c), device_id_type=MESH)
            cp.start()
            remote.append(cp)
        for cp in remote:
            cp.wait_recv()
        for cp in remote:
            cp.wait_send()
        for cp in local:
            cp.wait()

    hbm = pl.BlockSpec(memory_space=pltpu.HBM)
    return _pcall(
        body, name="swap_halves",
        out_shape=[jax.ShapeDtypeStruct((2,) + h.shape, h.dtype) for h in halves],
        in_specs=[hbm] * n, out_specs=[hbm] * n,
        scratch_shapes=[pltpu.SemaphoreType.DMA((n,)), pltpu.SemaphoreType.DMA((n,)), pltpu.SemaphoreType.DMA((n,))],
        compiler_params=pltpu.CompilerParams(vmem_limit_bytes=VMEM_LIMIT),
    )(*halves)


def _sum_leading(a, name):
    n, r, c = a.shape
    tr = _rows_tile(r, n * c * 4, budget=6 * 1024 * 1024)

    def body(a_ref, o_ref):
        acc = a_ref[0]
        for k in range(1, n):
            acc = acc + a_ref[k]
        o_ref[...] = acc

    return _pcall(
        body, name=name, grid=(r // tr,),
        out_shape=jax.ShapeDtypeStruct((r, c), a.dtype),
        in_specs=[pl.BlockSpec((n, tr, c), lambda i: (0, i, 0))],
        out_specs=pl.BlockSpec((tr, c), lambda i: (i, 0)),
        compiler_params=_params(("arbitrary",)),
    )(a)


def _adamw(w, g, m, v, name):
    r, c = w.shape
    tr = _rows_tile(r, c * 4, budget=1024 * 1024)
    c1 = 1.0 - ADAM_B1 ** ADAM_STEP
    c2 = 1.0 - ADAM_B2 ** ADAM_STEP

    def body(w_ref, g_ref, m_ref, v_ref, d_ref, nm_ref, nv_ref):
        gg = g_ref[...]
        nm = ADAM_B1 * m_ref[...] + (1.0 - ADAM_B1) * gg
        nv = ADAM_B2 * v_ref[...] + (1.0 - ADAM_B2) * (gg * gg)
        m_hat = nm / c1
        v_hat = nv / c2
        d_ref[...] = -ADAM_LR * (m_hat / (jnp.sqrt(v_hat) + ADAM_EPS) + ADAM_WD * w_ref[...])
        nm_ref[...] = nm
        nv_ref[...] = nv

    spec = pl.BlockSpec((tr, c), lambda i: (i, 0))
    return _pcall(
        body, name=name, grid=(r // tr,),
        out_shape=[jax.ShapeDtypeStruct((r, c), F32)] * 3,
        in_specs=[spec] * 4, out_specs=[spec] * 3,
        compiler_params=_params(("arbitrary",)),
    )(w, g, m, v)


def _ada_fwd(c_all, w_shard, b_shard):
    nb, d = c_all.shape
    cols = w_shard.shape[1]

    def body(c_ref, w_ref, b_ref, sc_ref, mod_ref):
        cv = c_ref[...]
        sc = cv * jax.nn.sigmoid(cv)
        sc_ref[...] = sc
        mod_ref[...] = jnp.dot(sc.astype(BF16), w_ref[...].astype(BF16), preferred_element_type=F32) + b_ref[...]

    return _pcall(
        body, name="ada_fwd",
        out_shape=[jax.ShapeDtypeStruct((nb, d), F32), jax.ShapeDtypeStruct((nb, cols), F32)],
        compiler_params=pltpu.CompilerParams(vmem_limit_bytes=VMEM_LIMIT),
    )(c_all, w_shard, b_shard)


def _ada_bwd(sc_t, dmod_cols):
    d, nb = sc_t.shape
    cols = dmod_cols.shape[1]
    tr = _rows_tile(d, cols * 4, budget=1024 * 1024)

    def body(s_ref, m_ref, o_ref):
        s = s_ref[...]
        m = m_ref[...]
        acc = s[:, 0:1] * m[0:1, :]
        for b in range(1, nb):
            acc = acc + s[:, b:b + 1] * m[b:b + 1, :]
        o_ref[...] = acc

    return _pcall(
        body, name="ada_bwd", grid=(d // tr,),
        out_shape=jax.ShapeDtypeStruct((d, cols), F32),
        in_specs=[pl.BlockSpec((tr, nb), lambda i: (i, 0)), pl.BlockSpec((nb, cols), lambda i: (0, 0))],
        out_specs=pl.BlockSpec((tr, cols), lambda i: (i, 0)),
        compiler_params=_params(("arbitrary",)),
    )(sc_t, dmod_cols)


def _log_sigmoid(x):
    return jnp.minimum(x, 0.0) - jnp.log1p(jnp.exp(-jnp.abs(x)))


def _fgate_fwd(fl2d, b_rows, tri_in, tri_blk):
    r = fl2d.shape[0]

    def body(x_ref, b_ref, u_ref, l_ref, f_ref):
        lf = _log_sigmoid(x_ref[...] + b_ref[...])
        c1 = _split_dot(lf, u_ref[...], 3)
        tot = jnp.broadcast_to(c1[:, LANES - 1:LANES], (r, LANES))
        acc = None
        for _ in range(3):
            part = tot.astype(BF16)
            dd = jnp.dot(l_ref[...], part, preferred_element_type=F32)
            acc = dd if acc is None else acc + dd
            tot = tot - part.astype(F32)
        f_ref[...] = c1 + acc

    return _pcall(
        body, name="fgate_fwd", out_shape=jax.ShapeDtypeStruct((r, LANES), F32),
        compiler_params=pltpu.CompilerParams(vmem_limit_bytes=VMEM_LIMIT),
    )(fl2d, b_rows, tri_in, tri_blk)


def _fgate_bwd(fl2d, b_rows, df_query, df_key, tri_in_rev, tri_blk_rev, head_rows):
    r = fl2d.shape[0]
    nhp = head_rows.shape[0]

    def body(x_ref, b_ref, dq_ref, dk_ref, u_ref, l_ref, hr_ref, o_ref, gb_ref):
        c1 = _split_dot(dq_ref[...] + dk_ref[...], u_ref[...], 3)
        tot = jnp.broadcast_to(c1[:, 0:1], (r, LANES))
        acc = None
        for _ in range(3):
            part = tot.astype(BF16)
            dd = jnp.dot(l_ref[...], part, preferred_element_type=F32)
            acc = dd if acc is None else acc + dd
            tot = tot - part.astype(F32)
        x = x_ref[...] + b_ref[...]
        e = jnp.exp(-jnp.abs(x))
        dfl = (c1 + acc) * (jnp.where(x >= 0, e, 1.0) / (1.0 + e))
        o_ref[...] = dfl
        rs = jnp.broadcast_to(jnp.sum(dfl, axis=1, keepdims=True), (r, LANES))
        gb = None
        for _ in range(3):
            part = rs.astype(BF16)
            dd = jnp.dot(hr_ref[...], part, preferred_element_type=F32)
            gb = dd if gb is None else gb + dd
            rs = rs - part.astype(F32)
        gb_ref[...] = gb

    return _pcall(
        body, name="fgate_bwd",
        out_shape=[jax.ShapeDtypeStruct((r, LANES), F32), jax.ShapeDtypeStruct((nhp, LANES), F32)],
        compiler_params=pltpu.CompilerParams(vmem_limit_bytes=VMEM_LIMIT),
    )(fl2d, b_rows, df_query, df_key, tri_in_rev, tri_blk_rev, head_rows)


def _norm_mod(x, g, scale, shift):
    r = lax.rsqrt(jnp.mean(x * x, axis=-1, keepdims=True) + EPS)
    return (x * r * g) * (1.0 + scale) + shift


def _norm_mod_bwd(x, dh, g, scale):
    r = lax.rsqrt(jnp.mean(x * x, axis=-1, keepdims=True) + EPS)
    xn = x * r
    dshift = jnp.sum(dh, axis=0, keepdims=True)
    dscale = jnp.sum(dh * (xn * g), axis=0, keepdims=True)
    dxn_g = dh * (1.0 + scale)
    dg = jnp.sum(dxn_g * xn, axis=0, keepdims=True)
    dxn = dxn_g * g
    dx = r * (dxn - xn * jnp.mean(dxn * xn, axis=-1, keepdims=True))
    return dx, dshift, dscale, dg


def _in_proj_fwd(x, mod8, g_attn, w_qkv, w_f, tm):
    t, d = x.shape
    dg = w_qkv.shape[1] // 6

    def body(x_ref, mod_ref, g_ref, w_ref, wf_ref, qkv_ref, fl_ref, h1_ref, h_sc):
        j = pl.program_id(1)

        @pl.when(j == 0)
        def _():
            h = _norm_mod(x_ref[...], g_ref[...], mod_ref[1:2, :], mod_ref[0:1, :]).astype(BF16)
            h_sc[...] = h
            h1_ref[...] = h
            fl_ref[...] = jnp.dot(h, wf_ref[...], preferred_element_type=F32)

        s = jnp.where((j == 0) | (j == 3), HEAD_DIM ** -0.5, 1.0)
        qkv_ref[...] = (jnp.dot(h_sc[...], w_ref[...], preferred_element_type=F32) * s).astype(BF16)

    return _pcall(
        body, name="in_proj_fwd", grid=(t // tm, 6),
        out_shape=[jax.ShapeDtypeStruct((t, 6 * dg), BF16), jax.ShapeDtypeStruct((t, LANES), F32),
                   jax.ShapeDtypeStruct((t, d), BF16)],
        in_specs=[pl.BlockSpec((tm, d), lambda i, j: (i, 0)), pl.BlockSpec((8, d), lambda i, j: (0, 0)),
                  pl.BlockSpec((1, d), lambda i, j: (0, 0)), pl.BlockSpec((d, dg), lambda i, j: (0, j)),
                  pl.BlockSpec((d, LANES), lambda i, j: (0, 0))],
        out_specs=[pl.BlockSpec((tm, dg), lambda i, j: (i, j)), pl.BlockSpec((tm, LANES), lambda i, j: (i, 0)),
                   pl.BlockSpec((tm, d), lambda i, j: (i, 0))],
        scratch_shapes=[pltpu.VMEM((tm, d), BF16)],
        compiler_params=_params(("arbitrary", "arbitrary")),
    )(x, mod8, g_attn, w_qkv, w_f)


def _head_rstd(o, bd):
    return lax.rsqrt(_split_dot(o * o, bd, 3) * (1.0 / HEAD_DIM) + EPS)


def _attn_out_fwd(x, o_fox, o_sb, g_fox, g_sb, w_out, mod8, bd, tm):
    t, d = x.shape
    dg = o_fox.shape[1]

    def body(x_ref, of_ref, os_ref, gf_ref, gs_ref, w_ref, mod_ref, bd_ref, x2_ref, mix_ref):
        of, osb = of_ref[...], os_ref[...]
        mf = (of * _head_rstd(of, bd_ref[...]) * gf_ref[...]).astype(BF16)
        ms = (osb * _head_rstd(osb, bd_ref[...]) * gs_ref[...]).astype(BF16)
        mix_ref[:, :dg] = mf
        mix_ref[:, dg:] = ms
        y = jnp.dot(mf, w_ref[:dg, :], preferred_element_type=F32) + jnp.dot(ms, w_ref[dg:, :], preferred_element_type=F32)
        x2_ref[...] = x_ref[...] + mod_ref[2:3, :] * y

    row = lambda w: pl.BlockSpec((tm, w), lambda i: (i, 0))
    full = lambda a: pl.BlockSpec(a.shape, lambda i: (0,) * a.ndim)
    return _pcall(
        body, name="attn_out_fwd", grid=(t // tm,),
        out_shape=[jax.ShapeDtypeStruct((t, d), F32), jax.ShapeDtypeStruct((t, 2 * dg), BF16)],
        in_specs=[row(d), row(dg), row(dg), full(g_fox), full(g_sb), full(w_out), full(mod8), full(bd)],
        out_specs=[row(d), row(2 * dg)],
        compiler_params=_params(("arbitrary",)),
    )(x, o_fox, o_sb, g_fox, g_sb, w_out, mod8, bd)


def _attn_out_bwd(dx2, mix, o_fox, o_sb, g_fox, g_sb, w_out, mod8, bd, hsel, tm):
    t, d = dx2.shape
    dg = o_fox.shape[1]

    def body(dx_ref, mix_ref, of_ref, os_ref, gf_ref, gs_ref, w_ref, mod_ref, bd_ref, hs_ref,
             dof_ref, dos_ref, dlt_ref, dxg_ref, part_ref):
        dx = dx_ref[...]
        gate = mod_ref[2:3, :]
        dxg = (dx * gate).astype(BF16)
        dxg_ref[...] = dxg
        mixv = mix_ref[...]
        y = jnp.dot(mixv[:, :dg], w_ref[:dg, :], preferred_element_type=F32)
        y = y + jnp.dot(mixv[:, dg:], w_ref[dg:, :], preferred_element_type=F32)
        part_ref[0] = jnp.zeros((8, d), F32)
        part_ref[0, 0:1, :] = jnp.sum(dx * y, axis=0, keepdims=True)
        for grp, (o_ref, g_ref, do_ref) in enumerate(((of_ref, gf_ref, dof_ref), (os_ref, gs_ref, dos_ref))):
            dmix = lax.dot_general(dxg, w_ref[grp * dg:(grp + 1) * dg, :], NT_DIMS, preferred_element_type=F32)
            o = o_ref[...]
            r = _head_rstd(o, bd_ref[...])
            n = o * r
            part_ref[0, 1:2, grp * dg:(grp + 1) * dg] = jnp.sum(dmix * n, axis=0, keepdims=True)
            dn = dmix * g_ref[...]
            mh = _split_dot(dn * n, bd_ref[...], 3) * (1.0 / HEAD_DIM)
            do = r * (dn - n * mh)
            do_ref[...] = do.astype(BF16)
            if grp == 0:
                dlt_ref[...] = _split_dot(do * o, hs_ref[...], 3)

    row = lambda w: pl.BlockSpec((tm, w), lambda i: (i, 0))
    full = lambda a: pl.BlockSpec(a.shape, lambda i: (0,) * a.ndim)
    nt = t // tm
    return _pcall(
        body, name="attn_out_bwd", grid=(nt,),
        out_shape=[jax.ShapeDtypeStruct((t, dg), BF16), jax.ShapeDtypeStruct((t, dg), BF16),
                   jax.ShapeDtypeStruct((t, LANES), F32), jax.ShapeDtypeStruct((t, d), BF16),
                   jax.ShapeDtypeStruct((nt, 8, d), F32)],
        in_specs=[row(d), row(2 * dg), row(dg), row(dg), full(g_fox), full(g_sb), full(w_out), full(mod8),
                  full(bd), full(hsel)],
        out_specs=[row(dg), row(dg), row(LANES), row(d), pl.BlockSpec((1, 8, d), lambda i: (i, 0, 0))],
        compiler_params=_params(("arbitrary",)),
    )(dx2, mix, o_fox, o_sb, g_fox, g_sb, w_out, mod8, bd, hsel)


def _in_proj_bwd(dparts, dfl, w_qkv, w_f, x, dx2, mod8, g_attn, tm):
    t, d = x.shape
    dg = dparts[0].shape[1]

    def body(*refs):
        d_refs = refs[:6]
        dfl_ref, w_ref, wf_ref, x_ref, dx2_ref, mod_ref, g_ref, gx_ref, dp_ref, dflb_ref, part_ref = refs[6:]
        dh = None
        for k in range(6):
            dk = d_refs[k][...]
            if k in (0, 3):
                dk = dk * HEAD_DIM ** -0.5
            db = dk.astype(BF16)
            dp_ref[:, k * dg:(k + 1) * dg] = db
            term = lax.dot_general(db, w_ref[:, k * dg:(k + 1) * dg], NT_DIMS, preferred_element_type=F32)
            dh = term if dh is None else dh + term
        dfb = dfl_ref[...].astype(BF16)
        dflb_ref[...] = dfb
        dh = dh + lax.dot_general(dfb, wf_ref[...], NT_DIMS, preferred_element_type=F32)
        dx, dshift, dscale, dgn = _norm_mod_bwd(x_ref[...], dh, g_ref[...], mod_ref[1:2, :])
        gx_ref[...] = dx2_ref[...] + dx
        part_ref[0] = jnp.zeros((8, d), F32)
        part_ref[0, 0:1, :] = dshift
        part_ref[0, 1:2, :] = dscale
        part_ref[0, 2:3, :] = dgn

    row = lambda w: pl.BlockSpec((tm, w), lambda i: (i, 0))
    full = lambda a: pl.BlockSpec(a.shape, lambda i: (0,) * a.ndim)
    nt = t // tm
    return _pcall(
        body, name="in_proj_bwd", grid=(nt,),
        out_shape=[jax.ShapeDtypeStruct((t, d), F32), jax.ShapeDtypeStruct((t, 6 * dg), BF16),
                   jax.ShapeDtypeStruct((t, LANES), BF16), jax.ShapeDtypeStruct((nt, 8, d), F32)],
        in_specs=[row(dg)] * 6 + [row(LANES), full(w_qkv), full(w_f), row(d), row(d), full(mod8), full(g_attn)],
        out_specs=[row(d), row(6 * dg), row(LANES), pl.BlockSpec((1, 8, d), lambda i: (i, 0, 0))],
        compiler_params=_params(("arbitrary",)),
    )(*dparts, dfl, w_qkv, w_f, x, dx2, mod8, g_attn)


def _matmul_tn(a, b, name):
    t, m = a.shape
    n = b.shape[1]
    tm_ = _tile(m, (512, 256, 128))
    tn_ = _tile(n, (1024, 512, 256, 128))
    tk = _tile(t, (512, 256, 128))
    nk = t // tk

    def body(a_ref, b_ref, o_ref):
        k = pl.program_id(2)

        @pl.when(k == 0)
        def _():
            o_ref[...] = jnp.zeros_like(o_ref)

        o_ref[...] += lax.dot_general(a_ref[...], b_ref[...], TN_DIMS, preferred_element_type=F32)

    return _pcall(
        body, name=name, grid=(m // tm_, n // tn_, nk),
        out_shape=jax.ShapeDtypeStruct((m, n), F32),
        in_specs=[pl.BlockSpec((tk, tm_), lambda i, j, k: (k, i)), pl.BlockSpec((tk, tn_), lambda i, j, k: (k, j))],
        out_specs=pl.BlockSpec((tm_, tn_), lambda i, j, k: (i, j)),
        compiler_params=_params(("arbitrary", "arbitrary", "arbitrary")),
    )(a, b)


HALO = 16


def _conv_taps(up_ext, cw, lo, rows):
    s1 = pltpu.roll(up_ext, 1, 0)
    s2 = pltpu.roll(up_ext, 2, 0)
    u = cw[2:3, :] * up_ext[lo:lo + rows] + cw[1:2, :] * s1[lo:lo + rows] + cw[0:1, :] * s2[lo:lo + rows] + cw[3:4, :]
    return u, s1, s2


def _ffn_fwd(x2, target, mod8, g_mlp, g_final, wg, wv, cwg, cwv, wd, tm, cf):
    t, d = x2.shape
    dfp = wg.shape[1]
    nt, nc = t // tm, dfp // cf
    hb = tm // HALO

    def body(x_ref, xp_ref, tg_ref, mod_ref, g_ref, gf_ref, wg_ref, wv_ref, cg_ref, cv_ref, wd_ref,
             dx3_ref, h2_ref, part_ref, h_sc, acc_sc):
        i, j = pl.program_id(0), pl.program_id(1)

        @pl.when(j == 0)
        def _():
            xe = jnp.concatenate([xp_ref[...], x_ref[...]], axis=0)
            h = _norm_mod(xe, g_ref[...], mod_ref[4:5, :], mod_ref[3:4, :]).astype(BF16)
            h_sc[...] = h
            h2_ref[...] = h[HALO:]
            acc_sc[...] = jnp.zeros_like(acc_sc)

        rowi = lax.broadcasted_iota(jnp.int32, (tm + HALO, 1), 0)
        keep = (rowi >= HALO) | (i > 0)
        hv = h_sc[...]
        upg = jnp.where(keep, jnp.dot(hv, wg_ref[...], preferred_element_type=F32), 0.0)
        upv = jnp.where(keep, jnp.dot(hv, wv_ref[...], preferred_element_type=F32), 0.0)
        ug, _, _ = _conv_taps(upg, cg_ref[...], HALO, tm)
        uv, _, _ = _conv_taps(upv, cv_ref[...], HALO, tm)
        act = (ug * jax.nn.sigmoid(ug) * uv).astype(BF16)
        acc_sc[...] += jnp.dot(act, wd_ref[...], preferred_element_type=F32)

        @pl.when(j == nc - 1)
        def _():
            y_ffn = acc_sc[...]
            x3 = x_ref[...] + mod_ref[5:6, :] * y_ffn
            r3 = lax.rsqrt(jnp.mean(x3 * x3, axis=-1, keepdims=True) + EPS)
            xn = x3 * r3
            gf = gf_ref[...]
            diff = xn * gf - tg_ref[...]
            dy = diff * (1.0 / d)
            dxn = dy * gf
            dx3 = r3 * (dxn - xn * jnp.mean(dxn * xn, axis=-1, keepdims=True))
            dx3_ref[...] = dx3
            part_ref[0] = jnp.zeros((8, d), F32)
            part_ref[0, 0:1, :] = jnp.sum(dy * xn, axis=0, keepdims=True)
            part_ref[0, 1:2, :] = jnp.sum(dx3 * y_ffn, axis=0, keepdims=True)
            part_ref[0, 2:3, :] = jnp.sum(diff * diff, axis=0, keepdims=True) * (0.5 / d)

    row = lambda w: pl.BlockSpec((tm, w), lambda i, j: (i, 0))
    full = lambda a: pl.BlockSpec(a.shape, lambda i, j: (0,) * a.ndim)
    return _pcall(
        body, name="ffn_fwd", grid=(nt, nc),
        out_shape=[jax.ShapeDtypeStruct((t, d), F32), jax.ShapeDtypeStruct((t, d), BF16),
                   jax.ShapeDtypeStruct((nt, 8, d), F32)],
        in_specs=[row(d), pl.BlockSpec((HALO, d), lambda i, j: (jnp.maximum(i * hb - 1, 0), 0)), row(d),
                  full(mod8), full(g_mlp), full(g_final),
                  pl.BlockSpec((d, cf), lambda i, j: (0, j)), pl.BlockSpec((d, cf), lambda i, j: (0, j)),
                  pl.BlockSpec((8, cf), lambda i, j: (0, j)), pl.BlockSpec((8, cf), lambda i, j: (0, j)),
                  pl.BlockSpec((cf, d), lambda i, j: (j, 0))],
        out_specs=[row(d), row(d), pl.BlockSpec((1, 8, d), lambda i, j: (i, 0, 0))],
        scratch_shapes=[pltpu.VMEM((tm + HALO, d), BF16), pltpu.VMEM((tm, d), F32)],
        compiler_params=_params(("arbitrary", "arbitrary")),
    )(x2, x2, target, mod8, g_mlp, g_final, wg, wv, cwg, cwv, wd)


def _ffn_bwd(x2, dx3, mod8, g_mlp, wg, wv, cwg, cwv, wd, tm, cf):
    t, d = x2.shape
    dfp = wg.shape[1]
    nt, nc = t // tm, dfp // cf
    hb = tm // HALO
    nhb = t // HALO
    ext = tm + 2 * HALO

    def body(x_ref, xp_ref, xn_ref, dx_ref, dxn_ref, mod_ref, g_ref, wg_ref, wv_ref, cg_ref, cv_ref, wd_ref,
             dx2_ref, dug_ref, duv_ref, act_ref, dxg_ref, part_ref, pcg_ref, pcv_ref, h_sc, dg_sc, dh_sc):
        i, j = pl.program_id(0), pl.program_id(1)

        @pl.when(j == 0)
        def _():
            xe = jnp.concatenate([xp_ref[...], x_ref[...], xn_ref[...]], axis=0)
            h_sc[...] = _norm_mod(xe, g_ref[...], mod_ref[4:5, :], mod_ref[3:4, :]).astype(BF16)
            de = (jnp.concatenate([dx_ref[...], dxn_ref[...]], axis=0) * mod_ref[5:6, :]).astype(BF16)
            dg_sc[...] = de
            dxg_ref[...] = de[:tm]
            dh_sc[...] = jnp.zeros_like(dh_sc)

        rowe = lax.broadcasted_iota(jnp.int32, (ext, 1), 0)
        keep_up = (rowe >= HALO) | (i > 0)
        rowu = lax.broadcasted_iota(jnp.int32, (tm + HALO, 1), 0)
        keep_du = (rowu < tm) | (i < nt - 1)
        hv = h_sc[...]
        upg = jnp.where(keep_up, jnp.dot(hv, wg_ref[...], preferred_element_type=F32), 0.0)
        upv = jnp.where(keep_up, jnp.dot(hv, wv_ref[...], preferred_element_type=F32), 0.0)
        cg, cv = cg_ref[...], cv_ref[...]
        ug, g1, g2 = _conv_taps(upg, cg, HALO, tm + HALO)
        uv, v1, v2 = _conv_taps(upv, cv, HALO, tm + HALO)
        dact = lax.dot_general(dg_sc[...], wd_ref[...], NT_DIMS, preferred_element_type=F32)
        sg = jax.nn.sigmoid(ug)
        sil = ug * sg
        act_ref[...] = (sil * uv)[:tm].astype(BF16)
        duv = jnp.where(keep_du, dact * sil, 0.0)
        dug = jnp.where(keep_du, dact * uv * (sg * (1.0 + ug * (1.0 - sg))), 0.0)

        def back(du, cw, up, s1, s2, pc_ref):
            n = tm + HALO
            dup = (cw[2:3, :] * du + cw[1:2, :] * pltpu.roll(du, n - 1, 0) + cw[0:1, :] * pltpu.roll(du, n - 2, 0))[:tm]
            dut = du[:tm]
            pc_ref[0] = jnp.zeros((8, cf), F32)
            pc_ref[0, 0:1, :] = jnp.sum(dut * s2[HALO:HALO + tm], axis=0, keepdims=True)
            pc_ref[0, 1:2, :] = jnp.sum(dut * s1[HALO:HALO + tm], axis=0, keepdims=True)
            pc_ref[0, 2:3, :] = jnp.sum(dut * up[HALO:HALO + tm], axis=0, keepdims=True)
            pc_ref[0, 3:4, :] = jnp.sum(dut, axis=0, keepdims=True)
            return dup.astype(BF16)

        dupg = back(dug, cg, upg, g1, g2, pcg_ref)
        dupv = back(duv, cv, upv, v1, v2, pcv_ref)
        dug_ref[...] = dupg
        duv_ref[...] = dupv
        dh_sc[...] += (lax.dot_general(dupg, wg_ref[...], NT_DIMS, preferred_element_type=F32)
                       + lax.dot_general(dupv, wv_ref[...], NT_DIMS, preferred_element_type=F32))

        @pl.when(j == nc - 1)
        def _():
            dx, dshift, dscale, dgn = _norm_mod_bwd(x_ref[...], dh_sc[...], g_ref[...], mod_ref[4:5, :])
            dx2_ref[...] = dx_ref[...] + dx
            part_ref[0] = jnp.zeros((8, d), F32)
            part_ref[0, 0:1, :] = dshift
            part_ref[0, 1:2, :] = dscale
            part_ref[0, 2:3, :] = dgn

    row = lambda w: pl.BlockSpec((tm, w), lambda i, j: (i, 0))
    prev = pl.BlockSpec((HALO, d), lambda i, j: (jnp.maximum(i * hb - 1, 0), 0))
    nxt = pl.BlockSpec((HALO, d), lambda i, j: (jnp.minimum((i + 1) * hb, nhb - 1), 0))
    full = lambda a: pl.BlockSpec(a.shape, lambda i, j: (0,) * a.ndim)
    chunk = pl.BlockSpec((tm, cf), lambda i, j: (i, j))
    pchunk = pl.BlockSpec((1, 8, cf), lambda i, j: (i, 0, j))
    return _pcall(
        body, name="ffn_bwd", grid=(nt, nc),
        out_shape=[jax.ShapeDtypeStruct((t, d), F32), jax.ShapeDtypeStruct((t, dfp), BF16),
                   jax.ShapeDtypeStruct((t, dfp), BF16), jax.ShapeDtypeStruct((t, dfp), BF16),
                   jax.ShapeDtypeStruct((t, d), BF16), jax.ShapeDtypeStruct((nt, 8, d), F32),
                   jax.ShapeDtypeStruct((nt, 8, dfp), F32), jax.ShapeDtypeStruct((nt, 8, dfp), F32)],
        in_specs=[row(d), prev, nxt, row(d), nxt, full(mod8), full(g_mlp),
                  pl.BlockSpec((d, cf), lambda i, j: (0, j)), pl.BlockSpec((d, cf), lambda i, j: (0, j)),
                  pl.BlockSpec((8, cf), lambda i, j: (0, j)), pl.BlockSpec((8, cf), lambda i, j: (0, j)),
                  pl.BlockSpec((cf, d), lambda i, j: (j, 0))],
        out_specs=[row(d), chunk, chunk, chunk, row(d), pl.BlockSpec((1, 8, d), lambda i, j: (i, 0, 0)), pchunk, pchunk],
        scratch_shapes=[pltpu.VMEM((ext, d), BF16), pltpu.VMEM((tm + HALO, d), BF16), pltpu.VMEM((tm, d), F32)],
        compiler_params=_params(("arbitrary", "arbitrary")),
    )(x2, x2, x2, dx3, dx3, mod8, g_mlp, wg, wv, cwg, cwv, wd)


def _head_masks():
    lane = lax.broadcasted_iota(jnp.int32, (1, LANES), 1)
    in_a = lane < HEAD_DIM
    return in_a, jnp.logical_not(in_a)


BLK = 2 * LANES


def _stack_heads(qkv, dg):
    t = qkv.shape[0]
    p = dg // LANES
    rows = _tile(t, (512, 256, 128))
    sub = rows // BLK

    def body(kf_ref, vf_ref, ks_ref, vs_ref, okf, ovf, oks, ovs):
        in_a, in_b = _head_masks()
        for src, dst in ((kf_ref, okf), (vf_ref, ovf), (ks_ref, oks), (vs_ref, ovs)):
            v = src[...]
            zero = jnp.zeros_like(v)
            va, vb = jnp.where(in_a, v, zero), jnp.where(in_b, v, zero)
            for s in range(sub):
                dst[0, s, :BLK, :] = va[s * BLK:(s + 1) * BLK]
                dst[0, s, BLK:, :] = vb[s * BLK:(s + 1) * BLK]

    col = lambda base: pl.BlockSpec((rows, LANES), lambda h, j: (j, base * p + h))
    out = pl.BlockSpec((1, sub, 2 * BLK, LANES), lambda h, j: (h, j, 0, 0))
    shape = jax.ShapeDtypeStruct((p, t // BLK, 2 * BLK, LANES), BF16)
    return _pcall(
        body, name="stack_heads", grid=(p, t // rows),
        out_shape=[shape] * 4, in_specs=[col(1), col(2), col(4), col(5)], out_specs=[out] * 4,
        compiler_params=_params(("arbitrary", "arbitrary")),
    )(qkv, qkv, qkv, qkv)


def _tile_masks():
    rowi = lax.broadcasted_iota(jnp.int32, (BLK, BLK), 0)
    coli = lax.broadcasted_iota(jnp.int32, (BLK, BLK), 1)
    return coli <= rowi, coli < rowi


def _pair_triangle(suffix):
    r = lax.broadcasted_iota(jnp.int32, (BLK, BLK), 0)
    c = lax.broadcasted_iota(jnp.int32, (BLK, BLK), 1)
    return ((r >= c) if suffix else (r <= c)).astype(BF16)


def _pair_cumsum(x2, tri, passes):
    return jnp.concatenate([_split_dot(x2[:, :BLK], tri, passes), _split_dot(x2[:, BLK:], tri, passes)], axis=1)


def _pair_specs(t, dg, base):
    p = dg // LANES
    q = pl.BlockSpec((BLK, LANES), lambda h, i: (i, base * p + h))
    kv = pl.BlockSpec((1, t // BLK, 2 * BLK, LANES), lambda h, i: (h, 0, 0, 0))
    return q, kv


def _fox_fwd(qkv, kst, vst, fcol, frow2, dg):
    t = qkv.shape[0]
    p, nq = dg // LANES, t // BLK
    nh = 2 * p

    def body(q_ref, k_ref, v_ref, ft_ref, fs_ref, o_ref, lse_ref):
        i = pl.program_id(1)
        in_a, _ = _head_masks()
        causal, _ = _tile_masks()
        q2 = q_ref[...]
        ft = (ft_ref[0], ft_ref[1])

        def tile(j, carry, masked):
            m, l, acc = carry
            kb, vb = k_ref[0, j], v_ref[0, j]
            s2 = lax.dot_general(q2, kb, NT_DIMS, preferred_element_type=F32)
            fs = fs_ref[0, j]
            m_new, l_new, alpha, pr = [], [], [], []
            for a in range(2):
                sl = slice(a * BLK, (a + 1) * BLK)
                s = (s2[:, sl] + ft[a]) - fs[:, sl]
                if masked:
                    s = jnp.where(causal, s, NEG)
                mn = jnp.maximum(m[a], jnp.max(s, axis=1, keepdims=True))
                pa = jnp.exp(s - mn)
                al = jnp.exp(m[a] - mn)
                m_new.append(mn)
                alpha.append(al)
                l_new.append(al * l[a] + jnp.sum(pa, axis=1, keepdims=True))
                pr.append(pa.astype(BF16))
            acc = jnp.where(in_a, alpha[0], alpha[1]) * acc + jnp.dot(
                jnp.concatenate(pr, axis=1), vb, preferred_element_type=F32)
            return tuple(m_new), tuple(l_new), acc

        neg, zero = jnp.full((BLK, 1), NEG, F32), jnp.zeros((BLK, 1), F32)
        carry = lax.fori_loop(0, i, functools.partial(tile, masked=False), ((neg, neg), (zero, zero), jnp.zeros((BLK, LANES), F32)))
        m, l, acc = tile(i, carry, True)
        o_ref[...] = acc / jnp.where(in_a, l[0], l[1])
        lse_ref[0] = m[0] + jnp.log(l[0])
        lse_ref[1] = m[1] + jnp.log(l[1])

    qs, kv = _pair_specs(t, dg, 0)
    col = pl.BlockSpec((2, BLK, 1), lambda h, i: (h, i, 0))
    return _pcall(
        body, name="fox_fwd", grid=(p, nq),
        out_shape=[jax.ShapeDtypeStruct((t, dg), F32), jax.ShapeDtypeStruct((nh, t, 1), F32)],
        in_specs=[qs, kv, kv, col, pl.BlockSpec((1, nq, 1, 2 * BLK), lambda h, i: (h, 0, 0, 0))],
        out_specs=[pl.BlockSpec((BLK, LANES), lambda h, i: (i, h)), col],
        compiler_params=_params(("arbitrary", "arbitrary")),
    )(qkv, kst, vst, fcol, frow2)


def _fold_heads(stacked, in_a):
    return jnp.where(in_a, stacked[:BLK], stacked[BLK:])


def _fox_bwd(qkv, kst, vst, do, fcol, frow2, lse, delta, dg):
    t = qkv.shape[0]
    p, nq = dg // LANES, t // BLK
    nh = 2 * p

    def body(q_ref, k_ref, v_ref, do_ref, ft_ref, fs_ref, lse_ref, dl_ref, dq_ref, dk_ref, dv_ref, dfs_ref, dft_ref):
        i = pl.program_id(1)

        @pl.when(i == 0)
        def _():
            dk_ref[...] = jnp.zeros_like(dk_ref)
            dv_ref[...] = jnp.zeros_like(dv_ref)
            dfs_ref[...] = jnp.zeros_like(dfs_ref)

        in_a, _ = _head_masks()
        causal, _ = _tile_masks()
        q2, do2 = q_ref[...], do_ref[...]
        ft, lse_h, dl = (ft_ref[0], ft_ref[1]), (lse_ref[0], lse_ref[1]), (dl_ref[0], dl_ref[1])

        def tile(j, carry, masked):
            dq, dft = carry
            kb, vb = k_ref[0, j], v_ref[0, j]
            s2 = lax.dot_general(q2, kb, NT_DIMS, preferred_element_type=F32)
            dp2 = lax.dot_general(do2, vb, NT_DIMS, preferred_element_type=F32)
            fs = fs_ref[0, j]
            pr, ds, dft_new = [], [], []
            for a in range(2):
                sl = slice(a * BLK, (a + 1) * BLK)
                s = (s2[:, sl] + ft[a]) - fs[:, sl]
                if masked:
                    s = jnp.where(causal, s, NEG)
                pa = jnp.exp(s - lse_h[a])
                dsa = pa * (dp2[:, sl] - dl[a])
                pr.append(pa.astype(BF16))
                ds.append(dsa)
                dft_new.append(dft[a] + jnp.sum(dsa, axis=1, keepdims=True))
            ds2 = jnp.concatenate(ds, axis=1)
            dsb = ds2.astype(BF16)
            off = pl.multiple_of(j * BLK, BLK)
            dk_ref[pl.ds(off, BLK), :] += _fold_heads(lax.dot_general(dsb, q2, TN_DIMS, preferred_element_type=F32), in_a)
            dv_ref[pl.ds(off, BLK), :] += _fold_heads(
                lax.dot_general(jnp.concatenate(pr, axis=1), do2, TN_DIMS, preferred_element_type=F32), in_a)
            dfs_ref[0, j] += -jnp.sum(ds2, axis=0, keepdims=True)
            return dq + jnp.dot(dsb, kb, preferred_element_type=F32), tuple(dft_new)

        zero = jnp.zeros((BLK, 1), F32)
        carry = lax.fori_loop(0, i, functools.partial(tile, masked=False), (jnp.zeros((BLK, LANES), F32), (zero, zero)))
        dq, dft = tile(i, carry, True)
        dq_ref[...] = dq
        dft_ref[0] = dft[0]
        dft_ref[1] = dft[1]

    qs, kv = _pair_specs(t, dg, 0)
    col = pl.BlockSpec((2, BLK, 1), lambda h, i: (h, i, 0))
    rowspec = pl.BlockSpec((1, nq, 1, 2 * BLK), lambda h, i: (h, 0, 0, 0))
    blk = pl.BlockSpec((BLK, LANES), lambda h, i: (i, h))
    acc = pl.BlockSpec((t, LANES), lambda h, i: (0, h))
    return _pcall(
        body, name="fox_bwd", grid=(p, nq),
        out_shape=[jax.ShapeDtypeStruct((t, dg), F32)] * 3 + [jax.ShapeDtypeStruct((p, nq, 1, 2 * BLK), F32),
                                                              jax.ShapeDtypeStruct((nh, t, 1), F32)],
        in_specs=[qs, kv, kv, blk, col, rowspec, col, col],
        out_specs=[blk, acc, acc, rowspec, col],
        compiler_params=_params(("arbitrary", "arbitrary")),
    )(qkv, kst, vst, do, fcol, frow2, lse, delta)


def _softplus_parts(z):
    e = jnp.exp(-jnp.abs(z))
    return jnp.maximum(z, 0.0) + jnp.log1p(e), e


def _sb_fwd(qkv, kst, vst, dg):
    t = qkv.shape[0]
    p, nq = dg // LANES, t // BLK
    nh = 2 * p

    def body(q_ref, k_ref, v_ref, o_ref, rt_ref):
        i = pl.program_id(1)
        _, strict = _tile_masks()
        strict2 = jnp.concatenate([strict, strict], axis=1)
        suffix = _pair_triangle(True)
        q2 = q_ref[...]

        def tile(j, carry, masked):
            rest, acc = carry
            kb, vb = k_ref[0, j], v_ref[0, j]
            z = lax.dot_general(q2, kb, NT_DIMS, preferred_element_type=F32)
            sp, _ = _softplus_parts(z)
            if masked:
                sp = jnp.where(strict2, sp, 0.0)
            cs = _pair_cumsum(sp, suffix, 2)
            w, rest_new = [], []
            for a in range(2):
                sl = slice(a * BLK, (a + 1) * BLK)
                wa = jnp.exp(z[:, sl] - cs[:, sl] - rest[a])
                if masked:
                    wa = jnp.where(strict, wa, 0.0)
                w.append(wa.astype(BF16))
                rest_new.append(rest[a] + cs[:, a * BLK:a * BLK + 1])
            acc = acc + jnp.dot(jnp.concatenate(w, axis=1), vb, preferred_element_type=F32)
            return tuple(rest_new), acc

        zero = jnp.zeros((BLK, 1), F32)
        carry = tile(i, ((zero, zero), jnp.zeros((BLK, LANES), F32)), True)
        rest, acc = lax.fori_loop(0, i, lambda jj, c: tile(i - 1 - jj, c, False), carry)
        o_ref[...] = acc
        rt_ref[0] = rest[0]
        rt_ref[1] = rest[1]

    qs, kv = _pair_specs(t, dg, 3)
    col = pl.BlockSpec((2, BLK, 1), lambda h, i: (h, i, 0))
    return _pcall(
        body, name="sb_fwd", grid=(p, nq),
        out_shape=[jax.ShapeDtypeStruct((t, dg), F32), jax.ShapeDtypeStruct((nh, t, 1), F32)],
        in_specs=[qs, kv, kv],
        out_specs=[pl.BlockSpec((BLK, LANES), lambda h, i: (i, h)), col],
        compiler_params=_params(("arbitrary", "arbitrary")),
    )(qkv, kst, vst)


def _sb_bwd(qkv, kst, vst, do, rtot, dg):
    t = qkv.shape[0]
    p, nq = dg // LANES, t // BLK

    def body(q_ref, k_ref, v_ref, do_ref, rt_ref, dq_ref, dk_ref, dv_ref):
        i = pl.program_id(1)

        @pl.when(i == 0)
        def _():
            dk_ref[...] = jnp.zeros_like(dk_ref)
            dv_ref[...] = jnp.zeros_like(dv_ref)

        in_a, _ = _head_masks()
        _, strict = _tile_masks()
        strict2 = jnp.concatenate([strict, strict], axis=1)
        prefix = _pair_triangle(False)
        q2, do2 = q_ref[...], do_ref[...]
        rt = (rt_ref[0], rt_ref[1])

        def tile(j, carry, masked):
            before, gbefore, dq = carry
            kb, vb = k_ref[0, j], v_ref[0, j]
            z = lax.dot_general(q2, kb, NT_DIMS, preferred_element_type=F32)
            da = lax.dot_general(do2, vb, NT_DIMS, preferred_element_type=F32)
            sp, e = _softplus_parts(z)
            sig = jnp.where(z >= 0, 1.0, e) / (1.0 + e)
            if masked:
                sp = jnp.where(strict2, sp, 0.0)
            pre = _pair_cumsum(sp, prefix, 2)
            w = []
            for a in range(2):
                sl = slice(a * BLK, (a + 1) * BLK)
                wa = jnp.exp(z[:, sl] + (before[a] - rt[a]) + pre[:, sl] - sp[:, sl])
                if masked:
                    wa = jnp.where(strict, wa, 0.0)
                w.append(wa)
            w2 = jnp.concatenate(w, axis=1)
            g = w2 * da
            preg = _pair_cumsum(g, prefix, 1)
            dz = []
            for a in range(2):
                sl = slice(a * BLK, (a + 1) * BLK)
                dza = g[:, sl] * (1.0 - sig[:, sl]) - sig[:, sl] * (gbefore[a] + preg[:, sl] - g[:, sl])
                if masked:
                    dza = jnp.where(strict, dza, 0.0)
                dz.append(dza.astype(BF16))
            dzb = jnp.concatenate(dz, axis=1)
            off = pl.multiple_of(j * BLK, BLK)
            dk_ref[pl.ds(off, BLK), :] += _fold_heads(lax.dot_general(dzb, q2, TN_DIMS, preferred_element_type=F32), in_a)
            dv_ref[pl.ds(off, BLK), :] += _fold_heads(
                lax.dot_general(w2.astype(BF16), do2, TN_DIMS, preferred_element_type=F32), in_a)
            last = lambda x, a: x[:, (a + 1) * BLK - 1:(a + 1) * BLK]
            return (tuple(before[a] + last(pre, a) for a in range(2)),
                    tuple(gbefore[a] + last(preg, a) for a in range(2)),
                    dq + jnp.dot(dzb, kb, preferred_element_type=F32))

        zero = jnp.zeros((BLK, 1), F32)
        carry = lax.fori_loop(0, i, functools.partial(tile, masked=False), ((zero, zero), (zero, zero), jnp.zeros((BLK, LANES), F32)))
        dq_ref[...] = tile(i, carry, True)[2]

    qs, kv = _pair_specs(t, dg, 3)
    col = pl.BlockSpec((2, BLK, 1), lambda h, i: (h, i, 0))
    blk = pl.BlockSpec((BLK, LANES), lambda h, i: (i, h))
    acc = pl.BlockSpec((t, LANES), lambda h, i: (0, h))
    return _pcall(
        body, name="sb_bwd", grid=(p, nq),
        out_shape=[jax.ShapeDtypeStruct((t, dg), F32)] * 3,
        in_specs=[qs, kv, kv, blk, col],
        out_specs=[blk, acc, acc],
        compiler_params=_params(("arbitrary", "arbitrary")),
    )(qkv, kst, vst, do, rtot)


def _tri_constants(nh, t):
    nb = t // LANES
    r = nh * nb
    li = np.arange(LANES)
    tri_in = (li[:, None] <= li[None, :])
    ri = np.arange(r)
    same = (ri[:, None] // nb) == (ri[None, :] // nb)
    blk = same & (ri[None, :] < ri[:, None])
    blk_rev = same & (ri[None, :] > ri[:, None])
    head_rows = (np.arange(max(8, nh))[:, None] == (ri[None, :] // nb))
    as_bf16 = lambda a: jnp.asarray(a.astype(np.float32), BF16)
    return as_bf16(tri_in), as_bf16(blk), as_bf16(tri_in.T), as_bf16(blk_rev), as_bf16(head_rows)


def kernel(x, c, w_ada, b_ada, g_attn, w_in, b_fgate, g_out_fox, g_out_sb, w_out, g_mlp, w_up, conv_w, conv_b, w_down, g_final, loss_target, m_w_ada, m_b_ada, m_g_attn, m_w_in, m_b_fgate, m_g_out_fox, m_g_out_sb, m_w_out, m_g_mlp, m_w_up, m_conv_w, m_conv_b, m_w_down, m_g_final, v_w_ada, v_b_ada, v_g_attn, v_w_in, v_b_fgate, v_g_out_fox, v_g_out_sb, v_w_out, v_g_mlp, v_w_up, v_conv_w, v_conv_b, v_w_down, v_g_final):
    t, d = x.shape[1], x.shape[2]
    dg = d // 2
    nh = dg // HEAD_DIM
    n_in = 6 * dg + nh
    dff = w_down.shape[1] * 4
    dfp = -(-dff // 256) * 256
    cf = 256
    tm = _tile(t, (512, 256, 128))
    nq = t // BLK
    xi, yi, ci = lax.axis_index("x"), lax.axis_index("y"), lax.axis_index("c")
    shard = 2 * xi + yi
    me = 4 * xi + 2 * yi + ci

    x2d, tg2d = x[0], loss_target[0]

    c_all = _all_gather8(jnp.pad(c, ((0, 7), (0, 0)))).reshape(8, 8, d)[:, 0, :]
    ada_cols = w_ada.shape[2]
    b_shard = lax.dynamic_slice(b_ada, (0, shard * ada_cols), (1, ada_cols))
    sc_all, mod_shard = _ada_fwd(c_all, w_ada[0], b_shard)
    mod_all = _all_gather8(mod_shard).reshape(4, 2, 8, ada_cols)
    mod_me = lax.dynamic_index_in_dim(mod_all[:, 0], me, axis=1, keepdims=False)
    mod8 = jnp.pad(mod_me.reshape(6, d), ((0, 2), (0, 0)))

    g_in, g_out, g_up, g_down, g_cw = _gather_xy(
        [w_in[0].astype(BF16), w_out[0].astype(BF16), w_up[0].astype(BF16), w_down[0].astype(BF16), conv_w[0]])
    w_in_full = jnp.transpose(g_in, (1, 0, 2)).reshape(d, n_in)
    w_qkv = w_in_full[:, :6 * dg]
    w_f = jnp.pad(w_in_full[:, 6 * dg:], ((0, 0), (0, LANES - nh)))
    w_out_full = g_out.reshape(2 * dg, d)
    w_up_full = jnp.transpose(g_up, (1, 0, 2)).reshape(d, 2 * dff)
    padc = ((0, 0), (0, dfp - dff))
    wg, wv = jnp.pad(w_up_full[:, :dff], padc), jnp.pad(w_up_full[:, dff:], padc)
    wd = jnp.pad(g_down.reshape(dff, d), ((0, dfp - dff), (0, 0)))
    cw_full = jnp.transpose(g_cw, (1, 0, 2)).reshape(3, 2 * dff)
    cw4 = jnp.concatenate([cw_full, conv_b], axis=0)
    cwg = jnp.pad(cw4[:, :dff], ((0, 4), (0, dfp - dff)))
    cwv = jnp.pad(cw4[:, dff:], ((0, 4), (0, dfp - dff)))

    qkv, fl, h1 = _in_proj_fwd(x2d, mod8, g_attn, w_qkv, w_f, tm)
    tri_in, tri_blk, tri_in_rev, tri_blk_rev, head_rows = _tri_constants(nh, t)
    fl2d = fl[:, :nh].T.reshape(nh * t // LANES, LANES)
    b_rows = jnp.repeat(b_fgate[0], t // LANES)[:, None]
    f2d = _fgate_fwd(fl2d, b_rows, tri_in, tri_blk)
    fcol = f2d.reshape(nh, t, 1)
    pairs = nh // 2
    frow2 = jnp.transpose(f2d.reshape(pairs, 2, nq, BLK), (0, 2, 1, 3)).reshape(pairs, nq, 1, 2 * BLK)
    k_fox, v_fox, k_sb, v_sb = _stack_heads(qkv, dg)
    o_fox, lse = _fox_fwd(qkv, k_fox, v_fox, fcol, frow2, dg)
    o_sb, rtot = _sb_fwd(qkv, k_sb, v_sb, dg)
    li = np.arange(dg)
    bd = jnp.asarray((li[:, None] // HEAD_DIM == li[None, :] // HEAD_DIM).astype(np.float32), BF16)
    hsel = jnp.asarray((li[:, None] // HEAD_DIM == np.arange(LANES)[None, :]).astype(np.float32), BF16)
    x2, mix = _attn_out_fwd(x2d, o_fox, o_sb, g_out_fox, g_out_sb, w_out_full, mod8, bd, tm)
    g_final2 = g_final[None, :]
    dx3, h2, part_f = _ffn_fwd(x2, tg2d, mod8, g_mlp, g_final2, wg, wv, cwg, cwv, wd, tm, cf)

    dx2, dupg, dupv, act, dxg3, part_b, pcg, pcv = _ffn_bwd(x2, dx3, mod8, g_mlp, wg, wv, cwg, cwv, wd, tm, cf)
    do_fox, do_sb, delta, dxg2, part_o = _attn_out_bwd(dx2, mix, o_fox, o_sb, g_out_fox, g_out_sb, w_out_full, mod8, bd, hsel, tm)
    dcol = delta[:, :nh].T.reshape(nh, t, 1)
    dq_f, dk_f, dv_f, dfs2, dft = _fox_bwd(qkv, k_fox, v_fox, do_fox, fcol, frow2, lse, dcol, dg)
    dq_s, dk_s, dv_s = _sb_bwd(qkv, k_sb, v_sb, do_sb, rtot, dg)
    f2d_shape = (nh * t // LANES, LANES)
    dfs = jnp.transpose(dfs2.reshape(pairs, nq, 2, BLK), (0, 2, 1, 3))
    dfl2d, gb8 = _fgate_bwd(fl2d, b_rows, dft.reshape(f2d_shape), dfs.reshape(f2d_shape), tri_in_rev, tri_blk_rev, head_rows)
    dfl = jnp.pad(dfl2d.reshape(nh, t).T, ((0, 0), (0, LANES - nh)))
    grad_x, dproj, dflb, part_i = _in_proj_bwd([dq_f, dk_f, dv_f, dq_s, dk_s, dv_s], dfl, w_qkv, w_f, x2d, dx2, mod8, g_attn, tm)

    gw_qkv = _matmul_tn(h1, dproj, "grad_w_qkv")
    gw_f = _matmul_tn(h1, dflb, "grad_w_f")
    gw_in = jnp.concatenate([gw_qkv, gw_f[:, :nh]], axis=1)
    gw_out = _matmul_tn(mix, dxg2, "grad_w_out")
    gw_upg = _matmul_tn(h2, dupg, "grad_w_up_gate")
    gw_upv = _matmul_tn(h2, dupv, "grad_w_up_val")
    gw_up = jnp.concatenate([gw_upg[:, :dff], gw_upv[:, :dff]], axis=1)
    gw_down = _matmul_tn(act, dxg3, "grad_w_down")[:dff]

    sf = _sum_leading(part_f, "sum_part_ffn_fwd")
    sb_ = _sum_leading(part_b, "sum_part_ffn_bwd")
    so = _sum_leading(part_o, "sum_part_attn_out")
    si = _sum_leading(part_i, "sum_part_in_proj")
    scg = _sum_leading(pcg, "sum_part_conv_gate")
    scv = _sum_leading(pcv, "sum_part_conv_val")
    gb_f = gb8[:nh, 0]
    dmod = jnp.concatenate([si[0], si[1], so[0], sb_[0], sb_[1], sf[1]])
    g_conv_w = jnp.concatenate([scg[0:3, :dff], scv[0:3, :dff]], axis=1).reshape(-1)
    g_conv_b = jnp.concatenate([scg[3, :dff], scv[3, :dff]])
    loss_part = jnp.sum(sf[2])
    fields = [dmod, si[2], gb_f, so[1, :dg], so[1, dg:], sb_[2], g_conv_b, sf[0], g_conv_w, loss_part[None]]
    sizes = [int(f.shape[0]) for f in fields]
    n_pack = sum(sizes)
    lanes_pack = -(-n_pack // (8 * LANES)) * LANES
    pack = jnp.pad(jnp.concatenate(fields), (0, 8 * lanes_pack - n_pack)).reshape(8, lanes_pack)
    gathered = _all_gather8(pack)
    tot = _sum_leading(gathered.reshape(8, 8, lanes_pack), "sum_pack").reshape(-1)
    offs = np.concatenate([[0], np.cumsum(sizes)])
    take = lambda k: tot[int(offs[k]):int(offs[k + 1])]
    g_b_ada, g_g_attn, g_b_fgate, g_g_fox, g_g_sb, g_g_mlp, g_cb, g_g_final, g_cw_full, loss_v = [take(k) for k in range(10)]
    loss = loss_v[0]
    dmod_all = gathered.reshape(8, 8 * lanes_pack)[:, :6 * d]
    dmod_cols = lax.dynamic_slice(dmod_all, (0, shard * ada_cols), (8, ada_cols))
    g_w_ada = _ada_bwd(sc_all.T, dmod_cols)

    def col_pieces(g):
        r, cc = g.shape
        return jnp.transpose(g.reshape(2, r // 2, 4, cc // 4), (2, 0, 1, 3)).reshape(8, r // 2, cc // 4)

    def row_pieces(g):
        r, cc = g.shape
        return g.reshape(8, r // 8, cc)

    recv = _scatter8([col_pieces(gw_in), row_pieces(gw_out), col_pieces(gw_up), row_pieces(gw_down)])
    halves = [_sum_leading(rv, nm) for rv, nm in zip(recv, ("sum_w_in", "sum_w_out", "sum_w_up", "sum_w_down"))]
    swapped = _swap_halves(halves)
    g_w_in, g_w_out, g_w_up, g_w_down = [s.reshape(2 * s.shape[1], s.shape[2]) for s in swapped]
    g_conv_w_shard = lax.dynamic_slice(g_cw_full.reshape(3, 2 * dff), (0, shard * (dff // 2)), (3, dff // 2))

    grads, deltas, new_m, new_v = {}, {}, {}, {}

    def step(name, w, g, m, v):
        shape = w.shape
        as2d = lambda a: a.reshape(-1, shape[-1])
        dl, nm, nv = _adamw(as2d(w), as2d(g), as2d(m), as2d(v), "adamw_" + name)
        grads[name], deltas[name], new_m[name], new_v[name] = g.reshape(shape), dl.reshape(shape), nm.reshape(shape), nv.reshape(shape)

    step("w_ada", w_ada, g_w_ada, m_w_ada, v_w_ada)
    step("w_in", w_in, g_w_in, m_w_in, v_w_in)
    step("w_out", w_out, g_w_out, m_w_out, v_w_out)
    step("w_up", w_up, g_w_up, m_w_up, v_w_up)
    step("conv_w", conv_w, g_conv_w_shard, m_conv_w, v_conv_w)
    step("w_down", w_down, g_w_down, m_w_down, v_w_down)

    small = [("b_ada", b_ada, g_b_ada, m_b_ada, v_b_ada), ("g_attn", g_attn, g_g_attn, m_g_attn, v_g_attn),
             ("b_fgate", b_fgate, g_b_fgate, m_b_fgate, v_b_fgate), ("g_out_fox", g_out_fox, g_g_fox, m_g_out_fox, v_g_out_fox),
             ("g_out_sb", g_out_sb, g_g_sb, m_g_out_sb, v_g_out_sb), ("g_mlp", g_mlp, g_g_mlp, m_g_mlp, v_g_mlp),
             ("conv_b", conv_b, g_cb, m_conv_b, v_conv_b), ("g_final", g_final, g_g_final, m_g_final, v_g_final)]
    ssz = [int(np.prod(s[1].shape)) for s in small]
    n_small = sum(ssz)
    lanes_small = -(-n_small // (8 * LANES)) * LANES
    packs = [jnp.pad(jnp.concatenate([s[k].reshape(-1) for s in small]), (0, 8 * lanes_small - n_small)).reshape(8, lanes_small)
             for k in (1, 2, 3, 4)]
    dl_s, nm_s, nv_s = _adamw(*packs, "adamw_small")
    so_ = np.concatenate([[0], np.cumsum(ssz)])
    for k, s in enumerate(small):
        cut = lambda a: a.reshape(-1)[int(so_[k]):int(so_[k + 1])].reshape(s[1].shape)
        grads[s[0]], deltas[s[0]], new_m[s[0]], new_v[s[0]] = s[2].reshape(s[1].shape), cut(dl_s), cut(nm_s), cut(nv_s)

    order = ["w_ada", "b_ada", "g_attn", "w_in", "b_fgate", "g_out_fox", "g_out_sb", "w_out", "g_mlp", "w_up",
             "conv_w", "conv_b", "w_down", "g_final"]
    return (loss, grad_x[None], *[grads[n] for n in order], *[deltas[n] for n in order],
            *[new_m[n] for n in order], *[new_v[n] for n in order])
```

```python
import functools

import numpy as np
import jax
import jax.numpy as jnp
from jax import lax
from jax.experimental import pallas as pl
from jax.experimental.pallas import tpu as pltpu

F32 = jnp.float32
BF16 = jnp.bfloat16
MESH = pl.DeviceIdType.MESH

HEAD_DIM = 64
LANES = 128
EPS = 1e-6
NEG = -1e30
ADAM_LR, ADAM_B1, ADAM_B2, ADAM_EPS, ADAM_WD, ADAM_STEP = 0.001, 0.9, 0.999, 1e-08, 0.01, 10
V7X_VMEM_BYTES = 64 * 1024 * 1024
VMEM_LIMIT = V7X_VMEM_BYTES - 12 * 1024 * 1024
NT_DIMS = (((1,), (1,)), ((), ()))
TN_DIMS = (((0,), (0,)), ((), ()))


def _pcall(body, **kw):
    return pl.pallas_call(body, **kw)


def _params(sem=None, **kw):
    return pltpu.CompilerParams(dimension_semantics=sem, vmem_limit_bytes=VMEM_LIMIT, **kw)


def _split_dot(x, m, passes):
    acc = None
    for _ in range(passes):
        part = x.astype(BF16)
        d = jnp.dot(part, m, preferred_element_type=F32)
        acc = d if acc is None else acc + d
        x = x - part.astype(F32)
    return acc


def _tile(n, candidates):
    for t in candidates:
        if n % t == 0:
            return t
    return n


def _rows_tile(rows, row_bytes, budget=2 * 1024 * 1024):
    best = None
    for t in range(8, rows + 1, 8):
        if rows % t == 0 and t * row_bytes <= budget:
            best = t
    return best if best is not None else rows


def _all_gather8(v):
    m_per, n = v.shape

    def body(x_ref, out_ref, send_sems, recv_sems, local_sem):
        x, y, c = lax.axis_index("x"), lax.axis_index("y"), lax.axis_index("c")
        me, sibling = (x, y, c), (x, y, 1 - c)
        chips = [(1 - x, y), (x, 1 - y), (1 - x, 1 - y)]

        def rows(px, py, pc):
            return out_ref.at[pl.ds((4 * px + 2 * py + pc) * m_per, m_per), :]

        def copy(k, block, to, src=None):
            return pltpu.make_async_remote_copy(
                src_ref=rows(*block) if src is None else src, dst_ref=rows(*block),
                send_sem=send_sems.at[k], recv_sem=recv_sems.at[k], device_id=to, device_id_type=MESH)

        mine = pltpu.make_async_copy(x_ref, rows(*me), local_sem)
        mine.start()
        first = [copy(0, me, sibling, src=x_ref)]
        first += [copy(1 + j, me, (*chip, c), src=x_ref) for j, chip in enumerate(chips)]
        for cp in first:
            cp.start()
        passed = [copy(4 + j, (*chip, c), sibling) for j, chip in enumerate(chips)]
        for j, chip in enumerate(chips):
            copy(1 + j, (*chip, c), me).wait_recv()
            passed[j].start()
        copy(0, sibling, me).wait_recv()
        for j, chip in enumerate(chips):
            copy(4 + j, (*chip, 1 - c), me).wait_recv()
        for cp in first + passed:
            cp.wait_send()
        mine.wait()

    return _pcall(
        body, name="all_gather8",
        out_shape=jax.ShapeDtypeStruct((8 * m_per, n), v.dtype),
        in_specs=[pl.BlockSpec(memory_space=pltpu.VMEM)],
        out_specs=pl.BlockSpec(memory_space=pltpu.VMEM),
        scratch_shapes=[pltpu.SemaphoreType.DMA((7,)), pltpu.SemaphoreType.DMA((7,)), pltpu.SemaphoreType.DMA],
        compiler_params=pltpu.CompilerParams(vmem_limit_bytes=VMEM_LIMIT),
    )(v)


def _gather_xy(shards):
    n = len(shards)

    def body(*refs):
        ins, outs = refs[:n], refs[n:2 * n]
        send_sems, recv_sems, local_sems = refs[2 * n:]
        x, y, c = lax.axis_index("x"), lax.axis_index("y"), lax.axis_index("c")
        chips = [(1 - x, y), (x, 1 - y), (1 - x, 1 - y)]
        mine = 2 * x + y
        local, remote = [], []
        for w in range(n):
            cp = pltpu.make_async_copy(ins[w], outs[w].at[mine], local_sems.at[w])
            cp.start()
            local.append(cp)
            for k, (px, py) in enumerate(chips):
                cp = pltpu.make_async_remote_copy(
                    src_ref=ins[w], dst_ref=outs[w].at[mine], send_sem=send_sems.at[3 * w + k],
                    recv_sem=recv_sems.at[3 * w + k], device_id=(px, py, c), device_id_type=MESH)
                cp.start()
                remote.append(cp)
        for cp in remote:
            cp.wait_recv()
        for cp in remote:
            cp.wait_send()
        for cp in local:
            cp.wait()

    hbm = pl.BlockSpec(memory_space=pltpu.HBM)
    return _pcall(
        body, name="gather_xy",
        out_shape=[jax.ShapeDtypeStruct((4,) + s.shape, s.dtype) for s in shards],
        in_specs=[hbm] * n, out_specs=[hbm] * n,
        scratch_shapes=[pltpu.SemaphoreType.DMA((3 * n,)), pltpu.SemaphoreType.DMA((3 * n,)),
                        pltpu.SemaphoreType.DMA((n,))],
        compiler_params=pltpu.CompilerParams(vmem_limit_bytes=VMEM_LIMIT),
    )(*shards)


def _scatter8(pieces):
    n = len(pieces)

    def body(*refs):
        ins, outs = refs[:n], refs[n:2 * n]
        send_sems, recv_sems, local_sems = refs[2 * n:]
        x, y, c = lax.axis_index("x"), lax.axis_index("y"), lax.axis_index("c")
        me = 4 * x + 2 * y + c
        local, remote = [], []
        for w in range(n):
            cp = pltpu.make_async_copy(ins[w].at[me], outs[w].at[me], local_sems.at[w])
            cp.start()
            local.append(cp)
            for f in range(1, 8):
                px = 1 - x if f & 4 else x
                py = 1 - y if f & 2 else y
                pc = 1 - c if f & 1 else c
                cp = pltpu.make_async_remote_copy(
                    src_ref=ins[w].at[4 * px + 2 * py + pc], dst_ref=outs[w].at[me],
                    send_sem=send_sems.at[7 * w + f - 1], recv_sem=recv_sems.at[7 * w + f - 1],
                    device_id=(px, py, pc), device_id_type=MESH)
                cp.start()
                remote.append(cp)
        for cp in remote:
            cp.wait_recv()
        for cp in remote:
            cp.wait_send()
        for cp in local:
            cp.wait()

    hbm = pl.BlockSpec(memory_space=pltpu.HBM)
    return _pcall(
        body, name="scatter8",
        out_shape=[jax.ShapeDtypeStruct(p.shape, p.dtype) for p in pieces],
        in_specs=[hbm] * n, out_specs=[hbm] * n,
        scratch_shapes=[pltpu.SemaphoreType.DMA((7 * n,)), pltpu.SemaphoreType.DMA((7 * n,)),
                        pltpu.SemaphoreType.DMA((n,))],
        compiler_params=pltpu.CompilerParams(vmem_limit_bytes=VMEM_LIMIT),
    )(*pieces)


def _swap_halves(halves):
    n = len(halves)

    def body(*refs):
        ins, outs = refs[:n], refs[n:2 * n]
        send_sems, recv_sems, local_sems = refs[2 * n:]
        x, y, c = lax.axis_index("x"), lax.axis_index("y"), lax.axis_index("c")
        local, remote = [], []
        for w in range(n):
            cp = pltpu.make_async_copy(ins[w], outs[w].at[c], local_sems.at[w])
            cp.start()
            local.append(cp)
            cp = pltpu.make_async_remote_copy(
                src_ref=ins[w], dst_ref=outs[w].at[c], send_sem=send_sems.at[w], recv_sem=recv_sems.at[w],
                device_id=(x, y, 1 - c), device_id_type=MESH)
            cp.start()
            remote.append(cp)
        for cp in remote:
            cp.wait_recv()
        for cp in remote:
            cp.wait_send()
        for cp in local:
            cp.wait()

    hbm = pl.BlockSpec(memory_space=pltpu.HBM)
    return _pcall(
        body, name="swap_halves",
        out_shape=[jax.ShapeDtypeStruct((2,) + h.shape, h.dtype) for h in halves],
        in_specs=[hbm] * n, out_specs=[hbm] * n,
        scratch_shapes=[pltpu.SemaphoreType.DMA((n,)), pltpu.SemaphoreType.DMA((n,)), pltpu.SemaphoreType.DMA((n,))],
        compiler_params=pltpu.CompilerParams(vmem_limit_bytes=VMEM_LIMIT),
    )(*halves)


def _sum_leading(a, name):
    n, r, c = a.shape
    tr = _rows_tile(r, n * c * 4, budget=6 * 1024 * 1024)
    if a.dtype == BF16 and tr % 16:
        tr = r

    def body(a_ref, o_ref):
        acc = a_ref[0].astype(F32)
        for k in range(1, n):
            acc = acc + a_ref[k].astype(F32)
        o_ref[...] = acc

    return _pcall(
        body, name=name, grid=(r // tr,),
        out_shape=jax.ShapeDtypeStruct((r, c), F32),
        in_specs=[pl.BlockSpec((n, tr, c), lambda i: (0, i, 0))],
        out_specs=pl.BlockSpec((tr, c), lambda i: (i, 0)),
        compiler_params=_params(("arbitrary",)),
    )(a)


def _adamw(w, g, m, v, name):
    r, c = w.shape
    tr = _rows_tile(r, c * 4, budget=1024 * 1024)
    c1 = 1.0 - ADAM_B1 ** ADAM_STEP
    c2 = 1.0 - ADAM_B2 ** ADAM_STEP

    def body(w_ref, g_ref, m_ref, v_ref, d_ref, nm_ref, nv_ref):
        gg = g_ref[...]
        nm = ADAM_B1 * m_ref[...] + (1.0 - ADAM_B1) * gg
        nv = ADAM_B2 * v_ref[...] + (1.0 - ADAM_B2) * (gg * gg)
        m_hat = nm / c1
        v_hat = nv / c2
        d_ref[...] = -ADAM_LR * (m_hat / (jnp.sqrt(v_hat) + ADAM_EPS) + ADAM_WD * w_ref[...])
        nm_ref[...] = nm
        nv_ref[...] = nv

    spec = pl.BlockSpec((tr, c), lambda i: (i, 0))
    return _pcall(
        body, name=name, grid=(r // tr,),
        out_shape=[jax.ShapeDtypeStruct((r, c), F32)] * 3,
        in_specs=[spec] * 4, out_specs=[spec] * 3,
        compiler_params=_params(("arbitrary",)),
    )(w, g, m, v)


def _ada_fwd(c_all, w_shard, b_shard):
    nb, d = c_all.shape
    cols = w_shard.shape[1]

    def body(c_ref, w_ref, b_ref, sc_ref, mod_ref):
        cv = c_ref[...]
        sc = cv * jax.nn.sigmoid(cv)
        sc_ref[...] = sc
        mod_ref[...] = jnp.dot(sc.astype(BF16), w_ref[...].astype(BF16), preferred_element_type=F32) + b_ref[...]

    return _pcall(
        body, name="ada_fwd",
        out_shape=[jax.ShapeDtypeStruct((nb, d), F32), jax.ShapeDtypeStruct((nb, cols), F32)],
        compiler_params=pltpu.CompilerParams(vmem_limit_bytes=VMEM_LIMIT),
    )(c_all, w_shard, b_shard)


def _ada_bwd(sc_t, dmod_cols):
    d, nb = sc_t.shape
    cols = dmod_cols.shape[1]
    tr = _rows_tile(d, cols * 4, budget=1024 * 1024)

    def body(s_ref, m_ref, o_ref):
        s = s_ref[...]
        m = m_ref[...]
        acc = s[:, 0:1] * m[0:1, :]
        for b in range(1, nb):
            acc = acc + s[:, b:b + 1] * m[b:b + 1, :]
        o_ref[...] = acc

    return _pcall(
        body, name="ada_bwd", grid=(d // tr,),
        out_shape=jax.ShapeDtypeStruct((d, cols), F32),
        in_specs=[pl.BlockSpec((tr, nb), lambda i: (i, 0)), pl.BlockSpec((nb, cols), lambda i: (0, 0))],
        out_specs=pl.BlockSpec((tr, cols), lambda i: (i, 0)),
        compiler_params=_params(("arbitrary",)),
    )(sc_t, dmod_cols)


def _log_sigmoid(x):
    return jnp.minimum(x, 0.0) - jnp.log1p(jnp.exp(-jnp.abs(x)))


def _fgate_fwd(fl2d, b_rows, tri_in, tri_blk):
    r = fl2d.shape[0]

    def body(x_ref, b_ref, u_ref, l_ref, f_ref):
        lf = _log_sigmoid(x_ref[...] + b_ref[...])
        c1 = _split_dot(lf, u_ref[...], 3)
        tot = jnp.broadcast_to(c1[:, LANES - 1:LANES], (r, LANES))
        acc = None
        for _ in range(3):
            part = tot.astype(BF16)
            dd = jnp.dot(l_ref[...], part, preferred_element_type=F32)
            acc = dd if acc is None else acc + dd
            tot = tot - part.astype(F32)
        f_ref[...] = c1 + acc

    return _pcall(
        body, name="fgate_fwd", out_shape=jax.ShapeDtypeStruct((r, LANES), F32),
        compiler_params=pltpu.CompilerParams(vmem_limit_bytes=VMEM_LIMIT),
    )(fl2d, b_rows, tri_in, tri_blk)


def _fgate_bwd(fl2d, b_rows, df_query, df_key, tri_in_rev, tri_blk_rev, head_rows):
    r = fl2d.shape[0]
    nhp = head_rows.shape[0]

    def body(x_ref, b_ref, dq_ref, dk_ref, u_ref, l_ref, hr_ref, o_ref, gb_ref):
        c1 = _split_dot(dq_ref[...] + dk_ref[...], u_ref[...], 3)
        tot = jnp.broadcast_to(c1[:, 0:1], (r, LANES))
        acc = None
        for _ in range(3):
            part = tot.astype(BF16)
            dd = jnp.dot(l_ref[...], part, preferred_element_type=F32)
            acc = dd if acc is None else acc + dd
            tot = tot - part.astype(F32)
        x = x_ref[...] + b_ref[...]
        e = jnp.exp(-jnp.abs(x))
        dfl = (c1 + acc) * (jnp.where(x >= 0, e, 1.0) / (1.0 + e))
        o_ref[...] = dfl
        rs = jnp.broadcast_to(jnp.sum(dfl, axis=1, keepdims=True), (r, LANES))
        gb = None
        for _ in range(3):
            part = rs.astype(BF16)
            dd = jnp.dot(hr_ref[...], part, preferred_element_type=F32)
            gb = dd if gb is None else gb + dd
            rs = rs - part.astype(F32)
        gb_ref[...] = gb

    return _pcall(
        body, name="fgate_bwd",
        out_shape=[jax.ShapeDtypeStruct((r, LANES), F32), jax.ShapeDtypeStruct((nhp, LANES), F32)],
        compiler_params=pltpu.CompilerParams(vmem_limit_bytes=VMEM_LIMIT),
    )(fl2d, b_rows, df_query, df_key, tri_in_rev, tri_blk_rev, head_rows)


def _norm_mod(x, g, scale, shift):
    r = lax.rsqrt(jnp.mean(x * x, axis=-1, keepdims=True) + EPS)
    return (x * r * g) * (1.0 + scale) + shift


def _norm_mod_bwd(x, dh, g, scale):
    r = lax.rsqrt(jnp.mean(x * x, axis=-1, keepdims=True) + EPS)
    xn = x * r
    dshift = jnp.sum(dh, axis=0, keepdims=True)
    dscale = jnp.sum(dh * (xn * g), axis=0, keepdims=True)
    dxn_g = dh * (1.0 + scale)
    dg = jnp.sum(dxn_g * xn, axis=0, keepdims=True)
    dxn = dxn_g * g
    dx = r * (dxn - xn * jnp.mean(dxn * xn, axis=-1, keepdims=True))
    return dx, dshift, dscale, dg


def _in_proj_fwd(x, mod8, g_attn, w_qkv, w_f, tm):
    t, d = x.shape
    dg = w_qkv.shape[1] // 6

    def body(x_ref, mod_ref, g_ref, w_ref, wf_ref, qkv_ref, fl_ref, h1_ref, h_sc):
        j = pl.program_id(1)

        @pl.when(j == 0)
        def _():
            h = _norm_mod(x_ref[...], g_ref[...], mod_ref[1:2, :], mod_ref[0:1, :]).astype(BF16)
            h_sc[...] = h
            h1_ref[...] = h
            fl_ref[...] = jnp.dot(h, wf_ref[...], preferred_element_type=F32)

        s = jnp.where((j == 0) | (j == 3), HEAD_DIM ** -0.5, 1.0)
        qkv_ref[...] = (jnp.dot(h_sc[...], w_ref[...], preferred_element_type=F32) * s).astype(BF16)

    return _pcall(
        body, name="in_proj_fwd", grid=(t // tm, 6),
        out_shape=[jax.ShapeDtypeStruct((t, 6 * dg), BF16), jax.ShapeDtypeStruct((t, LANES), F32),
                   jax.ShapeDtypeStruct((t, d), BF16)],
        in_specs=[pl.BlockSpec((tm, d), lambda i, j: (i, 0)), pl.BlockSpec((8, d), lambda i, j: (0, 0)),
                  pl.BlockSpec((1, d), lambda i, j: (0, 0)), pl.BlockSpec((d, dg), lambda i, j: (0, j)),
                  pl.BlockSpec((d, LANES), lambda i, j: (0, 0))],
        out_specs=[pl.BlockSpec((tm, dg), lambda i, j: (i, j)), pl.BlockSpec((tm, LANES), lambda i, j: (i, 0)),
                   pl.BlockSpec((tm, d), lambda i, j: (i, 0))],
        scratch_shapes=[pltpu.VMEM((tm, d), BF16)],
        compiler_params=_params(("arbitrary", "arbitrary")),
    )(x, mod8, g_attn, w_qkv, w_f)


def _head_rstd(o, bd):
    return lax.rsqrt(_split_dot(o * o, bd, 3) * (1.0 / HEAD_DIM) + EPS)


def _attn_out_fwd(x, o_fox, o_sb, g_fox, g_sb, w_out, mod8, bd, tm):
    t, d = x.shape
    dg = o_fox.shape[1]

    def body(x_ref, of_ref, os_ref, gf_ref, gs_ref, w_ref, mod_ref, bd_ref, x2_ref, mix_ref):
        of, osb = of_ref[...], os_ref[...]
        mf = (of * _head_rstd(of, bd_ref[...]) * gf_ref[...]).astype(BF16)
        ms = (osb * _head_rstd(osb, bd_ref[...]) * gs_ref[...]).astype(BF16)
        mix_ref[:, :dg] = mf
        mix_ref[:, dg:] = ms
        y = jnp.dot(mf, w_ref[:dg, :], preferred_element_type=F32) + jnp.dot(ms, w_ref[dg:, :], preferred_element_type=F32)
        x2_ref[...] = x_ref[...] + mod_ref[2:3, :] * y

    row = lambda w: pl.BlockSpec((tm, w), lambda i: (i, 0))
    full = lambda a: pl.BlockSpec(a.shape, lambda i: (0,) * a.ndim)
    return _pcall(
        body, name="attn_out_fwd", grid=(t // tm,),
        out_shape=[jax.ShapeDtypeStruct((t, d), F32), jax.ShapeDtypeStruct((t, 2 * dg), BF16)],
        in_specs=[row(d), row(dg), row(dg), full(g_fox), full(g_sb), full(w_out), full(mod8), full(bd)],
        out_specs=[row(d), row(2 * dg)],
        compiler_params=_params(("arbitrary",)),
    )(x, o_fox, o_sb, g_fox, g_sb, w_out, mod8, bd)


def _attn_out_bwd(dx2, mix, o_fox, o_sb, g_fox, g_sb, w_out, mod8, bd, hsel, tm):
    t, d = dx2.shape
    dg = o_fox.shape[1]

    def body(dx_ref, mix_ref, of_ref, os_ref, gf_ref, gs_ref, w_ref, mod_ref, bd_ref, hs_ref,
             dof_ref, dos_ref, dlt_ref, dxg_ref, part_ref):
        dx = dx_ref[...]
        gate = mod_ref[2:3, :]
        dxg = (dx * gate).astype(BF16)
        dxg_ref[...] = dxg
        mixv = mix_ref[...]
        y = jnp.dot(mixv[:, :dg], w_ref[:dg, :], preferred_element_type=F32)
        y = y + jnp.dot(mixv[:, dg:], w_ref[dg:, :], preferred_element_type=F32)
        part_ref[0] = jnp.zeros((8, d), F32)
        part_ref[0, 0:1, :] = jnp.sum(dx * y, axis=0, keepdims=True)
        for grp, (o_ref, g_ref, do_ref) in enumerate(((of_ref, gf_ref, dof_ref), (os_ref, gs_ref, dos_ref))):
            dmix = lax.dot_general(dxg, w_ref[grp * dg:(grp + 1) * dg, :], NT_DIMS, preferred_element_type=F32)
            o = o_ref[...]
            r = _head_rstd(o, bd_ref[...])
            n = o * r
            part_ref[0, 1:2, grp * dg:(grp + 1) * dg] = jnp.sum(dmix * n, axis=0, keepdims=True)
            dn = dmix * g_ref[...]
            mh = _split_dot(dn * n, bd_ref[...], 3) * (1.0 / HEAD_DIM)
            do = r * (dn - n * mh)
            do_ref[...] = do.astype(BF16)
            if grp == 0:
                dlt_ref[...] = _split_dot(do * o, hs_ref[...], 3)

    row = lambda w: pl.BlockSpec((tm, w), lambda i: (i, 0))
    full = lambda a: pl.BlockSpec(a.shape, lambda i: (0,) * a.ndim)
    nt = t // tm
    return _pcall(
        body, name="attn_out_bwd", grid=(nt,),
        out_shape=[jax.ShapeDtypeStruct((t, dg), BF16), jax.ShapeDtypeStruct((t, dg), BF16),
                   jax.ShapeDtypeStruct((t, LANES), F32), jax.ShapeDtypeStruct((t, d), BF16),
                   jax.ShapeDtypeStruct((nt, 8, d), F32)],
        in_specs=[row(d), row(2 * dg), row(dg), row(dg), full(g_fox), full(g_sb), full(w_out), full(mod8),
                  full(bd), full(hsel)],
        out_specs=[row(dg), row(dg), row(LANES), row(d), pl.BlockSpec((1, 8, d), lambda i: (i, 0, 0))],
        compiler_params=_params(("arbitrary",)),
    )(dx2, mix, o_fox, o_sb, g_fox, g_sb, w_out, mod8, bd, hsel)


def _in_proj_bwd(dparts, dfl, w_qkv, w_f, x, dx2, mod8, g_attn, tm):
    t, d = x.shape
    dg = dparts[0].shape[1]

    def body(*refs):
        d_refs = refs[:6]
        dfl_ref, w_ref, wf_ref, x_ref, dx2_ref, mod_ref, g_ref, gx_ref, dp_ref, dflb_ref, part_ref = refs[6:]
        dh = None
        for k in range(6):
            dk = d_refs[k][...]
            if k in (0, 3):
                dk = dk * HEAD_DIM ** -0.5
            db = dk.astype(BF16)
            dp_ref[:, k * dg:(k + 1) * dg] = db
            term = lax.dot_general(db, w_ref[:, k * dg:(k + 1) * dg], NT_DIMS, preferred_element_type=F32)
            dh = term if dh is None else dh + term
        dfb = dfl_ref[...].astype(BF16)
        dflb_ref[...] = dfb
        dh = dh + lax.dot_general(dfb, wf_ref[...], NT_DIMS, preferred_element_type=F32)
        dx, dshift, dscale, dgn = _norm_mod_bwd(x_ref[...], dh, g_ref[...], mod_ref[1:2, :])
        gx_ref[...] = dx2_ref[...] + dx
        part_ref[0] = jnp.zeros((8, d), F32)
        part_ref[0, 0:1, :] = dshift
        part_ref[0, 1:2, :] = dscale
        part_ref[0, 2:3, :] = dgn

    row = lambda w: pl.BlockSpec((tm, w), lambda i: (i, 0))
    full = lambda a: pl.BlockSpec(a.shape, lambda i: (0,) * a.ndim)
    nt = t // tm
    return _pcall(
        body, name="in_proj_bwd", grid=(nt,),
        out_shape=[jax.ShapeDtypeStruct((t, d), F32), jax.ShapeDtypeStruct((t, 6 * dg), BF16),
                   jax.ShapeDtypeStruct((t, LANES), BF16), jax.ShapeDtypeStruct((nt, 8, d), F32)],
        in_specs=[row(dg)] * 6 + [row(LANES), full(w_qkv), full(w_f), row(d), row(d), full(mod8), full(g_attn)],
        out_specs=[row(d), row(6 * dg), row(LANES), pl.BlockSpec((1, 8, d), lambda i: (i, 0, 0))],
        compiler_params=_params(("arbitrary",)),
    )(*dparts, dfl, w_qkv, w_f, x, dx2, mod8, g_attn)


def _matmul_tn(a, b, name):
    t, m = a.shape
    n = b.shape[1]
    a_t = a.T
    tm_ = _tile(m, (512, 256, 128))
    tn_ = _tile(n, (1024, 512, 256, 128))
    tk = _tile(t, (2048, 1024, 512, 256, 128))
    nk = t // tk

    def body(a_ref, b_ref, o_ref):
        k = pl.program_id(2)

        @pl.when(k == 0)
        def _():
            o_ref[...] = jnp.zeros_like(o_ref)

        o_ref[...] += jnp.dot(a_ref[...], b_ref[...], preferred_element_type=F32)

    return _pcall(
        body, name=name, grid=(m // tm_, n // tn_, nk),
        out_shape=jax.ShapeDtypeStruct((m, n), F32),
        in_specs=[pl.BlockSpec((tm_, tk), lambda i, j, k: (i, k)), pl.BlockSpec((tk, tn_), lambda i, j, k: (k, j))],
        out_specs=pl.BlockSpec((tm_, tn_), lambda i, j, k: (i, j)),
        compiler_params=_params(("arbitrary", "arbitrary", "arbitrary")),
    )(a_t, b)


HALO = 16


def _conv_taps(up_ext, cw, lo, rows):
    s1 = pltpu.roll(up_ext, 1, 0)
    s2 = pltpu.roll(up_ext, 2, 0)
    u = cw[2:3, :] * up_ext[lo:lo + rows] + cw[1:2, :] * s1[lo:lo + rows] + cw[0:1, :] * s2[lo:lo + rows] + cw[3:4, :]
    return u, s1, s2


def _ffn_fwd(x2, target, mod8, g_mlp, g_final, wg, wv, cwg, cwv, wd, tm, cf):
    t, d = x2.shape
    dfp = wg.shape[1]
    nt, nc = t // tm, dfp // cf
    hb = tm // HALO

    def body(x_ref, xp_ref, tg_ref, mod_ref, g_ref, gf_ref, wg_ref, wv_ref, cg_ref, cv_ref, wd_ref,
             dx3_ref, h2_ref, part_ref, h_sc, acc_sc):
        i, j = pl.program_id(0), pl.program_id(1)

        @pl.when(j == 0)
        def _():
            xe = jnp.concatenate([xp_ref[...], x_ref[...]], axis=0)
            h = _norm_mod(xe, g_ref[...], mod_ref[4:5, :], mod_ref[3:4, :]).astype(BF16)
            h_sc[...] = h
            h2_ref[...] = h[HALO:]
            acc_sc[...] = jnp.zeros_like(acc_sc)

        rowi = lax.broadcasted_iota(jnp.int32, (tm + HALO, 1), 0)
        keep = (rowi >= HALO) | (i > 0)
        hv = h_sc[...]
        upg = jnp.where(keep, jnp.dot(hv, wg_ref[...], preferred_element_type=F32), 0.0)
        upv = jnp.where(keep, jnp.dot(hv, wv_ref[...], preferred_element_type=F32), 0.0)
        ug, _, _ = _conv_taps(upg, cg_ref[...], HALO, tm)
        uv, _, _ = _conv_taps(upv, cv_ref[...], HALO, tm)
        act = (ug * jax.nn.sigmoid(ug) * uv).astype(BF16)
        acc_sc[...] += jnp.dot(act, wd_ref[...], preferred_element_type=F32)

        @pl.when(j == nc - 1)
        def _():
            y_ffn = acc_sc[...]
            x3 = x_ref[...] + mod_ref[5:6, :] * y_ffn
            r3 = lax.rsqrt(jnp.mean(x3 * x3, axis=-1, keepdims=True) + EPS)
            xn = x3 * r3
            gf = gf_ref[...]
            diff = xn * gf - tg_ref[...]
            dy = diff * (1.0 / d)
            dxn = dy * gf
            dx3 = r3 * (dxn - xn * jnp.mean(dxn * xn, axis=-1, keepdims=True))
            dx3_ref[...] = dx3
            part_ref[0] = jnp.zeros((8, d), F32)
            part_ref[0, 0:1, :] = jnp.sum(dy * xn, axis=0, keepdims=True)
            part_ref[0, 1:2, :] = jnp.sum(dx3 * y_ffn, axis=0, keepdims=True)
            part_ref[0, 2:3, :] = jnp.sum(diff * diff, axis=0, keepdims=True) * (0.5 / d)

    row = lambda w: pl.BlockSpec((tm, w), lambda i, j: (i, 0))
    full = lambda a: pl.BlockSpec(a.shape, lambda i, j: (0,) * a.ndim)
    return _pcall(
        body, name="ffn_fwd", grid=(nt, nc),
        out_shape=[jax.ShapeDtypeStruct((t, d), F32), jax.ShapeDtypeStruct((t, d), BF16),
                   jax.ShapeDtypeStruct((nt, 8, d), F32)],
        in_specs=[row(d), pl.BlockSpec((HALO, d), lambda i, j: (jnp.maximum(i * hb - 1, 0), 0)), row(d),
                  full(mod8), full(g_mlp), full(g_final),
                  pl.BlockSpec((d, cf), lambda i, j: (0, j)), pl.BlockSpec((d, cf), lambda i, j: (0, j)),
                  pl.BlockSpec((8, cf), lambda i, j: (0, j)), pl.BlockSpec((8, cf), lambda i, j: (0, j)),
                  pl.BlockSpec((cf, d), lambda i, j: (j, 0))],
        out_specs=[row(d), row(d), pl.BlockSpec((1, 8, d), lambda i, j: (i, 0, 0))],
        scratch_shapes=[pltpu.VMEM((tm + HALO, d), BF16), pltpu.VMEM((tm, d), F32)],
        compiler_params=_params(("arbitrary", "arbitrary")),
    )(x2, x2, target, mod8, g_mlp, g_final, wg, wv, cwg, cwv, wd)


def _ffn_bwd(x2, dx3, mod8, g_mlp, wg, wv, cwg, cwv, wd, tm, cf):
    t, d = x2.shape
    dfp = wg.shape[1]
    nt, nc = t // tm, dfp // cf
    hb = tm // HALO
    nhb = t // HALO
    ext = tm + 2 * HALO

    def body(x_ref, xp_ref, xn_ref, dx_ref, dxn_ref, mod_ref, g_ref, wg_ref, wv_ref, cg_ref, cv_ref, wd_ref,
             dx2_ref, dug_ref, duv_ref, act_ref, dxg_ref, part_ref, pcg_ref, pcv_ref, h_sc, dg_sc, dh_sc):
        i, j = pl.program_id(0), pl.program_id(1)

        @pl.when(j == 0)
        def _():
            xe = jnp.concatenate([xp_ref[...], x_ref[...], xn_ref[...]], axis=0)
            h_sc[...] = _norm_mod(xe, g_ref[...], mod_ref[4:5, :], mod_ref[3:4, :]).astype(BF16)
            de = (jnp.concatenate([dx_ref[...], dxn_ref[...]], axis=0) * mod_ref[5:6, :]).astype(BF16)
            dg_sc[...] = de
            dxg_ref[...] = de[:tm]
            dh_sc[...] = jnp.zeros_like(dh_sc)

        rowe = lax.broadcasted_iota(jnp.int32, (ext, 1), 0)
        keep_up = (rowe >= HALO) | (i > 0)
        rowu = lax.broadcasted_iota(jnp.int32, (tm + HALO, 1), 0)
        keep_du = (rowu < tm) | (i < nt - 1)
        hv = h_sc[...]
        upg = jnp.where(keep_up, jnp.dot(hv, wg_ref[...], preferred_element_type=F32), 0.0)
        upv = jnp.where(keep_up, jnp.dot(hv, wv_ref[...], preferred_element_type=F32), 0.0)
        cg, cv = cg_ref[...], cv_ref[...]
        ug, g1, g2 = _conv_taps(upg, cg, HALO, tm + HALO)
        uv, v1, v2 = _conv_taps(upv, cv, HALO, tm + HALO)
        dact = lax.dot_general(dg_sc[...], wd_ref[...], NT_DIMS, preferred_element_type=F32)
        sg = jax.nn.sigmoid(ug)
        sil = ug * sg
        act_ref[...] = (sil * uv)[:tm].astype(BF16)
        duv = jnp.where(keep_du, dact * sil, 0.0)
        dug = jnp.where(keep_du, dact * uv * (sg * (1.0 + ug * (1.0 - sg))), 0.0)

        def back(du, cw, up, s1, s2, pc_ref):
            n = tm + HALO
            dup = (cw[2:3, :] * du + cw[1:2, :] * pltpu.roll(du, n - 1, 0) + cw[0:1, :] * pltpu.roll(du, n - 2, 0))[:tm]
            dut = du[:tm]
            pc_ref[0] = jnp.zeros((8, cf), F32)
            pc_ref[0, 0:1, :] = jnp.sum(dut * s2[HALO:HALO + tm], axis=0, keepdims=True)
            pc_ref[0, 1:2, :] = jnp.sum(dut * s1[HALO:HALO + tm], axis=0, keepdims=True)
            pc_ref[0, 2:3, :] = jnp.sum(dut * up[HALO:HALO + tm], axis=0, keepdims=True)
            pc_ref[0, 3:4, :] = jnp.sum(dut, axis=0, keepdims=True)
            return dup.astype(BF16)

        dupg = back(dug, cg, upg, g1, g2, pcg_ref)
        dupv = back(duv, cv, upv, v1, v2, pcv_ref)
        dug_ref[...] = dupg
        duv_ref[...] = dupv
        dh_sc[...] += (lax.dot_general(dupg, wg_ref[...], NT_DIMS, preferred_element_type=F32)
                       + lax.dot_general(dupv, wv_ref[...], NT_DIMS, preferred_element_type=F32))

        @pl.when(j == nc - 1)
        def _():
            dx, dshift, dscale, dgn = _norm_mod_bwd(x_ref[...], dh_sc[...], g_ref[...], mod_ref[4:5, :])
            dx2_ref[...] = dx_ref[...] + dx
            part_ref[0] = jnp.zeros((8, d), F32)
            part_ref[0, 0:1, :] = dshift
            part_ref[0, 1:2, :] = dscale
            part_ref[0, 2:3, :] = dgn

    row = lambda w: pl.BlockSpec((tm, w), lambda i, j: (i, 0))
    prev = pl.BlockSpec((HALO, d), lambda i, j: (jnp.maximum(i * hb - 1, 0), 0))
    nxt = pl.BlockSpec((HALO, d), lambda i, j: (jnp.minimum((i + 1) * hb, nhb - 1), 0))
    full = lambda a: pl.BlockSpec(a.shape, lambda i, j: (0,) * a.ndim)
    chunk = pl.BlockSpec((tm, cf), lambda i, j: (i, j))
    pchunk = pl.BlockSpec((1, 8, cf), lambda i, j: (i, 0, j))
    return _pcall(
        body, name="ffn_bwd", grid=(nt, nc),
        out_shape=[jax.ShapeDtypeStruct((t, d), F32), jax.ShapeDtypeStruct((t, dfp), BF16),
                   jax.ShapeDtypeStruct((t, dfp), BF16), jax.ShapeDtypeStruct((t, dfp), BF16),
                   jax.ShapeDtypeStruct((t, d), BF16), jax.ShapeDtypeStruct((nt, 8, d), F32),
                   jax.ShapeDtypeStruct((nt, 8, dfp), F32), jax.ShapeDtypeStruct((nt, 8, dfp), F32)],
        in_specs=[row(d), prev, nxt, row(d), nxt, full(mod8), full(g_mlp),
                  pl.BlockSpec((d, cf), lambda i, j: (0, j)), pl.BlockSpec((d, cf), lambda i, j: (0, j)),
                  pl.BlockSpec((8, cf), lambda i, j: (0, j)), pl.BlockSpec((8, cf), lambda i, j: (0, j)),
                  pl.BlockSpec((cf, d), lambda i, j: (j, 0))],
        out_specs=[row(d), chunk, chunk, chunk, row(d), pl.BlockSpec((1, 8, d), lambda i, j: (i, 0, 0)), pchunk, pchunk],
        scratch_shapes=[pltpu.VMEM((ext, d), BF16), pltpu.VMEM((tm + HALO, d), BF16), pltpu.VMEM((tm, d), F32)],
        compiler_params=_params(("arbitrary", "arbitrary")),
    )(x2, x2, x2, dx3, dx3, mod8, g_mlp, wg, wv, cwg, cwv, wd)


def _head_masks():
    lane = lax.broadcasted_iota(jnp.int32, (1, LANES), 1)
    in_a = lane < HEAD_DIM
    return in_a, jnp.logical_not(in_a)


BLK = 2 * LANES


def _stack_heads(qkv, dg):
    t = qkv.shape[0]
    p = dg // LANES
    rows = _tile(t, (512, 256, 128))
    sub = rows // BLK

    def body(kf_ref, vf_ref, ks_ref, vs_ref, okf, ovf, oks, ovs):
        in_a, in_b = _head_masks()
        for src, dst in ((kf_ref, okf), (vf_ref, ovf), (ks_ref, oks), (vs_ref, ovs)):
            v = src[...]
            zero = jnp.zeros_like(v)
            va, vb = jnp.where(in_a, v, zero), jnp.where(in_b, v, zero)
            for s in range(sub):
                dst[0, s, :BLK, :] = va[s * BLK:(s + 1) * BLK]
                dst[0, s, BLK:, :] = vb[s * BLK:(s + 1) * BLK]

    col = lambda base: pl.BlockSpec((rows, LANES), lambda h, j: (j, base * p + h))
    out = pl.BlockSpec((1, sub, 2 * BLK, LANES), lambda h, j: (h, j, 0, 0))
    shape = jax.ShapeDtypeStruct((p, t // BLK, 2 * BLK, LANES), BF16)
    return _pcall(
        body, name="stack_heads", grid=(p, t // rows),
        out_shape=[shape] * 4, in_specs=[col(1), col(2), col(4), col(5)], out_specs=[out] * 4,
        compiler_params=_params(("arbitrary", "arbitrary")),
    )(qkv, qkv, qkv, qkv)


def _tile_masks():
    rowi = lax.broadcasted_iota(jnp.int32, (BLK, BLK), 0)
    coli = lax.broadcasted_iota(jnp.int32, (BLK, BLK), 1)
    return coli <= rowi, coli < rowi


def _pair_triangle(suffix):
    r = lax.broadcasted_iota(jnp.int32, (BLK, BLK), 0)
    c = lax.broadcasted_iota(jnp.int32, (BLK, BLK), 1)
    return ((r >= c) if suffix else (r <= c)).astype(BF16)


def _pair_cumsum(x2, tri, passes):
    return jnp.concatenate([_split_dot(x2[:, :BLK], tri, passes), _split_dot(x2[:, BLK:], tri, passes)], axis=1)


def _pair_specs(t, dg, base):
    p = dg // LANES
    q = pl.BlockSpec((BLK, LANES), lambda h, i: (i, base * p + h))
    kv = pl.BlockSpec((1, t // BLK, 2 * BLK, LANES), lambda h, i: (h, 0, 0, 0))
    return q, kv


def _fox_fwd(qkv, kst, vst, fcol, frow2, dg):
    t = qkv.shape[0]
    p, nq = dg // LANES, t // BLK
    nh = 2 * p

    def body(q_ref, k_ref, v_ref, ft_ref, fs_ref, o_ref, lse_ref):
        i = pl.program_id(1)
        in_a, _ = _head_masks()
        causal, _ = _tile_masks()
        q2 = q_ref[...]
        ft = tuple(jnp.broadcast_to(ft_ref[a], (BLK, BLK)) for a in range(2))

        def tile(j, carry, masked):
            m, l, acc = carry
            kb, vb = k_ref[0, j], v_ref[0, j]
            s2 = lax.dot_general(q2, kb, NT_DIMS, preferred_element_type=F32)
            fs = fs_ref[0, j]
            m_new, l_new, alpha, pr = [], [], [], []
            for a in range(2):
                sl = slice(a * BLK, (a + 1) * BLK)
                s = (s2[:, sl] + ft[a]) - fs[:, sl]
                if masked:
                    s = jnp.where(causal, s, NEG)
                mn = jnp.maximum(m[a], jnp.max(s, axis=1, keepdims=True))
                pa = jnp.exp(s - mn)
                al = jnp.exp(m[a] - mn)
                m_new.append(mn)
                alpha.append(al)
                l_new.append(al * l[a] + jnp.sum(pa, axis=1, keepdims=True))
                pr.append(pa.astype(BF16))
            acc = jnp.where(in_a, alpha[0], alpha[1]) * acc + jnp.dot(
                jnp.concatenate(pr, axis=1), vb, preferred_element_type=F32)
            return tuple(m_new), tuple(l_new), acc

        neg, zero = jnp.full((BLK, 1), NEG, F32), jnp.zeros((BLK, 1), F32)
        carry = lax.fori_loop(0, i, functools.partial(tile, masked=False), ((neg, neg), (zero, zero), jnp.zeros((BLK, LANES), F32)))
        m, l, acc = tile(i, carry, True)
        o_ref[...] = acc / jnp.where(in_a, l[0], l[1])
        lse_ref[0] = m[0] + jnp.log(l[0])
        lse_ref[1] = m[1] + jnp.log(l[1])

    qs, kv = _pair_specs(t, dg, 0)
    col = pl.BlockSpec((2, BLK, 1), lambda h, i: (h, i, 0))
    return _pcall(
        body, name="fox_fwd", grid=(p, nq),
        out_shape=[jax.ShapeDtypeStruct((t, dg), F32), jax.ShapeDtypeStruct((nh, t, 1), F32)],
        in_specs=[qs, kv, kv, col, pl.BlockSpec((1, nq, 1, 2 * BLK), lambda h, i: (h, 0, 0, 0))],
        out_specs=[pl.BlockSpec((BLK, LANES), lambda h, i: (i, h)), col],
        compiler_params=_params(("arbitrary", "arbitrary")),
    )(qkv, kst, vst, fcol, frow2)


def _fold_heads(stacked, in_a):
    return jnp.where(in_a, stacked[:BLK], stacked[BLK:])


def _fox_bwd(qkv, kst, vst, do, fcol, frow2, lse, delta, dg):
    t = qkv.shape[0]
    p, nq = dg // LANES, t // BLK
    nh = 2 * p

    def body(q_ref, k_ref, v_ref, do_ref, ft_ref, fs_ref, lse_ref, dl_ref, dq_ref, dk_ref, dv_ref, dfs_ref, dft_ref):
        i = pl.program_id(1)

        @pl.when(i == 0)
        def _():
            dk_ref[...] = jnp.zeros_like(dk_ref)
            dv_ref[...] = jnp.zeros_like(dv_ref)
            dfs_ref[...] = jnp.zeros_like(dfs_ref)

        in_a, _ = _head_masks()
        causal, _ = _tile_masks()
        q2, do2 = q_ref[...], do_ref[...]
        ft = tuple(jnp.broadcast_to(ft_ref[a] - lse_ref[a], (BLK, BLK)) for a in range(2))
        dl = tuple(jnp.broadcast_to(dl_ref[a], (BLK, BLK)) for a in range(2))

        def tile(j, carry, masked):
            dq, dft = carry
            kb, vb = k_ref[0, j], v_ref[0, j]
            s2 = lax.dot_general(q2, kb, NT_DIMS, preferred_element_type=F32)
            dp2 = lax.dot_general(do2, vb, NT_DIMS, preferred_element_type=F32)
            fs = fs_ref[0, j]
            pr, ds, dft_new = [], [], []
            for a in range(2):
                sl = slice(a * BLK, (a + 1) * BLK)
                s = (s2[:, sl] + ft[a]) - fs[:, sl]
                if masked:
                    s = jnp.where(causal, s, NEG)
                pa = jnp.exp(s)
                dsa = pa * (dp2[:, sl] - dl[a])
                pr.append(pa.astype(BF16))
                ds.append(dsa)
                dft_new.append(dft[a] + jnp.sum(dsa, axis=1, keepdims=True))
            ds2 = jnp.concatenate(ds, axis=1)
            dsb = ds2.astype(BF16)
            off = pl.multiple_of(j * BLK, BLK)
            dk_ref[pl.ds(off, BLK), :] += _fold_heads(lax.dot_general(dsb, q2, TN_DIMS, preferred_element_type=F32), in_a)
            dv_ref[pl.ds(off, BLK), :] += _fold_heads(
                lax.dot_general(jnp.concatenate(pr, axis=1), do2, TN_DIMS, preferred_element_type=F32), in_a)
            dfs_ref[0, j] += -jnp.sum(ds2, axis=0, keepdims=True)
            return dq + jnp.dot(dsb, kb, preferred_element_type=F32), tuple(dft_new)

        zero = jnp.zeros((BLK, 1), F32)
        carry = lax.fori_loop(0, i, functools.partial(tile, masked=False), (jnp.zeros((BLK, LANES), F32), (zero, zero)))
        dq, dft = tile(i, carry, True)
        dq_ref[...] = dq
        dft_ref[0] = dft[0]
        dft_ref[1] = dft[1]

    qs, kv = _pair_specs(t, dg, 0)
    col = pl.BlockSpec((2, BLK, 1), lambda h, i: (h, i, 0))
    rowspec = pl.BlockSpec((1, nq, 1, 2 * BLK), lambda h, i: (h, 0, 0, 0))
    blk = pl.BlockSpec((BLK, LANES), lambda h, i: (i, h))
    acc = pl.BlockSpec((t, LANES), lambda h, i: (0, h))
    return _pcall(
        body, name="fox_bwd", grid=(p, nq),
        out_shape=[jax.ShapeDtypeStruct((t, dg), F32)] * 3 + [jax.ShapeDtypeStruct((p, nq, 1, 2 * BLK), F32),
                                                              jax.ShapeDtypeStruct((nh, t, 1), F32)],
        in_specs=[qs, kv, kv, blk, col, rowspec, col, col],
        out_specs=[blk, acc, acc, rowspec, col],
        compiler_params=_params(("arbitrary", "arbitrary")),
    )(qkv, kst, vst, do, fcol, frow2, lse, delta)


def _softplus_parts(z):
    e = jnp.exp(-jnp.abs(z))
    return jnp.maximum(z, 0.0) + jnp.log(1.0 + e), e


def _sigmoid_from(z, e):
    d = 1.0 + e
    r = pl.reciprocal(d, approx=True)
    r = r * (2.0 - d * r)
    return jnp.where(z >= 0, 1.0, e) * r


def _sb_fwd(qkv, kst, vst, dg):
    t = qkv.shape[0]
    p, nq = dg // LANES, t // BLK
    nh = 2 * p

    def body(q_ref, k_ref, v_ref, o_ref, rt_ref):
        i = pl.program_id(1)
        _, strict = _tile_masks()
        strict2 = jnp.concatenate([strict, strict], axis=1)
        suffix = _pair_triangle(True)
        q2 = q_ref[...]

        def tile(j, carry, masked):
            rest, acc = carry
            kb, vb = k_ref[0, j], v_ref[0, j]
            z = lax.dot_general(q2, kb, NT_DIMS, preferred_element_type=F32)
            sp, _ = _softplus_parts(z)
            if masked:
                sp = jnp.where(strict2, sp, 0.0)
            cs = _pair_cumsum(sp, suffix, 2)
            w, rest_new = [], []
            for a in range(2):
                sl = slice(a * BLK, (a + 1) * BLK)
                wa = jnp.exp(z[:, sl] - cs[:, sl] - rest[a])
                if masked:
                    wa = jnp.where(strict, wa, 0.0)
                w.append(wa.astype(BF16))
                rest_new.append(rest[a] + cs[:, a * BLK:a * BLK + 1])
            acc = acc + jnp.dot(jnp.concatenate(w, axis=1), vb, preferred_element_type=F32)
            return tuple(rest_new), acc

        zero = jnp.zeros((BLK, 1), F32)
        carry = tile(i, ((zero, zero), jnp.zeros((BLK, LANES), F32)), True)
        rest, acc = lax.fori_loop(0, i, lambda jj, c: tile(i - 1 - jj, c, False), carry)
        o_ref[...] = acc
        rt_ref[0] = rest[0]
        rt_ref[1] = rest[1]

    qs, kv = _pair_specs(t, dg, 3)
    col = pl.BlockSpec((2, BLK, 1), lambda h, i: (h, i, 0))
    return _pcall(
        body, name="sb_fwd", grid=(p, nq),
        out_shape=[jax.ShapeDtypeStruct((t, dg), F32), jax.ShapeDtypeStruct((nh, t, 1), F32)],
        in_specs=[qs, kv, kv],
        out_specs=[pl.BlockSpec((BLK, LANES), lambda h, i: (i, h)), col],
        compiler_params=_params(("arbitrary", "arbitrary")),
    )(qkv, kst, vst)


def _sb_bwd(qkv, kst, vst, do, rtot, dg):
    t = qkv.shape[0]
    p, nq = dg // LANES, t // BLK

    def body(q_ref, k_ref, v_ref, do_ref, rt_ref, dq_ref, dk_ref, dv_ref):
        i = pl.program_id(1)

        @pl.when(i == 0)
        def _():
            dk_ref[...] = jnp.zeros_like(dk_ref)
            dv_ref[...] = jnp.zeros_like(dv_ref)

        in_a, _ = _head_masks()
        _, strict = _tile_masks()
        strict2 = jnp.concatenate([strict, strict], axis=1)
        prefix = _pair_triangle(False)
        q2, do2 = q_ref[...], do_ref[...]
        rt = (rt_ref[0], rt_ref[1])

        def tile(j, carry, masked):
            before, gbefore, dq = carry
            kb, vb = k_ref[0, j], v_ref[0, j]
            z = lax.dot_general(q2, kb, NT_DIMS, preferred_element_type=F32)
            da = lax.dot_general(do2, vb, NT_DIMS, preferred_element_type=F32)
            sp, e = _softplus_parts(z)
            sig = _sigmoid_from(z, e)
            if masked:
                sp = jnp.where(strict2, sp, 0.0)
            pre = _pair_cumsum(sp, prefix, 2)
            w = []
            for a in range(2):
                sl = slice(a * BLK, (a + 1) * BLK)
                wa = jnp.exp(z[:, sl] + (before[a] - rt[a]) + pre[:, sl] - sp[:, sl])
                if masked:
                    wa = jnp.where(strict, wa, 0.0)
                w.append(wa)
            w2 = jnp.concatenate(w, axis=1)
            g = w2 * da
            preg = _pair_cumsum(g, prefix, 1)
            dz = []
            for a in range(2):
                sl = slice(a * BLK, (a + 1) * BLK)
                dza = g[:, sl] * (1.0 - sig[:, sl]) - sig[:, sl] * (gbefore[a] + preg[:, sl] - g[:, sl])
                if masked:
                    dza = jnp.where(strict, dza, 0.0)
                dz.append(dza.astype(BF16))
            dzb = jnp.concatenate(dz, axis=1)
            off = pl.multiple_of(j * BLK, BLK)
            dk_ref[pl.ds(off, BLK), :] += _fold_heads(lax.dot_general(dzb, q2, TN_DIMS, preferred_element_type=F32), in_a)
            dv_ref[pl.ds(off, BLK), :] += _fold_heads(
                lax.dot_general(w2.astype(BF16), do2, TN_DIMS, preferred_element_type=F32), in_a)
            last = lambda x, a: x[:, (a + 1) * BLK - 1:(a + 1) * BLK]
            return (tuple(before[a] + last(pre, a) for a in range(2)),
                    tuple(gbefore[a] + last(preg, a) for a in range(2)),
                    dq + jnp.dot(dzb, kb, preferred_element_type=F32))

        zero = jnp.zeros((BLK, 1), F32)
        carry = lax.fori_loop(0, i, functools.partial(tile, masked=False), ((zero, zero), (zero, zero), jnp.zeros((BLK, LANES), F32)))
        dq_ref[...] = tile(i, carry, True)[2]

    qs, kv = _pair_specs(t, dg, 3)
    col = pl.BlockSpec((2, BLK, 1), lambda h, i: (h, i, 0))
    blk = pl.BlockSpec((BLK, LANES), lambda h, i: (i, h))
    acc = pl.BlockSpec((t, LANES), lambda h, i: (0, h))
    return _pcall(
        body, name="sb_bwd", grid=(p, nq),
        out_shape=[jax.ShapeDtypeStruct((t, dg), F32)] * 3,
        in_specs=[qs, kv, kv, blk, col],
        out_specs=[blk, acc, acc],
        compiler_params=_params(("arbitrary", "arbitrary")),
    )(qkv, kst, vst, do, rtot)


def _tri_constants(nh, t):
    nb = t // LANES
    r = nh * nb
    li = np.arange(LANES)
    tri_in = (li[:, None] <= li[None, :])
    ri = np.arange(r)
    same = (ri[:, None] // nb) == (ri[None, :] // nb)
    blk = same & (ri[None, :] < ri[:, None])
    blk_rev = same & (ri[None, :] > ri[:, None])
    head_rows = (np.arange(max(8, nh))[:, None] == (ri[None, :] // nb))
    as_bf16 = lambda a: jnp.asarray(a.astype(np.float32), BF16)
    return as_bf16(tri_in), as_bf16(blk), as_bf16(tri_in.T), as_bf16(blk_rev), as_bf16(head_rows)


def kernel(x, c, w_ada, b_ada, g_attn, w_in, b_fgate, g_out_fox, g_out_sb, w_out, g_mlp, w_up, conv_w, conv_b, w_down, g_final, loss_target, m_w_ada, m_b_ada, m_g_attn, m_w_in, m_b_fgate, m_g_out_fox, m_g_out_sb, m_w_out, m_g_mlp, m_w_up, m_conv_w, m_conv_b, m_w_down, m_g_final, v_w_ada, v_b_ada, v_g_attn, v_w_in, v_b_fgate, v_g_out_fox, v_g_out_sb, v_w_out, v_g_mlp, v_w_up, v_conv_w, v_conv_b, v_w_down, v_g_final):
    t, d = x.shape[1], x.shape[2]
    dg = d // 2
    nh = dg // HEAD_DIM
    n_in = 6 * dg + nh
    dff = w_down.shape[1] * 4
    dfp = -(-dff // 256) * 256
    cf = 256
    tm = _tile(t, (512, 256, 128))
    nq = t // BLK
    xi, yi, ci = lax.axis_index("x"), lax.axis_index("y"), lax.axis_index("c")
    shard = 2 * xi + yi
    me = 4 * xi + 2 * yi + ci

    x2d, tg2d = x[0], loss_target[0]

    c_all = _all_gather8(jnp.pad(c, ((0, 7), (0, 0)))).reshape(8, 8, d)[:, 0, :]
    ada_cols = w_ada.shape[2]
    b_shard = lax.dynamic_slice(b_ada, (0, shard * ada_cols), (1, ada_cols))
    sc_all, mod_shard = _ada_fwd(c_all, w_ada[0], b_shard)
    mod_all = _all_gather8(mod_shard).reshape(4, 2, 8, ada_cols)
    mod_me = lax.dynamic_index_in_dim(mod_all[:, 0], me, axis=1, keepdims=False)
    mod8 = jnp.pad(mod_me.reshape(6, d), ((0, 2), (0, 0)))

    g_in, g_out, g_up, g_down, g_cw = _gather_xy(
        [w_in[0].astype(BF16), w_out[0].astype(BF16), w_up[0].astype(BF16), w_down[0].astype(BF16), conv_w[0]])
    w_in_full = jnp.transpose(g_in, (1, 0, 2)).reshape(d, n_in)
    w_qkv = w_in_full[:, :6 * dg]
    w_f = jnp.pad(w_in_full[:, 6 * dg:], ((0, 0), (0, LANES - nh)))
    w_out_full = g_out.reshape(2 * dg, d)
    w_up_full = jnp.transpose(g_up, (1, 0, 2)).reshape(d, 2 * dff)
    padc = ((0, 0), (0, dfp - dff))
    wg, wv = jnp.pad(w_up_full[:, :dff], padc), jnp.pad(w_up_full[:, dff:], padc)
    wd = jnp.pad(g_down.reshape(dff, d), ((0, dfp - dff), (0, 0)))
    cw_full = jnp.transpose(g_cw, (1, 0, 2)).reshape(3, 2 * dff)
    cw4 = jnp.concatenate([cw_full, conv_b], axis=0)
    cwg = jnp.pad(cw4[:, :dff], ((0, 4), (0, dfp - dff)))
    cwv = jnp.pad(cw4[:, dff:], ((0, 4), (0, dfp - dff)))

    qkv, fl, h1 = _in_proj_fwd(x2d, mod8, g_attn, w_qkv, w_f, tm)
    tri_in, tri_blk, tri_in_rev, tri_blk_rev, head_rows = _tri_constants(nh, t)
    fl2d = fl[:, :nh].T.reshape(nh * t // LANES, LANES)
    b_rows = jnp.repeat(b_fgate[0], t // LANES)[:, None]
    f2d = _fgate_fwd(fl2d, b_rows, tri_in, tri_blk)
    fcol = f2d.reshape(nh, t, 1)
    pairs = nh // 2
    frow2 = jnp.transpose(f2d.reshape(pairs, 2, nq, BLK), (0, 2, 1, 3)).reshape(pairs, nq, 1, 2 * BLK)
    k_fox, v_fox, k_sb, v_sb = _stack_heads(qkv, dg)
    o_fox, lse = _fox_fwd(qkv, k_fox, v_fox, fcol, frow2, dg)
    o_sb, rtot = _sb_fwd(qkv, k_sb, v_sb, dg)
    li = np.arange(dg)
    bd = jnp.asarray((li[:, None] // HEAD_DIM == li[None, :] // HEAD_DIM).astype(np.float32), BF16)
    hsel = jnp.asarray((li[:, None] // HEAD_DIM == np.arange(LANES)[None, :]).astype(np.float32), BF16)
    x2, mix = _attn_out_fwd(x2d, o_fox, o_sb, g_out_fox, g_out_sb, w_out_full, mod8, bd, tm)
    g_final2 = g_final[None, :]
    dx3, h2, part_f = _ffn_fwd(x2, tg2d, mod8, g_mlp, g_final2, wg, wv, cwg, cwv, wd, tm, cf)

    dx2, dupg, dupv, act, dxg3, part_b, pcg, pcv = _ffn_bwd(x2, dx3, mod8, g_mlp, wg, wv, cwg, cwv, wd, tm, cf)
    do_fox, do_sb, delta, dxg2, part_o = _attn_out_bwd(dx2, mix, o_fox, o_sb, g_out_fox, g_out_sb, w_out_full, mod8, bd, hsel, tm)
    dcol = delta[:, :nh].T.reshape(nh, t, 1)
    dq_f, dk_f, dv_f, dfs2, dft = _fox_bwd(qkv, k_fox, v_fox, do_fox, fcol, frow2, lse, dcol, dg)
    dq_s, dk_s, dv_s = _sb_bwd(qkv, k_sb, v_sb, do_sb, rtot, dg)
    f2d_shape = (nh * t // LANES, LANES)
    dfs = jnp.transpose(dfs2.reshape(pairs, nq, 2, BLK), (0, 2, 1, 3))
    dfl2d, gb8 = _fgate_bwd(fl2d, b_rows, dft.reshape(f2d_shape), dfs.reshape(f2d_shape), tri_in_rev, tri_blk_rev, head_rows)
    dfl = jnp.pad(dfl2d.reshape(nh, t).T, ((0, 0), (0, LANES - nh)))
    grad_x, dproj, dflb, part_i = _in_proj_bwd([dq_f, dk_f, dv_f, dq_s, dk_s, dv_s], dfl, w_qkv, w_f, x2d, dx2, mod8, g_attn, tm)

    gw_qkv = _matmul_tn(h1, dproj, "grad_w_qkv")
    gw_f = _matmul_tn(h1, dflb, "grad_w_f")
    gw_in = jnp.concatenate([gw_qkv, gw_f[:, :nh]], axis=1)
    gw_out = _matmul_tn(mix, dxg2, "grad_w_out")
    gw_upg = _matmul_tn(h2, dupg, "grad_w_up_gate")
    gw_upv = _matmul_tn(h2, dupv, "grad_w_up_val")
    gw_up = jnp.concatenate([gw_upg[:, :dff], gw_upv[:, :dff]], axis=1)
    gw_down = _matmul_tn(act, dxg3, "grad_w_down")[:dff]

    sf = _sum_leading(part_f, "sum_part_ffn_fwd")
    sb_ = _sum_leading(part_b, "sum_part_ffn_bwd")
    so = _sum_leading(part_o, "sum_part_attn_out")
    si = _sum_leading(part_i, "sum_part_in_proj")
    scg = _sum_leading(pcg, "sum_part_conv_gate")
    scv = _sum_leading(pcv, "sum_part_conv_val")
    gb_f = gb8[:nh, 0]
    dmod = jnp.concatenate([si[0], si[1], so[0], sb_[0], sb_[1], sf[1]])
    g_conv_w = jnp.concatenate([scg[0:3, :dff], scv[0:3, :dff]], axis=1).reshape(-1)
    g_conv_b = jnp.concatenate([scg[3, :dff], scv[3, :dff]])
    loss_part = jnp.sum(sf[2])
    fields = [dmod, si[2], gb_f, so[1, :dg], so[1, dg:], sb_[2], g_conv_b, sf[0], g_conv_w, loss_part[None]]
    sizes = [int(f.shape[0]) for f in fields]
    n_pack = sum(sizes)
    lanes_pack = -(-n_pack // (8 * LANES)) * LANES
    pack = jnp.pad(jnp.concatenate(fields), (0, 8 * lanes_pack - n_pack)).reshape(8, lanes_pack)
    gathered = _all_gather8(pack)
    tot = _sum_leading(gathered.reshape(8, 8, lanes_pack), "sum_pack").reshape(-1)
    offs = np.concatenate([[0], np.cumsum(sizes)])
    take = lambda k: tot[int(offs[k]):int(offs[k + 1])]
    g_b_ada, g_g_attn, g_b_fgate, g_g_fox, g_g_sb, g_g_mlp, g_cb, g_g_final, g_cw_full, loss_v = [take(k) for k in range(10)]
    loss = loss_v[0]
    dmod_all = gathered.reshape(8, 8 * lanes_pack)[:, :6 * d]
    dmod_cols = lax.dynamic_slice(dmod_all, (0, shard * ada_cols), (8, ada_cols))
    g_w_ada = _ada_bwd(sc_all.T, dmod_cols)

    def col_pieces(g):
        r, cc = g.shape
        return jnp.transpose(g.reshape(2, r // 2, 4, cc // 4), (2, 0, 1, 3)).reshape(8, r // 2, cc // 4)

    def row_pieces(g):
        r, cc = g.shape
        return g.reshape(8, r // 8, cc)

    recv = _scatter8([p.astype(BF16) for p in (col_pieces(gw_in), row_pieces(gw_out), col_pieces(gw_up), row_pieces(gw_down))])
    halves = [_sum_leading(rv, nm) for rv, nm in zip(recv, ("sum_w_in", "sum_w_out", "sum_w_up", "sum_w_down"))]
    swapped = _swap_halves(halves)
    g_w_in, g_w_out, g_w_up, g_w_down = [s.reshape(2 * s.shape[1], s.shape[2]) for s in swapped]
    g_conv_w_shard = lax.dynamic_slice(g_cw_full.reshape(3, 2 * dff), (0, shard * (dff // 2)), (3, dff // 2))

    grads, deltas, new_m, new_v = {}, {}, {}, {}

    def step(name, w, g, m, v):
        shape = w.shape
        as2d = lambda a: a.reshape(-1, shape[-1])
        dl, nm, nv = _adamw(as2d(w), as2d(g), as2d(m), as2d(v), "adamw_" + name)
        grads[name], deltas[name], new_m[name], new_v[name] = g.reshape(shape), dl.reshape(shape), nm.reshape(shape), nv.reshape(shape)

    step("w_ada", w_ada, g_w_ada, m_w_ada, v_w_ada)
    step("w_in", w_in, g_w_in, m_w_in, v_w_in)
    step("w_out", w_out, g_w_out, m_w_out, v_w_out)
    step("w_up", w_up, g_w_up, m_w_up, v_w_up)
    step("conv_w", conv_w, g_conv_w_shard, m_conv_w, v_conv_w)
    step("w_down", w_down, g_w_down, m_w_down, v_w_down)

    small = [("b_ada", b_ada, g_b_ada, m_b_ada, v_b_ada), ("g_attn", g_attn, g_g_attn, m_g_attn, v_g_attn),
             ("b_fgate", b_fgate, g_b_fgate, m_b_fgate, v_b_fgate), ("g_out_fox", g_out_fox, g_g_fox, m_g_out_fox, v_g_out_fox),
             ("g_out_sb", g_out_sb, g_g_sb, m_g_out_sb, v_g_out_sb), ("g_mlp", g_mlp, g_g_mlp, m_g_mlp, v_g_mlp),
             ("conv_b", conv_b, g_cb, m_conv_b, v_conv_b), ("g_final", g_final, g_g_final, m_g_final, v_g_final)]
    ssz = [int(np.prod(s[1].shape)) for s in small]
    n_small = sum(ssz)
    lanes_small = -(-n_small // (8 * LANES)) * LANES
    packs = [jnp.pad(jnp.concatenate([s[k].reshape(-1) for s in small]), (0, 8 * lanes_small - n_small)).reshape(8, lanes_small)
             for k in (1, 2, 3, 4)]
    dl_s, nm_s, nv_s = _adamw(*packs, "adamw_small")
    so_ = np.concatenate([[0], np.cumsum(ssz)])
    for k, s in enumerate(small):
        cut = lambda a: a.reshape(-1)[int(so_[k]):int(so_[k + 1])].reshape(s[1].shape)
        grads[s[0]], deltas[s[0]], new_m[s[0]], new_v[s[0]] = s[2].reshape(s[1].shape), cut(dl_s), cut(nm_s), cut(nv_s)

    order = ["w_ada", "b_ada", "g_attn", "w_in", "b_fgate", "g_out_fox", "g_out_sb", "w_out", "g_mlp", "w_up",
             "conv_w", "conv_b", "w_down", "g_final"]
    return (loss, grad_x[None], *[grads[n] for n in order], *[deltas[n] for n in order],
            *[new_m[n] for n in order], *[new_v[n] for n in order])
```

```python
import functools

import numpy as np
import jax
import jax.numpy as jnp
from jax import lax
from jax.experimental import pallas as pl
from jax.experimental.pallas import tpu as pltpu

F32 = jnp.float32
BF16 = jnp.bfloat16
MESH = pl.DeviceIdType.MESH

HEAD_DIM = 64
LANES = 128
EPS = 1e-6
NEG = -1e30
ADAM_LR, ADAM_B1, ADAM_B2, ADAM_EPS, ADAM_WD, ADAM_STEP = 0.001, 0.9, 0.999, 1e-08, 0.01, 10
V7X_VMEM_BYTES = 64 * 1024 * 1024
VMEM_LIMIT = V7X_VMEM_BYTES - 12 * 1024 * 1024
NT_DIMS = (((1,), (1,)), ((), ()))
TN_DIMS = (((0,), (0,)), ((), ()))


def _pcall(body, **kw):
    return pl.pallas_call(body, **kw)


def _params(sem=None, **kw):
    return pltpu.CompilerParams(dimension_semantics=sem, vmem_limit_bytes=VMEM_LIMIT, **kw)


def _split_dot(x, m, passes):
    acc = None
    for _ in range(passes):
        part = x.astype(BF16)
        d = jnp.dot(part, m, preferred_element_type=F32)
        acc = d if acc is None else acc + d
        x = x - part.astype(F32)
    return acc


def _tile(n, candidates):
    for t in candidates:
        if n % t == 0:
            return t
    return n


def _rows_tile(rows, row_bytes, budget=2 * 1024 * 1024):
    best = None
    for t in range(8, rows + 1, 8):
        if rows % t == 0 and t * row_bytes <= budget:
            best = t
    return best if best is not None else rows


def _all_gather8(v):
    m_per, n = v.shape

    def body(x_ref, out_ref, send_sems, recv_sems, local_sem):
        x, y, c = lax.axis_index("x"), lax.axis_index("y"), lax.axis_index("c")
        me, sibling = (x, y, c), (x, y, 1 - c)
        chips = [(1 - x, y), (x, 1 - y), (1 - x, 1 - y)]

        def rows(px, py, pc):
            return out_ref.at[pl.ds((4 * px + 2 * py + pc) * m_per, m_per), :]

        def copy(k, block, to, src=None):
            return pltpu.make_async_remote_copy(
                src_ref=rows(*block) if src is None else src, dst_ref=rows(*block),
                send_sem=send_sems.at[k], recv_sem=recv_sems.at[k], device_id=to, device_id_type=MESH)

        mine = pltpu.make_async_copy(x_ref, rows(*me), local_sem)
        mine.start()
        first = [copy(0, me, sibling, src=x_ref)]
        first += [copy(1 + j, me, (*chip, c), src=x_ref) for j, chip in enumerate(chips)]
        for cp in first:
            cp.start()
        passed = [copy(4 + j, (*chip, c), sibling) for j, chip in enumerate(chips)]
        for j, chip in enumerate(chips):
            copy(1 + j, (*chip, c), me).wait_recv()
            passed[j].start()
        copy(0, sibling, me).wait_recv()
        for j, chip in enumerate(chips):
            copy(4 + j, (*chip, 1 - c), me).wait_recv()
        for cp in first + passed:
            cp.wait_send()
        mine.wait()

    return _pcall(
        body, name="all_gather8",
        out_shape=jax.ShapeDtypeStruct((8 * m_per, n), v.dtype),
        in_specs=[pl.BlockSpec(memory_space=pltpu.VMEM)],
        out_specs=pl.BlockSpec(memory_space=pltpu.VMEM),
        scratch_shapes=[pltpu.SemaphoreType.DMA((7,)), pltpu.SemaphoreType.DMA((7,)), pltpu.SemaphoreType.DMA],
        compiler_params=pltpu.CompilerParams(vmem_limit_bytes=VMEM_LIMIT),
    )(v)


def _gather_xy(shards):
    n = len(shards)

    def body(*refs):
        ins, outs = refs[:n], refs[n:2 * n]
        send_sems, recv_sems, local_sems = refs[2 * n:]
        x, y, c = lax.axis_index("x"), lax.axis_index("y"), lax.axis_index("c")
        chips = [(1 - x, y), (x, 1 - y), (1 - x, 1 - y)]
        mine = 2 * x + y
        local, remote = [], []
        for w in range(n):
            cp = pltpu.make_async_copy(ins[w], outs[w].at[mine], local_sems.at[w])
            cp.start()
            local.append(cp)
            for k, (px, py) in enumerate(chips):
                cp = pltpu.make_async_remote_copy(
                    src_ref=ins[w], dst_ref=outs[w].at[mine], send_sem=send_sems.at[3 * w + k],
                    recv_sem=recv_sems.at[3 * w + k], device_id=(px, py, c), device_id_type=MESH)
                cp.start()
                remote.append(cp)
        for cp in remote:
            cp.wait_recv()
        for cp in remote:
            cp.wait_send()
        for cp in local:
            cp.wait()

    hbm = pl.BlockSpec(memory_space=pltpu.HBM)
    return _pcall(
        body, name="gather_xy",
        out_shape=[jax.ShapeDtypeStruct((4,) + s.shape, s.dtype) for s in shards],
        in_specs=[hbm] * n, out_specs=[hbm] * n,
        scratch_shapes=[pltpu.SemaphoreType.DMA((3 * n,)), pltpu.SemaphoreType.DMA((3 * n,)),
                        pltpu.SemaphoreType.DMA((n,))],
        compiler_params=pltpu.CompilerParams(vmem_limit_bytes=VMEM_LIMIT),
    )(*shards)


def _scatter8(pieces):
    n = len(pieces)

    def body(*refs):
        ins, outs = refs[:n], refs[n:2 * n]
        send_sems, recv_sems, local_sems = refs[2 * n:]
        x, y, c = lax.axis_index("x"), lax.axis_index("y"), lax.axis_index("c")
        me = 4 * x + 2 * y + c
        local, remote = [], []
        for w in range(n):
            cp = pltpu.make_async_copy(ins[w].at[me], outs[w].at[me], local_sems.at[w])
            cp.start()
            local.append(cp)
            for f in range(1, 8):
                px = 1 - x if f & 4 else x
                py = 1 - y if f & 2 else y
                pc = 1 - c if f & 1 else c
                cp = pltpu.make_async_remote_copy(
                    src_ref=ins[w].at[4 * px + 2 * py + pc], dst_ref=outs[w].at[me],
                    send_sem=send_sems.at[7 * w + f - 1], recv_sem=recv_sems.at[7 * w + f - 1],
                    device_id=(px, py, pc), device_id_type=MESH)
                cp.start()
                remote.append(cp)
        for cp in remote:
            cp.wait_recv()
        for cp in remote:
            cp.wait_send()
        for cp in local:
            cp.wait()

    hbm = pl.BlockSpec(memory_space=pltpu.HBM)
    return _pcall(
        body, name="scatter8",
        out_shape=[jax.ShapeDtypeStruct(p.shape, p.dtype) for p in pieces],
        in_specs=[hbm] * n, out_specs=[hbm] * n,
        scratch_shapes=[pltpu.SemaphoreType.DMA((7 * n,)), pltpu.SemaphoreType.DMA((7 * n,)),
                        pltpu.SemaphoreType.DMA((n,))],
        compiler_params=pltpu.CompilerParams(vmem_limit_bytes=VMEM_LIMIT),
    )(*pieces)


def _swap_halves(halves):
    n = len(halves)

    def body(*refs):
        ins, outs = refs[:n], refs[n:2 * n]
        send_sems, recv_sems, local_sems = refs[2 * n:]
        x, y, c = lax.axis_index("x"), lax.axis_index("y"), lax.axis_index("c")
        local, remote = [], []
        for w in range(n):
            cp = pltpu.make_async_copy(ins[w], outs[w].at[c], local_sems.at[w])
            cp.start()
            local.append(cp)
            cp = pltpu.make_async_remote_copy(
                src_ref=ins[w], dst_ref=outs[w].at[c], send_sem=send_sems.at[w], recv_sem=recv_sems.at[w],
                device_id=(x, y, 1 - c), device_id_type=MESH)
            cp.start()
            remote.append(cp)
        for cp in remote:
            cp.wait_recv()
        for cp in remote:
            cp.wait_send()
        for cp in local:
            cp.wait()

    hbm = pl.BlockSpec(memory_space=pltpu.HBM)
    return _pcall(
        body, name="swap_halves",
        out_shape=[jax.ShapeDtypeStruct((2,) + h.shape, h.dtype) for h in halves],
        in_specs=[hbm] * n, out_specs=[hbm] * n,
        scratch_shapes=[pltpu.SemaphoreType.DMA((n,)), pltpu.SemaphoreType.DMA((n,)), pltpu.SemaphoreType.DMA((n,))],
        compiler_params=pltpu.CompilerParams(vmem_limit_bytes=VMEM_LIMIT),
    )(*halves)


def _sum_leading(a, name):
    n, r, c = a.shape
    tr = _rows_tile(r, n * c * 4, budget=6 * 1024 * 1024)
    if a.dtype == BF16 and tr % 16:
        tr = r

    def body(a_ref, o_ref):
        acc = a_ref[0].astype(F32)
        for k in range(1, n):
            acc = acc + a_ref[k].astype(F32)
        o_ref[...] = acc

    return _pcall(
        body, name=name, grid=(r // tr,),
        out_shape=jax.ShapeDtypeStruct((r, c), F32),
        in_specs=[pl.BlockSpec((n, tr, c), lambda i: (0, i, 0))],
        out_specs=pl.BlockSpec((tr, c), lambda i: (i, 0)),
        compiler_params=_params(("arbitrary",)),
    )(a)


def _to_bf16(a, name):
    n, r, c = a.shape

    def body(a_ref, o_ref):
        o_ref[...] = a_ref[...].astype(BF16)

    spec = pl.BlockSpec((1, r, c), lambda i: (i, 0, 0))
    return _pcall(
        body, name=name, grid=(n,), out_shape=jax.ShapeDtypeStruct(a.shape, BF16),
        in_specs=[spec], out_specs=spec, compiler_params=_params(("arbitrary",)),
    )(a)


def _adamw(w, g, m, v, name):
    r, c = w.shape
    tr = _rows_tile(r, c * 4, budget=1024 * 1024)
    c1 = 1.0 - ADAM_B1 ** ADAM_STEP
    c2 = 1.0 - ADAM_B2 ** ADAM_STEP

    def body(w_ref, g_ref, m_ref, v_ref, d_ref, nm_ref, nv_ref):
        gg = g_ref[...]
        nm = ADAM_B1 * m_ref[...] + (1.0 - ADAM_B1) * gg
        nv = ADAM_B2 * v_ref[...] + (1.0 - ADAM_B2) * (gg * gg)
        m_hat = nm / c1
        v_hat = nv / c2
        d_ref[...] = -ADAM_LR * (m_hat / (jnp.sqrt(v_hat) + ADAM_EPS) + ADAM_WD * w_ref[...])
        nm_ref[...] = nm
        nv_ref[...] = nv

    spec = pl.BlockSpec((tr, c), lambda i: (i, 0))
    return _pcall(
        body, name=name, grid=(r // tr,),
        out_shape=[jax.ShapeDtypeStruct((r, c), F32)] * 3,
        in_specs=[spec] * 4, out_specs=[spec] * 3,
        compiler_params=_params(("arbitrary",)),
    )(w, g, m, v)


def _ada_fwd(c_all, w_shard, b_shard):
    nb, d = c_all.shape
    cols = w_shard.shape[1]

    def body(c_ref, w_ref, b_ref, sc_ref, mod_ref):
        cv = c_ref[...]
        sc = cv * jax.nn.sigmoid(cv)
        sc_ref[...] = sc
        mod_ref[...] = jnp.dot(sc.astype(BF16), w_ref[...].astype(BF16), preferred_element_type=F32) + b_ref[...]

    return _pcall(
        body, name="ada_fwd",
        out_shape=[jax.ShapeDtypeStruct((nb, d), F32), jax.ShapeDtypeStruct((nb, cols), F32)],
        compiler_params=pltpu.CompilerParams(vmem_limit_bytes=VMEM_LIMIT),
    )(c_all, w_shard, b_shard)


def _ada_bwd(sc_t, dmod_cols):
    d, nb = sc_t.shape
    cols = dmod_cols.shape[1]
    tr = _rows_tile(d, cols * 4, budget=1024 * 1024)

    def body(s_ref, m_ref, o_ref):
        s = s_ref[...]
        m = m_ref[...]
        acc = s[:, 0:1] * m[0:1, :]
        for b in range(1, nb):
            acc = acc + s[:, b:b + 1] * m[b:b + 1, :]
        o_ref[...] = acc

    return _pcall(
        body, name="ada_bwd", grid=(d // tr,),
        out_shape=jax.ShapeDtypeStruct((d, cols), F32),
        in_specs=[pl.BlockSpec((tr, nb), lambda i: (i, 0)), pl.BlockSpec((nb, cols), lambda i: (0, 0))],
        out_specs=pl.BlockSpec((tr, cols), lambda i: (i, 0)),
        compiler_params=_params(("arbitrary",)),
    )(sc_t, dmod_cols)


def _log_sigmoid(x):
    return jnp.minimum(x, 0.0) - jnp.log1p(jnp.exp(-jnp.abs(x)))


def _fgate_fwd(fl2d, b_rows, tri_in, tri_blk):
    r = fl2d.shape[0]

    def body(x_ref, b_ref, u_ref, l_ref, f_ref):
        lf = _log_sigmoid(x_ref[...] + b_ref[...])
        c1 = _split_dot(lf, u_ref[...], 3)
        tot = jnp.broadcast_to(c1[:, LANES - 1:LANES], (r, LANES))
        acc = None
        for _ in range(3):
            part = tot.astype(BF16)
            dd = jnp.dot(l_ref[...], part, preferred_element_type=F32)
            acc = dd if acc is None else acc + dd
            tot = tot - part.astype(F32)
        f_ref[...] = c1 + acc

    return _pcall(
        body, name="fgate_fwd", out_shape=jax.ShapeDtypeStruct((r, LANES), F32),
        compiler_params=pltpu.CompilerParams(vmem_limit_bytes=VMEM_LIMIT),
    )(fl2d, b_rows, tri_in, tri_blk)


def _fgate_bwd(fl2d, b_rows, df_query, df_key, tri_in_rev, tri_blk_rev, head_rows):
    r = fl2d.shape[0]
    nhp = head_rows.shape[0]

    def body(x_ref, b_ref, dq_ref, dk_ref, u_ref, l_ref, hr_ref, o_ref, gb_ref):
        c1 = _split_dot(dq_ref[...] + dk_ref[...], u_ref[...], 3)
        tot = jnp.broadcast_to(c1[:, 0:1], (r, LANES))
        acc = None
        for _ in range(3):
            part = tot.astype(BF16)
            dd = jnp.dot(l_ref[...], part, preferred_element_type=F32)
            acc = dd if acc is None else acc + dd
            tot = tot - part.astype(F32)
        x = x_ref[...] + b_ref[...]
        e = jnp.exp(-jnp.abs(x))
        dfl = (c1 + acc) * (jnp.where(x >= 0, e, 1.0) / (1.0 + e))
        o_ref[...] = dfl
        rs = jnp.broadcast_to(jnp.sum(dfl, axis=1, keepdims=True), (r, LANES))
        gb = None
        for _ in range(3):
            part = rs.astype(BF16)
            dd = jnp.dot(hr_ref[...], part, preferred_element_type=F32)
            gb = dd if gb is None else gb + dd
            rs = rs - part.astype(F32)
        gb_ref[...] = gb

    return _pcall(
        body, name="fgate_bwd",
        out_shape=[jax.ShapeDtypeStruct((r, LANES), F32), jax.ShapeDtypeStruct((nhp, LANES), F32)],
        compiler_params=pltpu.CompilerParams(vmem_limit_bytes=VMEM_LIMIT),
    )(fl2d, b_rows, df_query, df_key, tri_in_rev, tri_blk_rev, head_rows)


def _norm_mod(x, g, scale, shift):
    r = lax.rsqrt(jnp.mean(x * x, axis=-1, keepdims=True) + EPS)
    return (x * r * g) * (1.0 + scale) + shift


def _norm_mod_bwd(x, dh, g, scale):
    r = lax.rsqrt(jnp.mean(x * x, axis=-1, keepdims=True) + EPS)
    xn = x * r
    dshift = jnp.sum(dh, axis=0, keepdims=True)
    dscale = jnp.sum(dh * (xn * g), axis=0, keepdims=True)
    dxn_g = dh * (1.0 + scale)
    dg = jnp.sum(dxn_g * xn, axis=0, keepdims=True)
    dxn = dxn_g * g
    dx = r * (dxn - xn * jnp.mean(dxn * xn, axis=-1, keepdims=True))
    return dx, dshift, dscale, dg


def _in_proj_fwd(x, mod8, g_attn, w_qkv, w_f, tm):
    t, d = x.shape
    dg = w_qkv.shape[1] // 6

    def body(x_ref, mod_ref, g_ref, w_ref, wf_ref, qkv_ref, fl_ref, h1_ref, h_sc):
        j = pl.program_id(1)

        @pl.when(j == 0)
        def _():
            h = _norm_mod(x_ref[...], g_ref[...], mod_ref[1:2, :], mod_ref[0:1, :]).astype(BF16)
            h_sc[...] = h
            h1_ref[...] = h
            fl_ref[...] = jnp.dot(h, wf_ref[...], preferred_element_type=F32)

        s = jnp.where((j == 0) | (j == 3), HEAD_DIM ** -0.5, 1.0)
        qkv_ref[...] = (jnp.dot(h_sc[...], w_ref[...], preferred_element_type=F32) * s).astype(BF16)

    return _pcall(
        body, name="in_proj_fwd", grid=(t // tm, 6),
        out_shape=[jax.ShapeDtypeStruct((t, 6 * dg), BF16), jax.ShapeDtypeStruct((t, LANES), F32),
                   jax.ShapeDtypeStruct((t, d), BF16)],
        in_specs=[pl.BlockSpec((tm, d), lambda i, j: (i, 0)), pl.BlockSpec((8, d), lambda i, j: (0, 0)),
                  pl.BlockSpec((1, d), lambda i, j: (0, 0)), pl.BlockSpec((d, dg), lambda i, j: (0, j)),
                  pl.BlockSpec((d, LANES), lambda i, j: (0, 0))],
        out_specs=[pl.BlockSpec((tm, dg), lambda i, j: (i, j)), pl.BlockSpec((tm, LANES), lambda i, j: (i, 0)),
                   pl.BlockSpec((tm, d), lambda i, j: (i, 0))],
        scratch_shapes=[pltpu.VMEM((tm, d), BF16)],
        compiler_params=_params(("arbitrary", "arbitrary")),
    )(x, mod8, g_attn, w_qkv, w_f)


def _head_rstd(o, bd):
    return lax.rsqrt(_split_dot(o * o, bd, 3) * (1.0 / HEAD_DIM) + EPS)


def _attn_out_fwd(x, o_fox, o_sb, g_fox, g_sb, w_out, mod8, bd, tm):
    t, d = x.shape
    dg = o_fox.shape[1]

    def body(x_ref, of_ref, os_ref, gf_ref, gs_ref, w_ref, mod_ref, bd_ref, x2_ref, mix_ref):
        of, osb = of_ref[...], os_ref[...]
        mf = (of * _head_rstd(of, bd_ref[...]) * gf_ref[...]).astype(BF16)
        ms = (osb * _head_rstd(osb, bd_ref[...]) * gs_ref[...]).astype(BF16)
        mix_ref[:, :dg] = mf
        mix_ref[:, dg:] = ms
        y = jnp.dot(mf, w_ref[:dg, :], preferred_element_type=F32) + jnp.dot(ms, w_ref[dg:, :], preferred_element_type=F32)
        x2_ref[...] = x_ref[...] + mod_ref[2:3, :] * y

    row = lambda w: pl.BlockSpec((tm, w), lambda i: (i, 0))
    full = lambda a: pl.BlockSpec(a.shape, lambda i: (0,) * a.ndim)
    return _pcall(
        body, name="attn_out_fwd", grid=(t // tm,),
        out_shape=[jax.ShapeDtypeStruct((t, d), F32), jax.ShapeDtypeStruct((t, 2 * dg), BF16)],
        in_specs=[row(d), row(dg), row(dg), full(g_fox), full(g_sb), full(w_out), full(mod8), full(bd)],
        out_specs=[row(d), row(2 * dg)],
        compiler_params=_params(("arbitrary",)),
    )(x, o_fox, o_sb, g_fox, g_sb, w_out, mod8, bd)


def _attn_out_bwd(dx2, mix, o_fox, o_sb, g_fox, g_sb, w_out, mod8, bd, hsel, tm):
    t, d = dx2.shape
    dg = o_fox.shape[1]

    def body(dx_ref, mix_ref, of_ref, os_ref, gf_ref, gs_ref, w_ref, mod_ref, bd_ref, hs_ref,
             dof_ref, dos_ref, dlt_ref, dxg_ref, part_ref):
        dx = dx_ref[...]
        gate = mod_ref[2:3, :]
        dxg = (dx * gate).astype(BF16)
        dxg_ref[...] = dxg
        mixv = mix_ref[...]
        y = jnp.dot(mixv[:, :dg], w_ref[:dg, :], preferred_element_type=F32)
        y = y + jnp.dot(mixv[:, dg:], w_ref[dg:, :], preferred_element_type=F32)
        part_ref[0] = jnp.zeros((8, d), F32)
        part_ref[0, 0:1, :] = jnp.sum(dx * y, axis=0, keepdims=True)
        for grp, (o_ref, g_ref, do_ref) in enumerate(((of_ref, gf_ref, dof_ref), (os_ref, gs_ref, dos_ref))):
            dmix = lax.dot_general(dxg, w_ref[grp * dg:(grp + 1) * dg, :], NT_DIMS, preferred_element_type=F32)
            o = o_ref[...]
            r = _head_rstd(o, bd_ref[...])
            n = o * r
            part_ref[0, 1:2, grp * dg:(grp + 1) * dg] = jnp.sum(dmix * n, axis=0, keepdims=True)
            dn = dmix * g_ref[...]
            mh = _split_dot(dn * n, bd_ref[...], 3) * (1.0 / HEAD_DIM)
            do = r * (dn - n * mh)
            do_ref[...] = do.astype(BF16)
            if grp == 0:
                dlt_ref[...] = _split_dot(do * o, hs_ref[...], 3)

    row = lambda w: pl.BlockSpec((tm, w), lambda i: (i, 0))
    full = lambda a: pl.BlockSpec(a.shape, lambda i: (0,) * a.ndim)
    nt = t // tm
    return _pcall(
        body, name="attn_out_bwd", grid=(nt,),
        out_shape=[jax.ShapeDtypeStruct((t, dg), BF16), jax.ShapeDtypeStruct((t, dg), BF16),
                   jax.ShapeDtypeStruct((t, LANES), F32), jax.ShapeDtypeStruct((t, d), BF16),
                   jax.ShapeDtypeStruct((nt, 8, d), F32)],
        in_specs=[row(d), row(2 * dg), row(dg), row(dg), full(g_fox), full(g_sb), full(w_out), full(mod8),
                  full(bd), full(hsel)],
        out_specs=[row(dg), row(dg), row(LANES), row(d), pl.BlockSpec((1, 8, d), lambda i: (i, 0, 0))],
        compiler_params=_params(("arbitrary",)),
    )(dx2, mix, o_fox, o_sb, g_fox, g_sb, w_out, mod8, bd, hsel)


def _in_proj_bwd(dparts, dfl, w_qkv, w_f, x, dx2, mod8, g_attn, tm):
    t, d = x.shape
    dg = dparts[0].shape[1]

    def body(*refs):
        d_refs = refs[:6]
        dfl_ref, w_ref, wf_ref, x_ref, dx2_ref, mod_ref, g_ref, gx_ref, dp_ref, dflb_ref, part_ref = refs[6:]
        dh = None
        for k in range(6):
            dk = d_refs[k][...]
            if k in (0, 3):
                dk = dk * HEAD_DIM ** -0.5
            db = dk.astype(BF16)
            dp_ref[:, k * dg:(k + 1) * dg] = db
            term = lax.dot_general(db, w_ref[:, k * dg:(k + 1) * dg], NT_DIMS, preferred_element_type=F32)
            dh = term if dh is None else dh + term
        dfb = dfl_ref[...].astype(BF16)
        dflb_ref[...] = dfb
        dh = dh + lax.dot_general(dfb, wf_ref[...], NT_DIMS, preferred_element_type=F32)
        dx, dshift, dscale, dgn = _norm_mod_bwd(x_ref[...], dh, g_ref[...], mod_ref[1:2, :])
        gx_ref[...] = dx2_ref[...] + dx
        part_ref[0] = jnp.zeros((8, d), F32)
        part_ref[0, 0:1, :] = dshift
        part_ref[0, 1:2, :] = dscale
        part_ref[0, 2:3, :] = dgn

    row = lambda w: pl.BlockSpec((tm, w), lambda i: (i, 0))
    full = lambda a: pl.BlockSpec(a.shape, lambda i: (0,) * a.ndim)
    nt = t // tm
    return _pcall(
        body, name="in_proj_bwd", grid=(nt,),
        out_shape=[jax.ShapeDtypeStruct((t, d), F32), jax.ShapeDtypeStruct((t, 6 * dg), BF16),
                   jax.ShapeDtypeStruct((t, LANES), BF16), jax.ShapeDtypeStruct((nt, 8, d), F32)],
        in_specs=[row(dg)] * 6 + [row(LANES), full(w_qkv), full(w_f), row(d), row(d), full(mod8), full(g_attn)],
        out_specs=[row(d), row(6 * dg), row(LANES), pl.BlockSpec((1, 8, d), lambda i: (i, 0, 0))],
        compiler_params=_params(("arbitrary",)),
    )(*dparts, dfl, w_qkv, w_f, x, dx2, mod8, g_attn)


def _matmul_tn(a, b, name):
    t, m = a.shape
    n = b.shape[1]
    a_t = a.T
    tm_ = _tile(m, (512, 256, 128))
    tn_ = _tile(n, (1024, 512, 256, 128))
    tk = _tile(t, (2048, 1024, 512, 256, 128))
    nk = t // tk

    def body(a_ref, b_ref, o_ref):
        k = pl.program_id(2)

        @pl.when(k == 0)
        def _():
            o_ref[...] = jnp.zeros_like(o_ref)

        o_ref[...] += jnp.dot(a_ref[...], b_ref[...], preferred_element_type=F32)

    return _pcall(
        body, name=name, grid=(m // tm_, n // tn_, nk),
        out_shape=jax.ShapeDtypeStruct((m, n), F32),
        in_specs=[pl.BlockSpec((tm_, tk), lambda i, j, k: (i, k)), pl.BlockSpec((tk, tn_), lambda i, j, k: (k, j))],
        out_specs=pl.BlockSpec((tm_, tn_), lambda i, j, k: (i, j)),
        compiler_params=_params(("arbitrary", "arbitrary", "arbitrary")),
    )(a_t, b)


HALO = 16


def _conv_taps(up_ext, cw, lo, rows):
    s1 = pltpu.roll(up_ext, 1, 0)
    s2 = pltpu.roll(up_ext, 2, 0)
    u = cw[2:3, :] * up_ext[lo:lo + rows] + cw[1:2, :] * s1[lo:lo + rows] + cw[0:1, :] * s2[lo:lo + rows] + cw[3:4, :]
    return u, s1, s2


def _ffn_fwd(x2, target, mod8, g_mlp, g_final, wg, wv, cwg, cwv, wd, tm, cf):
    t, d = x2.shape
    dfp = wg.shape[1]
    nt, nc = t // tm, dfp // cf
    hb = tm // HALO

    def body(x_ref, xp_ref, tg_ref, mod_ref, g_ref, gf_ref, wg_ref, wv_ref, cg_ref, cv_ref, wd_ref,
             dx3_ref, h2_ref, part_ref, h_sc, acc_sc):
        i, j = pl.program_id(0), pl.program_id(1)

        @pl.when(j == 0)
        def _():
            xe = jnp.concatenate([xp_ref[...], x_ref[...]], axis=0)
            h = _norm_mod(xe, g_ref[...], mod_ref[4:5, :], mod_ref[3:4, :]).astype(BF16)
            h_sc[...] = h
            h2_ref[...] = h[HALO:]
            acc_sc[...] = jnp.zeros_like(acc_sc)

        rowi = lax.broadcasted_iota(jnp.int32, (tm + HALO, 1), 0)
        keep = (rowi >= HALO) | (i > 0)
        hv = h_sc[...]
        upg = jnp.where(keep, jnp.dot(hv, wg_ref[...], preferred_element_type=F32), 0.0)
        upv = jnp.where(keep, jnp.dot(hv, wv_ref[...], preferred_element_type=F32), 0.0)
        ug, _, _ = _conv_taps(upg, cg_ref[...], HALO, tm)
        uv, _, _ = _conv_taps(upv, cv_ref[...], HALO, tm)
        act = (ug * jax.nn.sigmoid(ug) * uv).astype(BF16)
        acc_sc[...] += jnp.dot(act, wd_ref[...], preferred_element_type=F32)

        @pl.when(j == nc - 1)
        def _():
            y_ffn = acc_sc[...]
            x3 = x_ref[...] + mod_ref[5:6, :] * y_ffn
            r3 = lax.rsqrt(jnp.mean(x3 * x3, axis=-1, keepdims=True) + EPS)
            xn = x3 * r3
            gf = gf_ref[...]
            diff = xn * gf - tg_ref[...]
            dy = diff * (1.0 / d)
            dxn = dy * gf
            dx3 = r3 * (dxn - xn * jnp.mean(dxn * xn, axis=-1, keepdims=True))
            dx3_ref[...] = dx3
            part_ref[0] = jnp.zeros((8, d), F32)
            part_ref[0, 0:1, :] = jnp.sum(dy * xn, axis=0, keepdims=True)
            part_ref[0, 1:2, :] = jnp.sum(dx3 * y_ffn, axis=0, keepdims=True)
            part_ref[0, 2:3, :] = jnp.sum(diff * diff, axis=0, keepdims=True) * (0.5 / d)

    row = lambda w: pl.BlockSpec((tm, w), lambda i, j: (i, 0))
    full = lambda a: pl.BlockSpec(a.shape, lambda i, j: (0,) * a.ndim)
    return _pcall(
        body, name="ffn_fwd", grid=(nt, nc),
        out_shape=[jax.ShapeDtypeStruct((t, d), F32), jax.ShapeDtypeStruct((t, d), BF16),
                   jax.ShapeDtypeStruct((nt, 8, d), F32)],
        in_specs=[row(d), pl.BlockSpec((HALO, d), lambda i, j: (jnp.maximum(i * hb - 1, 0), 0)), row(d),
                  full(mod8), full(g_mlp), full(g_final),
                  pl.BlockSpec((d, cf), lambda i, j: (0, j)), pl.BlockSpec((d, cf), lambda i, j: (0, j)),
                  pl.BlockSpec((8, cf), lambda i, j: (0, j)), pl.BlockSpec((8, cf), lambda i, j: (0, j)),
                  pl.BlockSpec((cf, d), lambda i, j: (j, 0))],
        out_specs=[row(d), row(d), pl.BlockSpec((1, 8, d), lambda i, j: (i, 0, 0))],
        scratch_shapes=[pltpu.VMEM((tm + HALO, d), BF16), pltpu.VMEM((tm, d), F32)],
        compiler_params=_params(("arbitrary", "arbitrary")),
    )(x2, x2, target, mod8, g_mlp, g_final, wg, wv, cwg, cwv, wd)


def _ffn_bwd(x2, dx3, mod8, g_mlp, wg, wv, cwg, cwv, wd, tm, cf):
    t, d = x2.shape
    dfp = wg.shape[1]
    nt, nc = t // tm, dfp // cf
    hb = tm // HALO
    nhb = t // HALO
    ext = tm + 2 * HALO

    def body(x_ref, xp_ref, xn_ref, dx_ref, dxn_ref, mod_ref, g_ref, wg_ref, wv_ref, cg_ref, cv_ref, wd_ref,
             dx2_ref, dug_ref, duv_ref, act_ref, dxg_ref, part_ref, pcg_ref, pcv_ref, h_sc, dg_sc, dh_sc):
        i, j = pl.program_id(0), pl.program_id(1)

        @pl.when(j == 0)
        def _():
            xe = jnp.concatenate([xp_ref[...], x_ref[...], xn_ref[...]], axis=0)
            h_sc[...] = _norm_mod(xe, g_ref[...], mod_ref[4:5, :], mod_ref[3:4, :]).astype(BF16)
            de = (jnp.concatenate([dx_ref[...], dxn_ref[...]], axis=0) * mod_ref[5:6, :]).astype(BF16)
            dg_sc[...] = de
            dxg_ref[...] = de[:tm]
            dh_sc[...] = jnp.zeros_like(dh_sc)

        rowe = lax.broadcasted_iota(jnp.int32, (ext, 1), 0)
        keep_up = (rowe >= HALO) | (i > 0)
        rowu = lax.broadcasted_iota(jnp.int32, (tm + HALO, 1), 0)
        keep_du = (rowu < tm) | (i < nt - 1)
        hv = h_sc[...]
        upg = jnp.where(keep_up, jnp.dot(hv, wg_ref[...], preferred_element_type=F32), 0.0)
        upv = jnp.where(keep_up, jnp.dot(hv, wv_ref[...], preferred_element_type=F32), 0.0)
        cg, cv = cg_ref[...], cv_ref[...]
        ug, g1, g2 = _conv_taps(upg, cg, HALO, tm + HALO)
        uv, v1, v2 = _conv_taps(upv, cv, HALO, tm + HALO)
        dact = lax.dot_general(dg_sc[...], wd_ref[...], NT_DIMS, preferred_element_type=F32)
        sg = jax.nn.sigmoid(ug)
        sil = ug * sg
        act_ref[...] = (sil * uv)[:tm].astype(BF16)
        duv = jnp.where(keep_du, dact * sil, 0.0)
        dug = jnp.where(keep_du, dact * uv * (sg * (1.0 + ug * (1.0 - sg))), 0.0)

        def back(du, cw, up, s1, s2, pc_ref):
            n = tm + HALO
            dup = (cw[2:3, :] * du + cw[1:2, :] * pltpu.roll(du, n - 1, 0) + cw[0:1, :] * pltpu.roll(du, n - 2, 0))[:tm]
            dut = du[:tm]
            pc_ref[0] = jnp.zeros((8, cf), F32)
            pc_ref[0, 0:1, :] = jnp.sum(dut * s2[HALO:HALO + tm], axis=0, keepdims=True)
            pc_ref[0, 1:2, :] = jnp.sum(dut * s1[HALO:HALO + tm], axis=0, keepdims=True)
            pc_ref[0, 2:3, :] = jnp.sum(dut * up[HALO:HALO + tm], axis=0, keepdims=True)
            pc_ref[0, 3:4, :] = jnp.sum(dut, axis=0, keepdims=True)
            return dup.astype(BF16)

        dupg = back(dug, cg, upg, g1, g2, pcg_ref)
        dupv = back(duv, cv, upv, v1, v2, pcv_ref)
        dug_ref[...] = dupg
        duv_ref[...] = dupv
        dh_sc[...] += (lax.dot_general(dupg, wg_ref[...], NT_DIMS, preferred_element_type=F32)
                       + lax.dot_general(dupv, wv_ref[...], NT_DIMS, preferred_element_type=F32))

        @pl.when(j == nc - 1)
        def _():
            dx, dshift, dscale, dgn = _norm_mod_bwd(x_ref[...], dh_sc[...], g_ref[...], mod_ref[4:5, :])
            dx2_ref[...] = dx_ref[...] + dx
            part_ref[0] = jnp.zeros((8, d), F32)
            part_ref[0, 0:1, :] = dshift
            part_ref[0, 1:2, :] = dscale
            part_ref[0, 2:3, :] = dgn

    row = lambda w: pl.BlockSpec((tm, w), lambda i, j: (i, 0))
    prev = pl.BlockSpec((HALO, d), lambda i, j: (jnp.maximum(i * hb - 1, 0), 0))
    nxt = pl.BlockSpec((HALO, d), lambda i, j: (jnp.minimum((i + 1) * hb, nhb - 1), 0))
    full = lambda a: pl.BlockSpec(a.shape, lambda i, j: (0,) * a.ndim)
    chunk = pl.BlockSpec((tm, cf), lambda i, j: (i, j))
    pchunk = pl.BlockSpec((1, 8, cf), lambda i, j: (i, 0, j))
    return _pcall(
        body, name="ffn_bwd", grid=(nt, nc),
        out_shape=[jax.ShapeDtypeStruct((t, d), F32), jax.ShapeDtypeStruct((t, dfp), BF16),
                   jax.ShapeDtypeStruct((t, dfp), BF16), jax.ShapeDtypeStruct((t, dfp), BF16),
                   jax.ShapeDtypeStruct((t, d), BF16), jax.ShapeDtypeStruct((nt, 8, d), F32),
                   jax.ShapeDtypeStruct((nt, 8, dfp), F32), jax.ShapeDtypeStruct((nt, 8, dfp), F32)],
        in_specs=[row(d), prev, nxt, row(d), nxt, full(mod8), full(g_mlp),
                  pl.BlockSpec((d, cf), lambda i, j: (0, j)), pl.BlockSpec((d, cf), lambda i, j: (0, j)),
                  pl.BlockSpec((8, cf), lambda i, j: (0, j)), pl.BlockSpec((8, cf), lambda i, j: (0, j)),
                  pl.BlockSpec((cf, d), lambda i, j: (j, 0))],
        out_specs=[row(d), chunk, chunk, chunk, row(d), pl.BlockSpec((1, 8, d), lambda i, j: (i, 0, 0)), pchunk, pchunk],
        scratch_shapes=[pltpu.VMEM((ext, d), BF16), pltpu.VMEM((tm + HALO, d), BF16), pltpu.VMEM((tm, d), F32)],
        compiler_params=_params(("arbitrary", "arbitrary")),
    )(x2, x2, x2, dx3, dx3, mod8, g_mlp, wg, wv, cwg, cwv, wd)


def _head_masks():
    lane = lax.broadcasted_iota(jnp.int32, (1, LANES), 1)
    in_a = lane < HEAD_DIM
    return in_a, jnp.logical_not(in_a)


BLK = 2 * LANES


def _stack_heads(qkv, dg):
    t = qkv.shape[0]
    p = dg // LANES
    rows = _tile(t, (512, 256, 128))
    sub = rows // BLK

    def body(kf_ref, vf_ref, ks_ref, vs_ref, okf, ovf, oks, ovs):
        in_a, in_b = _head_masks()
        for src, dst in ((kf_ref, okf), (vf_ref, ovf), (ks_ref, oks), (vs_ref, ovs)):
            v = src[...]
            zero = jnp.zeros_like(v)
            va, vb = jnp.where(in_a, v, zero), jnp.where(in_b, v, zero)
            for s in range(sub):
                dst[0, s, :BLK, :] = va[s * BLK:(s + 1) * BLK]
                dst[0, s, BLK:, :] = vb[s * BLK:(s + 1) * BLK]

    col = lambda base: pl.BlockSpec((rows, LANES), lambda h, j: (j, base * p + h))
    out = pl.BlockSpec((1, sub, 2 * BLK, LANES), lambda h, j: (h, j, 0, 0))
    shape = jax.ShapeDtypeStruct((p, t // BLK, 2 * BLK, LANES), BF16)
    return _pcall(
        body, name="stack_heads", grid=(p, t // rows),
        out_shape=[shape] * 4, in_specs=[col(1), col(2), col(4), col(5)], out_specs=[out] * 4,
        compiler_params=_params(("arbitrary", "arbitrary")),
    )(qkv, qkv, qkv, qkv)


def _tile_masks():
    rowi = lax.broadcasted_iota(jnp.int32, (BLK, BLK), 0)
    coli = lax.broadcasted_iota(jnp.int32, (BLK, BLK), 1)
    return coli <= rowi, coli < rowi


def _pair_triangle(suffix):
    r = lax.broadcasted_iota(jnp.int32, (BLK, BLK), 0)
    c = lax.broadcasted_iota(jnp.int32, (BLK, BLK), 1)
    return ((r >= c) if suffix else (r <= c)).astype(BF16)


def _pair_cumsum(x2, tri, passes):
    return jnp.concatenate([_split_dot(x2[:, :BLK], tri, passes), _split_dot(x2[:, BLK:], tri, passes)], axis=1)


def _pair_specs(t, dg, base):
    p = dg // LANES
    q = pl.BlockSpec((BLK, LANES), lambda h, i: (i, base * p + h))
    kv = pl.BlockSpec((1, t // BLK, 2 * BLK, LANES), lambda h, i: (h, 0, 0, 0))
    return q, kv


def _fox_fwd(qkv, kst, vst, fcol, frow2, dg):
    t = qkv.shape[0]
    p, nq = dg // LANES, t // BLK
    nh = 2 * p

    def body(q_ref, k_ref, v_ref, ft_ref, fs_ref, o_ref, lse_ref):
        i = pl.program_id(1)
        in_a, _ = _head_masks()
        causal, _ = _tile_masks()
        q2 = q_ref[...]
        ft = tuple(jnp.broadcast_to(ft_ref[a], (BLK, BLK)) for a in range(2))

        def tile(j, carry, masked):
            m, l, acc = carry
            kb, vb = k_ref[0, j], v_ref[0, j]
            s2 = lax.dot_general(q2, kb, NT_DIMS, preferred_element_type=F32)
            fs = fs_ref[0, j]
            m_new, l_new, alpha, pr = [], [], [], []
            for a in range(2):
                sl = slice(a * BLK, (a + 1) * BLK)
                s = (s2[:, sl] + ft[a]) - fs[:, sl]
                if masked:
                    s = jnp.where(causal, s, NEG)
                mn = jnp.maximum(m[a], jnp.max(s, axis=1, keepdims=True))
                pa = jnp.exp(s - mn)
                al = jnp.exp(m[a] - mn)
                m_new.append(mn)
                alpha.append(al)
                l_new.append(al * l[a] + jnp.sum(pa, axis=1, keepdims=True))
                pr.append(pa.astype(BF16))
            acc = jnp.where(in_a, alpha[0], alpha[1]) * acc + jnp.dot(
                jnp.concatenate(pr, axis=1), vb, preferred_element_type=F32)
            return tuple(m_new), tuple(l_new), acc

        neg, zero = jnp.full((BLK, 1), NEG, F32), jnp.zeros((BLK, 1), F32)
        carry = lax.fori_loop(0, i, functools.partial(tile, masked=False), ((neg, neg), (zero, zero), jnp.zeros((BLK, LANES), F32)))
        m, l, acc = tile(i, carry, True)
        o_ref[...] = acc / jnp.where(in_a, l[0], l[1])
        lse_ref[0] = m[0] + jnp.log(l[0])
        lse_ref[1] = m[1] + jnp.log(l[1])

    qs, kv = _pair_specs(t, dg, 0)
    col = pl.BlockSpec((2, BLK, 1), lambda h, i: (h, i, 0))
    return _pcall(
        body, name="fox_fwd", grid=(p, nq),
        out_shape=[jax.ShapeDtypeStruct((t, dg), F32), jax.ShapeDtypeStruct((nh, t, 1), F32)],
        in_specs=[qs, kv, kv, col, pl.BlockSpec((1, nq, 1, 2 * BLK), lambda h, i: (h, 0, 0, 0))],
        out_specs=[pl.BlockSpec((BLK, LANES), lambda h, i: (i, h)), col],
        compiler_params=_params(("arbitrary", "arbitrary")),
    )(qkv, kst, vst, fcol, frow2)


def _fold_heads(stacked, in_a):
    return jnp.where(in_a, stacked[:BLK], stacked[BLK:])


def _fox_bwd(qkv, kst, vst, do, fcol, frow2, lse, delta, dg):
    t = qkv.shape[0]
    p, nq = dg // LANES, t // BLK
    nh = 2 * p

    def body(q_ref, k_ref, v_ref, do_ref, ft_ref, fs_ref, lse_ref, dl_ref, dq_ref, dk_ref, dv_ref, dfs_ref, dft_ref):
        i = pl.program_id(1)

        @pl.when(i == 0)
        def _():
            dk_ref[...] = jnp.zeros_like(dk_ref)
            dv_ref[...] = jnp.zeros_like(dv_ref)
            dfs_ref[...] = jnp.zeros_like(dfs_ref)

        in_a, _ = _head_masks()
        causal, _ = _tile_masks()
        q2, do2 = q_ref[...], do_ref[...]
        ft = tuple(jnp.broadcast_to(ft_ref[a] - lse_ref[a], (BLK, BLK)) for a in range(2))
        dl = tuple(jnp.broadcast_to(dl_ref[a], (BLK, BLK)) for a in range(2))

        def tile(j, carry, masked):
            dq, dft = carry
            kb, vb = k_ref[0, j], v_ref[0, j]
            s2 = lax.dot_general(q2, kb, NT_DIMS, preferred_element_type=F32)
            dp2 = lax.dot_general(do2, vb, NT_DIMS, preferred_element_type=F32)
            fs = fs_ref[0, j]
            pr, ds, dft_new = [], [], []
            for a in range(2):
                sl = slice(a * BLK, (a + 1) * BLK)
                s = (s2[:, sl] + ft[a]) - fs[:, sl]
                if masked:
                    s = jnp.where(causal, s, NEG)
                pa = jnp.exp(s)
                dsa = pa * (dp2[:, sl] - dl[a])
                pr.append(pa.astype(BF16))
                ds.append(dsa)
                dft_new.append(dft[a] + jnp.sum(dsa, axis=1, keepdims=True))
            ds2 = jnp.concatenate(ds, axis=1)
            dsb = ds2.astype(BF16)
            off = pl.multiple_of(j * BLK, BLK)
            dk_ref[pl.ds(off, BLK), :] += _fold_heads(lax.dot_general(dsb, q2, TN_DIMS, preferred_element_type=F32), in_a)
            dv_ref[pl.ds(off, BLK), :] += _fold_heads(
                lax.dot_general(jnp.concatenate(pr, axis=1), do2, TN_DIMS, preferred_element_type=F32), in_a)
            dfs_ref[0, j] += -jnp.sum(ds2, axis=0, keepdims=True)
            return dq + jnp.dot(dsb, kb, preferred_element_type=F32), tuple(dft_new)

        zero = jnp.zeros((BLK, 1), F32)
        carry = lax.fori_loop(0, i, functools.partial(tile, masked=False), (jnp.zeros((BLK, LANES), F32), (zero, zero)))
        dq, dft = tile(i, carry, True)
        dq_ref[...] = dq
        dft_ref[0] = dft[0]
        dft_ref[1] = dft[1]

    qs, kv = _pair_specs(t, dg, 0)
    col = pl.BlockSpec((2, BLK, 1), lambda h, i: (h, i, 0))
    rowspec = pl.BlockSpec((1, nq, 1, 2 * BLK), lambda h, i: (h, 0, 0, 0))
    blk = pl.BlockSpec((BLK, LANES), lambda h, i: (i, h))
    acc = pl.BlockSpec((t, LANES), lambda h, i: (0, h))
    return _pcall(
        body, name="fox_bwd", grid=(p, nq),
        out_shape=[jax.ShapeDtypeStruct((t, dg), F32)] * 3 + [jax.ShapeDtypeStruct((p, nq, 1, 2 * BLK), F32),
                                                              jax.ShapeDtypeStruct((nh, t, 1), F32)],
        in_specs=[qs, kv, kv, blk, col, rowspec, col, col],
        out_specs=[blk, acc, acc, rowspec, col],
        compiler_params=_params(("arbitrary", "arbitrary")),
    )(qkv, kst, vst, do, fcol, frow2, lse, delta)


def _softplus_parts(z):
    e = jnp.exp(-jnp.abs(z))
    return jnp.maximum(z, 0.0) + jnp.log(1.0 + e), e


def _sigmoid_from(z, e):
    d = 1.0 + e
    r = pl.reciprocal(d, approx=True)
    r = r * (2.0 - d * r)
    return jnp.where(z >= 0, 1.0, e) * r


def _sb_fwd(qkv, kst, vst, dg):
    t = qkv.shape[0]
    p, nq = dg // LANES, t // BLK
    nh = 2 * p

    def body(q_ref, k_ref, v_ref, o_ref, rt_ref):
        i = pl.program_id(1)
        _, strict = _tile_masks()
        strict2 = jnp.concatenate([strict, strict], axis=1)
        suffix = _pair_triangle(True)
        q2 = q_ref[...]

        def tile(j, carry, masked):
            rest, acc = carry
            kb, vb = k_ref[0, j], v_ref[0, j]
            z = lax.dot_general(q2, kb, NT_DIMS, preferred_element_type=F32)
            sp, _ = _softplus_parts(z)
            if masked:
                sp = jnp.where(strict2, sp, 0.0)
            cs = _pair_cumsum(sp, suffix, 2)
            w, rest_new = [], []
            for a in range(2):
                sl = slice(a * BLK, (a + 1) * BLK)
                wa = jnp.exp(z[:, sl] - cs[:, sl] - rest[a])
                if masked:
                    wa = jnp.where(strict, wa, 0.0)
                w.append(wa.astype(BF16))
                rest_new.append(rest[a] + cs[:, a * BLK:a * BLK + 1])
            acc = acc + jnp.dot(jnp.concatenate(w, axis=1), vb, preferred_element_type=F32)
            return tuple(rest_new), acc

        zero = jnp.zeros((BLK, 1), F32)
        carry = tile(i, ((zero, zero), jnp.zeros((BLK, LANES), F32)), True)
        rest, acc = lax.fori_loop(0, i, lambda jj, c: tile(i - 1 - jj, c, False), carry)
        o_ref[...] = acc
        rt_ref[0] = rest[0]
        rt_ref[1] = rest[1]

    qs, kv = _pair_specs(t, dg, 3)
    col = pl.BlockSpec((2, BLK, 1), lambda h, i: (h, i, 0))
    return _pcall(
        body, name="sb_fwd", grid=(p, nq),
        out_shape=[jax.ShapeDtypeStruct((t, dg), F32), jax.ShapeDtypeStruct((nh, t, 1), F32)],
        in_specs=[qs, kv, kv],
        out_specs=[pl.BlockSpec((BLK, LANES), lambda h, i: (i, h)), col],
        compiler_params=_params(("arbitrary", "arbitrary")),
    )(qkv, kst, vst)


def _sb_bwd(qkv, kst, vst, do, rtot, dg):
    t = qkv.shape[0]
    p, nq = dg // LANES, t // BLK

    def body(q_ref, k_ref, v_ref, do_ref, rt_ref, dq_ref, dk_ref, dv_ref):
        i = pl.program_id(1)

        @pl.when(i == 0)
        def _():
            dk_ref[...] = jnp.zeros_like(dk_ref)
            dv_ref[...] = jnp.zeros_like(dv_ref)

        in_a, _ = _head_masks()
        _, strict = _tile_masks()
        strict2 = jnp.concatenate([strict, strict], axis=1)
        prefix = _pair_triangle(False)
        q2, do2 = q_ref[...], do_ref[...]
        rt = (rt_ref[0], rt_ref[1])

        def tile(j, carry, masked):
            before, gbefore, dq = carry
            kb, vb = k_ref[0, j], v_ref[0, j]
            z = lax.dot_general(q2, kb, NT_DIMS, preferred_element_type=F32)
            da = lax.dot_general(do2, vb, NT_DIMS, preferred_element_type=F32)
            sp, e = _softplus_parts(z)
            sig = _sigmoid_from(z, e)
            if masked:
                sp = jnp.where(strict2, sp, 0.0)
            pre = _pair_cumsum(sp, prefix, 2)
            w = []
            for a in range(2):
                sl = slice(a * BLK, (a + 1) * BLK)
                wa = jnp.exp(z[:, sl] + (before[a] - rt[a]) + pre[:, sl] - sp[:, sl])
                if masked:
                    wa = jnp.where(strict, wa, 0.0)
                w.append(wa)
            w2 = jnp.concatenate(w, axis=1)
            g = w2 * da
            preg = _pair_cumsum(g, prefix, 1)
            dz = []
            for a in range(2):
                sl = slice(a * BLK, (a + 1) * BLK)
                dza = g[:, sl] * (1.0 - sig[:, sl]) - sig[:, sl] * (gbefore[a] + preg[:, sl] - g[:, sl])
                if masked:
                    dza = jnp.where(strict, dza, 0.0)
                dz.append(dza.astype(BF16))
            dzb = jnp.concatenate(dz, axis=1)
            off = pl.multiple_of(j * BLK, BLK)
            dk_ref[pl.ds(off, BLK), :] += _fold_heads(lax.dot_general(dzb, q2, TN_DIMS, preferred_element_type=F32), in_a)
            dv_ref[pl.ds(off, BLK), :] += _fold_heads(
                lax.dot_general(w2.astype(BF16), do2, TN_DIMS, preferred_element_type=F32), in_a)
            last = lambda x, a: x[:, (a + 1) * BLK - 1:(a + 1) * BLK]
            return (tuple(before[a] + last(pre, a) for a in range(2)),
                    tuple(gbefore[a] + last(preg, a) for a in range(2)),
                    dq + jnp.dot(dzb, kb, preferred_element_type=F32))

        zero = jnp.zeros((BLK, 1), F32)
        carry = lax.fori_loop(0, i, functools.partial(tile, masked=False), ((zero, zero), (zero, zero), jnp.zeros((BLK, LANES), F32)))
        dq_ref[...] = tile(i, carry, True)[2]

    qs, kv = _pair_specs(t, dg, 3)
    col = pl.BlockSpec((2, BLK, 1), lambda h, i: (h, i, 0))
    blk = pl.BlockSpec((BLK, LANES), lambda h, i: (i, h))
    acc = pl.BlockSpec((t, LANES), lambda h, i: (0, h))
    return _pcall(
        body, name="sb_bwd", grid=(p, nq),
        out_shape=[jax.ShapeDtypeStruct((t, dg), F32)] * 3,
        in_specs=[qs, kv, kv, blk, col],
        out_specs=[blk, acc, acc],
        compiler_params=_params(("arbitrary", "arbitrary")),
    )(qkv, kst, vst, do, rtot)


def _tri_constants(nh, t):
    nb = t // LANES
    r = nh * nb
    li = np.arange(LANES)
    tri_in = (li[:, None] <= li[None, :])
    ri = np.arange(r)
    same = (ri[:, None] // nb) == (ri[None, :] // nb)
    blk = same & (ri[None, :] < ri[:, None])
    blk_rev = same & (ri[None, :] > ri[:, None])
    head_rows = (np.arange(max(8, nh))[:, None] == (ri[None, :] // nb))
    as_bf16 = lambda a: jnp.asarray(a.astype(np.float32), BF16)
    return as_bf16(tri_in), as_bf16(blk), as_bf16(tri_in.T), as_bf16(blk_rev), as_bf16(head_rows)


def kernel(x, c, w_ada, b_ada, g_attn, w_in, b_fgate, g_out_fox, g_out_sb, w_out, g_mlp, w_up, conv_w, conv_b, w_down, g_final, loss_target, m_w_ada, m_b_ada, m_g_attn, m_w_in, m_b_fgate, m_g_out_fox, m_g_out_sb, m_w_out, m_g_mlp, m_w_up, m_conv_w, m_conv_b, m_w_down, m_g_final, v_w_ada, v_b_ada, v_g_attn, v_w_in, v_b_fgate, v_g_out_fox, v_g_out_sb, v_w_out, v_g_mlp, v_w_up, v_conv_w, v_conv_b, v_w_down, v_g_final):
    t, d = x.shape[1], x.shape[2]
    dg = d // 2
    nh = dg // HEAD_DIM
    n_in = 6 * dg + nh
    dff = w_down.shape[1] * 4
    dfp = -(-dff // 256) * 256
    cf = 256
    tm = _tile(t, (512, 256, 128))
    nq = t // BLK
    xi, yi, ci = lax.axis_index("x"), lax.axis_index("y"), lax.axis_index("c")
    shard = 2 * xi + yi
    me = 4 * xi + 2 * yi + ci

    x2d, tg2d = x[0], loss_target[0]

    c_all = _all_gather8(jnp.pad(c, ((0, 7), (0, 0)))).reshape(8, 8, d)[:, 0, :]
    ada_cols = w_ada.shape[2]
    b_shard = lax.dynamic_slice(b_ada, (0, shard * ada_cols), (1, ada_cols))
    sc_all, mod_shard = _ada_fwd(c_all, w_ada[0], b_shard)
    mod_all = _all_gather8(mod_shard).reshape(4, 2, 8, ada_cols)
    mod_me = lax.dynamic_index_in_dim(mod_all[:, 0], me, axis=1, keepdims=False)
    mod8 = jnp.pad(mod_me.reshape(6, d), ((0, 2), (0, 0)))

    g_in, g_out, g_up, g_down, g_cw = _gather_xy(
        [w_in[0].astype(BF16), w_out[0].astype(BF16), w_up[0].astype(BF16), w_down[0].astype(BF16), conv_w[0]])
    w_in_full = jnp.transpose(g_in, (1, 0, 2)).reshape(d, n_in)
    w_qkv = w_in_full[:, :6 * dg]
    w_f = jnp.pad(w_in_full[:, 6 * dg:], ((0, 0), (0, LANES - nh)))
    w_out_full = g_out.reshape(2 * dg, d)
    w_up_full = jnp.transpose(g_up, (1, 0, 2)).reshape(d, 2 * dff)
    padc = ((0, 0), (0, dfp - dff))
    wg, wv = jnp.pad(w_up_full[:, :dff], padc), jnp.pad(w_up_full[:, dff:], padc)
    wd = jnp.pad(g_down.reshape(dff, d), ((0, dfp - dff), (0, 0)))
    cw_full = jnp.transpose(g_cw, (1, 0, 2)).reshape(3, 2 * dff)
    cw4 = jnp.concatenate([cw_full, conv_b], axis=0)
    cwg = jnp.pad(cw4[:, :dff], ((0, 4), (0, dfp - dff)))
    cwv = jnp.pad(cw4[:, dff:], ((0, 4), (0, dfp - dff)))

    qkv, fl, h1 = _in_proj_fwd(x2d, mod8, g_attn, w_qkv, w_f, tm)
    tri_in, tri_blk, tri_in_rev, tri_blk_rev, head_rows = _tri_constants(nh, t)
    fl2d = fl[:, :nh].T.reshape(nh * t // LANES, LANES)
    b_rows = jnp.repeat(b_fgate[0], t // LANES)[:, None]
    f2d = _fgate_fwd(fl2d, b_rows, tri_in, tri_blk)
    fcol = f2d.reshape(nh, t, 1)
    pairs = nh // 2
    frow2 = jnp.transpose(f2d.reshape(pairs, 2, nq, BLK), (0, 2, 1, 3)).reshape(pairs, nq, 1, 2 * BLK)
    k_fox, v_fox, k_sb, v_sb = _stack_heads(qkv, dg)
    o_fox, lse = _fox_fwd(qkv, k_fox, v_fox, fcol, frow2, dg)
    o_sb, rtot = _sb_fwd(qkv, k_sb, v_sb, dg)
    li = np.arange(dg)
    bd = jnp.asarray((li[:, None] // HEAD_DIM == li[None, :] // HEAD_DIM).astype(np.float32), BF16)
    hsel = jnp.asarray((li[:, None] // HEAD_DIM == np.arange(LANES)[None, :]).astype(np.float32), BF16)
    x2, mix = _attn_out_fwd(x2d, o_fox, o_sb, g_out_fox, g_out_sb, w_out_full, mod8, bd, tm)
    g_final2 = g_final[None, :]
    dx3, h2, part_f = _ffn_fwd(x2, tg2d, mod8, g_mlp, g_final2, wg, wv, cwg, cwv, wd, tm, cf)

    dx2, dupg, dupv, act, dxg3, part_b, pcg, pcv = _ffn_bwd(x2, dx3, mod8, g_mlp, wg, wv, cwg, cwv, wd, tm, cf)
    do_fox, do_sb, delta, dxg2, part_o = _attn_out_bwd(dx2, mix, o_fox, o_sb, g_out_fox, g_out_sb, w_out_full, mod8, bd, hsel, tm)
    dcol = delta[:, :nh].T.reshape(nh, t, 1)
    dq_f, dk_f, dv_f, dfs2, dft = _fox_bwd(qkv, k_fox, v_fox, do_fox, fcol, frow2, lse, dcol, dg)
    dq_s, dk_s, dv_s = _sb_bwd(qkv, k_sb, v_sb, do_sb, rtot, dg)
    f2d_shape = (nh * t // LANES, LANES)
    dfs = jnp.transpose(dfs2.reshape(pairs, nq, 2, BLK), (0, 2, 1, 3))
    dfl2d, gb8 = _fgate_bwd(fl2d, b_rows, dft.reshape(f2d_shape), dfs.reshape(f2d_shape), tri_in_rev, tri_blk_rev, head_rows)
    dfl = jnp.pad(dfl2d.reshape(nh, t).T, ((0, 0), (0, LANES - nh)))
    grad_x, dproj, dflb, part_i = _in_proj_bwd([dq_f, dk_f, dv_f, dq_s, dk_s, dv_s], dfl, w_qkv, w_f, x2d, dx2, mod8, g_attn, tm)

    gw_qkv = _matmul_tn(h1, dproj, "grad_w_qkv")
    gw_f = _matmul_tn(h1, dflb, "grad_w_f")
    gw_in = jnp.concatenate([gw_qkv, gw_f[:, :nh]], axis=1)
    gw_out = _matmul_tn(mix, dxg2, "grad_w_out")
    gw_upg = _matmul_tn(h2, dupg, "grad_w_up_gate")
    gw_upv = _matmul_tn(h2, dupv, "grad_w_up_val")
    gw_up = jnp.concatenate([gw_upg[:, :dff], gw_upv[:, :dff]], axis=1)
    gw_down = _matmul_tn(act, dxg3, "grad_w_down")[:dff]

    sf = _sum_leading(part_f, "sum_part_ffn_fwd")
    sb_ = _sum_leading(part_b, "sum_part_ffn_bwd")
    so = _sum_leading(part_o, "sum_part_attn_out")
    si = _sum_leading(part_i, "sum_part_in_proj")
    scg = _sum_leading(pcg, "sum_part_conv_gate")
    scv = _sum_leading(pcv, "sum_part_conv_val")
    gb_f = gb8[:nh, 0]
    dmod = jnp.concatenate([si[0], si[1], so[0], sb_[0], sb_[1], sf[1]])
    g_conv_w = jnp.concatenate([scg[0:3, :dff], scv[0:3, :dff]], axis=1).reshape(-1)
    g_conv_b = jnp.concatenate([scg[3, :dff], scv[3, :dff]])
    loss_part = jnp.sum(sf[2])
    fields = [dmod, si[2], gb_f, so[1, :dg], so[1, dg:], sb_[2], g_conv_b, sf[0], g_conv_w, loss_part[None]]
    sizes = [int(f.shape[0]) for f in fields]
    n_pack = sum(sizes)
    lanes_pack = -(-n_pack // (8 * LANES)) * LANES
    pack = jnp.pad(jnp.concatenate(fields), (0, 8 * lanes_pack - n_pack)).reshape(8, lanes_pack)
    gathered = _all_gather8(pack)
    tot = _sum_leading(gathered.reshape(8, 8, lanes_pack), "sum_pack").reshape(-1)
    offs = np.concatenate([[0], np.cumsum(sizes)])
    take = lambda k: tot[int(offs[k]):int(offs[k + 1])]
    g_b_ada, g_g_attn, g_b_fgate, g_g_fox, g_g_sb, g_g_mlp, g_cb, g_g_final, g_cw_full, loss_v = [take(k) for k in range(10)]
    loss = loss_v[0]
    dmod_all = gathered.reshape(8, 8 * lanes_pack)[:, :6 * d]
    dmod_cols = lax.dynamic_slice(dmod_all, (0, shard * ada_cols), (8, ada_cols))
    g_w_ada = _ada_bwd(sc_all.T, dmod_cols)

    def col_pieces(g):
        r, cc = g.shape
        return jnp.transpose(g.reshape(2, r // 2, 4, cc // 4), (2, 0, 1, 3)).reshape(8, r // 2, cc // 4)

    def row_pieces(g):
        r, cc = g.shape
        return g.reshape(8, r // 8, cc)

    pieces = (col_pieces(gw_in), row_pieces(gw_out), col_pieces(gw_up), row_pieces(gw_down))
    recv = _scatter8([_to_bf16(p, "pieces_bf16_" + nm) for p, nm in zip(pieces, ("w_in", "w_out", "w_up", "w_down"))])
    halves = [_sum_leading(rv, nm) for rv, nm in zip(recv, ("sum_w_in", "sum_w_out", "sum_w_up", "sum_w_down"))]
    swapped = _swap_halves(halves)
    g_w_in, g_w_out, g_w_up, g_w_down = [s.reshape(2 * s.shape[1], s.shape[2]) for s in swapped]
    g_conv_w_shard = lax.dynamic_slice(g_cw_full.reshape(3, 2 * dff), (0, shard * (dff // 2)), (3, dff // 2))

    grads, deltas, new_m, new_v = {}, {}, {}, {}

    def step(name, w, g, m, v):
        shape = w.shape
        as2d = lambda a: a.reshape(-1, shape[-1])
        dl, nm, nv = _adamw(as2d(w), as2d(g), as2d(m), as2d(v), "adamw_" + name)
        grads[name], deltas[name], new_m[name], new_v[name] = g.reshape(shape), dl.reshape(shape), nm.reshape(shape), nv.reshape(shape)

    step("w_ada", w_ada, g_w_ada, m_w_ada, v_w_ada)
    step("w_in", w_in, g_w_in, m_w_in, v_w_in)
    step("w_out", w_out, g_w_out, m_w_out, v_w_out)
    step("w_up", w_up, g_w_up, m_w_up, v_w_up)
    step("conv_w", conv_w, g_conv_w_shard, m_conv_w, v_conv_w)
    step("w_down", w_down, g_w_down, m_w_down, v_w_down)

    small = [("b_ada", b_ada, g_b_ada, m_b_ada, v_b_ada), ("g_attn", g_attn, g_g_attn, m_g_attn, v_g_attn),
             ("b_fgate", b_fgate, g_b_fgate, m_b_fgate, v_b_fgate), ("g_out_fox", g_out_fox, g_g_fox, m_g_out_fox, v_g_out_fox),
             ("g_out_sb", g_out_sb, g_g_sb, m_g_out_sb, v_g_out_sb), ("g_mlp", g_mlp, g_g_mlp, m_g_mlp, v_g_mlp),
             ("conv_b", conv_b, g_cb, m_conv_b, v_conv_b), ("g_final", g_final, g_g_final, m_g_final, v_g_final)]
    ssz = [int(np.prod(s[1].shape)) for s in small]
    n_small = sum(ssz)
    lanes_small = -(-n_small // (8 * LANES)) * LANES
    packs = [jnp.pad(jnp.concatenate([s[k].reshape(-1) for s in small]), (0, 8 * lanes_small - n_small)).reshape(8, lanes_small)
             for k in (1, 2, 3, 4)]
    dl_s, nm_s, nv_s = _adamw(*packs, "adamw_small")
    so_ = np.concatenate([[0], np.cumsum(ssz)])
    for k, s in enumerate(small):
        cut = lambda a: a.reshape(-1)[int(so_[k]):int(so_[k + 1])].reshape(s[1].shape)
        grads[s[0]], deltas[s[0]], new_m[s[0]], new_v[s[0]] = s[2].reshape(s[1].shape), cut(dl_s), cut(nm_s), cut(nv_s)

    order = ["w_ada", "b_ada", "g_attn", "w_in", "b_fgate", "g_out_fox", "g_out_sb", "w_out", "g_mlp", "w_up",
             "conv_w", "conv_b", "w_down", "g_final"]
    return (loss, grad_x[None], *[grads[n] for n in order], *[deltas[n] for n in order],
            *[new_m[n] for n in order], *[new_v[n] for n in order])
```

```python
import functools

import numpy as np
import jax
import jax.numpy as jnp
from jax import lax
from jax.experimental import pallas as pl
from jax.experimental.pallas import tpu as pltpu

F32 = jnp.float32
BF16 = jnp.bfloat16
MESH = pl.DeviceIdType.MESH

HEAD_DIM = 64
LANES = 128
EPS = 1e-6
NEG = -1e30
ADAM_LR, ADAM_B1, ADAM_B2, ADAM_EPS, ADAM_WD, ADAM_STEP = 0.001, 0.9, 0.999, 1e-08, 0.01, 10
V7X_VMEM_BYTES = 64 * 1024 * 1024
VMEM_LIMIT = V7X_VMEM_BYTES - 12 * 1024 * 1024
NT_DIMS = (((1,), (1,)), ((), ()))
TN_DIMS = (((0,), (0,)), ((), ()))


def _pcall(body, **kw):
    return pl.pallas_call(body, **kw)


def _params(sem=None, **kw):
    return pltpu.CompilerParams(dimension_semantics=sem, vmem_limit_bytes=VMEM_LIMIT, **kw)


def _split_dot(x, m, passes):
    acc = None
    for _ in range(passes):
        part = x.astype(BF16)
        d = jnp.dot(part, m, preferred_element_type=F32)
        acc = d if acc is None else acc + d
        x = x - part.astype(F32)
    return acc


def _tile(n, candidates):
    for t in candidates:
        if n % t == 0:
            return t
    return n


def _rows_tile(rows, row_bytes, budget=2 * 1024 * 1024):
    best = None
    for t in range(8, rows + 1, 8):
        if rows % t == 0 and t * row_bytes <= budget:
            best = t
    return best if best is not None else rows


def _all_gather8(v):
    m_per, n = v.shape

    def body(x_ref, out_ref, send_sems, recv_sems, local_sem):
        x, y, c = lax.axis_index("x"), lax.axis_index("y"), lax.axis_index("c")
        me, sibling = (x, y, c), (x, y, 1 - c)
        chips = [(1 - x, y), (x, 1 - y), (1 - x, 1 - y)]

        def rows(px, py, pc):
            return out_ref.at[pl.ds((4 * px + 2 * py + pc) * m_per, m_per), :]

        def copy(k, block, to, src=None):
            return pltpu.make_async_remote_copy(
                src_ref=rows(*block) if src is None else src, dst_ref=rows(*block),
                send_sem=send_sems.at[k], recv_sem=recv_sems.at[k], device_id=to, device_id_type=MESH)

        mine = pltpu.make_async_copy(x_ref, rows(*me), local_sem)
        mine.start()
        first = [copy(0, me, sibling, src=x_ref)]
        first += [copy(1 + j, me, (*chip, c), src=x_ref) for j, chip in enumerate(chips)]
        for cp in first:
            cp.start()
        passed = [copy(4 + j, (*chip, c), sibling) for j, chip in enumerate(chips)]
        for j, chip in enumerate(chips):
            copy(1 + j, (*chip, c), me).wait_recv()
            passed[j].start()
        copy(0, sibling, me).wait_recv()
        for j, chip in enumerate(chips):
            copy(4 + j, (*chip, 1 - c), me).wait_recv()
        for cp in first + passed:
            cp.wait_send()
        mine.wait()

    return _pcall(
        body, name="all_gather8",
        out_shape=jax.ShapeDtypeStruct((8 * m_per, n), v.dtype),
        in_specs=[pl.BlockSpec(memory_space=pltpu.VMEM)],
        out_specs=pl.BlockSpec(memory_space=pltpu.VMEM),
        scratch_shapes=[pltpu.SemaphoreType.DMA((7,)), pltpu.SemaphoreType.DMA((7,)), pltpu.SemaphoreType.DMA],
        compiler_params=pltpu.CompilerParams(vmem_limit_bytes=VMEM_LIMIT),
    )(v)


def _gather_xy(shards):
    n = len(shards)

    def body(*refs):
        ins, outs = refs[:n], refs[n:2 * n]
        send_sems, recv_sems, local_sems = refs[2 * n:]
        x, y, c = lax.axis_index("x"), lax.axis_index("y"), lax.axis_index("c")
        chips = [(1 - x, y), (x, 1 - y), (1 - x, 1 - y)]
        mine = 2 * x + y
        local, remote = [], []
        for w in range(n):
            cp = pltpu.make_async_copy(ins[w], outs[w].at[mine], local_sems.at[w])
            cp.start()
            local.append(cp)
            for k, (px, py) in enumerate(chips):
                cp = pltpu.make_async_remote_copy(
                    src_ref=ins[w], dst_ref=outs[w].at[mine], send_sem=send_sems.at[3 * w + k],
                    recv_sem=recv_sems.at[3 * w + k], device_id=(px, py, c), device_id_type=MESH)
                cp.start()
                remote.append(cp)
        for cp in remote:
            cp.wait_recv()
        for cp in remote:
            cp.wait_send()
        for cp in local:
            cp.wait()

    hbm = pl.BlockSpec(memory_space=pltpu.HBM)
    return _pcall(
        body, name="gather_xy",
        out_shape=[jax.ShapeDtypeStruct((4,) + s.shape, s.dtype) for s in shards],
        in_specs=[hbm] * n, out_specs=[hbm] * n,
        scratch_shapes=[pltpu.SemaphoreType.DMA((3 * n,)), pltpu.SemaphoreType.DMA((3 * n,)),
                        pltpu.SemaphoreType.DMA((n,))],
        compiler_params=pltpu.CompilerParams(vmem_limit_bytes=VMEM_LIMIT),
    )(*shards)


def _scatter8(pieces):
    n = len(pieces)

    def body(*refs):
        ins, outs = refs[:n], refs[n:2 * n]
        send_sems, recv_sems, local_sems = refs[2 * n:]
        x, y, c = lax.axis_index("x"), lax.axis_index("y"), lax.axis_index("c")
        me = 4 * x + 2 * y + c
        local, remote = [], []
        for w in range(n):
            cp = pltpu.make_async_copy(ins[w].at[me], outs[w].at[me], local_sems.at[w])
            cp.start()
            local.append(cp)
            for f in range(1, 8):
                px = 1 - x if f & 4 else x
                py = 1 - y if f & 2 else y
                pc = 1 - c if f & 1 else c
                cp = pltpu.make_async_remote_copy(
                    src_ref=ins[w].at[4 * px + 2 * py + pc], dst_ref=outs[w].at[me],
                    send_sem=send_sems.at[7 * w + f - 1], recv_sem=recv_sems.at[7 * w + f - 1],
                    device_id=(px, py, pc), device_id_type=MESH)
                cp.start()
                remote.append(cp)
        for cp in remote:
            cp.wait_recv()
        for cp in remote:
            cp.wait_send()
        for cp in local:
            cp.wait()

    hbm = pl.BlockSpec(memory_space=pltpu.HBM)
    return _pcall(
        body, name="scatter8",
        out_shape=[jax.ShapeDtypeStruct(p.shape, p.dtype) for p in pieces],
        in_specs=[hbm] * n, out_specs=[hbm] * n,
        scratch_shapes=[pltpu.SemaphoreType.DMA((7 * n,)), pltpu.SemaphoreType.DMA((7 * n,)),
                        pltpu.SemaphoreType.DMA((n,))],
        compiler_params=pltpu.CompilerParams(vmem_limit_bytes=VMEM_LIMIT),
    )(*pieces)


def _swap_halves(halves):
    n = len(halves)

    def body(*refs):
        ins, outs = refs[:n], refs[n:2 * n]
        send_sems, recv_sems, local_sems = refs[2 * n:]
        x, y, c = lax.axis_index("x"), lax.axis_index("y"), lax.axis_index("c")
        local, remote = [], []
        for w in range(n):
            cp = pltpu.make_async_copy(ins[w], outs[w].at[c], local_sems.at[w])
            cp.start()
            local.append(cp)
            cp = pltpu.make_async_remote_copy(
                src_ref=ins[w], dst_ref=outs[w].at[c], send_sem=send_sems.at[w], recv_sem=recv_sems.at[w],
                device_id=(x, y, 1 - c), device_id_type=MESH)
            cp.start()
            remote.append(cp)
        for cp in remote:
            cp.wait_recv()
        for cp in remote:
            cp.wait_send()
        for cp in local:
            cp.wait()

    hbm = pl.BlockSpec(memory_space=pltpu.HBM)
    return _pcall(
        body, name="swap_halves",
        out_shape=[jax.ShapeDtypeStruct((2,) + h.shape, h.dtype) for h in halves],
        in_specs=[hbm] * n, out_specs=[hbm] * n,
        scratch_shapes=[pltpu.SemaphoreType.DMA((n,)), pltpu.SemaphoreType.DMA((n,)), pltpu.SemaphoreType.DMA((n,))],
        compiler_params=pltpu.CompilerParams(vmem_limit_bytes=VMEM_LIMIT),
    )(*halves)


def _sum_leading(a, name):
    n, r, c = a.shape
    tr = _rows_tile(r, n * c * 4, budget=6 * 1024 * 1024)
    if a.dtype == BF16 and tr % 16:
        tr = r

    def body(a_ref, o_ref):
        acc = a_ref[0].astype(F32)
        for k in range(1, n):
            acc = acc + a_ref[k].astype(F32)
        o_ref[...] = acc

    return _pcall(
        body, name=name, grid=(r // tr,),
        out_shape=jax.ShapeDtypeStruct((r, c), F32),
        in_specs=[pl.BlockSpec((n, tr, c), lambda i: (0, i, 0))],
        out_specs=pl.BlockSpec((tr, c), lambda i: (i, 0)),
        compiler_params=_params(("arbitrary",)),
    )(a)


def _to_bf16(a, name):
    n, r, c = a.shape

    def body(a_ref, o_ref):
        o_ref[...] = a_ref[...].astype(BF16)

    spec = pl.BlockSpec((1, r, c), lambda i: (i, 0, 0))
    return _pcall(
        body, name=name, grid=(n,), out_shape=jax.ShapeDtypeStruct(a.shape, BF16),
        in_specs=[spec], out_specs=spec, compiler_params=_params(("arbitrary",)),
    )(a)


def _adamw(w, g, m, v, name):
    r, c = w.shape
    tr = _rows_tile(r, c * 4, budget=1024 * 1024)
    c1 = 1.0 - ADAM_B1 ** ADAM_STEP
    c2 = 1.0 - ADAM_B2 ** ADAM_STEP

    def body(w_ref, g_ref, m_ref, v_ref, d_ref, nm_ref, nv_ref):
        gg = g_ref[...]
        nm = ADAM_B1 * m_ref[...] + (1.0 - ADAM_B1) * gg
        nv = ADAM_B2 * v_ref[...] + (1.0 - ADAM_B2) * (gg * gg)
        m_hat = nm / c1
        v_hat = nv / c2
        d_ref[...] = -ADAM_LR * (m_hat / (jnp.sqrt(v_hat) + ADAM_EPS) + ADAM_WD * w_ref[...])
        nm_ref[...] = nm
        nv_ref[...] = nv

    spec = pl.BlockSpec((tr, c), lambda i: (i, 0))
    return _pcall(
        body, name=name, grid=(r // tr,),
        out_shape=[jax.ShapeDtypeStruct((r, c), F32)] * 3,
        in_specs=[spec] * 4, out_specs=[spec] * 3,
        compiler_params=_params(("arbitrary",)),
    )(w, g, m, v)


def _ada_fwd(c_all, w_shard, b_shard):
    nb, d = c_all.shape
    cols = w_shard.shape[1]

    def body(c_ref, w_ref, b_ref, sc_ref, mod_ref):
        cv = c_ref[...]
        sc = cv * jax.nn.sigmoid(cv)
        sc_ref[...] = sc
        mod_ref[...] = jnp.dot(sc.astype(BF16), w_ref[...].astype(BF16), preferred_element_type=F32) + b_ref[...]

    return _pcall(
        body, name="ada_fwd",
        out_shape=[jax.ShapeDtypeStruct((nb, d), F32), jax.ShapeDtypeStruct((nb, cols), F32)],
        compiler_params=pltpu.CompilerParams(vmem_limit_bytes=VMEM_LIMIT),
    )(c_all, w_shard, b_shard)


def _ada_bwd(sc_t, dmod_cols):
    d, nb = sc_t.shape
    cols = dmod_cols.shape[1]
    tr = _rows_tile(d, cols * 4, budget=1024 * 1024)

    def body(s_ref, m_ref, o_ref):
        s = s_ref[...]
        m = m_ref[...]
        acc = s[:, 0:1] * m[0:1, :]
        for b in range(1, nb):
            acc = acc + s[:, b:b + 1] * m[b:b + 1, :]
        o_ref[...] = acc

    return _pcall(
        body, name="ada_bwd", grid=(d // tr,),
        out_shape=jax.ShapeDtypeStruct((d, cols), F32),
        in_specs=[pl.BlockSpec((tr, nb), lambda i: (i, 0)), pl.BlockSpec((nb, cols), lambda i: (0, 0))],
        out_specs=pl.BlockSpec((tr, cols), lambda i: (i, 0)),
        compiler_params=_params(("arbitrary",)),
    )(sc_t, dmod_cols)


def _log_sigmoid(x):
    return jnp.minimum(x, 0.0) - jnp.log1p(jnp.exp(-jnp.abs(x)))


def _fgate_fwd(fl2d, b_rows, tri_in, tri_blk):
    r = fl2d.shape[0]

    def body(x_ref, b_ref, u_ref, l_ref, f_ref):
        lf = _log_sigmoid(x_ref[...] + b_ref[...])
        c1 = _split_dot(lf, u_ref[...], 3)
        tot = jnp.broadcast_to(c1[:, LANES - 1:LANES], (r, LANES))
        acc = None
        for _ in range(3):
            part = tot.astype(BF16)
            dd = jnp.dot(l_ref[...], part, preferred_element_type=F32)
            acc = dd if acc is None else acc + dd
            tot = tot - part.astype(F32)
        f_ref[...] = c1 + acc

    return _pcall(
        body, name="fgate_fwd", out_shape=jax.ShapeDtypeStruct((r, LANES), F32),
        compiler_params=pltpu.CompilerParams(vmem_limit_bytes=VMEM_LIMIT),
    )(fl2d, b_rows, tri_in, tri_blk)


def _fgate_bwd(fl2d, b_rows, df_query, df_key, tri_in_rev, tri_blk_rev, head_rows):
    r = fl2d.shape[0]
    nhp = head_rows.shape[0]

    def body(x_ref, b_ref, dq_ref, dk_ref, u_ref, l_ref, hr_ref, o_ref, gb_ref):
        c1 = _split_dot(dq_ref[...] + dk_ref[...], u_ref[...], 3)
        tot = jnp.broadcast_to(c1[:, 0:1], (r, LANES))
        acc = None
        for _ in range(3):
            part = tot.astype(BF16)
            dd = jnp.dot(l_ref[...], part, preferred_element_type=F32)
            acc = dd if acc is None else acc + dd
            tot = tot - part.astype(F32)
        x = x_ref[...] + b_ref[...]
        e = jnp.exp(-jnp.abs(x))
        dfl = (c1 + acc) * (jnp.where(x >= 0, e, 1.0) / (1.0 + e))
        o_ref[...] = dfl
        rs = jnp.broadcast_to(jnp.sum(dfl, axis=1, keepdims=True), (r, LANES))
        gb = None
        for _ in range(3):
            part = rs.astype(BF16)
            dd = jnp.dot(hr_ref[...], part, preferred_element_type=F32)
            gb = dd if gb is None else gb + dd
            rs = rs - part.astype(F32)
        gb_ref[...] = gb

    return _pcall(
        body, name="fgate_bwd",
        out_shape=[jax.ShapeDtypeStruct((r, LANES), F32), jax.ShapeDtypeStruct((nhp, LANES), F32)],
        compiler_params=pltpu.CompilerParams(vmem_limit_bytes=VMEM_LIMIT),
    )(fl2d, b_rows, df_query, df_key, tri_in_rev, tri_blk_rev, head_rows)


def _norm_mod(x, g, scale, shift):
    r = lax.rsqrt(jnp.mean(x * x, axis=-1, keepdims=True) + EPS)
    return (x * r * g) * (1.0 + scale) + shift


def _norm_mod_bwd(x, dh, g, scale):
    r = lax.rsqrt(jnp.mean(x * x, axis=-1, keepdims=True) + EPS)
    xn = x * r
    dshift = jnp.sum(dh, axis=0, keepdims=True)
    dscale = jnp.sum(dh * (xn * g), axis=0, keepdims=True)
    dxn_g = dh * (1.0 + scale)
    dg = jnp.sum(dxn_g * xn, axis=0, keepdims=True)
    dxn = dxn_g * g
    dx = r * (dxn - xn * jnp.mean(dxn * xn, axis=-1, keepdims=True))
    return dx, dshift, dscale, dg


def _in_proj_fwd(x, mod8, g_attn, w_qkv, w_f, tm):
    t, d = x.shape
    dg = w_qkv.shape[1] // 6

    def body(x_ref, mod_ref, g_ref, w_ref, wf_ref, qkv_ref, fl_ref, h1_ref, h_sc):
        j = pl.program_id(1)

        @pl.when(j == 0)
        def _():
            h = _norm_mod(x_ref[...], g_ref[...], mod_ref[1:2, :], mod_ref[0:1, :]).astype(BF16)
            h_sc[...] = h
            h1_ref[...] = h
            fl_ref[...] = jnp.dot(h, wf_ref[...], preferred_element_type=F32)

        s = jnp.where((j == 0) | (j == 3), HEAD_DIM ** -0.5, 1.0)
        qkv_ref[...] = (jnp.dot(h_sc[...], w_ref[...], preferred_element_type=F32) * s).astype(BF16)

    return _pcall(
        body, name="in_proj_fwd", grid=(t // tm, 6),
        out_shape=[jax.ShapeDtypeStruct((t, 6 * dg), BF16), jax.ShapeDtypeStruct((t, LANES), F32),
                   jax.ShapeDtypeStruct((t, d), BF16)],
        in_specs=[pl.BlockSpec((tm, d), lambda i, j: (i, 0)), pl.BlockSpec((8, d), lambda i, j: (0, 0)),
                  pl.BlockSpec((1, d), lambda i, j: (0, 0)), pl.BlockSpec((d, dg), lambda i, j: (0, j)),
                  pl.BlockSpec((d, LANES), lambda i, j: (0, 0))],
        out_specs=[pl.BlockSpec((tm, dg), lambda i, j: (i, j)), pl.BlockSpec((tm, LANES), lambda i, j: (i, 0)),
                   pl.BlockSpec((tm, d), lambda i, j: (i, 0))],
        scratch_shapes=[pltpu.VMEM((tm, d), BF16)],
        compiler_params=_params(("arbitrary", "arbitrary")),
    )(x, mod8, g_attn, w_qkv, w_f)


def _head_rstd(o, bd):
    return lax.rsqrt(_split_dot(o * o, bd, 3) * (1.0 / HEAD_DIM) + EPS)


def _attn_out_fwd(x, o_fox, o_sb, g_fox, g_sb, w_out, mod8, bd, tm):
    t, d = x.shape
    dg = o_fox.shape[1]

    def body(x_ref, of_ref, os_ref, gf_ref, gs_ref, w_ref, mod_ref, bd_ref, x2_ref, mix_ref):
        of, osb = of_ref[...], os_ref[...]
        mf = (of * _head_rstd(of, bd_ref[...]) * gf_ref[...]).astype(BF16)
        ms = (osb * _head_rstd(osb, bd_ref[...]) * gs_ref[...]).astype(BF16)
        mix_ref[:, :dg] = mf
        mix_ref[:, dg:] = ms
        y = jnp.dot(mf, w_ref[:dg, :], preferred_element_type=F32) + jnp.dot(ms, w_ref[dg:, :], preferred_element_type=F32)
        x2_ref[...] = x_ref[...] + mod_ref[2:3, :] * y

    row = lambda w: pl.BlockSpec((tm, w), lambda i: (i, 0))
    full = lambda a: pl.BlockSpec(a.shape, lambda i: (0,) * a.ndim)
    return _pcall(
        body, name="attn_out_fwd", grid=(t // tm,),
        out_shape=[jax.ShapeDtypeStruct((t, d), F32), jax.ShapeDtypeStruct((t, 2 * dg), BF16)],
        in_specs=[row(d), row(dg), row(dg), full(g_fox), full(g_sb), full(w_out), full(mod8), full(bd)],
        out_specs=[row(d), row(2 * dg)],
        compiler_params=_params(("arbitrary",)),
    )(x, o_fox, o_sb, g_fox, g_sb, w_out, mod8, bd)


def _attn_out_bwd(dx2, mix, o_fox, o_sb, g_fox, g_sb, w_out, mod8, bd, hsel, tm):
    t, d = dx2.shape
    dg = o_fox.shape[1]

    def body(dx_ref, mix_ref, of_ref, os_ref, gf_ref, gs_ref, w_ref, mod_ref, bd_ref, hs_ref,
             dof_ref, dos_ref, dlt_ref, dxg_ref, part_ref):
        dx = dx_ref[...]
        gate = mod_ref[2:3, :]
        dxg = (dx * gate).astype(BF16)
        dxg_ref[...] = dxg
        mixv = mix_ref[...]
        y = jnp.dot(mixv[:, :dg], w_ref[:dg, :], preferred_element_type=F32)
        y = y + jnp.dot(mixv[:, dg:], w_ref[dg:, :], preferred_element_type=F32)
        part_ref[0] = jnp.zeros((8, d), F32)
        part_ref[0, 0:1, :] = jnp.sum(dx * y, axis=0, keepdims=True)
        for grp, (o_ref, g_ref, do_ref) in enumerate(((of_ref, gf_ref, dof_ref), (os_ref, gs_ref, dos_ref))):
            dmix = lax.dot_general(dxg, w_ref[grp * dg:(grp + 1) * dg, :], NT_DIMS, preferred_element_type=F32)
            o = o_ref[...]
            r = _head_rstd(o, bd_ref[...])
            n = o * r
            part_ref[0, 1:2, grp * dg:(grp + 1) * dg] = jnp.sum(dmix * n, axis=0, keepdims=True)
            dn = dmix * g_ref[...]
            mh = _split_dot(dn * n, bd_ref[...], 3) * (1.0 / HEAD_DIM)
            do = r * (dn - n * mh)
            do_ref[...] = do.astype(BF16)
            if grp == 0:
                dlt_ref[...] = _split_dot(do * o, hs_ref[...], 3)

    row = lambda w: pl.BlockSpec((tm, w), lambda i: (i, 0))
    full = lambda a: pl.BlockSpec(a.shape, lambda i: (0,) * a.ndim)
    nt = t // tm
    return _pcall(
        body, name="attn_out_bwd", grid=(nt,),
        out_shape=[jax.ShapeDtypeStruct((t, dg), BF16), jax.ShapeDtypeStruct((t, dg), BF16),
                   jax.ShapeDtypeStruct((t, LANES), F32), jax.ShapeDtypeStruct((t, d), BF16),
                   jax.ShapeDtypeStruct((nt, 8, d), F32)],
        in_specs=[row(d), row(2 * dg), row(dg), row(dg), full(g_fox), full(g_sb), full(w_out), full(mod8),
                  full(bd), full(hsel)],
        out_specs=[row(dg), row(dg), row(LANES), row(d), pl.BlockSpec((1, 8, d), lambda i: (i, 0, 0))],
        compiler_params=_params(("arbitrary",)),
    )(dx2, mix, o_fox, o_sb, g_fox, g_sb, w_out, mod8, bd, hsel)


def _in_proj_bwd(dparts, dfl, w_qkv, w_f, x, dx2, mod8, g_attn, tm):
    t, d = x.shape
    dg = dparts[0].shape[1]

    def body(*refs):
        d_refs = refs[:6]
        dfl_ref, w_ref, wf_ref, x_ref, dx2_ref, mod_ref, g_ref, gx_ref, dp_ref, dflb_ref, part_ref = refs[6:]
        dh = None
        for k in range(6):
            dk = d_refs[k][...]
            if k in (0, 3):
                dk = dk * HEAD_DIM ** -0.5
            db = dk.astype(BF16)
            dp_ref[:, k * dg:(k + 1) * dg] = db
            term = lax.dot_general(db, w_ref[:, k * dg:(k + 1) * dg], NT_DIMS, preferred_element_type=F32)
            dh = term if dh is None else dh + term
        dfb = dfl_ref[...].astype(BF16)
        dflb_ref[...] = dfb
        dh = dh + lax.dot_general(dfb, wf_ref[...], NT_DIMS, preferred_element_type=F32)
        dx, dshift, dscale, dgn = _norm_mod_bwd(x_ref[...], dh, g_ref[...], mod_ref[1:2, :])
        gx_ref[...] = dx2_ref[...] + dx
        part_ref[0] = jnp.zeros((8, d), F32)
        part_ref[0, 0:1, :] = dshift
        part_ref[0, 1:2, :] = dscale
        part_ref[0, 2:3, :] = dgn

    row = lambda w: pl.BlockSpec((tm, w), lambda i: (i, 0))
    full = lambda a: pl.BlockSpec(a.shape, lambda i: (0,) * a.ndim)
    nt = t // tm
    return _pcall(
        body, name="in_proj_bwd", grid=(nt,),
        out_shape=[jax.ShapeDtypeStruct((t, d), F32), jax.ShapeDtypeStruct((t, 6 * dg), BF16),
                   jax.ShapeDtypeStruct((t, LANES), BF16), jax.ShapeDtypeStruct((nt, 8, d), F32)],
        in_specs=[row(dg)] * 6 + [row(LANES), full(w_qkv), full(w_f), row(d), row(d), full(mod8), full(g_attn)],
        out_specs=[row(d), row(6 * dg), row(LANES), pl.BlockSpec((1, 8, d), lambda i: (i, 0, 0))],
        compiler_params=_params(("arbitrary",)),
    )(*dparts, dfl, w_qkv, w_f, x, dx2, mod8, g_attn)


def _matmul_tn(a, b, name):
    t, m = a.shape
    n = b.shape[1]
    a_t = a.T
    tm_ = _tile(m, (512, 256, 128))
    tn_ = _tile(n, (1024, 512, 256, 128))
    tk = _tile(t, (2048, 1024, 512, 256, 128))
    nk = t // tk

    def body(a_ref, b_ref, o_ref):
        k = pl.program_id(2)

        @pl.when(k == 0)
        def _():
            o_ref[...] = jnp.zeros_like(o_ref)

        o_ref[...] += jnp.dot(a_ref[...], b_ref[...], preferred_element_type=F32)

    return _pcall(
        body, name=name, grid=(m // tm_, n // tn_, nk),
        out_shape=jax.ShapeDtypeStruct((m, n), F32),
        in_specs=[pl.BlockSpec((tm_, tk), lambda i, j, k: (i, k)), pl.BlockSpec((tk, tn_), lambda i, j, k: (k, j))],
        out_specs=pl.BlockSpec((tm_, tn_), lambda i, j, k: (i, j)),
        compiler_params=_params(("arbitrary", "arbitrary", "arbitrary")),
    )(a_t, b)


HALO = 16


def _conv_taps(up_ext, cw, lo, rows):
    s1 = pltpu.roll(up_ext, 1, 0)
    s2 = pltpu.roll(up_ext, 2, 0)
    u = cw[2:3, :] * up_ext[lo:lo + rows] + cw[1:2, :] * s1[lo:lo + rows] + cw[0:1, :] * s2[lo:lo + rows] + cw[3:4, :]
    return u, s1, s2


def _ffn_fwd(x2, target, mod8, g_mlp, g_final, wg, wv, cwg, cwv, wd, tm, cf):
    t, d = x2.shape
    dfp = wg.shape[1]
    nt, nc = t // tm, dfp // cf
    hb = tm // HALO

    def body(x_ref, xp_ref, tg_ref, mod_ref, g_ref, gf_ref, wg_ref, wv_ref, cg_ref, cv_ref, wd_ref,
             dx3_ref, h2_ref, part_ref, h_sc, acc_sc):
        i, j = pl.program_id(0), pl.program_id(1)

        @pl.when(j == 0)
        def _():
            xe = jnp.concatenate([xp_ref[...], x_ref[...]], axis=0)
            h = _norm_mod(xe, g_ref[...], mod_ref[4:5, :], mod_ref[3:4, :]).astype(BF16)
            h_sc[...] = h
            h2_ref[...] = h[HALO:]
            acc_sc[...] = jnp.zeros_like(acc_sc)

        rowi = lax.broadcasted_iota(jnp.int32, (tm + HALO, 1), 0)
        keep = (rowi >= HALO) | (i > 0)
        hv = h_sc[...]
        upg = jnp.where(keep, jnp.dot(hv, wg_ref[...], preferred_element_type=F32), 0.0)
        upv = jnp.where(keep, jnp.dot(hv, wv_ref[...], preferred_element_type=F32), 0.0)
        ug, _, _ = _conv_taps(upg, cg_ref[...], HALO, tm)
        uv, _, _ = _conv_taps(upv, cv_ref[...], HALO, tm)
        act = (ug * jax.nn.sigmoid(ug) * uv).astype(BF16)
        acc_sc[...] += jnp.dot(act, wd_ref[...], preferred_element_type=F32)

        @pl.when(j == nc - 1)
        def _():
            y_ffn = acc_sc[...]
            x3 = x_ref[...] + mod_ref[5:6, :] * y_ffn
            r3 = lax.rsqrt(jnp.mean(x3 * x3, axis=-1, keepdims=True) + EPS)
            xn = x3 * r3
            gf = gf_ref[...]
            diff = xn * gf - tg_ref[...]
            dy = diff * (1.0 / d)
            dxn = dy * gf
            dx3 = r3 * (dxn - xn * jnp.mean(dxn * xn, axis=-1, keepdims=True))
            dx3_ref[...] = dx3
            part_ref[0] = jnp.zeros((8, d), F32)
            part_ref[0, 0:1, :] = jnp.sum(dy * xn, axis=0, keepdims=True)
            part_ref[0, 1:2, :] = jnp.sum(dx3 * y_ffn, axis=0, keepdims=True)
            part_ref[0, 2:3, :] = jnp.sum(diff * diff, axis=0, keepdims=True) * (0.5 / d)

    row = lambda w: pl.BlockSpec((tm, w), lambda i, j: (i, 0))
    full = lambda a: pl.BlockSpec(a.shape, lambda i, j: (0,) * a.ndim)
    return _pcall(
        body, name="ffn_fwd", grid=(nt, nc),
        out_shape=[jax.ShapeDtypeStruct((t, d), F32), jax.ShapeDtypeStruct((t, d), BF16),
                   jax.ShapeDtypeStruct((nt, 8, d), F32)],
        in_specs=[row(d), pl.BlockSpec((HALO, d), lambda i, j: (jnp.maximum(i * hb - 1, 0), 0)), row(d),
                  full(mod8), full(g_mlp), full(g_final),
                  pl.BlockSpec((d, cf), lambda i, j: (0, j)), pl.BlockSpec((d, cf), lambda i, j: (0, j)),
                  pl.BlockSpec((8, cf), lambda i, j: (0, j)), pl.BlockSpec((8, cf), lambda i, j: (0, j)),
                  pl.BlockSpec((cf, d), lambda i, j: (j, 0))],
        out_specs=[row(d), row(d), pl.BlockSpec((1, 8, d), lambda i, j: (i, 0, 0))],
        scratch_shapes=[pltpu.VMEM((tm + HALO, d), BF16), pltpu.VMEM((tm, d), F32)],
        compiler_params=_params(("arbitrary", "arbitrary")),
    )(x2, x2, target, mod8, g_mlp, g_final, wg, wv, cwg, cwv, wd)


def _ffn_bwd(x2, dx3, mod8, g_mlp, wg, wv, cwg, cwv, wd, tm, cf):
    t, d = x2.shape
    dfp = wg.shape[1]
    nt, nc = t // tm, dfp // cf
    hb = tm // HALO
    nhb = t // HALO
    ext = tm + 2 * HALO

    def body(x_ref, xp_ref, xn_ref, dx_ref, dxn_ref, mod_ref, g_ref, wg_ref, wv_ref, cg_ref, cv_ref, wd_ref,
             dx2_ref, dug_ref, duv_ref, act_ref, dxg_ref, part_ref, pcg_ref, pcv_ref, h_sc, dg_sc, dh_sc):
        i, j = pl.program_id(0), pl.program_id(1)

        @pl.when(j == 0)
        def _():
            xe = jnp.concatenate([xp_ref[...], x_ref[...], xn_ref[...]], axis=0)
            h_sc[...] = _norm_mod(xe, g_ref[...], mod_ref[4:5, :], mod_ref[3:4, :]).astype(BF16)
            de = (jnp.concatenate([dx_ref[...], dxn_ref[...]], axis=0) * mod_ref[5:6, :]).astype(BF16)
            dg_sc[...] = de
            dxg_ref[...] = de[:tm]
            dh_sc[...] = jnp.zeros_like(dh_sc)

        rowe = lax.broadcasted_iota(jnp.int32, (ext, 1), 0)
        keep_up = (rowe >= HALO) | (i > 0)
        rowu = lax.broadcasted_iota(jnp.int32, (tm + HALO, 1), 0)
        keep_du = (rowu < tm) | (i < nt - 1)
        hv = h_sc[...]
        upg = jnp.where(keep_up, jnp.dot(hv, wg_ref[...], preferred_element_type=F32), 0.0)
        upv = jnp.where(keep_up, jnp.dot(hv, wv_ref[...], preferred_element_type=F32), 0.0)
        cg, cv = cg_ref[...], cv_ref[...]
        ug, g1, g2 = _conv_taps(upg, cg, HALO, tm + HALO)
        uv, v1, v2 = _conv_taps(upv, cv, HALO, tm + HALO)
        dact = lax.dot_general(dg_sc[...], wd_ref[...], NT_DIMS, preferred_element_type=F32)
        sg = jax.nn.sigmoid(ug)
        sil = ug * sg
        act_ref[...] = (sil * uv)[:tm].astype(BF16)
        duv = jnp.where(keep_du, dact * sil, 0.0)
        dug = jnp.where(keep_du, dact * uv * (sg * (1.0 + ug * (1.0 - sg))), 0.0)

        def back(du, cw, up, s1, s2, pc_ref):
            n = tm + HALO
            dup = (cw[2:3, :] * du + cw[1:2, :] * pltpu.roll(du, n - 1, 0) + cw[0:1, :] * pltpu.roll(du, n - 2, 0))[:tm]
            dut = du[:tm]
            pc_ref[0] = jnp.zeros((8, cf), F32)
            pc_ref[0, 0:1, :] = jnp.sum(dut * s2[HALO:HALO + tm], axis=0, keepdims=True)
            pc_ref[0, 1:2, :] = jnp.sum(dut * s1[HALO:HALO + tm], axis=0, keepdims=True)
            pc_ref[0, 2:3, :] = jnp.sum(dut * up[HALO:HALO + tm], axis=0, keepdims=True)
            pc_ref[0, 3:4, :] = jnp.sum(dut, axis=0, keepdims=True)
            return dup.astype(BF16)

        dupg = back(dug, cg, upg, g1, g2, pcg_ref)
        dupv = back(duv, cv, upv, v1, v2, pcv_ref)
        dug_ref[...] = dupg
        duv_ref[...] = dupv
        dh_sc[...] += (lax.dot_general(dupg, wg_ref[...], NT_DIMS, preferred_element_type=F32)
                       + lax.dot_general(dupv, wv_ref[...], NT_DIMS, preferred_element_type=F32))

        @pl.when(j == nc - 1)
        def _():
            dx, dshift, dscale, dgn = _norm_mod_bwd(x_ref[...], dh_sc[...], g_ref[...], mod_ref[4:5, :])
            dx2_ref[...] = dx_ref[...] + dx
            part_ref[0] = jnp.zeros((8, d), F32)
            part_ref[0, 0:1, :] = dshift
            part_ref[0, 1:2, :] = dscale
            part_ref[0, 2:3, :] = dgn

    row = lambda w: pl.BlockSpec((tm, w), lambda i, j: (i, 0))
    prev = pl.BlockSpec((HALO, d), lambda i, j: (jnp.maximum(i * hb - 1, 0), 0))
    nxt = pl.BlockSpec((HALO, d), lambda i, j: (jnp.minimum((i + 1) * hb, nhb - 1), 0))
    full = lambda a: pl.BlockSpec(a.shape, lambda i, j: (0,) * a.ndim)
    chunk = pl.BlockSpec((tm, cf), lambda i, j: (i, j))
    pchunk = pl.BlockSpec((1, 8, cf), lambda i, j: (i, 0, j))
    return _pcall(
        body, name="ffn_bwd", grid=(nt, nc),
        out_shape=[jax.ShapeDtypeStruct((t, d), F32), jax.ShapeDtypeStruct((t, dfp), BF16),
                   jax.ShapeDtypeStruct((t, dfp), BF16), jax.ShapeDtypeStruct((t, dfp), BF16),
                   jax.ShapeDtypeStruct((t, d), BF16), jax.ShapeDtypeStruct((nt, 8, d), F32),
                   jax.ShapeDtypeStruct((nt, 8, dfp), F32), jax.ShapeDtypeStruct((nt, 8, dfp), F32)],
        in_specs=[row(d), prev, nxt, row(d), nxt, full(mod8), full(g_mlp),
                  pl.BlockSpec((d, cf), lambda i, j: (0, j)), pl.BlockSpec((d, cf), lambda i, j: (0, j)),
                  pl.BlockSpec((8, cf), lambda i, j: (0, j)), pl.BlockSpec((8, cf), lambda i, j: (0, j)),
                  pl.BlockSpec((cf, d), lambda i, j: (j, 0))],
        out_specs=[row(d), chunk, chunk, chunk, row(d), pl.BlockSpec((1, 8, d), lambda i, j: (i, 0, 0)), pchunk, pchunk],
        scratch_shapes=[pltpu.VMEM((ext, d), BF16), pltpu.VMEM((tm + HALO, d), BF16), pltpu.VMEM((tm, d), F32)],
        compiler_params=_params(("arbitrary", "arbitrary")),
    )(x2, x2, x2, dx3, dx3, mod8, g_mlp, wg, wv, cwg, cwv, wd)


def _head_masks():
    lane = lax.broadcasted_iota(jnp.int32, (1, LANES), 1)
    in_a = lane < HEAD_DIM
    return in_a, jnp.logical_not(in_a)


BLK = 2 * LANES


def _stack_heads(qkv, dg):
    t = qkv.shape[0]
    p = dg // LANES
    rows = _tile(t, (512, 256, 128))
    sub = rows // BLK

    def body(kf_ref, vf_ref, ks_ref, vs_ref, okf, ovf, oks, ovs):
        in_a, in_b = _head_masks()
        for src, dst in ((kf_ref, okf), (vf_ref, ovf), (ks_ref, oks), (vs_ref, ovs)):
            v = src[...]
            zero = jnp.zeros_like(v)
            va, vb = jnp.where(in_a, v, zero), jnp.where(in_b, v, zero)
            for s in range(sub):
                dst[0, s, :BLK, :] = va[s * BLK:(s + 1) * BLK]
                dst[0, s, BLK:, :] = vb[s * BLK:(s + 1) * BLK]

    col = lambda base: pl.BlockSpec((rows, LANES), lambda h, j: (j, base * p + h))
    out = pl.BlockSpec((1, sub, 2 * BLK, LANES), lambda h, j: (h, j, 0, 0))
    shape = jax.ShapeDtypeStruct((p, t // BLK, 2 * BLK, LANES), BF16)
    return _pcall(
        body, name="stack_heads", grid=(p, t // rows),
        out_shape=[shape] * 4, in_specs=[col(1), col(2), col(4), col(5)], out_specs=[out] * 4,
        compiler_params=_params(("arbitrary", "arbitrary")),
    )(qkv, qkv, qkv, qkv)


def _tile_masks():
    rowi = lax.broadcasted_iota(jnp.int32, (BLK, BLK), 0)
    coli = lax.broadcasted_iota(jnp.int32, (BLK, BLK), 1)
    return coli <= rowi, coli < rowi


def _pair_triangle(suffix):
    r = lax.broadcasted_iota(jnp.int32, (BLK, BLK), 0)
    c = lax.broadcasted_iota(jnp.int32, (BLK, BLK), 1)
    return ((r >= c) if suffix else (r <= c)).astype(BF16)


def _pair_cumsum(x2, tri, passes):
    return jnp.concatenate([_split_dot(x2[:, :BLK], tri, passes), _split_dot(x2[:, BLK:], tri, passes)], axis=1)


def _pair_specs(t, dg, base):
    p = dg // LANES
    q = pl.BlockSpec((BLK, LANES), lambda h, i: (i, base * p + h))
    kv = pl.BlockSpec((1, t // BLK, 2 * BLK, LANES), lambda h, i: (h, 0, 0, 0))
    return q, kv


def _fox_fwd(qkv, kst, vst, fcol, frow2, dg):
    t = qkv.shape[0]
    p, nq = dg // LANES, t // BLK
    nh = 2 * p

    def body(q_ref, k_ref, v_ref, ft_ref, fs_ref, o_ref, lse_ref):
        i = pl.program_id(1)
        in_a, _ = _head_masks()
        causal, _ = _tile_masks()
        q2 = q_ref[...]
        ft = tuple(jnp.broadcast_to(ft_ref[a], (BLK, BLK)) for a in range(2))

        def tile(j, carry, masked):
            m, l, acc = carry
            kb, vb = k_ref[0, j], v_ref[0, j]
            s2 = lax.dot_general(q2, kb, NT_DIMS, preferred_element_type=F32)
            fs = fs_ref[0, j]
            m_new, l_new, alpha, pr = [], [], [], []
            for a in range(2):
                sl = slice(a * BLK, (a + 1) * BLK)
                s = (s2[:, sl] + ft[a]) - fs[:, sl]
                if masked:
                    s = jnp.where(causal, s, NEG)
                mn = jnp.maximum(m[a], jnp.max(s, axis=1, keepdims=True))
                pa = jnp.exp(s - mn)
                al = jnp.exp(m[a] - mn)
                m_new.append(mn)
                alpha.append(al)
                l_new.append(al * l[a] + jnp.sum(pa, axis=1, keepdims=True))
                pr.append(pa.astype(BF16))
            acc = jnp.where(in_a, alpha[0], alpha[1]) * acc + jnp.dot(
                jnp.concatenate(pr, axis=1), vb, preferred_element_type=F32)
            return tuple(m_new), tuple(l_new), acc

        neg, zero = jnp.full((BLK, 1), NEG, F32), jnp.zeros((BLK, 1), F32)
        carry = lax.fori_loop(0, i, functools.partial(tile, masked=False), ((neg, neg), (zero, zero), jnp.zeros((BLK, LANES), F32)))
        m, l, acc = tile(i, carry, True)
        o_ref[...] = acc / jnp.where(in_a, l[0], l[1])
        lse_ref[0] = m[0] + jnp.log(l[0])
        lse_ref[1] = m[1] + jnp.log(l[1])

    qs, kv = _pair_specs(t, dg, 0)
    col = pl.BlockSpec((2, BLK, 1), lambda h, i: (h, i, 0))
    return _pcall(
        body, name="fox_fwd", grid=(p, nq),
        out_shape=[jax.ShapeDtypeStruct((t, dg), F32), jax.ShapeDtypeStruct((nh, t, 1), F32)],
        in_specs=[qs, kv, kv, col, pl.BlockSpec((1, nq, 1, 2 * BLK), lambda h, i: (h, 0, 0, 0))],
        out_specs=[pl.BlockSpec((BLK, LANES), lambda h, i: (i, h)), col],
        compiler_params=_params(("arbitrary", "arbitrary")),
    )(qkv, kst, vst, fcol, frow2)


def _fold_heads(stacked, in_a):
    return jnp.where(in_a, stacked[:BLK], stacked[BLK:])


def _fox_bwd(qkv, kst, vst, do, fcol, frow2, lse, delta, dg):
    t = qkv.shape[0]
    p, nq = dg // LANES, t // BLK
    nh = 2 * p

    def body(q_ref, k_ref, v_ref, do_ref, ft_ref, fs_ref, lse_ref, dl_ref, dq_ref, dk_ref, dv_ref, dfs_ref, dft_ref):
        i = pl.program_id(1)

        @pl.when(i == 0)
        def _():
            dk_ref[...] = jnp.zeros_like(dk_ref)
            dv_ref[...] = jnp.zeros_like(dv_ref)
            dfs_ref[...] = jnp.zeros_like(dfs_ref)

        in_a, _ = _head_masks()
        causal, _ = _tile_masks()
        q2, do2 = q_ref[...], do_ref[...]
        ft = tuple(jnp.broadcast_to(ft_ref[a] - lse_ref[a], (BLK, BLK)) for a in range(2))
        dl = tuple(jnp.broadcast_to(dl_ref[a], (BLK, BLK)) for a in range(2))

        def tile(j, carry, masked):
            dq, dft = carry
            kb, vb = k_ref[0, j], v_ref[0, j]
            s2 = lax.dot_general(q2, kb, NT_DIMS, preferred_element_type=F32)
            dp2 = lax.dot_general(do2, vb, NT_DIMS, preferred_element_type=F32)
            fs = fs_ref[0, j]
            pr, ds, dft_new = [], [], []
            for a in range(2):
                sl = slice(a * BLK, (a + 1) * BLK)
                s = (s2[:, sl] + ft[a]) - fs[:, sl]
                if masked:
                    s = jnp.where(causal, s, NEG)
                pa = jnp.exp(s)
                dsa = pa * (dp2[:, sl] - dl[a])
                pr.append(pa.astype(BF16))
                ds.append(dsa)
                dft_new.append(dft[a] + jnp.sum(dsa, axis=1, keepdims=True))
            ds2 = jnp.concatenate(ds, axis=1)
            dsb = ds2.astype(BF16)
            off = pl.multiple_of(j * BLK, BLK)
            dk_ref[pl.ds(off, BLK), :] += _fold_heads(lax.dot_general(dsb, q2, TN_DIMS, preferred_element_type=F32), in_a)
            dv_ref[pl.ds(off, BLK), :] += _fold_heads(
                lax.dot_general(jnp.concatenate(pr, axis=1), do2, TN_DIMS, preferred_element_type=F32), in_a)
            dfs_ref[0, j] += -jnp.sum(ds2, axis=0, keepdims=True)
            return dq + jnp.dot(dsb, kb, preferred_element_type=F32), tuple(dft_new)

        zero = jnp.zeros((BLK, 1), F32)
        carry = lax.fori_loop(0, i, functools.partial(tile, masked=False), (jnp.zeros((BLK, LANES), F32), (zero, zero)))
        dq, dft = tile(i, carry, True)
        dq_ref[...] = dq
        dft_ref[0] = dft[0]
        dft_ref[1] = dft[1]

    qs, kv = _pair_specs(t, dg, 0)
    col = pl.BlockSpec((2, BLK, 1), lambda h, i: (h, i, 0))
    rowspec = pl.BlockSpec((1, nq, 1, 2 * BLK), lambda h, i: (h, 0, 0, 0))
    blk = pl.BlockSpec((BLK, LANES), lambda h, i: (i, h))
    acc = pl.BlockSpec((t, LANES), lambda h, i: (0, h))
    return _pcall(
        body, name="fox_bwd", grid=(p, nq),
        out_shape=[jax.ShapeDtypeStruct((t, dg), F32)] * 3 + [jax.ShapeDtypeStruct((p, nq, 1, 2 * BLK), F32),
                                                              jax.ShapeDtypeStruct((nh, t, 1), F32)],
        in_specs=[qs, kv, kv, blk, col, rowspec, col, col],
        out_specs=[blk, acc, acc, rowspec, col],
        compiler_params=_params(("arbitrary", "arbitrary")),
    )(qkv, kst, vst, do, fcol, frow2, lse, delta)


def _softplus_parts(z):
    e = jnp.exp(-jnp.abs(z))
    return jnp.maximum(z, 0.0) + jnp.log(1.0 + e), e


def _sigmoid_from(z, e):
    d = 1.0 + e
    r = pl.reciprocal(d, approx=True)
    r = r * (2.0 - d * r)
    return jnp.where(z >= 0, 1.0, e) * r


def _sb_fwd(qkv, kst, vst, dg):
    t = qkv.shape[0]
    p, nq = dg // LANES, t // BLK
    nh = 2 * p

    def body(q_ref, k_ref, v_ref, o_ref, rt_ref):
        i = pl.program_id(1)
        _, strict = _tile_masks()
        strict2 = jnp.concatenate([strict, strict], axis=1)
        suffix = _pair_triangle(True)
        q2 = q_ref[...]

        def tile(j, carry, masked):
            rest, acc = carry
            kb, vb = k_ref[0, j], v_ref[0, j]
            z = lax.dot_general(q2, kb, NT_DIMS, preferred_element_type=F32)
            sp, _ = _softplus_parts(z)
            if masked:
                sp = jnp.where(strict2, sp, 0.0)
            cs = _pair_cumsum(sp, suffix, 2)
            w, rest_new = [], []
            for a in range(2):
                sl = slice(a * BLK, (a + 1) * BLK)
                wa = jnp.exp(z[:, sl] - cs[:, sl] - rest[a])
                if masked:
                    wa = jnp.where(strict, wa, 0.0)
                w.append(wa.astype(BF16))
                rest_new.append(rest[a] + cs[:, a * BLK:a * BLK + 1])
            acc = acc + jnp.dot(jnp.concatenate(w, axis=1), vb, preferred_element_type=F32)
            return tuple(rest_new), acc

        zero = jnp.zeros((BLK, 1), F32)
        carry = tile(i, ((zero, zero), jnp.zeros((BLK, LANES), F32)), True)
        rest, acc = lax.fori_loop(0, i, lambda jj, c: tile(i - 1 - jj, c, False), carry)
        o_ref[...] = acc
        rt_ref[0] = rest[0]
        rt_ref[1] = rest[1]

    qs, kv = _pair_specs(t, dg, 3)
    col = pl.BlockSpec((2, BLK, 1), lambda h, i: (h, i, 0))
    return _pcall(
        body, name="sb_fwd", grid=(p, nq),
        out_shape=[jax.ShapeDtypeStruct((t, dg), F32), jax.ShapeDtypeStruct((nh, t, 1), F32)],
        in_specs=[qs, kv, kv],
        out_specs=[pl.BlockSpec((BLK, LANES), lambda h, i: (i, h)), col],
        compiler_params=_params(("arbitrary", "arbitrary")),
    )(qkv, kst, vst)


def _sb_bwd(qkv, kst, vst, do, rtot, dg):
    t = qkv.shape[0]
    p, nq = dg // LANES, t // BLK

    def body(q_ref, k_ref, v_ref, do_ref, rt_ref, dq_ref, dk_ref, dv_ref):
        i = pl.program_id(1)

        @pl.when(i == 0)
        def _():
            dk_ref[...] = jnp.zeros_like(dk_ref)
            dv_ref[...] = jnp.zeros_like(dv_ref)

        in_a, _ = _head_masks()
        _, strict = _tile_masks()
        strict2 = jnp.concatenate([strict, strict], axis=1)
        prefix = _pair_triangle(False)
        q2, do2 = q_ref[...], do_ref[...]
        rt = (rt_ref[0], rt_ref[1])

        def tile(j, carry, masked):
            before, gbefore, dq = carry
            kb, vb = k_ref[0, j], v_ref[0, j]
            z = lax.dot_general(q2, kb, NT_DIMS, preferred_element_type=F32)
            da = lax.dot_general(do2, vb, NT_DIMS, preferred_element_type=F32)
            sp, e = _softplus_parts(z)
            sig = _sigmoid_from(z, e)
            if masked:
                sp = jnp.where(strict2, sp, 0.0)
            pre = _pair_cumsum(sp, prefix, 2)
            w = []
            for a in range(2):
                sl = slice(a * BLK, (a + 1) * BLK)
                wa = jnp.exp(z[:, sl] + (before[a] - rt[a]) + pre[:, sl] - sp[:, sl])
                if masked:
                    wa = jnp.where(strict, wa, 0.0)
                w.append(wa)
            w2 = jnp.concatenate(w, axis=1)
            g = w2 * da
            preg = _pair_cumsum(g, prefix, 1)
            dz = []
            for a in range(2):
                sl = slice(a * BLK, (a + 1) * BLK)
                dza = g[:, sl] * (1.0 - sig[:, sl]) - sig[:, sl] * (gbefore[a] + preg[:, sl] - g[:, sl])
                if masked:
                    dza = jnp.where(strict, dza, 0.0)
                dz.append(dza.astype(BF16))
            dzb = jnp.concatenate(dz, axis=1)
            off = pl.multiple_of(j * BLK, BLK)
            dk_ref[pl.ds(off, BLK), :] += _fold_heads(lax.dot_general(dzb, q2, TN_DIMS, preferred_element_type=F32), in_a)
            dv_ref[pl.ds(off, BLK), :] += _fold_heads(
                lax.dot_general(w2.astype(BF16), do2, TN_DIMS, preferred_element_type=F32), in_a)
            last = lambda x, a: x[:, (a + 1) * BLK - 1:(a + 1) * BLK]
            return (tuple(before[a] + last(pre, a) for a in range(2)),
                    tuple(gbefore[a] + last(preg, a) for a in range(2)),
                    dq + jnp.dot(dzb, kb, preferred_element_type=F32))

        zero = jnp.zeros((BLK, 1), F32)
        carry = lax.fori_loop(0, i, functools.partial(tile, masked=False), ((zero, zero), (zero, zero), jnp.zeros((BLK, LANES), F32)))
        dq_ref[...] = tile(i, carry, True)[2]

    qs, kv = _pair_specs(t, dg, 3)
    col = pl.BlockSpec((2, BLK, 1), lambda h, i: (h, i, 0))
    blk = pl.BlockSpec((BLK, LANES), lambda h, i: (i, h))
    acc = pl.BlockSpec((t, LANES), lambda h, i: (0, h))
    return _pcall(
        body, name="sb_bwd", grid=(p, nq),
        out_shape=[jax.ShapeDtypeStruct((t, dg), F32)] * 3,
        in_specs=[qs, kv, kv, blk, col],
        out_specs=[blk, acc, acc],
        compiler_params=_params(("arbitrary", "arbitrary")),
    )(qkv, kst, vst, do, rtot)


XROWS = 144
LANE_FS, LANE_FT_A, LANE_FT_B = 0, 3, 6


def _pieces3(x):
    hi = x.astype(BF16).astype(F32)
    r = x - hi
    mid = r.astype(BF16).astype(F32)
    return hi, mid, (r - mid).astype(BF16).astype(F32)


def _bias_lanes(rows, entries):
    lane = lax.broadcasted_iota(jnp.int32, (1, LANES), 1)
    out = jnp.zeros((rows, LANES), F32)
    for l, v in entries:
        out = jnp.where(lane == l, v, out)
    return out


def _three(first, values):
    return [(first + k, v) for k, v in enumerate(values)]


def _stack_rows(x, in_a, in_b):
    zero = jnp.zeros_like(x)
    return jnp.concatenate([jnp.where(in_a, x, zero), jnp.where(in_b, x, zero)], axis=0)


def _transposed(x):
    return x.astype(F32).T.astype(BF16)


def _attn_operands(qkv, fcol, dg):
    t = qkv.shape[0]
    p, nk = dg // LANES, t // BLK

    def body(qf_ref, kf_ref, vf_ref, ks_ref, vs_ref, f_ref, qx_ref, kx_ref, kxt_ref, vf_o, vft_o, ks_o, kst_o, vs_o, vst_o):
        in_a, in_b = _head_masks()
        fa, fb = _pieces3(f_ref[0]), _pieces3(f_ref[1])
        qx_ref[0, :, :LANES] = qf_ref[...]
        qx_ref[0, :, LANES:] = _bias_lanes(
            BLK, _three(LANE_FS, (-1.0,) * 3) + _three(LANE_FT_A, fa) + _three(LANE_FT_B, fb)).astype(BF16)
        kf = kf_ref[...]
        zero = jnp.zeros_like(kf)
        top = jnp.concatenate([jnp.where(in_a, kf, zero), _bias_lanes(
            BLK, _three(LANE_FS, fa) + _three(LANE_FT_A, (1.0,) * 3)).astype(BF16)], axis=1)
        bot = jnp.concatenate([jnp.where(in_b, kf, zero), _bias_lanes(
            BLK, _three(LANE_FS, fb) + _three(LANE_FT_B, (1.0,) * 3)).astype(BF16)], axis=1)
        kx = jnp.concatenate([top, bot], axis=0)
        kx_ref[0, 0] = kx
        kxt_ref[0, 0] = _transposed(kx)[:XROWS]
        for src, dst, dst_t in ((vf_ref, vf_o, vft_o), (ks_ref, ks_o, kst_o), (vs_ref, vs_o, vst_o)):
            st = _stack_rows(src[...], in_a, in_b)
            dst[0, 0] = st
            dst_t[0, 0] = _transposed(st)

    col = lambda base: pl.BlockSpec((BLK, LANES), lambda h, j: (j, base * p + h))
    blk4 = lambda r, c: pl.BlockSpec((1, 1, r, c), lambda h, j: (h, j, 0, 0))
    shp4 = lambda r, c: jax.ShapeDtypeStruct((p, nk, r, c), BF16)
    return _pcall(
        body, name="attn_operands", grid=(p, nk),
        out_shape=[jax.ShapeDtypeStruct((p, t, 2 * LANES), BF16), shp4(2 * BLK, 2 * LANES), shp4(XROWS, 2 * BLK)]
        + [shp4(2 * BLK, LANES), shp4(LANES, 2 * BLK)] * 3,
        in_specs=[col(0), col(1), col(2), col(4), col(5), pl.BlockSpec((2, BLK, 1), lambda h, j: (h, j, 0))],
        out_specs=[pl.BlockSpec((1, BLK, 2 * LANES), lambda h, j: (h, j, 0)), blk4(2 * BLK, 2 * LANES), blk4(XROWS, 2 * BLK)]
        + [blk4(2 * BLK, LANES), blk4(LANES, 2 * BLK)] * 3,
        compiler_params=_params(("arbitrary", "arbitrary")),
    )(qkv, qkv, qkv, qkv, qkv, fcol)


def _fox_q_bwd(qkv, fcol, lse_col, dg):
    t = qkv.shape[0]
    p = dg // LANES

    def body(q_ref, f_ref, l_ref, qx_ref):
        fa, fb = _pieces3(f_ref[0] - l_ref[0]), _pieces3(f_ref[1] - l_ref[1])
        qx_ref[0, :, :LANES] = q_ref[...]
        qx_ref[0, :, LANES:] = _bias_lanes(
            BLK, _three(LANE_FS, (-1.0,) * 3) + _three(LANE_FT_A, fa) + _three(LANE_FT_B, fb)).astype(BF16)

    col = pl.BlockSpec((2, BLK, 1), lambda h, j: (h, j, 0))
    return _pcall(
        body, name="fox_q_bwd", grid=(p, t // BLK),
        out_shape=jax.ShapeDtypeStruct((p, t, 2 * LANES), BF16),
        in_specs=[pl.BlockSpec((BLK, LANES), lambda h, j: (j, h)), col, col],
        out_specs=pl.BlockSpec((1, BLK, 2 * LANES), lambda h, j: (h, j, 0)),
        compiler_params=_params(("arbitrary", "arbitrary")),
    )(qkv, fcol, lse_col)


def _key_query_masks():
    key = lax.broadcasted_iota(jnp.int32, (BLK, BLK), 0)
    qry = lax.broadcasted_iota(jnp.int32, (BLK, BLK), 1)
    return key <= qry, key < qry


def _key_triangle(kind):
    s = lax.broadcasted_iota(jnp.int32, (BLK, BLK), 0)
    j = lax.broadcasted_iota(jnp.int32, (BLK, BLK), 1)
    return {"suffix": j >= s, "prefix": j <= s, "before": j < s}[kind].astype(BF16)


def _tri_dot(tri, x, passes):
    acc = None
    for _ in range(passes):
        part = x.astype(BF16)
        d = jnp.dot(tri, part, preferred_element_type=F32)
        acc = d if acc is None else acc + d
        x = x - part.astype(F32)
    return acc


def _loop_two_tiles(n, tile, two_tiles, carry, descending=False):
    at = (lambda k: n - 1 - k) if descending else (lambda k: k)
    carry = lax.fori_loop(0, n // 2, lambda h, c: two_tiles(at(2 * h), at(2 * h + 1), c), carry)
    return lax.fori_loop(0, n % 2, lambda _, c: tile(at(n - 1), c), carry)


def _resident(shape):
    return pl.BlockSpec((1,) + shape, lambda h, i: (h,) + (0,) * len(shape), pipeline_mode=pl.Buffered(1))


def _rows_per_head(a, b):
    return jnp.concatenate([jnp.broadcast_to(a, (HEAD_DIM, BLK)), jnp.broadcast_to(b, (HEAD_DIM, BLK))], axis=0)


def _fold_heads(stacked, in_a):
    return jnp.where(in_a, stacked[:BLK], stacked[BLK:])


def _fox_fwd(qx, kx, v_t, dg):
    p, t = qx.shape[0], qx.shape[1]
    nq = t // BLK
    nh = 2 * p

    def body(q_ref, k_ref, vt_ref, o_ref, lse_ref):
        i = pl.program_id(1)
        causal, _ = _key_query_masks()
        q = q_ref[0]

        def scores(j, masked):
            s2 = lax.dot_general(k_ref[0, j], q, NT_DIMS, preferred_element_type=F32)
            s = [s2[a * BLK:(a + 1) * BLK] for a in range(2)]
            return [jnp.where(causal, x, NEG) for x in s] if masked else s

        def update(blocks, carry):
            m, l, acc = carry
            m_new, l_new, alpha = [], [], []
            pr = [[] for _ in blocks]
            for a in range(2):
                mn = m[a]
                for _, s in blocks:
                    mn = jnp.maximum(mn, jnp.max(s[a], axis=0, keepdims=True))
                al = jnp.exp(m[a] - mn)
                ln = al * l[a]
                for k, (_, s) in enumerate(blocks):
                    pa = jnp.exp(s[a] - mn)
                    ln = ln + jnp.sum(pa, axis=0, keepdims=True)
                    pr[k].append(pa.astype(BF16))
                m_new.append(mn)
                alpha.append(al)
                l_new.append(ln)
            acc = _rows_per_head(*alpha) * acc
            for k, (j, _) in enumerate(blocks):
                acc = acc + jnp.dot(vt_ref[0, j], jnp.concatenate(pr[k], axis=0), preferred_element_type=F32)
            return tuple(m_new), tuple(l_new), acc

        tile = lambda j, c: update([(j, scores(j, False))], c)
        two_tiles = lambda ja, jb, c: update([(ja, scores(ja, False)), (jb, scores(jb, False))], c)
        neg, zero = jnp.full((1, BLK), NEG, F32), jnp.zeros((1, BLK), F32)
        carry = _loop_two_tiles(i, tile, two_tiles, ((neg, neg), (zero, zero), jnp.zeros((LANES, BLK), F32)))
        m, l, acc = update([(i, scores(i, True))], carry)
        o_ref[...] = acc / _rows_per_head(*l)
        lse_ref[0] = m[0] + jnp.log(l[0])
        lse_ref[1] = m[1] + jnp.log(l[1])

    row = pl.BlockSpec((2, 1, BLK), lambda h, i: (h, 0, i))
    return _pcall(
        body, name="fox_fwd", grid=(p, nq),
        out_shape=[jax.ShapeDtypeStruct((dg, t), F32), jax.ShapeDtypeStruct((nh, 1, t), F32)],
        in_specs=[pl.BlockSpec((1, BLK, 2 * LANES), lambda h, i: (h, i, 0)), _resident((nq, 2 * BLK, 2 * LANES)),
                  _resident((nq, LANES, 2 * BLK))],
        out_specs=[pl.BlockSpec((LANES, BLK), lambda h, i: (h, i)), row],
        compiler_params=_params(("arbitrary", "arbitrary")),
    )(qx, kx, v_t)


def _fox_bwd(qxb, kx, kx_t, v_st, do, delta, dg):
    p, t = qxb.shape[0], qxb.shape[1]
    nq = t // BLK
    nh = 2 * p

    def body(q_ref, k_ref, kt_ref, v_ref, do_ref, dl_ref, dq_ref, dft_ref, dk_ref, dv_ref, dkx_ref):
        i = pl.program_id(1)

        @pl.when(i == 0)
        def _():
            dk_ref[...] = jnp.zeros_like(dk_ref)
            dv_ref[...] = jnp.zeros_like(dv_ref)
            dkx_ref[...] = jnp.zeros_like(dkx_ref)

        in_a, _ = _head_masks()
        first_lane = lax.broadcasted_iota(jnp.int32, (1, LANES), 1) == 0
        causal, _ = _key_query_masks()
        q, do2 = q_ref[0], do_ref[...]
        dl = (dl_ref[0], dl_ref[1])

        def products(j):
            return (lax.dot_general(k_ref[0, j], q, NT_DIMS, preferred_element_type=F32),
                    lax.dot_general(v_ref[0, j], do2, NT_DIMS, preferred_element_type=F32))

        def dscores(prod, masked):
            s2, dp2 = prod
            pr, ds = [], []
            for a in range(2):
                s = s2[a * BLK:(a + 1) * BLK]
                if masked:
                    s = jnp.where(causal, s, NEG)
                pa = jnp.exp(s)
                ds.append((pa * (dp2[a * BLK:(a + 1) * BLK] - dl[a])).astype(BF16))
                pr.append(pa.astype(BF16))
            return jnp.concatenate(ds, axis=0), jnp.concatenate(pr, axis=0)

        def accumulate(j, dsb, prb, dq):
            off = pl.multiple_of(j * BLK, BLK)
            dk_full = jnp.dot(dsb, q, preferred_element_type=F32)
            dk_ref[pl.ds(off, BLK), :] += _fold_heads(dk_full[:, :LANES], in_a)
            dkx_ref[pl.ds(off, BLK), :] += jnp.where(first_lane, dk_full[:BLK, LANES:], dk_full[BLK:, LANES:])
            dv_ref[pl.ds(off, BLK), :] += _fold_heads(jnp.dot(prb, do2, preferred_element_type=F32), in_a)
            return dq + jnp.dot(kt_ref[0, j], dsb, preferred_element_type=F32)

        def tile(j, dq, masked=False):
            return accumulate(j, *dscores(products(j), masked), dq)

        def two_tiles(ja, jb, dq):
            pa, pb = products(ja), products(jb)
            da, db = dscores(pa, False), dscores(pb, False)
            return accumulate(jb, *db, accumulate(ja, *da, dq))

        dq = _loop_two_tiles(i, tile, two_tiles, jnp.zeros((XROWS, BLK), F32))
        dq = tile(i, dq, True)
        dq_ref[...] = dq[:LANES]
        dft_ref[0] = dq[LANES + LANE_FT_A:LANES + LANE_FT_A + 1]
        dft_ref[1] = dq[LANES + LANE_FT_B:LANES + LANE_FT_B + 1]

    row = pl.BlockSpec((2, 1, BLK), lambda h, i: (h, 0, i))
    acc = pl.BlockSpec((t, LANES), lambda h, i: (0, h))
    return _pcall(
        body, name="fox_bwd", grid=(p, nq),
        out_shape=[jax.ShapeDtypeStruct((dg, t), F32), jax.ShapeDtypeStruct((nh, 1, t), F32)] + [jax.ShapeDtypeStruct((t, dg), F32)] * 3,
        in_specs=[pl.BlockSpec((1, BLK, 2 * LANES), lambda h, i: (h, i, 0)), _resident((nq, 2 * BLK, 2 * LANES)),
                  _resident((nq, XROWS, 2 * BLK)), _resident((nq, 2 * BLK, LANES)),
                  pl.BlockSpec((BLK, LANES), lambda h, i: (i, h)), row],
        out_specs=[pl.BlockSpec((LANES, BLK), lambda h, i: (h, i)), row, acc, acc, acc],
        compiler_params=_params(("arbitrary", "arbitrary")),
    )(qxb, kx, kx_t, v_st, do, delta)


def _softplus_of(z):
    return jnp.maximum(z, 0.0) + jnp.log(1.0 + jnp.exp(-jnp.abs(z)))


def _sb_fwd(qkv, k_st, v_t, dg):
    t = qkv.shape[0]
    p, nq = dg // LANES, t // BLK
    nh = 2 * p

    def body(q_ref, k_ref, vt_ref, o_ref, rt_ref):
        i = pl.program_id(1)
        _, strict = _key_query_masks()
        suffix = _key_triangle("suffix")
        q = q_ref[...]

        def scores(j):
            z2 = lax.dot_general(k_ref[0, j], q, NT_DIMS, preferred_element_type=F32)
            return [z2[a * BLK:(a + 1) * BLK] for a in range(2)]

        def suffix_sums(z, masked):
            out = []
            for a in range(2):
                sp = _softplus_of(z[a])
                if masked:
                    sp = jnp.where(strict, sp, 0.0)
                out.append(_tri_dot(suffix, sp, 2))
            return out

        def weights(z, cs, rest, masked):
            w, rest_new = [], []
            for a in range(2):
                wa = jnp.exp(z[a] - cs[a] - rest[a])
                if masked:
                    wa = jnp.where(strict, wa, 0.0)
                w.append(wa.astype(BF16))
                rest_new.append(rest[a] + cs[a][0:1])
            return jnp.concatenate(w, axis=0), tuple(rest_new)

        def tile(j, carry, masked=False):
            rest, acc = carry
            z = scores(j)
            w2, rest = weights(z, suffix_sums(z, masked), rest, masked)
            return rest, acc + jnp.dot(vt_ref[0, j], w2, preferred_element_type=F32)

        def two_tiles(ja, jb, carry):
            rest, acc = carry
            za, zb = scores(ja), scores(jb)
            csa, csb = suffix_sums(za, False), suffix_sums(zb, False)
            wa, rest = weights(za, csa, rest, False)
            wb, rest = weights(zb, csb, rest, False)
            acc = acc + jnp.dot(vt_ref[0, ja], wa, preferred_element_type=F32)
            return rest, acc + jnp.dot(vt_ref[0, jb], wb, preferred_element_type=F32)

        zero = jnp.zeros((1, BLK), F32)
        carry = tile(i, ((zero, zero), jnp.zeros((LANES, BLK), F32)), True)
        rest, acc = _loop_two_tiles(i, tile, two_tiles, carry, descending=True)
        o_ref[...] = acc
        rt_ref[0] = rest[0]
        rt_ref[1] = rest[1]

    return _pcall(
        body, name="sb_fwd", grid=(p, nq),
        out_shape=[jax.ShapeDtypeStruct((dg, t), F32), jax.ShapeDtypeStruct((nh, 1, t), F32)],
        in_specs=[pl.BlockSpec((BLK, LANES), lambda h, i: (i, 3 * p + h)), _resident((nq, 2 * BLK, LANES)),
                  _resident((nq, LANES, 2 * BLK))],
        out_specs=[pl.BlockSpec((LANES, BLK), lambda h, i: (h, i)), pl.BlockSpec((2, 1, BLK), lambda h, i: (h, 0, i))],
        compiler_params=_params(("arbitrary", "arbitrary")),
    )(qkv, k_st, v_t)


def _sb_bwd(qkv, k_st, k_t, v_st, do, rtot, dg):
    t = qkv.shape[0]
    p, nq = dg // LANES, t // BLK

    def body(q_ref, k_ref, kt_ref, v_ref, do_ref, rt_ref, dq_ref, dk_ref, dv_ref):
        i = pl.program_id(1)

        @pl.when(i == 0)
        def _():
            dk_ref[...] = jnp.zeros_like(dk_ref)
            dv_ref[...] = jnp.zeros_like(dv_ref)

        in_a, _ = _head_masks()
        _, strict = _key_query_masks()
        before_m, prefix_m = _key_triangle("before"), _key_triangle("prefix")
        q, do2 = q_ref[...], do_ref[...]
        rt = (rt_ref[0], rt_ref[1])

        def products(j):
            z2 = lax.dot_general(k_ref[0, j], q, NT_DIMS, preferred_element_type=F32)
            da2 = lax.dot_general(v_ref[0, j], do2, NT_DIMS, preferred_element_type=F32)
            return [z2[a * BLK:(a + 1) * BLK] for a in range(2)], [da2[a * BLK:(a + 1) * BLK] for a in range(2)]

        def softplus_sums(z, masked):
            sp = [_softplus_of(x) for x in z]
            if masked:
                sp = [jnp.where(strict, x, 0.0) for x in sp]
            return sp, [_tri_dot(before_m, x, 2) for x in sp]

        def weight_grads(z, da, sp, pre, before, masked):
            w, g, pg, before_new = [], [], [], []
            for a in range(2):
                wa = jnp.exp(z[a] + (before[a] - rt[a]) + pre[a])
                if masked:
                    wa = jnp.where(strict, wa, 0.0)
                ga = wa * da[a]
                w.append(wa.astype(BF16))
                g.append(ga)
                pg.append(jnp.dot(prefix_m, ga.astype(BF16), preferred_element_type=F32))
                before_new.append(before[a] + pre[a][BLK - 1:BLK] + sp[a][BLK - 1:BLK])
            return jnp.concatenate(w, axis=0), g, pg, tuple(before_new)

        def dlogits(sp, g, pg, gbefore, masked):
            dz, gbefore_new = [], []
            for a in range(2):
                s_incl = gbefore[a] + pg[a]
                dza = (g[a] - s_incl) + jnp.exp(-sp[a]) * s_incl
                if masked:
                    dza = jnp.where(strict, dza, 0.0)
                dz.append(dza.astype(BF16))
                gbefore_new.append(s_incl[BLK - 1:BLK])
            return jnp.concatenate(dz, axis=0), tuple(gbefore_new)

        def accumulate(j, dzb, wb, dq):
            off = pl.multiple_of(j * BLK, BLK)
            dk_ref[pl.ds(off, BLK), :] += _fold_heads(jnp.dot(dzb, q, preferred_element_type=F32), in_a)
            dv_ref[pl.ds(off, BLK), :] += _fold_heads(jnp.dot(wb, do2, preferred_element_type=F32), in_a)
            return dq + jnp.dot(kt_ref[0, j], dzb, preferred_element_type=F32)

        def tile(j, carry, masked=False):
            before, gbefore, dq = carry
            z, da = products(j)
            sp, pre = softplus_sums(z, masked)
            wb, g, pg, before = weight_grads(z, da, sp, pre, before, masked)
            dzb, gbefore = dlogits(sp, g, pg, gbefore, masked)
            return before, gbefore, accumulate(j, dzb, wb, dq)

        def two_tiles(ja, jb, carry):
            before, gbefore, dq = carry
            (za, daa), (zb, dab) = products(ja), products(jb)
            (spa, prea), (spb, preb) = softplus_sums(za, False), softplus_sums(zb, False)
            wa, ga, pga, before = weight_grads(za, daa, spa, prea, before, False)
            wb, gb, pgb, before = weight_grads(zb, dab, spb, preb, before, False)
            dza, gbefore = dlogits(spa, ga, pga, gbefore, False)
            dzb, gbefore = dlogits(spb, gb, pgb, gbefore, False)
            return before, gbefore, accumulate(jb, dzb, wb, accumulate(ja, dza, wa, dq))

        zero = jnp.zeros((1, BLK), F32)
        carry = _loop_two_tiles(i, tile, two_tiles, ((zero, zero), (zero, zero), jnp.zeros((LANES, BLK), F32)))
        dq_ref[...] = tile(i, carry, True)[2]

    acc = pl.BlockSpec((t, LANES), lambda h, i: (0, h))
    return _pcall(
        body, name="sb_bwd", grid=(p, nq),
        out_shape=[jax.ShapeDtypeStruct((dg, t), F32)] + [jax.ShapeDtypeStruct((t, dg), F32)] * 2,
        in_specs=[pl.BlockSpec((BLK, LANES), lambda h, i: (i, 3 * p + h)), _resident((nq, 2 * BLK, LANES)),
                  _resident((nq, LANES, 2 * BLK)), _resident((nq, 2 * BLK, LANES)),
                  pl.BlockSpec((BLK, LANES), lambda h, i: (i, h)), pl.BlockSpec((2, 1, BLK), lambda h, i: (h, 0, i))],
        out_specs=[pl.BlockSpec((LANES, BLK), lambda h, i: (h, i)), acc, acc],
        compiler_params=_params(("arbitrary", "arbitrary")),
    )(qkv, k_st, k_t, v_st, do, rtot)


def _tri_constants(nh, t):
    nb = t // LANES
    r = nh * nb
    li = np.arange(LANES)
    tri_in = (li[:, None] <= li[None, :])
    ri = np.arange(r)
    same = (ri[:, None] // nb) == (ri[None, :] // nb)
    blk = same & (ri[None, :] < ri[:, None])
    blk_rev = same & (ri[None, :] > ri[:, None])
    head_rows = (np.arange(max(8, nh))[:, None] == (ri[None, :] // nb))
    as_bf16 = lambda a: jnp.asarray(a.astype(np.float32), BF16)
    return as_bf16(tri_in), as_bf16(blk), as_bf16(tri_in.T), as_bf16(blk_rev), as_bf16(head_rows)


def kernel(x, c, w_ada, b_ada, g_attn, w_in, b_fgate, g_out_fox, g_out_sb, w_out, g_mlp, w_up, conv_w, conv_b, w_down, g_final, loss_target, m_w_ada, m_b_ada, m_g_attn, m_w_in, m_b_fgate, m_g_out_fox, m_g_out_sb, m_w_out, m_g_mlp, m_w_up, m_conv_w, m_conv_b, m_w_down, m_g_final, v_w_ada, v_b_ada, v_g_attn, v_w_in, v_b_fgate, v_g_out_fox, v_g_out_sb, v_w_out, v_g_mlp, v_w_up, v_conv_w, v_conv_b, v_w_down, v_g_final):
    t, d = x.shape[1], x.shape[2]
    dg = d // 2
    nh = dg // HEAD_DIM
    n_in = 6 * dg + nh
    dff = w_down.shape[1] * 4
    dfp = -(-dff // 256) * 256
    cf = 256
    tm = _tile(t, (512, 256, 128))
    nq = t // BLK
    xi, yi, ci = lax.axis_index("x"), lax.axis_index("y"), lax.axis_index("c")
    shard = 2 * xi + yi
    me = 4 * xi + 2 * yi + ci

    x2d, tg2d = x[0], loss_target[0]

    c_all = _all_gather8(jnp.pad(c, ((0, 7), (0, 0)))).reshape(8, 8, d)[:, 0, :]
    ada_cols = w_ada.shape[2]
    b_shard = lax.dynamic_slice(b_ada, (0, shard * ada_cols), (1, ada_cols))
    sc_all, mod_shard = _ada_fwd(c_all, w_ada[0], b_shard)
    mod_all = _all_gather8(mod_shard).reshape(4, 2, 8, ada_cols)
    mod_me = lax.dynamic_index_in_dim(mod_all[:, 0], me, axis=1, keepdims=False)
    mod8 = jnp.pad(mod_me.reshape(6, d), ((0, 2), (0, 0)))

    g_in, g_out, g_up, g_down, g_cw = _gather_xy(
        [w_in[0].astype(BF16), w_out[0].astype(BF16), w_up[0].astype(BF16), w_down[0].astype(BF16), conv_w[0]])
    w_in_full = jnp.transpose(g_in, (1, 0, 2)).reshape(d, n_in)
    w_qkv = w_in_full[:, :6 * dg]
    w_f = jnp.pad(w_in_full[:, 6 * dg:], ((0, 0), (0, LANES - nh)))
    w_out_full = g_out.reshape(2 * dg, d)
    w_up_full = jnp.transpose(g_up, (1, 0, 2)).reshape(d, 2 * dff)
    padc = ((0, 0), (0, dfp - dff))
    wg, wv = jnp.pad(w_up_full[:, :dff], padc), jnp.pad(w_up_full[:, dff:], padc)
    wd = jnp.pad(g_down.reshape(dff, d), ((0, dfp - dff), (0, 0)))
    cw_full = jnp.transpose(g_cw, (1, 0, 2)).reshape(3, 2 * dff)
    cw4 = jnp.concatenate([cw_full, conv_b], axis=0)
    cwg = jnp.pad(cw4[:, :dff], ((0, 4), (0, dfp - dff)))
    cwv = jnp.pad(cw4[:, dff:], ((0, 4), (0, dfp - dff)))

    qkv, fl, h1 = _in_proj_fwd(x2d, mod8, g_attn, w_qkv, w_f, tm)
    tri_in, tri_blk, tri_in_rev, tri_blk_rev, head_rows = _tri_constants(nh, t)
    fl2d = fl[:, :nh].T.reshape(nh * t // LANES, LANES)
    b_rows = jnp.repeat(b_fgate[0], t // LANES)[:, None]
    f2d = _fgate_fwd(fl2d, b_rows, tri_in, tri_blk)
    fcol = f2d.reshape(nh, t, 1)
    pairs = nh // 2
    qx, kx, kx_t, vf_st, vf_t, ks_st, ks_t, vs_st, vs_t = _attn_operands(qkv, fcol, dg)
    o_fox_t, lse = _fox_fwd(qx, kx, vf_t, dg)
    o_sb_t, rtot = _sb_fwd(qkv, ks_st, vs_t, dg)
    o_fox, o_sb = o_fox_t.T, o_sb_t.T
    li = np.arange(dg)
    bd = jnp.asarray((li[:, None] // HEAD_DIM == li[None, :] // HEAD_DIM).astype(np.float32), BF16)
    hsel = jnp.asarray((li[:, None] // HEAD_DIM == np.arange(LANES)[None, :]).astype(np.float32), BF16)
    x2, mix = _attn_out_fwd(x2d, o_fox, o_sb, g_out_fox, g_out_sb, w_out_full, mod8, bd, tm)
    g_final2 = g_final[None, :]
    dx3, h2, part_f = _ffn_fwd(x2, tg2d, mod8, g_mlp, g_final2, wg, wv, cwg, cwv, wd, tm, cf)

    dx2, dupg, dupv, act, dxg3, part_b, pcg, pcv = _ffn_bwd(x2, dx3, mod8, g_mlp, wg, wv, cwg, cwv, wd, tm, cf)
    do_fox, do_sb, delta, dxg2, part_o = _attn_out_bwd(dx2, mix, o_fox, o_sb, g_out_fox, g_out_sb, w_out_full, mod8, bd, hsel, tm)
    drow = delta[:, :nh].T.reshape(nh, 1, t)
    qxb = _fox_q_bwd(qkv, fcol, lse.reshape(nh, t, 1), dg)
    dq_f_t, dft, dk_f, dv_f, dkx = _fox_bwd(qxb, kx, kx_t, vf_st, do_fox, drow, dg)
    dq_s_t, dk_s, dv_s = _sb_bwd(qkv, ks_st, ks_t, vs_st, do_sb, rtot, dg)
    dq_f, dq_s = dq_f_t.T, dq_s_t.T
    f2d_shape = (nh * t // LANES, LANES)
    dfs = jnp.transpose(dkx.reshape(t, pairs, LANES)[:, :, :2], (1, 2, 0))
    dfl2d, gb8 = _fgate_bwd(fl2d, b_rows, dft.reshape(f2d_shape), dfs.reshape(f2d_shape), tri_in_rev, tri_blk_rev, head_rows)
    dfl = jnp.pad(dfl2d.reshape(nh, t).T, ((0, 0), (0, LANES - nh)))
    grad_x, dproj, dflb, part_i = _in_proj_bwd([dq_f, dk_f, dv_f, dq_s, dk_s, dv_s], dfl, w_qkv, w_f, x2d, dx2, mod8, g_attn, tm)

    gw_qkv = _matmul_tn(h1, dproj, "grad_w_qkv")
    gw_f = _matmul_tn(h1, dflb, "grad_w_f")
    gw_in = jnp.concatenate([gw_qkv, gw_f[:, :nh]], axis=1)
    gw_out = _matmul_tn(mix, dxg2, "grad_w_out")
    gw_upg = _matmul_tn(h2, dupg, "grad_w_up_gate")
    gw_upv = _matmul_tn(h2, dupv, "grad_w_up_val")
    gw_up = jnp.concatenate([gw_upg[:, :dff], gw_upv[:, :dff]], axis=1)
    gw_down = _matmul_tn(act, dxg3, "grad_w_down")[:dff]

    sf = _sum_leading(part_f, "sum_part_ffn_fwd")
    sb_ = _sum_leading(part_b, "sum_part_ffn_bwd")
    so = _sum_leading(part_o, "sum_part_attn_out")
    si = _sum_leading(part_i, "sum_part_in_proj")
    scg = _sum_leading(pcg, "sum_part_conv_gate")
    scv = _sum_leading(pcv, "sum_part_conv_val")
    gb_f = gb8[:nh, 0]
    dmod = jnp.concatenate([si[0], si[1], so[0], sb_[0], sb_[1], sf[1]])
    g_conv_w = jnp.concatenate([scg[0:3, :dff], scv[0:3, :dff]], axis=1).reshape(-1)
    g_conv_b = jnp.concatenate([scg[3, :dff], scv[3, :dff]])
    loss_part = jnp.sum(sf[2])
    fields = [dmod, si[2], gb_f, so[1, :dg], so[1, dg:], sb_[2], g_conv_b, sf[0], g_conv_w, loss_part[None]]
    sizes = [int(f.shape[0]) for f in fields]
    n_pack = sum(sizes)
    lanes_pack = -(-n_pack // (8 * LANES)) * LANES
    pack = jnp.pad(jnp.concatenate(fields), (0, 8 * lanes_pack - n_pack)).reshape(8, lanes_pack)
    gathered = _all_gather8(pack)
    tot = _sum_leading(gathered.reshape(8, 8, lanes_pack), "sum_pack").reshape(-1)
    offs = np.concatenate([[0], np.cumsum(sizes)])
    take = lambda k: tot[int(offs[k]):int(offs[k + 1])]
    g_b_ada, g_g_attn, g_b_fgate, g_g_fox, g_g_sb, g_g_mlp, g_cb, g_g_final, g_cw_full, loss_v = [take(k) for k in range(10)]
    loss = loss_v[0]
    dmod_all = gathered.reshape(8, 8 * lanes_pack)[:, :6 * d]
    dmod_cols = lax.dynamic_slice(dmod_all, (0, shard * ada_cols), (8, ada_cols))
    g_w_ada = _ada_bwd(sc_all.T, dmod_cols)

    def col_pieces(g):
        r, cc = g.shape
        return jnp.transpose(g.reshape(2, r // 2, 4, cc // 4), (2, 0, 1, 3)).reshape(8, r // 2, cc // 4)

    def row_pieces(g):
        r, cc = g.shape
        return g.reshape(8, r // 8, cc)

    pieces = (col_pieces(gw_in), row_pieces(gw_out), col_pieces(gw_up), row_pieces(gw_down))
    recv = _scatter8([_to_bf16(p, "pieces_bf16_" + nm) for p, nm in zip(pieces, ("w_in", "w_out", "w_up", "w_down"))])
    halves = [_sum_leading(rv, nm) for rv, nm in zip(recv, ("sum_w_in", "sum_w_out", "sum_w_up", "sum_w_down"))]
    swapped = _swap_halves(halves)
    g_w_in, g_w_out, g_w_up, g_w_down = [s.reshape(2 * s.shape[1], s.shape[2]) for s in swapped]
    g_conv_w_shard = lax.dynamic_slice(g_cw_full.reshape(3, 2 * dff), (0, shard * (dff // 2)), (3, dff // 2))

    grads, deltas, new_m, new_v = {}, {}, {}, {}

    def step(name, w, g, m, v):
        shape = w.shape
        as2d = lambda a: a.reshape(-1, shape[-1])
        dl, nm, nv = _adamw(as2d(w), as2d(g), as2d(m), as2d(v), "adamw_" + name)
        grads[name], deltas[name], new_m[name], new_v[name] = g.reshape(shape), dl.reshape(shape), nm.reshape(shape), nv.reshape(shape)

    step("w_ada", w_ada, g_w_ada, m_w_ada, v_w_ada)
    step("w_in", w_in, g_w_in, m_w_in, v_w_in)
    step("w_out", w_out, g_w_out, m_w_out, v_w_out)
    step("w_up", w_up, g_w_up, m_w_up, v_w_up)
    step("conv_w", conv_w, g_conv_w_shard, m_conv_w, v_conv_w)
    step("w_down", w_down, g_w_down, m_w_down, v_w_down)

    small = [("b_ada", b_ada, g_b_ada, m_b_ada, v_b_ada), ("g_attn", g_attn, g_g_attn, m_g_attn, v_g_attn),
             ("b_fgate", b_fgate, g_b_fgate, m_b_fgate, v_b_fgate), ("g_out_fox", g_out_fox, g_g_fox, m_g_out_fox, v_g_out_fox),
             ("g_out_sb", g_out_sb, g_g_sb, m_g_out_sb, v_g_out_sb), ("g_mlp", g_mlp, g_g_mlp, m_g_mlp, v_g_mlp),
             ("conv_b", conv_b, g_cb, m_conv_b, v_conv_b), ("g_final", g_final, g_g_final, m_g_final, v_g_final)]
    ssz = [int(np.prod(s[1].shape)) for s in small]
    n_small = sum(ssz)
    lanes_small = -(-n_small // (8 * LANES)) * LANES
    packs = [jnp.pad(jnp.concatenate([s[k].reshape(-1) for s in small]), (0, 8 * lanes_small - n_small)).reshape(8, lanes_small)
             for k in (1, 2, 3, 4)]
    dl_s, nm_s, nv_s = _adamw(*packs, "adamw_small")
    so_ = np.concatenate([[0], np.cumsum(ssz)])
    for k, s in enumerate(small):
        cut = lambda a: a.reshape(-1)[int(so_[k]):int(so_[k + 1])].reshape(s[1].shape)
        grads[s[0]], deltas[s[0]], new_m[s[0]], new_v[s[0]] = s[2].reshape(s[1].shape), cut(dl_s), cut(nm_s), cut(nv_s)

    order = ["w_ada", "b_ada", "g_attn", "w_in", "b_fgate", "g_out_fox", "g_out_sb", "w_out", "g_mlp", "w_up",
             "conv_w", "conv_b", "w_down", "g_final"]
    return (loss, grad_x[None], *[grads[n] for n in order], *[deltas[n] for n in order],
            *[new_m[n] for n in order], *[new_v[n] for n in order])
```

```python
import functools

import numpy as np
import jax
import jax.numpy as jnp
from jax import lax
from jax.experimental import pallas as pl
from jax.experimental.pallas import tpu as pltpu

F32 = jnp.float32
BF16 = jnp.bfloat16
MESH = pl.DeviceIdType.MESH

HEAD_DIM = 64
LANES = 128
EPS = 1e-6
NEG = -1e30
ADAM_LR, ADAM_B1, ADAM_B2, ADAM_EPS, ADAM_WD, ADAM_STEP = 0.001, 0.9, 0.999, 1e-08, 0.01, 10
V7X_VMEM_BYTES = 64 * 1024 * 1024
VMEM_LIMIT = V7X_VMEM_BYTES - 12 * 1024 * 1024
NT_DIMS = (((1,), (1,)), ((), ()))
TN_DIMS = (((0,), (0,)), ((), ()))


def _pcall(body, **kw):
    return pl.pallas_call(body, **kw)


def _params(sem=None, **kw):
    return pltpu.CompilerParams(dimension_semantics=sem, vmem_limit_bytes=VMEM_LIMIT, **kw)


def _split_dot(x, m, passes):
    acc = None
    for _ in range(passes):
        part = x.astype(BF16)
        d = jnp.dot(part, m, preferred_element_type=F32)
        acc = d if acc is None else acc + d
        x = x - part.astype(F32)
    return acc


def _tile(n, candidates):
    for t in candidates:
        if n % t == 0:
            return t
    return n


def _rows_tile(rows, row_bytes, budget=2 * 1024 * 1024):
    best = None
    for t in range(8, rows + 1, 8):
        if rows % t == 0 and t * row_bytes <= budget:
            best = t
    return best if best is not None else rows


def _all_gather8(v):
    m_per, n = v.shape

    def body(x_ref, out_ref, send_sems, recv_sems, local_sem):
        x, y, c = lax.axis_index("x"), lax.axis_index("y"), lax.axis_index("c")
        me, sibling = (x, y, c), (x, y, 1 - c)
        chips = [(1 - x, y), (x, 1 - y), (1 - x, 1 - y)]

        def rows(px, py, pc):
            return out_ref.at[pl.ds((4 * px + 2 * py + pc) * m_per, m_per), :]

        def copy(k, block, to, src=None):
            return pltpu.make_async_remote_copy(
                src_ref=rows(*block) if src is None else src, dst_ref=rows(*block),
                send_sem=send_sems.at[k], recv_sem=recv_sems.at[k], device_id=to, device_id_type=MESH)

        mine = pltpu.make_async_copy(x_ref, rows(*me), local_sem)
        mine.start()
        first = [copy(0, me, sibling, src=x_ref)]
        first += [copy(1 + j, me, (*chip, c), src=x_ref) for j, chip in enumerate(chips)]
        for cp in first:
            cp.start()
        passed = [copy(4 + j, (*chip, c), sibling) for j, chip in enumerate(chips)]
        for j, chip in enumerate(chips):
            copy(1 + j, (*chip, c), me).wait_recv()
            passed[j].start()
        copy(0, sibling, me).wait_recv()
        for j, chip in enumerate(chips):
            copy(4 + j, (*chip, 1 - c), me).wait_recv()
        for cp in first + passed:
            cp.wait_send()
        mine.wait()

    return _pcall(
        body, name="all_gather8",
        out_shape=jax.ShapeDtypeStruct((8 * m_per, n), v.dtype),
        in_specs=[pl.BlockSpec(memory_space=pltpu.VMEM)],
        out_specs=pl.BlockSpec(memory_space=pltpu.VMEM),
        scratch_shapes=[pltpu.SemaphoreType.DMA((7,)), pltpu.SemaphoreType.DMA((7,)), pltpu.SemaphoreType.DMA],
        compiler_params=pltpu.CompilerParams(vmem_limit_bytes=VMEM_LIMIT),
    )(v)


def _gather_xy(shards):
    n = len(shards)

    def body(*refs):
        ins, outs = refs[:n], refs[n:2 * n]
        send_sems, recv_sems, local_sems = refs[2 * n:]
        x, y, c = lax.axis_index("x"), lax.axis_index("y"), lax.axis_index("c")
        chips = [(1 - x, y), (x, 1 - y), (1 - x, 1 - y)]
        mine = 2 * x + y
        local, remote = [], []
        for w in range(n):
            cp = pltpu.make_async_copy(ins[w], outs[w].at[mine], local_sems.at[w])
            cp.start()
            local.append(cp)
            for k, (px, py) in enumerate(chips):
                cp = pltpu.make_async_remote_copy(
                    src_ref=ins[w], dst_ref=outs[w].at[mine], send_sem=send_sems.at[3 * w + k],
                    recv_sem=recv_sems.at[3 * w + k], device_id=(px, py, c), device_id_type=MESH)
                cp.start()
                remote.append(cp)
        for cp in remote:
            cp.wait_recv()
        for cp in remote:
            cp.wait_send()
        for cp in local:
            cp.wait()

    hbm = pl.BlockSpec(memory_space=pltpu.HBM)
    return _pcall(
        body, name="gather_xy",
        out_shape=[jax.ShapeDtypeStruct((4,) + s.shape, s.dtype) for s in shards],
        in_specs=[hbm] * n, out_specs=[hbm] * n,
        scratch_shapes=[pltpu.SemaphoreType.DMA((3 * n,)), pltpu.SemaphoreType.DMA((3 * n,)),
                        pltpu.SemaphoreType.DMA((n,))],
        compiler_params=pltpu.CompilerParams(vmem_limit_bytes=VMEM_LIMIT),
    )(*shards)


def _scatter8(pieces):
    n = len(pieces)

    def body(*refs):
        ins, outs = refs[:n], refs[n:2 * n]
        send_sems, recv_sems, local_sems = refs[2 * n:]
        x, y, c = lax.axis_index("x"), lax.axis_index("y"), lax.axis_index("c")
        me = 4 * x + 2 * y + c
        local, remote = [], []
        for w in range(n):
            cp = pltpu.make_async_copy(ins[w].at[me], outs[w].at[me], local_sems.at[w])
            cp.start()
            local.append(cp)
            for f in range(1, 8):
                px = 1 - x if f & 4 else x
                py = 1 - y if f & 2 else y
                pc = 1 - c if f & 1 else c
                cp = pltpu.make_async_remote_copy(
                    src_ref=ins[w].at[4 * px + 2 * py + pc], dst_ref=outs[w].at[me],
                    send_sem=send_sems.at[7 * w + f - 1], recv_sem=recv_sems.at[7 * w + f - 1],
                    device_id=(px, py, pc), device_id_type=MESH)
                cp.start()
                remote.append(cp)
        for cp in remote:
            cp.wait_recv()
        for cp in remote:
            cp.wait_send()
        for cp in local:
            cp.wait()

    hbm = pl.BlockSpec(memory_space=pltpu.HBM)
    return _pcall(
        body, name="scatter8",
        out_shape=[jax.ShapeDtypeStruct(p.shape, p.dtype) for p in pieces],
        in_specs=[hbm] * n, out_specs=[hbm] * n,
        scratch_shapes=[pltpu.SemaphoreType.DMA((7 * n,)), pltpu.SemaphoreType.DMA((7 * n,)),
                        pltpu.SemaphoreType.DMA((n,))],
        compiler_params=pltpu.CompilerParams(vmem_limit_bytes=VMEM_LIMIT),
    )(*pieces)


def _swap_halves(halves):
    n = len(halves)

    def body(*refs):
        ins, outs = refs[:n], refs[n:2 * n]
        send_sems, recv_sems, local_sems = refs[2 * n:]
        x, y, c = lax.axis_index("x"), lax.axis_index("y"), lax.axis_index("c")
        local, remote = [], []
        for w in range(n):
            cp = pltpu.make_async_copy(ins[w], outs[w].at[c], local_sems.at[w])
            cp.start()
            local.append(cp)
            cp = pltpu.make_async_remote_copy(
                src_ref=ins[w], dst_ref=outs[w].at[c], send_sem=send_sems.at[w], recv_sem=recv_sems.at[w],
                device_id=(x, y, 1 - c), device_id_type=MESH)
            cp.start()
            remote.append(cp)
        for cp in remote:
            cp.wait_recv()
        for cp in remote:
            cp.wait_send()
        for cp in local:
            cp.wait()

    hbm = pl.BlockSpec(memory_space=pltpu.HBM)
    return _pcall(
        body, name="swap_halves",
        out_shape=[jax.ShapeDtypeStruct((2,) + h.shape, h.dtype) for h in halves],
        in_specs=[hbm] * n, out_specs=[hbm] * n,
        scratch_shapes=[pltpu.SemaphoreType.DMA((n,)), pltpu.SemaphoreType.DMA((n,)), pltpu.SemaphoreType.DMA((n,))],
        compiler_params=pltpu.CompilerParams(vmem_limit_bytes=VMEM_LIMIT),
    )(*halves)


def _sum_leading(a, name):
    n, r, c = a.shape
    tr = _rows_tile(r, n * c * 4, budget=6 * 1024 * 1024)
    if a.dtype == BF16 and tr % 16:
        tr = r

    def body(a_ref, o_ref):
        acc = a_ref[0].astype(F32)
        for k in range(1, n):
            acc = acc + a_ref[k].astype(F32)
        o_ref[...] = acc

    return _pcall(
        body, name=name, grid=(r // tr,),
        out_shape=jax.ShapeDtypeStruct((r, c), F32),
        in_specs=[pl.BlockSpec((n, tr, c), lambda i: (0, i, 0))],
        out_specs=pl.BlockSpec((tr, c), lambda i: (i, 0)),
        compiler_params=_params(("arbitrary",)),
    )(a)


def _to_bf16(a, name):
    n, r, c = a.shape

    def body(a_ref, o_ref):
        o_ref[...] = a_ref[...].astype(BF16)

    spec = pl.BlockSpec((1, r, c), lambda i: (i, 0, 0))
    return _pcall(
        body, name=name, grid=(n,), out_shape=jax.ShapeDtypeStruct(a.shape, BF16),
        in_specs=[spec], out_specs=spec, compiler_params=_params(("arbitrary",)),
    )(a)


def _adamw(w, g, m, v, name):
    r, c = w.shape
    tr = _rows_tile(r, c * 4, budget=1024 * 1024)
    c1 = 1.0 - ADAM_B1 ** ADAM_STEP
    c2 = 1.0 - ADAM_B2 ** ADAM_STEP

    def body(w_ref, g_ref, m_ref, v_ref, d_ref, nm_ref, nv_ref):
        gg = g_ref[...]
        nm = ADAM_B1 * m_ref[...] + (1.0 - ADAM_B1) * gg
        nv = ADAM_B2 * v_ref[...] + (1.0 - ADAM_B2) * (gg * gg)
        m_hat = nm / c1
        v_hat = nv / c2
        d_ref[...] = -ADAM_LR * (m_hat / (jnp.sqrt(v_hat) + ADAM_EPS) + ADAM_WD * w_ref[...])
        nm_ref[...] = nm
        nv_ref[...] = nv

    spec = pl.BlockSpec((tr, c), lambda i: (i, 0))
    return _pcall(
        body, name=name, grid=(r // tr,),
        out_shape=[jax.ShapeDtypeStruct((r, c), F32)] * 3,
        in_specs=[spec] * 4, out_specs=[spec] * 3,
        compiler_params=_params(("arbitrary",)),
    )(w, g, m, v)


def _ada_fwd(c_all, w_shard, b_shard):
    nb, d = c_all.shape
    cols = w_shard.shape[1]

    def body(c_ref, w_ref, b_ref, sc_ref, mod_ref):
        cv = c_ref[...]
        sc = cv * jax.nn.sigmoid(cv)
        sc_ref[...] = sc
        mod_ref[...] = jnp.dot(sc.astype(BF16), w_ref[...].astype(BF16), preferred_element_type=F32) + b_ref[...]

    return _pcall(
        body, name="ada_fwd",
        out_shape=[jax.ShapeDtypeStruct((nb, d), F32), jax.ShapeDtypeStruct((nb, cols), F32)],
        compiler_params=pltpu.CompilerParams(vmem_limit_bytes=VMEM_LIMIT),
    )(c_all, w_shard, b_shard)


def _ada_bwd(sc_t, dmod_cols):
    d, nb = sc_t.shape
    cols = dmod_cols.shape[1]
    tr = _rows_tile(d, cols * 4, budget=1024 * 1024)

    def body(s_ref, m_ref, o_ref):
        s = s_ref[...]
        m = m_ref[...]
        acc = s[:, 0:1] * m[0:1, :]
        for b in range(1, nb):
            acc = acc + s[:, b:b + 1] * m[b:b + 1, :]
        o_ref[...] = acc

    return _pcall(
        body, name="ada_bwd", grid=(d // tr,),
        out_shape=jax.ShapeDtypeStruct((d, cols), F32),
        in_specs=[pl.BlockSpec((tr, nb), lambda i: (i, 0)), pl.BlockSpec((nb, cols), lambda i: (0, 0))],
        out_specs=pl.BlockSpec((tr, cols), lambda i: (i, 0)),
        compiler_params=_params(("arbitrary",)),
    )(sc_t, dmod_cols)


def _log_sigmoid(x):
    return jnp.minimum(x, 0.0) - jnp.log1p(jnp.exp(-jnp.abs(x)))


def _fgate_fwd(fl2d, b_rows, tri_in, tri_blk):
    r = fl2d.shape[0]

    def body(x_ref, b_ref, u_ref, l_ref, f_ref):
        lf = _log_sigmoid(x_ref[...] + b_ref[...])
        c1 = _split_dot(lf, u_ref[...], 3)
        tot = jnp.broadcast_to(c1[:, LANES - 1:LANES], (r, LANES))
        acc = None
        for _ in range(3):
            part = tot.astype(BF16)
            dd = jnp.dot(l_ref[...], part, preferred_element_type=F32)
            acc = dd if acc is None else acc + dd
            tot = tot - part.astype(F32)
        f_ref[...] = c1 + acc

    return _pcall(
        body, name="fgate_fwd", out_shape=jax.ShapeDtypeStruct((r, LANES), F32),
        compiler_params=pltpu.CompilerParams(vmem_limit_bytes=VMEM_LIMIT),
    )(fl2d, b_rows, tri_in, tri_blk)


def _fgate_bwd(fl2d, b_rows, df_query, df_key, tri_in_rev, tri_blk_rev, head_rows):
    r = fl2d.shape[0]
    nhp = head_rows.shape[0]

    def body(x_ref, b_ref, dq_ref, dk_ref, u_ref, l_ref, hr_ref, o_ref, gb_ref):
        c1 = _split_dot(dq_ref[...] + dk_ref[...], u_ref[...], 3)
        tot = jnp.broadcast_to(c1[:, 0:1], (r, LANES))
        acc = None
        for _ in range(3):
            part = tot.astype(BF16)
            dd = jnp.dot(l_ref[...], part, preferred_element_type=F32)
            acc = dd if acc is None else acc + dd
            tot = tot - part.astype(F32)
        x = x_ref[...] + b_ref[...]
        e = jnp.exp(-jnp.abs(x))
        dfl = (c1 + acc) * (jnp.where(x >= 0, e, 1.0) / (1.0 + e))
        o_ref[...] = dfl
        rs = jnp.broadcast_to(jnp.sum(dfl, axis=1, keepdims=True), (r, LANES))
        gb = None
        for _ in range(3):
            part = rs.astype(BF16)
            dd = jnp.dot(hr_ref[...], part, preferred_element_type=F32)
            gb = dd if gb is None else gb + dd
            rs = rs - part.astype(F32)
        gb_ref[...] = gb

    return _pcall(
        body, name="fgate_bwd",
        out_shape=[jax.ShapeDtypeStruct((r, LANES), F32), jax.ShapeDtypeStruct((nhp, LANES), F32)],
        compiler_params=pltpu.CompilerParams(vmem_limit_bytes=VMEM_LIMIT),
    )(fl2d, b_rows, df_query, df_key, tri_in_rev, tri_blk_rev, head_rows)


def _norm_mod(x, g, scale, shift):
    r = lax.rsqrt(jnp.mean(x * x, axis=-1, keepdims=True) + EPS)
    return (x * r * g) * (1.0 + scale) + shift


def _norm_mod_bwd(x, dh, g, scale):
    r = lax.rsqrt(jnp.mean(x * x, axis=-1, keepdims=True) + EPS)
    xn = x * r
    dshift = jnp.sum(dh, axis=0, keepdims=True)
    dscale = jnp.sum(dh * (xn * g), axis=0, keepdims=True)
    dxn_g = dh * (1.0 + scale)
    dg = jnp.sum(dxn_g * xn, axis=0, keepdims=True)
    dxn = dxn_g * g
    dx = r * (dxn - xn * jnp.mean(dxn * xn, axis=-1, keepdims=True))
    return dx, dshift, dscale, dg


def _in_proj_fwd(x, mod8, g_attn, w_qkv, w_f, tm):
    t, d = x.shape
    dg = w_qkv.shape[1] // 6

    def body(x_ref, mod_ref, g_ref, w_ref, wf_ref, qkv_ref, fl_ref, h1_ref, h_sc):
        j = pl.program_id(1)

        @pl.when(j == 0)
        def _():
            h = _norm_mod(x_ref[...], g_ref[...], mod_ref[1:2, :], mod_ref[0:1, :]).astype(BF16)
            h_sc[...] = h
            h1_ref[...] = h
            fl_ref[...] = jnp.dot(h, wf_ref[...], preferred_element_type=F32)

        s = jnp.where((j == 0) | (j == 3), HEAD_DIM ** -0.5, 1.0)
        qkv_ref[...] = (jnp.dot(h_sc[...], w_ref[...], preferred_element_type=F32) * s).astype(BF16)

    return _pcall(
        body, name="in_proj_fwd", grid=(t // tm, 6),
        out_shape=[jax.ShapeDtypeStruct((t, 6 * dg), BF16), jax.ShapeDtypeStruct((t, LANES), F32),
                   jax.ShapeDtypeStruct((t, d), BF16)],
        in_specs=[pl.BlockSpec((tm, d), lambda i, j: (i, 0)), pl.BlockSpec((8, d), lambda i, j: (0, 0)),
                  pl.BlockSpec((1, d), lambda i, j: (0, 0)), pl.BlockSpec((d, dg), lambda i, j: (0, j)),
                  pl.BlockSpec((d, LANES), lambda i, j: (0, 0))],
        out_specs=[pl.BlockSpec((tm, dg), lambda i, j: (i, j)), pl.BlockSpec((tm, LANES), lambda i, j: (i, 0)),
                   pl.BlockSpec((tm, d), lambda i, j: (i, 0))],
        scratch_shapes=[pltpu.VMEM((tm, d), BF16)],
        compiler_params=_params(("arbitrary", "arbitrary")),
    )(x, mod8, g_attn, w_qkv, w_f)


def _head_rstd(o, bd):
    return lax.rsqrt(_split_dot(o * o, bd, 3) * (1.0 / HEAD_DIM) + EPS)


def _attn_out_fwd(x, o_fox, o_sb, g_fox, g_sb, w_out, mod8, bd, tm):
    t, d = x.shape
    dg = o_fox.shape[1]

    def body(x_ref, of_ref, os_ref, gf_ref, gs_ref, w_ref, mod_ref, bd_ref, x2_ref, mix_ref):
        of, osb = of_ref[...], os_ref[...]
        mf = (of * _head_rstd(of, bd_ref[...]) * gf_ref[...]).astype(BF16)
        ms = (osb * _head_rstd(osb, bd_ref[...]) * gs_ref[...]).astype(BF16)
        mix_ref[:, :dg] = mf
        mix_ref[:, dg:] = ms
        y = jnp.dot(mf, w_ref[:dg, :], preferred_element_type=F32) + jnp.dot(ms, w_ref[dg:, :], preferred_element_type=F32)
        x2_ref[...] = x_ref[...] + mod_ref[2:3, :] * y

    row = lambda w: pl.BlockSpec((tm, w), lambda i: (i, 0))
    full = lambda a: pl.BlockSpec(a.shape, lambda i: (0,) * a.ndim)
    return _pcall(
        body, name="attn_out_fwd", grid=(t // tm,),
        out_shape=[jax.ShapeDtypeStruct((t, d), F32), jax.ShapeDtypeStruct((t, 2 * dg), BF16)],
        in_specs=[row(d), row(dg), row(dg), full(g_fox), full(g_sb), full(w_out), full(mod8), full(bd)],
        out_specs=[row(d), row(2 * dg)],
        compiler_params=_params(("arbitrary",)),
    )(x, o_fox, o_sb, g_fox, g_sb, w_out, mod8, bd)


def _attn_out_bwd(dx2, mix, o_fox, o_sb, g_fox, g_sb, w_out, mod8, bd, hsel, tm):
    t, d = dx2.shape
    dg = o_fox.shape[1]

    def body(dx_ref, mix_ref, of_ref, os_ref, gf_ref, gs_ref, w_ref, mod_ref, bd_ref, hs_ref,
             dof_ref, dos_ref, dlt_ref, dxg_ref, part_ref):
        dx = dx_ref[...]
        gate = mod_ref[2:3, :]
        dxg = (dx * gate).astype(BF16)
        dxg_ref[...] = dxg
        mixv = mix_ref[...]
        y = jnp.dot(mixv[:, :dg], w_ref[:dg, :], preferred_element_type=F32)
        y = y + jnp.dot(mixv[:, dg:], w_ref[dg:, :], preferred_element_type=F32)
        part_ref[0] = jnp.zeros((8, d), F32)
        part_ref[0, 0:1, :] = jnp.sum(dx * y, axis=0, keepdims=True)
        for grp, (o_ref, g_ref, do_ref) in enumerate(((of_ref, gf_ref, dof_ref), (os_ref, gs_ref, dos_ref))):
            dmix = lax.dot_general(dxg, w_ref[grp * dg:(grp + 1) * dg, :], NT_DIMS, preferred_element_type=F32)
            o = o_ref[...]
            r = _head_rstd(o, bd_ref[...])
            n = o * r
            part_ref[0, 1:2, grp * dg:(grp + 1) * dg] = jnp.sum(dmix * n, axis=0, keepdims=True)
            dn = dmix * g_ref[...]
            mh = _split_dot(dn * n, bd_ref[...], 3) * (1.0 / HEAD_DIM)
            do = r * (dn - n * mh)
            do_ref[...] = do.astype(BF16)
            if grp == 0:
                dlt_ref[...] = _split_dot(do * o, hs_ref[...], 3)

    row = lambda w: pl.BlockSpec((tm, w), lambda i: (i, 0))
    full = lambda a: pl.BlockSpec(a.shape, lambda i: (0,) * a.ndim)
    nt = t // tm
    return _pcall(
        body, name="attn_out_bwd", grid=(nt,),
        out_shape=[jax.ShapeDtypeStruct((t, dg), BF16), jax.ShapeDtypeStruct((t, dg), BF16),
                   jax.ShapeDtypeStruct((t, LANES), F32), jax.ShapeDtypeStruct((t, d), BF16),
                   jax.ShapeDtypeStruct((nt, 8, d), F32)],
        in_specs=[row(d), row(2 * dg), row(dg), row(dg), full(g_fox), full(g_sb), full(w_out), full(mod8),
                  full(bd), full(hsel)],
        out_specs=[row(dg), row(dg), row(LANES), row(d), pl.BlockSpec((1, 8, d), lambda i: (i, 0, 0))],
        compiler_params=_params(("arbitrary",)),
    )(dx2, mix, o_fox, o_sb, g_fox, g_sb, w_out, mod8, bd, hsel)


def _in_proj_bwd(dparts, dfl, w_qkv, w_f, x, dx2, mod8, g_attn, tm):
    t, d = x.shape
    dg = dparts[0].shape[1]

    def body(*refs):
        d_refs = refs[:6]
        dfl_ref, w_ref, wf_ref, x_ref, dx2_ref, mod_ref, g_ref, gx_ref, dp_ref, dflb_ref, part_ref = refs[6:]
        dh = None
        for k in range(6):
            dk = d_refs[k][...]
            if k in (0, 3):
                dk = dk * HEAD_DIM ** -0.5
            db = dk.astype(BF16)
            dp_ref[:, k * dg:(k + 1) * dg] = db
            term = lax.dot_general(db, w_ref[:, k * dg:(k + 1) * dg], NT_DIMS, preferred_element_type=F32)
            dh = term if dh is None else dh + term
        dfb = dfl_ref[...].astype(BF16)
        dflb_ref[...] = dfb
        dh = dh + lax.dot_general(dfb, wf_ref[...], NT_DIMS, preferred_element_type=F32)
        dx, dshift, dscale, dgn = _norm_mod_bwd(x_ref[...], dh, g_ref[...], mod_ref[1:2, :])
        gx_ref[...] = dx2_ref[...] + dx
        part_ref[0] = jnp.zeros((8, d), F32)
        part_ref[0, 0:1, :] = dshift
        part_ref[0, 1:2, :] = dscale
        part_ref[0, 2:3, :] = dgn

    row = lambda w: pl.BlockSpec((tm, w), lambda i: (i, 0))
    full = lambda a: pl.BlockSpec(a.shape, lambda i: (0,) * a.ndim)
    nt = t // tm
    return _pcall(
        body, name="in_proj_bwd", grid=(nt,),
        out_shape=[jax.ShapeDtypeStruct((t, d), F32), jax.ShapeDtypeStruct((t, 6 * dg), BF16),
                   jax.ShapeDtypeStruct((t, LANES), BF16), jax.ShapeDtypeStruct((nt, 8, d), F32)],
        in_specs=[row(dg)] * 6 + [row(LANES), full(w_qkv), full(w_f), row(d), row(d), full(mod8), full(g_attn)],
        out_specs=[row(d), row(6 * dg), row(LANES), pl.BlockSpec((1, 8, d), lambda i: (i, 0, 0))],
        compiler_params=_params(("arbitrary",)),
    )(*dparts, dfl, w_qkv, w_f, x, dx2, mod8, g_attn)


def _matmul_tn(a, b, name):
    t, m = a.shape
    n = b.shape[1]
    a_t = a.T
    tm_ = _tile(m, (512, 256, 128))
    tn_ = _tile(n, (1024, 512, 256, 128))
    tk = _tile(t, (2048, 1024, 512, 256, 128))
    nk = t // tk

    def body(a_ref, b_ref, o_ref):
        k = pl.program_id(2)

        @pl.when(k == 0)
        def _():
            o_ref[...] = jnp.zeros_like(o_ref)

        o_ref[...] += jnp.dot(a_ref[...], b_ref[...], preferred_element_type=F32)

    return _pcall(
        body, name=name, grid=(m // tm_, n // tn_, nk),
        out_shape=jax.ShapeDtypeStruct((m, n), F32),
        in_specs=[pl.BlockSpec((tm_, tk), lambda i, j, k: (i, k)), pl.BlockSpec((tk, tn_), lambda i, j, k: (k, j))],
        out_specs=pl.BlockSpec((tm_, tn_), lambda i, j, k: (i, j)),
        compiler_params=_params(("arbitrary", "arbitrary", "arbitrary")),
    )(a_t, b)


HALO = 16


def _conv_taps(up_ext, cw, lo, rows):
    s1 = pltpu.roll(up_ext, 1, 0)
    s2 = pltpu.roll(up_ext, 2, 0)
    u = cw[2:3, :] * up_ext[lo:lo + rows] + cw[1:2, :] * s1[lo:lo + rows] + cw[0:1, :] * s2[lo:lo + rows] + cw[3:4, :]
    return u, s1, s2


def _ffn_fwd(x2, target, mod8, g_mlp, g_final, wg, wv, cwg, cwv, wd, tm, cf):
    t, d = x2.shape
    dfp = wg.shape[1]
    nt, nc = t // tm, dfp // cf
    hb = tm // HALO

    def body(x_ref, xp_ref, tg_ref, mod_ref, g_ref, gf_ref, wg_ref, wv_ref, cg_ref, cv_ref, wd_ref,
             dx3_ref, h2_ref, part_ref, h_sc, acc_sc):
        i, j = pl.program_id(0), pl.program_id(1)

        @pl.when(j == 0)
        def _():
            xe = jnp.concatenate([xp_ref[...], x_ref[...]], axis=0)
            h = _norm_mod(xe, g_ref[...], mod_ref[4:5, :], mod_ref[3:4, :]).astype(BF16)
            h_sc[...] = h
            h2_ref[...] = h[HALO:]
            acc_sc[...] = jnp.zeros_like(acc_sc)

        rowi = lax.broadcasted_iota(jnp.int32, (tm + HALO, 1), 0)
        keep = (rowi >= HALO) | (i > 0)
        hv = h_sc[...]
        upg = jnp.where(keep, jnp.dot(hv, wg_ref[...], preferred_element_type=F32), 0.0)
        upv = jnp.where(keep, jnp.dot(hv, wv_ref[...], preferred_element_type=F32), 0.0)
        ug, _, _ = _conv_taps(upg, cg_ref[...], HALO, tm)
        uv, _, _ = _conv_taps(upv, cv_ref[...], HALO, tm)
        act = (ug * jax.nn.sigmoid(ug) * uv).astype(BF16)
        acc_sc[...] += jnp.dot(act, wd_ref[...], preferred_element_type=F32)

        @pl.when(j == nc - 1)
        def _():
            y_ffn = acc_sc[...]
            x3 = x_ref[...] + mod_ref[5:6, :] * y_ffn
            r3 = lax.rsqrt(jnp.mean(x3 * x3, axis=-1, keepdims=True) + EPS)
            xn = x3 * r3
            gf = gf_ref[...]
            diff = xn * gf - tg_ref[...]
            dy = diff * (1.0 / d)
            dxn = dy * gf
            dx3 = r3 * (dxn - xn * jnp.mean(dxn * xn, axis=-1, keepdims=True))
            dx3_ref[...] = dx3
            part_ref[0] = jnp.zeros((8, d), F32)
            part_ref[0, 0:1, :] = jnp.sum(dy * xn, axis=0, keepdims=True)
            part_ref[0, 1:2, :] = jnp.sum(dx3 * y_ffn, axis=0, keepdims=True)
            part_ref[0, 2:3, :] = jnp.sum(diff * diff, axis=0, keepdims=True) * (0.5 / d)

    row = lambda w: pl.BlockSpec((tm, w), lambda i, j: (i, 0))
    full = lambda a: pl.BlockSpec(a.shape, lambda i, j: (0,) * a.ndim)
    return _pcall(
        body, name="ffn_fwd", grid=(nt, nc),
        out_shape=[jax.ShapeDtypeStruct((t, d), F32), jax.ShapeDtypeStruct((t, d), BF16),
                   jax.ShapeDtypeStruct((nt, 8, d), F32)],
        in_specs=[row(d), pl.BlockSpec((HALO, d), lambda i, j: (jnp.maximum(i * hb - 1, 0), 0)), row(d),
                  full(mod8), full(g_mlp), full(g_final),
                  pl.BlockSpec((d, cf), lambda i, j: (0, j)), pl.BlockSpec((d, cf), lambda i, j: (0, j)),
                  pl.BlockSpec((8, cf), lambda i, j: (0, j)), pl.BlockSpec((8, cf), lambda i, j: (0, j)),
                  pl.BlockSpec((cf, d), lambda i, j: (j, 0))],
        out_specs=[row(d), row(d), pl.BlockSpec((1, 8, d), lambda i, j: (i, 0, 0))],
        scratch_shapes=[pltpu.VMEM((tm + HALO, d), BF16), pltpu.VMEM((tm, d), F32)],
        compiler_params=_params(("arbitrary", "arbitrary")),
    )(x2, x2, target, mod8, g_mlp, g_final, wg, wv, cwg, cwv, wd)


def _ffn_bwd(x2, dx3, mod8, g_mlp, wg, wv, cwg, cwv, wd, tm, cf):
    t, d = x2.shape
    dfp = wg.shape[1]
    nt, nc = t // tm, dfp // cf
    hb = tm // HALO
    nhb = t // HALO
    ext = tm + 2 * HALO

    def body(x_ref, xp_ref, xn_ref, dx_ref, dxn_ref, mod_ref, g_ref, wg_ref, wv_ref, cg_ref, cv_ref, wd_ref,
             dx2_ref, dug_ref, duv_ref, act_ref, dxg_ref, part_ref, pcg_ref, pcv_ref, h_sc, dg_sc, dh_sc):
        i, j = pl.program_id(0), pl.program_id(1)

        @pl.when(j == 0)
        def _():
            xe = jnp.concatenate([xp_ref[...], x_ref[...], xn_ref[...]], axis=0)
            h_sc[...] = _norm_mod(xe, g_ref[...], mod_ref[4:5, :], mod_ref[3:4, :]).astype(BF16)
            de = (jnp.concatenate([dx_ref[...], dxn_ref[...]], axis=0) * mod_ref[5:6, :]).astype(BF16)
            dg_sc[...] = de
            dxg_ref[...] = de[:tm]
            dh_sc[...] = jnp.zeros_like(dh_sc)

        rowe = lax.broadcasted_iota(jnp.int32, (ext, 1), 0)
        keep_up = (rowe >= HALO) | (i > 0)
        rowu = lax.broadcasted_iota(jnp.int32, (tm + HALO, 1), 0)
        keep_du = (rowu < tm) | (i < nt - 1)
        hv = h_sc[...]
        upg = jnp.where(keep_up, jnp.dot(hv, wg_ref[...], preferred_element_type=F32), 0.0)
        upv = jnp.where(keep_up, jnp.dot(hv, wv_ref[...], preferred_element_type=F32), 0.0)
        cg, cv = cg_ref[...], cv_ref[...]
        ug, g1, g2 = _conv_taps(upg, cg, HALO, tm + HALO)
        uv, v1, v2 = _conv_taps(upv, cv, HALO, tm + HALO)
        dact = lax.dot_general(dg_sc[...], wd_ref[...], NT_DIMS, preferred_element_type=F32)
        sg = jax.nn.sigmoid(ug)
        sil = ug * sg
        act_ref[...] = (sil * uv)[:tm].astype(BF16)
        duv = jnp.where(keep_du, dact * sil, 0.0)
        dug = jnp.where(keep_du, dact * uv * (sg * (1.0 + ug * (1.0 - sg))), 0.0)

        def back(du, cw, up, s1, s2, pc_ref):
            n = tm + HALO
            dup = (cw[2:3, :] * du + cw[1:2, :] * pltpu.roll(du, n - 1, 0) + cw[0:1, :] * pltpu.roll(du, n - 2, 0))[:tm]
            dut = du[:tm]
            pc_ref[0] = jnp.zeros((8, cf), F32)
            pc_ref[0, 0:1, :] = jnp.sum(dut * s2[HALO:HALO + tm], axis=0, keepdims=True)
            pc_ref[0, 1:2, :] = jnp.sum(dut * s1[HALO:HALO + tm], axis=0, keepdims=True)
            pc_ref[0, 2:3, :] = jnp.sum(dut * up[HALO:HALO + tm], axis=0, keepdims=True)
            pc_ref[0, 3:4, :] = jnp.sum(dut, axis=0, keepdims=True)
            return dup.astype(BF16)

        dupg = back(dug, cg, upg, g1, g2, pcg_ref)
        dupv = back(duv, cv, upv, v1, v2, pcv_ref)
        dug_ref[...] = dupg
        duv_ref[...] = dupv
        dh_sc[...] += (lax.dot_general(dupg, wg_ref[...], NT_DIMS, preferred_element_type=F32)
                       + lax.dot_general(dupv, wv_ref[...], NT_DIMS, preferred_element_type=F32))

        @pl.when(j == nc - 1)
        def _():
            dx, dshift, dscale, dgn = _norm_mod_bwd(x_ref[...], dh_sc[...], g_ref[...], mod_ref[4:5, :])
            dx2_ref[...] = dx_ref[...] + dx
            part_ref[0] = jnp.zeros((8, d), F32)
            part_ref[0, 0:1, :] = dshift
            part_ref[0, 1:2, :] = dscale
            part_ref[0, 2:3, :] = dgn

    row = lambda w: pl.BlockSpec((tm, w), lambda i, j: (i, 0))
    prev = pl.BlockSpec((HALO, d), lambda i, j: (jnp.maximum(i * hb - 1, 0), 0))
    nxt = pl.BlockSpec((HALO, d), lambda i, j: (jnp.minimum((i + 1) * hb, nhb - 1), 0))
    full = lambda a: pl.BlockSpec(a.shape, lambda i, j: (0,) * a.ndim)
    chunk = pl.BlockSpec((tm, cf), lambda i, j: (i, j))
    pchunk = pl.BlockSpec((1, 8, cf), lambda i, j: (i, 0, j))
    return _pcall(
        body, name="ffn_bwd", grid=(nt, nc),
        out_shape=[jax.ShapeDtypeStruct((t, d), F32), jax.ShapeDtypeStruct((t, dfp), BF16),
                   jax.ShapeDtypeStruct((t, dfp), BF16), jax.ShapeDtypeStruct((t, dfp), BF16),
                   jax.ShapeDtypeStruct((t, d), BF16), jax.ShapeDtypeStruct((nt, 8, d), F32),
                   jax.ShapeDtypeStruct((nt, 8, dfp), F32), jax.ShapeDtypeStruct((nt, 8, dfp), F32)],
        in_specs=[row(d), prev, nxt, row(d), nxt, full(mod8), full(g_mlp),
                  pl.BlockSpec((d, cf), lambda i, j: (0, j)), pl.BlockSpec((d, cf), lambda i, j: (0, j)),
                  pl.BlockSpec((8, cf), lambda i, j: (0, j)), pl.BlockSpec((8, cf), lambda i, j: (0, j)),
                  pl.BlockSpec((cf, d), lambda i, j: (j, 0))],
        out_specs=[row(d), chunk, chunk, chunk, row(d), pl.BlockSpec((1, 8, d), lambda i, j: (i, 0, 0)), pchunk, pchunk],
        scratch_shapes=[pltpu.VMEM((ext, d), BF16), pltpu.VMEM((tm + HALO, d), BF16), pltpu.VMEM((tm, d), F32)],
        compiler_params=_params(("arbitrary", "arbitrary")),
    )(x2, x2, x2, dx3, dx3, mod8, g_mlp, wg, wv, cwg, cwv, wd)


def _head_masks():
    lane = lax.broadcasted_iota(jnp.int32, (1, LANES), 1)
    in_a = lane < HEAD_DIM
    return in_a, jnp.logical_not(in_a)


BLK = 2 * LANES


def _stack_heads(qkv, dg):
    t = qkv.shape[0]
    p = dg // LANES
    rows = _tile(t, (512, 256, 128))
    sub = rows // BLK

    def body(kf_ref, vf_ref, ks_ref, vs_ref, okf, ovf, oks, ovs):
        in_a, in_b = _head_masks()
        for src, dst in ((kf_ref, okf), (vf_ref, ovf), (ks_ref, oks), (vs_ref, ovs)):
            v = src[...]
            zero = jnp.zeros_like(v)
            va, vb = jnp.where(in_a, v, zero), jnp.where(in_b, v, zero)
            for s in range(sub):
                dst[0, s, :BLK, :] = va[s * BLK:(s + 1) * BLK]
                dst[0, s, BLK:, :] = vb[s * BLK:(s + 1) * BLK]

    col = lambda base: pl.BlockSpec((rows, LANES), lambda h, j: (j, base * p + h))
    out = pl.BlockSpec((1, sub, 2 * BLK, LANES), lambda h, j: (h, j, 0, 0))
    shape = jax.ShapeDtypeStruct((p, t // BLK, 2 * BLK, LANES), BF16)
    return _pcall(
        body, name="stack_heads", grid=(p, t // rows),
        out_shape=[shape] * 4, in_specs=[col(1), col(2), col(4), col(5)], out_specs=[out] * 4,
        compiler_params=_params(("arbitrary", "arbitrary")),
    )(qkv, qkv, qkv, qkv)


def _tile_masks():
    rowi = lax.broadcasted_iota(jnp.int32, (BLK, BLK), 0)
    coli = lax.broadcasted_iota(jnp.int32, (BLK, BLK), 1)
    return coli <= rowi, coli < rowi


def _pair_triangle(suffix):
    r = lax.broadcasted_iota(jnp.int32, (BLK, BLK), 0)
    c = lax.broadcasted_iota(jnp.int32, (BLK, BLK), 1)
    return ((r >= c) if suffix else (r <= c)).astype(BF16)


def _pair_cumsum(x2, tri, passes):
    return jnp.concatenate([_split_dot(x2[:, :BLK], tri, passes), _split_dot(x2[:, BLK:], tri, passes)], axis=1)


def _pair_specs(t, dg, base):
    p = dg // LANES
    q = pl.BlockSpec((BLK, LANES), lambda h, i: (i, base * p + h))
    kv = pl.BlockSpec((1, t // BLK, 2 * BLK, LANES), lambda h, i: (h, 0, 0, 0))
    return q, kv


def _fox_fwd(qkv, kst, vst, fcol, frow2, dg):
    t = qkv.shape[0]
    p, nq = dg // LANES, t // BLK
    nh = 2 * p

    def body(q_ref, k_ref, v_ref, ft_ref, fs_ref, o_ref, lse_ref):
        i = pl.program_id(1)
        in_a, _ = _head_masks()
        causal, _ = _tile_masks()
        q2 = q_ref[...]
        ft = tuple(jnp.broadcast_to(ft_ref[a], (BLK, BLK)) for a in range(2))

        def tile(j, carry, masked):
            m, l, acc = carry
            kb, vb = k_ref[0, j], v_ref[0, j]
            s2 = lax.dot_general(q2, kb, NT_DIMS, preferred_element_type=F32)
            fs = fs_ref[0, j]
            m_new, l_new, alpha, pr = [], [], [], []
            for a in range(2):
                sl = slice(a * BLK, (a + 1) * BLK)
                s = (s2[:, sl] + ft[a]) - fs[:, sl]
                if masked:
                    s = jnp.where(causal, s, NEG)
                mn = jnp.maximum(m[a], jnp.max(s, axis=1, keepdims=True))
                pa = jnp.exp(s - mn)
                al = jnp.exp(m[a] - mn)
                m_new.append(mn)
                alpha.append(al)
                l_new.append(al * l[a] + jnp.sum(pa, axis=1, keepdims=True))
                pr.append(pa.astype(BF16))
            acc = jnp.where(in_a, alpha[0], alpha[1]) * acc + jnp.dot(
                jnp.concatenate(pr, axis=1), vb, preferred_element_type=F32)
            return tuple(m_new), tuple(l_new), acc

        neg, zero = jnp.full((BLK, 1), NEG, F32), jnp.zeros((BLK, 1), F32)
        carry = lax.fori_loop(0, i, functools.partial(tile, masked=False), ((neg, neg), (zero, zero), jnp.zeros((BLK, LANES), F32)))
        m, l, acc = tile(i, carry, True)
        o_ref[...] = acc / jnp.where(in_a, l[0], l[1])
        lse_ref[0] = m[0] + jnp.log(l[0])
        lse_ref[1] = m[1] + jnp.log(l[1])

    qs, kv = _pair_specs(t, dg, 0)
    col = pl.BlockSpec((2, BLK, 1), lambda h, i: (h, i, 0))
    return _pcall(
        body, name="fox_fwd", grid=(p, nq),
        out_shape=[jax.ShapeDtypeStruct((t, dg), F32), jax.ShapeDtypeStruct((nh, t, 1), F32)],
        in_specs=[qs, kv, kv, col, pl.BlockSpec((1, nq, 1, 2 * BLK), lambda h, i: (h, 0, 0, 0))],
        out_specs=[pl.BlockSpec((BLK, LANES), lambda h, i: (i, h)), col],
        compiler_params=_params(("arbitrary", "arbitrary")),
    )(qkv, kst, vst, fcol, frow2)


def _fold_heads(stacked, in_a):
    return jnp.where(in_a, stacked[:BLK], stacked[BLK:])


def _fox_bwd(qkv, kst, vst, do, fcol, frow2, lse, delta, dg):
    t = qkv.shape[0]
    p, nq = dg // LANES, t // BLK
    nh = 2 * p

    def body(q_ref, k_ref, v_ref, do_ref, ft_ref, fs_ref, lse_ref, dl_ref, dq_ref, dk_ref, dv_ref, dfs_ref, dft_ref):
        i = pl.program_id(1)

        @pl.when(i == 0)
        def _():
            dk_ref[...] = jnp.zeros_like(dk_ref)
            dv_ref[...] = jnp.zeros_like(dv_ref)
            dfs_ref[...] = jnp.zeros_like(dfs_ref)

        in_a, _ = _head_masks()
        causal, _ = _tile_masks()
        q2, do2 = q_ref[...], do_ref[...]
        ft = tuple(jnp.broadcast_to(ft_ref[a] - lse_ref[a], (BLK, BLK)) for a in range(2))
        dl = tuple(jnp.broadcast_to(dl_ref[a], (BLK, BLK)) for a in range(2))

        def tile(j, carry, masked):
            dq, dft = carry
            kb, vb = k_ref[0, j], v_ref[0, j]
            s2 = lax.dot_general(q2, kb, NT_DIMS, preferred_element_type=F32)
            dp2 = lax.dot_general(do2, vb, NT_DIMS, preferred_element_type=F32)
            fs = fs_ref[0, j]
            pr, ds, dft_new = [], [], []
            for a in range(2):
                sl = slice(a * BLK, (a + 1) * BLK)
                s = (s2[:, sl] + ft[a]) - fs[:, sl]
                if masked:
                    s = jnp.where(causal, s, NEG)
                pa = jnp.exp(s)
                dsa = pa * (dp2[:, sl] - dl[a])
                pr.append(pa.astype(BF16))
                ds.append(dsa)
                dft_new.append(dft[a] + jnp.sum(dsa, axis=1, keepdims=True))
            ds2 = jnp.concatenate(ds, axis=1)
            dsb = ds2.astype(BF16)
            off = pl.multiple_of(j * BLK, BLK)
            dk_ref[pl.ds(off, BLK), :] += _fold_heads(lax.dot_general(dsb, q2, TN_DIMS, preferred_element_type=F32), in_a)
            dv_ref[pl.ds(off, BLK), :] += _fold_heads(
                lax.dot_general(jnp.concatenate(pr, axis=1), do2, TN_DIMS, preferred_element_type=F32), in_a)
            dfs_ref[0, j] += -jnp.sum(ds2, axis=0, keepdims=True)
            return dq + jnp.dot(dsb, kb, preferred_element_type=F32), tuple(dft_new)

        zero = jnp.zeros((BLK, 1), F32)
        carry = lax.fori_loop(0, i, functools.partial(tile, masked=False), (jnp.zeros((BLK, LANES), F32), (zero, zero)))
        dq, dft = tile(i, carry, True)
        dq_ref[...] = dq
        dft_ref[0] = dft[0]
        dft_ref[1] = dft[1]

    qs, kv = _pair_specs(t, dg, 0)
    col = pl.BlockSpec((2, BLK, 1), lambda h, i: (h, i, 0))
    rowspec = pl.BlockSpec((1, nq, 1, 2 * BLK), lambda h, i: (h, 0, 0, 0))
    blk = pl.BlockSpec((BLK, LANES), lambda h, i: (i, h))
    acc = pl.BlockSpec((t, LANES), lambda h, i: (0, h))
    return _pcall(
        body, name="fox_bwd", grid=(p, nq),
        out_shape=[jax.ShapeDtypeStruct((t, dg), F32)] * 3 + [jax.ShapeDtypeStruct((p, nq, 1, 2 * BLK), F32),
                                                              jax.ShapeDtypeStruct((nh, t, 1), F32)],
        in_specs=[qs, kv, kv, blk, col, rowspec, col, col],
        out_specs=[blk, acc, acc, rowspec, col],
        compiler_params=_params(("arbitrary", "arbitrary")),
    )(qkv, kst, vst, do, fcol, frow2, lse, delta)


def _softplus_parts(z):
    e = jnp.exp(-jnp.abs(z))
    return jnp.maximum(z, 0.0) + jnp.log(1.0 + e), e


def _sigmoid_from(z, e):
    d = 1.0 + e
    r = pl.reciprocal(d, approx=True)
    r = r * (2.0 - d * r)
    return jnp.where(z >= 0, 1.0, e) * r


def _sb_fwd(qkv, kst, vst, dg):
    t = qkv.shape[0]
    p, nq = dg // LANES, t // BLK
    nh = 2 * p

    def body(q_ref, k_ref, v_ref, o_ref, rt_ref):
        i = pl.program_id(1)
        _, strict = _tile_masks()
        strict2 = jnp.concatenate([strict, strict], axis=1)
        suffix = _pair_triangle(True)
        q2 = q_ref[...]

        def tile(j, carry, masked):
            rest, acc = carry
            kb, vb = k_ref[0, j], v_ref[0, j]
            z = lax.dot_general(q2, kb, NT_DIMS, preferred_element_type=F32)
            sp, _ = _softplus_parts(z)
            if masked:
                sp = jnp.where(strict2, sp, 0.0)
            cs = _pair_cumsum(sp, suffix, 2)
            w, rest_new = [], []
            for a in range(2):
                sl = slice(a * BLK, (a + 1) * BLK)
                wa = jnp.exp(z[:, sl] - cs[:, sl] - rest[a])
                if masked:
                    wa = jnp.where(strict, wa, 0.0)
                w.append(wa.astype(BF16))
                rest_new.append(rest[a] + cs[:, a * BLK:a * BLK + 1])
            acc = acc + jnp.dot(jnp.concatenate(w, axis=1), vb, preferred_element_type=F32)
            return tuple(rest_new), acc

        zero = jnp.zeros((BLK, 1), F32)
        carry = tile(i, ((zero, zero), jnp.zeros((BLK, LANES), F32)), True)
        rest, acc = lax.fori_loop(0, i, lambda jj, c: tile(i - 1 - jj, c, False), carry)
        o_ref[...] = acc
        rt_ref[0] = rest[0]
        rt_ref[1] = rest[1]

    qs, kv = _pair_specs(t, dg, 3)
    col = pl.BlockSpec((2, BLK, 1), lambda h, i: (h, i, 0))
    return _pcall(
        body, name="sb_fwd", grid=(p, nq),
        out_shape=[jax.ShapeDtypeStruct((t, dg), F32), jax.ShapeDtypeStruct((nh, t, 1), F32)],
        in_specs=[qs, kv, kv],
        out_specs=[pl.BlockSpec((BLK, LANES), lambda h, i: (i, h)), col],
        compiler_params=_params(("arbitrary", "arbitrary")),
    )(qkv, kst, vst)


def _sb_bwd(qkv, kst, vst, do, rtot, dg):
    t = qkv.shape[0]
    p, nq = dg // LANES, t // BLK

    def body(q_ref, k_ref, v_ref, do_ref, rt_ref, dq_ref, dk_ref, dv_ref):
        i = pl.program_id(1)

        @pl.when(i == 0)
        def _():
            dk_ref[...] = jnp.zeros_like(dk_ref)
            dv_ref[...] = jnp.zeros_like(dv_ref)

        in_a, _ = _head_masks()
        _, strict = _tile_masks()
        strict2 = jnp.concatenate([strict, strict], axis=1)
        prefix = _pair_triangle(False)
        q2, do2 = q_ref[...], do_ref[...]
        rt = (rt_ref[0], rt_ref[1])

        def tile(j, carry, masked):
            before, gbefore, dq = carry
            kb, vb = k_ref[0, j], v_ref[0, j]
            z = lax.dot_general(q2, kb, NT_DIMS, preferred_element_type=F32)
            da = lax.dot_general(do2, vb, NT_DIMS, preferred_element_type=F32)
            sp, e = _softplus_parts(z)
            sig = _sigmoid_from(z, e)
            if masked:
                sp = jnp.where(strict2, sp, 0.0)
            pre = _pair_cumsum(sp, prefix, 2)
            w = []
            for a in range(2):
                sl = slice(a * BLK, (a + 1) * BLK)
                wa = jnp.exp(z[:, sl] + (before[a] - rt[a]) + pre[:, sl] - sp[:, sl])
                if masked:
                    wa = jnp.where(strict, wa, 0.0)
                w.append(wa)
            w2 = jnp.concatenate(w, axis=1)
            g = w2 * da
            preg = _pair_cumsum(g, prefix, 1)
            dz = []
            for a in range(2):
                sl = slice(a * BLK, (a + 1) * BLK)
                dza = g[:, sl] * (1.0 - sig[:, sl]) - sig[:, sl] * (gbefore[a] + preg[:, sl] - g[:, sl])
                if masked:
                    dza = jnp.where(strict, dza, 0.0)
                dz.append(dza.astype(BF16))
            dzb = jnp.concatenate(dz, axis=1)
            off = pl.multiple_of(j * BLK, BLK)
            dk_ref[pl.ds(off, BLK), :] += _fold_heads(lax.dot_general(dzb, q2, TN_DIMS, preferred_element_type=F32), in_a)
            dv_ref[pl.ds(off, BLK), :] += _fold_heads(
                lax.dot_general(w2.astype(BF16), do2, TN_DIMS, preferred_element_type=F32), in_a)
            last = lambda x, a: x[:, (a + 1) * BLK - 1:(a + 1) * BLK]
            return (tuple(before[a] + last(pre, a) for a in range(2)),
                    tuple(gbefore[a] + last(preg, a) for a in range(2)),
                    dq + jnp.dot(dzb, kb, preferred_element_type=F32))

        zero = jnp.zeros((BLK, 1), F32)
        carry = lax.fori_loop(0, i, functools.partial(tile, masked=False), ((zero, zero), (zero, zero), jnp.zeros((BLK, LANES), F32)))
        dq_ref[...] = tile(i, carry, True)[2]

    qs, kv = _pair_specs(t, dg, 3)
    col = pl.BlockSpec((2, BLK, 1), lambda h, i: (h, i, 0))
    blk = pl.BlockSpec((BLK, LANES), lambda h, i: (i, h))
    acc = pl.BlockSpec((t, LANES), lambda h, i: (0, h))
    return _pcall(
        body, name="sb_bwd", grid=(p, nq),
        out_shape=[jax.ShapeDtypeStruct((t, dg), F32)] * 3,
        in_specs=[qs, kv, kv, blk, col],
        out_specs=[blk, acc, acc],
        compiler_params=_params(("arbitrary", "arbitrary")),
    )(qkv, kst, vst, do, rtot)


XROWS = 144
LANE_FS, LANE_FT_A, LANE_FT_B = 0, 3, 6


def _pieces3(x):
    hi = x.astype(BF16).astype(F32)
    r = x - hi
    mid = r.astype(BF16).astype(F32)
    return hi, mid, (r - mid).astype(BF16).astype(F32)


def _bias_lanes(rows, entries):
    lane = lax.broadcasted_iota(jnp.int32, (1, LANES), 1)
    out = jnp.zeros((rows, LANES), F32)
    for l, v in entries:
        out = jnp.where(lane == l, v, out)
    return out


def _three(first, values):
    return [(first + k, v) for k, v in enumerate(values)]


def _stack_rows(x, in_a, in_b):
    zero = jnp.zeros_like(x)
    return jnp.concatenate([jnp.where(in_a, x, zero), jnp.where(in_b, x, zero)], axis=0)


def _transposed(x):
    return x.astype(F32).T.astype(BF16)


def _attn_operands(qkv, fcol, dg):
    t = qkv.shape[0]
    p, nk = dg // LANES, t // BLK

    def body(qf_ref, kf_ref, vf_ref, ks_ref, vs_ref, f_ref, qx_ref, kx_ref, kxt_ref, vf_o, vft_o, ks_o, kst_o, vs_o, vst_o):
        in_a, in_b = _head_masks()
        fa, fb = _pieces3(f_ref[0]), _pieces3(f_ref[1])
        qx_ref[0, :, :LANES] = qf_ref[...]
        qx_ref[0, :, LANES:] = _bias_lanes(
            BLK, _three(LANE_FS, (-1.0,) * 3) + _three(LANE_FT_A, fa) + _three(LANE_FT_B, fb)).astype(BF16)
        kf = kf_ref[...]
        zero = jnp.zeros_like(kf)
        top = jnp.concatenate([jnp.where(in_a, kf, zero), _bias_lanes(
            BLK, _three(LANE_FS, fa) + _three(LANE_FT_A, (1.0,) * 3)).astype(BF16)], axis=1)
        bot = jnp.concatenate([jnp.where(in_b, kf, zero), _bias_lanes(
            BLK, _three(LANE_FS, fb) + _three(LANE_FT_B, (1.0,) * 3)).astype(BF16)], axis=1)
        kx = jnp.concatenate([top, bot], axis=0)
        kx_ref[0, 0] = kx
        kxt_ref[0, 0] = _transposed(kx)[:XROWS]
        for src, dst, dst_t in ((vf_ref, vf_o, vft_o), (ks_ref, ks_o, kst_o), (vs_ref, vs_o, vst_o)):
            st = _stack_rows(src[...], in_a, in_b)
            dst[0, 0] = st
            dst_t[0, 0] = _transposed(st)

    col = lambda base: pl.BlockSpec((BLK, LANES), lambda h, j: (j, base * p + h))
    blk4 = lambda r, c: pl.BlockSpec((1, 1, r, c), lambda h, j: (h, j, 0, 0))
    shp4 = lambda r, c: jax.ShapeDtypeStruct((p, nk, r, c), BF16)
    return _pcall(
        body, name="attn_operands", grid=(p, nk),
        out_shape=[jax.ShapeDtypeStruct((p, t, 2 * LANES), BF16), shp4(2 * BLK, 2 * LANES), shp4(XROWS, 2 * BLK)]
        + [shp4(2 * BLK, LANES), shp4(LANES, 2 * BLK)] * 3,
        in_specs=[col(0), col(1), col(2), col(4), col(5), pl.BlockSpec((2, BLK, 1), lambda h, j: (h, j, 0))],
        out_specs=[pl.BlockSpec((1, BLK, 2 * LANES), lambda h, j: (h, j, 0)), blk4(2 * BLK, 2 * LANES), blk4(XROWS, 2 * BLK)]
        + [blk4(2 * BLK, LANES), blk4(LANES, 2 * BLK)] * 3,
        compiler_params=_params(("arbitrary", "arbitrary")),
    )(qkv, qkv, qkv, qkv, qkv, fcol)


def _fox_q_bwd(qkv, fcol, lse_col, dg):
    t = qkv.shape[0]
    p = dg // LANES

    def body(q_ref, f_ref, l_ref, qx_ref):
        fa, fb = _pieces3(f_ref[0] - l_ref[0]), _pieces3(f_ref[1] - l_ref[1])
        qx_ref[0, :, :LANES] = q_ref[...]
        qx_ref[0, :, LANES:] = _bias_lanes(
            BLK, _three(LANE_FS, (-1.0,) * 3) + _three(LANE_FT_A, fa) + _three(LANE_FT_B, fb)).astype(BF16)

    col = pl.BlockSpec((2, BLK, 1), lambda h, j: (h, j, 0))
    return _pcall(
        body, name="fox_q_bwd", grid=(p, t // BLK),
        out_shape=jax.ShapeDtypeStruct((p, t, 2 * LANES), BF16),
        in_specs=[pl.BlockSpec((BLK, LANES), lambda h, j: (j, h)), col, col],
        out_specs=pl.BlockSpec((1, BLK, 2 * LANES), lambda h, j: (h, j, 0)),
        compiler_params=_params(("arbitrary", "arbitrary")),
    )(qkv, fcol, lse_col)


def _key_query_masks():
    key = lax.broadcasted_iota(jnp.int32, (BLK, BLK), 0)
    qry = lax.broadcasted_iota(jnp.int32, (BLK, BLK), 1)
    return key <= qry, key < qry


def _key_triangle(kind):
    s = lax.broadcasted_iota(jnp.int32, (BLK, BLK), 0)
    j = lax.broadcasted_iota(jnp.int32, (BLK, BLK), 1)
    return {"suffix": j >= s, "prefix": j <= s, "before": j < s}[kind].astype(BF16)


def _tri_dot(tri, x, passes):
    acc = None
    for _ in range(passes):
        part = x.astype(BF16)
        d = jnp.dot(tri, part, preferred_element_type=F32)
        acc = d if acc is None else acc + d
        x = x - part.astype(F32)
    return acc


GROUPS = (4, 2, 1)


def _loop_blocks(n, tiles, carry, descending=False, groups=GROUPS):
    at = (lambda k: n - 1 - k) if descending else (lambda k: k)
    done = 0
    for g in groups:
        left = n - done
        carry = lax.fori_loop(0, left // g, lambda h, c, g=g, done=done: tiles([at(done + g * h + k) for k in range(g)], c), carry)
        done = done + (left // g) * g
    return carry


def _resident(shape):
    return pl.BlockSpec((1,) + shape, lambda h, i: (h,) + (0,) * len(shape), pipeline_mode=pl.Buffered(1))


def _rows_per_head(a, b):
    return jnp.concatenate([jnp.broadcast_to(a, (HEAD_DIM, BLK)), jnp.broadcast_to(b, (HEAD_DIM, BLK))], axis=0)


def _fold_heads(stacked, in_a):
    return jnp.where(in_a, stacked[:BLK], stacked[BLK:])


def _fox_fwd(qx, kx, v_t, dg):
    p, t = qx.shape[0], qx.shape[1]
    nq = t // BLK
    nh = 2 * p

    def body(q_ref, k_ref, vt_ref, o_ref, lse_ref):
        i = pl.program_id(1)
        causal, _ = _key_query_masks()
        q = q_ref[0]

        def scores(j, masked):
            s2 = lax.dot_general(k_ref[0, j], q, NT_DIMS, preferred_element_type=F32)
            s = [s2[a * BLK:(a + 1) * BLK] for a in range(2)]
            return [jnp.where(causal, x, NEG) for x in s] if masked else s

        def update(blocks, carry):
            m, l, acc = carry
            m_new, l_new, alpha = [], [], []
            pr = [[] for _ in blocks]
            for a in range(2):
                mn = m[a]
                for _, s in blocks:
                    mn = jnp.maximum(mn, jnp.max(s[a], axis=0, keepdims=True))
                al = jnp.exp(m[a] - mn)
                ln = al * l[a]
                for k, (_, s) in enumerate(blocks):
                    pa = jnp.exp(s[a] - mn)
                    ln = ln + jnp.sum(pa, axis=0, keepdims=True)
                    pr[k].append(pa.astype(BF16))
                m_new.append(mn)
                alpha.append(al)
                l_new.append(ln)
            acc = _rows_per_head(*alpha) * acc
            for k, (j, _) in enumerate(blocks):
                acc = acc + jnp.dot(vt_ref[0, j], jnp.concatenate(pr[k], axis=0), preferred_element_type=F32)
            return tuple(m_new), tuple(l_new), acc

        tiles = lambda js, c: update([(j, scores(j, False)) for j in js], c)
        neg, zero = jnp.full((1, BLK), NEG, F32), jnp.zeros((1, BLK), F32)
        carry = _loop_blocks(i, tiles, ((neg, neg), (zero, zero), jnp.zeros((LANES, BLK), F32)))
        m, l, acc = update([(i, scores(i, True))], carry)
        o_ref[...] = acc / _rows_per_head(*l)
        lse_ref[0] = m[0] + jnp.log(l[0])
        lse_ref[1] = m[1] + jnp.log(l[1])

    row = pl.BlockSpec((2, 1, BLK), lambda h, i: (h, 0, i))
    return _pcall(
        body, name="fox_fwd", grid=(p, nq),
        out_shape=[jax.ShapeDtypeStruct((dg, t), F32), jax.ShapeDtypeStruct((nh, 1, t), F32)],
        in_specs=[pl.BlockSpec((1, BLK, 2 * LANES), lambda h, i: (h, i, 0)), _resident((nq, 2 * BLK, 2 * LANES)),
                  _resident((nq, LANES, 2 * BLK))],
        out_specs=[pl.BlockSpec((LANES, BLK), lambda h, i: (h, i)), row],
        compiler_params=_params(("arbitrary", "arbitrary")),
    )(qx, kx, v_t)


def _fox_bwd(qxb, kx, kx_t, v_st, do, delta, dg):
    p, t = qxb.shape[0], qxb.shape[1]
    nq = t // BLK
    nh = 2 * p

    def body(q_ref, k_ref, kt_ref, v_ref, do_ref, dl_ref, dq_ref, dft_ref, dk_ref, dv_ref, dkx_ref):
        i = pl.program_id(1)

        @pl.when(i == 0)
        def _():
            dk_ref[...] = jnp.zeros_like(dk_ref)
            dv_ref[...] = jnp.zeros_like(dv_ref)
            dkx_ref[...] = jnp.zeros_like(dkx_ref)

        in_a, _ = _head_masks()
        first_lane = lax.broadcasted_iota(jnp.int32, (1, LANES), 1) == 0
        causal, _ = _key_query_masks()
        q, do2 = q_ref[0], do_ref[...]
        dl = (dl_ref[0], dl_ref[1])

        def products(j):
            return (lax.dot_general(k_ref[0, j], q, NT_DIMS, preferred_element_type=F32),
                    lax.dot_general(v_ref[0, j], do2, NT_DIMS, preferred_element_type=F32))

        def dscores(prod, masked):
            s2, dp2 = prod
            pr, ds = [], []
            for a in range(2):
                s = s2[a * BLK:(a + 1) * BLK]
                if masked:
                    s = jnp.where(causal, s, NEG)
                pa = jnp.exp(s)
                ds.append((pa * (dp2[a * BLK:(a + 1) * BLK] - dl[a])).astype(BF16))
                pr.append(pa.astype(BF16))
            return jnp.concatenate(ds, axis=0), jnp.concatenate(pr, axis=0)

        def accumulate(j, dsb, prb, dq):
            off = pl.multiple_of(j * BLK, BLK)
            dk_full = jnp.dot(dsb, q, preferred_element_type=F32)
            dk_ref[pl.ds(off, BLK), :] += _fold_heads(dk_full[:, :LANES], in_a)
            dkx_ref[pl.ds(off, BLK), :] += jnp.where(first_lane, dk_full[:BLK, LANES:], dk_full[BLK:, LANES:])
            dv_ref[pl.ds(off, BLK), :] += _fold_heads(jnp.dot(prb, do2, preferred_element_type=F32), in_a)
            return dq + jnp.dot(kt_ref[0, j], dsb, preferred_element_type=F32)

        def tiles(js, dq, masked=False):
            prods = [products(j) for j in js]
            grads = [dscores(pr, masked) for pr in prods]
            for j, (dsb, prb) in zip(js, grads):
                dq = accumulate(j, dsb, prb, dq)
            return dq

        dq = _loop_blocks(i, tiles, jnp.zeros((XROWS, BLK), F32))
        dq = tiles([i], dq, True)
        dq_ref[...] = dq[:LANES]
        dft_ref[0] = dq[LANES + LANE_FT_A:LANES + LANE_FT_A + 1]
        dft_ref[1] = dq[LANES + LANE_FT_B:LANES + LANE_FT_B + 1]

    row = pl.BlockSpec((2, 1, BLK), lambda h, i: (h, 0, i))
    acc = pl.BlockSpec((t, LANES), lambda h, i: (0, h))
    return _pcall(
        body, name="fox_bwd", grid=(p, nq),
        out_shape=[jax.ShapeDtypeStruct((dg, t), F32), jax.ShapeDtypeStruct((nh, 1, t), F32)] + [jax.ShapeDtypeStruct((t, dg), F32)] * 3,
        in_specs=[pl.BlockSpec((1, BLK, 2 * LANES), lambda h, i: (h, i, 0)), _resident((nq, 2 * BLK, 2 * LANES)),
                  _resident((nq, XROWS, 2 * BLK)), _resident((nq, 2 * BLK, LANES)),
                  pl.BlockSpec((BLK, LANES), lambda h, i: (i, h)), row],
        out_specs=[pl.BlockSpec((LANES, BLK), lambda h, i: (h, i)), row, acc, acc, acc],
        compiler_params=_params(("arbitrary", "arbitrary")),
    )(qxb, kx, kx_t, v_st, do, delta)


def _softplus_of(z):
    return jnp.maximum(z, 0.0) + jnp.log(1.0 + jnp.exp(-jnp.abs(z)))


def _sb_fwd(qkv, k_st, v_t, dg):
    t = qkv.shape[0]
    p, nq = dg // LANES, t // BLK
    nh = 2 * p

    def body(q_ref, k_ref, vt_ref, o_ref, rt_ref):
        i = pl.program_id(1)
        _, strict = _key_query_masks()
        suffix = _key_triangle("suffix")
        q = q_ref[...]

        def scores(j):
            z2 = lax.dot_general(k_ref[0, j], q, NT_DIMS, preferred_element_type=F32)
            return [z2[a * BLK:(a + 1) * BLK] for a in range(2)]

        def suffix_sums(z, masked):
            out = []
            for a in range(2):
                sp = _softplus_of(z[a])
                if masked:
                    sp = jnp.where(strict, sp, 0.0)
                out.append(_tri_dot(suffix, sp, 2))
            return out

        def weights(z, cs, rest, masked):
            w, rest_new = [], []
            for a in range(2):
                wa = jnp.exp(z[a] - cs[a] - rest[a])
                if masked:
                    wa = jnp.where(strict, wa, 0.0)
                w.append(wa.astype(BF16))
                rest_new.append(rest[a] + cs[a][0:1])
            return jnp.concatenate(w, axis=0), tuple(rest_new)

        def tiles(js, carry, masked=False):
            rest, acc = carry
            zs = [scores(j) for j in js]
            css = [suffix_sums(z, masked) for z in zs]
            ws = []
            for z, cs in zip(zs, css):
                w2, rest = weights(z, cs, rest, masked)
                ws.append(w2)
            for j, w2 in zip(js, ws):
                acc = acc + jnp.dot(vt_ref[0, j], w2, preferred_element_type=F32)
            return rest, acc

        zero = jnp.zeros((1, BLK), F32)
        carry = tiles([i], ((zero, zero), jnp.zeros((LANES, BLK), F32)), True)
        rest, acc = _loop_blocks(i, tiles, carry, descending=True)
        o_ref[...] = acc
        rt_ref[0] = rest[0]
        rt_ref[1] = rest[1]

    return _pcall(
        body, name="sb_fwd", grid=(p, nq),
        out_shape=[jax.ShapeDtypeStruct((dg, t), F32), jax.ShapeDtypeStruct((nh, 1, t), F32)],
        in_specs=[pl.BlockSpec((BLK, LANES), lambda h, i: (i, 3 * p + h)), _resident((nq, 2 * BLK, LANES)),
                  _resident((nq, LANES, 2 * BLK))],
        out_specs=[pl.BlockSpec((LANES, BLK), lambda h, i: (h, i)), pl.BlockSpec((2, 1, BLK), lambda h, i: (h, 0, i))],
        compiler_params=_params(("arbitrary", "arbitrary")),
    )(qkv, k_st, v_t)


def _sb_bwd(qkv, k_st, k_t, v_st, do, rtot, dg):
    t = qkv.shape[0]
    p, nq = dg // LANES, t // BLK

    def body(q_ref, k_ref, kt_ref, v_ref, do_ref, rt_ref, dq_ref, dk_ref, dv_ref):
        i = pl.program_id(1)

        @pl.when(i == 0)
        def _():
            dk_ref[...] = jnp.zeros_like(dk_ref)
            dv_ref[...] = jnp.zeros_like(dv_ref)

        in_a, _ = _head_masks()
        _, strict = _key_query_masks()
        before_m, prefix_m = _key_triangle("before"), _key_triangle("prefix")
        q, do2 = q_ref[...], do_ref[...]
        rt = (rt_ref[0], rt_ref[1])

        def products(j):
            z2 = lax.dot_general(k_ref[0, j], q, NT_DIMS, preferred_element_type=F32)
            da2 = lax.dot_general(v_ref[0, j], do2, NT_DIMS, preferred_element_type=F32)
            return [z2[a * BLK:(a + 1) * BLK] for a in range(2)], [da2[a * BLK:(a + 1) * BLK] for a in range(2)]

        def softplus_sums(z, masked):
            sp = [_softplus_of(x) for x in z]
            if masked:
                sp = [jnp.where(strict, x, 0.0) for x in sp]
            return sp, [_tri_dot(before_m, x, 2) for x in sp]

        def weight_grads(z, da, sp, pre, before, masked):
            w, g, pg, before_new = [], [], [], []
            for a in range(2):
                wa = jnp.exp(z[a] + (before[a] - rt[a]) + pre[a])
                if masked:
                    wa = jnp.where(strict, wa, 0.0)
                ga = wa * da[a]
                w.append(wa.astype(BF16))
                g.append(ga)
                pg.append(jnp.dot(prefix_m, ga.astype(BF16), preferred_element_type=F32))
                before_new.append(before[a] + pre[a][BLK - 1:BLK] + sp[a][BLK - 1:BLK])
            return jnp.concatenate(w, axis=0), g, pg, tuple(before_new)

        def dlogits(sp, g, pg, gbefore, masked):
            dz, gbefore_new = [], []
            for a in range(2):
                s_incl = gbefore[a] + pg[a]
                dza = (g[a] - s_incl) + jnp.exp(-sp[a]) * s_incl
                if masked:
                    dza = jnp.where(strict, dza, 0.0)
                dz.append(dza.astype(BF16))
                gbefore_new.append(s_incl[BLK - 1:BLK])
            return jnp.concatenate(dz, axis=0), tuple(gbefore_new)

        def accumulate(j, dzb, wb, dq):
            off = pl.multiple_of(j * BLK, BLK)
            dk_ref[pl.ds(off, BLK), :] += _fold_heads(jnp.dot(dzb, q, preferred_element_type=F32), in_a)
            dv_ref[pl.ds(off, BLK), :] += _fold_heads(jnp.dot(wb, do2, preferred_element_type=F32), in_a)
            return dq + jnp.dot(kt_ref[0, j], dzb, preferred_element_type=F32)

        def tiles(js, carry, masked=False):
            before, gbefore, dq = carry
            prods = [products(j) for j in js]
            sums = [softplus_sums(z, masked) for z, _ in prods]
            grads = []
            for (z, da), (sp, pre) in zip(prods, sums):
                wb, g, pg, before = weight_grads(z, da, sp, pre, before, masked)
                grads.append((wb, g, pg))
            for j, (sp, _), (wb, g, pg) in zip(js, sums, grads):
                dzb, gbefore = dlogits(sp, g, pg, gbefore, masked)
                dq = accumulate(j, dzb, wb, dq)
            return before, gbefore, dq

        zero = jnp.zeros((1, BLK), F32)
        carry = _loop_blocks(i, tiles, ((zero, zero), (zero, zero), jnp.zeros((LANES, BLK), F32)), groups=(2, 1))
        dq_ref[...] = tiles([i], carry, True)[2]

    acc = pl.BlockSpec((t, LANES), lambda h, i: (0, h))
    return _pcall(
        body, name="sb_bwd", grid=(p, nq),
        out_shape=[jax.ShapeDtypeStruct((dg, t), F32)] + [jax.ShapeDtypeStruct((t, dg), F32)] * 2,
        in_specs=[pl.BlockSpec((BLK, LANES), lambda h, i: (i, 3 * p + h)), _resident((nq, 2 * BLK, LANES)),
                  _resident((nq, LANES, 2 * BLK)), _resident((nq, 2 * BLK, LANES)),
                  pl.BlockSpec((BLK, LANES), lambda h, i: (i, h)), pl.BlockSpec((2, 1, BLK), lambda h, i: (h, 0, i))],
        out_specs=[pl.BlockSpec((LANES, BLK), lambda h, i: (h, i)), acc, acc],
        compiler_params=_params(("arbitrary", "arbitrary")),
    )(qkv, k_st, k_t, v_st, do, rtot)


def _tri_constants(nh, t):
    nb = t // LANES
    r = nh * nb
    li = np.arange(LANES)
    tri_in = (li[:, None] <= li[None, :])
    ri = np.arange(r)
    same = (ri[:, None] // nb) == (ri[None, :] // nb)
    blk = same & (ri[None, :] < ri[:, None])
    blk_rev = same & (ri[None, :] > ri[:, None])
    head_rows = (np.arange(max(8, nh))[:, None] == (ri[None, :] // nb))
    as_bf16 = lambda a: jnp.asarray(a.astype(np.float32), BF16)
    return as_bf16(tri_in), as_bf16(blk), as_bf16(tri_in.T), as_bf16(blk_rev), as_bf16(head_rows)


def kernel(x, c, w_ada, b_ada, g_attn, w_in, b_fgate, g_out_fox, g_out_sb, w_out, g_mlp, w_up, conv_w, conv_b, w_down, g_final, loss_target, m_w_ada, m_b_ada, m_g_attn, m_w_in, m_b_fgate, m_g_out_fox, m_g_out_sb, m_w_out, m_g_mlp, m_w_up, m_conv_w, m_conv_b, m_w_down, m_g_final, v_w_ada, v_b_ada, v_g_attn, v_w_in, v_b_fgate, v_g_out_fox, v_g_out_sb, v_w_out, v_g_mlp, v_w_up, v_conv_w, v_conv_b, v_w_down, v_g_final):
    t, d = x.shape[1], x.shape[2]
    dg = d // 2
    nh = dg // HEAD_DIM
    n_in = 6 * dg + nh
    dff = w_down.shape[1] * 4
    dfp = -(-dff // 256) * 256
    cf = 256
    tm = _tile(t, (512, 256, 128))
    nq = t // BLK
    xi, yi, ci = lax.axis_index("x"), lax.axis_index("y"), lax.axis_index("c")
    shard = 2 * xi + yi
    me = 4 * xi + 2 * yi + ci

    x2d, tg2d = x[0], loss_target[0]

    c_all = _all_gather8(jnp.pad(c, ((0, 7), (0, 0)))).reshape(8, 8, d)[:, 0, :]
    ada_cols = w_ada.shape[2]
    b_shard = lax.dynamic_slice(b_ada, (0, shard * ada_cols), (1, ada_cols))
    sc_all, mod_shard = _ada_fwd(c_all, w_ada[0], b_shard)
    mod_all = _all_gather8(mod_shard).reshape(4, 2, 8, ada_cols)
    mod_me = lax.dynamic_index_in_dim(mod_all[:, 0], me, axis=1, keepdims=False)
    mod8 = jnp.pad(mod_me.reshape(6, d), ((0, 2), (0, 0)))

    lane_pad = lambda a: jnp.pad(a, ((0, 0),) * (a.ndim - 1) + ((0, -a.shape[-1] % LANES),))
    g_in, g_out, g_up, g_down, g_cw = _gather_xy(
        [lane_pad(w_in[0].astype(BF16)), w_out[0].astype(BF16), lane_pad(w_up[0].astype(BF16)), w_down[0].astype(BF16),
         lane_pad(conv_w[0])])
    g_in, g_up, g_cw = g_in[:, :, :n_in // 4], g_up[:, :, :dff // 2], g_cw[:, :, :dff // 2]
    w_in_full = jnp.transpose(g_in, (1, 0, 2)).reshape(d, n_in)
    w_qkv = w_in_full[:, :6 * dg]
    w_f = jnp.pad(w_in_full[:, 6 * dg:], ((0, 0), (0, LANES - nh)))
    w_out_full = g_out.reshape(2 * dg, d)
    w_up_full = jnp.transpose(g_up, (1, 0, 2)).reshape(d, 2 * dff)
    padc = ((0, 0), (0, dfp - dff))
    wg, wv = jnp.pad(w_up_full[:, :dff], padc), jnp.pad(w_up_full[:, dff:], padc)
    wd = jnp.pad(g_down.reshape(dff, d), ((0, dfp - dff), (0, 0)))
    cw_full = jnp.transpose(g_cw, (1, 0, 2)).reshape(3, 2 * dff)
    cw4 = jnp.concatenate([cw_full, conv_b], axis=0)
    cwg = jnp.pad(cw4[:, :dff], ((0, 4), (0, dfp - dff)))
    cwv = jnp.pad(cw4[:, dff:], ((0, 4), (0, dfp - dff)))

    qkv, fl, h1 = _in_proj_fwd(x2d, mod8, g_attn, w_qkv, w_f, tm)
    tri_in, tri_blk, tri_in_rev, tri_blk_rev, head_rows = _tri_constants(nh, t)
    fl2d = fl[:, :nh].T.reshape(nh * t // LANES, LANES)
    b_rows = jnp.repeat(b_fgate[0], t // LANES)[:, None]
    f2d = _fgate_fwd(fl2d, b_rows, tri_in, tri_blk)
    fcol = f2d.reshape(nh, t, 1)
    pairs = nh // 2
    qx, kx, kx_t, vf_st, vf_t, ks_st, ks_t, vs_st, vs_t = _attn_operands(qkv, fcol, dg)
    o_fox_t, lse = _fox_fwd(qx, kx, vf_t, dg)
    o_sb_t, rtot = _sb_fwd(qkv, ks_st, vs_t, dg)
    o_fox, o_sb = o_fox_t.T, o_sb_t.T
    li = np.arange(dg)
    bd = jnp.asarray((li[:, None] // HEAD_DIM == li[None, :] // HEAD_DIM).astype(np.float32), BF16)
    hsel = jnp.asarray((li[:, None] // HEAD_DIM == np.arange(LANES)[None, :]).astype(np.float32), BF16)
    x2, mix = _attn_out_fwd(x2d, o_fox, o_sb, g_out_fox, g_out_sb, w_out_full, mod8, bd, tm)
    g_final2 = g_final[None, :]
    dx3, h2, part_f = _ffn_fwd(x2, tg2d, mod8, g_mlp, g_final2, wg, wv, cwg, cwv, wd, tm, cf)

    dx2, dupg, dupv, act, dxg3, part_b, pcg, pcv = _ffn_bwd(x2, dx3, mod8, g_mlp, wg, wv, cwg, cwv, wd, tm, cf)
    do_fox, do_sb, delta, dxg2, part_o = _attn_out_bwd(dx2, mix, o_fox, o_sb, g_out_fox, g_out_sb, w_out_full, mod8, bd, hsel, tm)
    drow = delta[:, :nh].T.reshape(nh, 1, t)
    qxb = _fox_q_bwd(qkv, fcol, lse.reshape(nh, t, 1), dg)
    dq_f_t, dft, dk_f, dv_f, dkx = _fox_bwd(qxb, kx, kx_t, vf_st, do_fox, drow, dg)
    dq_s_t, dk_s, dv_s = _sb_bwd(qkv, ks_st, ks_t, vs_st, do_sb, rtot, dg)
    dq_f, dq_s = dq_f_t.T, dq_s_t.T
    f2d_shape = (nh * t // LANES, LANES)
    dfs = jnp.transpose(dkx.reshape(t, pairs, LANES)[:, :, :2], (1, 2, 0))
    dfl2d, gb8 = _fgate_bwd(fl2d, b_rows, dft.reshape(f2d_shape), dfs.reshape(f2d_shape), tri_in_rev, tri_blk_rev, head_rows)
    dfl = jnp.pad(dfl2d.reshape(nh, t).T, ((0, 0), (0, LANES - nh)))
    grad_x, dproj, dflb, part_i = _in_proj_bwd([dq_f, dk_f, dv_f, dq_s, dk_s, dv_s], dfl, w_qkv, w_f, x2d, dx2, mod8, g_attn, tm)

    gw_qkv = _matmul_tn(h1, dproj, "grad_w_qkv")
    gw_f = _matmul_tn(h1, dflb, "grad_w_f")
    gw_in = jnp.concatenate([gw_qkv, gw_f[:, :nh]], axis=1)
    gw_out = _matmul_tn(mix, dxg2, "grad_w_out")
    gw_upg = _matmul_tn(h2, dupg, "grad_w_up_gate")
    gw_upv = _matmul_tn(h2, dupv, "grad_w_up_val")
    gw_up = jnp.concatenate([gw_upg[:, :dff], gw_upv[:, :dff]], axis=1)
    gw_down = _matmul_tn(act, dxg3, "grad_w_down")[:dff]

    sf = _sum_leading(part_f, "sum_part_ffn_fwd")
    sb_ = _sum_leading(part_b, "sum_part_ffn_bwd")
    so = _sum_leading(part_o, "sum_part_attn_out")
    si = _sum_leading(part_i, "sum_part_in_proj")
    scg = _sum_leading(pcg, "sum_part_conv_gate")
    scv = _sum_leading(pcv, "sum_part_conv_val")
    gb_f = gb8[:nh, 0]
    dmod = jnp.concatenate([si[0], si[1], so[0], sb_[0], sb_[1], sf[1]])
    g_conv_w = jnp.concatenate([scg[0:3, :dff], scv[0:3, :dff]], axis=1).reshape(-1)
    g_conv_b = jnp.concatenate([scg[3, :dff], scv[3, :dff]])
    loss_part = jnp.sum(sf[2])
    fields = [dmod, si[2], gb_f, so[1, :dg], so[1, dg:], sb_[2], g_conv_b, sf[0], g_conv_w, loss_part[None]]
    sizes = [int(f.shape[0]) for f in fields]
    n_pack = sum(sizes)
    lanes_pack = -(-n_pack // (8 * LANES)) * LANES
    pack = jnp.pad(jnp.concatenate(fields), (0, 8 * lanes_pack - n_pack)).reshape(8, lanes_pack)
    gathered = _all_gather8(pack)
    tot = _sum_leading(gathered.reshape(8, 8, lanes_pack), "sum_pack").reshape(-1)
    offs = np.concatenate([[0], np.cumsum(sizes)])
    take = lambda k: tot[int(offs[k]):int(offs[k + 1])]
    g_b_ada, g_g_attn, g_b_fgate, g_g_fox, g_g_sb, g_g_mlp, g_cb, g_g_final, g_cw_full, loss_v = [take(k) for k in range(10)]
    loss = loss_v[0]
    dmod_all = gathered.reshape(8, 8 * lanes_pack)[:, :6 * d]
    dmod_cols = lax.dynamic_slice(dmod_all, (0, shard * ada_cols), (8, ada_cols))
    g_w_ada = _ada_bwd(sc_all.T, dmod_cols)

    def col_pieces(g):
        r, cc = g.shape
        return jnp.transpose(g.reshape(2, r // 2, 4, cc // 4), (2, 0, 1, 3)).reshape(8, r // 2, cc // 4)

    def row_pieces(g):
        r, cc = g.shape
        return g.reshape(8, r // 8, cc)

    pieces = (lane_pad(col_pieces(gw_in)), row_pieces(gw_out), lane_pad(col_pieces(gw_up)), row_pieces(gw_down))
    recv = _scatter8([_to_bf16(p, "pieces_bf16_" + nm) for p, nm in zip(pieces, ("w_in", "w_out", "w_up", "w_down"))])
    halves = [_sum_leading(rv, nm) for rv, nm in zip(recv, ("sum_w_in", "sum_w_out", "sum_w_up", "sum_w_down"))]
    swapped = _swap_halves(halves)
    shard_cols = (n_in // 4, d, dff // 2, d)
    g_w_in, g_w_out, g_w_up, g_w_down = [s.reshape(2 * s.shape[1], s.shape[2])[:, :cc] for s, cc in zip(swapped, shard_cols)]
    g_conv_w_shard = lax.dynamic_slice(g_cw_full.reshape(3, 2 * dff), (0, shard * (dff // 2)), (3, dff // 2))

    grads, deltas, new_m, new_v = {}, {}, {}, {}

    def step(name, w, g, m, v):
        shape = w.shape
        as2d = lambda a: a.reshape(-1, shape[-1])
        dl, nm, nv = _adamw(as2d(w), as2d(g), as2d(m), as2d(v), "adamw_" + name)
        grads[name], deltas[name], new_m[name], new_v[name] = g.reshape(shape), dl.reshape(shape), nm.reshape(shape), nv.reshape(shape)

    step("w_ada", w_ada, g_w_ada, m_w_ada, v_w_ada)
    step("w_in", w_in, g_w_in, m_w_in, v_w_in)
    step("w_out", w_out, g_w_out, m_w_out, v_w_out)
    step("w_up", w_up, g_w_up, m_w_up, v_w_up)
    step("conv_w", conv_w, g_conv_w_shard, m_conv_w, v_conv_w)
    step("w_down", w_down, g_w_down, m_w_down, v_w_down)

    small = [("b_ada", b_ada, g_b_ada, m_b_ada, v_b_ada), ("g_attn", g_attn, g_g_attn, m_g_attn, v_g_attn),
             ("b_fgate", b_fgate, g_b_fgate, m_b_fgate, v_b_fgate), ("g_out_fox", g_out_fox, g_g_fox, m_g_out_fox, v_g_out_fox),
             ("g_out_sb", g_out_sb, g_g_sb, m_g_out_sb, v_g_out_sb), ("g_mlp", g_mlp, g_g_mlp, m_g_mlp, v_g_mlp),
             ("conv_b", conv_b, g_cb, m_conv_b, v_conv_b), ("g_final", g_final, g_g_final, m_g_final, v_g_final)]
    ssz = [int(np.prod(s[1].shape)) for s in small]
    n_small = sum(ssz)
    lanes_small = -(-n_small // (8 * LANES)) * LANES
    packs = [jnp.pad(jnp.concatenate([s[k].reshape(-1) for s in small]), (0, 8 * lanes_small - n_small)).reshape(8, lanes_small)
             for k in (1, 2, 3, 4)]
    dl_s, nm_s, nv_s = _adamw(*packs, "adamw_small")
    so_ = np.concatenate([[0], np.cumsum(ssz)])
    for k, s in enumerate(small):
        cut = lambda a: a.reshape(-1)[int(so_[k]):int(so_[k + 1])].reshape(s[1].shape)
        grads[s[0]], deltas[s[0]], new_m[s[0]], new_v[s[0]] = s[2].reshape(s[1].shape), cut(dl_s), cut(nm_s), cut(nv_s)

    order = ["w_ada", "b_ada", "g_attn", "w_in", "b_fgate", "g_out_fox", "g_out_sb", "w_out", "g_mlp", "w_up",
             "conv_w", "conv_b", "w_down", "g_final"]
    return (loss, grad_x[None], *[grads[n] for n in order], *[deltas[n] for n in order],
            *[new_m[n] for n in order], *[new_v[n] for n in order])
```

```python
import functools

import numpy as np
import jax
import jax.numpy as jnp
from jax import lax
from jax.experimental import pallas as pl
from jax.experimental.pallas import tpu as pltpu

F32 = jnp.float32
BF16 = jnp.bfloat16
MESH = pl.DeviceIdType.MESH

HEAD_DIM = 64
LANES = 128
EPS = 1e-6
NEG = -1e30
ADAM_LR, ADAM_B1, ADAM_B2, ADAM_EPS, ADAM_WD, ADAM_STEP = 0.001, 0.9, 0.999, 1e-08, 0.01, 10
V7X_VMEM_BYTES = 64 * 1024 * 1024
VMEM_LIMIT = V7X_VMEM_BYTES - 12 * 1024 * 1024
NT_DIMS = (((1,), (1,)), ((), ()))
TN_DIMS = (((0,), (0,)), ((), ()))


def _pcall(body, **kw):
    return pl.pallas_call(body, **kw)


def _params(sem=None, **kw):
    return pltpu.CompilerParams(dimension_semantics=sem, vmem_limit_bytes=VMEM_LIMIT, **kw)


def _split_dot(x, m, passes):
    acc = None
    for _ in range(passes):
        part = x.astype(BF16)
        d = jnp.dot(part, m, preferred_element_type=F32)
        acc = d if acc is None else acc + d
        x = x - part.astype(F32)
    return acc


def _tile(n, candidates):
    for t in candidates:
        if n % t == 0:
            return t
    return n


def _rows_tile(rows, row_bytes, budget=2 * 1024 * 1024):
    best = None
    for t in range(8, rows + 1, 8):
        if rows % t == 0 and t * row_bytes <= budget:
            best = t
    return best if best is not None else rows


def _all_gather8(v):
    m_per, n = v.shape

    def body(x_ref, out_ref, send_sems, recv_sems, local_sem):
        x, y, c = lax.axis_index("x"), lax.axis_index("y"), lax.axis_index("c")
        me, sibling = (x, y, c), (x, y, 1 - c)
        chips = [(1 - x, y), (x, 1 - y), (1 - x, 1 - y)]

        def rows(px, py, pc):
            return out_ref.at[pl.ds((4 * px + 2 * py + pc) * m_per, m_per), :]

        def copy(k, block, to, src=None):
            return pltpu.make_async_remote_copy(
                src_ref=rows(*block) if src is None else src, dst_ref=rows(*block),
                send_sem=send_sems.at[k], recv_sem=recv_sems.at[k], device_id=to, device_id_type=MESH)

        mine = pltpu.make_async_copy(x_ref, rows(*me), local_sem)
        mine.start()
        first = [copy(0, me, sibling, src=x_ref)]
        first += [copy(1 + j, me, (*chip, c), src=x_ref) for j, chip in enumerate(chips)]
        for cp in first:
            cp.start()
        passed = [copy(4 + j, (*chip, c), sibling) for j, chip in enumerate(chips)]
        for j, chip in enumerate(chips):
            copy(1 + j, (*chip, c), me).wait_recv()
            passed[j].start()
        copy(0, sibling, me).wait_recv()
        for j, chip in enumerate(chips):
            copy(4 + j, (*chip, 1 - c), me).wait_recv()
        for cp in first + passed:
            cp.wait_send()
        mine.wait()

    return _pcall(
        body, name="all_gather8",
        out_shape=jax.ShapeDtypeStruct((8 * m_per, n), v.dtype),
        in_specs=[pl.BlockSpec(memory_space=pltpu.VMEM)],
        out_specs=pl.BlockSpec(memory_space=pltpu.VMEM),
        scratch_shapes=[pltpu.SemaphoreType.DMA((7,)), pltpu.SemaphoreType.DMA((7,)), pltpu.SemaphoreType.DMA],
        compiler_params=pltpu.CompilerParams(vmem_limit_bytes=VMEM_LIMIT),
    )(v)


def _gather_xy(shards):
    n = len(shards)

    def body(*refs):
        ins, outs = refs[:n], refs[n:2 * n]
        send_sems, recv_sems, local_sems = refs[2 * n:]
        x, y, c = lax.axis_index("x"), lax.axis_index("y"), lax.axis_index("c")
        chips = [(1 - x, y), (x, 1 - y), (1 - x, 1 - y)]
        mine = 2 * x + y
        local, remote = [], []
        for w in range(n):
            cp = pltpu.make_async_copy(ins[w], outs[w].at[mine], local_sems.at[w])
            cp.start()
            local.append(cp)
            for k, (px, py) in enumerate(chips):
                cp = pltpu.make_async_remote_copy(
                    src_ref=ins[w], dst_ref=outs[w].at[mine], send_sem=send_sems.at[3 * w + k],
                    recv_sem=recv_sems.at[3 * w + k], device_id=(px, py, c), device_id_type=MESH)
                cp.start()
                remote.append(cp)
        for cp in remote:
            cp.wait_recv()
        for cp in remote:
            cp.wait_send()
        for cp in local:
            cp.wait()

    hbm = pl.BlockSpec(memory_space=pltpu.HBM)
    return _pcall(
        body, name="gather_xy",
        out_shape=[jax.ShapeDtypeStruct((4,) + s.shape, s.dtype) for s in shards],
        in_specs=[hbm] * n, out_specs=[hbm] * n,
        scratch_shapes=[pltpu.SemaphoreType.DMA((3 * n,)), pltpu.SemaphoreType.DMA((3 * n,)),
                        pltpu.SemaphoreType.DMA((n,))],
        compiler_params=pltpu.CompilerParams(vmem_limit_bytes=VMEM_LIMIT),
    )(*shards)


def _scatter8(pieces):
    n = len(pieces)

    def body(*refs):
        ins, outs = refs[:n], refs[n:2 * n]
        send_sems, recv_sems, local_sems = refs[2 * n:]
        x, y, c = lax.axis_index("x"), lax.axis_index("y"), lax.axis_index("c")
        me = 4 * x + 2 * y + c
        local, remote = [], []
        for w in range(n):
            cp = pltpu.make_async_copy(ins[w].at[me], outs[w].at[me], local_sems.at[w])
            cp.start()
            local.append(cp)
            for f in range(1, 8):
                px = 1 - x if f & 4 else x
                py = 1 - y if f & 2 else y
                pc = 1 - c if f & 1 else c
                cp = pltpu.make_async_remote_copy(
                    src_ref=ins[w].at[4 * px + 2 * py + pc], dst_ref=outs[w].at[me],
                    send_sem=send_sems.at[7 * w + f - 1], recv_sem=recv_sems.at[7 * w + f - 1],
                    device_id=(px, py, pc), device_id_type=MESH)
                cp.start()
                remote.append(cp)
        for cp in remote:
            cp.wait_recv()
        for cp in remote:
            cp.wait_send()
        for cp in local:
            cp.wait()

    hbm = pl.BlockSpec(memory_space=pltpu.HBM)
    return _pcall(
        body, name="scatter8",
        out_shape=[jax.ShapeDtypeStruct(p.shape, p.dtype) for p in pieces],
        in_specs=[hbm] * n, out_specs=[hbm] * n,
        scratch_shapes=[pltpu.SemaphoreType.DMA((7 * n,)), pltpu.SemaphoreType.DMA((7 * n,)),
                        pltpu.SemaphoreType.DMA((n,))],
        compiler_params=pltpu.CompilerParams(vmem_limit_bytes=VMEM_LIMIT),
    )(*pieces)


def _swap_halves(halves):
    n = len(halves)
    chunks = 8
    n_chunks = [max(k for k in (chunks, 4, 2, 1) if h.shape[0] % (8 * k) == 0) for h in halves]

    def body(*refs):
        ins, outs = refs[:n], refs[n:2 * n]
        send_sems, recv_sems, local_sems = refs[2 * n:]
        x, y, c = lax.axis_index("x"), lax.axis_index("y"), lax.axis_index("c")
        local, remote = [], []
        for w in range(n):
            cp = pltpu.make_async_copy(ins[w], outs[w].at[c], local_sems.at[w])
            cp.start()
            local.append(cp)
            rows = ins[w].shape[0] // n_chunks[w]
            for k in range(n_chunks[w]):
                cp = pltpu.make_async_remote_copy(
                    src_ref=ins[w].at[pl.ds(k * rows, rows)], dst_ref=outs[w].at[c, pl.ds(k * rows, rows)],
                    send_sem=send_sems.at[chunks * w + k], recv_sem=recv_sems.at[chunks * w + k],
                    device_id=(x, y, 1 - c), device_id_type=MESH)
                cp.start()
                remote.append(cp)
        for cp in remote:
            cp.wait_recv()
        for cp in remote:
            cp.wait_send()
        for cp in local:
            cp.wait()

    hbm = pl.BlockSpec(memory_space=pltpu.HBM)
    return _pcall(
        body, name="swap_halves",
        out_shape=[jax.ShapeDtypeStruct((2,) + h.shape, h.dtype) for h in halves],
        in_specs=[hbm] * n, out_specs=[hbm] * n,
        scratch_shapes=[pltpu.SemaphoreType.DMA((chunks * n,)), pltpu.SemaphoreType.DMA((chunks * n,)),
                        pltpu.SemaphoreType.DMA((n,))],
        compiler_params=pltpu.CompilerParams(vmem_limit_bytes=VMEM_LIMIT),
    )(*halves)


def _sum_leading(a, name):
    n, r, c = a.shape
    tr = _rows_tile(r, n * c * 4, budget=6 * 1024 * 1024)
    if a.dtype == BF16 and tr % 16:
        tr = r

    def body(a_ref, o_ref):
        acc = a_ref[0].astype(F32)
        for k in range(1, n):
            acc = acc + a_ref[k].astype(F32)
        o_ref[...] = acc

    return _pcall(
        body, name=name, grid=(r // tr,),
        out_shape=jax.ShapeDtypeStruct((r, c), F32),
        in_specs=[pl.BlockSpec((n, tr, c), lambda i: (0, i, 0))],
        out_specs=pl.BlockSpec((tr, c), lambda i: (i, 0)),
        compiler_params=_params(("arbitrary",)),
    )(a)


def _to_bf16(a, name):
    n, r, c = a.shape

    def body(a_ref, o_ref):
        o_ref[...] = a_ref[...].astype(BF16)

    spec = pl.BlockSpec((1, r, c), lambda i: (i, 0, 0))
    return _pcall(
        body, name=name, grid=(n,), out_shape=jax.ShapeDtypeStruct(a.shape, BF16),
        in_specs=[spec], out_specs=spec, compiler_params=_params(("arbitrary",)),
    )(a)


def _adamw(w, g, m, v, name):
    r, c = w.shape
    tr = _rows_tile(r, c * 4, budget=1024 * 1024)
    c1 = 1.0 - ADAM_B1 ** ADAM_STEP
    c2 = 1.0 - ADAM_B2 ** ADAM_STEP

    def body(w_ref, g_ref, m_ref, v_ref, d_ref, nm_ref, nv_ref):
        gg = g_ref[...]
        nm = ADAM_B1 * m_ref[...] + (1.0 - ADAM_B1) * gg
        nv = ADAM_B2 * v_ref[...] + (1.0 - ADAM_B2) * (gg * gg)
        m_hat = nm / c1
        v_hat = nv / c2
        d_ref[...] = -ADAM_LR * (m_hat / (jnp.sqrt(v_hat) + ADAM_EPS) + ADAM_WD * w_ref[...])
        nm_ref[...] = nm
        nv_ref[...] = nv

    spec = pl.BlockSpec((tr, c), lambda i: (i, 0))
    return _pcall(
        body, name=name, grid=(r // tr,),
        out_shape=[jax.ShapeDtypeStruct((r, c), F32)] * 3,
        in_specs=[spec] * 4, out_specs=[spec] * 3,
        compiler_params=_params(("arbitrary",)),
    )(w, g, m, v)


def _ada_fwd(c_all, w_shard, b_shard):
    nb, d = c_all.shape
    cols = w_shard.shape[1]

    def body(c_ref, w_ref, b_ref, sc_ref, mod_ref):
        cv = c_ref[...]
        sc = cv * jax.nn.sigmoid(cv)
        sc_ref[...] = sc
        mod_ref[...] = jnp.dot(sc.astype(BF16), w_ref[...].astype(BF16), preferred_element_type=F32) + b_ref[...]

    return _pcall(
        body, name="ada_fwd",
        out_shape=[jax.ShapeDtypeStruct((nb, d), F32), jax.ShapeDtypeStruct((nb, cols), F32)],
        compiler_params=pltpu.CompilerParams(vmem_limit_bytes=VMEM_LIMIT),
    )(c_all, w_shard, b_shard)


def _ada_bwd(sc_t, dmod_cols):
    d, nb = sc_t.shape
    cols = dmod_cols.shape[1]
    tr = _rows_tile(d, cols * 4, budget=1024 * 1024)

    def body(s_ref, m_ref, o_ref):
        s = s_ref[...]
        m = m_ref[...]
        acc = s[:, 0:1] * m[0:1, :]
        for b in range(1, nb):
            acc = acc + s[:, b:b + 1] * m[b:b + 1, :]
        o_ref[...] = acc

    return _pcall(
        body, name="ada_bwd", grid=(d // tr,),
        out_shape=jax.ShapeDtypeStruct((d, cols), F32),
        in_specs=[pl.BlockSpec((tr, nb), lambda i: (i, 0)), pl.BlockSpec((nb, cols), lambda i: (0, 0))],
        out_specs=pl.BlockSpec((tr, cols), lambda i: (i, 0)),
        compiler_params=_params(("arbitrary",)),
    )(sc_t, dmod_cols)


def _log_sigmoid(x):
    return jnp.minimum(x, 0.0) - jnp.log1p(jnp.exp(-jnp.abs(x)))


def _fgate_fwd(fl2d, b_rows, tri_in, tri_blk):
    r = fl2d.shape[0]

    def body(x_ref, b_ref, u_ref, l_ref, f_ref):
        lf = _log_sigmoid(x_ref[...] + b_ref[...])
        c1 = _split_dot(lf, u_ref[...], 3)
        tot = jnp.broadcast_to(c1[:, LANES - 1:LANES], (r, LANES))
        acc = None
        for _ in range(3):
            part = tot.astype(BF16)
            dd = jnp.dot(l_ref[...], part, preferred_element_type=F32)
            acc = dd if acc is None else acc + dd
            tot = tot - part.astype(F32)
        f_ref[...] = c1 + acc

    return _pcall(
        body, name="fgate_fwd", out_shape=jax.ShapeDtypeStruct((r, LANES), F32),
        compiler_params=pltpu.CompilerParams(vmem_limit_bytes=VMEM_LIMIT),
    )(fl2d, b_rows, tri_in, tri_blk)


def _fgate_bwd(fl2d, b_rows, df_query, df_key, tri_in_rev, tri_blk_rev, head_rows):
    r = fl2d.shape[0]
    nhp = head_rows.shape[0]

    def body(x_ref, b_ref, dq_ref, dk_ref, u_ref, l_ref, hr_ref, o_ref, gb_ref):
        c1 = _split_dot(dq_ref[...] + dk_ref[...], u_ref[...], 3)
        tot = jnp.broadcast_to(c1[:, 0:1], (r, LANES))
        acc = None
        for _ in range(3):
            part = tot.astype(BF16)
            dd = jnp.dot(l_ref[...], part, preferred_element_type=F32)
            acc = dd if acc is None else acc + dd
            tot = tot - part.astype(F32)
        x = x_ref[...] + b_ref[...]
        e = jnp.exp(-jnp.abs(x))
        dfl = (c1 + acc) * (jnp.where(x >= 0, e, 1.0) / (1.0 + e))
        o_ref[...] = dfl
        rs = jnp.broadcast_to(jnp.sum(dfl, axis=1, keepdims=True), (r, LANES))
        gb = None
        for _ in range(3):
            part = rs.astype(BF16)
            dd = jnp.dot(hr_ref[...], part, preferred_element_type=F32)
            gb = dd if gb is None else gb + dd
            rs = rs - part.astype(F32)
        gb_ref[...] = gb

    return _pcall(
        body, name="fgate_bwd",
        out_shape=[jax.ShapeDtypeStruct((r, LANES), F32), jax.ShapeDtypeStruct((nhp, LANES), F32)],
        compiler_params=pltpu.CompilerParams(vmem_limit_bytes=VMEM_LIMIT),
    )(fl2d, b_rows, df_query, df_key, tri_in_rev, tri_blk_rev, head_rows)


def _norm_mod(x, g, scale, shift):
    r = lax.rsqrt(jnp.mean(x * x, axis=-1, keepdims=True) + EPS)
    return (x * r * g) * (1.0 + scale) + shift


def _norm_mod_bwd(x, dh, g, scale):
    r = lax.rsqrt(jnp.mean(x * x, axis=-1, keepdims=True) + EPS)
    xn = x * r
    dshift = jnp.sum(dh, axis=0, keepdims=True)
    dscale = jnp.sum(dh * (xn * g), axis=0, keepdims=True)
    dxn_g = dh * (1.0 + scale)
    dg = jnp.sum(dxn_g * xn, axis=0, keepdims=True)
    dxn = dxn_g * g
    dx = r * (dxn - xn * jnp.mean(dxn * xn, axis=-1, keepdims=True))
    return dx, dshift, dscale, dg


def _in_proj_fwd(x, mod8, g_attn, w_qkv, w_f, tm):
    t, d = x.shape
    dg = w_qkv.shape[1] // 6

    def body(x_ref, mod_ref, g_ref, w_ref, wf_ref, qkv_ref, fl_ref, h1_ref, h_sc):
        j = pl.program_id(1)

        @pl.when(j == 0)
        def _():
            h = _norm_mod(x_ref[...], g_ref[...], mod_ref[1:2, :], mod_ref[0:1, :]).astype(BF16)
            h_sc[...] = h
            h1_ref[...] = h
            fl_ref[...] = jnp.dot(h, wf_ref[...], preferred_element_type=F32)

        s = jnp.where((j == 0) | (j == 3), HEAD_DIM ** -0.5, 1.0)
        qkv_ref[...] = (jnp.dot(h_sc[...], w_ref[...], preferred_element_type=F32) * s).astype(BF16)

    return _pcall(
        body, name="in_proj_fwd", grid=(t // tm, 6),
        out_shape=[jax.ShapeDtypeStruct((t, 6 * dg), BF16), jax.ShapeDtypeStruct((t, LANES), F32),
                   jax.ShapeDtypeStruct((t, d), BF16)],
        in_specs=[pl.BlockSpec((tm, d), lambda i, j: (i, 0)), pl.BlockSpec((8, d), lambda i, j: (0, 0)),
                  pl.BlockSpec((1, d), lambda i, j: (0, 0)), pl.BlockSpec((d, dg), lambda i, j: (0, j)),
                  pl.BlockSpec((d, LANES), lambda i, j: (0, 0))],
        out_specs=[pl.BlockSpec((tm, dg), lambda i, j: (i, j)), pl.BlockSpec((tm, LANES), lambda i, j: (i, 0)),
                   pl.BlockSpec((tm, d), lambda i, j: (i, 0))],
        scratch_shapes=[pltpu.VMEM((tm, d), BF16)],
        compiler_params=_params(("arbitrary", "arbitrary")),
    )(x, mod8, g_attn, w_qkv, w_f)


def _head_rstd(o, bd):
    return lax.rsqrt(_split_dot(o * o, bd, 3) * (1.0 / HEAD_DIM) + EPS)


def _attn_out_fwd(x, o_fox, o_sb, g_fox, g_sb, w_out, mod8, bd, tm):
    t, d = x.shape
    dg = o_fox.shape[0]

    def body(x_ref, of_ref, os_ref, gf_ref, gs_ref, w_ref, mod_ref, bd_ref, x2_ref, mix_ref):
        of, osb = of_ref[...].T, os_ref[...].T
        mf = (of * _head_rstd(of, bd_ref[...]) * gf_ref[...]).astype(BF16)
        ms = (osb * _head_rstd(osb, bd_ref[...]) * gs_ref[...]).astype(BF16)
        mix_ref[:, :dg] = mf
        mix_ref[:, dg:] = ms
        y = jnp.dot(mf, w_ref[:dg, :], preferred_element_type=F32) + jnp.dot(ms, w_ref[dg:, :], preferred_element_type=F32)
        x2_ref[...] = x_ref[...] + mod_ref[2:3, :] * y

    row = lambda w: pl.BlockSpec((tm, w), lambda i: (i, 0))
    full = lambda a: pl.BlockSpec(a.shape, lambda i: (0,) * a.ndim)
    return _pcall(
        body, name="attn_out_fwd", grid=(t // tm,),
        out_shape=[jax.ShapeDtypeStruct((t, d), F32), jax.ShapeDtypeStruct((t, 2 * dg), BF16)],
        in_specs=[row(d), pl.BlockSpec((dg, tm), lambda i: (0, i)), pl.BlockSpec((dg, tm), lambda i: (0, i)),
                  full(g_fox), full(g_sb), full(w_out), full(mod8), full(bd)],
        out_specs=[row(d), row(2 * dg)],
        compiler_params=_params(("arbitrary",)),
    )(x, o_fox, o_sb, g_fox, g_sb, w_out, mod8, bd)


def _attn_out_bwd(dx2, mix, o_fox, o_sb, g_fox, g_sb, w_out, mod8, bd, hsel, tm):
    t, d = dx2.shape
    dg = o_fox.shape[0]

    def body(dx_ref, mix_ref, of_ref, os_ref, gf_ref, gs_ref, w_ref, mod_ref, bd_ref, hs_ref,
             dof_ref, dos_ref, dlt_ref, dxg_ref, part_ref):
        dx = dx_ref[...]
        gate = mod_ref[2:3, :]
        dxg = (dx * gate).astype(BF16)
        dxg_ref[...] = dxg
        mixv = mix_ref[...]
        y = jnp.dot(mixv[:, :dg], w_ref[:dg, :], preferred_element_type=F32)
        y = y + jnp.dot(mixv[:, dg:], w_ref[dg:, :], preferred_element_type=F32)
        part_ref[0] = jnp.zeros((8, d), F32)
        part_ref[0, 0:1, :] = jnp.sum(dx * y, axis=0, keepdims=True)
        for grp, (o_ref, g_ref, do_ref) in enumerate(((of_ref, gf_ref, dof_ref), (os_ref, gs_ref, dos_ref))):
            dmix = lax.dot_general(dxg, w_ref[grp * dg:(grp + 1) * dg, :], NT_DIMS, preferred_element_type=F32)
            o = o_ref[...].T
            r = _head_rstd(o, bd_ref[...])
            n = o * r
            part_ref[0, 1:2, grp * dg:(grp + 1) * dg] = jnp.sum(dmix * n, axis=0, keepdims=True)
            dn = dmix * g_ref[...]
            mh = _split_dot(dn * n, bd_ref[...], 3) * (1.0 / HEAD_DIM)
            do = r * (dn - n * mh)
            do_ref[...] = do.astype(BF16)
            if grp == 0:
                dlt_ref[...] = _split_dot(do * o, hs_ref[...], 3)

    row = lambda w: pl.BlockSpec((tm, w), lambda i: (i, 0))
    full = lambda a: pl.BlockSpec(a.shape, lambda i: (0,) * a.ndim)
    nt = t // tm
    return _pcall(
        body, name="attn_out_bwd", grid=(nt,),
        out_shape=[jax.ShapeDtypeStruct((t, dg), BF16), jax.ShapeDtypeStruct((t, dg), BF16),
                   jax.ShapeDtypeStruct((t, LANES), F32), jax.ShapeDtypeStruct((t, d), BF16),
                   jax.ShapeDtypeStruct((nt, 8, d), F32)],
        in_specs=[row(d), row(2 * dg), pl.BlockSpec((dg, tm), lambda i: (0, i)), pl.BlockSpec((dg, tm), lambda i: (0, i)),
                  full(g_fox), full(g_sb), full(w_out), full(mod8),
                  full(bd), full(hsel)],
        out_specs=[row(dg), row(dg), row(LANES), row(d), pl.BlockSpec((1, 8, d), lambda i: (i, 0, 0))],
        compiler_params=_params(("arbitrary",)),
    )(dx2, mix, o_fox, o_sb, g_fox, g_sb, w_out, mod8, bd, hsel)


def _in_proj_bwd(dparts, dfl, w_qkv, w_f, x, dx2, mod8, g_attn, tm):
    t, d = x.shape
    dg = dparts[1].shape[1]

    def body(*refs):
        d_refs = refs[:6]
        dfl_ref, w_ref, wf_ref, x_ref, dx2_ref, mod_ref, g_ref, gx_ref, dp_ref, dflb_ref, part_ref = refs[6:]
        dh = None
        for k in range(6):
            dk = d_refs[k][...].T if k in (0, 3) else d_refs[k][...]
            if k in (0, 3):
                dk = dk * HEAD_DIM ** -0.5
            db = dk.astype(BF16)
            dp_ref[:, k * dg:(k + 1) * dg] = db
            term = lax.dot_general(db, w_ref[:, k * dg:(k + 1) * dg], NT_DIMS, preferred_element_type=F32)
            dh = term if dh is None else dh + term
        dfb = dfl_ref[...].astype(BF16)
        dflb_ref[...] = dfb
        dh = dh + lax.dot_general(dfb, wf_ref[...], NT_DIMS, preferred_element_type=F32)
        dx, dshift, dscale, dgn = _norm_mod_bwd(x_ref[...], dh, g_ref[...], mod_ref[1:2, :])
        gx_ref[...] = dx2_ref[...] + dx
        part_ref[0] = jnp.zeros((8, d), F32)
        part_ref[0, 0:1, :] = dshift
        part_ref[0, 1:2, :] = dscale
        part_ref[0, 2:3, :] = dgn

    row = lambda w: pl.BlockSpec((tm, w), lambda i: (i, 0))
    full = lambda a: pl.BlockSpec(a.shape, lambda i: (0,) * a.ndim)
    nt = t // tm
    return _pcall(
        body, name="in_proj_bwd", grid=(nt,),
        out_shape=[jax.ShapeDtypeStruct((t, d), F32), jax.ShapeDtypeStruct((t, 6 * dg), BF16),
                   jax.ShapeDtypeStruct((t, LANES), BF16), jax.ShapeDtypeStruct((nt, 8, d), F32)],
        in_specs=[pl.BlockSpec((dg, tm), lambda i: (0, i)), row(dg), row(dg)] * 2
        + [row(LANES), full(w_qkv), full(w_f), row(d), row(d), full(mod8), full(g_attn)],
        out_specs=[row(d), row(6 * dg), row(LANES), pl.BlockSpec((1, 8, d), lambda i: (i, 0, 0))],
        compiler_params=_params(("arbitrary",)),
    )(*dparts, dfl, w_qkv, w_f, x, dx2, mod8, g_attn)


def _matmul_tn(a, b, name):
    t, m = a.shape
    n = b.shape[1]
    a_t = a.T
    tm_ = _tile(m, (512, 256, 128))
    tn_ = _tile(n, (1024, 512, 256, 128))
    tk = _tile(t, (2048, 1024, 512, 256, 128))
    nk = t // tk

    def body(a_ref, b_ref, o_ref):
        k = pl.program_id(2)

        @pl.when(k == 0)
        def _():
            o_ref[...] = jnp.zeros_like(o_ref)

        o_ref[...] += jnp.dot(a_ref[...], b_ref[...], preferred_element_type=F32)

    return _pcall(
        body, name=name, grid=(m // tm_, n // tn_, nk),
        out_shape=jax.ShapeDtypeStruct((m, n), F32),
        in_specs=[pl.BlockSpec((tm_, tk), lambda i, j, k: (i, k)), pl.BlockSpec((tk, tn_), lambda i, j, k: (k, j))],
        out_specs=pl.BlockSpec((tm_, tn_), lambda i, j, k: (i, j)),
        compiler_params=_params(("arbitrary", "arbitrary", "arbitrary")),
    )(a_t, b)


HALO = 16


def _conv_taps(up_ext, cw, lo, rows):
    s1 = pltpu.roll(up_ext, 1, 0)
    s2 = pltpu.roll(up_ext, 2, 0)
    u = cw[2:3, :] * up_ext[lo:lo + rows] + cw[1:2, :] * s1[lo:lo + rows] + cw[0:1, :] * s2[lo:lo + rows] + cw[3:4, :]
    return u, s1, s2


def _ffn_fwd(x2, target, mod8, g_mlp, g_final, wg, wv, cwg, cwv, wd, tm, cf):
    t, d = x2.shape
    dfp = wg.shape[1]
    nt, nc = t // tm, dfp // cf
    hb = tm // HALO

    def body(x_ref, xp_ref, tg_ref, mod_ref, g_ref, gf_ref, wg_ref, wv_ref, cg_ref, cv_ref, wd_ref,
             dx3_ref, h2_ref, part_ref, h_sc, acc_sc):
        i, j = pl.program_id(0), pl.program_id(1)

        @pl.when(j == 0)
        def _():
            xe = jnp.concatenate([xp_ref[...], x_ref[...]], axis=0)
            h = _norm_mod(xe, g_ref[...], mod_ref[4:5, :], mod_ref[3:4, :]).astype(BF16)
            h_sc[...] = h
            h2_ref[...] = h[HALO:]
            acc_sc[...] = jnp.zeros_like(acc_sc)

        rowi = lax.broadcasted_iota(jnp.int32, (tm + HALO, 1), 0)
        keep = (rowi >= HALO) | (i > 0)
        hv = h_sc[...]
        upg = jnp.where(keep, jnp.dot(hv, wg_ref[...], preferred_element_type=F32), 0.0)
        upv = jnp.where(keep, jnp.dot(hv, wv_ref[...], preferred_element_type=F32), 0.0)
        ug, _, _ = _conv_taps(upg, cg_ref[...], HALO, tm)
        uv, _, _ = _conv_taps(upv, cv_ref[...], HALO, tm)
        act = (ug * jax.nn.sigmoid(ug) * uv).astype(BF16)
        acc_sc[...] += jnp.dot(act, wd_ref[...], preferred_element_type=F32)

        @pl.when(j == nc - 1)
        def _():
            y_ffn = acc_sc[...]
            x3 = x_ref[...] + mod_ref[5:6, :] * y_ffn
            r3 = lax.rsqrt(jnp.mean(x3 * x3, axis=-1, keepdims=True) + EPS)
            xn = x3 * r3
            gf = gf_ref[...]
            diff = xn * gf - tg_ref[...]
            dy = diff * (1.0 / d)
            dxn = dy * gf
            dx3 = r3 * (dxn - xn * jnp.mean(dxn * xn, axis=-1, keepdims=True))
            dx3_ref[...] = dx3
            part_ref[0] = jnp.zeros((8, d), F32)
            part_ref[0, 0:1, :] = jnp.sum(dy * xn, axis=0, keepdims=True)
            part_ref[0, 1:2, :] = jnp.sum(dx3 * y_ffn, axis=0, keepdims=True)
            part_ref[0, 2:3, :] = jnp.sum(diff * diff, axis=0, keepdims=True) * (0.5 / d)

    row = lambda w: pl.BlockSpec((tm, w), lambda i, j: (i, 0))
    full = lambda a: pl.BlockSpec(a.shape, lambda i, j: (0,) * a.ndim)
    return _pcall(
        body, name="ffn_fwd", grid=(nt, nc),
        out_shape=[jax.ShapeDtypeStruct((t, d), F32), jax.ShapeDtypeStruct((t, d), BF16),
                   jax.ShapeDtypeStruct((nt, 8, d), F32)],
        in_specs=[row(d), pl.BlockSpec((HALO, d), lambda i, j: (jnp.maximum(i * hb - 1, 0), 0)), row(d),
                  full(mod8), full(g_mlp), full(g_final),
                  pl.BlockSpec((d, cf), lambda i, j: (0, j)), pl.BlockSpec((d, cf), lambda i, j: (0, j)),
                  pl.BlockSpec((8, cf), lambda i, j: (0, j)), pl.BlockSpec((8, cf), lambda i, j: (0, j)),
                  pl.BlockSpec((cf, d), lambda i, j: (j, 0))],
        out_specs=[row(d), row(d), pl.BlockSpec((1, 8, d), lambda i, j: (i, 0, 0))],
        scratch_shapes=[pltpu.VMEM((tm + HALO, d), BF16), pltpu.VMEM((tm, d), F32)],
        compiler_params=_params(("arbitrary", "arbitrary")),
    )(x2, x2, target, mod8, g_mlp, g_final, wg, wv, cwg, cwv, wd)


def _ffn_bwd(x2, dx3, mod8, g_mlp, wg, wv, cwg, cwv, wd, tm, cf):
    t, d = x2.shape
    dfp = wg.shape[1]
    nt, nc = t // tm, dfp // cf
    hb = tm // HALO
    nhb = t // HALO
    ext = tm + 2 * HALO

    def body(x_ref, xp_ref, xn_ref, dx_ref, dxn_ref, mod_ref, g_ref, wg_ref, wv_ref, cg_ref, cv_ref, wd_ref,
             dx2_ref, dug_ref, duv_ref, act_ref, dxg_ref, part_ref, pcg_ref, pcv_ref, h_sc, dg_sc, dh_sc):
        i, j = pl.program_id(0), pl.program_id(1)

        @pl.when(j == 0)
        def _():
            xe = jnp.concatenate([xp_ref[...], x_ref[...], xn_ref[...]], axis=0)
            h_sc[...] = _norm_mod(xe, g_ref[...], mod_ref[4:5, :], mod_ref[3:4, :]).astype(BF16)
            de = (jnp.concatenate([dx_ref[...], dxn_ref[...]], axis=0) * mod_ref[5:6, :]).astype(BF16)
            dg_sc[...] = de
            dxg_ref[...] = de[:tm]
            dh_sc[...] = jnp.zeros_like(dh_sc)

        rowe = lax.broadcasted_iota(jnp.int32, (ext, 1), 0)
        keep_up = (rowe >= HALO) | (i > 0)
        rowu = lax.broadcasted_iota(jnp.int32, (tm + HALO, 1), 0)
        keep_du = (rowu < tm) | (i < nt - 1)
        hv = h_sc[...]
        upg = jnp.where(keep_up, jnp.dot(hv, wg_ref[...], preferred_element_type=F32), 0.0)
        upv = jnp.where(keep_up, jnp.dot(hv, wv_ref[...], preferred_element_type=F32), 0.0)
        cg, cv = cg_ref[...], cv_ref[...]
        ug, g1, g2 = _conv_taps(upg, cg, HALO, tm + HALO)
        uv, v1, v2 = _conv_taps(upv, cv, HALO, tm + HALO)
        dact = lax.dot_general(dg_sc[...], wd_ref[...], NT_DIMS, preferred_element_type=F32)
        sg = jax.nn.sigmoid(ug)
        sil = ug * sg
        act_ref[...] = (sil * uv)[:tm].astype(BF16)
        duv = jnp.where(keep_du, dact * sil, 0.0)
        dug = jnp.where(keep_du, dact * uv * (sg * (1.0 + ug * (1.0 - sg))), 0.0)

        def back(du, cw, up, s1, s2, pc_ref):
            n = tm + HALO
            dup = (cw[2:3, :] * du + cw[1:2, :] * pltpu.roll(du, n - 1, 0) + cw[0:1, :] * pltpu.roll(du, n - 2, 0))[:tm]
            dut = du[:tm]
            pc_ref[0] = jnp.zeros((8, cf), F32)
            pc_ref[0, 0:1, :] = jnp.sum(dut * s2[HALO:HALO + tm], axis=0, keepdims=True)
            pc_ref[0, 1:2, :] = jnp.sum(dut * s1[HALO:HALO + tm], axis=0, keepdims=True)
            pc_ref[0, 2:3, :] = jnp.sum(dut * up[HALO:HALO + tm], axis=0, keepdims=True)
            pc_ref[0, 3:4, :] = jnp.sum(dut, axis=0, keepdims=True)
            return dup.astype(BF16)

        dupg = back(dug, cg, upg, g1, g2, pcg_ref)
        dupv = back(duv, cv, upv, v1, v2, pcv_ref)
        dug_ref[...] = dupg
        duv_ref[...] = dupv
        dh_sc[...] += (lax.dot_general(dupg, wg_ref[...], NT_DIMS, preferred_element_type=F32)
                       + lax.dot_general(dupv, wv_ref[...], NT_DIMS, preferred_element_type=F32))

        @pl.when(j == nc - 1)
        def _():
            dx, dshift, dscale, dgn = _norm_mod_bwd(x_ref[...], dh_sc[...], g_ref[...], mod_ref[4:5, :])
            dx2_ref[...] = dx_ref[...] + dx
            part_ref[0] = jnp.zeros((8, d), F32)
            part_ref[0, 0:1, :] = dshift
            part_ref[0, 1:2, :] = dscale
            part_ref[0, 2:3, :] = dgn

    row = lambda w: pl.BlockSpec((tm, w), lambda i, j: (i, 0))
    prev = pl.BlockSpec((HALO, d), lambda i, j: (jnp.maximum(i * hb - 1, 0), 0))
    nxt = pl.BlockSpec((HALO, d), lambda i, j: (jnp.minimum((i + 1) * hb, nhb - 1), 0))
    full = lambda a: pl.BlockSpec(a.shape, lambda i, j: (0,) * a.ndim)
    chunk = pl.BlockSpec((tm, cf), lambda i, j: (i, j))
    pchunk = pl.BlockSpec((1, 8, cf), lambda i, j: (i, 0, j))
    return _pcall(
        body, name="ffn_bwd", grid=(nt, nc),
        out_shape=[jax.ShapeDtypeStruct((t, d), F32), jax.ShapeDtypeStruct((t, dfp), BF16),
                   jax.ShapeDtypeStruct((t, dfp), BF16), jax.ShapeDtypeStruct((t, dfp), BF16),
                   jax.ShapeDtypeStruct((t, d), BF16), jax.ShapeDtypeStruct((nt, 8, d), F32),
                   jax.ShapeDtypeStruct((nt, 8, dfp), F32), jax.ShapeDtypeStruct((nt, 8, dfp), F32)],
        in_specs=[row(d), prev, nxt, row(d), nxt, full(mod8), full(g_mlp),
                  pl.BlockSpec((d, cf), lambda i, j: (0, j)), pl.BlockSpec((d, cf), lambda i, j: (0, j)),
                  pl.BlockSpec((8, cf), lambda i, j: (0, j)), pl.BlockSpec((8, cf), lambda i, j: (0, j)),
                  pl.BlockSpec((cf, d), lambda i, j: (j, 0))],
        out_specs=[row(d), chunk, chunk, chunk, row(d), pl.BlockSpec((1, 8, d), lambda i, j: (i, 0, 0)), pchunk, pchunk],
        scratch_shapes=[pltpu.VMEM((ext, d), BF16), pltpu.VMEM((tm + HALO, d), BF16), pltpu.VMEM((tm, d), F32)],
        compiler_params=_params(("arbitrary", "arbitrary")),
    )(x2, x2, x2, dx3, dx3, mod8, g_mlp, wg, wv, cwg, cwv, wd)


def _head_masks():
    lane = lax.broadcasted_iota(jnp.int32, (1, LANES), 1)
    in_a = lane < HEAD_DIM
    return in_a, jnp.logical_not(in_a)


BLK = 2 * LANES


def _stack_heads(qkv, dg):
    t = qkv.shape[0]
    p = dg // LANES
    rows = _tile(t, (512, 256, 128))
    sub = rows // BLK

    def body(kf_ref, vf_ref, ks_ref, vs_ref, okf, ovf, oks, ovs):
        in_a, in_b = _head_masks()
        for src, dst in ((kf_ref, okf), (vf_ref, ovf), (ks_ref, oks), (vs_ref, ovs)):
            v = src[...]
            zero = jnp.zeros_like(v)
            va, vb = jnp.where(in_a, v, zero), jnp.where(in_b, v, zero)
            for s in range(sub):
                dst[0, s, :BLK, :] = va[s * BLK:(s + 1) * BLK]
                dst[0, s, BLK:, :] = vb[s * BLK:(s + 1) * BLK]

    col = lambda base: pl.BlockSpec((rows, LANES), lambda h, j: (j, base * p + h))
    out = pl.BlockSpec((1, sub, 2 * BLK, LANES), lambda h, j: (h, j, 0, 0))
    shape = jax.ShapeDtypeStruct((p, t // BLK, 2 * BLK, LANES), BF16)
    return _pcall(
        body, name="stack_heads", grid=(p, t // rows),
        out_shape=[shape] * 4, in_specs=[col(1), col(2), col(4), col(5)], out_specs=[out] * 4,
        compiler_params=_params(("arbitrary", "arbitrary")),
    )(qkv, qkv, qkv, qkv)


def _tile_masks():
    rowi = lax.broadcasted_iota(jnp.int32, (BLK, BLK), 0)
    coli = lax.broadcasted_iota(jnp.int32, (BLK, BLK), 1)
    return coli <= rowi, coli < rowi


def _pair_triangle(suffix):
    r = lax.broadcasted_iota(jnp.int32, (BLK, BLK), 0)
    c = lax.broadcasted_iota(jnp.int32, (BLK, BLK), 1)
    return ((r >= c) if suffix else (r <= c)).astype(BF16)


def _pair_cumsum(x2, tri, passes):
    return jnp.concatenate([_split_dot(x2[:, :BLK], tri, passes), _split_dot(x2[:, BLK:], tri, passes)], axis=1)


def _pair_specs(t, dg, base):
    p = dg // LANES
    q = pl.BlockSpec((BLK, LANES), lambda h, i: (i, base * p + h))
    kv = pl.BlockSpec((1, t // BLK, 2 * BLK, LANES), lambda h, i: (h, 0, 0, 0))
    return q, kv


def _fox_fwd(qkv, kst, vst, fcol, frow2, dg):
    t = qkv.shape[0]
    p, nq = dg // LANES, t // BLK
    nh = 2 * p

    def body(q_ref, k_ref, v_ref, ft_ref, fs_ref, o_ref, lse_ref):
        i = pl.program_id(1)
        in_a, _ = _head_masks()
        causal, _ = _tile_masks()
        q2 = q_ref[...]
        ft = tuple(jnp.broadcast_to(ft_ref[a], (BLK, BLK)) for a in range(2))

        def tile(j, carry, masked):
            m, l, acc = carry
            kb, vb = k_ref[0, j], v_ref[0, j]
            s2 = lax.dot_general(q2, kb, NT_DIMS, preferred_element_type=F32)
            fs = fs_ref[0, j]
            m_new, l_new, alpha, pr = [], [], [], []
            for a in range(2):
                sl = slice(a * BLK, (a + 1) * BLK)
                s = (s2[:, sl] + ft[a]) - fs[:, sl]
                if masked:
                    s = jnp.where(causal, s, NEG)
                mn = jnp.maximum(m[a], jnp.max(s, axis=1, keepdims=True))
                pa = jnp.exp(s - mn)
                al = jnp.exp(m[a] - mn)
                m_new.append(mn)
                alpha.append(al)
                l_new.append(al * l[a] + jnp.sum(pa, axis=1, keepdims=True))
                pr.append(pa.astype(BF16))
            acc = jnp.where(in_a, alpha[0], alpha[1]) * acc + jnp.dot(
                jnp.concatenate(pr, axis=1), vb, preferred_element_type=F32)
            return tuple(m_new), tuple(l_new), acc

        neg, zero = jnp.full((BLK, 1), NEG, F32), jnp.zeros((BLK, 1), F32)
        carry = lax.fori_loop(0, i, functools.partial(tile, masked=False), ((neg, neg), (zero, zero), jnp.zeros((BLK, LANES), F32)))
        m, l, acc = tile(i, carry, True)
        o_ref[...] = acc / jnp.where(in_a, l[0], l[1])
        lse_ref[0] = m[0] + jnp.log(l[0])
        lse_ref[1] = m[1] + jnp.log(l[1])

    qs, kv = _pair_specs(t, dg, 0)
    col = pl.BlockSpec((2, BLK, 1), lambda h, i: (h, i, 0))
    return _pcall(
        body, name="fox_fwd", grid=(p, nq),
        out_shape=[jax.ShapeDtypeStruct((t, dg), F32), jax.ShapeDtypeStruct((nh, t, 1), F32)],
        in_specs=[qs, kv, kv, col, pl.BlockSpec((1, nq, 1, 2 * BLK), lambda h, i: (h, 0, 0, 0))],
        out_specs=[pl.BlockSpec((BLK, LANES), lambda h, i: (i, h)), col],
        compiler_params=_params(("arbitrary", "arbitrary")),
    )(qkv, kst, vst, fcol, frow2)


def _fold_heads(stacked, in_a):
    return jnp.where(in_a, stacked[:BLK], stacked[BLK:])


def _fox_bwd(qkv, kst, vst, do, fcol, frow2, lse, delta, dg):
    t = qkv.shape[0]
    p, nq = dg // LANES, t // BLK
    nh = 2 * p

    def body(q_ref, k_ref, v_ref, do_ref, ft_ref, fs_ref, lse_ref, dl_ref, dq_ref, dk_ref, dv_ref, dfs_ref, dft_ref):
        i = pl.program_id(1)

        @pl.when(i == 0)
        def _():
            dk_ref[...] = jnp.zeros_like(dk_ref)
            dv_ref[...] = jnp.zeros_like(dv_ref)
            dfs_ref[...] = jnp.zeros_like(dfs_ref)

        in_a, _ = _head_masks()
        causal, _ = _tile_masks()
        q2, do2 = q_ref[...], do_ref[...]
        ft = tuple(jnp.broadcast_to(ft_ref[a] - lse_ref[a], (BLK, BLK)) for a in range(2))
        dl = tuple(jnp.broadcast_to(dl_ref[a], (BLK, BLK)) for a in range(2))

        def tile(j, carry, masked):
            dq, dft = carry
            kb, vb = k_ref[0, j], v_ref[0, j]
            s2 = lax.dot_general(q2, kb, NT_DIMS, preferred_element_type=F32)
            dp2 = lax.dot_general(do2, vb, NT_DIMS, preferred_element_type=F32)
            fs = fs_ref[0, j]
            pr, ds, dft_new = [], [], []
            for a in range(2):
                sl = slice(a * BLK, (a + 1) * BLK)
                s = (s2[:, sl] + ft[a]) - fs[:, sl]
                if masked:
                    s = jnp.where(causal, s, NEG)
                pa = jnp.exp(s)
                dsa = pa * (dp2[:, sl] - dl[a])
                pr.append(pa.astype(BF16))
                ds.append(dsa)
                dft_new.append(dft[a] + jnp.sum(dsa, axis=1, keepdims=True))
            ds2 = jnp.concatenate(ds, axis=1)
            dsb = ds2.astype(BF16)
            off = pl.multiple_of(j * BLK, BLK)
            dk_ref[pl.ds(off, BLK), :] += _fold_heads(lax.dot_general(dsb, q2, TN_DIMS, preferred_element_type=F32), in_a)
            dv_ref[pl.ds(off, BLK), :] += _fold_heads(
                lax.dot_general(jnp.concatenate(pr, axis=1), do2, TN_DIMS, preferred_element_type=F32), in_a)
            dfs_ref[0, j] += -jnp.sum(ds2, axis=0, keepdims=True)
            return dq + jnp.dot(dsb, kb, preferred_element_type=F32), tuple(dft_new)

        zero = jnp.zeros((BLK, 1), F32)
        carry = lax.fori_loop(0, i, functools.partial(tile, masked=False), (jnp.zeros((BLK, LANES), F32), (zero, zero)))
        dq, dft = tile(i, carry, True)
        dq_ref[...] = dq
        dft_ref[0] = dft[0]
        dft_ref[1] = dft[1]

    qs, kv = _pair_specs(t, dg, 0)
    col = pl.BlockSpec((2, BLK, 1), lambda h, i: (h, i, 0))
    rowspec = pl.BlockSpec((1, nq, 1, 2 * BLK), lambda h, i: (h, 0, 0, 0))
    blk = pl.BlockSpec((BLK, LANES), lambda h, i: (i, h))
    acc = pl.BlockSpec((t, LANES), lambda h, i: (0, h))
    return _pcall(
        body, name="fox_bwd", grid=(p, nq),
        out_shape=[jax.ShapeDtypeStruct((t, dg), F32)] * 3 + [jax.ShapeDtypeStruct((p, nq, 1, 2 * BLK), F32),
                                                              jax.ShapeDtypeStruct((nh, t, 1), F32)],
        in_specs=[qs, kv, kv, blk, col, rowspec, col, col],
        out_specs=[blk, acc, acc, rowspec, col],
        compiler_params=_params(("arbitrary", "arbitrary")),
    )(qkv, kst, vst, do, fcol, frow2, lse, delta)


def _softplus_parts(z):
    e = jnp.exp(-jnp.abs(z))
    return jnp.maximum(z, 0.0) + jnp.log(1.0 + e), e


def _sigmoid_from(z, e):
    d = 1.0 + e
    r = pl.reciprocal(d, approx=True)
    r = r * (2.0 - d * r)
    return jnp.where(z >= 0, 1.0, e) * r


def _sb_fwd(qkv, kst, vst, dg):
    t = qkv.shape[0]
    p, nq = dg // LANES, t // BLK
    nh = 2 * p

    def body(q_ref, k_ref, v_ref, o_ref, rt_ref):
        i = pl.program_id(1)
        _, strict = _tile_masks()
        strict2 = jnp.concatenate([strict, strict], axis=1)
        suffix = _pair_triangle(True)
        q2 = q_ref[...]

        def tile(j, carry, masked):
            rest, acc = carry
            kb, vb = k_ref[0, j], v_ref[0, j]
            z = lax.dot_general(q2, kb, NT_DIMS, preferred_element_type=F32)
            sp, _ = _softplus_parts(z)
            if masked:
                sp = jnp.where(strict2, sp, 0.0)
            cs = _pair_cumsum(sp, suffix, 2)
            w, rest_new = [], []
            for a in range(2):
                sl = slice(a * BLK, (a + 1) * BLK)
                wa = jnp.exp(z[:, sl] - cs[:, sl] - rest[a])
                if masked:
                    wa = jnp.where(strict, wa, 0.0)
                w.append(wa.astype(BF16))
                rest_new.append(rest[a] + cs[:, a * BLK:a * BLK + 1])
            acc = acc + jnp.dot(jnp.concatenate(w, axis=1), vb, preferred_element_type=F32)
            return tuple(rest_new), acc

        zero = jnp.zeros((BLK, 1), F32)
        carry = tile(i, ((zero, zero), jnp.zeros((BLK, LANES), F32)), True)
        rest, acc = lax.fori_loop(0, i, lambda jj, c: tile(i - 1 - jj, c, False), carry)
        o_ref[...] = acc
        rt_ref[0] = rest[0]
        rt_ref[1] = rest[1]

    qs, kv = _pair_specs(t, dg, 3)
    col = pl.BlockSpec((2, BLK, 1), lambda h, i: (h, i, 0))
    return _pcall(
        body, name="sb_fwd", grid=(p, nq),
        out_shape=[jax.ShapeDtypeStruct((t, dg), F32), jax.ShapeDtypeStruct((nh, t, 1), F32)],
        in_specs=[qs, kv, kv],
        out_specs=[pl.BlockSpec((BLK, LANES), lambda h, i: (i, h)), col],
        compiler_params=_params(("arbitrary", "arbitrary")),
    )(qkv, kst, vst)


def _sb_bwd(qkv, kst, vst, do, rtot, dg):
    t = qkv.shape[0]
    p, nq = dg // LANES, t // BLK

    def body(q_ref, k_ref, v_ref, do_ref, rt_ref, dq_ref, dk_ref, dv_ref):
        i = pl.program_id(1)

        @pl.when(i == 0)
        def _():
            dk_ref[...] = jnp.zeros_like(dk_ref)
            dv_ref[...] = jnp.zeros_like(dv_ref)

        in_a, _ = _head_masks()
        _, strict = _tile_masks()
        strict2 = jnp.concatenate([strict, strict], axis=1)
        prefix = _pair_triangle(False)
        q2, do2 = q_ref[...], do_ref[...]
        rt = (rt_ref[0], rt_ref[1])

        def tile(j, carry, masked):
            before, gbefore, dq = carry
            kb, vb = k_ref[0, j], v_ref[0, j]
            z = lax.dot_general(q2, kb, NT_DIMS, preferred_element_type=F32)
            da = lax.dot_general(do2, vb, NT_DIMS, preferred_element_type=F32)
            sp, e = _softplus_parts(z)
            sig = _sigmoid_from(z, e)
            if masked:
                sp = jnp.where(strict2, sp, 0.0)
            pre = _pair_cumsum(sp, prefix, 2)
            w = []
            for a in range(2):
                sl = slice(a * BLK, (a + 1) * BLK)
                wa = jnp.exp(z[:, sl] + (before[a] - rt[a]) + pre[:, sl] - sp[:, sl])
                if masked:
                    wa = jnp.where(strict, wa, 0.0)
                w.append(wa)
            w2 = jnp.concatenate(w, axis=1)
            g = w2 * da
            preg = _pair_cumsum(g, prefix, 1)
            dz = []
            for a in range(2):
                sl = slice(a * BLK, (a + 1) * BLK)
                dza = g[:, sl] * (1.0 - sig[:, sl]) - sig[:, sl] * (gbefore[a] + preg[:, sl] - g[:, sl])
                if masked:
                    dza = jnp.where(strict, dza, 0.0)
                dz.append(dza.astype(BF16))
            dzb = jnp.concatenate(dz, axis=1)
            off = pl.multiple_of(j * BLK, BLK)
            dk_ref[pl.ds(off, BLK), :] += _fold_heads(lax.dot_general(dzb, q2, TN_DIMS, preferred_element_type=F32), in_a)
            dv_ref[pl.ds(off, BLK), :] += _fold_heads(
                lax.dot_general(w2.astype(BF16), do2, TN_DIMS, preferred_element_type=F32), in_a)
            last = lambda x, a: x[:, (a + 1) * BLK - 1:(a + 1) * BLK]
            return (tuple(before[a] + last(pre, a) for a in range(2)),
                    tuple(gbefore[a] + last(preg, a) for a in range(2)),
                    dq + jnp.dot(dzb, kb, preferred_element_type=F32))

        zero = jnp.zeros((BLK, 1), F32)
        carry = lax.fori_loop(0, i, functools.partial(tile, masked=False), ((zero, zero), (zero, zero), jnp.zeros((BLK, LANES), F32)))
        dq_ref[...] = tile(i, carry, True)[2]

    qs, kv = _pair_specs(t, dg, 3)
    col = pl.BlockSpec((2, BLK, 1), lambda h, i: (h, i, 0))
    blk = pl.BlockSpec((BLK, LANES), lambda h, i: (i, h))
    acc = pl.BlockSpec((t, LANES), lambda h, i: (0, h))
    return _pcall(
        body, name="sb_bwd", grid=(p, nq),
        out_shape=[jax.ShapeDtypeStruct((t, dg), F32)] * 3,
        in_specs=[qs, kv, kv, blk, col],
        out_specs=[blk, acc, acc],
        compiler_params=_params(("arbitrary", "arbitrary")),
    )(qkv, kst, vst, do, rtot)


XROWS = 144
LANE_FS, LANE_FT_A, LANE_FT_B = 0, 3, 6


def _pieces3(x):
    hi = x.astype(BF16).astype(F32)
    r = x - hi
    mid = r.astype(BF16).astype(F32)
    return hi, mid, (r - mid).astype(BF16).astype(F32)


def _bias_lanes(rows, entries):
    sub = lax.broadcasted_iota(jnp.int32, (16, 1), 0)
    out = jnp.zeros((16, rows), F32)
    for l, v in entries:
        out = jnp.where(sub == l, v, out)
    return jnp.concatenate([out, jnp.zeros((LANES - 16, rows), F32)], axis=0).T


def _three(first, values):
    return [(first + k, v) for k, v in enumerate(values)]


def _stack_rows(x, in_a, in_b):
    zero = jnp.zeros_like(x)
    return jnp.concatenate([jnp.where(in_a, x, zero), jnp.where(in_b, x, zero)], axis=0)


def _transposed(x):
    return x.astype(F32).T.astype(BF16)


def _attn_operands(qkv, fcol, dg):
    t = qkv.shape[0]
    p, nk = dg // LANES, t // BLK

    def body(qf_ref, kf_ref, vf_ref, ks_ref, vs_ref, f_ref, qx_ref, kx_ref, kxt_ref, vf_o, vft_o, ks_o, kst_o, vs_o, vst_o):
        in_a, in_b = _head_masks()
        fa, fb = _pieces3(f_ref[0]), _pieces3(f_ref[1])
        qx_ref[0, :, :LANES] = qf_ref[...]
        qx_ref[0, :, LANES:] = _bias_lanes(
            BLK, _three(LANE_FS, (-1.0,) * 3) + _three(LANE_FT_A, fa) + _three(LANE_FT_B, fb)).astype(BF16)
        kf = kf_ref[...]
        zero = jnp.zeros_like(kf)
        top = jnp.concatenate([jnp.where(in_a, kf, zero), _bias_lanes(
            BLK, _three(LANE_FS, fa) + _three(LANE_FT_A, (1.0,) * 3)).astype(BF16)], axis=1)
        bot = jnp.concatenate([jnp.where(in_b, kf, zero), _bias_lanes(
            BLK, _three(LANE_FS, fb) + _three(LANE_FT_B, (1.0,) * 3)).astype(BF16)], axis=1)
        kx = jnp.concatenate([top, bot], axis=0)
        kx_ref[0, 0] = kx
        kxt_ref[0, 0] = _transposed(kx)[:XROWS]
        for src, dst, dst_t in ((vf_ref, vf_o, vft_o), (ks_ref, ks_o, kst_o), (vs_ref, vs_o, vst_o)):
            st = _stack_rows(src[...], in_a, in_b)
            dst[0, 0] = st
            dst_t[0, 0] = _transposed(st)

    col = lambda base: pl.BlockSpec((BLK, LANES), lambda h, j: (j, base * p + h))
    blk4 = lambda r, c: pl.BlockSpec((1, 1, r, c), lambda h, j: (h, j, 0, 0))
    shp4 = lambda r, c: jax.ShapeDtypeStruct((p, nk, r, c), BF16)
    return _pcall(
        body, name="attn_operands", grid=(p, nk),
        out_shape=[jax.ShapeDtypeStruct((p, t, 2 * LANES), BF16), shp4(2 * BLK, 2 * LANES), shp4(XROWS, 2 * BLK)]
        + [shp4(2 * BLK, LANES), shp4(LANES, 2 * BLK)] * 3,
        in_specs=[col(0), col(1), col(2), col(4), col(5), pl.BlockSpec((2, 1, BLK), lambda h, j: (h, 0, j))],
        out_specs=[pl.BlockSpec((1, BLK, 2 * LANES), lambda h, j: (h, j, 0)), blk4(2 * BLK, 2 * LANES), blk4(XROWS, 2 * BLK)]
        + [blk4(2 * BLK, LANES), blk4(LANES, 2 * BLK)] * 3,
        compiler_params=_params(("arbitrary", "arbitrary")),
    )(qkv, qkv, qkv, qkv, qkv, fcol)


def _fox_q_bwd(qkv, fcol, lse_col, dg):
    t = qkv.shape[0]
    p = dg // LANES

    def body(q_ref, f_ref, l_ref, qx_ref):
        fa, fb = _pieces3(f_ref[0] - l_ref[0]), _pieces3(f_ref[1] - l_ref[1])
        qx_ref[0, :, :LANES] = q_ref[...]
        qx_ref[0, :, LANES:] = _bias_lanes(
            BLK, _three(LANE_FS, (-1.0,) * 3) + _three(LANE_FT_A, fa) + _three(LANE_FT_B, fb)).astype(BF16)

    row = pl.BlockSpec((2, 1, BLK), lambda h, j: (h, 0, j))
    return _pcall(
        body, name="fox_q_bwd", grid=(p, t // BLK),
        out_shape=jax.ShapeDtypeStruct((p, t, 2 * LANES), BF16),
        in_specs=[pl.BlockSpec((BLK, LANES), lambda h, j: (j, h)), row, row],
        out_specs=pl.BlockSpec((1, BLK, 2 * LANES), lambda h, j: (h, j, 0)),
        compiler_params=_params(("arbitrary", "arbitrary")),
    )(qkv, fcol, lse_col)


def _key_query_masks():
    key = lax.broadcasted_iota(jnp.int32, (BLK, BLK), 0)
    qry = lax.broadcasted_iota(jnp.int32, (BLK, BLK), 1)
    return key <= qry, key < qry


def _key_triangle(kind):
    s = lax.broadcasted_iota(jnp.int32, (BLK, BLK), 0)
    j = lax.broadcasted_iota(jnp.int32, (BLK, BLK), 1)
    return {"suffix": j >= s, "prefix": j <= s, "before": j < s}[kind].astype(BF16)


def _tri_dot(tri, x, passes):
    acc = None
    for _ in range(passes):
        part = x.astype(BF16)
        d = jnp.dot(tri, part, preferred_element_type=F32)
        acc = d if acc is None else acc + d
        x = x - part.astype(F32)
    return acc


GROUPS = (4, 2, 1)


def _loop_blocks(n, tiles, carry, descending=False, groups=GROUPS):
    at = (lambda k: n - 1 - k) if descending else (lambda k: k)
    done = 0
    for g in groups:
        left = n - done
        carry = lax.fori_loop(0, left // g, lambda h, c, g=g, done=done: tiles([at(done + g * h + k) for k in range(g)], c), carry)
        done = done + (left // g) * g
    return carry


def _resident(shape):
    return pl.BlockSpec((1,) + shape, lambda h, i: (h,) + (0,) * len(shape), pipeline_mode=pl.Buffered(1))


def _rows_per_head(a, b):
    return jnp.concatenate([jnp.broadcast_to(a, (HEAD_DIM, BLK)), jnp.broadcast_to(b, (HEAD_DIM, BLK))], axis=0)


def _fold_heads(stacked, in_a):
    return jnp.where(in_a, stacked[:BLK], stacked[BLK:])


def _fox_fwd(qx, kx, v_t, dg):
    p, t = qx.shape[0], qx.shape[1]
    nq = t // BLK
    nh = 2 * p

    def body(q_ref, k_ref, vt_ref, o_ref, lse_ref):
        i = pl.program_id(1)
        causal, _ = _key_query_masks()
        q = q_ref[0]

        def scores(j, masked):
            s2 = lax.dot_general(k_ref[0, j], q, NT_DIMS, preferred_element_type=F32)
            s = [s2[a * BLK:(a + 1) * BLK] for a in range(2)]
            return [jnp.where(causal, x, NEG) for x in s] if masked else s

        def update(blocks, carry):
            m, l, acc = carry
            m_new, l_new, alpha = [], [], []
            pr = [[] for _ in blocks]
            for a in range(2):
                mn = m[a]
                for _, s in blocks:
                    mn = jnp.maximum(mn, jnp.max(s[a], axis=0, keepdims=True))
                al = jnp.exp(m[a] - mn)
                ln = al * l[a]
                for k, (_, s) in enumerate(blocks):
                    pa = jnp.exp(s[a] - mn)
                    ln = ln + jnp.sum(pa, axis=0, keepdims=True)
                    pr[k].append(pa.astype(BF16))
                m_new.append(mn)
                alpha.append(al)
                l_new.append(ln)
            acc = _rows_per_head(*alpha) * acc
            for k, (j, _) in enumerate(blocks):
                acc = acc + jnp.dot(vt_ref[0, j], jnp.concatenate(pr[k], axis=0), preferred_element_type=F32)
            return tuple(m_new), tuple(l_new), acc

        tiles = lambda js, c: update([(j, scores(j, False)) for j in js], c)
        neg, zero = jnp.full((1, BLK), NEG, F32), jnp.zeros((1, BLK), F32)
        carry = _loop_blocks(i, tiles, ((neg, neg), (zero, zero), jnp.zeros((LANES, BLK), F32)))
        m, l, acc = update([(i, scores(i, True))], carry)
        o_ref[...] = acc / _rows_per_head(*l)
        lse_ref[0] = m[0] + jnp.log(l[0])
        lse_ref[1] = m[1] + jnp.log(l[1])

    row = pl.BlockSpec((2, 1, BLK), lambda h, i: (h, 0, i))
    return _pcall(
        body, name="fox_fwd", grid=(p, nq),
        out_shape=[jax.ShapeDtypeStruct((dg, t), F32), jax.ShapeDtypeStruct((nh, 1, t), F32)],
        in_specs=[pl.BlockSpec((1, BLK, 2 * LANES), lambda h, i: (h, i, 0)), _resident((nq, 2 * BLK, 2 * LANES)),
                  _resident((nq, LANES, 2 * BLK))],
        out_specs=[pl.BlockSpec((LANES, BLK), lambda h, i: (h, i)), row],
        compiler_params=_params(("arbitrary", "arbitrary")),
    )(qx, kx, v_t)


def _fox_bwd(qxb, kx, kx_t, v_st, do, delta, dg):
    p, t = qxb.shape[0], qxb.shape[1]
    nq = t // BLK
    nh = 2 * p

    def body(q_ref, k_ref, kt_ref, v_ref, do_ref, dl_ref, dq_ref, dft_ref, dk_ref, dv_ref, dkx_ref):
        i = pl.program_id(1)

        @pl.when(i == 0)
        def _():
            dk_ref[...] = jnp.zeros_like(dk_ref)
            dv_ref[...] = jnp.zeros_like(dv_ref)
            dkx_ref[...] = jnp.zeros_like(dkx_ref)

        in_a, _ = _head_masks()
        first_lane = lax.broadcasted_iota(jnp.int32, (1, LANES), 1) == 0
        causal, _ = _key_query_masks()
        q, do2 = q_ref[0], do_ref[...]
        dl = (dl_ref[0], dl_ref[1])

        def products(j):
            return (lax.dot_general(k_ref[0, j], q, NT_DIMS, preferred_element_type=F32),
                    lax.dot_general(v_ref[0, j], do2, NT_DIMS, preferred_element_type=F32))

        def dscores(prod, masked):
            s2, dp2 = prod
            pr, ds = [], []
            for a in range(2):
                s = s2[a * BLK:(a + 1) * BLK]
                if masked:
                    s = jnp.where(causal, s, NEG)
                pa = jnp.exp(s)
                ds.append((pa * (dp2[a * BLK:(a + 1) * BLK] - dl[a])).astype(BF16))
                pr.append(pa.astype(BF16))
            return jnp.concatenate(ds, axis=0), jnp.concatenate(pr, axis=0)

        def accumulate(j, dsb, prb, dq):
            off = pl.multiple_of(j * BLK, BLK)
            dk_full = jnp.dot(dsb, q, preferred_element_type=F32)
            dk_ref[pl.ds(off, BLK), :] += _fold_heads(dk_full[:, :LANES], in_a)
            dkx_ref[pl.ds(off, BLK), :] += jnp.where(first_lane, dk_full[:BLK, LANES:], dk_full[BLK:, LANES:])
            dv_ref[pl.ds(off, BLK), :] += _fold_heads(jnp.dot(prb, do2, preferred_element_type=F32), in_a)
            return dq + jnp.dot(kt_ref[0, j], dsb, preferred_element_type=F32)

        def tiles(js, dq, masked=False):
            prods = [products(j) for j in js]
            grads = [dscores(pr, masked) for pr in prods]
            for j, (dsb, prb) in zip(js, grads):
                dq = accumulate(j, dsb, prb, dq)
            return dq

        dq = _loop_blocks(i, tiles, jnp.zeros((XROWS, BLK), F32))
        dq = tiles([i], dq, True)
        dq_ref[...] = dq[:LANES]
        dft_ref[0] = dq[LANES + LANE_FT_A:LANES + LANE_FT_A + 1]
        dft_ref[1] = dq[LANES + LANE_FT_B:LANES + LANE_FT_B + 1]

    row = pl.BlockSpec((2, 1, BLK), lambda h, i: (h, 0, i))
    acc = pl.BlockSpec((t, LANES), lambda h, i: (0, h))
    return _pcall(
        body, name="fox_bwd", grid=(p, nq),
        out_shape=[jax.ShapeDtypeStruct((dg, t), F32), jax.ShapeDtypeStruct((nh, 1, t), F32)] + [jax.ShapeDtypeStruct((t, dg), F32)] * 3,
        in_specs=[pl.BlockSpec((1, BLK, 2 * LANES), lambda h, i: (h, i, 0)), _resident((nq, 2 * BLK, 2 * LANES)),
                  _resident((nq, XROWS, 2 * BLK)), _resident((nq, 2 * BLK, LANES)),
                  pl.BlockSpec((BLK, LANES), lambda h, i: (i, h)), row],
        out_specs=[pl.BlockSpec((LANES, BLK), lambda h, i: (h, i)), row, acc, acc, acc],
        compiler_params=_params(("arbitrary", "arbitrary")),
    )(qxb, kx, kx_t, v_st, do, delta)


def _softplus_of(z):
    return jnp.maximum(z, 0.0) + jnp.log(1.0 + jnp.exp(-jnp.abs(z)))


def _sb_fwd(qkv, k_st, v_t, dg):
    t = qkv.shape[0]
    p, nq = dg // LANES, t // BLK
    nh = 2 * p

    def body(q_ref, k_ref, vt_ref, o_ref, rt_ref):
        i = pl.program_id(1)
        _, strict = _key_query_masks()
        suffix = _key_triangle("suffix")
        q = q_ref[...]

        def scores(j):
            z2 = lax.dot_general(k_ref[0, j], q, NT_DIMS, preferred_element_type=F32)
            return [z2[a * BLK:(a + 1) * BLK] for a in range(2)]

        def suffix_sums(z, masked):
            out = []
            for a in range(2):
                sp = _softplus_of(z[a])
                if masked:
                    sp = jnp.where(strict, sp, 0.0)
                out.append(_tri_dot(suffix, sp, 2))
            return out

        def weights(z, cs, rest, masked):
            w, rest_new = [], []
            for a in range(2):
                wa = jnp.exp(z[a] - cs[a] - rest[a])
                if masked:
                    wa = jnp.where(strict, wa, 0.0)
                w.append(wa.astype(BF16))
                rest_new.append(rest[a] + cs[a][0:1])
            return jnp.concatenate(w, axis=0), tuple(rest_new)

        def tiles(js, carry, masked=False):
            rest, acc = carry
            zs = [scores(j) for j in js]
            css = [suffix_sums(z, masked) for z in zs]
            ws = []
            for z, cs in zip(zs, css):
                w2, rest = weights(z, cs, rest, masked)
                ws.append(w2)
            for j, w2 in zip(js, ws):
                acc = acc + jnp.dot(vt_ref[0, j], w2, preferred_element_type=F32)
            return rest, acc

        zero = jnp.zeros((1, BLK), F32)
        carry = tiles([i], ((zero, zero), jnp.zeros((LANES, BLK), F32)), True)
        rest, acc = _loop_blocks(i, tiles, carry, descending=True)
        o_ref[...] = acc
        rt_ref[0] = rest[0]
        rt_ref[1] = rest[1]

    return _pcall(
        body, name="sb_fwd", grid=(p, nq),
        out_shape=[jax.ShapeDtypeStruct((dg, t), F32), jax.ShapeDtypeStruct((nh, 1, t), F32)],
        in_specs=[pl.BlockSpec((BLK, LANES), lambda h, i: (i, 3 * p + h)), _resident((nq, 2 * BLK, LANES)),
                  _resident((nq, LANES, 2 * BLK))],
        out_specs=[pl.BlockSpec((LANES, BLK), lambda h, i: (h, i)), pl.BlockSpec((2, 1, BLK), lambda h, i: (h, 0, i))],
        compiler_params=_params(("arbitrary", "arbitrary")),
    )(qkv, k_st, v_t)


def _sb_bwd(qkv, k_st, k_t, v_st, do, rtot, dg):
    t = qkv.shape[0]
    p, nq = dg // LANES, t // BLK

    def body(q_ref, k_ref, kt_ref, v_ref, do_ref, rt_ref, dq_ref, dk_ref, dv_ref):
        i = pl.program_id(1)

        @pl.when(i == 0)
        def _():
            dk_ref[...] = jnp.zeros_like(dk_ref)
            dv_ref[...] = jnp.zeros_like(dv_ref)

        in_a, _ = _head_masks()
        _, strict = _key_query_masks()
        before_m, prefix_m = _key_triangle("before"), _key_triangle("prefix")
        q, do2 = q_ref[...], do_ref[...]
        rt = (rt_ref[0], rt_ref[1])

        def products(j):
            z2 = lax.dot_general(k_ref[0, j], q, NT_DIMS, preferred_element_type=F32)
            da2 = lax.dot_general(v_ref[0, j], do2, NT_DIMS, preferred_element_type=F32)
            return [z2[a * BLK:(a + 1) * BLK] for a in range(2)], [da2[a * BLK:(a + 1) * BLK] for a in range(2)]

        def softplus_sums(z, masked):
            sp = [_softplus_of(x) for x in z]
            if masked:
                sp = [jnp.where(strict, x, 0.0) for x in sp]
            return sp, [_tri_dot(before_m, x, 2) for x in sp]

        def weight_grads(z, da, sp, pre, before, masked):
            w, g, pg, before_new = [], [], [], []
            for a in range(2):
                wa = jnp.exp(z[a] + (before[a] - rt[a]) + pre[a])
                if masked:
                    wa = jnp.where(strict, wa, 0.0)
                ga = wa * da[a]
                w.append(wa.astype(BF16))
                g.append(ga)
                pg.append(jnp.dot(prefix_m, ga.astype(BF16), preferred_element_type=F32))
                before_new.append(before[a] + pre[a][BLK - 1:BLK] + sp[a][BLK - 1:BLK])
            return jnp.concatenate(w, axis=0), g, pg, tuple(before_new)

        def dlogits(sp, g, pg, gbefore, masked):
            dz, gbefore_new = [], []
            for a in range(2):
                s_incl = gbefore[a] + pg[a]
                dza = (g[a] - s_incl) + jnp.exp(-sp[a]) * s_incl
                if masked:
                    dza = jnp.where(strict, dza, 0.0)
                dz.append(dza.astype(BF16))
                gbefore_new.append(s_incl[BLK - 1:BLK])
            return jnp.concatenate(dz, axis=0), tuple(gbefore_new)

        def accumulate(j, dzb, wb, dq):
            off = pl.multiple_of(j * BLK, BLK)
            dk_ref[pl.ds(off, BLK), :] += _fold_heads(jnp.dot(dzb, q, preferred_element_type=F32), in_a)
            dv_ref[pl.ds(off, BLK), :] += _fold_heads(jnp.dot(wb, do2, preferred_element_type=F32), in_a)
            return dq + jnp.dot(kt_ref[0, j], dzb, preferred_element_type=F32)

        def tiles(js, carry, masked=False):
            before, gbefore, dq = carry
            prods = [products(j) for j in js]
            sums = [softplus_sums(z, masked) for z, _ in prods]
            grads = []
            for (z, da), (sp, pre) in zip(prods, sums):
                wb, g, pg, before = weight_grads(z, da, sp, pre, before, masked)
                grads.append((wb, g, pg))
            for j, (sp, _), (wb, g, pg) in zip(js, sums, grads):
                dzb, gbefore = dlogits(sp, g, pg, gbefore, masked)
                dq = accumulate(j, dzb, wb, dq)
            return before, gbefore, dq

        zero = jnp.zeros((1, BLK), F32)
        carry = _loop_blocks(i, tiles, ((zero, zero), (zero, zero), jnp.zeros((LANES, BLK), F32)), groups=(2, 1))
        dq_ref[...] = tiles([i], carry, True)[2]

    acc = pl.BlockSpec((t, LANES), lambda h, i: (0, h))
    return _pcall(
        body, name="sb_bwd", grid=(p, nq),
        out_shape=[jax.ShapeDtypeStruct((dg, t), F32)] + [jax.ShapeDtypeStruct((t, dg), F32)] * 2,
        in_specs=[pl.BlockSpec((BLK, LANES), lambda h, i: (i, 3 * p + h)), _resident((nq, 2 * BLK, LANES)),
                  _resident((nq, LANES, 2 * BLK)), _resident((nq, 2 * BLK, LANES)),
                  pl.BlockSpec((BLK, LANES), lambda h, i: (i, h)), pl.BlockSpec((2, 1, BLK), lambda h, i: (h, 0, i))],
        out_specs=[pl.BlockSpec((LANES, BLK), lambda h, i: (h, i)), acc, acc],
        compiler_params=_params(("arbitrary", "arbitrary")),
    )(qkv, k_st, k_t, v_st, do, rtot)


def _tri_constants(nh, t):
    nb = t // LANES
    r = nh * nb
    li = np.arange(LANES)
    tri_in = (li[:, None] <= li[None, :])
    ri = np.arange(r)
    same = (ri[:, None] // nb) == (ri[None, :] // nb)
    blk = same & (ri[None, :] < ri[:, None])
    blk_rev = same & (ri[None, :] > ri[:, None])
    head_rows = (np.arange(max(8, nh))[:, None] == (ri[None, :] // nb))
    as_bf16 = lambda a: jnp.asarray(a.astype(np.float32), BF16)
    return as_bf16(tri_in), as_bf16(blk), as_bf16(tri_in.T), as_bf16(blk_rev), as_bf16(head_rows)


def kernel(x, c, w_ada, b_ada, g_attn, w_in, b_fgate, g_out_fox, g_out_sb, w_out, g_mlp, w_up, conv_w, conv_b, w_down, g_final, loss_target, m_w_ada, m_b_ada, m_g_attn, m_w_in, m_b_fgate, m_g_out_fox, m_g_out_sb, m_w_out, m_g_mlp, m_w_up, m_conv_w, m_conv_b, m_w_down, m_g_final, v_w_ada, v_b_ada, v_g_attn, v_w_in, v_b_fgate, v_g_out_fox, v_g_out_sb, v_w_out, v_g_mlp, v_w_up, v_conv_w, v_conv_b, v_w_down, v_g_final):
    t, d = x.shape[1], x.shape[2]
    dg = d // 2
    nh = dg // HEAD_DIM
    n_in = 6 * dg + nh
    dff = w_down.shape[1] * 4
    dfp = -(-dff // 256) * 256
    cf = 256
    tm = _tile(t, (512, 256, 128))
    nq = t // BLK
    xi, yi, ci = lax.axis_index("x"), lax.axis_index("y"), lax.axis_index("c")
    shard = 2 * xi + yi
    me = 4 * xi + 2 * yi + ci

    x2d, tg2d = x[0], loss_target[0]

    c_all = _all_gather8(jnp.pad(c, ((0, 7), (0, 0)))).reshape(8, 8, d)[:, 0, :]
    ada_cols = w_ada.shape[2]
    b_shard = lax.dynamic_slice(b_ada, (0, shard * ada_cols), (1, ada_cols))
    sc_all, mod_shard = _ada_fwd(c_all, w_ada[0], b_shard)
    mod_all = _all_gather8(mod_shard).reshape(4, 2, 8, ada_cols)
    mod_me = lax.dynamic_index_in_dim(mod_all[:, 0], me, axis=1, keepdims=False)
    mod8 = jnp.pad(mod_me.reshape(6, d), ((0, 2), (0, 0)))

    lane_pad = lambda a: jnp.pad(a, ((0, 0),) * (a.ndim - 1) + ((0, -a.shape[-1] % LANES),))
    g_in, g_out, g_up, g_down, g_cw = _gather_xy(
        [lane_pad(w_in[0].astype(BF16)), w_out[0].astype(BF16), lane_pad(w_up[0].astype(BF16)), w_down[0].astype(BF16),
         lane_pad(conv_w[0])])
    g_in, g_up, g_cw = g_in[:, :, :n_in // 4], g_up[:, :, :dff // 2], g_cw[:, :, :dff // 2]
    w_in_full = jnp.transpose(g_in, (1, 0, 2)).reshape(d, n_in)
    w_qkv = w_in_full[:, :6 * dg]
    w_f = jnp.pad(w_in_full[:, 6 * dg:], ((0, 0), (0, LANES - nh)))
    w_out_full = g_out.reshape(2 * dg, d)
    w_up_full = jnp.transpose(g_up, (1, 0, 2)).reshape(d, 2 * dff)
    padc = ((0, 0), (0, dfp - dff))
    wg, wv = jnp.pad(w_up_full[:, :dff], padc), jnp.pad(w_up_full[:, dff:], padc)
    wd = jnp.pad(g_down.reshape(dff, d), ((0, dfp - dff), (0, 0)))
    cw_full = jnp.transpose(g_cw, (1, 0, 2)).reshape(3, 2 * dff)
    cw4 = jnp.concatenate([cw_full, conv_b], axis=0)
    cwg = jnp.pad(cw4[:, :dff], ((0, 4), (0, dfp - dff)))
    cwv = jnp.pad(cw4[:, dff:], ((0, 4), (0, dfp - dff)))

    qkv, fl, h1 = _in_proj_fwd(x2d, mod8, g_attn, w_qkv, w_f, tm)
    tri_in, tri_blk, tri_in_rev, tri_blk_rev, head_rows = _tri_constants(nh, t)
    fl2d = fl[:, :nh].T.reshape(nh * t // LANES, LANES)
    b_rows = jnp.repeat(b_fgate[0], t // LANES)[:, None]
    f2d = _fgate_fwd(fl2d, b_rows, tri_in, tri_blk)
    fcol = f2d.reshape(nh, 1, t)
    pairs = nh // 2
    qx, kx, kx_t, vf_st, vf_t, ks_st, ks_t, vs_st, vs_t = _attn_operands(qkv, fcol, dg)
    o_fox_t, lse = _fox_fwd(qx, kx, vf_t, dg)
    o_sb_t, rtot = _sb_fwd(qkv, ks_st, vs_t, dg)
    o_fox, o_sb = o_fox_t, o_sb_t
    li = np.arange(dg)
    bd = jnp.asarray((li[:, None] // HEAD_DIM == li[None, :] // HEAD_DIM).astype(np.float32), BF16)
    hsel = jnp.asarray((li[:, None] // HEAD_DIM == np.arange(LANES)[None, :]).astype(np.float32), BF16)
    x2, mix = _attn_out_fwd(x2d, o_fox, o_sb, g_out_fox, g_out_sb, w_out_full, mod8, bd, tm)
    g_final2 = g_final[None, :]
    dx3, h2, part_f = _ffn_fwd(x2, tg2d, mod8, g_mlp, g_final2, wg, wv, cwg, cwv, wd, tm, cf)

    dx2, dupg, dupv, act, dxg3, part_b, pcg, pcv = _ffn_bwd(x2, dx3, mod8, g_mlp, wg, wv, cwg, cwv, wd, tm, cf)
    do_fox, do_sb, delta, dxg2, part_o = _attn_out_bwd(dx2, mix, o_fox, o_sb, g_out_fox, g_out_sb, w_out_full, mod8, bd, hsel, tm)
    drow = delta[:, :nh].T.reshape(nh, 1, t)
    qxb = _fox_q_bwd(qkv, fcol, lse, dg)
    dq_f_t, dft, dk_f, dv_f, dkx = _fox_bwd(qxb, kx, kx_t, vf_st, do_fox, drow, dg)
    dq_s_t, dk_s, dv_s = _sb_bwd(qkv, ks_st, ks_t, vs_st, do_sb, rtot, dg)
    dq_f, dq_s = dq_f_t, dq_s_t
    f2d_shape = (nh * t // LANES, LANES)
    dfs = jnp.transpose(dkx.reshape(t, pairs, LANES)[:, :, :2], (1, 2, 0))
    dfl2d, gb8 = _fgate_bwd(fl2d, b_rows, dft.reshape(f2d_shape), dfs.reshape(f2d_shape), tri_in_rev, tri_blk_rev, head_rows)
    dfl = jnp.pad(dfl2d.reshape(nh, t).T, ((0, 0), (0, LANES - nh)))
    grad_x, dproj, dflb, part_i = _in_proj_bwd([dq_f, dk_f, dv_f, dq_s, dk_s, dv_s], dfl, w_qkv, w_f, x2d, dx2, mod8, g_attn, tm)

    gw_qkv = _matmul_tn(h1, dproj, "grad_w_qkv")
    gw_f = _matmul_tn(h1, dflb, "grad_w_f")
    gw_in = jnp.concatenate([gw_qkv, gw_f[:, :nh]], axis=1)
    gw_out = _matmul_tn(mix, dxg2, "grad_w_out")
    gw_upg = _matmul_tn(h2, dupg, "grad_w_up_gate")
    gw_upv = _matmul_tn(h2, dupv, "grad_w_up_val")
    gw_up = jnp.concatenate([gw_upg[:, :dff], gw_upv[:, :dff]], axis=1)
    gw_down = _matmul_tn(act, dxg3, "grad_w_down")[:dff]

    sf = _sum_leading(part_f, "sum_part_ffn_fwd")
    sb_ = _sum_leading(part_b, "sum_part_ffn_bwd")
    so = _sum_leading(part_o, "sum_part_attn_out")
    si = _sum_leading(part_i, "sum_part_in_proj")
    scg = _sum_leading(pcg, "sum_part_conv_gate")
    scv = _sum_leading(pcv, "sum_part_conv_val")
    gb_f = gb8[:nh, 0]
    dmod = jnp.concatenate([si[0], si[1], so[0], sb_[0], sb_[1], sf[1]])
    g_conv_w = jnp.concatenate([scg[0:3, :dff], scv[0:3, :dff]], axis=1).reshape(-1)
    g_conv_b = jnp.concatenate([scg[3, :dff], scv[3, :dff]])
    loss_part = jnp.sum(sf[2])
    fields = [dmod, si[2], gb_f, so[1, :dg], so[1, dg:], sb_[2], g_conv_b, sf[0], g_conv_w, loss_part[None]]
    sizes = [int(f.shape[0]) for f in fields]
    n_pack = sum(sizes)
    lanes_pack = -(-n_pack // (8 * LANES)) * LANES
    pack = jnp.pad(jnp.concatenate(fields), (0, 8 * lanes_pack - n_pack)).reshape(8, lanes_pack)
    gathered = _all_gather8(pack)
    tot = _sum_leading(gathered.reshape(8, 8, lanes_pack), "sum_pack").reshape(-1)
    offs = np.concatenate([[0], np.cumsum(sizes)])
    take = lambda k: tot[int(offs[k]):int(offs[k + 1])]
    g_b_ada, g_g_attn, g_b_fgate, g_g_fox, g_g_sb, g_g_mlp, g_cb, g_g_final, g_cw_full, loss_v = [take(k) for k in range(10)]
    loss = loss_v[0]
    dmod_all = gathered.reshape(8, 8 * lanes_pack)[:, :6 * d]
    dmod_cols = lax.dynamic_slice(dmod_all, (0, shard * ada_cols), (8, ada_cols))
    g_w_ada = _ada_bwd(sc_all.T, dmod_cols)

    def col_pieces(g):
        r, cc = g.shape
        return jnp.transpose(g.reshape(2, r // 2, 4, cc // 4), (2, 0, 1, 3)).reshape(8, r // 2, cc // 4)

    def row_pieces(g):
        r, cc = g.shape
        return g.reshape(8, r // 8, cc)

    pieces = (lane_pad(col_pieces(gw_in)), row_pieces(gw_out), lane_pad(col_pieces(gw_up)), row_pieces(gw_down))
    recv = _scatter8([_to_bf16(p, "pieces_bf16_" + nm) for p, nm in zip(pieces, ("w_in", "w_out", "w_up", "w_down"))])
    halves = [_sum_leading(rv, nm) for rv, nm in zip(recv, ("sum_w_in", "sum_w_out", "sum_w_up", "sum_w_down"))]
    swapped = _swap_halves(halves)
    shard_cols = (n_in // 4, d, dff // 2, d)
    g_w_in, g_w_out, g_w_up, g_w_down = [s.reshape(2 * s.shape[1], s.shape[2])[:, :cc] for s, cc in zip(swapped, shard_cols)]
    g_conv_w_shard = lax.dynamic_slice(g_cw_full.reshape(3, 2 * dff), (0, shard * (dff // 2)), (3, dff // 2))

    grads, deltas, new_m, new_v = {}, {}, {}, {}

    def step(name, w, g, m, v):
        shape = w.shape
        as2d = lambda a: a.reshape(-1, shape[-1])
        dl, nm, nv = _adamw(as2d(w), as2d(g), as2d(m), as2d(v), "adamw_" + name)
        grads[name], deltas[name], new_m[name], new_v[name] = g.reshape(shape), dl.reshape(shape), nm.reshape(shape), nv.reshape(shape)

    step("w_ada", w_ada, g_w_ada, m_w_ada, v_w_ada)
    step("w_in", w_in, g_w_in, m_w_in, v_w_in)
    step("w_out", w_out, g_w_out, m_w_out, v_w_out)
    step("w_up", w_up, g_w_up, m_w_up, v_w_up)
    step("conv_w", conv_w, g_conv_w_shard, m_conv_w, v_conv_w)
    step("w_down", w_down, g_w_down, m_w_down, v_w_down)

    small = [("b_ada", b_ada, g_b_ada, m_b_ada, v_b_ada), ("g_attn", g_attn, g_g_attn, m_g_attn, v_g_attn),
             ("b_fgate", b_fgate, g_b_fgate, m_b_fgate, v_b_fgate), ("g_out_fox", g_out_fox, g_g_fox, m_g_out_fox, v_g_out_fox),
             ("g_out_sb", g_out_sb, g_g_sb, m_g_out_sb, v_g_out_sb), ("g_mlp", g_mlp, g_g_mlp, m_g_mlp, v_g_mlp),
             ("conv_b", conv_b, g_cb, m_conv_b, v_conv_b), ("g_final", g_final, g_g_final, m_g_final, v_g_final)]
    ssz = [int(np.prod(s[1].shape)) for s in small]
    n_small = sum(ssz)
    lanes_small = -(-n_small // (8 * LANES)) * LANES
    packs = [jnp.pad(jnp.concatenate([s[k].reshape(-1) for s in small]), (0, 8 * lanes_small - n_small)).reshape(8, lanes_small)
             for k in (1, 2, 3, 4)]
    dl_s, nm_s, nv_s = _adamw(*packs, "adamw_small")
    so_ = np.concatenate([[0], np.cumsum(ssz)])
    for k, s in enumerate(small):
        cut = lambda a: a.reshape(-1)[int(so_[k]):int(so_[k + 1])].reshape(s[1].shape)
        grads[s[0]], deltas[s[0]], new_m[s[0]], new_v[s[0]] = s[2].reshape(s[1].shape), cut(dl_s), cut(nm_s), cut(nv_s)

    order = ["w_ada", "b_ada", "g_attn", "w_in", "b_fgate", "g_out_fox", "g_out_sb", "w_out", "g_mlp", "w_up",
             "conv_w", "conv_b", "w_down", "g_final"]
    return (loss, grad_x[None], *[grads[n] for n in order], *[deltas[n] for n in order],
            *[new_m[n] for n in order], *[new_v[n] for n in order])
```

```python
import functools

import numpy as np
import jax
import jax.numpy as jnp
from jax import lax
from jax.experimental import pallas as pl
from jax.experimental.pallas import tpu as pltpu

F32 = jnp.float32
BF16 = jnp.bfloat16
MESH = pl.DeviceIdType.MESH

HEAD_DIM = 64
LANES = 128
EPS = 1e-6
NEG = -1e30
ADAM_LR, ADAM_B1, ADAM_B2, ADAM_EPS, ADAM_WD, ADAM_STEP = 0.001, 0.9, 0.999, 1e-08, 0.01, 10
V7X_VMEM_BYTES = 64 * 1024 * 1024
VMEM_LIMIT = V7X_VMEM_BYTES - 12 * 1024 * 1024
NT_DIMS = (((1,), (1,)), ((), ()))
TN_DIMS = (((0,), (0,)), ((), ()))


def _pcall(body, **kw):
    return pl.pallas_call(body, **kw)


def _params(sem=None, **kw):
    return pltpu.CompilerParams(dimension_semantics=sem, vmem_limit_bytes=VMEM_LIMIT, **kw)


def _split_dot(x, m, passes):
    acc = None
    for _ in range(passes):
        part = x.astype(BF16)
        d = jnp.dot(part, m, preferred_element_type=F32)
        acc = d if acc is None else acc + d
        x = x - part.astype(F32)
    return acc


def _tile(n, candidates):
    for t in candidates:
        if n % t == 0:
            return t
    return n


def _rows_tile(rows, row_bytes, budget=2 * 1024 * 1024):
    best = None
    for t in range(8, rows + 1, 8):
        if rows % t == 0 and t * row_bytes <= budget:
            best = t
    return best if best is not None else rows


def _all_gather8(v):
    m_per, n = v.shape

    def body(x_ref, out_ref, send_sems, recv_sems, local_sem):
        x, y, c = lax.axis_index("x"), lax.axis_index("y"), lax.axis_index("c")
        me, sibling = (x, y, c), (x, y, 1 - c)
        chips = [(1 - x, y), (x, 1 - y), (1 - x, 1 - y)]

        def rows(px, py, pc):
            return out_ref.at[pl.ds((4 * px + 2 * py + pc) * m_per, m_per), :]

        def copy(k, block, to, src=None):
            return pltpu.make_async_remote_copy(
                src_ref=rows(*block) if src is None else src, dst_ref=rows(*block),
                send_sem=send_sems.at[k], recv_sem=recv_sems.at[k], device_id=to, device_id_type=MESH)

        mine = pltpu.make_async_copy(x_ref, rows(*me), local_sem)
        mine.start()
        first = [copy(0, me, sibling, src=x_ref)]
        first += [copy(1 + j, me, (*chip, c), src=x_ref) for j, chip in enumerate(chips)]
        for cp in first:
            cp.start()
        passed = [copy(4 + j, (*chip, c), sibling) for j, chip in enumerate(chips)]
        for j, chip in enumerate(chips):
            copy(1 + j, (*chip, c), me).wait_recv()
            passed[j].start()
        copy(0, sibling, me).wait_recv()
        for j, chip in enumerate(chips):
            copy(4 + j, (*chip, 1 - c), me).wait_recv()
        for cp in first + passed:
            cp.wait_send()
        mine.wait()

    return _pcall(
        body, name="all_gather8",
        out_shape=jax.ShapeDtypeStruct((8 * m_per, n), v.dtype),
        in_specs=[pl.BlockSpec(memory_space=pltpu.VMEM)],
        out_specs=pl.BlockSpec(memory_space=pltpu.VMEM),
        scratch_shapes=[pltpu.SemaphoreType.DMA((7,)), pltpu.SemaphoreType.DMA((7,)), pltpu.SemaphoreType.DMA],
        compiler_params=pltpu.CompilerParams(vmem_limit_bytes=VMEM_LIMIT),
    )(v)


def _gather_xy(shards):
    n = len(shards)

    def body(*refs):
        ins, outs = refs[:n], refs[n:2 * n]
        send_sems, recv_sems, local_sems = refs[2 * n:]
        x, y, c = lax.axis_index("x"), lax.axis_index("y"), lax.axis_index("c")
        chips = [(1 - x, y), (x, 1 - y), (1 - x, 1 - y)]
        mine = 2 * x + y
        local, remote = [], []
        for w in range(n):
            cp = pltpu.make_async_copy(ins[w], outs[w].at[mine], local_sems.at[w])
            cp.start()
            local.append(cp)
            for k, (px, py) in enumerate(chips):
                cp = pltpu.make_async_remote_copy(
                    src_ref=ins[w], dst_ref=outs[w].at[mine], send_sem=send_sems.at[3 * w + k],
                    recv_sem=recv_sems.at[3 * w + k], device_id=(px, py, c), device_id_type=MESH)
                cp.start()
                remote.append(cp)
        for cp in remote:
            cp.wait_recv()
        for cp in remote:
            cp.wait_send()
        for cp in local:
            cp.wait()

    hbm = pl.BlockSpec(memory_space=pltpu.HBM)
    return _pcall(
        body, name="gather_xy",
        out_shape=[jax.ShapeDtypeStruct((4,) + s.shape, s.dtype) for s in shards],
        in_specs=[hbm] * n, out_specs=[hbm] * n,
        scratch_shapes=[pltpu.SemaphoreType.DMA((3 * n,)), pltpu.SemaphoreType.DMA((3 * n,)),
                        pltpu.SemaphoreType.DMA((n,))],
        compiler_params=pltpu.CompilerParams(vmem_limit_bytes=VMEM_LIMIT),
    )(*shards)


def _scatter8(pieces):
    n = len(pieces)

    def body(*refs):
        ins, outs = refs[:n], refs[n:2 * n]
        send_sems, recv_sems, local_sems = refs[2 * n:]
        x, y, c = lax.axis_index("x"), lax.axis_index("y"), lax.axis_index("c")
        me = 4 * x + 2 * y + c
        local, remote = [], []
        for w in range(n):
            cp = pltpu.make_async_copy(ins[w].at[me], outs[w].at[me], local_sems.at[w])
            cp.start()
            local.append(cp)
            for f in range(1, 8):
                px = 1 - x if f & 4 else x
                py = 1 - y if f & 2 else y
                pc = 1 - c if f & 1 else c
                cp = pltpu.make_async_remote_copy(
                    src_ref=ins[w].at[4 * px + 2 * py + pc], dst_ref=outs[w].at[me],
                    send_sem=send_sems.at[7 * w + f - 1], recv_sem=recv_sems.at[7 * w + f - 1],
                    device_id=(px, py, pc), device_id_type=MESH)
                cp.start()
                remote.append(cp)
        for cp in remote:
            cp.wait_recv()
        for cp in remote:
            cp.wait_send()
        for cp in local:
            cp.wait()

    hbm = pl.BlockSpec(memory_space=pltpu.HBM)
    return _pcall(
        body, name="scatter8",
        out_shape=[jax.ShapeDtypeStruct(p.shape, p.dtype) for p in pieces],
        in_specs=[hbm] * n, out_specs=[hbm] * n,
        scratch_shapes=[pltpu.SemaphoreType.DMA((7 * n,)), pltpu.SemaphoreType.DMA((7 * n,)),
                        pltpu.SemaphoreType.DMA((n,))],
        compiler_params=pltpu.CompilerParams(vmem_limit_bytes=VMEM_LIMIT),
    )(*pieces)


def _swap_halves(halves):
    n = len(halves)
    chunks = 8
    n_chunks = [max(k for k in (chunks, 4, 2, 1) if h.shape[0] % (8 * k) == 0) for h in halves]

    def body(*refs):
        ins, outs = refs[:n], refs[n:2 * n]
        send_sems, recv_sems, local_sems = refs[2 * n:]
        x, y, c = lax.axis_index("x"), lax.axis_index("y"), lax.axis_index("c")
        local, remote = [], []
        for w in range(n):
            cp = pltpu.make_async_copy(ins[w], outs[w].at[c], local_sems.at[w])
            cp.start()
            local.append(cp)
            rows = ins[w].shape[0] // n_chunks[w]
            for k in range(n_chunks[w]):
                cp = pltpu.make_async_remote_copy(
                    src_ref=ins[w].at[pl.ds(k * rows, rows)], dst_ref=outs[w].at[c, pl.ds(k * rows, rows)],
                    send_sem=send_sems.at[chunks * w + k], recv_sem=recv_sems.at[chunks * w + k],
                    device_id=(x, y, 1 - c), device_id_type=MESH)
                cp.start()
                remote.append(cp)
        for cp in remote:
            cp.wait_recv()
        for cp in remote:
            cp.wait_send()
        for cp in local:
            cp.wait()

    hbm = pl.BlockSpec(memory_space=pltpu.HBM)
    return _pcall(
        body, name="swap_halves",
        out_shape=[jax.ShapeDtypeStruct((2,) + h.shape, h.dtype) for h in halves],
        in_specs=[hbm] * n, out_specs=[hbm] * n,
        scratch_shapes=[pltpu.SemaphoreType.DMA((chunks * n,)), pltpu.SemaphoreType.DMA((chunks * n,)),
                        pltpu.SemaphoreType.DMA((n,))],
        compiler_params=pltpu.CompilerParams(vmem_limit_bytes=VMEM_LIMIT),
    )(*halves)


def _sum_leading(a, name):
    n, r, c = a.shape
    tr = _rows_tile(r, n * c * 4, budget=6 * 1024 * 1024)
    if a.dtype == BF16 and tr % 16:
        tr = r

    def body(a_ref, o_ref):
        acc = a_ref[0].astype(F32)
        for k in range(1, n):
            acc = acc + a_ref[k].astype(F32)
        o_ref[...] = acc

    return _pcall(
        body, name=name, grid=(r // tr,),
        out_shape=jax.ShapeDtypeStruct((r, c), F32),
        in_specs=[pl.BlockSpec((n, tr, c), lambda i: (0, i, 0))],
        out_specs=pl.BlockSpec((tr, c), lambda i: (i, 0)),
        compiler_params=_params(("arbitrary",)),
    )(a)


def _to_bf16(a, name):
    n, r, c = a.shape

    def body(a_ref, o_ref):
        o_ref[...] = a_ref[...].astype(BF16)

    spec = pl.BlockSpec((1, r, c), lambda i: (i, 0, 0))
    return _pcall(
        body, name=name, grid=(n,), out_shape=jax.ShapeDtypeStruct(a.shape, BF16),
        in_specs=[spec], out_specs=spec, compiler_params=_params(("arbitrary",)),
    )(a)


def _adamw(w, g, m, v, name):
    r, c = w.shape
    tr = _rows_tile(r, c * 4, budget=1024 * 1024)
    c1 = 1.0 - ADAM_B1 ** ADAM_STEP
    c2 = 1.0 - ADAM_B2 ** ADAM_STEP

    def body(w_ref, g_ref, m_ref, v_ref, d_ref, nm_ref, nv_ref):
        gg = g_ref[...]
        nm = ADAM_B1 * m_ref[...] + (1.0 - ADAM_B1) * gg
        nv = ADAM_B2 * v_ref[...] + (1.0 - ADAM_B2) * (gg * gg)
        m_hat = nm / c1
        v_hat = nv / c2
        d_ref[...] = -ADAM_LR * (m_hat / (jnp.sqrt(v_hat) + ADAM_EPS) + ADAM_WD * w_ref[...])
        nm_ref[...] = nm
        nv_ref[...] = nv

    spec = pl.BlockSpec((tr, c), lambda i: (i, 0))
    return _pcall(
        body, name=name, grid=(r // tr,),
        out_shape=[jax.ShapeDtypeStruct((r, c), F32)] * 3,
        in_specs=[spec] * 4, out_specs=[spec] * 3,
        compiler_params=_params(("arbitrary",)),
    )(w, g, m, v)


def _ada_fwd(c_all, w_shard, b_shard):
    nb, d = c_all.shape
    cols = w_shard.shape[1]

    def body(c_ref, w_ref, b_ref, sc_ref, mod_ref):
        cv = c_ref[...]
        sc = cv * jax.nn.sigmoid(cv)
        sc_ref[...] = sc
        mod_ref[...] = jnp.dot(sc.astype(BF16), w_ref[...].astype(BF16), preferred_element_type=F32) + b_ref[...]

    return _pcall(
        body, name="ada_fwd",
        out_shape=[jax.ShapeDtypeStruct((nb, d), F32), jax.ShapeDtypeStruct((nb, cols), F32)],
        compiler_params=pltpu.CompilerParams(vmem_limit_bytes=VMEM_LIMIT),
    )(c_all, w_shard, b_shard)


def _ada_bwd(sc_t, dmod_cols):
    d, nb = sc_t.shape
    cols = dmod_cols.shape[1]
    tr = _rows_tile(d, cols * 4, budget=1024 * 1024)

    def body(s_ref, m_ref, o_ref):
        s = s_ref[...]
        m = m_ref[...]
        acc = s[:, 0:1] * m[0:1, :]
        for b in range(1, nb):
            acc = acc + s[:, b:b + 1] * m[b:b + 1, :]
        o_ref[...] = acc

    return _pcall(
        body, name="ada_bwd", grid=(d // tr,),
        out_shape=jax.ShapeDtypeStruct((d, cols), F32),
        in_specs=[pl.BlockSpec((tr, nb), lambda i: (i, 0)), pl.BlockSpec((nb, cols), lambda i: (0, 0))],
        out_specs=pl.BlockSpec((tr, cols), lambda i: (i, 0)),
        compiler_params=_params(("arbitrary",)),
    )(sc_t, dmod_cols)


def _log_sigmoid(x):
    return jnp.minimum(x, 0.0) - jnp.log1p(jnp.exp(-jnp.abs(x)))


def _fgate_fwd(fl2d, b_rows, tri_in, tri_blk):
    r = fl2d.shape[0]

    def body(x_ref, b_ref, u_ref, l_ref, f_ref):
        lf = _log_sigmoid(x_ref[...] + b_ref[...])
        c1 = _split_dot(lf, u_ref[...], 3)
        tot = jnp.broadcast_to(c1[:, LANES - 1:LANES], (r, LANES))
        acc = None
        for _ in range(3):
            part = tot.astype(BF16)
            dd = jnp.dot(l_ref[...], part, preferred_element_type=F32)
            acc = dd if acc is None else acc + dd
            tot = tot - part.astype(F32)
        f_ref[...] = c1 + acc

    return _pcall(
        body, name="fgate_fwd", out_shape=jax.ShapeDtypeStruct((r, LANES), F32),
        compiler_params=pltpu.CompilerParams(vmem_limit_bytes=VMEM_LIMIT),
    )(fl2d, b_rows, tri_in, tri_blk)


def _fgate_bwd(fl2d, b_rows, df_query, df_key, tri_in_rev, tri_blk_rev, head_rows):
    r = fl2d.shape[0]
    nhp = head_rows.shape[0]

    def body(x_ref, b_ref, dq_ref, dk_ref, u_ref, l_ref, hr_ref, o_ref, gb_ref):
        c1 = _split_dot(dq_ref[...] + dk_ref[...], u_ref[...], 3)
        tot = jnp.broadcast_to(c1[:, 0:1], (r, LANES))
        acc = None
        for _ in range(3):
            part = tot.astype(BF16)
            dd = jnp.dot(l_ref[...], part, preferred_element_type=F32)
            acc = dd if acc is None else acc + dd
            tot = tot - part.astype(F32)
        x = x_ref[...] + b_ref[...]
        e = jnp.exp(-jnp.abs(x))
        dfl = (c1 + acc) * (jnp.where(x >= 0, e, 1.0) / (1.0 + e))
        o_ref[...] = dfl
        rs = jnp.broadcast_to(jnp.sum(dfl, axis=1, keepdims=True), (r, LANES))
        gb = None
        for _ in range(3):
            part = rs.astype(BF16)
            dd = jnp.dot(hr_ref[...], part, preferred_element_type=F32)
            gb = dd if gb is None else gb + dd
            rs = rs - part.astype(F32)
        gb_ref[...] = gb

    return _pcall(
        body, name="fgate_bwd",
        out_shape=[jax.ShapeDtypeStruct((r, LANES), F32), jax.ShapeDtypeStruct((nhp, LANES), F32)],
        compiler_params=pltpu.CompilerParams(vmem_limit_bytes=VMEM_LIMIT),
    )(fl2d, b_rows, df_query, df_key, tri_in_rev, tri_blk_rev, head_rows)


def _norm_mod(x, g, scale, shift):
    r = lax.rsqrt(jnp.mean(x * x, axis=-1, keepdims=True) + EPS)
    return (x * r * g) * (1.0 + scale) + shift


def _norm_mod_bwd(x, dh, g, scale):
    r = lax.rsqrt(jnp.mean(x * x, axis=-1, keepdims=True) + EPS)
    xn = x * r
    dshift = jnp.sum(dh, axis=0, keepdims=True)
    dscale = jnp.sum(dh * (xn * g), axis=0, keepdims=True)
    dxn_g = dh * (1.0 + scale)
    dg = jnp.sum(dxn_g * xn, axis=0, keepdims=True)
    dxn = dxn_g * g
    dx = r * (dxn - xn * jnp.mean(dxn * xn, axis=-1, keepdims=True))
    return dx, dshift, dscale, dg


def _in_proj_fwd(x, mod8, g_attn, w_qkv, w_f, tm):
    t, d = x.shape
    dg = w_qkv.shape[1] // 6

    def body(x_ref, mod_ref, g_ref, w_ref, wf_ref, qkv_ref, fl_ref, h1_ref, h_sc):
        j = pl.program_id(1)

        @pl.when(j == 0)
        def _():
            h = _norm_mod(x_ref[...], g_ref[...], mod_ref[1:2, :], mod_ref[0:1, :]).astype(BF16)
            h_sc[...] = h
            h1_ref[...] = h
            fl_ref[...] = jnp.dot(h, wf_ref[...], preferred_element_type=F32)

        s = jnp.where((j == 0) | (j == 3), HEAD_DIM ** -0.5, 1.0)
        qkv_ref[...] = (jnp.dot(h_sc[...], w_ref[...], preferred_element_type=F32) * s).astype(BF16)

    return _pcall(
        body, name="in_proj_fwd", grid=(t // tm, 6),
        out_shape=[jax.ShapeDtypeStruct((t, 6 * dg), BF16), jax.ShapeDtypeStruct((t, LANES), F32),
                   jax.ShapeDtypeStruct((t, d), BF16)],
        in_specs=[pl.BlockSpec((tm, d), lambda i, j: (i, 0)), pl.BlockSpec((8, d), lambda i, j: (0, 0)),
                  pl.BlockSpec((1, d), lambda i, j: (0, 0)), pl.BlockSpec((d, dg), lambda i, j: (0, j)),
                  pl.BlockSpec((d, LANES), lambda i, j: (0, 0))],
        out_specs=[pl.BlockSpec((tm, dg), lambda i, j: (i, j)), pl.BlockSpec((tm, LANES), lambda i, j: (i, 0)),
                   pl.BlockSpec((tm, d), lambda i, j: (i, 0))],
        scratch_shapes=[pltpu.VMEM((tm, d), BF16)],
        compiler_params=_params(("arbitrary", "arbitrary")),
    )(x, mod8, g_attn, w_qkv, w_f)


def _head_rstd(o, bd):
    return lax.rsqrt(_split_dot(o * o, bd, 3) * (1.0 / HEAD_DIM) + EPS)


def _attn_out_fwd(x, o_fox, o_sb, g_fox, g_sb, w_out, mod8, bd, tm):
    t, d = x.shape
    dg = o_fox.shape[0]

    def body(x_ref, of_ref, os_ref, gf_ref, gs_ref, w_ref, mod_ref, bd_ref, x2_ref, mix_ref):
        of, osb = of_ref[...].T, os_ref[...].T
        mf = (of * _head_rstd(of, bd_ref[...]) * gf_ref[...]).astype(BF16)
        ms = (osb * _head_rstd(osb, bd_ref[...]) * gs_ref[...]).astype(BF16)
        mix_ref[:, :dg] = mf
        mix_ref[:, dg:] = ms
        y = jnp.dot(mf, w_ref[:dg, :], preferred_element_type=F32) + jnp.dot(ms, w_ref[dg:, :], preferred_element_type=F32)
        x2_ref[...] = x_ref[...] + mod_ref[2:3, :] * y

    row = lambda w: pl.BlockSpec((tm, w), lambda i: (i, 0))
    full = lambda a: pl.BlockSpec(a.shape, lambda i: (0,) * a.ndim)
    return _pcall(
        body, name="attn_out_fwd", grid=(t // tm,),
        out_shape=[jax.ShapeDtypeStruct((t, d), F32), jax.ShapeDtypeStruct((t, 2 * dg), BF16)],
        in_specs=[row(d), pl.BlockSpec((dg, tm), lambda i: (0, i)), pl.BlockSpec((dg, tm), lambda i: (0, i)),
                  full(g_fox), full(g_sb), full(w_out), full(mod8), full(bd)],
        out_specs=[row(d), row(2 * dg)],
        compiler_params=_params(("arbitrary",)),
    )(x, o_fox, o_sb, g_fox, g_sb, w_out, mod8, bd)


def _attn_out_bwd(dx2, mix, o_fox, o_sb, g_fox, g_sb, w_out, mod8, bd, hsel, tm):
    t, d = dx2.shape
    dg = o_fox.shape[0]

    def body(dx_ref, mix_ref, of_ref, os_ref, gf_ref, gs_ref, w_ref, mod_ref, bd_ref, hs_ref,
             dof_ref, dos_ref, dlt_ref, dxg_ref, part_ref):
        dx = dx_ref[...]
        gate = mod_ref[2:3, :]
        dxg = (dx * gate).astype(BF16)
        dxg_ref[...] = dxg
        mixv = mix_ref[...]
        y = jnp.dot(mixv[:, :dg], w_ref[:dg, :], preferred_element_type=F32)
        y = y + jnp.dot(mixv[:, dg:], w_ref[dg:, :], preferred_element_type=F32)
        part_ref[0] = jnp.zeros((8, d), F32)
        part_ref[0, 0:1, :] = jnp.sum(dx * y, axis=0, keepdims=True)
        for grp, (o_ref, g_ref, do_ref) in enumerate(((of_ref, gf_ref, dof_ref), (os_ref, gs_ref, dos_ref))):
            dmix = lax.dot_general(dxg, w_ref[grp * dg:(grp + 1) * dg, :], NT_DIMS, preferred_element_type=F32)
            o = o_ref[...].T
            r = _head_rstd(o, bd_ref[...])
            n = o * r
            part_ref[0, 1:2, grp * dg:(grp + 1) * dg] = jnp.sum(dmix * n, axis=0, keepdims=True)
            dn = dmix * g_ref[...]
            mh = _split_dot(dn * n, bd_ref[...], 3) * (1.0 / HEAD_DIM)
            do = r * (dn - n * mh)
            do_ref[...] = do.astype(BF16)
            if grp == 0:
                dlt_ref[...] = _split_dot(do * o, hs_ref[...], 3)

    row = lambda w: pl.BlockSpec((tm, w), lambda i: (i, 0))
    full = lambda a: pl.BlockSpec(a.shape, lambda i: (0,) * a.ndim)
    nt = t // tm
    return _pcall(
        body, name="attn_out_bwd", grid=(nt,),
        out_shape=[jax.ShapeDtypeStruct((t, dg), BF16), jax.ShapeDtypeStruct((t, dg), BF16),
                   jax.ShapeDtypeStruct((t, LANES), F32), jax.ShapeDtypeStruct((t, d), BF16),
                   jax.ShapeDtypeStruct((nt, 8, d), F32)],
        in_specs=[row(d), row(2 * dg), pl.BlockSpec((dg, tm), lambda i: (0, i)), pl.BlockSpec((dg, tm), lambda i: (0, i)),
                  full(g_fox), full(g_sb), full(w_out), full(mod8),
                  full(bd), full(hsel)],
        out_specs=[row(dg), row(dg), row(LANES), row(d), pl.BlockSpec((1, 8, d), lambda i: (i, 0, 0))],
        compiler_params=_params(("arbitrary",)),
    )(dx2, mix, o_fox, o_sb, g_fox, g_sb, w_out, mod8, bd, hsel)


def _in_proj_bwd(dparts, dfl, w_qkv, w_f, x, dx2, mod8, g_attn, tm):
    t, d = x.shape
    dg = dparts[1].shape[1]

    def body(*refs):
        d_refs = refs[:6]
        dfl_ref, w_ref, wf_ref, x_ref, dx2_ref, mod_ref, g_ref, gx_ref, dp_ref, dflb_ref, part_ref = refs[6:]
        dh = None
        for k in range(6):
            dk = d_refs[k][...].T if k in (0, 3) else d_refs[k][...]
            if k in (0, 3):
                dk = dk * HEAD_DIM ** -0.5
            db = dk.astype(BF16)
            dp_ref[:, k * dg:(k + 1) * dg] = db
            term = lax.dot_general(db, w_ref[:, k * dg:(k + 1) * dg], NT_DIMS, preferred_element_type=F32)
            dh = term if dh is None else dh + term
        dfb = dfl_ref[...].astype(BF16)
        dflb_ref[...] = dfb
        dh = dh + lax.dot_general(dfb, wf_ref[...], NT_DIMS, preferred_element_type=F32)
        dx, dshift, dscale, dgn = _norm_mod_bwd(x_ref[...], dh, g_ref[...], mod_ref[1:2, :])
        gx_ref[...] = dx2_ref[...] + dx
        part_ref[0] = jnp.zeros((8, d), F32)
        part_ref[0, 0:1, :] = dshift
        part_ref[0, 1:2, :] = dscale
        part_ref[0, 2:3, :] = dgn

    row = lambda w: pl.BlockSpec((tm, w), lambda i: (i, 0))
    full = lambda a: pl.BlockSpec(a.shape, lambda i: (0,) * a.ndim)
    nt = t // tm
    return _pcall(
        body, name="in_proj_bwd", grid=(nt,),
        out_shape=[jax.ShapeDtypeStruct((t, d), F32), jax.ShapeDtypeStruct((t, 6 * dg), BF16),
                   jax.ShapeDtypeStruct((t, LANES), BF16), jax.ShapeDtypeStruct((nt, 8, d), F32)],
        in_specs=[pl.BlockSpec((dg, tm), lambda i: (0, i)), row(dg), row(dg)] * 2
        + [row(LANES), full(w_qkv), full(w_f), row(d), row(d), full(mod8), full(g_attn)],
        out_specs=[row(d), row(6 * dg), row(LANES), pl.BlockSpec((1, 8, d), lambda i: (i, 0, 0))],
        compiler_params=_params(("arbitrary",)),
    )(*dparts, dfl, w_qkv, w_f, x, dx2, mod8, g_attn)


def _matmul_tn(a, b, name):
    t, m = a.shape
    n = b.shape[1]
    a_t = a.T
    tm_ = _tile(m, (512, 256, 128))
    tn_ = _tile(n, (1024, 512, 256, 128))
    tk = _tile(t, (2048, 1024, 512, 256, 128))
    nk = t // tk

    def body(a_ref, b_ref, o_ref):
        k = pl.program_id(2)

        @pl.when(k == 0)
        def _():
            o_ref[...] = jnp.zeros_like(o_ref)

        o_ref[...] += jnp.dot(a_ref[...], b_ref[...], preferred_element_type=F32)

    return _pcall(
        body, name=name, grid=(m // tm_, n // tn_, nk),
        out_shape=jax.ShapeDtypeStruct((m, n), F32),
        in_specs=[pl.BlockSpec((tm_, tk), lambda i, j, k: (i, k)), pl.BlockSpec((tk, tn_), lambda i, j, k: (k, j))],
        out_specs=pl.BlockSpec((tm_, tn_), lambda i, j, k: (i, j)),
        compiler_params=_params(("arbitrary", "arbitrary", "arbitrary")),
    )(a_t, b)


HALO = 16


def _conv_taps(up_ext, cw, lo, rows):
    s1 = pltpu.roll(up_ext, 1, 0)
    s2 = pltpu.roll(up_ext, 2, 0)
    u = cw[2:3, :] * up_ext[lo:lo + rows] + cw[1:2, :] * s1[lo:lo + rows] + cw[0:1, :] * s2[lo:lo + rows] + cw[3:4, :]
    return u, s1, s2


def _ffn_fwd(x2, target, mod8, g_mlp, g_final, wg, wv, cwg, cwv, wd, tm, cf):
    t, d = x2.shape
    dfp = wg.shape[1]
    nt, nc = t // tm, dfp // cf
    hb = tm // HALO

    def body(x_ref, xp_ref, tg_ref, mod_ref, g_ref, gf_ref, wg_ref, wv_ref, cg_ref, cv_ref, wd_ref,
             dx3_ref, h2_ref, part_ref, h_sc, acc_sc):
        i, j = pl.program_id(0), pl.program_id(1)

        @pl.when(j == 0)
        def _():
            xe = jnp.concatenate([xp_ref[...], x_ref[...]], axis=0)
            h = _norm_mod(xe, g_ref[...], mod_ref[4:5, :], mod_ref[3:4, :]).astype(BF16)
            h_sc[...] = h
            h2_ref[...] = h[HALO:]
            acc_sc[...] = jnp.zeros_like(acc_sc)

        rowi = lax.broadcasted_iota(jnp.int32, (tm + HALO, 1), 0)
        keep = (rowi >= HALO) | (i > 0)
        hv = h_sc[...]
        upg = jnp.where(keep, jnp.dot(hv, wg_ref[...], preferred_element_type=F32), 0.0)
        upv = jnp.where(keep, jnp.dot(hv, wv_ref[...], preferred_element_type=F32), 0.0)
        ug, _, _ = _conv_taps(upg, cg_ref[...], HALO, tm)
        uv, _, _ = _conv_taps(upv, cv_ref[...], HALO, tm)
        act = (ug * jax.nn.sigmoid(ug) * uv).astype(BF16)
        acc_sc[...] += jnp.dot(act, wd_ref[...], preferred_element_type=F32)

        @pl.when(j == nc - 1)
        def _():
            y_ffn = acc_sc[...]
            x3 = x_ref[...] + mod_ref[5:6, :] * y_ffn
            r3 = lax.rsqrt(jnp.mean(x3 * x3, axis=-1, keepdims=True) + EPS)
            xn = x3 * r3
            gf = gf_ref[...]
            diff = xn * gf - tg_ref[...]
            dy = diff * (1.0 / d)
            dxn = dy * gf
            dx3 = r3 * (dxn - xn * jnp.mean(dxn * xn, axis=-1, keepdims=True))
            dx3_ref[...] = dx3
            part_ref[0] = jnp.zeros((8, d), F32)
            part_ref[0, 0:1, :] = jnp.sum(dy * xn, axis=0, keepdims=True)
            part_ref[0, 1:2, :] = jnp.sum(dx3 * y_ffn, axis=0, keepdims=True)
            part_ref[0, 2:3, :] = jnp.sum(diff * diff, axis=0, keepdims=True) * (0.5 / d)

    row = lambda w: pl.BlockSpec((tm, w), lambda i, j: (i, 0))
    full = lambda a: pl.BlockSpec(a.shape, lambda i, j: (0,) * a.ndim)
    return _pcall(
        body, name="ffn_fwd", grid=(nt, nc),
        out_shape=[jax.ShapeDtypeStruct((t, d), F32), jax.ShapeDtypeStruct((t, d), BF16),
                   jax.ShapeDtypeStruct((nt, 8, d), F32)],
        in_specs=[row(d), pl.BlockSpec((HALO, d), lambda i, j: (jnp.maximum(i * hb - 1, 0), 0)), row(d),
                  full(mod8), full(g_mlp), full(g_final),
                  pl.BlockSpec((d, cf), lambda i, j: (0, j)), pl.BlockSpec((d, cf), lambda i, j: (0, j)),
                  pl.BlockSpec((8, cf), lambda i, j: (0, j)), pl.BlockSpec((8, cf), lambda i, j: (0, j)),
                  pl.BlockSpec((cf, d), lambda i, j: (j, 0))],
        out_specs=[row(d), row(d), pl.BlockSpec((1, 8, d), lambda i, j: (i, 0, 0))],
        scratch_shapes=[pltpu.VMEM((tm + HALO, d), BF16), pltpu.VMEM((tm, d), F32)],
        compiler_params=_params(("arbitrary", "arbitrary")),
    )(x2, x2, target, mod8, g_mlp, g_final, wg, wv, cwg, cwv, wd)


def _chunk_major(w, cf):
    d, n = w.shape[0], w.shape[1] // cf
    return jnp.transpose(w.reshape(d, n, cf), (1, 0, 2))


def _ffn_fwd(x2, target, mod8, g_mlp, g_final, wg, wv, cwg, cwv, wd, tm, cf):
    t, d = x2.shape
    dfp = wg.shape[1]
    nt, nc = t // tm, dfp // cf
    hb = tm // HALO
    wg_c, wv_c = _chunk_major(wg, cf), _chunk_major(wv, cf)

    def body(x_ref, xp_ref, tg_ref, mod_ref, g_ref, gf_ref, wg_ref, wv_ref, cg_ref, cv_ref, wd_ref,
             dx3_ref, h2_ref, part_ref, act_sc):
        i = pl.program_id(0)
        xe = jnp.concatenate([xp_ref[...], x_ref[...]], axis=0)
        h = _norm_mod(xe, g_ref[...], mod_ref[4:5, :], mod_ref[3:4, :]).astype(BF16)
        h2_ref[...] = h[HALO:]
        first = jnp.where(i > 0, h[:HALO], jnp.zeros_like(h[:HALO]))
        h = jnp.concatenate([first, h[HALO:]], axis=0)

        def up(c):
            return (jnp.dot(h, wg_ref[c], preferred_element_type=F32), jnp.dot(h, wv_ref[c], preferred_element_type=F32))

        def activation(c, ups):
            cols = slice(c * cf, (c + 1) * cf)
            ug, _, _ = _conv_taps(ups[0], cg_ref[:, cols], HALO, tm)
            uv, _, _ = _conv_taps(ups[1], cv_ref[:, cols], HALO, tm)
            act_sc[:, cols] = (ug * jax.nn.sigmoid(ug) * uv).astype(BF16)

        for c0 in range(0, nc, 2):
            group = list(range(c0, min(c0 + 2, nc)))
            ups = [up(c) for c in group]
            for c, u in zip(group, ups):
                activation(c, u)

        y_ffn = jnp.dot(act_sc[...], wd_ref[...], preferred_element_type=F32)
        x3 = x_ref[...] + mod_ref[5:6, :] * y_ffn
        r3 = lax.rsqrt(jnp.mean(x3 * x3, axis=-1, keepdims=True) + EPS)
        xn = x3 * r3
        gf = gf_ref[...]
        diff = xn * gf - tg_ref[...]
        dy = diff * (1.0 / d)
        dxn = dy * gf
        dx3 = r3 * (dxn - xn * jnp.mean(dxn * xn, axis=-1, keepdims=True))
        dx3_ref[...] = dx3
        part_ref[0] = jnp.zeros((8, d), F32)
        part_ref[0, 0:1, :] = jnp.sum(dy * xn, axis=0, keepdims=True)
        part_ref[0, 1:2, :] = jnp.sum(dx3 * y_ffn, axis=0, keepdims=True)
        part_ref[0, 2:3, :] = jnp.sum(diff * diff, axis=0, keepdims=True) * (0.5 / d)

    row = lambda w: pl.BlockSpec((tm, w), lambda i: (i, 0))
    full = lambda a: pl.BlockSpec(a.shape, lambda i: (0,) * a.ndim)
    once = lambda a: pl.BlockSpec(a.shape, lambda i: (0,) * a.ndim, pipeline_mode=pl.Buffered(1))
    return _pcall(
        body, name="ffn_fwd", grid=(nt,),
        out_shape=[jax.ShapeDtypeStruct((t, d), F32), jax.ShapeDtypeStruct((t, d), BF16),
                   jax.ShapeDtypeStruct((nt, 8, d), F32)],
        in_specs=[row(d), pl.BlockSpec((HALO, d), lambda i: (jnp.maximum(i * hb - 1, 0), 0)), row(d),
                  full(mod8), full(g_mlp), full(g_final), once(wg_c), once(wv_c), once(cwg), once(cwv), once(wd)],
        out_specs=[row(d), row(d), pl.BlockSpec((1, 8, d), lambda i: (i, 0, 0))],
        scratch_shapes=[pltpu.VMEM((tm, dfp), BF16)],
        compiler_params=_params(("arbitrary",)),
    )(x2, x2, target, mod8, g_mlp, g_final, wg_c, wv_c, cwg, cwv, wd)


def _ffn_bwd(x2, dx3, mod8, g_mlp, wg, wv, cwg, cwv, wd, tm, cf):
    t, d = x2.shape
    dfp = wg.shape[1]
    nt, nc = t // tm, dfp // cf
    hb = tm // HALO
    nhb = t // HALO
    ext = tm + 2 * HALO

    def body(x_ref, xp_ref, xn_ref, dx_ref, dxn_ref, mod_ref, g_ref, wg_ref, wv_ref, cg_ref, cv_ref, wd_ref,
             dx2_ref, dug_ref, duv_ref, act_ref, dxg_ref, part_ref, pcg_ref, pcv_ref, h_sc, dg_sc, dh_sc):
        i, j = pl.program_id(0), pl.program_id(1)

        @pl.when(j == 0)
        def _():
            xe = jnp.concatenate([xp_ref[...], x_ref[...], xn_ref[...]], axis=0)
            h_sc[...] = _norm_mod(xe, g_ref[...], mod_ref[4:5, :], mod_ref[3:4, :]).astype(BF16)
            de = (jnp.concatenate([dx_ref[...], dxn_ref[...]], axis=0) * mod_ref[5:6, :]).astype(BF16)
            dg_sc[...] = de
            dxg_ref[...] = de[:tm]
            dh_sc[...] = jnp.zeros_like(dh_sc)

        rowe = lax.broadcasted_iota(jnp.int32, (ext, 1), 0)
        keep_up = (rowe >= HALO) | (i > 0)
        rowu = lax.broadcasted_iota(jnp.int32, (tm + HALO, 1), 0)
        keep_du = (rowu < tm) | (i < nt - 1)
        hv = h_sc[...]
        upg = jnp.where(keep_up, jnp.dot(hv, wg_ref[...], preferred_element_type=F32), 0.0)
        upv = jnp.where(keep_up, jnp.dot(hv, wv_ref[...], preferred_element_type=F32), 0.0)
        cg, cv = cg_ref[...], cv_ref[...]
        ug, g1, g2 = _conv_taps(upg, cg, HALO, tm + HALO)
        uv, v1, v2 = _conv_taps(upv, cv, HALO, tm + HALO)
        dact = lax.dot_general(dg_sc[...], wd_ref[...], NT_DIMS, preferred_element_type=F32)
        sg = jax.nn.sigmoid(ug)
        sil = ug * sg
        act_ref[...] = (sil * uv)[:tm].astype(BF16)
        duv = jnp.where(keep_du, dact * sil, 0.0)
        dug = jnp.where(keep_du, dact * uv * (sg * (1.0 + ug * (1.0 - sg))), 0.0)

        def back(du, cw, up, s1, s2, pc_ref):
            n = tm + HALO
            dup = (cw[2:3, :] * du + cw[1:2, :] * pltpu.roll(du, n - 1, 0) + cw[0:1, :] * pltpu.roll(du, n - 2, 0))[:tm]
            dut = du[:tm]
            pc_ref[0] = jnp.zeros((8, cf), F32)
            pc_ref[0, 0:1, :] = jnp.sum(dut * s2[HALO:HALO + tm], axis=0, keepdims=True)
            pc_ref[0, 1:2, :] = jnp.sum(dut * s1[HALO:HALO + tm], axis=0, keepdims=True)
            pc_ref[0, 2:3, :] = jnp.sum(dut * up[HALO:HALO + tm], axis=0, keepdims=True)
            pc_ref[0, 3:4, :] = jnp.sum(dut, axis=0, keepdims=True)
            return dup.astype(BF16)

        dupg = back(dug, cg, upg, g1, g2, pcg_ref)
        dupv = back(duv, cv, upv, v1, v2, pcv_ref)
        dug_ref[...] = dupg
        duv_ref[...] = dupv
        dh_sc[...] += (lax.dot_general(dupg, wg_ref[...], NT_DIMS, preferred_element_type=F32)
                       + lax.dot_general(dupv, wv_ref[...], NT_DIMS, preferred_element_type=F32))

        @pl.when(j == nc - 1)
        def _():
            dx, dshift, dscale, dgn = _norm_mod_bwd(x_ref[...], dh_sc[...], g_ref[...], mod_ref[4:5, :])
            dx2_ref[...] = dx_ref[...] + dx
            part_ref[0] = jnp.zeros((8, d), F32)
            part_ref[0, 0:1, :] = dshift
            part_ref[0, 1:2, :] = dscale
            part_ref[0, 2:3, :] = dgn

    row = lambda w: pl.BlockSpec((tm, w), lambda i, j: (i, 0))
    prev = pl.BlockSpec((HALO, d), lambda i, j: (jnp.maximum(i * hb - 1, 0), 0))
    nxt = pl.BlockSpec((HALO, d), lambda i, j: (jnp.minimum((i + 1) * hb, nhb - 1), 0))
    full = lambda a: pl.BlockSpec(a.shape, lambda i, j: (0,) * a.ndim)
    chunk = pl.BlockSpec((tm, cf), lambda i, j: (i, j))
    pchunk = pl.BlockSpec((1, 8, cf), lambda i, j: (i, 0, j))
    return _pcall(
        body, name="ffn_bwd", grid=(nt, nc),
        out_shape=[jax.ShapeDtypeStruct((t, d), F32), jax.ShapeDtypeStruct((t, dfp), BF16),
                   jax.ShapeDtypeStruct((t, dfp), BF16), jax.ShapeDtypeStruct((t, dfp), BF16),
                   jax.ShapeDtypeStruct((t, d), BF16), jax.ShapeDtypeStruct((nt, 8, d), F32),
                   jax.ShapeDtypeStruct((nt, 8, dfp), F32), jax.ShapeDtypeStruct((nt, 8, dfp), F32)],
        in_specs=[row(d), prev, nxt, row(d), nxt, full(mod8), full(g_mlp),
                  pl.BlockSpec((d, cf), lambda i, j: (0, j)), pl.BlockSpec((d, cf), lambda i, j: (0, j)),
                  pl.BlockSpec((8, cf), lambda i, j: (0, j)), pl.BlockSpec((8, cf), lambda i, j: (0, j)),
                  pl.BlockSpec((cf, d), lambda i, j: (j, 0))],
        out_specs=[row(d), chunk, chunk, chunk, row(d), pl.BlockSpec((1, 8, d), lambda i, j: (i, 0, 0)), pchunk, pchunk],
        scratch_shapes=[pltpu.VMEM((ext, d), BF16), pltpu.VMEM((tm + HALO, d), BF16), pltpu.VMEM((tm, d), F32)],
        compiler_params=_params(("arbitrary", "arbitrary")),
    )(x2, x2, x2, dx3, dx3, mod8, g_mlp, wg, wv, cwg, cwv, wd)


def _ffn_bwd(x2, dx3, mod8, g_mlp, wg, wv, cwg, cwv, wd, tm, cf):
    t, d = x2.shape
    dfp = wg.shape[1]
    nt, nc = t // tm, dfp // cf
    hb = tm // HALO
    nhb = t // HALO
    n = tm + HALO

    def body(x_ref, xp_ref, xn_ref, dx_ref, dxn_ref, mod_ref, g_ref, wg_ref, wv_ref, cg_ref, cv_ref, wd_ref,
             dx2_ref, dug_ref, duv_ref, act_ref, dxg_ref, part_ref, pcg_ref, pcv_ref):
        i = pl.program_id(0)
        xe = jnp.concatenate([xp_ref[...], x_ref[...], xn_ref[...]], axis=0)
        h = _norm_mod(xe, g_ref[...], mod_ref[4:5, :], mod_ref[3:4, :]).astype(BF16)
        h = jnp.concatenate([jnp.where(i > 0, h[:HALO], jnp.zeros_like(h[:HALO])), h[HALO:]], axis=0)
        dx = dx_ref[...] * mod_ref[5:6, :]
        dxn = jnp.where(i < nt - 1, dxn_ref[...] * mod_ref[5:6, :], 0.0)
        de = jnp.concatenate([dx, dxn], axis=0).astype(BF16)
        dxg_ref[...] = de[:tm]
        pcg_ref[0] = jnp.zeros((8, dfp), F32)
        pcv_ref[0] = jnp.zeros((8, dfp), F32)

        def products(c):
            cols = slice(c * cf, (c + 1) * cf)
            return (jnp.dot(h, wg_ref[:, cols], preferred_element_type=F32), jnp.dot(h, wv_ref[:, cols], preferred_element_type=F32),
                    lax.dot_general(de, wd_ref[cols, :], NT_DIMS, preferred_element_type=F32))

        def back(du, cw, up, s1, s2, pc_ref, cols):
            dup = (cw[2:3, :] * du + cw[1:2, :] * pltpu.roll(du, n - 1, 0) + cw[0:1, :] * pltpu.roll(du, n - 2, 0))[:tm]
            dut = du[:tm]
            pc_ref[0, 0:1, cols] = jnp.sum(dut * s2[HALO:HALO + tm], axis=0, keepdims=True)
            pc_ref[0, 1:2, cols] = jnp.sum(dut * s1[HALO:HALO + tm], axis=0, keepdims=True)
            pc_ref[0, 2:3, cols] = jnp.sum(dut * up[HALO:HALO + tm], axis=0, keepdims=True)
            pc_ref[0, 3:4, cols] = jnp.sum(dut, axis=0, keepdims=True)
            return dup.astype(BF16)

        def chunk(c, prods):
            cols = slice(c * cf, (c + 1) * cf)
            upg, upv, dact = prods
            cg, cv = cg_ref[:, cols], cv_ref[:, cols]
            ug, g1, g2 = _conv_taps(upg, cg, HALO, n)
            uv, v1, v2 = _conv_taps(upv, cv, HALO, n)
            sg = jax.nn.sigmoid(ug)
            sil = ug * sg
            act_ref[:, cols] = (sil * uv)[:tm].astype(BF16)
            dug_ref[:, cols] = back(dact * uv * (sg * (1.0 + ug * (1.0 - sg))), cg, upg, g1, g2, pcg_ref, cols)
            duv_ref[:, cols] = back(dact * sil, cv, upv, v1, v2, pcv_ref, cols)

        for c0 in range(0, nc, 2):
            group = list(range(c0, min(c0 + 2, nc)))
            prods = [products(c) for c in group]
            for c, pr in zip(group, prods):
                chunk(c, pr)

        dh = (lax.dot_general(dug_ref[...], wg_ref[...], NT_DIMS, preferred_element_type=F32)
              + lax.dot_general(duv_ref[...], wv_ref[...], NT_DIMS, preferred_element_type=F32))
        dxt, dshift, dscale, dgn = _norm_mod_bwd(x_ref[...], dh, g_ref[...], mod_ref[4:5, :])
        dx2_ref[...] = dx_ref[...] + dxt
        part_ref[0] = jnp.zeros((8, d), F32)
        part_ref[0, 0:1, :] = dshift
        part_ref[0, 1:2, :] = dscale
        part_ref[0, 2:3, :] = dgn

    row = lambda w: pl.BlockSpec((tm, w), lambda i: (i, 0))
    prev = pl.BlockSpec((HALO, d), lambda i: (jnp.maximum(i * hb - 1, 0), 0))
    nxt = pl.BlockSpec((HALO, d), lambda i: (jnp.minimum((i + 1) * hb, nhb - 1), 0))
    full = lambda a: pl.BlockSpec(a.shape, lambda i: (0,) * a.ndim)
    once = lambda a: pl.BlockSpec(a.shape, lambda i: (0,) * a.ndim, pipeline_mode=pl.Buffered(1))
    part = lambda w: pl.BlockSpec((1, 8, w), lambda i: (i, 0, 0))
    return _pcall(
        body, name="ffn_bwd", grid=(nt,),
        out_shape=[jax.ShapeDtypeStruct((t, d), F32), jax.ShapeDtypeStruct((t, dfp), BF16),
                   jax.ShapeDtypeStruct((t, dfp), BF16), jax.ShapeDtypeStruct((t, dfp), BF16),
                   jax.ShapeDtypeStruct((t, d), BF16), jax.ShapeDtypeStruct((nt, 8, d), F32),
                   jax.ShapeDtypeStruct((nt, 8, dfp), F32), jax.ShapeDtypeStruct((nt, 8, dfp), F32)],
        in_specs=[row(d), prev, nxt, row(d), nxt, full(mod8), full(g_mlp), once(wg), once(wv), once(cwg), once(cwv), once(wd)],
        out_specs=[row(d), row(dfp), row(dfp), row(dfp), row(d), part(d), part(dfp), part(dfp)],
        compiler_params=_params(("arbitrary",)),
    )(x2, x2, x2, dx3, dx3, mod8, g_mlp, wg, wv, cwg, cwv, wd)


def _head_masks():
    lane = lax.broadcasted_iota(jnp.int32, (1, LANES), 1)
    in_a = lane < HEAD_DIM
    return in_a, jnp.logical_not(in_a)


BLK = 2 * LANES


def _stack_heads(qkv, dg):
    t = qkv.shape[0]
    p = dg // LANES
    rows = _tile(t, (512, 256, 128))
    sub = rows // BLK

    def body(kf_ref, vf_ref, ks_ref, vs_ref, okf, ovf, oks, ovs):
        in_a, in_b = _head_masks()
        for src, dst in ((kf_ref, okf), (vf_ref, ovf), (ks_ref, oks), (vs_ref, ovs)):
            v = src[...]
            zero = jnp.zeros_like(v)
            va, vb = jnp.where(in_a, v, zero), jnp.where(in_b, v, zero)
            for s in range(sub):
                dst[0, s, :BLK, :] = va[s * BLK:(s + 1) * BLK]
                dst[0, s, BLK:, :] = vb[s * BLK:(s + 1) * BLK]

    col = lambda base: pl.BlockSpec((rows, LANES), lambda h, j: (j, base * p + h))
    out = pl.BlockSpec((1, sub, 2 * BLK, LANES), lambda h, j: (h, j, 0, 0))
    shape = jax.ShapeDtypeStruct((p, t // BLK, 2 * BLK, LANES), BF16)
    return _pcall(
        body, name="stack_heads", grid=(p, t // rows),
        out_shape=[shape] * 4, in_specs=[col(1), col(2), col(4), col(5)], out_specs=[out] * 4,
        compiler_params=_params(("arbitrary", "arbitrary")),
    )(qkv, qkv, qkv, qkv)


def _tile_masks():
    rowi = lax.broadcasted_iota(jnp.int32, (BLK, BLK), 0)
    coli = lax.broadcasted_iota(jnp.int32, (BLK, BLK), 1)
    return coli <= rowi, coli < rowi


def _pair_triangle(suffix):
    r = lax.broadcasted_iota(jnp.int32, (BLK, BLK), 0)
    c = lax.broadcasted_iota(jnp.int32, (BLK, BLK), 1)
    return ((r >= c) if suffix else (r <= c)).astype(BF16)


def _pair_cumsum(x2, tri, passes):
    return jnp.concatenate([_split_dot(x2[:, :BLK], tri, passes), _split_dot(x2[:, BLK:], tri, passes)], axis=1)


def _pair_specs(t, dg, base):
    p = dg // LANES
    q = pl.BlockSpec((BLK, LANES), lambda h, i: (i, base * p + h))
    kv = pl.BlockSpec((1, t // BLK, 2 * BLK, LANES), lambda h, i: (h, 0, 0, 0))
    return q, kv


def _fox_fwd(qkv, kst, vst, fcol, frow2, dg):
    t = qkv.shape[0]
    p, nq = dg // LANES, t // BLK
    nh = 2 * p

    def body(q_ref, k_ref, v_ref, ft_ref, fs_ref, o_ref, lse_ref):
        i = pl.program_id(1)
        in_a, _ = _head_masks()
        causal, _ = _tile_masks()
        q2 = q_ref[...]
        ft = tuple(jnp.broadcast_to(ft_ref[a], (BLK, BLK)) for a in range(2))

        def tile(j, carry, masked):
            m, l, acc = carry
            kb, vb = k_ref[0, j], v_ref[0, j]
            s2 = lax.dot_general(q2, kb, NT_DIMS, preferred_element_type=F32)
            fs = fs_ref[0, j]
            m_new, l_new, alpha, pr = [], [], [], []
            for a in range(2):
                sl = slice(a * BLK, (a + 1) * BLK)
                s = (s2[:, sl] + ft[a]) - fs[:, sl]
                if masked:
                    s = jnp.where(causal, s, NEG)
                mn = jnp.maximum(m[a], jnp.max(s, axis=1, keepdims=True))
                pa = jnp.exp(s - mn)
                al = jnp.exp(m[a] - mn)
                m_new.append(mn)
                alpha.append(al)
                l_new.append(al * l[a] + jnp.sum(pa, axis=1, keepdims=True))
                pr.append(pa.astype(BF16))
            acc = jnp.where(in_a, alpha[0], alpha[1]) * acc + jnp.dot(
                jnp.concatenate(pr, axis=1), vb, preferred_element_type=F32)
            return tuple(m_new), tuple(l_new), acc

        neg, zero = jnp.full((BLK, 1), NEG, F32), jnp.zeros((BLK, 1), F32)
        carry = lax.fori_loop(0, i, functools.partial(tile, masked=False), ((neg, neg), (zero, zero), jnp.zeros((BLK, LANES), F32)))
        m, l, acc = tile(i, carry, True)
        o_ref[...] = acc / jnp.where(in_a, l[0], l[1])
        lse_ref[0] = m[0] + jnp.log(l[0])
        lse_ref[1] = m[1] + jnp.log(l[1])

    qs, kv = _pair_specs(t, dg, 0)
    col = pl.BlockSpec((2, BLK, 1), lambda h, i: (h, i, 0))
    return _pcall(
        body, name="fox_fwd", grid=(p, nq),
        out_shape=[jax.ShapeDtypeStruct((t, dg), F32), jax.ShapeDtypeStruct((nh, t, 1), F32)],
        in_specs=[qs, kv, kv, col, pl.BlockSpec((1, nq, 1, 2 * BLK), lambda h, i: (h, 0, 0, 0))],
        out_specs=[pl.BlockSpec((BLK, LANES), lambda h, i: (i, h)), col],
        compiler_params=_params(("arbitrary", "arbitrary")),
    )(qkv, kst, vst, fcol, frow2)


def _fold_heads(stacked, in_a):
    return jnp.where(in_a, stacked[:BLK], stacked[BLK:])


def _fox_bwd(qkv, kst, vst, do, fcol, frow2, lse, delta, dg):
    t = qkv.shape[0]
    p, nq = dg // LANES, t // BLK
    nh = 2 * p

    def body(q_ref, k_ref, v_ref, do_ref, ft_ref, fs_ref, lse_ref, dl_ref, dq_ref, dk_ref, dv_ref, dfs_ref, dft_ref):
        i = pl.program_id(1)

        @pl.when(i == 0)
        def _():
            dk_ref[...] = jnp.zeros_like(dk_ref)
            dv_ref[...] = jnp.zeros_like(dv_ref)
            dfs_ref[...] = jnp.zeros_like(dfs_ref)

        in_a, _ = _head_masks()
        causal, _ = _tile_masks()
        q2, do2 = q_ref[...], do_ref[...]
        ft = tuple(jnp.broadcast_to(ft_ref[a] - lse_ref[a], (BLK, BLK)) for a in range(2))
        dl = tuple(jnp.broadcast_to(dl_ref[a], (BLK, BLK)) for a in range(2))

        def tile(j, carry, masked):
            dq, dft = carry
            kb, vb = k_ref[0, j], v_ref[0, j]
            s2 = lax.dot_general(q2, kb, NT_DIMS, preferred_element_type=F32)
            dp2 = lax.dot_general(do2, vb, NT_DIMS, preferred_element_type=F32)
            fs = fs_ref[0, j]
            pr, ds, dft_new = [], [], []
            for a in range(2):
                sl = slice(a * BLK, (a + 1) * BLK)
                s = (s2[:, sl] + ft[a]) - fs[:, sl]
                if masked:
                    s = jnp.where(causal, s, NEG)
                pa = jnp.exp(s)
                dsa = pa * (dp2[:, sl] - dl[a])
                pr.append(pa.astype(BF16))
                ds.append(dsa)
                dft_new.append(dft[a] + jnp.sum(dsa, axis=1, keepdims=True))
            ds2 = jnp.concatenate(ds, axis=1)
            dsb = ds2.astype(BF16)
            off = pl.multiple_of(j * BLK, BLK)
            dk_ref[pl.ds(off, BLK), :] += _fold_heads(lax.dot_general(dsb, q2, TN_DIMS, preferred_element_type=F32), in_a)
            dv_ref[pl.ds(off, BLK), :] += _fold_heads(
                lax.dot_general(jnp.concatenate(pr, axis=1), do2, TN_DIMS, preferred_element_type=F32), in_a)
            dfs_ref[0, j] += -jnp.sum(ds2, axis=0, keepdims=True)
            return dq + jnp.dot(dsb, kb, preferred_element_type=F32), tuple(dft_new)

        zero = jnp.zeros((BLK, 1), F32)
        carry = lax.fori_loop(0, i, functools.partial(tile, masked=False), (jnp.zeros((BLK, LANES), F32), (zero, zero)))
        dq, dft = tile(i, carry, True)
        dq_ref[...] = dq
        dft_ref[0] = dft[0]
        dft_ref[1] = dft[1]

    qs, kv = _pair_specs(t, dg, 0)
    col = pl.BlockSpec((2, BLK, 1), lambda h, i: (h, i, 0))
    rowspec = pl.BlockSpec((1, nq, 1, 2 * BLK), lambda h, i: (h, 0, 0, 0))
    blk = pl.BlockSpec((BLK, LANES), lambda h, i: (i, h))
    acc = pl.BlockSpec((t, LANES), lambda h, i: (0, h))
    return _pcall(
        body, name="fox_bwd", grid=(p, nq),
        out_shape=[jax.ShapeDtypeStruct((t, dg), F32)] * 3 + [jax.ShapeDtypeStruct((p, nq, 1, 2 * BLK), F32),
                                                              jax.ShapeDtypeStruct((nh, t, 1), F32)],
        in_specs=[qs, kv, kv, blk, col, rowspec, col, col],
        out_specs=[blk, acc, acc, rowspec, col],
        compiler_params=_params(("arbitrary", "arbitrary")),
    )(qkv, kst, vst, do, fcol, frow2, lse, delta)


def _softplus_parts(z):
    e = jnp.exp(-jnp.abs(z))
    return jnp.maximum(z, 0.0) + jnp.log(1.0 + e), e


def _sigmoid_from(z, e):
    d = 1.0 + e
    r = pl.reciprocal(d, approx=True)
    r = r * (2.0 - d * r)
    return jnp.where(z >= 0, 1.0, e) * r


def _sb_fwd(qkv, kst, vst, dg):
    t = qkv.shape[0]
    p, nq = dg // LANES, t // BLK
    nh = 2 * p

    def body(q_ref, k_ref, v_ref, o_ref, rt_ref):
        i = pl.program_id(1)
        _, strict = _tile_masks()
        strict2 = jnp.concatenate([strict, strict], axis=1)
        suffix = _pair_triangle(True)
        q2 = q_ref[...]

        def tile(j, carry, masked):
            rest, acc = carry
            kb, vb = k_ref[0, j], v_ref[0, j]
            z = lax.dot_general(q2, kb, NT_DIMS, preferred_element_type=F32)
            sp, _ = _softplus_parts(z)
            if masked:
                sp = jnp.where(strict2, sp, 0.0)
            cs = _pair_cumsum(sp, suffix, 2)
            w, rest_new = [], []
            for a in range(2):
                sl = slice(a * BLK, (a + 1) * BLK)
                wa = jnp.exp(z[:, sl] - cs[:, sl] - rest[a])
                if masked:
                    wa = jnp.where(strict, wa, 0.0)
                w.append(wa.astype(BF16))
                rest_new.append(rest[a] + cs[:, a * BLK:a * BLK + 1])
            acc = acc + jnp.dot(jnp.concatenate(w, axis=1), vb, preferred_element_type=F32)
            return tuple(rest_new), acc

        zero = jnp.zeros((BLK, 1), F32)
        carry = tile(i, ((zero, zero), jnp.zeros((BLK, LANES), F32)), True)
        rest, acc = lax.fori_loop(0, i, lambda jj, c: tile(i - 1 - jj, c, False), carry)
        o_ref[...] = acc
        rt_ref[0] = rest[0]
        rt_ref[1] = rest[1]

    qs, kv = _pair_specs(t, dg, 3)
    col = pl.BlockSpec((2, BLK, 1), lambda h, i: (h, i, 0))
    return _pcall(
        body, name="sb_fwd", grid=(p, nq),
        out_shape=[jax.ShapeDtypeStruct((t, dg), F32), jax.ShapeDtypeStruct((nh, t, 1), F32)],
        in_specs=[qs, kv, kv],
        out_specs=[pl.BlockSpec((BLK, LANES), lambda h, i: (i, h)), col],
        compiler_params=_params(("arbitrary", "arbitrary")),
    )(qkv, kst, vst)


def _sb_bwd(qkv, kst, vst, do, rtot, dg):
    t = qkv.shape[0]
    p, nq = dg // LANES, t // BLK

    def body(q_ref, k_ref, v_ref, do_ref, rt_ref, dq_ref, dk_ref, dv_ref):
        i = pl.program_id(1)

        @pl.when(i == 0)
        def _():
            dk_ref[...] = jnp.zeros_like(dk_ref)
            dv_ref[...] = jnp.zeros_like(dv_ref)

        in_a, _ = _head_masks()
        _, strict = _tile_masks()
        strict2 = jnp.concatenate([strict, strict], axis=1)
        prefix = _pair_triangle(False)
        q2, do2 = q_ref[...], do_ref[...]
        rt = (rt_ref[0], rt_ref[1])

        def tile(j, carry, masked):
            before, gbefore, dq = carry
            kb, vb = k_ref[0, j], v_ref[0, j]
            z = lax.dot_general(q2, kb, NT_DIMS, preferred_element_type=F32)
            da = lax.dot_general(do2, vb, NT_DIMS, preferred_element_type=F32)
            sp, e = _softplus_parts(z)
            sig = _sigmoid_from(z, e)
            if masked:
                sp = jnp.where(strict2, sp, 0.0)
            pre = _pair_cumsum(sp, prefix, 2)
            w = []
            for a in range(2):
                sl = slice(a * BLK, (a + 1) * BLK)
                wa = jnp.exp(z[:, sl] + (before[a] - rt[a]) + pre[:, sl] - sp[:, sl])
                if masked:
                    wa = jnp.where(strict, wa, 0.0)
                w.append(wa)
            w2 = jnp.concatenate(w, axis=1)
            g = w2 * da
            preg = _pair_cumsum(g, prefix, 1)
            dz = []
            for a in range(2):
                sl = slice(a * BLK, (a + 1) * BLK)
                dza = g[:, sl] * (1.0 - sig[:, sl]) - sig[:, sl] * (gbefore[a] + preg[:, sl] - g[:, sl])
                if masked:
                    dza = jnp.where(strict, dza, 0.0)
                dz.append(dza.astype(BF16))
            dzb = jnp.concatenate(dz, axis=1)
            off = pl.multiple_of(j * BLK, BLK)
            dk_ref[pl.ds(off, BLK), :] += _fold_heads(lax.dot_general(dzb, q2, TN_DIMS, preferred_element_type=F32), in_a)
            dv_ref[pl.ds(off, BLK), :] += _fold_heads(
                lax.dot_general(w2.astype(BF16), do2, TN_DIMS, preferred_element_type=F32), in_a)
            last = lambda x, a: x[:, (a + 1) * BLK - 1:(a + 1) * BLK]
            return (tuple(before[a] + last(pre, a) for a in range(2)),
                    tuple(gbefore[a] + last(preg, a) for a in range(2)),
                    dq + jnp.dot(dzb, kb, preferred_element_type=F32))

        zero = jnp.zeros((BLK, 1), F32)
        carry = lax.fori_loop(0, i, functools.partial(tile, masked=False), ((zero, zero), (zero, zero), jnp.zeros((BLK, LANES), F32)))
        dq_ref[...] = tile(i, carry, True)[2]

    qs, kv = _pair_specs(t, dg, 3)
    col = pl.BlockSpec((2, BLK, 1), lambda h, i: (h, i, 0))
    blk = pl.BlockSpec((BLK, LANES), lambda h, i: (i, h))
    acc = pl.BlockSpec((t, LANES), lambda h, i: (0, h))
    return _pcall(
        body, name="sb_bwd", grid=(p, nq),
        out_shape=[jax.ShapeDtypeStruct((t, dg), F32)] * 3,
        in_specs=[qs, kv, kv, blk, col],
        out_specs=[blk, acc, acc],
        compiler_params=_params(("arbitrary", "arbitrary")),
    )(qkv, kst, vst, do, rtot)


XROWS = 144
LANE_FS, LANE_FT_A, LANE_FT_B = 0, 3, 6


def _pieces3(x):
    hi = x.astype(BF16).astype(F32)
    r = x - hi
    mid = r.astype(BF16).astype(F32)
    return hi, mid, (r - mid).astype(BF16).astype(F32)


def _bias_lanes(rows, entries):
    sub = lax.broadcasted_iota(jnp.int32, (16, 1), 0)
    out = jnp.zeros((16, rows), F32)
    for l, v in entries:
        out = jnp.where(sub == l, v, out)
    return jnp.concatenate([out, jnp.zeros((LANES - 16, rows), F32)], axis=0).T


def _three(first, values):
    return [(first + k, v) for k, v in enumerate(values)]


def _stack_rows(x, in_a, in_b):
    zero = jnp.zeros_like(x)
    return jnp.concatenate([jnp.where(in_a, x, zero), jnp.where(in_b, x, zero)], axis=0)


def _transposed(x):
    return x.astype(F32).T.astype(BF16)


def _attn_operands(qkv, fcol, dg):
    t = qkv.shape[0]
    p, nk = dg // LANES, t // BLK

    def body(qf_ref, kf_ref, vf_ref, ks_ref, vs_ref, f_ref, qx_ref, kx_ref, kxt_ref, vf_o, vft_o, ks_o, kst_o, vs_o, vst_o):
        in_a, in_b = _head_masks()
        fa, fb = _pieces3(f_ref[0]), _pieces3(f_ref[1])
        qx_ref[0, :, :LANES] = qf_ref[...]
        qx_ref[0, :, LANES:] = _bias_lanes(
            BLK, _three(LANE_FS, (-1.0,) * 3) + _three(LANE_FT_A, fa) + _three(LANE_FT_B, fb)).astype(BF16)
        kf = kf_ref[...]
        zero = jnp.zeros_like(kf)
        top = jnp.concatenate([jnp.where(in_a, kf, zero), _bias_lanes(
            BLK, _three(LANE_FS, fa) + _three(LANE_FT_A, (1.0,) * 3)).astype(BF16)], axis=1)
        bot = jnp.concatenate([jnp.where(in_b, kf, zero), _bias_lanes(
            BLK, _three(LANE_FS, fb) + _three(LANE_FT_B, (1.0,) * 3)).astype(BF16)], axis=1)
        kx = jnp.concatenate([top, bot], axis=0)
        kx_ref[0, 0] = kx
        kxt_ref[0, 0] = _transposed(kx)[:XROWS]
        for src, dst, dst_t in ((vf_ref, vf_o, vft_o), (ks_ref, ks_o, kst_o), (vs_ref, vs_o, vst_o)):
            st = _stack_rows(src[...], in_a, in_b)
            dst[0, 0] = st
            dst_t[0, 0] = _transposed(st)

    col = lambda base: pl.BlockSpec((BLK, LANES), lambda h, j: (j, base * p + h))
    blk4 = lambda r, c: pl.BlockSpec((1, 1, r, c), lambda h, j: (h, j, 0, 0))
    shp4 = lambda r, c: jax.ShapeDtypeStruct((p, nk, r, c), BF16)
    return _pcall(
        body, name="attn_operands", grid=(p, nk),
        out_shape=[jax.ShapeDtypeStruct((p, t, 2 * LANES), BF16), shp4(2 * BLK, 2 * LANES), shp4(XROWS, 2 * BLK)]
        + [shp4(2 * BLK, LANES), shp4(LANES, 2 * BLK)] * 3,
        in_specs=[col(0), col(1), col(2), col(4), col(5), pl.BlockSpec((2, 1, BLK), lambda h, j: (h, 0, j))],
        out_specs=[pl.BlockSpec((1, BLK, 2 * LANES), lambda h, j: (h, j, 0)), blk4(2 * BLK, 2 * LANES), blk4(XROWS, 2 * BLK)]
        + [blk4(2 * BLK, LANES), blk4(LANES, 2 * BLK)] * 3,
        compiler_params=_params(("arbitrary", "arbitrary")),
    )(qkv, qkv, qkv, qkv, qkv, fcol)


def _fox_q_bwd(qkv, fcol, lse_col, dg):
    t = qkv.shape[0]
    p = dg // LANES

    def body(q_ref, f_ref, l_ref, qx_ref):
        fa, fb = _pieces3(f_ref[0] - l_ref[0]), _pieces3(f_ref[1] - l_ref[1])
        qx_ref[0, :, :LANES] = q_ref[...]
        qx_ref[0, :, LANES:] = _bias_lanes(
            BLK, _three(LANE_FS, (-1.0,) * 3) + _three(LANE_FT_A, fa) + _three(LANE_FT_B, fb)).astype(BF16)

    row = pl.BlockSpec((2, 1, BLK), lambda h, j: (h, 0, j))
    return _pcall(
        body, name="fox_q_bwd", grid=(p, t // BLK),
        out_shape=jax.ShapeDtypeStruct((p, t, 2 * LANES), BF16),
        in_specs=[pl.BlockSpec((BLK, LANES), lambda h, j: (j, h)), row, row],
        out_specs=pl.BlockSpec((1, BLK, 2 * LANES), lambda h, j: (h, j, 0)),
        compiler_params=_params(("arbitrary", "arbitrary")),
    )(qkv, fcol, lse_col)


def _key_query_masks():
    key = lax.broadcasted_iota(jnp.int32, (BLK, BLK), 0)
    qry = lax.broadcasted_iota(jnp.int32, (BLK, BLK), 1)
    return key <= qry, key < qry


def _key_triangle(kind):
    s = lax.broadcasted_iota(jnp.int32, (BLK, BLK), 0)
    j = lax.broadcasted_iota(jnp.int32, (BLK, BLK), 1)
    return {"suffix": j >= s, "prefix": j <= s, "before": j < s}[kind].astype(BF16)


def _tri_dot(tri, x, passes):
    acc = None
    for _ in range(passes):
        part = x.astype(BF16)
        d = jnp.dot(tri, part, preferred_element_type=F32)
        acc = d if acc is None else acc + d
        x = x - part.astype(F32)
    return acc


GROUPS = (4, 2, 1)


def _loop_blocks(n, tiles, carry, descending=False, groups=GROUPS):
    at = (lambda k: n - 1 - k) if descending else (lambda k: k)
    done = 0
    for g in groups:
        left = n - done
        carry = lax.fori_loop(0, left // g, lambda h, c, g=g, done=done: tiles([at(done + g * h + k) for k in range(g)], c), carry)
        done = done + (left // g) * g
    return carry


def _resident(shape):
    return pl.BlockSpec((1,) + shape, lambda h, i: (h,) + (0,) * len(shape), pipeline_mode=pl.Buffered(1))


def _rows_per_head(a, b):
    return jnp.concatenate([jnp.broadcast_to(a, (HEAD_DIM, BLK)), jnp.broadcast_to(b, (HEAD_DIM, BLK))], axis=0)


def _fold_heads(stacked, in_a):
    return jnp.where(in_a, stacked[:BLK], stacked[BLK:])


def _fox_fwd(qx, kx, v_t, dg):
    p, t = qx.shape[0], qx.shape[1]
    nq = t // BLK
    nh = 2 * p

    def body(q_ref, k_ref, vt_ref, o_ref, lse_ref):
        i = pl.program_id(1)
        causal, _ = _key_query_masks()
        q = q_ref[0]

        def scores(j, masked):
            s2 = lax.dot_general(k_ref[0, j], q, NT_DIMS, preferred_element_type=F32)
            s = [s2[a * BLK:(a + 1) * BLK] for a in range(2)]
            return [jnp.where(causal, x, NEG) for x in s] if masked else s

        def update(blocks, carry):
            m, l, acc = carry
            m_new, l_new, alpha = [], [], []
            pr = [[] for _ in blocks]
            for a in range(2):
                mn = m[a]
                for _, s in blocks:
                    mn = jnp.maximum(mn, jnp.max(s[a], axis=0, keepdims=True))
                al = jnp.exp(m[a] - mn)
                ln = al * l[a]
                for k, (_, s) in enumerate(blocks):
                    pa = jnp.exp(s[a] - mn)
                    ln = ln + jnp.sum(pa, axis=0, keepdims=True)
                    pr[k].append(pa.astype(BF16))
                m_new.append(mn)
                alpha.append(al)
                l_new.append(ln)
            acc = _rows_per_head(*alpha) * acc
            for k, (j, _) in enumerate(blocks):
                acc = acc + jnp.dot(vt_ref[0, j], jnp.concatenate(pr[k], axis=0), preferred_element_type=F32)
            return tuple(m_new), tuple(l_new), acc

        tiles = lambda js, c: update([(j, scores(j, False)) for j in js], c)
        neg, zero = jnp.full((1, BLK), NEG, F32), jnp.zeros((1, BLK), F32)
        carry = _loop_blocks(i, tiles, ((neg, neg), (zero, zero), jnp.zeros((LANES, BLK), F32)))
        m, l, acc = update([(i, scores(i, True))], carry)
        o_ref[...] = acc / _rows_per_head(*l)
        lse_ref[0] = m[0] + jnp.log(l[0])
        lse_ref[1] = m[1] + jnp.log(l[1])

    row = pl.BlockSpec((2, 1, BLK), lambda h, i: (h, 0, i))
    return _pcall(
        body, name="fox_fwd", grid=(p, nq),
        out_shape=[jax.ShapeDtypeStruct((dg, t), F32), jax.ShapeDtypeStruct((nh, 1, t), F32)],
        in_specs=[pl.BlockSpec((1, BLK, 2 * LANES), lambda h, i: (h, i, 0)), _resident((nq, 2 * BLK, 2 * LANES)),
                  _resident((nq, LANES, 2 * BLK))],
        out_specs=[pl.BlockSpec((LANES, BLK), lambda h, i: (h, i)), row],
        compiler_params=_params(("arbitrary", "arbitrary")),
    )(qx, kx, v_t)


def _fox_bwd(qxb, kx, kx_t, v_st, do, delta, dg):
    p, t = qxb.shape[0], qxb.shape[1]
    nq = t // BLK
    nh = 2 * p

    def body(q_ref, k_ref, kt_ref, v_ref, do_ref, dl_ref, dq_ref, dft_ref, dk_ref, dv_ref, dkx_ref):
        i = pl.program_id(1)

        @pl.when(i == 0)
        def _():
            dk_ref[...] = jnp.zeros_like(dk_ref)
            dv_ref[...] = jnp.zeros_like(dv_ref)
            dkx_ref[...] = jnp.zeros_like(dkx_ref)

        in_a, _ = _head_masks()
        first_lane = lax.broadcasted_iota(jnp.int32, (1, LANES), 1) == 0
        causal, _ = _key_query_masks()
        q, do2 = q_ref[0], do_ref[...]
        dl = (dl_ref[0], dl_ref[1])

        def products(j):
            return (lax.dot_general(k_ref[0, j], q, NT_DIMS, preferred_element_type=F32),
                    lax.dot_general(v_ref[0, j], do2, NT_DIMS, preferred_element_type=F32))

        def dscores(prod, masked):
            s2, dp2 = prod
            pr, ds = [], []
            for a in range(2):
                s = s2[a * BLK:(a + 1) * BLK]
                if masked:
                    s = jnp.where(causal, s, NEG)
                pa = jnp.exp(s)
                ds.append((pa * (dp2[a * BLK:(a + 1) * BLK] - dl[a])).astype(BF16))
                pr.append(pa.astype(BF16))
            return jnp.concatenate(ds, axis=0), jnp.concatenate(pr, axis=0)

        def accumulate(j, dsb, prb, dq):
            off = pl.multiple_of(j * BLK, BLK)
            dk_full = jnp.dot(dsb, q, preferred_element_type=F32)
            dk_ref[pl.ds(off, BLK), :] += _fold_heads(dk_full[:, :LANES], in_a)
            dkx_ref[pl.ds(off, BLK), :] += jnp.where(first_lane, dk_full[:BLK, LANES:], dk_full[BLK:, LANES:])
            dv_ref[pl.ds(off, BLK), :] += _fold_heads(jnp.dot(prb, do2, preferred_element_type=F32), in_a)
            return dq + jnp.dot(kt_ref[0, j], dsb, preferred_element_type=F32)

        def tiles(js, dq, masked=False):
            prods = [products(j) for j in js]
            grads = [dscores(pr, masked) for pr in prods]
            for j, (dsb, prb) in zip(js, grads):
                dq = accumulate(j, dsb, prb, dq)
            return dq

        dq = _loop_blocks(i, tiles, jnp.zeros((XROWS, BLK), F32))
        dq = tiles([i], dq, True)
        dq_ref[...] = dq[:LANES]
        dft_ref[0] = dq[LANES + LANE_FT_A:LANES + LANE_FT_A + 1]
        dft_ref[1] = dq[LANES + LANE_FT_B:LANES + LANE_FT_B + 1]

    row = pl.BlockSpec((2, 1, BLK), lambda h, i: (h, 0, i))
    acc = pl.BlockSpec((t, LANES), lambda h, i: (0, h))
    return _pcall(
        body, name="fox_bwd", grid=(p, nq),
        out_shape=[jax.ShapeDtypeStruct((dg, t), F32), jax.ShapeDtypeStruct((nh, 1, t), F32)] + [jax.ShapeDtypeStruct((t, dg), F32)] * 3,
        in_specs=[pl.BlockSpec((1, BLK, 2 * LANES), lambda h, i: (h, i, 0)), _resident((nq, 2 * BLK, 2 * LANES)),
                  _resident((nq, XROWS, 2 * BLK)), _resident((nq, 2 * BLK, LANES)),
                  pl.BlockSpec((BLK, LANES), lambda h, i: (i, h)), row],
        out_specs=[pl.BlockSpec((LANES, BLK), lambda h, i: (h, i)), row, acc, acc, acc],
        compiler_params=_params(("arbitrary", "arbitrary")),
    )(qxb, kx, kx_t, v_st, do, delta)


def _softplus_of(z):
    return jnp.maximum(z, 0.0) + jnp.log(1.0 + jnp.exp(-jnp.abs(z)))


def _sb_fwd(qkv, k_st, v_t, dg):
    t = qkv.shape[0]
    p, nq = dg // LANES, t // BLK
    nh = 2 * p

    def body(q_ref, k_ref, vt_ref, o_ref, rt_ref):
        i = pl.program_id(1)
        _, strict = _key_query_masks()
        suffix = _key_triangle("suffix")
        q = q_ref[...]

        def scores(j):
            z2 = lax.dot_general(k_ref[0, j], q, NT_DIMS, preferred_element_type=F32)
            return [z2[a * BLK:(a + 1) * BLK] for a in range(2)]

        def suffix_sums(z, masked):
            out = []
            for a in range(2):
                sp = _softplus_of(z[a])
                if masked:
                    sp = jnp.where(strict, sp, 0.0)
                out.append(_tri_dot(suffix, sp, 2))
            return out

        def weights(z, cs, rest, masked):
            w, rest_new = [], []
            for a in range(2):
                wa = jnp.exp(z[a] - cs[a] - rest[a])
                if masked:
                    wa = jnp.where(strict, wa, 0.0)
                w.append(wa.astype(BF16))
                rest_new.append(rest[a] + cs[a][0:1])
            return jnp.concatenate(w, axis=0), tuple(rest_new)

        def tiles(js, carry, masked=False):
            rest, acc = carry
            zs = [scores(j) for j in js]
            css = [suffix_sums(z, masked) for z in zs]
            ws = []
            for z, cs in zip(zs, css):
                w2, rest = weights(z, cs, rest, masked)
                ws.append(w2)
            for j, w2 in zip(js, ws):
                acc = acc + jnp.dot(vt_ref[0, j], w2, preferred_element_type=F32)
            return rest, acc

        zero = jnp.zeros((1, BLK), F32)
        carry = tiles([i], ((zero, zero), jnp.zeros((LANES, BLK), F32)), True)
        rest, acc = _loop_blocks(i, tiles, carry, descending=True)
        o_ref[...] = acc
        rt_ref[0] = rest[0]
        rt_ref[1] = rest[1]

    return _pcall(
        body, name="sb_fwd", grid=(p, nq),
        out_shape=[jax.ShapeDtypeStruct((dg, t), F32), jax.ShapeDtypeStruct((nh, 1, t), F32)],
        in_specs=[pl.BlockSpec((BLK, LANES), lambda h, i: (i, 3 * p + h)), _resident((nq, 2 * BLK, LANES)),
                  _resident((nq, LANES, 2 * BLK))],
        out_specs=[pl.BlockSpec((LANES, BLK), lambda h, i: (h, i)), pl.BlockSpec((2, 1, BLK), lambda h, i: (h, 0, i))],
        compiler_params=_params(("arbitrary", "arbitrary")),
    )(qkv, k_st, v_t)


def _sb_bwd(qkv, k_st, k_t, v_st, do, rtot, dg):
    t = qkv.shape[0]
    p, nq = dg // LANES, t // BLK

    def body(q_ref, k_ref, kt_ref, v_ref, do_ref, rt_ref, dq_ref, dk_ref, dv_ref):
        i = pl.program_id(1)

        @pl.when(i == 0)
        def _():
            dk_ref[...] = jnp.zeros_like(dk_ref)
            dv_ref[...] = jnp.zeros_like(dv_ref)

        in_a, _ = _head_masks()
        _, strict = _key_query_masks()
        before_m, prefix_m = _key_triangle("before"), _key_triangle("prefix")
        q, do2 = q_ref[...], do_ref[...]
        rt = (rt_ref[0], rt_ref[1])

        def products(j):
            z2 = lax.dot_general(k_ref[0, j], q, NT_DIMS, preferred_element_type=F32)
            da2 = lax.dot_general(v_ref[0, j], do2, NT_DIMS, preferred_element_type=F32)
            return [z2[a * BLK:(a + 1) * BLK] for a in range(2)], [da2[a * BLK:(a + 1) * BLK] for a in range(2)]

        def softplus_sums(z, masked):
            sp = [_softplus_of(x) for x in z]
            if masked:
                sp = [jnp.where(strict, x, 0.0) for x in sp]
            return sp, [_tri_dot(before_m, x, 2) for x in sp]

        def weight_grads(z, da, sp, pre, before, masked):
            w, g, pg, before_new = [], [], [], []
            for a in range(2):
                wa = jnp.exp(z[a] + (before[a] - rt[a]) + pre[a])
                if masked:
                    wa = jnp.where(strict, wa, 0.0)
                ga = wa * da[a]
                w.append(wa.astype(BF16))
                g.append(ga)
                pg.append(jnp.dot(prefix_m, ga.astype(BF16), preferred_element_type=F32))
                before_new.append(before[a] + pre[a][BLK - 1:BLK] + sp[a][BLK - 1:BLK])
            return jnp.concatenate(w, axis=0), g, pg, tuple(before_new)

        def dlogits(sp, g, pg, gbefore, masked):
            dz, gbefore_new = [], []
            for a in range(2):
                s_incl = gbefore[a] + pg[a]
                dza = (g[a] - s_incl) + jnp.exp(-sp[a]) * s_incl
                if masked:
                    dza = jnp.where(strict, dza, 0.0)
                dz.append(dza.astype(BF16))
                gbefore_new.append(s_incl[BLK - 1:BLK])
            return jnp.concatenate(dz, axis=0), tuple(gbefore_new)

        def accumulate(j, dzb, wb, dq):
            off = pl.multiple_of(j * BLK, BLK)
            dk_ref[pl.ds(off, BLK), :] += _fold_heads(jnp.dot(dzb, q, preferred_element_type=F32), in_a)
            dv_ref[pl.ds(off, BLK), :] += _fold_heads(jnp.dot(wb, do2, preferred_element_type=F32), in_a)
            return dq + jnp.dot(kt_ref[0, j], dzb, preferred_element_type=F32)

        def tiles(js, carry, masked=False):
            before, gbefore, dq = carry
            prods = [products(j) for j in js]
            sums = [softplus_sums(z, masked) for z, _ in prods]
            grads = []
            for (z, da), (sp, pre) in zip(prods, sums):
                wb, g, pg, before = weight_grads(z, da, sp, pre, before, masked)
                grads.append((wb, g, pg))
            for j, (sp, _), (wb, g, pg) in zip(js, sums, grads):
                dzb, gbefore = dlogits(sp, g, pg, gbefore, masked)
                dq = accumulate(j, dzb, wb, dq)
            return before, gbefore, dq

        zero = jnp.zeros((1, BLK), F32)
        carry = _loop_blocks(i, tiles, ((zero, zero), (zero, zero), jnp.zeros((LANES, BLK), F32)), groups=(2, 1))
        dq_ref[...] = tiles([i], carry, True)[2]

    acc = pl.BlockSpec((t, LANES), lambda h, i: (0, h))
    return _pcall(
        body, name="sb_bwd", grid=(p, nq),
        out_shape=[jax.ShapeDtypeStruct((dg, t), F32)] + [jax.ShapeDtypeStruct((t, dg), F32)] * 2,
        in_specs=[pl.BlockSpec((BLK, LANES), lambda h, i: (i, 3 * p + h)), _resident((nq, 2 * BLK, LANES)),
                  _resident((nq, LANES, 2 * BLK)), _resident((nq, 2 * BLK, LANES)),
                  pl.BlockSpec((BLK, LANES), lambda h, i: (i, h)), pl.BlockSpec((2, 1, BLK), lambda h, i: (h, 0, i))],
        out_specs=[pl.BlockSpec((LANES, BLK), lambda h, i: (h, i)), acc, acc],
        compiler_params=_params(("arbitrary", "arbitrary")),
    )(qkv, k_st, k_t, v_st, do, rtot)


def _tri_constants(nh, t):
    nb = t // LANES
    r = nh * nb
    li = np.arange(LANES)
    tri_in = (li[:, None] <= li[None, :])
    ri = np.arange(r)
    same = (ri[:, None] // nb) == (ri[None, :] // nb)
    blk = same & (ri[None, :] < ri[:, None])
    blk_rev = same & (ri[None, :] > ri[:, None])
    head_rows = (np.arange(max(8, nh))[:, None] == (ri[None, :] // nb))
    as_bf16 = lambda a: jnp.asarray(a.astype(np.float32), BF16)
    return as_bf16(tri_in), as_bf16(blk), as_bf16(tri_in.T), as_bf16(blk_rev), as_bf16(head_rows)


def kernel(x, c, w_ada, b_ada, g_attn, w_in, b_fgate, g_out_fox, g_out_sb, w_out, g_mlp, w_up, conv_w, conv_b, w_down, g_final, loss_target, m_w_ada, m_b_ada, m_g_attn, m_w_in, m_b_fgate, m_g_out_fox, m_g_out_sb, m_w_out, m_g_mlp, m_w_up, m_conv_w, m_conv_b, m_w_down, m_g_final, v_w_ada, v_b_ada, v_g_attn, v_w_in, v_b_fgate, v_g_out_fox, v_g_out_sb, v_w_out, v_g_mlp, v_w_up, v_conv_w, v_conv_b, v_w_down, v_g_final):
    t, d = x.shape[1], x.shape[2]
    dg = d // 2
    nh = dg // HEAD_DIM
    n_in = 6 * dg + nh
    dff = w_down.shape[1] * 4
    dfp = -(-dff // 256) * 256
    cf = 256
    tm = _tile(t, (512, 256, 128))
    nq = t // BLK
    xi, yi, ci = lax.axis_index("x"), lax.axis_index("y"), lax.axis_index("c")
    shard = 2 * xi + yi
    me = 4 * xi + 2 * yi + ci

    x2d, tg2d = x[0], loss_target[0]

    c_all = _all_gather8(jnp.pad(c, ((0, 7), (0, 0)))).reshape(8, 8, d)[:, 0, :]
    ada_cols = w_ada.shape[2]
    b_shard = lax.dynamic_slice(b_ada, (0, shard * ada_cols), (1, ada_cols))
    sc_all, mod_shard = _ada_fwd(c_all, w_ada[0], b_shard)
    mod_all = _all_gather8(mod_shard).reshape(4, 2, 8, ada_cols)
    mod_me = lax.dynamic_index_in_dim(mod_all[:, 0], me, axis=1, keepdims=False)
    mod8 = jnp.pad(mod_me.reshape(6, d), ((0, 2), (0, 0)))

    lane_pad = lambda a: jnp.pad(a, ((0, 0),) * (a.ndim - 1) + ((0, -a.shape[-1] % LANES),))
    g_in, g_out, g_up, g_down, g_cw = _gather_xy(
        [lane_pad(w_in[0].astype(BF16)), w_out[0].astype(BF16), lane_pad(w_up[0].astype(BF16)), w_down[0].astype(BF16),
         lane_pad(conv_w[0])])
    g_in, g_up, g_cw = g_in[:, :, :n_in // 4], g_up[:, :, :dff // 2], g_cw[:, :, :dff // 2]
    w_in_full = jnp.transpose(g_in, (1, 0, 2)).reshape(d, n_in)
    w_qkv = w_in_full[:, :6 * dg]
    w_f = jnp.pad(w_in_full[:, 6 * dg:], ((0, 0), (0, LANES - nh)))
    w_out_full = g_out.reshape(2 * dg, d)
    w_up_full = jnp.transpose(g_up, (1, 0, 2)).reshape(d, 2 * dff)
    padc = ((0, 0), (0, dfp - dff))
    wg, wv = jnp.pad(w_up_full[:, :dff], padc), jnp.pad(w_up_full[:, dff:], padc)
    wd = jnp.pad(g_down.reshape(dff, d), ((0, dfp - dff), (0, 0)))
    cw_full = jnp.transpose(g_cw, (1, 0, 2)).reshape(3, 2 * dff)
    cw4 = jnp.concatenate([cw_full, conv_b], axis=0)
    cwg = jnp.pad(cw4[:, :dff], ((0, 4), (0, dfp - dff)))
    cwv = jnp.pad(cw4[:, dff:], ((0, 4), (0, dfp - dff)))

    qkv, fl, h1 = _in_proj_fwd(x2d, mod8, g_attn, w_qkv, w_f, tm)
    tri_in, tri_blk, tri_in_rev, tri_blk_rev, head_rows = _tri_constants(nh, t)
    fl2d = fl[:, :nh].T.reshape(nh * t // LANES, LANES)
    b_rows = jnp.repeat(b_fgate[0], t // LANES)[:, None]
    f2d = _fgate_fwd(fl2d, b_rows, tri_in, tri_blk)
    fcol = f2d.reshape(nh, 1, t)
    pairs = nh // 2
    qx, kx, kx_t, vf_st, vf_t, ks_st, ks_t, vs_st, vs_t = _attn_operands(qkv, fcol, dg)
    o_fox_t, lse = _fox_fwd(qx, kx, vf_t, dg)
    o_sb_t, rtot = _sb_fwd(qkv, ks_st, vs_t, dg)
    o_fox, o_sb = o_fox_t, o_sb_t
    li = np.arange(dg)
    bd = jnp.asarray((li[:, None] // HEAD_DIM == li[None, :] // HEAD_DIM).astype(np.float32), BF16)
    hsel = jnp.asarray((li[:, None] // HEAD_DIM == np.arange(LANES)[None, :]).astype(np.float32), BF16)
    x2, mix = _attn_out_fwd(x2d, o_fox, o_sb, g_out_fox, g_out_sb, w_out_full, mod8, bd, tm)
    g_final2 = g_final[None, :]
    dx3, h2, part_f = _ffn_fwd(x2, tg2d, mod8, g_mlp, g_final2, wg, wv, cwg, cwv, wd, tm, cf)

    tm_ffn_bwd = _tile(t, (256, 128))
    dx2, dupg, dupv, act, dxg3, part_b, pcg, pcv = _ffn_bwd(x2, dx3, mod8, g_mlp, wg, wv, cwg, cwv, wd, tm_ffn_bwd, cf)
    do_fox, do_sb, delta, dxg2, part_o = _attn_out_bwd(dx2, mix, o_fox, o_sb, g_out_fox, g_out_sb, w_out_full, mod8, bd, hsel, tm)
    drow = delta[:, :nh].T.reshape(nh, 1, t)
    qxb = _fox_q_bwd(qkv, fcol, lse, dg)
    dq_f_t, dft, dk_f, dv_f, dkx = _fox_bwd(qxb, kx, kx_t, vf_st, do_fox, drow, dg)
    dq_s_t, dk_s, dv_s = _sb_bwd(qkv, ks_st, ks_t, vs_st, do_sb, rtot, dg)
    dq_f, dq_s = dq_f_t, dq_s_t
    f2d_shape = (nh * t // LANES, LANES)
    dfs = jnp.transpose(dkx.reshape(t, pairs, LANES)[:, :, :2], (1, 2, 0))
    dfl2d, gb8 = _fgate_bwd(fl2d, b_rows, dft.reshape(f2d_shape), dfs.reshape(f2d_shape), tri_in_rev, tri_blk_rev, head_rows)
    dfl = jnp.pad(dfl2d.reshape(nh, t).T, ((0, 0), (0, LANES - nh)))
    grad_x, dproj, dflb, part_i = _in_proj_bwd([dq_f, dk_f, dv_f, dq_s, dk_s, dv_s], dfl, w_qkv, w_f, x2d, dx2, mod8, g_attn, tm)

    gw_qkv = _matmul_tn(h1, dproj, "grad_w_qkv")
    gw_f = _matmul_tn(h1, dflb, "grad_w_f")
    gw_in = jnp.concatenate([gw_qkv, gw_f[:, :nh]], axis=1)
    gw_out = _matmul_tn(mix, dxg2, "grad_w_out")
    gw_upg = _matmul_tn(h2, dupg, "grad_w_up_gate")
    gw_upv = _matmul_tn(h2, dupv, "grad_w_up_val")
    gw_up = jnp.concatenate([gw_upg[:, :dff], gw_upv[:, :dff]], axis=1)
    gw_down = _matmul_tn(act, dxg3, "grad_w_down")[:dff]

    sf = _sum_leading(part_f, "sum_part_ffn_fwd")
    sb_ = _sum_leading(part_b, "sum_part_ffn_bwd")
    so = _sum_leading(part_o, "sum_part_attn_out")
    si = _sum_leading(part_i, "sum_part_in_proj")
    scg = _sum_leading(pcg, "sum_part_conv_gate")
    scv = _sum_leading(pcv, "sum_part_conv_val")
    gb_f = gb8[:nh, 0]
    dmod = jnp.concatenate([si[0], si[1], so[0], sb_[0], sb_[1], sf[1]])
    g_conv_w = jnp.concatenate([scg[0:3, :dff], scv[0:3, :dff]], axis=1).reshape(-1)
    g_conv_b = jnp.concatenate([scg[3, :dff], scv[3, :dff]])
    loss_part = jnp.sum(sf[2])
    fields = [dmod, si[2], gb_f, so[1, :dg], so[1, dg:], sb_[2], g_conv_b, sf[0], g_conv_w, loss_part[None]]
    sizes = [int(f.shape[0]) for f in fields]
    n_pack = sum(sizes)
    lanes_pack = -(-n_pack // (8 * LANES)) * LANES
    pack = jnp.pad(jnp.concatenate(fields), (0, 8 * lanes_pack - n_pack)).reshape(8, lanes_pack)
    gathered = _all_gather8(pack)
    tot = _sum_leading(gathered.reshape(8, 8, lanes_pack), "sum_pack").reshape(-1)
    offs = np.concatenate([[0], np.cumsum(sizes)])
    take = lambda k: tot[int(offs[k]):int(offs[k + 1])]
    g_b_ada, g_g_attn, g_b_fgate, g_g_fox, g_g_sb, g_g_mlp, g_cb, g_g_final, g_cw_full, loss_v = [take(k) for k in range(10)]
    loss = loss_v[0]
    dmod_all = gathered.reshape(8, 8 * lanes_pack)[:, :6 * d]
    dmod_cols = lax.dynamic_slice(dmod_all, (0, shard * ada_cols), (8, ada_cols))
    g_w_ada = _ada_bwd(sc_all.T, dmod_cols)

    def col_pieces(g):
        r, cc = g.shape
        return jnp.transpose(g.reshape(2, r // 2, 4, cc // 4), (2, 0, 1, 3)).reshape(8, r // 2, cc // 4)

    def row_pieces(g):
        r, cc = g.shape
        return g.reshape(8, r // 8, cc)

    pieces = (lane_pad(col_pieces(gw_in)), row_pieces(gw_out), lane_pad(col_pieces(gw_up)), row_pieces(gw_down))
    recv = _scatter8([_to_bf16(p, "pieces_bf16_" + nm) for p, nm in zip(pieces, ("w_in", "w_out", "w_up", "w_down"))])
    halves = [_sum_leading(rv, nm) for rv, nm in zip(recv, ("sum_w_in", "sum_w_out", "sum_w_up", "sum_w_down"))]
    swapped = _swap_halves(halves)
    shard_cols = (n_in // 4, d, dff // 2, d)
    g_w_in, g_w_out, g_w_up, g_w_down = [s.reshape(2 * s.shape[1], s.shape[2])[:, :cc] for s, cc in zip(swapped, shard_cols)]
    g_conv_w_shard = lax.dynamic_slice(g_cw_full.reshape(3, 2 * dff), (0, shard * (dff // 2)), (3, dff // 2))

    grads, deltas, new_m, new_v = {}, {}, {}, {}

    def step(name, w, g, m, v):
        shape = w.shape
        as2d = lambda a: a.reshape(-1, shape[-1])
        dl, nm, nv = _adamw(as2d(w), as2d(g), as2d(m), as2d(v), "adamw_" + name)
        grads[name], deltas[name], new_m[name], new_v[name] = g.reshape(shape), dl.reshape(shape), nm.reshape(shape), nv.reshape(shape)

    step("w_ada", w_ada, g_w_ada, m_w_ada, v_w_ada)
    step("w_in", w_in, g_w_in, m_w_in, v_w_in)
    step("w_out", w_out, g_w_out, m_w_out, v_w_out)
    step("w_up", w_up, g_w_up, m_w_up, v_w_up)
    step("conv_w", conv_w, g_conv_w_shard, m_conv_w, v_conv_w)
    step("w_down", w_down, g_w_down, m_w_down, v_w_down)

    small = [("b_ada", b_ada, g_b_ada, m_b_ada, v_b_ada), ("g_attn", g_attn, g_g_attn, m_g_attn, v_g_attn),
             ("b_fgate", b_fgate, g_b_fgate, m_b_fgate, v_b_fgate), ("g_out_fox", g_out_fox, g_g_fox, m_g_out_fox, v_g_out_fox),
             ("g_out_sb", g_out_sb, g_g_sb, m_g_out_sb, v_g_out_sb), ("g_mlp", g_mlp, g_g_mlp, m_g_mlp, v_g_mlp),
             ("conv_b", conv_b, g_cb, m_conv_b, v_conv_b), ("g_final", g_final, g_g_final, m_g_final, v_g_final)]
    ssz = [int(np.prod(s[1].shape)) for s in small]
    n_small = sum(ssz)
    lanes_small = -(-n_small // (8 * LANES)) * LANES
    packs = [jnp.pad(jnp.concatenate([s[k].reshape(-1) for s in small]), (0, 8 * lanes_small - n_small)).reshape(8, lanes_small)
             for k in (1, 2, 3, 4)]
    dl_s, nm_s, nv_s = _adamw(*packs, "adamw_small")
    so_ = np.concatenate([[0], np.cumsum(ssz)])
    for k, s in enumerate(small):
        cut = lambda a: a.reshape(-1)[int(so_[k]):int(so_[k + 1])].reshape(s[1].shape)
        grads[s[0]], deltas[s[0]], new_m[s[0]], new_v[s[0]] = s[2].reshape(s[1].shape), cut(dl_s), cut(nm_s), cut(nv_s)

    order = ["w_ada", "b_ada", "g_attn", "w_in", "b_fgate", "g_out_fox", "g_out_sb", "w_out", "g_mlp", "w_up",
             "conv_w", "conv_b", "w_down", "g_final"]
    return (loss, grad_x[None], *[grads[n] for n in order], *[deltas[n] for n in order],
            *[new_m[n] for n in order], *[new_v[n] for n in order])
```

```python
import functools

import numpy as np
import jax
import jax.numpy as jnp
from jax import lax
from jax.experimental import pallas as pl
from jax.experimental.pallas import tpu as pltpu

F32 = jnp.float32
BF16 = jnp.bfloat16
MESH = pl.DeviceIdType.MESH

HEAD_DIM = 64
LANES = 128
EPS = 1e-6
NEG = -1e30
ADAM_LR, ADAM_B1, ADAM_B2, ADAM_EPS, ADAM_WD, ADAM_STEP = 0.001, 0.9, 0.999, 1e-08, 0.01, 10
V7X_VMEM_BYTES = 64 * 1024 * 1024
VMEM_LIMIT = V7X_VMEM_BYTES - 12 * 1024 * 1024
NT_DIMS = (((1,), (1,)), ((), ()))
TN_DIMS = (((0,), (0,)), ((), ()))


def _pcall(body, **kw):
    return pl.pallas_call(body, **kw)


def _params(sem=None, **kw):
    return pltpu.CompilerParams(dimension_semantics=sem, vmem_limit_bytes=VMEM_LIMIT, **kw)


def _split_dot(x, m, passes):
    acc = None
    for _ in range(passes):
        part = x.astype(BF16)
        d = jnp.dot(part, m, preferred_element_type=F32)
        acc = d if acc is None else acc + d
        x = x - part.astype(F32)
    return acc


def _tile(n, candidates):
    for t in candidates:
        if n % t == 0:
            return t
    return n


def _rows_tile(rows, row_bytes, budget=2 * 1024 * 1024):
    best = None
    for t in range(8, rows + 1, 8):
        if rows % t == 0 and t * row_bytes <= budget:
            best = t
    return best if best is not None else rows


def _all_gather8(v):
    m_per, n = v.shape

    def body(x_ref, out_ref, send_sems, recv_sems, local_sem):
        x, y, c = lax.axis_index("x"), lax.axis_index("y"), lax.axis_index("c")
        me, sibling = (x, y, c), (x, y, 1 - c)
        chips = [(1 - x, y), (x, 1 - y), (1 - x, 1 - y)]

        def rows(px, py, pc):
            return out_ref.at[pl.ds((4 * px + 2 * py + pc) * m_per, m_per), :]

        def copy(k, block, to, src=None):
            return pltpu.make_async_remote_copy(
                src_ref=rows(*block) if src is None else src, dst_ref=rows(*block),
                send_sem=send_sems.at[k], recv_sem=recv_sems.at[k], device_id=to, device_id_type=MESH)

        mine = pltpu.make_async_copy(x_ref, rows(*me), local_sem)
        mine.start()
        first = [copy(0, me, sibling, src=x_ref)]
        first += [copy(1 + j, me, (*chip, c), src=x_ref) for j, chip in enumerate(chips)]
        for cp in first:
            cp.start()
        passed = [copy(4 + j, (*chip, c), sibling) for j, chip in enumerate(chips)]
        for j, chip in enumerate(chips):
            copy(1 + j, (*chip, c), me).wait_recv()
            passed[j].start()
        copy(0, sibling, me).wait_recv()
        for j, chip in enumerate(chips):
            copy(4 + j, (*chip, 1 - c), me).wait_recv()
        for cp in first + passed:
            cp.wait_send()
        mine.wait()

    return _pcall(
        body, name="all_gather8",
        out_shape=jax.ShapeDtypeStruct((8 * m_per, n), v.dtype),
        in_specs=[pl.BlockSpec(memory_space=pltpu.VMEM)],
        out_specs=pl.BlockSpec(memory_space=pltpu.VMEM),
        scratch_shapes=[pltpu.SemaphoreType.DMA((7,)), pltpu.SemaphoreType.DMA((7,)), pltpu.SemaphoreType.DMA],
        compiler_params=pltpu.CompilerParams(vmem_limit_bytes=VMEM_LIMIT),
    )(v)


def _gather_xy(shards):
    n = len(shards)

    def body(*refs):
        ins, outs = refs[:n], refs[n:2 * n]
        send_sems, recv_sems, local_sems = refs[2 * n:]
        x, y, c = lax.axis_index("x"), lax.axis_index("y"), lax.axis_index("c")
        chips = [(1 - x, y), (x, 1 - y), (1 - x, 1 - y)]
        mine = 2 * x + y
        local, remote = [], []
        for w in range(n):
            cp = pltpu.make_async_copy(ins[w], outs[w].at[mine], local_sems.at[w])
            cp.start()
            local.append(cp)
            for k, (px, py) in enumerate(chips):
                cp = pltpu.make_async_remote_copy(
                    src_ref=ins[w], dst_ref=outs[w].at[mine], send_sem=send_sems.at[3 * w + k],
                    recv_sem=recv_sems.at[3 * w + k], device_id=(px, py, c), device_id_type=MESH)
                cp.start()
                remote.append(cp)
        for cp in remote:
            cp.wait_recv()
        for cp in remote:
            cp.wait_send()
        for cp in local:
            cp.wait()

    hbm = pl.BlockSpec(memory_space=pltpu.HBM)
    return _pcall(
        body, name="gather_xy",
        out_shape=[jax.ShapeDtypeStruct((4,) + s.shape, s.dtype) for s in shards],
        in_specs=[hbm] * n, out_specs=[hbm] * n,
        scratch_shapes=[pltpu.SemaphoreType.DMA((3 * n,)), pltpu.SemaphoreType.DMA((3 * n,)),
                        pltpu.SemaphoreType.DMA((n,))],
        compiler_params=pltpu.CompilerParams(vmem_limit_bytes=VMEM_LIMIT),
    )(*shards)


def _scatter8(pieces):
    n = len(pieces)

    def body(*refs):
        ins, outs = refs[:n], refs[n:2 * n]
        send_sems, recv_sems, local_sems = refs[2 * n:]
        x, y, c = lax.axis_index("x"), lax.axis_index("y"), lax.axis_index("c")
        me = 4 * x + 2 * y + c
        local, remote = [], []
        for w in range(n):
            cp = pltpu.make_async_copy(ins[w].at[me], outs[w].at[me], local_sems.at[w])
            cp.start()
            local.append(cp)
            for f in range(1, 8):
                px = 1 - x if f & 4 else x
                py = 1 - y if f & 2 else y
                pc = 1 - c if f & 1 else c
                cp = pltpu.make_async_remote_copy(
                    src_ref=ins[w].at[4 * px + 2 * py + pc], dst_ref=outs[w].at[me],
                    send_sem=send_sems.at[7 * w + f - 1], recv_sem=recv_sems.at[7 * w + f - 1],
                    device_id=(px, py, pc), device_id_type=MESH)
                cp.start()
                remote.append(cp)
        for cp in remote:
            cp.wait_recv()
        for cp in remote:
            cp.wait_send()
        for cp in local:
            cp.wait()

    hbm = pl.BlockSpec(memory_space=pltpu.HBM)
    return _pcall(
        body, name="scatter8",
        out_shape=[jax.ShapeDtypeStruct(p.shape, p.dtype) for p in pieces],
        in_specs=[hbm] * n, out_specs=[hbm] * n,
        scratch_shapes=[pltpu.SemaphoreType.DMA((7 * n,)), pltpu.SemaphoreType.DMA((7 * n,)),
                        pltpu.SemaphoreType.DMA((n,))],
        compiler_params=pltpu.CompilerParams(vmem_limit_bytes=VMEM_LIMIT),
    )(*pieces)


def _swap_halves(halves):
    n = len(halves)
    chunks = 8
    n_chunks = [max(k for k in (chunks, 4, 2, 1) if h.shape[0] % (8 * k) == 0) for h in halves]

    def body(*refs):
        ins, outs = refs[:n], refs[n:2 * n]
        send_sems, recv_sems, local_sems = refs[2 * n:]
        x, y, c = lax.axis_index("x"), lax.axis_index("y"), lax.axis_index("c")
        local, remote = [], []
        for w in range(n):
            cp = pltpu.make_async_copy(ins[w], outs[w].at[c], local_sems.at[w])
            cp.start()
            local.append(cp)
            rows = ins[w].shape[0] // n_chunks[w]
            for k in range(n_chunks[w]):
                cp = pltpu.make_async_remote_copy(
                    src_ref=ins[w].at[pl.ds(k * rows, rows)], dst_ref=outs[w].at[c, pl.ds(k * rows, rows)],
                    send_sem=send_sems.at[chunks * w + k], recv_sem=recv_sems.at[chunks * w + k],
                    device_id=(x, y, 1 - c), device_id_type=MESH)
                cp.start()
                remote.append(cp)
        for cp in remote:
            cp.wait_recv()
        for cp in remote:
            cp.wait_send()
        for cp in local:
            cp.wait()

    hbm = pl.BlockSpec(memory_space=pltpu.HBM)
    return _pcall(
        body, name="swap_halves",
        out_shape=[jax.ShapeDtypeStruct((2,) + h.shape, h.dtype) for h in halves],
        in_specs=[hbm] * n, out_specs=[hbm] * n,
        scratch_shapes=[pltpu.SemaphoreType.DMA((chunks * n,)), pltpu.SemaphoreType.DMA((chunks * n,)),
                        pltpu.SemaphoreType.DMA((n,))],
        compiler_params=pltpu.CompilerParams(vmem_limit_bytes=VMEM_LIMIT),
    )(*halves)


def _sum_leading(a, name):
    n, r, c = a.shape
    tr = _rows_tile(r, n * c * 4, budget=6 * 1024 * 1024)
    if a.dtype == BF16 and tr % 16:
        tr = r

    def body(a_ref, o_ref):
        acc = a_ref[0].astype(F32)
        for k in range(1, n):
            acc = acc + a_ref[k].astype(F32)
        o_ref[...] = acc

    return _pcall(
        body, name=name, grid=(r // tr,),
        out_shape=jax.ShapeDtypeStruct((r, c), F32),
        in_specs=[pl.BlockSpec((n, tr, c), lambda i: (0, i, 0))],
        out_specs=pl.BlockSpec((tr, c), lambda i: (i, 0)),
        compiler_params=_params(("arbitrary",)),
    )(a)


def _to_bf16(a, name):
    n, r, c = a.shape

    def body(a_ref, o_ref):
        o_ref[...] = a_ref[...].astype(BF16)

    spec = pl.BlockSpec((1, r, c), lambda i: (i, 0, 0))
    return _pcall(
        body, name=name, grid=(n,), out_shape=jax.ShapeDtypeStruct(a.shape, BF16),
        in_specs=[spec], out_specs=spec, compiler_params=_params(("arbitrary",)),
    )(a)


def _adamw(w, g, m, v, name):
    r, c = w.shape
    tr = _rows_tile(r, c * 4, budget=1024 * 1024)
    c1 = 1.0 - ADAM_B1 ** ADAM_STEP
    c2 = 1.0 - ADAM_B2 ** ADAM_STEP

    def body(w_ref, g_ref, m_ref, v_ref, d_ref, nm_ref, nv_ref):
        gg = g_ref[...]
        nm = ADAM_B1 * m_ref[...] + (1.0 - ADAM_B1) * gg
        nv = ADAM_B2 * v_ref[...] + (1.0 - ADAM_B2) * (gg * gg)
        m_hat = nm / c1
        v_hat = nv / c2
        d_ref[...] = -ADAM_LR * (m_hat / (jnp.sqrt(v_hat) + ADAM_EPS) + ADAM_WD * w_ref[...])
        nm_ref[...] = nm
        nv_ref[...] = nv

    spec = pl.BlockSpec((tr, c), lambda i: (i, 0))
    return _pcall(
        body, name=name, grid=(r // tr,),
        out_shape=[jax.ShapeDtypeStruct((r, c), F32)] * 3,
        in_specs=[spec] * 4, out_specs=[spec] * 3,
        compiler_params=_params(("arbitrary",)),
    )(w, g, m, v)


def _ada_fwd(c_all, w_shard, b_shard):
    nb, d = c_all.shape
    cols = w_shard.shape[1]

    def body(c_ref, w_ref, b_ref, sc_ref, mod_ref):
        cv = c_ref[...]
        sc = cv * jax.nn.sigmoid(cv)
        sc_ref[...] = sc
        mod_ref[...] = jnp.dot(sc.astype(BF16), w_ref[...].astype(BF16), preferred_element_type=F32) + b_ref[...]

    return _pcall(
        body, name="ada_fwd",
        out_shape=[jax.ShapeDtypeStruct((nb, d), F32), jax.ShapeDtypeStruct((nb, cols), F32)],
        compiler_params=pltpu.CompilerParams(vmem_limit_bytes=VMEM_LIMIT),
    )(c_all, w_shard, b_shard)


def _ada_bwd(sc_t, dmod_cols):
    d, nb = sc_t.shape
    cols = dmod_cols.shape[1]
    tr = _rows_tile(d, cols * 4, budget=1024 * 1024)

    def body(s_ref, m_ref, o_ref):
        s = s_ref[...]
        m = m_ref[...]
        acc = s[:, 0:1] * m[0:1, :]
        for b in range(1, nb):
            acc = acc + s[:, b:b + 1] * m[b:b + 1, :]
        o_ref[...] = acc

    return _pcall(
        body, name="ada_bwd", grid=(d // tr,),
        out_shape=jax.ShapeDtypeStruct((d, cols), F32),
        in_specs=[pl.BlockSpec((tr, nb), lambda i: (i, 0)), pl.BlockSpec((nb, cols), lambda i: (0, 0))],
        out_specs=pl.BlockSpec((tr, cols), lambda i: (i, 0)),
        compiler_params=_params(("arbitrary",)),
    )(sc_t, dmod_cols)


def _log_sigmoid(x):
    return jnp.minimum(x, 0.0) - jnp.log1p(jnp.exp(-jnp.abs(x)))


def _fgate_fwd(fl2d, b_rows, tri_in, tri_blk):
    r = fl2d.shape[0]

    def body(x_ref, b_ref, u_ref, l_ref, f_ref):
        lf = _log_sigmoid(x_ref[...] + b_ref[...])
        c1 = _split_dot(lf, u_ref[...], 3)
        tot = jnp.broadcast_to(c1[:, LANES - 1:LANES], (r, LANES))
        acc = None
        for _ in range(3):
            part = tot.astype(BF16)
            dd = jnp.dot(l_ref[...], part, preferred_element_type=F32)
            acc = dd if acc is None else acc + dd
            tot = tot - part.astype(F32)
        f_ref[...] = c1 + acc

    return _pcall(
        body, name="fgate_fwd", out_shape=jax.ShapeDtypeStruct((r, LANES), F32),
        compiler_params=pltpu.CompilerParams(vmem_limit_bytes=VMEM_LIMIT),
    )(fl2d, b_rows, tri_in, tri_blk)


def _fgate_bwd(fl2d, b_rows, df_query, df_key, tri_in_rev, tri_blk_rev, head_rows):
    r = fl2d.shape[0]
    nhp = head_rows.shape[0]

    def body(x_ref, b_ref, dq_ref, dk_ref, u_ref, l_ref, hr_ref, o_ref, gb_ref):
        c1 = _split_dot(dq_ref[...] + dk_ref[...], u_ref[...], 3)
        tot = jnp.broadcast_to(c1[:, 0:1], (r, LANES))
        acc = None
        for _ in range(3):
            part = tot.astype(BF16)
            dd = jnp.dot(l_ref[...], part, preferred_element_type=F32)
            acc = dd if acc is None else acc + dd
            tot = tot - part.astype(F32)
        x = x_ref[...] + b_ref[...]
        e = jnp.exp(-jnp.abs(x))
        dfl = (c1 + acc) * (jnp.where(x >= 0, e, 1.0) / (1.0 + e))
        o_ref[...] = dfl
        rs = jnp.broadcast_to(jnp.sum(dfl, axis=1, keepdims=True), (r, LANES))
        gb = None
        for _ in range(3):
            part = rs.astype(BF16)
            dd = jnp.dot(hr_ref[...], part, preferred_element_type=F32)
            gb = dd if gb is None else gb + dd
            rs = rs - part.astype(F32)
        gb_ref[...] = gb

    return _pcall(
        body, name="fgate_bwd",
        out_shape=[jax.ShapeDtypeStruct((r, LANES), F32), jax.ShapeDtypeStruct((nhp, LANES), F32)],
        compiler_params=pltpu.CompilerParams(vmem_limit_bytes=VMEM_LIMIT),
    )(fl2d, b_rows, df_query, df_key, tri_in_rev, tri_blk_rev, head_rows)


def _norm_mod(x, g, scale, shift):
    r = lax.rsqrt(jnp.mean(x * x, axis=-1, keepdims=True) + EPS)
    return (x * r * g) * (1.0 + scale) + shift


def _norm_mod_bwd(x, dh, g, scale):
    r = lax.rsqrt(jnp.mean(x * x, axis=-1, keepdims=True) + EPS)
    xn = x * r
    dshift = jnp.sum(dh, axis=0, keepdims=True)
    dscale = jnp.sum(dh * (xn * g), axis=0, keepdims=True)
    dxn_g = dh * (1.0 + scale)
    dg = jnp.sum(dxn_g * xn, axis=0, keepdims=True)
    dxn = dxn_g * g
    dx = r * (dxn - xn * jnp.mean(dxn * xn, axis=-1, keepdims=True))
    return dx, dshift, dscale, dg


def _in_proj_fwd(x, mod8, g_attn, w_qkv, w_f, tm):
    t, d = x.shape
    dg = w_qkv.shape[1] // 6

    def body(x_ref, mod_ref, g_ref, w_ref, wf_ref, qkv_ref, fl_ref, h1_ref, h_sc):
        j = pl.program_id(1)

        @pl.when(j == 0)
        def _():
            h = _norm_mod(x_ref[...], g_ref[...], mod_ref[1:2, :], mod_ref[0:1, :]).astype(BF16)
            h_sc[...] = h
            h1_ref[...] = h
            fl_ref[...] = jnp.dot(h, wf_ref[...], preferred_element_type=F32)

        s = jnp.where((j == 0) | (j == 3), HEAD_DIM ** -0.5, 1.0)
        qkv_ref[...] = (jnp.dot(h_sc[...], w_ref[...], preferred_element_type=F32) * s).astype(BF16)

    return _pcall(
        body, name="in_proj_fwd", grid=(t // tm, 6),
        out_shape=[jax.ShapeDtypeStruct((t, 6 * dg), BF16), jax.ShapeDtypeStruct((t, LANES), F32),
                   jax.ShapeDtypeStruct((t, d), BF16)],
        in_specs=[pl.BlockSpec((tm, d), lambda i, j: (i, 0)), pl.BlockSpec((8, d), lambda i, j: (0, 0)),
                  pl.BlockSpec((1, d), lambda i, j: (0, 0)), pl.BlockSpec((d, dg), lambda i, j: (0, j)),
                  pl.BlockSpec((d, LANES), lambda i, j: (0, 0))],
        out_specs=[pl.BlockSpec((tm, dg), lambda i, j: (i, j)), pl.BlockSpec((tm, LANES), lambda i, j: (i, 0)),
                   pl.BlockSpec((tm, d), lambda i, j: (i, 0))],
        scratch_shapes=[pltpu.VMEM((tm, d), BF16)],
        compiler_params=_params(("arbitrary", "arbitrary")),
    )(x, mod8, g_attn, w_qkv, w_f)


def _head_rstd(o, bd):
    return lax.rsqrt(_split_dot(o * o, bd, 3) * (1.0 / HEAD_DIM) + EPS)


def _attn_out_fwd(x, o_fox, o_sb, g_fox, g_sb, w_out, mod8, bd, tm):
    t, d = x.shape
    dg = o_fox.shape[0]

    def body(x_ref, of_ref, os_ref, gf_ref, gs_ref, w_ref, mod_ref, bd_ref, x2_ref, mix_ref):
        of, osb = of_ref[...].T, os_ref[...].T
        mf = (of * _head_rstd(of, bd_ref[...]) * gf_ref[...]).astype(BF16)
        ms = (osb * _head_rstd(osb, bd_ref[...]) * gs_ref[...]).astype(BF16)
        mix_ref[:, :dg] = mf
        mix_ref[:, dg:] = ms
        y = jnp.dot(mf, w_ref[:dg, :], preferred_element_type=F32) + jnp.dot(ms, w_ref[dg:, :], preferred_element_type=F32)
        x2_ref[...] = x_ref[...] + mod_ref[2:3, :] * y

    row = lambda w: pl.BlockSpec((tm, w), lambda i: (i, 0))
    full = lambda a: pl.BlockSpec(a.shape, lambda i: (0,) * a.ndim)
    return _pcall(
        body, name="attn_out_fwd", grid=(t // tm,),
        out_shape=[jax.ShapeDtypeStruct((t, d), F32), jax.ShapeDtypeStruct((t, 2 * dg), BF16)],
        in_specs=[row(d), pl.BlockSpec((dg, tm), lambda i: (0, i)), pl.BlockSpec((dg, tm), lambda i: (0, i)),
                  full(g_fox), full(g_sb), full(w_out), full(mod8), full(bd)],
        out_specs=[row(d), row(2 * dg)],
        compiler_params=_params(("arbitrary",)),
    )(x, o_fox, o_sb, g_fox, g_sb, w_out, mod8, bd)


def _attn_out_bwd(dx2, mix, o_fox, o_sb, g_fox, g_sb, w_out, mod8, bd, hsel, tm):
    t, d = dx2.shape
    dg = o_fox.shape[0]

    def body(dx_ref, mix_ref, of_ref, os_ref, gf_ref, gs_ref, w_ref, mod_ref, bd_ref, hs_ref,
             dof_ref, dos_ref, dlt_ref, dxg_ref, part_ref):
        dx = dx_ref[...]
        gate = mod_ref[2:3, :]
        dxg = (dx * gate).astype(BF16)
        dxg_ref[...] = dxg
        mixv = mix_ref[...]
        y = jnp.dot(mixv[:, :dg], w_ref[:dg, :], preferred_element_type=F32)
        y = y + jnp.dot(mixv[:, dg:], w_ref[dg:, :], preferred_element_type=F32)
        part_ref[0] = jnp.zeros((8, d), F32)
        part_ref[0, 0:1, :] = jnp.sum(dx * y, axis=0, keepdims=True)
        for grp, (o_ref, g_ref, do_ref) in enumerate(((of_ref, gf_ref, dof_ref), (os_ref, gs_ref, dos_ref))):
            dmix = lax.dot_general(dxg, w_ref[grp * dg:(grp + 1) * dg, :], NT_DIMS, preferred_element_type=F32)
            o = o_ref[...].T
            r = _head_rstd(o, bd_ref[...])
            n = o * r
            part_ref[0, 1:2, grp * dg:(grp + 1) * dg] = jnp.sum(dmix * n, axis=0, keepdims=True)
            dn = dmix * g_ref[...]
            mh = _split_dot(dn * n, bd_ref[...], 3) * (1.0 / HEAD_DIM)
            do = r * (dn - n * mh)
            do_ref[...] = do.astype(BF16)
            if grp == 0:
                dlt_ref[...] = _split_dot(do * o, hs_ref[...], 3)

    row = lambda w: pl.BlockSpec((tm, w), lambda i: (i, 0))
    full = lambda a: pl.BlockSpec(a.shape, lambda i: (0,) * a.ndim)
    nt = t // tm
    return _pcall(
        body, name="attn_out_bwd", grid=(nt,),
        out_shape=[jax.ShapeDtypeStruct((t, dg), BF16), jax.ShapeDtypeStruct((t, dg), BF16),
                   jax.ShapeDtypeStruct((t, LANES), F32), jax.ShapeDtypeStruct((t, d), BF16),
                   jax.ShapeDtypeStruct((nt, 8, d), F32)],
        in_specs=[row(d), row(2 * dg), pl.BlockSpec((dg, tm), lambda i: (0, i)), pl.BlockSpec((dg, tm), lambda i: (0, i)),
                  full(g_fox), full(g_sb), full(w_out), full(mod8),
                  full(bd), full(hsel)],
        out_specs=[row(dg), row(dg), row(LANES), row(d), pl.BlockSpec((1, 8, d), lambda i: (i, 0, 0))],
        compiler_params=_params(("arbitrary",)),
    )(dx2, mix, o_fox, o_sb, g_fox, g_sb, w_out, mod8, bd, hsel)


def _in_proj_bwd(dparts, dfl, w_qkv, w_f, x, dx2, mod8, g_attn, tm):
    t, d = x.shape
    dg = dparts[1].shape[1]

    def body(*refs):
        d_refs = refs[:6]
        dfl_ref, w_ref, wf_ref, x_ref, dx2_ref, mod_ref, g_ref, gx_ref, dp_ref, dflb_ref, part_ref = refs[6:]
        dh = None
        for k in range(6):
            dk = d_refs[k][...].T if k in (0, 3) else d_refs[k][...]
            if k in (0, 3):
                dk = dk * HEAD_DIM ** -0.5
            db = dk.astype(BF16)
            dp_ref[:, k * dg:(k + 1) * dg] = db
            term = lax.dot_general(db, w_ref[:, k * dg:(k + 1) * dg], NT_DIMS, preferred_element_type=F32)
            dh = term if dh is None else dh + term
        dfb = dfl_ref[...].astype(BF16)
        dflb_ref[...] = dfb
        dh = dh + lax.dot_general(dfb, wf_ref[...], NT_DIMS, preferred_element_type=F32)
        dx, dshift, dscale, dgn = _norm_mod_bwd(x_ref[...], dh, g_ref[...], mod_ref[1:2, :])
        gx_ref[...] = dx2_ref[...] + dx
        part_ref[0] = jnp.zeros((8, d), F32)
        part_ref[0, 0:1, :] = dshift
        part_ref[0, 1:2, :] = dscale
        part_ref[0, 2:3, :] = dgn

    row = lambda w: pl.BlockSpec((tm, w), lambda i: (i, 0))
    full = lambda a: pl.BlockSpec(a.shape, lambda i: (0,) * a.ndim)
    nt = t // tm
    return _pcall(
        body, name="in_proj_bwd", grid=(nt,),
        out_shape=[jax.ShapeDtypeStruct((t, d), F32), jax.ShapeDtypeStruct((t, 6 * dg), BF16),
                   jax.ShapeDtypeStruct((t, LANES), BF16), jax.ShapeDtypeStruct((nt, 8, d), F32)],
        in_specs=[pl.BlockSpec((dg, tm), lambda i: (0, i)), row(dg), row(dg)] * 2
        + [row(LANES), full(w_qkv), full(w_f), row(d), row(d), full(mod8), full(g_attn)],
        out_specs=[row(d), row(6 * dg), row(LANES), pl.BlockSpec((1, 8, d), lambda i: (i, 0, 0))],
        compiler_params=_params(("arbitrary",)),
    )(*dparts, dfl, w_qkv, w_f, x, dx2, mod8, g_attn)


def _matmul_tn(a, b, name):
    t, m = a.shape
    n = b.shape[1]
    a_t = a.T
    tm_ = _tile(m, (512, 256, 128))
    tn_ = _tile(n, (1024, 512, 256, 128))
    tk = _tile(t, (2048, 1024, 512, 256, 128))
    nk = t // tk

    def body(a_ref, b_ref, o_ref):
        k = pl.program_id(2)

        @pl.when(k == 0)
        def _():
            o_ref[...] = jnp.zeros_like(o_ref)

        o_ref[...] += jnp.dot(a_ref[...], b_ref[...], preferred_element_type=F32)

    return _pcall(
        body, name=name, grid=(m // tm_, n // tn_, nk),
        out_shape=jax.ShapeDtypeStruct((m, n), F32),
        in_specs=[pl.BlockSpec((tm_, tk), lambda i, j, k: (i, k)), pl.BlockSpec((tk, tn_), lambda i, j, k: (k, j))],
        out_specs=pl.BlockSpec((tm_, tn_), lambda i, j, k: (i, j)),
        compiler_params=_params(("arbitrary", "arbitrary", "arbitrary")),
    )(a_t, b)


HALO = 16


def _conv_taps(up_ext, cw, lo, rows):
    s1 = pltpu.roll(up_ext, 1, 0)
    s2 = pltpu.roll(up_ext, 2, 0)
    u = cw[2:3, :] * up_ext[lo:lo + rows] + cw[1:2, :] * s1[lo:lo + rows] + cw[0:1, :] * s2[lo:lo + rows] + cw[3:4, :]
    return u, s1, s2


def _ffn_fwd(x2, target, mod8, g_mlp, g_final, wg, wv, cwg, cwv, wd, tm, cf):
    t, d = x2.shape
    dfp = wg.shape[1]
    nt, nc = t // tm, dfp // cf
    hb = tm // HALO

    def body(x_ref, xp_ref, tg_ref, mod_ref, g_ref, gf_ref, wg_ref, wv_ref, cg_ref, cv_ref, wd_ref,
             dx3_ref, h2_ref, part_ref, h_sc, acc_sc):
        i, j = pl.program_id(0), pl.program_id(1)

        @pl.when(j == 0)
        def _():
            xe = jnp.concatenate([xp_ref[...], x_ref[...]], axis=0)
            h = _norm_mod(xe, g_ref[...], mod_ref[4:5, :], mod_ref[3:4, :]).astype(BF16)
            h_sc[...] = h
            h2_ref[...] = h[HALO:]
            acc_sc[...] = jnp.zeros_like(acc_sc)

        rowi = lax.broadcasted_iota(jnp.int32, (tm + HALO, 1), 0)
        keep = (rowi >= HALO) | (i > 0)
        hv = h_sc[...]
        upg = jnp.where(keep, jnp.dot(hv, wg_ref[...], preferred_element_type=F32), 0.0)
        upv = jnp.where(keep, jnp.dot(hv, wv_ref[...], preferred_element_type=F32), 0.0)
        ug, _, _ = _conv_taps(upg, cg_ref[...], HALO, tm)
        uv, _, _ = _conv_taps(upv, cv_ref[...], HALO, tm)
        act = (ug * jax.nn.sigmoid(ug) * uv).astype(BF16)
        acc_sc[...] += jnp.dot(act, wd_ref[...], preferred_element_type=F32)

        @pl.when(j == nc - 1)
        def _():
            y_ffn = acc_sc[...]
            x3 = x_ref[...] + mod_ref[5:6, :] * y_ffn
            r3 = lax.rsqrt(jnp.mean(x3 * x3, axis=-1, keepdims=True) + EPS)
            xn = x3 * r3
            gf = gf_ref[...]
            diff = xn * gf - tg_ref[...]
            dy = diff * (1.0 / d)
            dxn = dy * gf
            dx3 = r3 * (dxn - xn * jnp.mean(dxn * xn, axis=-1, keepdims=True))
            dx3_ref[...] = dx3
            part_ref[0] = jnp.zeros((8, d), F32)
            part_ref[0, 0:1, :] = jnp.sum(dy * xn, axis=0, keepdims=True)
            part_ref[0, 1:2, :] = jnp.sum(dx3 * y_ffn, axis=0, keepdims=True)
            part_ref[0, 2:3, :] = jnp.sum(diff * diff, axis=0, keepdims=True) * (0.5 / d)

    row = lambda w: pl.BlockSpec((tm, w), lambda i, j: (i, 0))
    full = lambda a: pl.BlockSpec(a.shape, lambda i, j: (0,) * a.ndim)
    return _pcall(
        body, name="ffn_fwd", grid=(nt, nc),
        out_shape=[jax.ShapeDtypeStruct((t, d), F32), jax.ShapeDtypeStruct((t, d), BF16),
                   jax.ShapeDtypeStruct((nt, 8, d), F32)],
        in_specs=[row(d), pl.BlockSpec((HALO, d), lambda i, j: (jnp.maximum(i * hb - 1, 0), 0)), row(d),
                  full(mod8), full(g_mlp), full(g_final),
                  pl.BlockSpec((d, cf), lambda i, j: (0, j)), pl.BlockSpec((d, cf), lambda i, j: (0, j)),
                  pl.BlockSpec((8, cf), lambda i, j: (0, j)), pl.BlockSpec((8, cf), lambda i, j: (0, j)),
                  pl.BlockSpec((cf, d), lambda i, j: (j, 0))],
        out_specs=[row(d), row(d), pl.BlockSpec((1, 8, d), lambda i, j: (i, 0, 0))],
        scratch_shapes=[pltpu.VMEM((tm + HALO, d), BF16), pltpu.VMEM((tm, d), F32)],
        compiler_params=_params(("arbitrary", "arbitrary")),
    )(x2, x2, target, mod8, g_mlp, g_final, wg, wv, cwg, cwv, wd)


def _chunk_major(w, cf):
    d, n = w.shape[0], w.shape[1] // cf
    return jnp.transpose(w.reshape(d, n, cf), (1, 0, 2))


def _ffn_fwd(x2, target, mod8, g_mlp, g_final, wg, wv, cwg, cwv, wd, tm, cf):
    t, d = x2.shape
    dfp = wg.shape[1]
    nt, nc = t // tm, dfp // cf
    hb = tm // HALO
    wg_c, wv_c = _chunk_major(wg, cf), _chunk_major(wv, cf)

    def body(x_ref, xp_ref, tg_ref, mod_ref, g_ref, gf_ref, wg_ref, wv_ref, cg_ref, cv_ref, wd_ref,
             dx3_ref, h2_ref, part_ref, act_sc):
        i = pl.program_id(0)
        xe = jnp.concatenate([xp_ref[...], x_ref[...]], axis=0)
        h = _norm_mod(xe, g_ref[...], mod_ref[4:5, :], mod_ref[3:4, :]).astype(BF16)
        h2_ref[...] = h[HALO:]
        first = jnp.where(i > 0, h[:HALO], jnp.zeros_like(h[:HALO]))
        h = jnp.concatenate([first, h[HALO:]], axis=0)

        def up(c):
            return (jnp.dot(h, wg_ref[c], preferred_element_type=F32), jnp.dot(h, wv_ref[c], preferred_element_type=F32))

        def activation(c, ups):
            cols = slice(c * cf, (c + 1) * cf)
            ug, _, _ = _conv_taps(ups[0], cg_ref[:, cols], HALO, tm)
            uv, _, _ = _conv_taps(ups[1], cv_ref[:, cols], HALO, tm)
            act_sc[:, cols] = (ug * jax.nn.sigmoid(ug) * uv).astype(BF16)

        for c0 in range(0, nc, 2):
            group = list(range(c0, min(c0 + 2, nc)))
            ups = [up(c) for c in group]
            for c, u in zip(group, ups):
                activation(c, u)

        y_ffn = jnp.dot(act_sc[...], wd_ref[...], preferred_element_type=F32)
        x3 = x_ref[...] + mod_ref[5:6, :] * y_ffn
        r3 = lax.rsqrt(jnp.mean(x3 * x3, axis=-1, keepdims=True) + EPS)
        xn = x3 * r3
        gf = gf_ref[...]
        diff = xn * gf - tg_ref[...]
        dy = diff * (1.0 / d)
        dxn = dy * gf
        dx3 = r3 * (dxn - xn * jnp.mean(dxn * xn, axis=-1, keepdims=True))
        dx3_ref[...] = dx3
        part_ref[0] = jnp.zeros((8, d), F32)
        part_ref[0, 0:1, :] = jnp.sum(dy * xn, axis=0, keepdims=True)
        part_ref[0, 1:2, :] = jnp.sum(dx3 * y_ffn, axis=0, keepdims=True)
        part_ref[0, 2:3, :] = jnp.sum(diff * diff, axis=0, keepdims=True) * (0.5 / d)

    row = lambda w: pl.BlockSpec((tm, w), lambda i: (i, 0))
    full = lambda a: pl.BlockSpec(a.shape, lambda i: (0,) * a.ndim)
    once = lambda a: pl.BlockSpec(a.shape, lambda i: (0,) * a.ndim, pipeline_mode=pl.Buffered(1))
    return _pcall(
        body, name="ffn_fwd", grid=(nt,),
        out_shape=[jax.ShapeDtypeStruct((t, d), F32), jax.ShapeDtypeStruct((t, d), BF16),
                   jax.ShapeDtypeStruct((nt, 8, d), F32)],
        in_specs=[row(d), pl.BlockSpec((HALO, d), lambda i: (jnp.maximum(i * hb - 1, 0), 0)), row(d),
                  full(mod8), full(g_mlp), full(g_final), once(wg_c), once(wv_c), once(cwg), once(cwv), once(wd)],
        out_specs=[row(d), row(d), pl.BlockSpec((1, 8, d), lambda i: (i, 0, 0))],
        scratch_shapes=[pltpu.VMEM((tm, dfp), BF16)],
        compiler_params=_params(("arbitrary",)),
    )(x2, x2, target, mod8, g_mlp, g_final, wg_c, wv_c, cwg, cwv, wd)


def _ffn_bwd(x2, dx3, mod8, g_mlp, wg, wv, cwg, cwv, wd, tm, cf):
    t, d = x2.shape
    dfp = wg.shape[1]
    nt, nc = t // tm, dfp // cf
    hb = tm // HALO
    nhb = t // HALO
    ext = tm + 2 * HALO

    def body(x_ref, xp_ref, xn_ref, dx_ref, dxn_ref, mod_ref, g_ref, wg_ref, wv_ref, cg_ref, cv_ref, wd_ref,
             dx2_ref, dug_ref, duv_ref, act_ref, dxg_ref, part_ref, pcg_ref, pcv_ref, h_sc, dg_sc, dh_sc):
        i, j = pl.program_id(0), pl.program_id(1)

        @pl.when(j == 0)
        def _():
            xe = jnp.concatenate([xp_ref[...], x_ref[...], xn_ref[...]], axis=0)
            h_sc[...] = _norm_mod(xe, g_ref[...], mod_ref[4:5, :], mod_ref[3:4, :]).astype(BF16)
            de = (jnp.concatenate([dx_ref[...], dxn_ref[...]], axis=0) * mod_ref[5:6, :]).astype(BF16)
            dg_sc[...] = de
            dxg_ref[...] = de[:tm]
            dh_sc[...] = jnp.zeros_like(dh_sc)

        rowe = lax.broadcasted_iota(jnp.int32, (ext, 1), 0)
        keep_up = (rowe >= HALO) | (i > 0)
        rowu = lax.broadcasted_iota(jnp.int32, (tm + HALO, 1), 0)
        keep_du = (rowu < tm) | (i < nt - 1)
        hv = h_sc[...]
        upg = jnp.where(keep_up, jnp.dot(hv, wg_ref[...], preferred_element_type=F32), 0.0)
        upv = jnp.where(keep_up, jnp.dot(hv, wv_ref[...], preferred_element_type=F32), 0.0)
        cg, cv = cg_ref[...], cv_ref[...]
        ug, g1, g2 = _conv_taps(upg, cg, HALO, tm + HALO)
        uv, v1, v2 = _conv_taps(upv, cv, HALO, tm + HALO)
        dact = lax.dot_general(dg_sc[...], wd_ref[...], NT_DIMS, preferred_element_type=F32)
        sg = jax.nn.sigmoid(ug)
        sil = ug * sg
        act_ref[...] = (sil * uv)[:tm].astype(BF16)
        duv = jnp.where(keep_du, dact * sil, 0.0)
        dug = jnp.where(keep_du, dact * uv * (sg * (1.0 + ug * (1.0 - sg))), 0.0)

        def back(du, cw, up, s1, s2, pc_ref):
            n = tm + HALO
            dup = (cw[2:3, :] * du + cw[1:2, :] * pltpu.roll(du, n - 1, 0) + cw[0:1, :] * pltpu.roll(du, n - 2, 0))[:tm]
            dut = du[:tm]
            pc_ref[0] = jnp.zeros((8, cf), F32)
            pc_ref[0, 0:1, :] = jnp.sum(dut * s2[HALO:HALO + tm], axis=0, keepdims=True)
            pc_ref[0, 1:2, :] = jnp.sum(dut * s1[HALO:HALO + tm], axis=0, keepdims=True)
            pc_ref[0, 2:3, :] = jnp.sum(dut * up[HALO:HALO + tm], axis=0, keepdims=True)
            pc_ref[0, 3:4, :] = jnp.sum(dut, axis=0, keepdims=True)
            return dup.astype(BF16)

        dupg = back(dug, cg, upg, g1, g2, pcg_ref)
        dupv = back(duv, cv, upv, v1, v2, pcv_ref)
        dug_ref[...] = dupg
        duv_ref[...] = dupv
        dh_sc[...] += (lax.dot_general(dupg, wg_ref[...], NT_DIMS, preferred_element_type=F32)
                       + lax.dot_general(dupv, wv_ref[...], NT_DIMS, preferred_element_type=F32))

        @pl.when(j == nc - 1)
        def _():
            dx, dshift, dscale, dgn = _norm_mod_bwd(x_ref[...], dh_sc[...], g_ref[...], mod_ref[4:5, :])
            dx2_ref[...] = dx_ref[...] + dx
            part_ref[0] = jnp.zeros((8, d), F32)
            part_ref[0, 0:1, :] = dshift
            part_ref[0, 1:2, :] = dscale
            part_ref[0, 2:3, :] = dgn

    row = lambda w: pl.BlockSpec((tm, w), lambda i, j: (i, 0))
    prev = pl.BlockSpec((HALO, d), lambda i, j: (jnp.maximum(i * hb - 1, 0), 0))
    nxt = pl.BlockSpec((HALO, d), lambda i, j: (jnp.minimum((i + 1) * hb, nhb - 1), 0))
    full = lambda a: pl.BlockSpec(a.shape, lambda i, j: (0,) * a.ndim)
    chunk = pl.BlockSpec((tm, cf), lambda i, j: (i, j))
    pchunk = pl.BlockSpec((1, 8, cf), lambda i, j: (i, 0, j))
    return _pcall(
        body, name="ffn_bwd", grid=(nt, nc),
        out_shape=[jax.ShapeDtypeStruct((t, d), F32), jax.ShapeDtypeStruct((t, dfp), BF16),
                   jax.ShapeDtypeStruct((t, dfp), BF16), jax.ShapeDtypeStruct((t, dfp), BF16),
                   jax.ShapeDtypeStruct((t, d), BF16), jax.ShapeDtypeStruct((nt, 8, d), F32),
                   jax.ShapeDtypeStruct((nt, 8, dfp), F32), jax.ShapeDtypeStruct((nt, 8, dfp), F32)],
        in_specs=[row(d), prev, nxt, row(d), nxt, full(mod8), full(g_mlp),
                  pl.BlockSpec((d, cf), lambda i, j: (0, j)), pl.BlockSpec((d, cf), lambda i, j: (0, j)),
                  pl.BlockSpec((8, cf), lambda i, j: (0, j)), pl.BlockSpec((8, cf), lambda i, j: (0, j)),
                  pl.BlockSpec((cf, d), lambda i, j: (j, 0))],
        out_specs=[row(d), chunk, chunk, chunk, row(d), pl.BlockSpec((1, 8, d), lambda i, j: (i, 0, 0)), pchunk, pchunk],
        scratch_shapes=[pltpu.VMEM((ext, d), BF16), pltpu.VMEM((tm + HALO, d), BF16), pltpu.VMEM((tm, d), F32)],
        compiler_params=_params(("arbitrary", "arbitrary")),
    )(x2, x2, x2, dx3, dx3, mod8, g_mlp, wg, wv, cwg, cwv, wd)


def _ffn_bwd(x2, dx3, mod8, g_mlp, wg, wv, cwg, cwv, wd, tm, cf):
    t, d = x2.shape
    dfp = wg.shape[1]
    nt, nc = t // tm, dfp // cf
    hb = tm // HALO
    nhb = t // HALO
    n = tm + HALO

    def body(x_ref, xp_ref, xn_ref, dx_ref, dxn_ref, mod_ref, g_ref, wg_ref, wv_ref, cg_ref, cv_ref, wd_ref,
             dx2_ref, dug_ref, duv_ref, act_ref, dxg_ref, part_ref, pcg_ref, pcv_ref):
        i = pl.program_id(0)
        xe = jnp.concatenate([xp_ref[...], x_ref[...], xn_ref[...]], axis=0)
        h = _norm_mod(xe, g_ref[...], mod_ref[4:5, :], mod_ref[3:4, :]).astype(BF16)
        h = jnp.concatenate([jnp.where(i > 0, h[:HALO], jnp.zeros_like(h[:HALO])), h[HALO:]], axis=0)
        dx = dx_ref[...] * mod_ref[5:6, :]
        dxn = jnp.where(i < nt - 1, dxn_ref[...] * mod_ref[5:6, :], 0.0)
        de = jnp.concatenate([dx, dxn], axis=0).astype(BF16)
        dxg_ref[...] = de[:tm]
        pcg_ref[0] = jnp.zeros((8, dfp), F32)
        pcv_ref[0] = jnp.zeros((8, dfp), F32)

        def products(c):
            cols = slice(c * cf, (c + 1) * cf)
            return (jnp.dot(h, wg_ref[:, cols], preferred_element_type=F32), jnp.dot(h, wv_ref[:, cols], preferred_element_type=F32),
                    lax.dot_general(de, wd_ref[cols, :], NT_DIMS, preferred_element_type=F32))

        def back(du, cw, up, s1, s2, pc_ref, cols):
            dup = (cw[2:3, :] * du + cw[1:2, :] * pltpu.roll(du, n - 1, 0) + cw[0:1, :] * pltpu.roll(du, n - 2, 0))[:tm]
            dut = du[:tm]
            pc_ref[0, 0:1, cols] = jnp.sum(dut * s2[HALO:HALO + tm], axis=0, keepdims=True)
            pc_ref[0, 1:2, cols] = jnp.sum(dut * s1[HALO:HALO + tm], axis=0, keepdims=True)
            pc_ref[0, 2:3, cols] = jnp.sum(dut * up[HALO:HALO + tm], axis=0, keepdims=True)
            pc_ref[0, 3:4, cols] = jnp.sum(dut, axis=0, keepdims=True)
            return dup.astype(BF16)

        def chunk(c, prods):
            cols = slice(c * cf, (c + 1) * cf)
            upg, upv, dact = prods
            cg, cv = cg_ref[:, cols], cv_ref[:, cols]
            ug, g1, g2 = _conv_taps(upg, cg, HALO, n)
            uv, v1, v2 = _conv_taps(upv, cv, HALO, n)
            sg = jax.nn.sigmoid(ug)
            sil = ug * sg
            act_ref[:, cols] = (sil * uv)[:tm].astype(BF16)
            dug_ref[:, cols] = back(dact * uv * (sg * (1.0 + ug * (1.0 - sg))), cg, upg, g1, g2, pcg_ref, cols)
            duv_ref[:, cols] = back(dact * sil, cv, upv, v1, v2, pcv_ref, cols)

        for c0 in range(0, nc, 2):
            group = list(range(c0, min(c0 + 2, nc)))
            prods = [products(c) for c in group]
            for c, pr in zip(group, prods):
                chunk(c, pr)

        dh = (lax.dot_general(dug_ref[...], wg_ref[...], NT_DIMS, preferred_element_type=F32)
              + lax.dot_general(duv_ref[...], wv_ref[...], NT_DIMS, preferred_element_type=F32))
        dxt, dshift, dscale, dgn = _norm_mod_bwd(x_ref[...], dh, g_ref[...], mod_ref[4:5, :])
        dx2_ref[...] = dx_ref[...] + dxt
        part_ref[0] = jnp.zeros((8, d), F32)
        part_ref[0, 0:1, :] = dshift
        part_ref[0, 1:2, :] = dscale
        part_ref[0, 2:3, :] = dgn

    row = lambda w: pl.BlockSpec((tm, w), lambda i: (i, 0))
    prev = pl.BlockSpec((HALO, d), lambda i: (jnp.maximum(i * hb - 1, 0), 0))
    nxt = pl.BlockSpec((HALO, d), lambda i: (jnp.minimum((i + 1) * hb, nhb - 1), 0))
    full = lambda a: pl.BlockSpec(a.shape, lambda i: (0,) * a.ndim)
    once = lambda a: pl.BlockSpec(a.shape, lambda i: (0,) * a.ndim, pipeline_mode=pl.Buffered(1))
    part = lambda w: pl.BlockSpec((1, 8, w), lambda i: (i, 0, 0))
    return _pcall(
        body, name="ffn_bwd", grid=(nt,),
        out_shape=[jax.ShapeDtypeStruct((t, d), F32), jax.ShapeDtypeStruct((t, dfp), BF16),
                   jax.ShapeDtypeStruct((t, dfp), BF16), jax.ShapeDtypeStruct((t, dfp), BF16),
                   jax.ShapeDtypeStruct((t, d), BF16), jax.ShapeDtypeStruct((nt, 8, d), F32),
                   jax.ShapeDtypeStruct((nt, 8, dfp), F32), jax.ShapeDtypeStruct((nt, 8, dfp), F32)],
        in_specs=[row(d), prev, nxt, row(d), nxt, full(mod8), full(g_mlp), once(wg), once(wv), once(cwg), once(cwv), once(wd)],
        out_specs=[row(d), row(dfp), row(dfp), row(dfp), row(d), part(d), part(dfp), part(dfp)],
        compiler_params=_params(("arbitrary",)),
    )(x2, x2, x2, dx3, dx3, mod8, g_mlp, wg, wv, cwg, cwv, wd)


def _head_masks():
    lane = lax.broadcasted_iota(jnp.int32, (1, LANES), 1)
    in_a = lane < HEAD_DIM
    return in_a, jnp.logical_not(in_a)


BLK = 2 * LANES


def _stack_heads(qkv, dg):
    t = qkv.shape[0]
    p = dg // LANES
    rows = _tile(t, (512, 256, 128))
    sub = rows // BLK

    def body(kf_ref, vf_ref, ks_ref, vs_ref, okf, ovf, oks, ovs):
        in_a, in_b = _head_masks()
        for src, dst in ((kf_ref, okf), (vf_ref, ovf), (ks_ref, oks), (vs_ref, ovs)):
            v = src[...]
            zero = jnp.zeros_like(v)
            va, vb = jnp.where(in_a, v, zero), jnp.where(in_b, v, zero)
            for s in range(sub):
                dst[0, s, :BLK, :] = va[s * BLK:(s + 1) * BLK]
                dst[0, s, BLK:, :] = vb[s * BLK:(s + 1) * BLK]

    col = lambda base: pl.BlockSpec((rows, LANES), lambda h, j: (j, base * p + h))
    out = pl.BlockSpec((1, sub, 2 * BLK, LANES), lambda h, j: (h, j, 0, 0))
    shape = jax.ShapeDtypeStruct((p, t // BLK, 2 * BLK, LANES), BF16)
    return _pcall(
        body, name="stack_heads", grid=(p, t // rows),
        out_shape=[shape] * 4, in_specs=[col(1), col(2), col(4), col(5)], out_specs=[out] * 4,
        compiler_params=_params(("arbitrary", "arbitrary")),
    )(qkv, qkv, qkv, qkv)


def _tile_masks():
    rowi = lax.broadcasted_iota(jnp.int32, (BLK, BLK), 0)
    coli = lax.broadcasted_iota(jnp.int32, (BLK, BLK), 1)
    return coli <= rowi, coli < rowi


def _pair_triangle(suffix):
    r = lax.broadcasted_iota(jnp.int32, (BLK, BLK), 0)
    c = lax.broadcasted_iota(jnp.int32, (BLK, BLK), 1)
    return ((r >= c) if suffix else (r <= c)).astype(BF16)


def _pair_cumsum(x2, tri, passes):
    return jnp.concatenate([_split_dot(x2[:, :BLK], tri, passes), _split_dot(x2[:, BLK:], tri, passes)], axis=1)


def _pair_specs(t, dg, base):
    p = dg // LANES
    q = pl.BlockSpec((BLK, LANES), lambda h, i: (i, base * p + h))
    kv = pl.BlockSpec((1, t // BLK, 2 * BLK, LANES), lambda h, i: (h, 0, 0, 0))
    return q, kv


def _fox_fwd(qkv, kst, vst, fcol, frow2, dg):
    t = qkv.shape[0]
    p, nq = dg // LANES, t // BLK
    nh = 2 * p

    def body(q_ref, k_ref, v_ref, ft_ref, fs_ref, o_ref, lse_ref):
        i = pl.program_id(1)
        in_a, _ = _head_masks()
        causal, _ = _tile_masks()
        q2 = q_ref[...]
        ft = tuple(jnp.broadcast_to(ft_ref[a], (BLK, BLK)) for a in range(2))

        def tile(j, carry, masked):
            m, l, acc = carry
            kb, vb = k_ref[0, j], v_ref[0, j]
            s2 = lax.dot_general(q2, kb, NT_DIMS, preferred_element_type=F32)
            fs = fs_ref[0, j]
            m_new, l_new, alpha, pr = [], [], [], []
            for a in range(2):
                sl = slice(a * BLK, (a + 1) * BLK)
                s = (s2[:, sl] + ft[a]) - fs[:, sl]
                if masked:
                    s = jnp.where(causal, s, NEG)
                mn = jnp.maximum(m[a], jnp.max(s, axis=1, keepdims=True))
                pa = jnp.exp(s - mn)
                al = jnp.exp(m[a] - mn)
                m_new.append(mn)
                alpha.append(al)
                l_new.append(al * l[a] + jnp.sum(pa, axis=1, keepdims=True))
                pr.append(pa.astype(BF16))
            acc = jnp.where(in_a, alpha[0], alpha[1]) * acc + jnp.dot(
                jnp.concatenate(pr, axis=1), vb, preferred_element_type=F32)
            return tuple(m_new), tuple(l_new), acc

        neg, zero = jnp.full((BLK, 1), NEG, F32), jnp.zeros((BLK, 1), F32)
        carry = lax.fori_loop(0, i, functools.partial(tile, masked=False), ((neg, neg), (zero, zero), jnp.zeros((BLK, LANES), F32)))
        m, l, acc = tile(i, carry, True)
        o_ref[...] = acc / jnp.where(in_a, l[0], l[1])
        lse_ref[0] = m[0] + jnp.log(l[0])
        lse_ref[1] = m[1] + jnp.log(l[1])

    qs, kv = _pair_specs(t, dg, 0)
    col = pl.BlockSpec((2, BLK, 1), lambda h, i: (h, i, 0))
    return _pcall(
        body, name="fox_fwd", grid=(p, nq),
        out_shape=[jax.ShapeDtypeStruct((t, dg), F32), jax.ShapeDtypeStruct((nh, t, 1), F32)],
        in_specs=[qs, kv, kv, col, pl.BlockSpec((1, nq, 1, 2 * BLK), lambda h, i: (h, 0, 0, 0))],
        out_specs=[pl.BlockSpec((BLK, LANES), lambda h, i: (i, h)), col],
        compiler_params=_params(("arbitrary", "arbitrary")),
    )(qkv, kst, vst, fcol, frow2)


def _fold_heads(stacked, in_a):
    return jnp.where(in_a, stacked[:BLK], stacked[BLK:])


def _fox_bwd(qkv, kst, vst, do, fcol, frow2, lse, delta, dg):
    t = qkv.shape[0]
    p, nq = dg // LANES, t // BLK
    nh = 2 * p

    def body(q_ref, k_ref, v_ref, do_ref, ft_ref, fs_ref, lse_ref, dl_ref, dq_ref, dk_ref, dv_ref, dfs_ref, dft_ref):
        i = pl.program_id(1)

        @pl.when(i == 0)
        def _():
            dk_ref[...] = jnp.zeros_like(dk_ref)
            dv_ref[...] = jnp.zeros_like(dv_ref)
            dfs_ref[...] = jnp.zeros_like(dfs_ref)

        in_a, _ = _head_masks()
        causal, _ = _tile_masks()
        q2, do2 = q_ref[...], do_ref[...]
        ft = tuple(jnp.broadcast_to(ft_ref[a] - lse_ref[a], (BLK, BLK)) for a in range(2))
        dl = tuple(jnp.broadcast_to(dl_ref[a], (BLK, BLK)) for a in range(2))

        def tile(j, carry, masked):
            dq, dft = carry
            kb, vb = k_ref[0, j], v_ref[0, j]
            s2 = lax.dot_general(q2, kb, NT_DIMS, preferred_element_type=F32)
            dp2 = lax.dot_general(do2, vb, NT_DIMS, preferred_element_type=F32)
            fs = fs_ref[0, j]
            pr, ds, dft_new = [], [], []
            for a in range(2):
                sl = slice(a * BLK, (a + 1) * BLK)
                s = (s2[:, sl] + ft[a]) - fs[:, sl]
                if masked:
                    s = jnp.where(causal, s, NEG)
                pa = jnp.exp(s)
                dsa = pa * (dp2[:, sl] - dl[a])
                pr.append(pa.astype(BF16))
                ds.append(dsa)
                dft_new.append(dft[a] + jnp.sum(dsa, axis=1, keepdims=True))
            ds2 = jnp.concatenate(ds, axis=1)
            dsb = ds2.astype(BF16)
            off = pl.multiple_of(j * BLK, BLK)
            dk_ref[pl.ds(off, BLK), :] += _fold_heads(lax.dot_general(dsb, q2, TN_DIMS, preferred_element_type=F32), in_a)
            dv_ref[pl.ds(off, BLK), :] += _fold_heads(
                lax.dot_general(jnp.concatenate(pr, axis=1), do2, TN_DIMS, preferred_element_type=F32), in_a)
            dfs_ref[0, j] += -jnp.sum(ds2, axis=0, keepdims=True)
            return dq + jnp.dot(dsb, kb, preferred_element_type=F32), tuple(dft_new)

        zero = jnp.zeros((BLK, 1), F32)
        carry = lax.fori_loop(0, i, functools.partial(tile, masked=False), (jnp.zeros((BLK, LANES), F32), (zero, zero)))
        dq, dft = tile(i, carry, True)
        dq_ref[...] = dq
        dft_ref[0] = dft[0]
        dft_ref[1] = dft[1]

    qs, kv = _pair_specs(t, dg, 0)
    col = pl.BlockSpec((2, BLK, 1), lambda h, i: (h, i, 0))
    rowspec = pl.BlockSpec((1, nq, 1, 2 * BLK), lambda h, i: (h, 0, 0, 0))
    blk = pl.BlockSpec((BLK, LANES), lambda h, i: (i, h))
    acc = pl.BlockSpec((t, LANES), lambda h, i: (0, h))
    return _pcall(
        body, name="fox_bwd", grid=(p, nq),
        out_shape=[jax.ShapeDtypeStruct((t, dg), F32)] * 3 + [jax.ShapeDtypeStruct((p, nq, 1, 2 * BLK), F32),
                                                              jax.ShapeDtypeStruct((nh, t, 1), F32)],
        in_specs=[qs, kv, kv, blk, col, rowspec, col, col],
        out_specs=[blk, acc, acc, rowspec, col],
        compiler_params=_params(("arbitrary", "arbitrary")),
    )(qkv, kst, vst, do, fcol, frow2, lse, delta)


def _softplus_parts(z):
    e = jnp.exp(-jnp.abs(z))
    return jnp.maximum(z, 0.0) + jnp.log(1.0 + e), e


def _sigmoid_from(z, e):
    d = 1.0 + e
    r = pl.reciprocal(d, approx=True)
    r = r * (2.0 - d * r)
    return jnp.where(z >= 0, 1.0, e) * r


def _sb_fwd(qkv, kst, vst, dg):
    t = qkv.shape[0]
    p, nq = dg // LANES, t // BLK
    nh = 2 * p

    def body(q_ref, k_ref, v_ref, o_ref, rt_ref):
        i = pl.program_id(1)
        _, strict = _tile_masks()
        strict2 = jnp.concatenate([strict, strict], axis=1)
        suffix = _pair_triangle(True)
        q2 = q_ref[...]

        def tile(j, carry, masked):
            rest, acc = carry
            kb, vb = k_ref[0, j], v_ref[0, j]
            z = lax.dot_general(q2, kb, NT_DIMS, preferred_element_type=F32)
            sp, _ = _softplus_parts(z)
            if masked:
                sp = jnp.where(strict2, sp, 0.0)
            cs = _pair_cumsum(sp, suffix, 2)
            w, rest_new = [], []
            for a in range(2):
                sl = slice(a * BLK, (a + 1) * BLK)
                wa = jnp.exp(z[:, sl] - cs[:, sl] - rest[a])
                if masked:
                    wa = jnp.where(strict, wa, 0.0)
                w.append(wa.astype(BF16))
                rest_new.append(rest[a] + cs[:, a * BLK:a * BLK + 1])
            acc = acc + jnp.dot(jnp.concatenate(w, axis=1), vb, preferred_element_type=F32)
            return tuple(rest_new), acc

        zero = jnp.zeros((BLK, 1), F32)
        carry = tile(i, ((zero, zero), jnp.zeros((BLK, LANES), F32)), True)
        rest, acc = lax.fori_loop(0, i, lambda jj, c: tile(i - 1 - jj, c, False), carry)
        o_ref[...] = acc
        rt_ref[0] = rest[0]
        rt_ref[1] = rest[1]

    qs, kv = _pair_specs(t, dg, 3)
    col = pl.BlockSpec((2, BLK, 1), lambda h, i: (h, i, 0))
    return _pcall(
        body, name="sb_fwd", grid=(p, nq),
        out_shape=[jax.ShapeDtypeStruct((t, dg), F32), jax.ShapeDtypeStruct((nh, t, 1), F32)],
        in_specs=[qs, kv, kv],
        out_specs=[pl.BlockSpec((BLK, LANES), lambda h, i: (i, h)), col],
        compiler_params=_params(("arbitrary", "arbitrary")),
    )(qkv, kst, vst)


def _sb_bwd(qkv, kst, vst, do, rtot, dg):
    t = qkv.shape[0]
    p, nq = dg // LANES, t // BLK

    def body(q_ref, k_ref, v_ref, do_ref, rt_ref, dq_ref, dk_ref, dv_ref):
        i = pl.program_id(1)

        @pl.when(i == 0)
        def _():
            dk_ref[...] = jnp.zeros_like(dk_ref)
            dv_ref[...] = jnp.zeros_like(dv_ref)

        in_a, _ = _head_masks()
        _, strict = _tile_masks()
        strict2 = jnp.concatenate([strict, strict], axis=1)
        prefix = _pair_triangle(False)
        q2, do2 = q_ref[...], do_ref[...]
        rt = (rt_ref[0], rt_ref[1])

        def tile(j, carry, masked):
            before, gbefore, dq = carry
            kb, vb = k_ref[0, j], v_ref[0, j]
            z = lax.dot_general(q2, kb, NT_DIMS, preferred_element_type=F32)
            da = lax.dot_general(do2, vb, NT_DIMS, preferred_element_type=F32)
            sp, e = _softplus_parts(z)
            sig = _sigmoid_from(z, e)
            if masked:
                sp = jnp.where(strict2, sp, 0.0)
            pre = _pair_cumsum(sp, prefix, 2)
            w = []
            for a in range(2):
                sl = slice(a * BLK, (a + 1) * BLK)
                wa = jnp.exp(z[:, sl] + (before[a] - rt[a]) + pre[:, sl] - sp[:, sl])
                if masked:
                    wa = jnp.where(strict, wa, 0.0)
                w.append(wa)
            w2 = jnp.concatenate(w, axis=1)
            g = w2 * da
            preg = _pair_cumsum(g, prefix, 1)
            dz = []
            for a in range(2):
                sl = slice(a * BLK, (a + 1) * BLK)
                dza = g[:, sl] * (1.0 - sig[:, sl]) - sig[:, sl] * (gbefore[a] + preg[:, sl] - g[:, sl])
                if masked:
                    dza = jnp.where(strict, dza, 0.0)
                dz.append(dza.astype(BF16))
            dzb = jnp.concatenate(dz, axis=1)
            off = pl.multiple_of(j * BLK, BLK)
            dk_ref[pl.ds(off, BLK), :] += _fold_heads(lax.dot_general(dzb, q2, TN_DIMS, preferred_element_type=F32), in_a)
            dv_ref[pl.ds(off, BLK), :] += _fold_heads(
                lax.dot_general(w2.astype(BF16), do2, TN_DIMS, preferred_element_type=F32), in_a)
            last = lambda x, a: x[:, (a + 1) * BLK - 1:(a + 1) * BLK]
            return (tuple(before[a] + last(pre, a) for a in range(2)),
                    tuple(gbefore[a] + last(preg, a) for a in range(2)),
                    dq + jnp.dot(dzb, kb, preferred_element_type=F32))

        zero = jnp.zeros((BLK, 1), F32)
        carry = lax.fori_loop(0, i, functools.partial(tile, masked=False), ((zero, zero), (zero, zero), jnp.zeros((BLK, LANES), F32)))
        dq_ref[...] = tile(i, carry, True)[2]

    qs, kv = _pair_specs(t, dg, 3)
    col = pl.BlockSpec((2, BLK, 1), lambda h, i: (h, i, 0))
    blk = pl.BlockSpec((BLK, LANES), lambda h, i: (i, h))
    acc = pl.BlockSpec((t, LANES), lambda h, i: (0, h))
    return _pcall(
        body, name="sb_bwd", grid=(p, nq),
        out_shape=[jax.ShapeDtypeStruct((t, dg), F32)] * 3,
        in_specs=[qs, kv, kv, blk, col],
        out_specs=[blk, acc, acc],
        compiler_params=_params(("arbitrary", "arbitrary")),
    )(qkv, kst, vst, do, rtot)


XROWS = 144
LANE_FS, LANE_FT_A, LANE_FT_B = 0, 3, 6


def _pieces3(x):
    hi = x.astype(BF16).astype(F32)
    r = x - hi
    mid = r.astype(BF16).astype(F32)
    return hi, mid, (r - mid).astype(BF16).astype(F32)


def _bias_lanes(rows, entries):
    sub = lax.broadcasted_iota(jnp.int32, (16, 1), 0)
    out = jnp.zeros((16, rows), F32)
    for l, v in entries:
        out = jnp.where(sub == l, v, out)
    return jnp.concatenate([out, jnp.zeros((LANES - 16, rows), F32)], axis=0).T


def _three(first, values):
    return [(first + k, v) for k, v in enumerate(values)]


def _stack_rows(x, in_a, in_b):
    zero = jnp.zeros_like(x)
    return jnp.concatenate([jnp.where(in_a, x, zero), jnp.where(in_b, x, zero)], axis=0)


def _transposed(x):
    return x.astype(F32).T.astype(BF16)


def _attn_operands(qkv, fcol, dg):
    t = qkv.shape[0]
    p, nk = dg // LANES, t // BLK

    def body(qf_ref, kf_ref, vf_ref, ks_ref, vs_ref, f_ref, qx_ref, kx_ref, kxt_ref, vf_o, vft_o, ks_o, kst_o, vs_o, vst_o):
        in_a, in_b = _head_masks()
        fa, fb = _pieces3(f_ref[0]), _pieces3(f_ref[1])
        qx_ref[0, :, :LANES] = qf_ref[...]
        qx_ref[0, :, LANES:] = _bias_lanes(
            BLK, _three(LANE_FS, (-1.0,) * 3) + _three(LANE_FT_A, fa) + _three(LANE_FT_B, fb)).astype(BF16)
        kf = kf_ref[...]
        zero = jnp.zeros_like(kf)
        top = jnp.concatenate([jnp.where(in_a, kf, zero), _bias_lanes(
            BLK, _three(LANE_FS, fa) + _three(LANE_FT_A, (1.0,) * 3)).astype(BF16)], axis=1)
        bot = jnp.concatenate([jnp.where(in_b, kf, zero), _bias_lanes(
            BLK, _three(LANE_FS, fb) + _three(LANE_FT_B, (1.0,) * 3)).astype(BF16)], axis=1)
        kx = jnp.concatenate([top, bot], axis=0)
        kx_ref[0, 0] = kx
        kxt_ref[0, 0] = _transposed(kx)[:XROWS]
        for src, dst, dst_t in ((vf_ref, vf_o, vft_o), (ks_ref, ks_o, kst_o), (vs_ref, vs_o, vst_o)):
            st = _stack_rows(src[...], in_a, in_b)
            dst[0, 0] = st
            dst_t[0, 0] = _transposed(st)

    col = lambda base: pl.BlockSpec((BLK, LANES), lambda h, j: (j, base * p + h))
    blk4 = lambda r, c: pl.BlockSpec((1, 1, r, c), lambda h, j: (h, j, 0, 0))
    shp4 = lambda r, c: jax.ShapeDtypeStruct((p, nk, r, c), BF16)
    return _pcall(
        body, name="attn_operands", grid=(p, nk),
        out_shape=[jax.ShapeDtypeStruct((p, t, 2 * LANES), BF16), shp4(2 * BLK, 2 * LANES), shp4(XROWS, 2 * BLK)]
        + [shp4(2 * BLK, LANES), shp4(LANES, 2 * BLK)] * 3,
        in_specs=[col(0), col(1), col(2), col(4), col(5), pl.BlockSpec((2, 1, BLK), lambda h, j: (h, 0, j))],
        out_specs=[pl.BlockSpec((1, BLK, 2 * LANES), lambda h, j: (h, j, 0)), blk4(2 * BLK, 2 * LANES), blk4(XROWS, 2 * BLK)]
        + [blk4(2 * BLK, LANES), blk4(LANES, 2 * BLK)] * 3,
        compiler_params=_params(("arbitrary", "arbitrary")),
    )(qkv, qkv, qkv, qkv, qkv, fcol)


def _fox_q_bwd(qkv, fcol, lse_col, dg):
    t = qkv.shape[0]
    p = dg // LANES

    def body(q_ref, f_ref, l_ref, qx_ref):
        fa, fb = _pieces3(f_ref[0] - l_ref[0]), _pieces3(f_ref[1] - l_ref[1])
        qx_ref[0, :, :LANES] = q_ref[...]
        qx_ref[0, :, LANES:] = _bias_lanes(
            BLK, _three(LANE_FS, (-1.0,) * 3) + _three(LANE_FT_A, fa) + _three(LANE_FT_B, fb)).astype(BF16)

    row = pl.BlockSpec((2, 1, BLK), lambda h, j: (h, 0, j))
    return _pcall(
        body, name="fox_q_bwd", grid=(p, t // BLK),
        out_shape=jax.ShapeDtypeStruct((p, t, 2 * LANES), BF16),
        in_specs=[pl.BlockSpec((BLK, LANES), lambda h, j: (j, h)), row, row],
        out_specs=pl.BlockSpec((1, BLK, 2 * LANES), lambda h, j: (h, j, 0)),
        compiler_params=_params(("arbitrary", "arbitrary")),
    )(qkv, fcol, lse_col)


def _key_query_masks():
    key = lax.broadcasted_iota(jnp.int32, (BLK, BLK), 0)
    qry = lax.broadcasted_iota(jnp.int32, (BLK, BLK), 1)
    return key <= qry, key < qry


def _key_triangle(kind):
    s = lax.broadcasted_iota(jnp.int32, (BLK, BLK), 0)
    j = lax.broadcasted_iota(jnp.int32, (BLK, BLK), 1)
    return {"suffix": j >= s, "prefix": j <= s, "before": j < s}[kind].astype(BF16)


def _tri_dot(tri, x, passes):
    acc = None
    for _ in range(passes):
        part = x.astype(BF16)
        d = jnp.dot(tri, part, preferred_element_type=F32)
        acc = d if acc is None else acc + d
        x = x - part.astype(F32)
    return acc


GROUPS = (4, 2, 1)


def _loop_blocks(n, tiles, carry, descending=False, groups=GROUPS):
    at = (lambda k: n - 1 - k) if descending else (lambda k: k)
    done = 0
    for g in groups:
        left = n - done
        carry = lax.fori_loop(0, left // g, lambda h, c, g=g, done=done: tiles([at(done + g * h + k) for k in range(g)], c), carry)
        done = done + (left // g) * g
    return carry


def _resident(shape):
    return pl.BlockSpec((1,) + shape, lambda h, i: (h,) + (0,) * len(shape), pipeline_mode=pl.Buffered(1))


def _rows_per_head(a, b):
    return jnp.concatenate([jnp.broadcast_to(a, (HEAD_DIM, BLK)), jnp.broadcast_to(b, (HEAD_DIM, BLK))], axis=0)


def _fold_heads(stacked, in_a):
    return jnp.where(in_a, stacked[:BLK], stacked[BLK:])


def _xy_gather_copies(ins, outs, send_sems, recv_sems, local_sems):
    x, y, c = lax.axis_index("x"), lax.axis_index("y"), lax.axis_index("c")
    chips = [(1 - x, y), (x, 1 - y), (1 - x, 1 - y)]
    mine = 2 * x + y
    local, remote = [], []
    for w in range(len(ins)):
        local.append(pltpu.make_async_copy(ins[w], outs[w].at[mine], local_sems.at[w]))
        for k, (px, py) in enumerate(chips):
            remote.append(pltpu.make_async_remote_copy(
                src_ref=ins[w], dst_ref=outs[w].at[mine], send_sem=send_sems.at[3 * w + k],
                recv_sem=recv_sems.at[3 * w + k], device_id=(px, py, c), device_id_type=MESH))
    return local, remote


def _fox_fwd(qx, kx, v_t, dg, shards):
    p, t = qx.shape[0], qx.shape[1]
    nq = t // BLK
    nh = 2 * p
    ns = len(shards)

    def body(q_ref, k_ref, vt_ref, *rest):
        shard_refs, (o_ref, lse_ref), gathered = rest[:ns], rest[ns:ns + 2], rest[ns + 2:2 * ns + 2]
        local, remote = _xy_gather_copies(shard_refs, gathered, *rest[2 * ns + 2:])
        i = pl.program_id(1)

        @pl.when((pl.program_id(0) == 0) & (i == 0))
        def _():
            for cp in local + remote:
                cp.start()

        causal, _ = _key_query_masks()
        q = q_ref[0]

        def scores(j, masked):
            s2 = lax.dot_general(k_ref[0, j], q, NT_DIMS, preferred_element_type=F32)
            s = [s2[a * BLK:(a + 1) * BLK] for a in range(2)]
            return [jnp.where(causal, x, NEG) for x in s] if masked else s

        def update(blocks, carry):
            m, l, acc = carry
            m_new, l_new, alpha = [], [], []
            pr = [[] for _ in blocks]
            for a in range(2):
                mn = m[a]
                for _, s in blocks:
                    mn = jnp.maximum(mn, jnp.max(s[a], axis=0, keepdims=True))
                al = jnp.exp(m[a] - mn)
                ln = al * l[a]
                for k, (_, s) in enumerate(blocks):
                    pa = jnp.exp(s[a] - mn)
                    ln = ln + jnp.sum(pa, axis=0, keepdims=True)
                    pr[k].append(pa.astype(BF16))
                m_new.append(mn)
                alpha.append(al)
                l_new.append(ln)
            acc = _rows_per_head(*alpha) * acc
            for k, (j, _) in enumerate(blocks):
                acc = acc + jnp.dot(vt_ref[0, j], jnp.concatenate(pr[k], axis=0), preferred_element_type=F32)
            return tuple(m_new), tuple(l_new), acc

        tiles = lambda js, c: update([(j, scores(j, False)) for j in js], c)
        neg, zero = jnp.full((1, BLK), NEG, F32), jnp.zeros((1, BLK), F32)
        carry = _loop_blocks(i, tiles, ((neg, neg), (zero, zero), jnp.zeros((LANES, BLK), F32)))
        m, l, acc = update([(i, scores(i, True))], carry)
        o_ref[...] = acc / _rows_per_head(*l)
        lse_ref[0] = m[0] + jnp.log(l[0])
        lse_ref[1] = m[1] + jnp.log(l[1])

        @pl.when((pl.program_id(0) == p - 1) & (i == nq - 1))
        def _():
            for cp in remote:
                cp.wait_recv()
            for cp in remote:
                cp.wait_send()
            for cp in local:
                cp.wait()

    row = pl.BlockSpec((2, 1, BLK), lambda h, i: (h, 0, i))
    hbm = pl.BlockSpec(memory_space=pltpu.HBM)
    return _pcall(
        body, name="fox_fwd", grid=(p, nq),
        out_shape=[jax.ShapeDtypeStruct((dg, t), F32), jax.ShapeDtypeStruct((nh, 1, t), F32)]
        + [jax.ShapeDtypeStruct((4,) + s.shape, s.dtype) for s in shards],
        in_specs=[pl.BlockSpec((1, BLK, 2 * LANES), lambda h, i: (h, i, 0)), _resident((nq, 2 * BLK, 2 * LANES)),
                  _resident((nq, LANES, 2 * BLK))] + [hbm] * ns,
        out_specs=[pl.BlockSpec((LANES, BLK), lambda h, i: (h, i)), row] + [hbm] * ns,
        scratch_shapes=[pltpu.SemaphoreType.DMA((3 * ns,)), pltpu.SemaphoreType.DMA((3 * ns,)), pltpu.SemaphoreType.DMA((ns,))],
        compiler_params=_params(("arbitrary", "arbitrary")),
    )(qx, kx, v_t, *shards)


def _fox_bwd(qxb, kx, kx_t, v_st, do, delta, dg):
    p, t = qxb.shape[0], qxb.shape[1]
    nq = t // BLK
    nh = 2 * p

    def body(q_ref, k_ref, kt_ref, v_ref, do_ref, dl_ref, dq_ref, dft_ref, dk_ref, dv_ref, dkx_ref):
        i = pl.program_id(1)

        @pl.when(i == 0)
        def _():
            dk_ref[...] = jnp.zeros_like(dk_ref)
            dv_ref[...] = jnp.zeros_like(dv_ref)
            dkx_ref[...] = jnp.zeros_like(dkx_ref)

        in_a, _ = _head_masks()
        first_lane = lax.broadcasted_iota(jnp.int32, (1, LANES), 1) == 0
        causal, _ = _key_query_masks()
        q, do2 = q_ref[0], do_ref[...]
        dl = (dl_ref[0], dl_ref[1])

        def products(j):
            return (lax.dot_general(k_ref[0, j], q, NT_DIMS, preferred_element_type=F32),
                    lax.dot_general(v_ref[0, j], do2, NT_DIMS, preferred_element_type=F32))

        def dscores(prod, masked):
            s2, dp2 = prod
            pr, ds = [], []
            for a in range(2):
                s = s2[a * BLK:(a + 1) * BLK]
                if masked:
                    s = jnp.where(causal, s, NEG)
                pa = jnp.exp(s)
                ds.append((pa * (dp2[a * BLK:(a + 1) * BLK] - dl[a])).astype(BF16))
                pr.append(pa.astype(BF16))
            return jnp.concatenate(ds, axis=0), jnp.concatenate(pr, axis=0)

        def accumulate(j, dsb, prb, dq):
            off = pl.multiple_of(j * BLK, BLK)
            dk_full = jnp.dot(dsb, q, preferred_element_type=F32)
            dk_ref[pl.ds(off, BLK), :] += _fold_heads(dk_full[:, :LANES], in_a)
            dkx_ref[pl.ds(off, BLK), :] += jnp.where(first_lane, dk_full[:BLK, LANES:], dk_full[BLK:, LANES:])
            dv_ref[pl.ds(off, BLK), :] += _fold_heads(jnp.dot(prb, do2, preferred_element_type=F32), in_a)
            return dq + jnp.dot(kt_ref[0, j], dsb, preferred_element_type=F32)

        def tiles(js, dq, masked=False):
            prods = [products(j) for j in js]
            grads = [dscores(pr, masked) for pr in prods]
            for j, (dsb, prb) in zip(js, grads):
                dq = accumulate(j, dsb, prb, dq)
            return dq

        dq = _loop_blocks(i, tiles, jnp.zeros((XROWS, BLK), F32))
        dq = tiles([i], dq, True)
        dq_ref[...] = dq[:LANES]
        dft_ref[0] = dq[LANES + LANE_FT_A:LANES + LANE_FT_A + 1]
        dft_ref[1] = dq[LANES + LANE_FT_B:LANES + LANE_FT_B + 1]

    row = pl.BlockSpec((2, 1, BLK), lambda h, i: (h, 0, i))
    acc = pl.BlockSpec((t, LANES), lambda h, i: (0, h))
    return _pcall(
        body, name="fox_bwd", grid=(p, nq),
        out_shape=[jax.ShapeDtypeStruct((dg, t), F32), jax.ShapeDtypeStruct((nh, 1, t), F32)] + [jax.ShapeDtypeStruct((t, dg), F32)] * 3,
        in_specs=[pl.BlockSpec((1, BLK, 2 * LANES), lambda h, i: (h, i, 0)), _resident((nq, 2 * BLK, 2 * LANES)),
                  _resident((nq, XROWS, 2 * BLK)), _resident((nq, 2 * BLK, LANES)),
                  pl.BlockSpec((BLK, LANES), lambda h, i: (i, h)), row],
        out_specs=[pl.BlockSpec((LANES, BLK), lambda h, i: (h, i)), row, acc, acc, acc],
        compiler_params=_params(("arbitrary", "arbitrary")),
    )(qxb, kx, kx_t, v_st, do, delta)


def _softplus_of(z):
    return jnp.maximum(z, 0.0) + jnp.log(1.0 + jnp.exp(-jnp.abs(z)))


def _sb_fwd(qkv, k_st, v_t, dg):
    t = qkv.shape[0]
    p, nq = dg // LANES, t // BLK
    nh = 2 * p

    def body(q_ref, k_ref, vt_ref, o_ref, rt_ref):
        i = pl.program_id(1)
        _, strict = _key_query_masks()
        suffix = _key_triangle("suffix")
        q = q_ref[...]

        def scores(j):
            z2 = lax.dot_general(k_ref[0, j], q, NT_DIMS, preferred_element_type=F32)
            return [z2[a * BLK:(a + 1) * BLK] for a in range(2)]

        def suffix_sums(z, masked):
            out = []
            for a in range(2):
                sp = _softplus_of(z[a])
                if masked:
                    sp = jnp.where(strict, sp, 0.0)
                out.append(_tri_dot(suffix, sp, 2))
            return out

        def weights(z, cs, rest, masked):
            w, rest_new = [], []
            for a in range(2):
                wa = jnp.exp(z[a] - cs[a] - rest[a])
                if masked:
                    wa = jnp.where(strict, wa, 0.0)
                w.append(wa.astype(BF16))
                rest_new.append(rest[a] + cs[a][0:1])
            return jnp.concatenate(w, axis=0), tuple(rest_new)

        def tiles(js, carry, masked=False):
            rest, acc = carry
            zs = [scores(j) for j in js]
            css = [suffix_sums(z, masked) for z in zs]
            ws = []
            for z, cs in zip(zs, css):
                w2, rest = weights(z, cs, rest, masked)
                ws.append(w2)
            for j, w2 in zip(js, ws):
                acc = acc + jnp.dot(vt_ref[0, j], w2, preferred_element_type=F32)
            return rest, acc

        zero = jnp.zeros((1, BLK), F32)
        carry = tiles([i], ((zero, zero), jnp.zeros((LANES, BLK), F32)), True)
        rest, acc = _loop_blocks(i, tiles, carry, descending=True)
        o_ref[...] = acc
        rt_ref[0] = rest[0]
        rt_ref[1] = rest[1]

    return _pcall(
        body, name="sb_fwd", grid=(p, nq),
        out_shape=[jax.ShapeDtypeStruct((dg, t), F32), jax.ShapeDtypeStruct((nh, 1, t), F32)],
        in_specs=[pl.BlockSpec((BLK, LANES), lambda h, i: (i, 3 * p + h)), _resident((nq, 2 * BLK, LANES)),
                  _resident((nq, LANES, 2 * BLK))],
        out_specs=[pl.BlockSpec((LANES, BLK), lambda h, i: (h, i)), pl.BlockSpec((2, 1, BLK), lambda h, i: (h, 0, i))],
        compiler_params=_params(("arbitrary", "arbitrary")),
    )(qkv, k_st, v_t)


def _scatter8_copies(ins, outs, send_sems, recv_sems, local_sems):
    x, y, c = lax.axis_index("x"), lax.axis_index("y"), lax.axis_index("c")
    me = 4 * x + 2 * y + c
    local, remote = [], []
    for w in range(len(ins)):
        local.append(pltpu.make_async_copy(ins[w].at[me], outs[w].at[me], local_sems.at[w]))
        for f in range(1, 8):
            px = 1 - x if f & 4 else x
            py = 1 - y if f & 2 else y
            pc = 1 - c if f & 1 else c
            remote.append(pltpu.make_async_remote_copy(
                src_ref=ins[w].at[4 * px + 2 * py + pc], dst_ref=outs[w].at[me],
                send_sem=send_sems.at[7 * w + f - 1], recv_sem=recv_sems.at[7 * w + f - 1],
                device_id=(px, py, pc), device_id_type=MESH))
    return local, remote


def _sb_bwd(qkv, k_st, k_t, v_st, do, rtot, dg, pieces):
    t = qkv.shape[0]
    p, nq = dg // LANES, t // BLK
    ns = len(pieces)

    def body(q_ref, k_ref, kt_ref, v_ref, do_ref, rt_ref, *rest):
        piece_refs, (dq_ref, dk_ref, dv_ref), recv_refs = rest[:ns], rest[ns:ns + 3], rest[ns + 3:2 * ns + 3]
        local, remote = _scatter8_copies(piece_refs, recv_refs, *rest[2 * ns + 3:])
        i = pl.program_id(1)

        @pl.when((pl.program_id(0) == 0) & (i == 0))
        def _():
            for cp in local + remote:
                cp.start()

        @pl.when(i == 0)
        def _():
            dk_ref[...] = jnp.zeros_like(dk_ref)
            dv_ref[...] = jnp.zeros_like(dv_ref)

        in_a, _ = _head_masks()
        _, strict = _key_query_masks()
        before_m, prefix_m = _key_triangle("before"), _key_triangle("prefix")
        q, do2 = q_ref[...], do_ref[...]
        rt = (rt_ref[0], rt_ref[1])

        def products(j):
            z2 = lax.dot_general(k_ref[0, j], q, NT_DIMS, preferred_element_type=F32)
            da2 = lax.dot_general(v_ref[0, j], do2, NT_DIMS, preferred_element_type=F32)
            return [z2[a * BLK:(a + 1) * BLK] for a in range(2)], [da2[a * BLK:(a + 1) * BLK] for a in range(2)]

        def softplus_sums(z, masked):
            sp = [_softplus_of(x) for x in z]
            if masked:
                sp = [jnp.where(strict, x, 0.0) for x in sp]
            return sp, [_tri_dot(before_m, x, 2) for x in sp]

        def weight_grads(z, da, sp, pre, before, masked):
            w, g, pg, before_new = [], [], [], []
            for a in range(2):
                wa = jnp.exp(z[a] + (before[a] - rt[a]) + pre[a])
                if masked:
                    wa = jnp.where(strict, wa, 0.0)
                ga = wa * da[a]
                w.append(wa.astype(BF16))
                g.append(ga)
                pg.append(jnp.dot(prefix_m, ga.astype(BF16), preferred_element_type=F32))
                before_new.append(before[a] + pre[a][BLK - 1:BLK] + sp[a][BLK - 1:BLK])
            return jnp.concatenate(w, axis=0), g, pg, tuple(before_new)

        def dlogits(sp, g, pg, gbefore, masked):
            dz, gbefore_new = [], []
            for a in range(2):
                s_incl = gbefore[a] + pg[a]
                dza = (g[a] - s_incl) + jnp.exp(-sp[a]) * s_incl
                if masked:
                    dza = jnp.where(strict, dza, 0.0)
                dz.append(dza.astype(BF16))
                gbefore_new.append(s_incl[BLK - 1:BLK])
            return jnp.concatenate(dz, axis=0), tuple(gbefore_new)

        def accumulate(j, dzb, wb, dq):
            off = pl.multiple_of(j * BLK, BLK)
            dk_ref[pl.ds(off, BLK), :] += _fold_heads(jnp.dot(dzb, q, preferred_element_type=F32), in_a)
            dv_ref[pl.ds(off, BLK), :] += _fold_heads(jnp.dot(wb, do2, preferred_element_type=F32), in_a)
            return dq + jnp.dot(kt_ref[0, j], dzb, preferred_element_type=F32)

        def tiles(js, carry, masked=False):
            before, gbefore, dq = carry
            prods = [products(j) for j in js]
            sums = [softplus_sums(z, masked) for z, _ in prods]
            grads = []
            for (z, da), (sp, pre) in zip(prods, sums):
                wb, g, pg, before = weight_grads(z, da, sp, pre, before, masked)
                grads.append((wb, g, pg))
            for j, (sp, _), (wb, g, pg) in zip(js, sums, grads):
                dzb, gbefore = dlogits(sp, g, pg, gbefore, masked)
                dq = accumulate(j, dzb, wb, dq)
            return before, gbefore, dq

        zero = jnp.zeros((1, BLK), F32)
        carry = _loop_blocks(i, tiles, ((zero, zero), (zero, zero), jnp.zeros((LANES, BLK), F32)), groups=(2, 1))
        dq_ref[...] = tiles([i], carry, True)[2]

        @pl.when((pl.program_id(0) == p - 1) & (i == nq - 1))
        def _():
            for cp in remote:
                cp.wait_recv()
            for cp in remote:
                cp.wait_send()
            for cp in local:
                cp.wait()

    acc = pl.BlockSpec((t, LANES), lambda h, i: (0, h))
    hbm = pl.BlockSpec(memory_space=pltpu.HBM)
    return _pcall(
        body, name="sb_bwd", grid=(p, nq),
        out_shape=[jax.ShapeDtypeStruct((dg, t), F32)] + [jax.ShapeDtypeStruct((t, dg), F32)] * 2
        + [jax.ShapeDtypeStruct(pc.shape, pc.dtype) for pc in pieces],
        in_specs=[pl.BlockSpec((BLK, LANES), lambda h, i: (i, 3 * p + h)), _resident((nq, 2 * BLK, LANES)),
                  _resident((nq, LANES, 2 * BLK)), _resident((nq, 2 * BLK, LANES)),
                  pl.BlockSpec((BLK, LANES), lambda h, i: (i, h)), pl.BlockSpec((2, 1, BLK), lambda h, i: (h, 0, i))] + [hbm] * ns,
        out_specs=[pl.BlockSpec((LANES, BLK), lambda h, i: (h, i)), acc, acc] + [hbm] * ns,
        scratch_shapes=[pltpu.SemaphoreType.DMA((7 * ns,)), pltpu.SemaphoreType.DMA((7 * ns,)), pltpu.SemaphoreType.DMA((ns,))],
        compiler_params=_params(("arbitrary", "arbitrary")),
    )(qkv, k_st, k_t, v_st, do, rtot, *pieces)


def _tri_constants(nh, t):
    nb = t // LANES
    r = nh * nb
    li = np.arange(LANES)
    tri_in = (li[:, None] <= li[None, :])
    ri = np.arange(r)
    same = (ri[:, None] // nb) == (ri[None, :] // nb)
    blk = same & (ri[None, :] < ri[:, None])
    blk_rev = same & (ri[None, :] > ri[:, None])
    head_rows = (np.arange(max(8, nh))[:, None] == (ri[None, :] // nb))
    as_bf16 = lambda a: jnp.asarray(a.astype(np.float32), BF16)
    return as_bf16(tri_in), as_bf16(blk), as_bf16(tri_in.T), as_bf16(blk_rev), as_bf16(head_rows)


def kernel(x, c, w_ada, b_ada, g_attn, w_in, b_fgate, g_out_fox, g_out_sb, w_out, g_mlp, w_up, conv_w, conv_b, w_down, g_final, loss_target, m_w_ada, m_b_ada, m_g_attn, m_w_in, m_b_fgate, m_g_out_fox, m_g_out_sb, m_w_out, m_g_mlp, m_w_up, m_conv_w, m_conv_b, m_w_down, m_g_final, v_w_ada, v_b_ada, v_g_attn, v_w_in, v_b_fgate, v_g_out_fox, v_g_out_sb, v_w_out, v_g_mlp, v_w_up, v_conv_w, v_conv_b, v_w_down, v_g_final):
    t, d = x.shape[1], x.shape[2]
    dg = d // 2
    nh = dg // HEAD_DIM
    n_in = 6 * dg + nh
    dff = w_down.shape[1] * 4
    dfp = -(-dff // 256) * 256
    cf = 256
    tm = _tile(t, (512, 256, 128))
    nq = t // BLK
    xi, yi, ci = lax.axis_index("x"), lax.axis_index("y"), lax.axis_index("c")
    shard = 2 * xi + yi
    me = 4 * xi + 2 * yi + ci

    x2d, tg2d = x[0], loss_target[0]

    c_all = _all_gather8(jnp.pad(c, ((0, 7), (0, 0)))).reshape(8, 8, d)[:, 0, :]
    ada_cols = w_ada.shape[2]
    b_shard = lax.dynamic_slice(b_ada, (0, shard * ada_cols), (1, ada_cols))
    sc_all, mod_shard = _ada_fwd(c_all, w_ada[0], b_shard)
    mod_all = _all_gather8(mod_shard).reshape(4, 2, 8, ada_cols)
    mod_me = lax.dynamic_index_in_dim(mod_all[:, 0], me, axis=1, keepdims=False)
    mod8 = jnp.pad(mod_me.reshape(6, d), ((0, 2), (0, 0)))

    lane_pad = lambda a: jnp.pad(a, ((0, 0),) * (a.ndim - 1) + ((0, -a.shape[-1] % LANES),))
    (g_in,) = _gather_xy([lane_pad(w_in[0].astype(BF16))])
    later_shards = [w_out[0].astype(BF16), lane_pad(w_up[0].astype(BF16)), w_down[0].astype(BF16), lane_pad(conv_w[0])]
    w_in_full = jnp.transpose(g_in[:, :, :n_in // 4], (1, 0, 2)).reshape(d, n_in)
    w_qkv = w_in_full[:, :6 * dg]
    w_f = jnp.pad(w_in_full[:, 6 * dg:], ((0, 0), (0, LANES - nh)))

    qkv, fl, h1 = _in_proj_fwd(x2d, mod8, g_attn, w_qkv, w_f, tm)
    tri_in, tri_blk, tri_in_rev, tri_blk_rev, head_rows = _tri_constants(nh, t)
    fl2d = fl[:, :nh].T.reshape(nh * t // LANES, LANES)
    b_rows = jnp.repeat(b_fgate[0], t // LANES)[:, None]
    f2d = _fgate_fwd(fl2d, b_rows, tri_in, tri_blk)
    fcol = f2d.reshape(nh, 1, t)
    pairs = nh // 2
    qx, kx, kx_t, vf_st, vf_t, ks_st, ks_t, vs_st, vs_t = _attn_operands(qkv, fcol, dg)
    o_fox_t, lse, g_out, g_up, g_down, g_cw = _fox_fwd(qx, kx, vf_t, dg, later_shards)
    g_up, g_cw = g_up[:, :, :dff // 2], g_cw[:, :, :dff // 2]
    w_out_full = g_out.reshape(2 * dg, d)
    w_up_full = jnp.transpose(g_up, (1, 0, 2)).reshape(d, 2 * dff)
    padc = ((0, 0), (0, dfp - dff))
    wg, wv = jnp.pad(w_up_full[:, :dff], padc), jnp.pad(w_up_full[:, dff:], padc)
    wd = jnp.pad(g_down.reshape(dff, d), ((0, dfp - dff), (0, 0)))
    cw_full = jnp.transpose(g_cw, (1, 0, 2)).reshape(3, 2 * dff)
    cw4 = jnp.concatenate([cw_full, conv_b], axis=0)
    cwg = jnp.pad(cw4[:, :dff], ((0, 4), (0, dfp - dff)))
    cwv = jnp.pad(cw4[:, dff:], ((0, 4), (0, dfp - dff)))
    o_sb_t, rtot = _sb_fwd(qkv, ks_st, vs_t, dg)
    o_fox, o_sb = o_fox_t, o_sb_t
    li = np.arange(dg)
    bd = jnp.asarray((li[:, None] // HEAD_DIM == li[None, :] // HEAD_DIM).astype(np.float32), BF16)
    hsel = jnp.asarray((li[:, None] // HEAD_DIM == np.arange(LANES)[None, :]).astype(np.float32), BF16)
    x2, mix = _attn_out_fwd(x2d, o_fox, o_sb, g_out_fox, g_out_sb, w_out_full, mod8, bd, tm)
    g_final2 = g_final[None, :]
    dx3, h2, part_f = _ffn_fwd(x2, tg2d, mod8, g_mlp, g_final2, wg, wv, cwg, cwv, wd, tm, cf)

    tm_ffn_bwd = _tile(t, (256, 128))
    dx2, dupg, dupv, act, dxg3, part_b, pcg, pcv = _ffn_bwd(x2, dx3, mod8, g_mlp, wg, wv, cwg, cwv, wd, tm_ffn_bwd, cf)
    do_fox, do_sb, delta, dxg2, part_o = _attn_out_bwd(dx2, mix, o_fox, o_sb, g_out_fox, g_out_sb, w_out_full, mod8, bd, hsel, tm)
    drow = delta[:, :nh].T.reshape(nh, 1, t)

    def col_pieces(g):
        r, cc = g.shape
        return jnp.transpose(g.reshape(2, r // 2, 4, cc // 4), (2, 0, 1, 3)).reshape(8, r // 2, cc // 4)

    def row_pieces(g):
        r, cc = g.shape
        return g.reshape(8, r // 8, cc)

    gw_out = _matmul_tn(mix, dxg2, "grad_w_out")
    gw_upg = _matmul_tn(h2, dupg, "grad_w_up_gate")
    gw_upv = _matmul_tn(h2, dupv, "grad_w_up_val")
    gw_up = jnp.concatenate([gw_upg[:, :dff], gw_upv[:, :dff]], axis=1)
    gw_down = _matmul_tn(act, dxg3, "grad_w_down")[:dff]
    early = (row_pieces(gw_out), lane_pad(col_pieces(gw_up)), row_pieces(gw_down))
    early = [_to_bf16(pc, "pieces_bf16_" + nm) for pc, nm in zip(early, ("w_out", "w_up", "w_down"))]

    qxb = _fox_q_bwd(qkv, fcol, lse, dg)
    dq_f_t, dft, dk_f, dv_f, dkx = _fox_bwd(qxb, kx, kx_t, vf_st, do_fox, drow, dg)
    dq_s_t, dk_s, dv_s, recv_out, recv_up, recv_down = _sb_bwd(qkv, ks_st, ks_t, vs_st, do_sb, rtot, dg, early)
    dq_f, dq_s = dq_f_t, dq_s_t
    f2d_shape = (nh * t // LANES, LANES)
    dfs = jnp.transpose(dkx.reshape(t, pairs, LANES)[:, :, :2], (1, 2, 0))
    dfl2d, gb8 = _fgate_bwd(fl2d, b_rows, dft.reshape(f2d_shape), dfs.reshape(f2d_shape), tri_in_rev, tri_blk_rev, head_rows)
    dfl = jnp.pad(dfl2d.reshape(nh, t).T, ((0, 0), (0, LANES - nh)))
    grad_x, dproj, dflb, part_i = _in_proj_bwd([dq_f, dk_f, dv_f, dq_s, dk_s, dv_s], dfl, w_qkv, w_f, x2d, dx2, mod8, g_attn, tm)

    gw_qkv = _matmul_tn(h1, dproj, "grad_w_qkv")
    gw_f = _matmul_tn(h1, dflb, "grad_w_f")
    gw_in = jnp.concatenate([gw_qkv, gw_f[:, :nh]], axis=1)

    sf = _sum_leading(part_f, "sum_part_ffn_fwd")
    sb_ = _sum_leading(part_b, "sum_part_ffn_bwd")
    so = _sum_leading(part_o, "sum_part_attn_out")
    si = _sum_leading(part_i, "sum_part_in_proj")
    scg = _sum_leading(pcg, "sum_part_conv_gate")
    scv = _sum_leading(pcv, "sum_part_conv_val")
    gb_f = gb8[:nh, 0]
    dmod = jnp.concatenate([si[0], si[1], so[0], sb_[0], sb_[1], sf[1]])
    g_conv_w = jnp.concatenate([scg[0:3, :dff], scv[0:3, :dff]], axis=1).reshape(-1)
    g_conv_b = jnp.concatenate([scg[3, :dff], scv[3, :dff]])
    loss_part = jnp.sum(sf[2])
    fields = [dmod, si[2], gb_f, so[1, :dg], so[1, dg:], sb_[2], g_conv_b, sf[0], g_conv_w, loss_part[None]]
    sizes = [int(f.shape[0]) for f in fields]
    n_pack = sum(sizes)
    lanes_pack = -(-n_pack // (8 * LANES)) * LANES
    pack = jnp.pad(jnp.concatenate(fields), (0, 8 * lanes_pack - n_pack)).reshape(8, lanes_pack)
    gathered = _all_gather8(pack)
    tot = _sum_leading(gathered.reshape(8, 8, lanes_pack), "sum_pack").reshape(-1)
    offs = np.concatenate([[0], np.cumsum(sizes)])
    take = lambda k: tot[int(offs[k]):int(offs[k + 1])]
    g_b_ada, g_g_attn, g_b_fgate, g_g_fox, g_g_sb, g_g_mlp, g_cb, g_g_final, g_cw_full, loss_v = [take(k) for k in range(10)]
    loss = loss_v[0]
    dmod_all = gathered.reshape(8, 8 * lanes_pack)[:, :6 * d]
    dmod_cols = lax.dynamic_slice(dmod_all, (0, shard * ada_cols), (8, ada_cols))
    g_w_ada = _ada_bwd(sc_all.T, dmod_cols)

    (recv_in,) = _scatter8([_to_bf16(lane_pad(col_pieces(gw_in)), "pieces_bf16_w_in")])
    recv = (recv_in, recv_out, recv_up, recv_down)
    halves = [_sum_leading(rv, nm) for rv, nm in zip(recv, ("sum_w_in", "sum_w_out", "sum_w_up", "sum_w_down"))]
    swapped = _swap_halves(halves)
    shard_cols = (n_in // 4, d, dff // 2, d)
    g_w_in, g_w_out, g_w_up, g_w_down = [s.reshape(2 * s.shape[1], s.shape[2])[:, :cc] for s, cc in zip(swapped, shard_cols)]
    g_conv_w_shard = lax.dynamic_slice(g_cw_full.reshape(3, 2 * dff), (0, shard * (dff // 2)), (3, dff // 2))

    grads, deltas, new_m, new_v = {}, {}, {}, {}

    def step(name, w, g, m, v):
        shape = w.shape
        as2d = lambda a: a.reshape(-1, shape[-1])
        dl, nm, nv = _adamw(as2d(w), as2d(g), as2d(m), as2d(v), "adamw_" + name)
        grads[name], deltas[name], new_m[name], new_v[name] = g.reshape(shape), dl.reshape(shape), nm.reshape(shape), nv.reshape(shape)

    step("w_ada", w_ada, g_w_ada, m_w_ada, v_w_ada)
    step("w_in", w_in, g_w_in, m_w_in, v_w_in)
    step("w_out", w_out, g_w_out, m_w_out, v_w_out)
    step("w_up", w_up, g_w_up, m_w_up, v_w_up)
    step("conv_w", conv_w, g_conv_w_shard, m_conv_w, v_conv_w)
    step("w_down", w_down, g_w_down, m_w_down, v_w_down)

    small = [("b_ada", b_ada, g_b_ada, m_b_ada, v_b_ada), ("g_attn", g_attn, g_g_attn, m_g_attn, v_g_attn),
             ("b_fgate", b_fgate, g_b_fgate, m_b_fgate, v_b_fgate), ("g_out_fox", g_out_fox, g_g_fox, m_g_out_fox, v_g_out_fox),
             ("g_out_sb", g_out_sb, g_g_sb, m_g_out_sb, v_g_out_sb), ("g_mlp", g_mlp, g_g_mlp, m_g_mlp, v_g_mlp),
             ("conv_b", conv_b, g_cb, m_conv_b, v_conv_b), ("g_final", g_final, g_g_final, m_g_final, v_g_final)]
    ssz = [int(np.prod(s[1].shape)) for s in small]
    n_small = sum(ssz)
    lanes_small = -(-n_small // (8 * LANES)) * LANES
    packs = [jnp.pad(jnp.concatenate([s[k].reshape(-1) for s in small]), (0, 8 * lanes_small - n_small)).reshape(8, lanes_small)
             for k in (1, 2, 3, 4)]
    dl_s, nm_s, nv_s = _adamw(*packs, "adamw_small")
    so_ = np.concatenate([[0], np.cumsum(ssz)])
    for k, s in enumerate(small):
        cut = lambda a: a.reshape(-1)[int(so_[k]):int(so_[k + 1])].reshape(s[1].shape)
        grads[s[0]], deltas[s[0]], new_m[s[0]], new_v[s[0]] = s[2].reshape(s[1].shape), cut(dl_s), cut(nm_s), cut(nv_s)

    order = ["w_ada", "b_ada", "g_attn", "w_in", "b_fgate", "g_out_fox", "g_out_sb", "w_out", "g_mlp", "w_up",
             "conv_w", "conv_b", "w_down", "g_final"]
    return (loss, grad_x[None], *[grads[n] for n in order], *[deltas[n] for n in order],
            *[new_m[n] for n in order], *[new_v[n] for n in order])
```

```python
import functools

import numpy as np
import jax
import jax.numpy as jnp
from jax import lax
from jax.experimental import pallas as pl
from jax.experimental.pallas import tpu as pltpu

F32 = jnp.float32
BF16 = jnp.bfloat16
MESH = pl.DeviceIdType.MESH

HEAD_DIM = 64
LANES = 128
EPS = 1e-6
NEG = -1e30
ADAM_LR, ADAM_B1, ADAM_B2, ADAM_EPS, ADAM_WD, ADAM_STEP = 0.001, 0.9, 0.999, 1e-08, 0.01, 10
V7X_VMEM_BYTES = 64 * 1024 * 1024
VMEM_LIMIT = V7X_VMEM_BYTES - 12 * 1024 * 1024
NT_DIMS = (((1,), (1,)), ((), ()))
TN_DIMS = (((0,), (0,)), ((), ()))


def _pcall(body, **kw):
    return pl.pallas_call(body, **kw)


def _params(sem=None, **kw):
    return pltpu.CompilerParams(dimension_semantics=sem, vmem_limit_bytes=VMEM_LIMIT, **kw)


def _split_dot(x, m, passes):
    acc = None
    for _ in range(passes):
        part = x.astype(BF16)
        d = jnp.dot(part, m, preferred_element_type=F32)
        acc = d if acc is None else acc + d
        x = x - part.astype(F32)
    return acc


def _tile(n, candidates):
    for t in candidates:
        if n % t == 0:
            return t
    return n


def _rows_tile(rows, row_bytes, budget=2 * 1024 * 1024):
    best = None
    for t in range(8, rows + 1, 8):
        if rows % t == 0 and t * row_bytes <= budget:
            best = t
    return best if best is not None else rows


def _all_gather8(v):
    m_per, n = v.shape

    def body(x_ref, out_ref, send_sems, recv_sems, local_sem):
        x, y, c = lax.axis_index("x"), lax.axis_index("y"), lax.axis_index("c")
        me, sibling = (x, y, c), (x, y, 1 - c)
        chips = [(1 - x, y), (x, 1 - y), (1 - x, 1 - y)]

        def rows(px, py, pc):
            return out_ref.at[pl.ds((4 * px + 2 * py + pc) * m_per, m_per), :]

        def copy(k, block, to, src=None):
            return pltpu.make_async_remote_copy(
                src_ref=rows(*block) if src is None else src, dst_ref=rows(*block),
                send_sem=send_sems.at[k], recv_sem=recv_sems.at[k], device_id=to, device_id_type=MESH)

        mine = pltpu.make_async_copy(x_ref, rows(*me), local_sem)
        mine.start()
        first = [copy(0, me, sibling, src=x_ref)]
        first += [copy(1 + j, me, (*chip, c), src=x_ref) for j, chip in enumerate(chips)]
        for cp in first:
            cp.start()
        passed = [copy(4 + j, (*chip, c), sibling) for j, chip in enumerate(chips)]
        for j, chip in enumerate(chips):
            copy(1 + j, (*chip, c), me).wait_recv()
            passed[j].start()
        copy(0, sibling, me).wait_recv()
        for j, chip in enumerate(chips):
            copy(4 + j, (*chip, 1 - c), me).wait_recv()
        for cp in first + passed:
            cp.wait_send()
        mine.wait()

    return _pcall(
        body, name="all_gather8",
        out_shape=jax.ShapeDtypeStruct((8 * m_per, n), v.dtype),
        in_specs=[pl.BlockSpec(memory_space=pltpu.VMEM)],
        out_specs=pl.BlockSpec(memory_space=pltpu.VMEM),
        scratch_shapes=[pltpu.SemaphoreType.DMA((7,)), pltpu.SemaphoreType.DMA((7,)), pltpu.SemaphoreType.DMA],
        compiler_params=pltpu.CompilerParams(vmem_limit_bytes=VMEM_LIMIT),
    )(v)


def _gather_xy(shards):
    n = len(shards)

    def body(*refs):
        ins, outs = refs[:n], refs[n:2 * n]
        send_sems, recv_sems, local_sems = refs[2 * n:]
        x, y, c = lax.axis_index("x"), lax.axis_index("y"), lax.axis_index("c")
        chips = [(1 - x, y), (x, 1 - y), (1 - x, 1 - y)]
        mine = 2 * x + y
        local, remote = [], []
        for w in range(n):
            cp = pltpu.make_async_copy(ins[w], outs[w].at[mine], local_sems.at[w])
            cp.start()
            local.append(cp)
            for k, (px, py) in enumerate(chips):
                cp = pltpu.make_async_remote_copy(
                    src_ref=ins[w], dst_ref=outs[w].at[mine], send_sem=send_sems.at[3 * w + k],
                    recv_sem=recv_sems.at[3 * w + k], device_id=(px, py, c), device_id_type=MESH)
                cp.start()
                remote.append(cp)
        for cp in remote:
            cp.wait_recv()
        for cp in remote:
            cp.wait_send()
        for cp in local:
            cp.wait()

    hbm = pl.BlockSpec(memory_space=pltpu.HBM)
    return _pcall(
        body, name="gather_xy",
        out_shape=[jax.ShapeDtypeStruct((4,) + s.shape, s.dtype) for s in shards],
        in_specs=[hbm] * n, out_specs=[hbm] * n,
        scratch_shapes=[pltpu.SemaphoreType.DMA((3 * n,)), pltpu.SemaphoreType.DMA((3 * n,)),
                        pltpu.SemaphoreType.DMA((n,))],
        compiler_params=pltpu.CompilerParams(vmem_limit_bytes=VMEM_LIMIT),
    )(*shards)


def _scatter8(pieces):
    n = len(pieces)

    def body(*refs):
        ins, outs = refs[:n], refs[n:2 * n]
        send_sems, recv_sems, local_sems = refs[2 * n:]
        x, y, c = lax.axis_index("x"), lax.axis_index("y"), lax.axis_index("c")
        me = 4 * x + 2 * y + c
        local, remote = [], []
        for w in range(n):
            cp = pltpu.make_async_copy(ins[w].at[me], outs[w].at[me], local_sems.at[w])
            cp.start()
            local.append(cp)
            for f in range(1, 8):
                px = 1 - x if f & 4 else x
                py = 1 - y if f & 2 else y
                pc = 1 - c if f & 1 else c
                cp = pltpu.make_async_remote_copy(
                    src_ref=ins[w].at[4 * px + 2 * py + pc], dst_ref=outs[w].at[me],
                    send_sem=send_sems.at[7 * w + f - 1], recv_sem=recv_sems.at[7 * w + f - 1],
                    device_id=(px, py, pc), device_id_type=MESH)
                cp.start()
                remote.append(cp)
        for cp in remote:
            cp.wait_recv()
        for cp in remote:
            cp.wait_send()
        for cp in local:
            cp.wait()

    hbm = pl.BlockSpec(memory_space=pltpu.HBM)
    return _pcall(
        body, name="scatter8",
        out_shape=[jax.ShapeDtypeStruct(p.shape, p.dtype) for p in pieces],
        in_specs=[hbm] * n, out_specs=[hbm] * n,
        scratch_shapes=[pltpu.SemaphoreType.DMA((7 * n,)), pltpu.SemaphoreType.DMA((7 * n,)),
                        pltpu.SemaphoreType.DMA((n,))],
        compiler_params=pltpu.CompilerParams(vmem_limit_bytes=VMEM_LIMIT),
    )(*pieces)


def _swap_halves(halves):
    n = len(halves)
    chunks = 8
    n_chunks = [max(k for k in (chunks, 4, 2, 1) if h.shape[0] % (8 * k) == 0) for h in halves]

    def body(*refs):
        ins, outs = refs[:n], refs[n:2 * n]
        send_sems, recv_sems, local_sems = refs[2 * n:]
        x, y, c = lax.axis_index("x"), lax.axis_index("y"), lax.axis_index("c")
        local, remote = [], []
        for w in range(n):
            cp = pltpu.make_async_copy(ins[w], outs[w].at[c], local_sems.at[w])
            cp.start()
            local.append(cp)
            rows = ins[w].shape[0] // n_chunks[w]
            for k in range(n_chunks[w]):
                cp = pltpu.make_async_remote_copy(
                    src_ref=ins[w].at[pl.ds(k * rows, rows)], dst_ref=outs[w].at[c, pl.ds(k * rows, rows)],
                    send_sem=send_sems.at[chunks * w + k], recv_sem=recv_sems.at[chunks * w + k],
                    device_id=(x, y, 1 - c), device_id_type=MESH)
                cp.start()
                remote.append(cp)
        for cp in remote:
            cp.wait_recv()
        for cp in remote:
            cp.wait_send()
        for cp in local:
            cp.wait()

    hbm = pl.BlockSpec(memory_space=pltpu.HBM)
    return _pcall(
        body, name="swap_halves",
        out_shape=[jax.ShapeDtypeStruct((2,) + h.shape, h.dtype) for h in halves],
        in_specs=[hbm] * n, out_specs=[hbm] * n,
        scratch_shapes=[pltpu.SemaphoreType.DMA((chunks * n,)), pltpu.SemaphoreType.DMA((chunks * n,)),
                        pltpu.SemaphoreType.DMA((n,))],
        compiler_params=pltpu.CompilerParams(vmem_limit_bytes=VMEM_LIMIT),
    )(*halves)


def _sum_leading(a, name):
    n, r, c = a.shape
    tr = _rows_tile(r, n * c * 4, budget=6 * 1024 * 1024)
    if a.dtype == BF16 and tr % 16:
        tr = r

    def body(a_ref, o_ref):
        acc = a_ref[0].astype(F32)
        for k in range(1, n):
            acc = acc + a_ref[k].astype(F32)
        o_ref[...] = acc

    return _pcall(
        body, name=name, grid=(r // tr,),
        out_shape=jax.ShapeDtypeStruct((r, c), F32),
        in_specs=[pl.BlockSpec((n, tr, c), lambda i: (0, i, 0))],
        out_specs=pl.BlockSpec((tr, c), lambda i: (i, 0)),
        compiler_params=_params(("arbitrary",)),
    )(a)


def _to_bf16(a, name):
    n, r, c = a.shape

    def body(a_ref, o_ref):
        o_ref[...] = a_ref[...].astype(BF16)

    spec = pl.BlockSpec((1, r, c), lambda i: (i, 0, 0))
    return _pcall(
        body, name=name, grid=(n,), out_shape=jax.ShapeDtypeStruct(a.shape, BF16),
        in_specs=[spec], out_specs=spec, compiler_params=_params(("arbitrary",)),
    )(a)


def _adamw(w, g, m, v, name):
    r, c = w.shape
    tr = _rows_tile(r, c * 4, budget=1024 * 1024)
    c1 = 1.0 - ADAM_B1 ** ADAM_STEP
    c2 = 1.0 - ADAM_B2 ** ADAM_STEP

    def body(w_ref, g_ref, m_ref, v_ref, d_ref, nm_ref, nv_ref):
        gg = g_ref[...]
        nm = ADAM_B1 * m_ref[...] + (1.0 - ADAM_B1) * gg
        nv = ADAM_B2 * v_ref[...] + (1.0 - ADAM_B2) * (gg * gg)
        m_hat = nm / c1
        v_hat = nv / c2
        d_ref[...] = -ADAM_LR * (m_hat / (jnp.sqrt(v_hat) + ADAM_EPS) + ADAM_WD * w_ref[...])
        nm_ref[...] = nm
        nv_ref[...] = nv

    spec = pl.BlockSpec((tr, c), lambda i: (i, 0))
    return _pcall(
        body, name=name, grid=(r // tr,),
        out_shape=[jax.ShapeDtypeStruct((r, c), F32)] * 3,
        in_specs=[spec] * 4, out_specs=[spec] * 3,
        compiler_params=_params(("arbitrary",)),
    )(w, g, m, v)


def _ada_fwd(c_all, w_shard, b_shard):
    nb, d = c_all.shape
    cols = w_shard.shape[1]

    def body(c_ref, w_ref, b_ref, sc_ref, mod_ref):
        cv = c_ref[...]
        sc = cv * jax.nn.sigmoid(cv)
        sc_ref[...] = sc
        mod_ref[...] = jnp.dot(sc.astype(BF16), w_ref[...].astype(BF16), preferred_element_type=F32) + b_ref[...]

    return _pcall(
        body, name="ada_fwd",
        out_shape=[jax.ShapeDtypeStruct((nb, d), F32), jax.ShapeDtypeStruct((nb, cols), F32)],
        compiler_params=pltpu.CompilerParams(vmem_limit_bytes=VMEM_LIMIT),
    )(c_all, w_shard, b_shard)


def _ada_bwd(sc_t, dmod_cols):
    d, nb = sc_t.shape
    cols = dmod_cols.shape[1]
    tr = _rows_tile(d, cols * 4, budget=1024 * 1024)

    def body(s_ref, m_ref, o_ref):
        s = s_ref[...]
        m = m_ref[...]
        acc = s[:, 0:1] * m[0:1, :]
        for b in range(1, nb):
            acc = acc + s[:, b:b + 1] * m[b:b + 1, :]
        o_ref[...] = acc

    return _pcall(
        body, name="ada_bwd", grid=(d // tr,),
        out_shape=jax.ShapeDtypeStruct((d, cols), F32),
        in_specs=[pl.BlockSpec((tr, nb), lambda i: (i, 0)), pl.BlockSpec((nb, cols), lambda i: (0, 0))],
        out_specs=pl.BlockSpec((tr, cols), lambda i: (i, 0)),
        compiler_params=_params(("arbitrary",)),
    )(sc_t, dmod_cols)


def _log_sigmoid(x):
    return jnp.minimum(x, 0.0) - jnp.log1p(jnp.exp(-jnp.abs(x)))


def _fgate_fwd(fl2d, b_rows, tri_in, tri_blk):
    r = fl2d.shape[0]

    def body(x_ref, b_ref, u_ref, l_ref, f_ref):
        lf = _log_sigmoid(x_ref[...] + b_ref[...])
        c1 = _split_dot(lf, u_ref[...], 3)
        tot = jnp.broadcast_to(c1[:, LANES - 1:LANES], (r, LANES))
        acc = None
        for _ in range(3):
            part = tot.astype(BF16)
            dd = jnp.dot(l_ref[...], part, preferred_element_type=F32)
            acc = dd if acc is None else acc + dd
            tot = tot - part.astype(F32)
        f_ref[...] = c1 + acc

    return _pcall(
        body, name="fgate_fwd", out_shape=jax.ShapeDtypeStruct((r, LANES), F32),
        compiler_params=pltpu.CompilerParams(vmem_limit_bytes=VMEM_LIMIT),
    )(fl2d, b_rows, tri_in, tri_blk)


def _fgate_bwd(fl2d, b_rows, df_query, df_key, tri_in_rev, tri_blk_rev, head_rows):
    r = fl2d.shape[0]
    nhp = head_rows.shape[0]

    def body(x_ref, b_ref, dq_ref, dk_ref, u_ref, l_ref, hr_ref, o_ref, gb_ref):
        c1 = _split_dot(dq_ref[...] + dk_ref[...], u_ref[...], 3)
        tot = jnp.broadcast_to(c1[:, 0:1], (r, LANES))
        acc = None
        for _ in range(3):
            part = tot.astype(BF16)
            dd = jnp.dot(l_ref[...], part, preferred_element_type=F32)
            acc = dd if acc is None else acc + dd
            tot = tot - part.astype(F32)
        x = x_ref[...] + b_ref[...]
        e = jnp.exp(-jnp.abs(x))
        dfl = (c1 + acc) * (jnp.where(x >= 0, e, 1.0) / (1.0 + e))
        o_ref[...] = dfl
        rs = jnp.broadcast_to(jnp.sum(dfl, axis=1, keepdims=True), (r, LANES))
        gb = None
        for _ in range(3):
            part = rs.astype(BF16)
            dd = jnp.dot(hr_ref[...], part, preferred_element_type=F32)
            gb = dd if gb is None else gb + dd
            rs = rs - part.astype(F32)
        gb_ref[...] = gb

    return _pcall(
        body, name="fgate_bwd",
        out_shape=[jax.ShapeDtypeStruct((r, LANES), F32), jax.ShapeDtypeStruct((nhp, LANES), F32)],
        compiler_params=pltpu.CompilerParams(vmem_limit_bytes=VMEM_LIMIT),
    )(fl2d, b_rows, df_query, df_key, tri_in_rev, tri_blk_rev, head_rows)


def _norm_mod(x, g, scale, shift):
    r = lax.rsqrt(jnp.mean(x * x, axis=-1, keepdims=True) + EPS)
    return (x * r * g) * (1.0 + scale) + shift


def _norm_mod_bwd(x, dh, g, scale):
    r = lax.rsqrt(jnp.mean(x * x, axis=-1, keepdims=True) + EPS)
    xn = x * r
    dshift = jnp.sum(dh, axis=0, keepdims=True)
    dscale = jnp.sum(dh * (xn * g), axis=0, keepdims=True)
    dxn_g = dh * (1.0 + scale)
    dg = jnp.sum(dxn_g * xn, axis=0, keepdims=True)
    dxn = dxn_g * g
    dx = r * (dxn - xn * jnp.mean(dxn * xn, axis=-1, keepdims=True))
    return dx, dshift, dscale, dg


def _in_proj_fwd(x, mod8, g_attn, w_qkv, w_f, tm):
    t, d = x.shape
    dg = w_qkv.shape[1] // 6

    def body(x_ref, mod_ref, g_ref, w_ref, wf_ref, qkv_ref, fl_ref, h1_ref, h_sc):
        j = pl.program_id(1)

        @pl.when(j == 0)
        def _():
            h = _norm_mod(x_ref[...], g_ref[...], mod_ref[1:2, :], mod_ref[0:1, :]).astype(BF16)
            h_sc[...] = h
            h1_ref[...] = _transposed(h)
            fl_ref[...] = jnp.dot(h, wf_ref[...], preferred_element_type=F32)

        s = jnp.where((j == 0) | (j == 3), HEAD_DIM ** -0.5, 1.0)
        qkv_ref[...] = (jnp.dot(h_sc[...], w_ref[...], preferred_element_type=F32) * s).astype(BF16)

    return _pcall(
        body, name="in_proj_fwd", grid=(t // tm, 6),
        out_shape=[jax.ShapeDtypeStruct((t, 6 * dg), BF16), jax.ShapeDtypeStruct((t, LANES), F32),
                   jax.ShapeDtypeStruct((d, t), BF16)],
        in_specs=[pl.BlockSpec((tm, d), lambda i, j: (i, 0)), pl.BlockSpec((8, d), lambda i, j: (0, 0)),
                  pl.BlockSpec((1, d), lambda i, j: (0, 0)), pl.BlockSpec((d, dg), lambda i, j: (0, j)),
                  pl.BlockSpec((d, LANES), lambda i, j: (0, 0))],
        out_specs=[pl.BlockSpec((tm, dg), lambda i, j: (i, j)), pl.BlockSpec((tm, LANES), lambda i, j: (i, 0)),
                   pl.BlockSpec((d, tm), lambda i, j: (0, i))],
        scratch_shapes=[pltpu.VMEM((tm, d), BF16)],
        compiler_params=_params(("arbitrary", "arbitrary")),
    )(x, mod8, g_attn, w_qkv, w_f)


def _head_rstd(o, bd):
    return lax.rsqrt(_split_dot(o * o, bd, 3) * (1.0 / HEAD_DIM) + EPS)


def _attn_out_fwd(x, o_fox, o_sb, g_fox, g_sb, w_out, mod8, bd, tm):
    t, d = x.shape
    dg = o_fox.shape[0]

    def body(x_ref, of_ref, os_ref, gf_ref, gs_ref, w_ref, mod_ref, bd_ref, x2_ref, mix_ref, mixt_ref):
        of, osb = of_ref[...].T, os_ref[...].T
        mf = (of * _head_rstd(of, bd_ref[...]) * gf_ref[...]).astype(BF16)
        ms = (osb * _head_rstd(osb, bd_ref[...]) * gs_ref[...]).astype(BF16)
        mix_ref[:, :dg] = mf
        mix_ref[:, dg:] = ms
        mixt_ref[:dg, :] = _transposed(mf)
        mixt_ref[dg:, :] = _transposed(ms)
        y = jnp.dot(mf, w_ref[:dg, :], preferred_element_type=F32) + jnp.dot(ms, w_ref[dg:, :], preferred_element_type=F32)
        x2_ref[...] = x_ref[...] + mod_ref[2:3, :] * y

    row = lambda w: pl.BlockSpec((tm, w), lambda i: (i, 0))
    full = lambda a: pl.BlockSpec(a.shape, lambda i: (0,) * a.ndim)
    return _pcall(
        body, name="attn_out_fwd", grid=(t // tm,),
        out_shape=[jax.ShapeDtypeStruct((t, d), F32), jax.ShapeDtypeStruct((t, 2 * dg), BF16),
                   jax.ShapeDtypeStruct((2 * dg, t), BF16)],
        in_specs=[row(d), pl.BlockSpec((dg, tm), lambda i: (0, i)), pl.BlockSpec((dg, tm), lambda i: (0, i)),
                  full(g_fox), full(g_sb), full(w_out), full(mod8), full(bd)],
        out_specs=[row(d), row(2 * dg), pl.BlockSpec((2 * dg, tm), lambda i: (0, i))],
        compiler_params=_params(("arbitrary",)),
    )(x, o_fox, o_sb, g_fox, g_sb, w_out, mod8, bd)


def _attn_out_bwd(dx2, mix, o_fox, o_sb, g_fox, g_sb, w_out, mod8, bd, hsel, tm):
    t, d = dx2.shape
    dg = o_fox.shape[0]

    def body(dx_ref, mix_ref, of_ref, os_ref, gf_ref, gs_ref, w_ref, mod_ref, bd_ref, hs_ref,
             dof_ref, dos_ref, dlt_ref, dxg_ref, part_ref):
        dx = dx_ref[...]
        gate = mod_ref[2:3, :]
        dxg = (dx * gate).astype(BF16)
        dxg_ref[...] = dxg
        mixv = mix_ref[...]
        y = jnp.dot(mixv[:, :dg], w_ref[:dg, :], preferred_element_type=F32)
        y = y + jnp.dot(mixv[:, dg:], w_ref[dg:, :], preferred_element_type=F32)
        part_ref[0] = jnp.zeros((8, d), F32)
        part_ref[0, 0:1, :] = jnp.sum(dx * y, axis=0, keepdims=True)
        for grp, (o_ref, g_ref, do_ref) in enumerate(((of_ref, gf_ref, dof_ref), (os_ref, gs_ref, dos_ref))):
            dmix = lax.dot_general(dxg, w_ref[grp * dg:(grp + 1) * dg, :], NT_DIMS, preferred_element_type=F32)
            o = o_ref[...].T
            r = _head_rstd(o, bd_ref[...])
            n = o * r
            part_ref[0, 1:2, grp * dg:(grp + 1) * dg] = jnp.sum(dmix * n, axis=0, keepdims=True)
            dn = dmix * g_ref[...]
            mh = _split_dot(dn * n, bd_ref[...], 3) * (1.0 / HEAD_DIM)
            do = r * (dn - n * mh)
            do_ref[...] = do.astype(BF16)
            if grp == 0:
                dlt_ref[...] = _split_dot(do * o, hs_ref[...], 3)

    row = lambda w: pl.BlockSpec((tm, w), lambda i: (i, 0))
    full = lambda a: pl.BlockSpec(a.shape, lambda i: (0,) * a.ndim)
    nt = t // tm
    return _pcall(
        body, name="attn_out_bwd", grid=(nt,),
        out_shape=[jax.ShapeDtypeStruct((t, dg), BF16), jax.ShapeDtypeStruct((t, dg), BF16),
                   jax.ShapeDtypeStruct((t, LANES), F32), jax.ShapeDtypeStruct((t, d), BF16),
                   jax.ShapeDtypeStruct((nt, 8, d), F32)],
        in_specs=[row(d), row(2 * dg), pl.BlockSpec((dg, tm), lambda i: (0, i)), pl.BlockSpec((dg, tm), lambda i: (0, i)),
                  full(g_fox), full(g_sb), full(w_out), full(mod8),
                  full(bd), full(hsel)],
        out_specs=[row(dg), row(dg), row(LANES), row(d), pl.BlockSpec((1, 8, d), lambda i: (i, 0, 0))],
        compiler_params=_params(("arbitrary",)),
    )(dx2, mix, o_fox, o_sb, g_fox, g_sb, w_out, mod8, bd, hsel)


def _in_proj_bwd(dparts, dfl, w_qkv, w_f, x, dx2, mod8, g_attn, tm):
    t, d = x.shape
    dg = dparts[1].shape[1]

    def body(*refs):
        d_refs = refs[:6]
        dfl_ref, w_ref, wf_ref, x_ref, dx2_ref, mod_ref, g_ref, gx_ref, dp_ref, dflb_ref, part_ref = refs[6:]
        dh = None
        for k in range(6):
            dk = d_refs[k][...].T if k in (0, 3) else d_refs[k][...]
            if k in (0, 3):
                dk = dk * HEAD_DIM ** -0.5
            db = dk.astype(BF16)
            dp_ref[:, k * dg:(k + 1) * dg] = db
            term = lax.dot_general(db, w_ref[:, k * dg:(k + 1) * dg], NT_DIMS, preferred_element_type=F32)
            dh = term if dh is None else dh + term
        dfb = dfl_ref[...].astype(BF16)
        dflb_ref[...] = dfb
        dh = dh + lax.dot_general(dfb, wf_ref[...], NT_DIMS, preferred_element_type=F32)
        dx, dshift, dscale, dgn = _norm_mod_bwd(x_ref[...], dh, g_ref[...], mod_ref[1:2, :])
        gx_ref[...] = dx2_ref[...] + dx
        part_ref[0] = jnp.zeros((8, d), F32)
        part_ref[0, 0:1, :] = dshift
        part_ref[0, 1:2, :] = dscale
        part_ref[0, 2:3, :] = dgn

    row = lambda w: pl.BlockSpec((tm, w), lambda i: (i, 0))
    full = lambda a: pl.BlockSpec(a.shape, lambda i: (0,) * a.ndim)
    nt = t // tm
    return _pcall(
        body, name="in_proj_bwd", grid=(nt,),
        out_shape=[jax.ShapeDtypeStruct((t, d), F32), jax.ShapeDtypeStruct((t, 6 * dg), BF16),
                   jax.ShapeDtypeStruct((t, LANES), BF16), jax.ShapeDtypeStruct((nt, 8, d), F32)],
        in_specs=[pl.BlockSpec((dg, tm), lambda i: (0, i)), row(dg), row(dg)] * 2
        + [row(LANES), full(w_qkv), full(w_f), row(d), row(d), full(mod8), full(g_attn)],
        out_specs=[row(d), row(6 * dg), row(LANES), pl.BlockSpec((1, 8, d), lambda i: (i, 0, 0))],
        compiler_params=_params(("arbitrary",)),
    )(*dparts, dfl, w_qkv, w_f, x, dx2, mod8, g_attn)


def _matmul_tn(a_t, b, name):
    m, t = a_t.shape
    n = b.shape[1]
    tm_ = _tile(m, (512, 256, 128))
    tn_ = _tile(n, (1024, 512, 256, 128))
    tk = _tile(t, (2048, 1024, 512, 256, 128))
    nk = t // tk

    def body(a_ref, b_ref, o_ref):
        k = pl.program_id(2)

        @pl.when(k == 0)
        def _():
            o_ref[...] = jnp.zeros_like(o_ref)

        o_ref[...] += jnp.dot(a_ref[...], b_ref[...], preferred_element_type=F32)

    return _pcall(
        body, name=name, grid=(m // tm_, n // tn_, nk),
        out_shape=jax.ShapeDtypeStruct((m, n), F32),
        in_specs=[pl.BlockSpec((tm_, tk), lambda i, j, k: (i, k)), pl.BlockSpec((tk, tn_), lambda i, j, k: (k, j))],
        out_specs=pl.BlockSpec((tm_, tn_), lambda i, j, k: (i, j)),
        compiler_params=_params(("arbitrary", "arbitrary", "arbitrary")),
    )(a_t, b)


HALO = 16


def _conv_taps(up_ext, cw, lo, rows):
    s1 = pltpu.roll(up_ext, 1, 0)
    s2 = pltpu.roll(up_ext, 2, 0)
    u = cw[2:3, :] * up_ext[lo:lo + rows] + cw[1:2, :] * s1[lo:lo + rows] + cw[0:1, :] * s2[lo:lo + rows] + cw[3:4, :]
    return u, s1, s2


def _ffn_fwd(x2, target, mod8, g_mlp, g_final, wg, wv, cwg, cwv, wd, tm, cf):
    t, d = x2.shape
    dfp = wg.shape[1]
    nt, nc = t // tm, dfp // cf
    hb = tm // HALO

    def body(x_ref, xp_ref, tg_ref, mod_ref, g_ref, gf_ref, wg_ref, wv_ref, cg_ref, cv_ref, wd_ref,
             dx3_ref, h2_ref, part_ref, h_sc, acc_sc):
        i, j = pl.program_id(0), pl.program_id(1)

        @pl.when(j == 0)
        def _():
            xe = jnp.concatenate([xp_ref[...], x_ref[...]], axis=0)
            h = _norm_mod(xe, g_ref[...], mod_ref[4:5, :], mod_ref[3:4, :]).astype(BF16)
            h_sc[...] = h
            h2_ref[...] = h[HALO:]
            acc_sc[...] = jnp.zeros_like(acc_sc)

        rowi = lax.broadcasted_iota(jnp.int32, (tm + HALO, 1), 0)
        keep = (rowi >= HALO) | (i > 0)
        hv = h_sc[...]
        upg = jnp.where(keep, jnp.dot(hv, wg_ref[...], preferred_element_type=F32), 0.0)
        upv = jnp.where(keep, jnp.dot(hv, wv_ref[...], preferred_element_type=F32), 0.0)
        ug, _, _ = _conv_taps(upg, cg_ref[...], HALO, tm)
        uv, _, _ = _conv_taps(upv, cv_ref[...], HALO, tm)
        act = (ug * jax.nn.sigmoid(ug) * uv).astype(BF16)
        acc_sc[...] += jnp.dot(act, wd_ref[...], preferred_element_type=F32)

        @pl.when(j == nc - 1)
        def _():
            y_ffn = acc_sc[...]
            x3 = x_ref[...] + mod_ref[5:6, :] * y_ffn
            r3 = lax.rsqrt(jnp.mean(x3 * x3, axis=-1, keepdims=True) + EPS)
            xn = x3 * r3
            gf = gf_ref[...]
            diff = xn * gf - tg_ref[...]
            dy = diff * (1.0 / d)
            dxn = dy * gf
            dx3 = r3 * (dxn - xn * jnp.mean(dxn * xn, axis=-1, keepdims=True))
            dx3_ref[...] = dx3
            part_ref[0] = jnp.zeros((8, d), F32)
            part_ref[0, 0:1, :] = jnp.sum(dy * xn, axis=0, keepdims=True)
            part_ref[0, 1:2, :] = jnp.sum(dx3 * y_ffn, axis=0, keepdims=True)
            part_ref[0, 2:3, :] = jnp.sum(diff * diff, axis=0, keepdims=True) * (0.5 / d)

    row = lambda w: pl.BlockSpec((tm, w), lambda i, j: (i, 0))
    full = lambda a: pl.BlockSpec(a.shape, lambda i, j: (0,) * a.ndim)
    return _pcall(
        body, name="ffn_fwd", grid=(nt, nc),
        out_shape=[jax.ShapeDtypeStruct((t, d), F32), jax.ShapeDtypeStruct((t, d), BF16),
                   jax.ShapeDtypeStruct((nt, 8, d), F32)],
        in_specs=[row(d), pl.BlockSpec((HALO, d), lambda i, j: (jnp.maximum(i * hb - 1, 0), 0)), row(d),
                  full(mod8), full(g_mlp), full(g_final),
                  pl.BlockSpec((d, cf), lambda i, j: (0, j)), pl.BlockSpec((d, cf), lambda i, j: (0, j)),
                  pl.BlockSpec((8, cf), lambda i, j: (0, j)), pl.BlockSpec((8, cf), lambda i, j: (0, j)),
                  pl.BlockSpec((cf, d), lambda i, j: (j, 0))],
        out_specs=[row(d), row(d), pl.BlockSpec((1, 8, d), lambda i, j: (i, 0, 0))],
        scratch_shapes=[pltpu.VMEM((tm + HALO, d), BF16), pltpu.VMEM((tm, d), F32)],
        compiler_params=_params(("arbitrary", "arbitrary")),
    )(x2, x2, target, mod8, g_mlp, g_final, wg, wv, cwg, cwv, wd)


def _chunk_major(w, cf):
    d, n = w.shape[0], w.shape[1] // cf
    return jnp.transpose(w.reshape(d, n, cf), (1, 0, 2))


def _ffn_fwd(x2, target, mod8, g_mlp, g_final, wg, wv, cwg, cwv, wd, tm, cf):
    t, d = x2.shape
    dfp = wg.shape[1]
    nt, nc = t // tm, dfp // cf
    hb = tm // HALO
    wg_c, wv_c = _chunk_major(wg, cf), _chunk_major(wv, cf)

    def body(x_ref, xp_ref, tg_ref, mod_ref, g_ref, gf_ref, wg_ref, wv_ref, cg_ref, cv_ref, wd_ref,
             dx3_ref, h2_ref, part_ref, act_sc):
        i = pl.program_id(0)
        xe = jnp.concatenate([xp_ref[...], x_ref[...]], axis=0)
        h = _norm_mod(xe, g_ref[...], mod_ref[4:5, :], mod_ref[3:4, :]).astype(BF16)
        h2_ref[...] = _transposed(h[HALO:])
        first = jnp.where(i > 0, h[:HALO], jnp.zeros_like(h[:HALO]))
        h = jnp.concatenate([first, h[HALO:]], axis=0)

        def up(c):
            return (jnp.dot(h, wg_ref[c], preferred_element_type=F32), jnp.dot(h, wv_ref[c], preferred_element_type=F32))

        def activation(c, ups):
            cols = slice(c * cf, (c + 1) * cf)
            ug, _, _ = _conv_taps(ups[0], cg_ref[:, cols], HALO, tm)
            uv, _, _ = _conv_taps(ups[1], cv_ref[:, cols], HALO, tm)
            act_sc[:, cols] = (ug * jax.nn.sigmoid(ug) * uv).astype(BF16)

        for c0 in range(0, nc, 2):
            group = list(range(c0, min(c0 + 2, nc)))
            ups = [up(c) for c in group]
            for c, u in zip(group, ups):
                activation(c, u)

        y_ffn = jnp.dot(act_sc[...], wd_ref[...], preferred_element_type=F32)
        x3 = x_ref[...] + mod_ref[5:6, :] * y_ffn
        r3 = lax.rsqrt(jnp.mean(x3 * x3, axis=-1, keepdims=True) + EPS)
        xn = x3 * r3
        gf = gf_ref[...]
        diff = xn * gf - tg_ref[...]
        dy = diff * (1.0 / d)
        dxn = dy * gf
        dx3 = r3 * (dxn - xn * jnp.mean(dxn * xn, axis=-1, keepdims=True))
        dx3_ref[...] = dx3
        part_ref[0] = jnp.zeros((8, d), F32)
        part_ref[0, 0:1, :] = jnp.sum(dy * xn, axis=0, keepdims=True)
        part_ref[0, 1:2, :] = jnp.sum(dx3 * y_ffn, axis=0, keepdims=True)
        part_ref[0, 2:3, :] = jnp.sum(diff * diff, axis=0, keepdims=True) * (0.5 / d)

    row = lambda w: pl.BlockSpec((tm, w), lambda i: (i, 0))
    full = lambda a: pl.BlockSpec(a.shape, lambda i: (0,) * a.ndim)
    once = lambda a: pl.BlockSpec(a.shape, lambda i: (0,) * a.ndim, pipeline_mode=pl.Buffered(1))
    return _pcall(
        body, name="ffn_fwd", grid=(nt,),
        out_shape=[jax.ShapeDtypeStruct((t, d), F32), jax.ShapeDtypeStruct((d, t), BF16),
                   jax.ShapeDtypeStruct((nt, 8, d), F32)],
        in_specs=[row(d), pl.BlockSpec((HALO, d), lambda i: (jnp.maximum(i * hb - 1, 0), 0)), row(d),
                  full(mod8), full(g_mlp), full(g_final), once(wg_c), once(wv_c), once(cwg), once(cwv), once(wd)],
        out_specs=[row(d), pl.BlockSpec((d, tm), lambda i: (0, i)), pl.BlockSpec((1, 8, d), lambda i: (i, 0, 0))],
        scratch_shapes=[pltpu.VMEM((tm, dfp), BF16)],
        compiler_params=_params(("arbitrary",)),
    )(x2, x2, target, mod8, g_mlp, g_final, wg_c, wv_c, cwg, cwv, wd)


def _ffn_bwd(x2, dx3, mod8, g_mlp, wg, wv, cwg, cwv, wd, tm, cf):
    t, d = x2.shape
    dfp = wg.shape[1]
    nt, nc = t // tm, dfp // cf
    hb = tm // HALO
    nhb = t // HALO
    ext = tm + 2 * HALO

    def body(x_ref, xp_ref, xn_ref, dx_ref, dxn_ref, mod_ref, g_ref, wg_ref, wv_ref, cg_ref, cv_ref, wd_ref,
             dx2_ref, dug_ref, duv_ref, act_ref, dxg_ref, part_ref, pcg_ref, pcv_ref, h_sc, dg_sc, dh_sc):
        i, j = pl.program_id(0), pl.program_id(1)

        @pl.when(j == 0)
        def _():
            xe = jnp.concatenate([xp_ref[...], x_ref[...], xn_ref[...]], axis=0)
            h_sc[...] = _norm_mod(xe, g_ref[...], mod_ref[4:5, :], mod_ref[3:4, :]).astype(BF16)
            de = (jnp.concatenate([dx_ref[...], dxn_ref[...]], axis=0) * mod_ref[5:6, :]).astype(BF16)
            dg_sc[...] = de
            dxg_ref[...] = de[:tm]
            dh_sc[...] = jnp.zeros_like(dh_sc)

        rowe = lax.broadcasted_iota(jnp.int32, (ext, 1), 0)
        keep_up = (rowe >= HALO) | (i > 0)
        rowu = lax.broadcasted_iota(jnp.int32, (tm + HALO, 1), 0)
        keep_du = (rowu < tm) | (i < nt - 1)
        hv = h_sc[...]
        upg = jnp.where(keep_up, jnp.dot(hv, wg_ref[...], preferred_element_type=F32), 0.0)
        upv = jnp.where(keep_up, jnp.dot(hv, wv_ref[...], preferred_element_type=F32), 0.0)
        cg, cv = cg_ref[...], cv_ref[...]
        ug, g1, g2 = _conv_taps(upg, cg, HALO, tm + HALO)
        uv, v1, v2 = _conv_taps(upv, cv, HALO, tm + HALO)
        dact = lax.dot_general(dg_sc[...], wd_ref[...], NT_DIMS, preferred_element_type=F32)
        sg = jax.nn.sigmoid(ug)
        sil = ug * sg
        act_ref[...] = (sil * uv)[:tm].astype(BF16)
        duv = jnp.where(keep_du, dact * sil, 0.0)
        dug = jnp.where(keep_du, dact * uv * (sg * (1.0 + ug * (1.0 - sg))), 0.0)

        def back(du, cw, up, s1, s2, pc_ref):
            n = tm + HALO
            dup = (cw[2:3, :] * du + cw[1:2, :] * pltpu.roll(du, n - 1, 0) + cw[0:1, :] * pltpu.roll(du, n - 2, 0))[:tm]
            dut = du[:tm]
            pc_ref[0] = jnp.zeros((8, cf), F32)
            pc_ref[0, 0:1, :] = jnp.sum(dut * s2[HALO:HALO + tm], axis=0, keepdims=True)
            pc_ref[0, 1:2, :] = jnp.sum(dut * s1[HALO:HALO + tm], axis=0, keepdims=True)
            pc_ref[0, 2:3, :] = jnp.sum(dut * up[HALO:HALO + tm], axis=0, keepdims=True)
            pc_ref[0, 3:4, :] = jnp.sum(dut, axis=0, keepdims=True)
            return dup.astype(BF16)

        dupg = back(dug, cg, upg, g1, g2, pcg_ref)
        dupv = back(duv, cv, upv, v1, v2, pcv_ref)
        dug_ref[...] = dupg
        duv_ref[...] = dupv
        dh_sc[...] += (lax.dot_general(dupg, wg_ref[...], NT_DIMS, preferred_element_type=F32)
                       + lax.dot_general(dupv, wv_ref[...], NT_DIMS, preferred_element_type=F32))

        @pl.when(j == nc - 1)
        def _():
            dx, dshift, dscale, dgn = _norm_mod_bwd(x_ref[...], dh_sc[...], g_ref[...], mod_ref[4:5, :])
            dx2_ref[...] = dx_ref[...] + dx
            part_ref[0] = jnp.zeros((8, d), F32)
            part_ref[0, 0:1, :] = dshift
            part_ref[0, 1:2, :] = dscale
            part_ref[0, 2:3, :] = dgn

    row = lambda w: pl.BlockSpec((tm, w), lambda i, j: (i, 0))
    prev = pl.BlockSpec((HALO, d), lambda i, j: (jnp.maximum(i * hb - 1, 0), 0))
    nxt = pl.BlockSpec((HALO, d), lambda i, j: (jnp.minimum((i + 1) * hb, nhb - 1), 0))
    full = lambda a: pl.BlockSpec(a.shape, lambda i, j: (0,) * a.ndim)
    chunk = pl.BlockSpec((tm, cf), lambda i, j: (i, j))
    pchunk = pl.BlockSpec((1, 8, cf), lambda i, j: (i, 0, j))
    return _pcall(
        body, name="ffn_bwd", grid=(nt, nc),
        out_shape=[jax.ShapeDtypeStruct((t, d), F32), jax.ShapeDtypeStruct((t, dfp), BF16),
                   jax.ShapeDtypeStruct((t, dfp), BF16), jax.ShapeDtypeStruct((t, dfp), BF16),
                   jax.ShapeDtypeStruct((t, d), BF16), jax.ShapeDtypeStruct((nt, 8, d), F32),
                   jax.ShapeDtypeStruct((nt, 8, dfp), F32), jax.ShapeDtypeStruct((nt, 8, dfp), F32)],
        in_specs=[row(d), prev, nxt, row(d), nxt, full(mod8), full(g_mlp),
                  pl.BlockSpec((d, cf), lambda i, j: (0, j)), pl.BlockSpec((d, cf), lambda i, j: (0, j)),
                  pl.BlockSpec((8, cf), lambda i, j: (0, j)), pl.BlockSpec((8, cf), lambda i, j: (0, j)),
                  pl.BlockSpec((cf, d), lambda i, j: (j, 0))],
        out_specs=[row(d), chunk, chunk, chunk, row(d), pl.BlockSpec((1, 8, d), lambda i, j: (i, 0, 0)), pchunk, pchunk],
        scratch_shapes=[pltpu.VMEM((ext, d), BF16), pltpu.VMEM((tm + HALO, d), BF16), pltpu.VMEM((tm, d), F32)],
        compiler_params=_params(("arbitrary", "arbitrary")),
    )(x2, x2, x2, dx3, dx3, mod8, g_mlp, wg, wv, cwg, cwv, wd)


def _ffn_bwd(x2, dx3, mod8, g_mlp, wg, wv, cwg, cwv, wd, tm, cf):
    t, d = x2.shape
    dfp = wg.shape[1]
    nt, nc = t // tm, dfp // cf
    hb = tm // HALO
    nhb = t // HALO
    n = tm + HALO

    def body(x_ref, xp_ref, xn_ref, dx_ref, dxn_ref, mod_ref, g_ref, wg_ref, wv_ref, cg_ref, cv_ref, wd_ref,
             dx2_ref, dug_ref, duv_ref, act_ref, dxg_ref, part_ref, pcg_ref, pcv_ref):
        i = pl.program_id(0)
        xe = jnp.concatenate([xp_ref[...], x_ref[...], xn_ref[...]], axis=0)
        h = _norm_mod(xe, g_ref[...], mod_ref[4:5, :], mod_ref[3:4, :]).astype(BF16)
        h = jnp.concatenate([jnp.where(i > 0, h[:HALO], jnp.zeros_like(h[:HALO])), h[HALO:]], axis=0)
        dx = dx_ref[...] * mod_ref[5:6, :]
        dxn = jnp.where(i < nt - 1, dxn_ref[...] * mod_ref[5:6, :], 0.0)
        de = jnp.concatenate([dx, dxn], axis=0).astype(BF16)
        dxg_ref[...] = de[:tm]
        pcg_ref[0] = jnp.zeros((8, dfp), F32)
        pcv_ref[0] = jnp.zeros((8, dfp), F32)

        def products(c):
            cols = slice(c * cf, (c + 1) * cf)
            return (jnp.dot(h, wg_ref[:, cols], preferred_element_type=F32), jnp.dot(h, wv_ref[:, cols], preferred_element_type=F32),
                    lax.dot_general(de, wd_ref[cols, :], NT_DIMS, preferred_element_type=F32))

        def back(du, cw, up, s1, s2, pc_ref, cols):
            dup = (cw[2:3, :] * du + cw[1:2, :] * pltpu.roll(du, n - 1, 0) + cw[0:1, :] * pltpu.roll(du, n - 2, 0))[:tm]
            dut = du[:tm]
            pc_ref[0, 0:1, cols] = jnp.sum(dut * s2[HALO:HALO + tm], axis=0, keepdims=True)
            pc_ref[0, 1:2, cols] = jnp.sum(dut * s1[HALO:HALO + tm], axis=0, keepdims=True)
            pc_ref[0, 2:3, cols] = jnp.sum(dut * up[HALO:HALO + tm], axis=0, keepdims=True)
            pc_ref[0, 3:4, cols] = jnp.sum(dut, axis=0, keepdims=True)
            return dup.astype(BF16)

        def chunk(c, prods):
            cols = slice(c * cf, (c + 1) * cf)
            upg, upv, dact = prods
            cg, cv = cg_ref[:, cols], cv_ref[:, cols]
            ug, g1, g2 = _conv_taps(upg, cg, HALO, n)
            uv, v1, v2 = _conv_taps(upv, cv, HALO, n)
            sg = jax.nn.sigmoid(ug)
            sil = ug * sg
            act_ref[cols, :] = _transposed((sil * uv)[:tm].astype(BF16))
            dug_ref[:, cols] = back(dact * uv * (sg * (1.0 + ug * (1.0 - sg))), cg, upg, g1, g2, pcg_ref, cols)
            duv_ref[:, cols] = back(dact * sil, cv, upv, v1, v2, pcv_ref, cols)

        for c0 in range(0, nc, 2):
            group = list(range(c0, min(c0 + 2, nc)))
            prods = [products(c) for c in group]
            for c, pr in zip(group, prods):
                chunk(c, pr)

        dh = (lax.dot_general(dug_ref[...], wg_ref[...], NT_DIMS, preferred_element_type=F32)
              + lax.dot_general(duv_ref[...], wv_ref[...], NT_DIMS, preferred_element_type=F32))
        dxt, dshift, dscale, dgn = _norm_mod_bwd(x_ref[...], dh, g_ref[...], mod_ref[4:5, :])
        dx2_ref[...] = dx_ref[...] + dxt
        part_ref[0] = jnp.zeros((8, d), F32)
        part_ref[0, 0:1, :] = dshift
        part_ref[0, 1:2, :] = dscale
        part_ref[0, 2:3, :] = dgn

    row = lambda w: pl.BlockSpec((tm, w), lambda i: (i, 0))
    prev = pl.BlockSpec((HALO, d), lambda i: (jnp.maximum(i * hb - 1, 0), 0))
    nxt = pl.BlockSpec((HALO, d), lambda i: (jnp.minimum((i + 1) * hb, nhb - 1), 0))
    full = lambda a: pl.BlockSpec(a.shape, lambda i: (0,) * a.ndim)
    once = lambda a: pl.BlockSpec(a.shape, lambda i: (0,) * a.ndim, pipeline_mode=pl.Buffered(1))
    part = lambda w: pl.BlockSpec((1, 8, w), lambda i: (i, 0, 0))
    return _pcall(
        body, name="ffn_bwd", grid=(nt,),
        out_shape=[jax.ShapeDtypeStruct((t, d), F32), jax.ShapeDtypeStruct((t, dfp), BF16),
                   jax.ShapeDtypeStruct((t, dfp), BF16), jax.ShapeDtypeStruct((dfp, t), BF16),
                   jax.ShapeDtypeStruct((t, d), BF16), jax.ShapeDtypeStruct((nt, 8, d), F32),
                   jax.ShapeDtypeStruct((nt, 8, dfp), F32), jax.ShapeDtypeStruct((nt, 8, dfp), F32)],
        in_specs=[row(d), prev, nxt, row(d), nxt, full(mod8), full(g_mlp), once(wg), once(wv), once(cwg), once(cwv), once(wd)],
        out_specs=[row(d), row(dfp), row(dfp), pl.BlockSpec((dfp, tm), lambda i: (0, i)), row(d), part(d), part(dfp), part(dfp)],
        compiler_params=_params(("arbitrary",)),
    )(x2, x2, x2, dx3, dx3, mod8, g_mlp, wg, wv, cwg, cwv, wd)


def _head_masks():
    lane = lax.broadcasted_iota(jnp.int32, (1, LANES), 1)
    in_a = lane < HEAD_DIM
    return in_a, jnp.logical_not(in_a)


BLK = 2 * LANES


def _stack_heads(qkv, dg):
    t = qkv.shape[0]
    p = dg // LANES
    rows = _tile(t, (512, 256, 128))
    sub = rows // BLK

    def body(kf_ref, vf_ref, ks_ref, vs_ref, okf, ovf, oks, ovs):
        in_a, in_b = _head_masks()
        for src, dst in ((kf_ref, okf), (vf_ref, ovf), (ks_ref, oks), (vs_ref, ovs)):
            v = src[...]
            zero = jnp.zeros_like(v)
            va, vb = jnp.where(in_a, v, zero), jnp.where(in_b, v, zero)
            for s in range(sub):
                dst[0, s, :BLK, :] = va[s * BLK:(s + 1) * BLK]
                dst[0, s, BLK:, :] = vb[s * BLK:(s + 1) * BLK]

    col = lambda base: pl.BlockSpec((rows, LANES), lambda h, j: (j, base * p + h))
    out = pl.BlockSpec((1, sub, 2 * BLK, LANES), lambda h, j: (h, j, 0, 0))
    shape = jax.ShapeDtypeStruct((p, t // BLK, 2 * BLK, LANES), BF16)
    return _pcall(
        body, name="stack_heads", grid=(p, t // rows),
        out_shape=[shape] * 4, in_specs=[col(1), col(2), col(4), col(5)], out_specs=[out] * 4,
        compiler_params=_params(("arbitrary", "arbitrary")),
    )(qkv, qkv, qkv, qkv)


def _tile_masks():
    rowi = lax.broadcasted_iota(jnp.int32, (BLK, BLK), 0)
    coli = lax.broadcasted_iota(jnp.int32, (BLK, BLK), 1)
    return coli <= rowi, coli < rowi


def _pair_triangle(suffix):
    r = lax.broadcasted_iota(jnp.int32, (BLK, BLK), 0)
    c = lax.broadcasted_iota(jnp.int32, (BLK, BLK), 1)
    return ((r >= c) if suffix else (r <= c)).astype(BF16)


def _pair_cumsum(x2, tri, passes):
    return jnp.concatenate([_split_dot(x2[:, :BLK], tri, passes), _split_dot(x2[:, BLK:], tri, passes)], axis=1)


def _pair_specs(t, dg, base):
    p = dg // LANES
    q = pl.BlockSpec((BLK, LANES), lambda h, i: (i, base * p + h))
    kv = pl.BlockSpec((1, t // BLK, 2 * BLK, LANES), lambda h, i: (h, 0, 0, 0))
    return q, kv


def _fox_fwd(qkv, kst, vst, fcol, frow2, dg):
    t = qkv.shape[0]
    p, nq = dg // LANES, t // BLK
    nh = 2 * p

    def body(q_ref, k_ref, v_ref, ft_ref, fs_ref, o_ref, lse_ref):
        i = pl.program_id(1)
        in_a, _ = _head_masks()
        causal, _ = _tile_masks()
        q2 = q_ref[...]
        ft = tuple(jnp.broadcast_to(ft_ref[a], (BLK, BLK)) for a in range(2))

        def tile(j, carry, masked):
            m, l, acc = carry
            kb, vb = k_ref[0, j], v_ref[0, j]
            s2 = lax.dot_general(q2, kb, NT_DIMS, preferred_element_type=F32)
            fs = fs_ref[0, j]
            m_new, l_new, alpha, pr = [], [], [], []
            for a in range(2):
                sl = slice(a * BLK, (a + 1) * BLK)
                s = (s2[:, sl] + ft[a]) - fs[:, sl]
                if masked:
                    s = jnp.where(causal, s, NEG)
                mn = jnp.maximum(m[a], jnp.max(s, axis=1, keepdims=True))
                pa = jnp.exp(s - mn)
                al = jnp.exp(m[a] - mn)
                m_new.append(mn)
                alpha.append(al)
                l_new.append(al * l[a] + jnp.sum(pa, axis=1, keepdims=True))
                pr.append(pa.astype(BF16))
            acc = jnp.where(in_a, alpha[0], alpha[1]) * acc + jnp.dot(
                jnp.concatenate(pr, axis=1), vb, preferred_element_type=F32)
            return tuple(m_new), tuple(l_new), acc

        neg, zero = jnp.full((BLK, 1), NEG, F32), jnp.zeros((BLK, 1), F32)
        carry = lax.fori_loop(0, i, functools.partial(tile, masked=False), ((neg, neg), (zero, zero), jnp.zeros((BLK, LANES), F32)))
        m, l, acc = tile(i, carry, True)
        o_ref[...] = acc / jnp.where(in_a, l[0], l[1])
        lse_ref[0] = m[0] + jnp.log(l[0])
        lse_ref[1] = m[1] + jnp.log(l[1])

    qs, kv = _pair_specs(t, dg, 0)
    col = pl.BlockSpec((2, BLK, 1), lambda h, i: (h, i, 0))
    return _pcall(
        body, name="fox_fwd", grid=(p, nq),
        out_shape=[jax.ShapeDtypeStruct((t, dg), F32), jax.ShapeDtypeStruct((nh, t, 1), F32)],
        in_specs=[qs, kv, kv, col, pl.BlockSpec((1, nq, 1, 2 * BLK), lambda h, i: (h, 0, 0, 0))],
        out_specs=[pl.BlockSpec((BLK, LANES), lambda h, i: (i, h)), col],
        compiler_params=_params(("arbitrary", "arbitrary")),
    )(qkv, kst, vst, fcol, frow2)


def _fold_heads(stacked, in_a):
    return jnp.where(in_a, stacked[:BLK], stacked[BLK:])


def _fox_bwd(qkv, kst, vst, do, fcol, frow2, lse, delta, dg):
    t = qkv.shape[0]
    p, nq = dg // LANES, t // BLK
    nh = 2 * p

    def body(q_ref, k_ref, v_ref, do_ref, ft_ref, fs_ref, lse_ref, dl_ref, dq_ref, dk_ref, dv_ref, dfs_ref, dft_ref):
        i = pl.program_id(1)

        @pl.when(i == 0)
        def _():
            dk_ref[...] = jnp.zeros_like(dk_ref)
            dv_ref[...] = jnp.zeros_like(dv_ref)
            dfs_ref[...] = jnp.zeros_like(dfs_ref)

        in_a, _ = _head_masks()
        causal, _ = _tile_masks()
        q2, do2 = q_ref[...], do_ref[...]
        ft = tuple(jnp.broadcast_to(ft_ref[a] - lse_ref[a], (BLK, BLK)) for a in range(2))
        dl = tuple(jnp.broadcast_to(dl_ref[a], (BLK, BLK)) for a in range(2))

        def tile(j, carry, masked):
            dq, dft = carry
            kb, vb = k_ref[0, j], v_ref[0, j]
            s2 = lax.dot_general(q2, kb, NT_DIMS, preferred_element_type=F32)
            dp2 = lax.dot_general(do2, vb, NT_DIMS, preferred_element_type=F32)
            fs = fs_ref[0, j]
            pr, ds, dft_new = [], [], []
            for a in range(2):
                sl = slice(a * BLK, (a + 1) * BLK)
                s = (s2[:, sl] + ft[a]) - fs[:, sl]
                if masked:
                    s = jnp.where(causal, s, NEG)
                pa = jnp.exp(s)
                dsa = pa * (dp2[:, sl] - dl[a])
                pr.append(pa.astype(BF16))
                ds.append(dsa)
                dft_new.append(dft[a] + jnp.sum(dsa, axis=1, keepdims=True))
            ds2 = jnp.concatenate(ds, axis=1)
            dsb = ds2.astype(BF16)
            off = pl.multiple_of(j * BLK, BLK)
            dk_ref[pl.ds(off, BLK), :] += _fold_heads(lax.dot_general(dsb, q2, TN_DIMS, preferred_element_type=F32), in_a)
            dv_ref[pl.ds(off, BLK), :] += _fold_heads(
                lax.dot_general(jnp.concatenate(pr, axis=1), do2, TN_DIMS, preferred_element_type=F32), in_a)
            dfs_ref[0, j] += -jnp.sum(ds2, axis=0, keepdims=True)
            return dq + jnp.dot(dsb, kb, preferred_element_type=F32), tuple(dft_new)

        zero = jnp.zeros((BLK, 1), F32)
        carry = lax.fori_loop(0, i, functools.partial(tile, masked=False), (jnp.zeros((BLK, LANES), F32), (zero, zero)))
        dq, dft = tile(i, carry, True)
        dq_ref[...] = dq
        dft_ref[0] = dft[0]
        dft_ref[1] = dft[1]

    qs, kv = _pair_specs(t, dg, 0)
    col = pl.BlockSpec((2, BLK, 1), lambda h, i: (h, i, 0))
    rowspec = pl.BlockSpec((1, nq, 1, 2 * BLK), lambda h, i: (h, 0, 0, 0))
    blk = pl.BlockSpec((BLK, LANES), lambda h, i: (i, h))
    acc = pl.BlockSpec((t, LANES), lambda h, i: (0, h))
    return _pcall(
        body, name="fox_bwd", grid=(p, nq),
        out_shape=[jax.ShapeDtypeStruct((t, dg), F32)] * 3 + [jax.ShapeDtypeStruct((p, nq, 1, 2 * BLK), F32),
                                                              jax.ShapeDtypeStruct((nh, t, 1), F32)],
        in_specs=[qs, kv, kv, blk, col, rowspec, col, col],
        out_specs=[blk, acc, acc, rowspec, col],
        compiler_params=_params(("arbitrary", "arbitrary")),
    )(qkv, kst, vst, do, fcol, frow2, lse, delta)


def _softplus_parts(z):
    e = jnp.exp(-jnp.abs(z))
    return jnp.maximum(z, 0.0) + jnp.log(1.0 + e), e


def _sigmoid_from(z, e):
    d = 1.0 + e
    r = pl.reciprocal(d, approx=True)
    r = r * (2.0 - d * r)
    return jnp.where(z >= 0, 1.0, e) * r


def _sb_fwd(qkv, kst, vst, dg):
    t = qkv.shape[0]
    p, nq = dg // LANES, t // BLK
    nh = 2 * p

    def body(q_ref, k_ref, v_ref, o_ref, rt_ref):
        i = pl.program_id(1)
        _, strict = _tile_masks()
        strict2 = jnp.concatenate([strict, strict], axis=1)
        suffix = _pair_triangle(True)
        q2 = q_ref[...]

        def tile(j, carry, masked):
            rest, acc = carry
            kb, vb = k_ref[0, j], v_ref[0, j]
            z = lax.dot_general(q2, kb, NT_DIMS, preferred_element_type=F32)
            sp, _ = _softplus_parts(z)
            if masked:
                sp = jnp.where(strict2, sp, 0.0)
            cs = _pair_cumsum(sp, suffix, 2)
            w, rest_new = [], []
            for a in range(2):
                sl = slice(a * BLK, (a + 1) * BLK)
                wa = jnp.exp(z[:, sl] - cs[:, sl] - rest[a])
                if masked:
                    wa = jnp.where(strict, wa, 0.0)
                w.append(wa.astype(BF16))
                rest_new.append(rest[a] + cs[:, a * BLK:a * BLK + 1])
            acc = acc + jnp.dot(jnp.concatenate(w, axis=1), vb, preferred_element_type=F32)
            return tuple(rest_new), acc

        zero = jnp.zeros((BLK, 1), F32)
        carry = tile(i, ((zero, zero), jnp.zeros((BLK, LANES), F32)), True)
        rest, acc = lax.fori_loop(0, i, lambda jj, c: tile(i - 1 - jj, c, False), carry)
        o_ref[...] = acc
        rt_ref[0] = rest[0]
        rt_ref[1] = rest[1]

    qs, kv = _pair_specs(t, dg, 3)
    col = pl.BlockSpec((2, BLK, 1), lambda h, i: (h, i, 0))
    return _pcall(
        body, name="sb_fwd", grid=(p, nq),
        out_shape=[jax.ShapeDtypeStruct((t, dg), F32), jax.ShapeDtypeStruct((nh, t, 1), F32)],
        in_specs=[qs, kv, kv],
        out_specs=[pl.BlockSpec((BLK, LANES), lambda h, i: (i, h)), col],
        compiler_params=_params(("arbitrary", "arbitrary")),
    )(qkv, kst, vst)


def _sb_bwd(qkv, kst, vst, do, rtot, dg):
    t = qkv.shape[0]
    p, nq = dg // LANES, t // BLK

    def body(q_ref, k_ref, v_ref, do_ref, rt_ref, dq_ref, dk_ref, dv_ref):
        i = pl.program_id(1)

        @pl.when(i == 0)
        def _():
            dk_ref[...] = jnp.zeros_like(dk_ref)
            dv_ref[...] = jnp.zeros_like(dv_ref)

        in_a, _ = _head_masks()
        _, strict = _tile_masks()
        strict2 = jnp.concatenate([strict, strict], axis=1)
        prefix = _pair_triangle(False)
        q2, do2 = q_ref[...], do_ref[...]
        rt = (rt_ref[0], rt_ref[1])

        def tile(j, carry, masked):
            before, gbefore, dq = carry
            kb, vb = k_ref[0, j], v_ref[0, j]
            z = lax.dot_general(q2, kb, NT_DIMS, preferred_element_type=F32)
            da = lax.dot_general(do2, vb, NT_DIMS, preferred_element_type=F32)
            sp, e = _softplus_parts(z)
            sig = _sigmoid_from(z, e)
            if masked:
                sp = jnp.where(strict2, sp, 0.0)
            pre = _pair_cumsum(sp, prefix, 2)
            w = []
            for a in range(2):
                sl = slice(a * BLK, (a + 1) * BLK)
                wa = jnp.exp(z[:, sl] + (before[a] - rt[a]) + pre[:, sl] - sp[:, sl])
                if masked:
                    wa = jnp.where(strict, wa, 0.0)
                w.append(wa)
            w2 = jnp.concatenate(w, axis=1)
            g = w2 * da
            preg = _pair_cumsum(g, prefix, 1)
            dz = []
            for a in range(2):
                sl = slice(a * BLK, (a + 1) * BLK)
                dza = g[:, sl] * (1.0 - sig[:, sl]) - sig[:, sl] * (gbefore[a] + preg[:, sl] - g[:, sl])
                if masked:
                    dza = jnp.where(strict, dza, 0.0)
                dz.append(dza.astype(BF16))
            dzb = jnp.concatenate(dz, axis=1)
            off = pl.multiple_of(j * BLK, BLK)
            dk_ref[pl.ds(off, BLK), :] += _fold_heads(lax.dot_general(dzb, q2, TN_DIMS, preferred_element_type=F32), in_a)
            dv_ref[pl.ds(off, BLK), :] += _fold_heads(
                lax.dot_general(w2.astype(BF16), do2, TN_DIMS, preferred_element_type=F32), in_a)
            last = lambda x, a: x[:, (a + 1) * BLK - 1:(a + 1) * BLK]
            return (tuple(before[a] + last(pre, a) for a in range(2)),
                    tuple(gbefore[a] + last(preg, a) for a in range(2)),
                    dq + jnp.dot(dzb, kb, preferred_element_type=F32))

        zero = jnp.zeros((BLK, 1), F32)
        carry = lax.fori_loop(0, i, functools.partial(tile, masked=False), ((zero, zero), (zero, zero), jnp.zeros((BLK, LANES), F32)))
        dq_ref[...] = tile(i, carry, True)[2]

    qs, kv = _pair_specs(t, dg, 3)
    col = pl.BlockSpec((2, BLK, 1), lambda h, i: (h, i, 0))
    blk = pl.BlockSpec((BLK, LANES), lambda h, i: (i, h))
    acc = pl.BlockSpec((t, LANES), lambda h, i: (0, h))
    return _pcall(
        body, name="sb_bwd", grid=(p, nq),
        out_shape=[jax.ShapeDtypeStruct((t, dg), F32)] * 3,
        in_specs=[qs, kv, kv, blk, col],
        out_specs=[blk, acc, acc],
        compiler_params=_params(("arbitrary", "arbitrary")),
    )(qkv, kst, vst, do, rtot)


XROWS = 144
LANE_FS, LANE_FT_A, LANE_FT_B = 0, 3, 6


def _pieces3(x):
    hi = x.astype(BF16).astype(F32)
    r = x - hi
    mid = r.astype(BF16).astype(F32)
    return hi, mid, (r - mid).astype(BF16).astype(F32)


def _bias_lanes(rows, entries):
    sub = lax.broadcasted_iota(jnp.int32, (16, 1), 0)
    out = jnp.zeros((16, rows), F32)
    for l, v in entries:
        out = jnp.where(sub == l, v, out)
    return jnp.concatenate([out, jnp.zeros((LANES - 16, rows), F32)], axis=0).T


def _three(first, values):
    return [(first + k, v) for k, v in enumerate(values)]


def _stack_rows(x, in_a, in_b):
    zero = jnp.zeros_like(x)
    return jnp.concatenate([jnp.where(in_a, x, zero), jnp.where(in_b, x, zero)], axis=0)


def _transposed(x):
    return x.astype(F32).T.astype(BF16)


def _attn_operands(qkv, fcol, dg):
    t = qkv.shape[0]
    p, nk = dg // LANES, t // BLK

    def body(qf_ref, kf_ref, vf_ref, ks_ref, vs_ref, f_ref, qx_ref, kx_ref, kxt_ref, vf_o, vft_o, ks_o, kst_o, vs_o, vst_o):
        in_a, in_b = _head_masks()
        fa, fb = _pieces3(f_ref[0]), _pieces3(f_ref[1])
        qx_ref[0, :, :LANES] = qf_ref[...]
        qx_ref[0, :, LANES:] = _bias_lanes(
            BLK, _three(LANE_FS, (-1.0,) * 3) + _three(LANE_FT_A, fa) + _three(LANE_FT_B, fb)).astype(BF16)
        kf = kf_ref[...]
        zero = jnp.zeros_like(kf)
        top = jnp.concatenate([jnp.where(in_a, kf, zero), _bias_lanes(
            BLK, _three(LANE_FS, fa) + _three(LANE_FT_A, (1.0,) * 3)).astype(BF16)], axis=1)
        bot = jnp.concatenate([jnp.where(in_b, kf, zero), _bias_lanes(
            BLK, _three(LANE_FS, fb) + _three(LANE_FT_B, (1.0,) * 3)).astype(BF16)], axis=1)
        kx = jnp.concatenate([top, bot], axis=0)
        kx_ref[0, 0] = kx
        kxt_ref[0, 0] = _transposed(kx)[:XROWS]
        for src, dst, dst_t in ((vf_ref, vf_o, vft_o), (ks_ref, ks_o, kst_o), (vs_ref, vs_o, vst_o)):
            st = _stack_rows(src[...], in_a, in_b)
            dst[0, 0] = st
            dst_t[0, 0] = _transposed(st)

    col = lambda base: pl.BlockSpec((BLK, LANES), lambda h, j: (j, base * p + h))
    blk4 = lambda r, c: pl.BlockSpec((1, 1, r, c), lambda h, j: (h, j, 0, 0))
    shp4 = lambda r, c: jax.ShapeDtypeStruct((p, nk, r, c), BF16)
    return _pcall(
        body, name="attn_operands", grid=(p, nk),
        out_shape=[jax.ShapeDtypeStruct((p, t, 2 * LANES), BF16), shp4(2 * BLK, 2 * LANES), shp4(XROWS, 2 * BLK)]
        + [shp4(2 * BLK, LANES), shp4(LANES, 2 * BLK)] * 3,
        in_specs=[col(0), col(1), col(2), col(4), col(5), pl.BlockSpec((2, 1, BLK), lambda h, j: (h, 0, j))],
        out_specs=[pl.BlockSpec((1, BLK, 2 * LANES), lambda h, j: (h, j, 0)), blk4(2 * BLK, 2 * LANES), blk4(XROWS, 2 * BLK)]
        + [blk4(2 * BLK, LANES), blk4(LANES, 2 * BLK)] * 3,
        compiler_params=_params(("arbitrary", "arbitrary")),
    )(qkv, qkv, qkv, qkv, qkv, fcol)


def _fox_q_bwd(qkv, fcol, lse_col, dg):
    t = qkv.shape[0]
    p = dg // LANES

    def body(q_ref, f_ref, l_ref, qx_ref):
        fa, fb = _pieces3(f_ref[0] - l_ref[0]), _pieces3(f_ref[1] - l_ref[1])
        qx_ref[0, :, :LANES] = q_ref[...]
        qx_ref[0, :, LANES:] = _bias_lanes(
            BLK, _three(LANE_FS, (-1.0,) * 3) + _three(LANE_FT_A, fa) + _three(LANE_FT_B, fb)).astype(BF16)

    row = pl.BlockSpec((2, 1, BLK), lambda h, j: (h, 0, j))
    return _pcall(
        body, name="fox_q_bwd", grid=(p, t // BLK),
        out_shape=jax.ShapeDtypeStruct((p, t, 2 * LANES), BF16),
        in_specs=[pl.BlockSpec((BLK, LANES), lambda h, j: (j, h)), row, row],
        out_specs=pl.BlockSpec((1, BLK, 2 * LANES), lambda h, j: (h, j, 0)),
        compiler_params=_params(("arbitrary", "arbitrary")),
    )(qkv, fcol, lse_col)


def _key_query_masks():
    key = lax.broadcasted_iota(jnp.int32, (BLK, BLK), 0)
    qry = lax.broadcasted_iota(jnp.int32, (BLK, BLK), 1)
    return key <= qry, key < qry


def _key_triangle(kind):
    s = lax.broadcasted_iota(jnp.int32, (BLK, BLK), 0)
    j = lax.broadcasted_iota(jnp.int32, (BLK, BLK), 1)
    return {"suffix": j >= s, "prefix": j <= s, "before": j < s}[kind].astype(BF16)


def _tri_dot(tri, x, passes):
    acc = None
    for _ in range(passes):
        part = x.astype(BF16)
        d = jnp.dot(tri, part, preferred_element_type=F32)
        acc = d if acc is None else acc + d
        x = x - part.astype(F32)
    return acc


GROUPS = (4, 2, 1)


def _loop_blocks(n, tiles, carry, descending=False, groups=GROUPS):
    at = (lambda k: n - 1 - k) if descending else (lambda k: k)
    done = 0
    for g in groups:
        left = n - done
        carry = lax.fori_loop(0, left // g, lambda h, c, g=g, done=done: tiles([at(done + g * h + k) for k in range(g)], c), carry)
        done = done + (left // g) * g
    return carry


def _resident(shape):
    return pl.BlockSpec((1,) + shape, lambda h, i: (h,) + (0,) * len(shape), pipeline_mode=pl.Buffered(1))


def _rows_per_head(a, b):
    return jnp.concatenate([jnp.broadcast_to(a, (HEAD_DIM, BLK)), jnp.broadcast_to(b, (HEAD_DIM, BLK))], axis=0)


def _fold_heads(stacked, in_a):
    return jnp.where(in_a, stacked[:BLK], stacked[BLK:])


def _xy_gather_copies(ins, outs, send_sems, recv_sems, local_sems):
    x, y, c = lax.axis_index("x"), lax.axis_index("y"), lax.axis_index("c")
    chips = [(1 - x, y), (x, 1 - y), (1 - x, 1 - y)]
    mine = 2 * x + y
    local, remote = [], []
    for w in range(len(ins)):
        local.append(pltpu.make_async_copy(ins[w], outs[w].at[mine], local_sems.at[w]))
        for k, (px, py) in enumerate(chips):
            remote.append(pltpu.make_async_remote_copy(
                src_ref=ins[w], dst_ref=outs[w].at[mine], send_sem=send_sems.at[3 * w + k],
                recv_sem=recv_sems.at[3 * w + k], device_id=(px, py, c), device_id_type=MESH))
    return local, remote


def _fox_fwd(qx, kx, v_t, dg, shards):
    p, t = qx.shape[0], qx.shape[1]
    nq = t // BLK
    nh = 2 * p
    ns = len(shards)

    def body(q_ref, k_ref, vt_ref, *rest):
        shard_refs, (o_ref, lse_ref), gathered = rest[:ns], rest[ns:ns + 2], rest[ns + 2:2 * ns + 2]
        local, remote = _xy_gather_copies(shard_refs, gathered, *rest[2 * ns + 2:])
        i = pl.program_id(1)

        @pl.when((pl.program_id(0) == 0) & (i == 0))
        def _():
            for cp in local + remote:
                cp.start()

        causal, _ = _key_query_masks()
        q = q_ref[0]

        def scores(j, masked):
            s2 = lax.dot_general(k_ref[0, j], q, NT_DIMS, preferred_element_type=F32)
            s = [s2[a * BLK:(a + 1) * BLK] for a in range(2)]
            return [jnp.where(causal, x, NEG) for x in s] if masked else s

        def update(blocks, carry):
            m, l, acc = carry
            m_new, l_new, alpha = [], [], []
            pr = [[] for _ in blocks]
            for a in range(2):
                mn = m[a]
                for _, s in blocks:
                    mn = jnp.maximum(mn, jnp.max(s[a], axis=0, keepdims=True))
                al = jnp.exp(m[a] - mn)
                ln = al * l[a]
                for k, (_, s) in enumerate(blocks):
                    pa = jnp.exp(s[a] - mn)
                    ln = ln + jnp.sum(pa, axis=0, keepdims=True)
                    pr[k].append(pa.astype(BF16))
                m_new.append(mn)
                alpha.append(al)
                l_new.append(ln)
            acc = _rows_per_head(*alpha) * acc
            for k, (j, _) in enumerate(blocks):
                acc = acc + jnp.dot(vt_ref[0, j], jnp.concatenate(pr[k], axis=0), preferred_element_type=F32)
            return tuple(m_new), tuple(l_new), acc

        tiles = lambda js, c: update([(j, scores(j, False)) for j in js], c)
        neg, zero = jnp.full((1, BLK), NEG, F32), jnp.zeros((1, BLK), F32)
        carry = _loop_blocks(i, tiles, ((neg, neg), (zero, zero), jnp.zeros((LANES, BLK), F32)))
        m, l, acc = update([(i, scores(i, True))], carry)
        o_ref[...] = acc / _rows_per_head(*l)
        lse_ref[0] = m[0] + jnp.log(l[0])
        lse_ref[1] = m[1] + jnp.log(l[1])

        @pl.when((pl.program_id(0) == p - 1) & (i == nq - 1))
        def _():
            for cp in remote:
                cp.wait_recv()
            for cp in remote:
                cp.wait_send()
            for cp in local:
                cp.wait()

    row = pl.BlockSpec((2, 1, BLK), lambda h, i: (h, 0, i))
    hbm = pl.BlockSpec(memory_space=pltpu.HBM)
    return _pcall(
        body, name="fox_fwd", grid=(p, nq),
        out_shape=[jax.ShapeDtypeStruct((dg, t), F32), jax.ShapeDtypeStruct((nh, 1, t), F32)]
        + [jax.ShapeDtypeStruct((4,) + s.shape, s.dtype) for s in shards],
        in_specs=[pl.BlockSpec((1, BLK, 2 * LANES), lambda h, i: (h, i, 0)), _resident((nq, 2 * BLK, 2 * LANES)),
                  _resident((nq, LANES, 2 * BLK))] + [hbm] * ns,
        out_specs=[pl.BlockSpec((LANES, BLK), lambda h, i: (h, i)), row] + [hbm] * ns,
        scratch_shapes=[pltpu.SemaphoreType.DMA((3 * ns,)), pltpu.SemaphoreType.DMA((3 * ns,)), pltpu.SemaphoreType.DMA((ns,))],
        compiler_params=_params(("arbitrary", "arbitrary")),
    )(qx, kx, v_t, *shards)


def _swap_copies(ins, outs, send_sems, recv_sems, local_sems):
    x, y, c = lax.axis_index("x"), lax.axis_index("y"), lax.axis_index("c")
    local, remote = [], []
    for w in range(len(ins)):
        local.append(pltpu.make_async_copy(ins[w], outs[w].at[c], local_sems.at[w]))
        remote.append(pltpu.make_async_remote_copy(
            src_ref=ins[w], dst_ref=outs[w].at[c], send_sem=send_sems.at[w], recv_sem=recv_sems.at[w],
            device_id=(x, y, 1 - c), device_id_type=MESH))
    return local, remote


def _fox_bwd(qxb, kx, kx_t, v_st, do, delta, dg, halves):
    p, t = qxb.shape[0], qxb.shape[1]
    nq = t // BLK
    nh = 2 * p
    ns = len(halves)

    def body(q_ref, k_ref, kt_ref, v_ref, do_ref, dl_ref, *rest):
        half_refs, (dq_ref, dft_ref, dk_ref, dv_ref, dkx_ref), both_refs = rest[:ns], rest[ns:ns + 5], rest[ns + 5:2 * ns + 5]
        local, remote = _swap_copies(half_refs, both_refs, *rest[2 * ns + 5:])
        i = pl.program_id(1)

        @pl.when((pl.program_id(0) == 0) & (i == 0))
        def _():
            for cp in local + remote:
                cp.start()

        @pl.when(i == 0)
        def _():
            dk_ref[...] = jnp.zeros_like(dk_ref)
            dv_ref[...] = jnp.zeros_like(dv_ref)
            dkx_ref[...] = jnp.zeros_like(dkx_ref)

        in_a, _ = _head_masks()
        first_lane = lax.broadcasted_iota(jnp.int32, (1, LANES), 1) == 0
        causal, _ = _key_query_masks()
        q, do2 = q_ref[0], do_ref[...]
        dl = (dl_ref[0], dl_ref[1])

        def products(j):
            return (lax.dot_general(k_ref[0, j], q, NT_DIMS, preferred_element_type=F32),
                    lax.dot_general(v_ref[0, j], do2, NT_DIMS, preferred_element_type=F32))

        def dscores(prod, masked):
            s2, dp2 = prod
            pr, ds = [], []
            for a in range(2):
                s = s2[a * BLK:(a + 1) * BLK]
                if masked:
                    s = jnp.where(causal, s, NEG)
                pa = jnp.exp(s)
                ds.append((pa * (dp2[a * BLK:(a + 1) * BLK] - dl[a])).astype(BF16))
                pr.append(pa.astype(BF16))
            return jnp.concatenate(ds, axis=0), jnp.concatenate(pr, axis=0)

        def accumulate(j, dsb, prb, dq):
            off = pl.multiple_of(j * BLK, BLK)
            dk_full = jnp.dot(dsb, q, preferred_element_type=F32)
            dk_ref[pl.ds(off, BLK), :] += _fold_heads(dk_full[:, :LANES], in_a)
            dkx_ref[pl.ds(off, BLK), :] += jnp.where(first_lane, dk_full[:BLK, LANES:], dk_full[BLK:, LANES:])
            dv_ref[pl.ds(off, BLK), :] += _fold_heads(jnp.dot(prb, do2, preferred_element_type=F32), in_a)
            return dq + jnp.dot(kt_ref[0, j], dsb, preferred_element_type=F32)

        def tiles(js, dq, masked=False):
            prods = [products(j) for j in js]
            grads = [dscores(pr, masked) for pr in prods]
            for j, (dsb, prb) in zip(js, grads):
                dq = accumulate(j, dsb, prb, dq)
            return dq

        dq = _loop_blocks(i, tiles, jnp.zeros((XROWS, BLK), F32))
        dq = tiles([i], dq, True)
        dq_ref[...] = dq[:LANES]
        dft_ref[0] = dq[LANES + LANE_FT_A:LANES + LANE_FT_A + 1]
        dft_ref[1] = dq[LANES + LANE_FT_B:LANES + LANE_FT_B + 1]

        @pl.when((pl.program_id(0) == p - 1) & (i == nq - 1))
        def _():
            for cp in remote:
                cp.wait_recv()
            for cp in remote:
                cp.wait_send()
            for cp in local:
                cp.wait()

    row = pl.BlockSpec((2, 1, BLK), lambda h, i: (h, 0, i))
    acc = pl.BlockSpec((t, LANES), lambda h, i: (0, h))
    hbm = pl.BlockSpec(memory_space=pltpu.HBM)
    return _pcall(
        body, name="fox_bwd", grid=(p, nq),
        out_shape=[jax.ShapeDtypeStruct((dg, t), F32), jax.ShapeDtypeStruct((nh, 1, t), F32)] + [jax.ShapeDtypeStruct((t, dg), F32)] * 3
        + [jax.ShapeDtypeStruct((2,) + h.shape, h.dtype) for h in halves],
        in_specs=[pl.BlockSpec((1, BLK, 2 * LANES), lambda h, i: (h, i, 0)), _resident((nq, 2 * BLK, 2 * LANES)),
                  _resident((nq, XROWS, 2 * BLK)), _resident((nq, 2 * BLK, LANES)),
                  pl.BlockSpec((BLK, LANES), lambda h, i: (i, h)), row] + [hbm] * ns,
        out_specs=[pl.BlockSpec((LANES, BLK), lambda h, i: (h, i)), row, acc, acc, acc] + [hbm] * ns,
        scratch_shapes=[pltpu.SemaphoreType.DMA((ns,)), pltpu.SemaphoreType.DMA((ns,)), pltpu.SemaphoreType.DMA((ns,))],
        compiler_params=_params(("arbitrary", "arbitrary")),
    )(qxb, kx, kx_t, v_st, do, delta, *halves)


def _softplus_of(z):
    return jnp.maximum(z, 0.0) + jnp.log(1.0 + jnp.exp(-jnp.abs(z)))


def _sb_fwd(qkv, k_st, v_t, dg):
    t = qkv.shape[0]
    p, nq = dg // LANES, t // BLK
    nh = 2 * p

    def body(q_ref, k_ref, vt_ref, o_ref, rt_ref):
        i = pl.program_id(1)
        _, strict = _key_query_masks()
        suffix = _key_triangle("suffix")
        q = q_ref[...]

        def scores(j):
            z2 = lax.dot_general(k_ref[0, j], q, NT_DIMS, preferred_element_type=F32)
            return [z2[a * BLK:(a + 1) * BLK] for a in range(2)]

        def suffix_sums(z, masked):
            out = []
            for a in range(2):
                sp = _softplus_of(z[a])
                if masked:
                    sp = jnp.where(strict, sp, 0.0)
                out.append(_tri_dot(suffix, sp, 2))
            return out

        def weights(z, cs, rest, masked):
            w, rest_new = [], []
            for a in range(2):
                wa = jnp.exp(z[a] - cs[a] - rest[a])
                if masked:
                    wa = jnp.where(strict, wa, 0.0)
                w.append(wa.astype(BF16))
                rest_new.append(rest[a] + cs[a][0:1])
            return jnp.concatenate(w, axis=0), tuple(rest_new)

        def tiles(js, carry, masked=False):
            rest, acc = carry
            zs = [scores(j) for j in js]
            css = [suffix_sums(z, masked) for z in zs]
            ws = []
            for z, cs in zip(zs, css):
                w2, rest = weights(z, cs, rest, masked)
                ws.append(w2)
            for j, w2 in zip(js, ws):
                acc = acc + jnp.dot(vt_ref[0, j], w2, preferred_element_type=F32)
            return rest, acc

        zero = jnp.zeros((1, BLK), F32)
        carry = tiles([i], ((zero, zero), jnp.zeros((LANES, BLK), F32)), True)
        rest, acc = _loop_blocks(i, tiles, carry, descending=True)
        o_ref[...] = acc
        rt_ref[0] = rest[0]
        rt_ref[1] = rest[1]

    return _pcall(
        body, name="sb_fwd", grid=(p, nq),
        out_shape=[jax.ShapeDtypeStruct((dg, t), F32), jax.ShapeDtypeStruct((nh, 1, t), F32)],
        in_specs=[pl.BlockSpec((BLK, LANES), lambda h, i: (i, 3 * p + h)), _resident((nq, 2 * BLK, LANES)),
                  _resident((nq, LANES, 2 * BLK))],
        out_specs=[pl.BlockSpec((LANES, BLK), lambda h, i: (h, i)), pl.BlockSpec((2, 1, BLK), lambda h, i: (h, 0, i))],
        compiler_params=_params(("arbitrary", "arbitrary")),
    )(qkv, k_st, v_t)


def _scatter8_copies(ins, outs, send_sems, recv_sems, local_sems):
    x, y, c = lax.axis_index("x"), lax.axis_index("y"), lax.axis_index("c")
    me = 4 * x + 2 * y + c
    local, remote = [], []
    for w in range(len(ins)):
        local.append(pltpu.make_async_copy(ins[w].at[me], outs[w].at[me], local_sems.at[w]))
        for f in range(1, 8):
            px = 1 - x if f & 4 else x
            py = 1 - y if f & 2 else y
            pc = 1 - c if f & 1 else c
            remote.append(pltpu.make_async_remote_copy(
                src_ref=ins[w].at[4 * px + 2 * py + pc], dst_ref=outs[w].at[me],
                send_sem=send_sems.at[7 * w + f - 1], recv_sem=recv_sems.at[7 * w + f - 1],
                device_id=(px, py, pc), device_id_type=MESH))
    return local, remote


def _sb_bwd(qkv, k_st, k_t, v_st, do, rtot, dg, pieces):
    t = qkv.shape[0]
    p, nq = dg // LANES, t // BLK
    ns = len(pieces)

    def body(q_ref, k_ref, kt_ref, v_ref, do_ref, rt_ref, *rest):
        piece_refs, (dq_ref, dk_ref, dv_ref), recv_refs = rest[:ns], rest[ns:ns + 3], rest[ns + 3:2 * ns + 3]
        local, remote = _scatter8_copies(piece_refs, recv_refs, *rest[2 * ns + 3:])
        i = pl.program_id(1)

        @pl.when((pl.program_id(0) == 0) & (i == 0))
        def _():
            for cp in local + remote:
                cp.start()

        @pl.when(i == 0)
        def _():
            dk_ref[...] = jnp.zeros_like(dk_ref)
            dv_ref[...] = jnp.zeros_like(dv_ref)

        in_a, _ = _head_masks()
        _, strict = _key_query_masks()
        before_m, prefix_m = _key_triangle("before"), _key_triangle("prefix")
        q, do2 = q_ref[...], do_ref[...]
        rt = (rt_ref[0], rt_ref[1])

        def products(j):
            z2 = lax.dot_general(k_ref[0, j], q, NT_DIMS, preferred_element_type=F32)
            da2 = lax.dot_general(v_ref[0, j], do2, NT_DIMS, preferred_element_type=F32)
            return [z2[a * BLK:(a + 1) * BLK] for a in range(2)], [da2[a * BLK:(a + 1) * BLK] for a in range(2)]

        def softplus_sums(z, masked):
            sp = [_softplus_of(x) for x in z]
            if masked:
                sp = [jnp.where(strict, x, 0.0) for x in sp]
            return sp, [_tri_dot(before_m, x, 2) for x in sp]

        def weight_grads(z, da, sp, pre, before, masked):
            w, g, pg, before_new = [], [], [], []
            for a in range(2):
                wa = jnp.exp(z[a] + (before[a] - rt[a]) + pre[a])
                if masked:
                    wa = jnp.where(strict, wa, 0.0)
                ga = wa * da[a]
                w.append(wa.astype(BF16))
                g.append(ga)
                pg.append(jnp.dot(prefix_m, ga.astype(BF16), preferred_element_type=F32))
                before_new.append(before[a] + pre[a][BLK - 1:BLK] + sp[a][BLK - 1:BLK])
            return jnp.concatenate(w, axis=0), g, pg, tuple(before_new)

        def dlogits(sp, g, pg, gbefore, masked):
            dz, gbefore_new = [], []
            for a in range(2):
                s_incl = gbefore[a] + pg[a]
                dza = (g[a] - s_incl) + jnp.exp(-sp[a]) * s_incl
                if masked:
                    dza = jnp.where(strict, dza, 0.0)
                dz.append(dza.astype(BF16))
                gbefore_new.append(s_incl[BLK - 1:BLK])
            return jnp.concatenate(dz, axis=0), tuple(gbefore_new)

        def accumulate(j, dzb, wb, dq):
            off = pl.multiple_of(j * BLK, BLK)
            dk_ref[pl.ds(off, BLK), :] += _fold_heads(jnp.dot(dzb, q, preferred_element_type=F32), in_a)
            dv_ref[pl.ds(off, BLK), :] += _fold_heads(jnp.dot(wb, do2, preferred_element_type=F32), in_a)
            return dq + jnp.dot(kt_ref[0, j], dzb, preferred_element_type=F32)

        def tiles(js, carry, masked=False):
            before, gbefore, dq = carry
            prods = [products(j) for j in js]
            sums = [softplus_sums(z, masked) for z, _ in prods]
            grads = []
            for (z, da), (sp, pre) in zip(prods, sums):
                wb, g, pg, before = weight_grads(z, da, sp, pre, before, masked)
                grads.append((wb, g, pg))
            for j, (sp, _), (wb, g, pg) in zip(js, sums, grads):
                dzb, gbefore = dlogits(sp, g, pg, gbefore, masked)
                dq = accumulate(j, dzb, wb, dq)
            return before, gbefore, dq

        zero = jnp.zeros((1, BLK), F32)
        carry = _loop_blocks(i, tiles, ((zero, zero), (zero, zero), jnp.zeros((LANES, BLK), F32)), groups=(2, 1))
        dq_ref[...] = tiles([i], carry, True)[2]

        @pl.when((pl.program_id(0) == p - 1) & (i == nq - 1))
        def _():
            for cp in remote:
                cp.wait_recv()
            for cp in remote:
                cp.wait_send()
            for cp in local:
                cp.wait()

    acc = pl.BlockSpec((t, LANES), lambda h, i: (0, h))
    hbm = pl.BlockSpec(memory_space=pltpu.HBM)
    return _pcall(
        body, name="sb_bwd", grid=(p, nq),
        out_shape=[jax.ShapeDtypeStruct((dg, t), F32)] + [jax.ShapeDtypeStruct((t, dg), F32)] * 2
        + [jax.ShapeDtypeStruct(pc.shape, pc.dtype) for pc in pieces],
        in_specs=[pl.BlockSpec((BLK, LANES), lambda h, i: (i, 3 * p + h)), _resident((nq, 2 * BLK, LANES)),
                  _resident((nq, LANES, 2 * BLK)), _resident((nq, 2 * BLK, LANES)),
                  pl.BlockSpec((BLK, LANES), lambda h, i: (i, h)), pl.BlockSpec((2, 1, BLK), lambda h, i: (h, 0, i))] + [hbm] * ns,
        out_specs=[pl.BlockSpec((LANES, BLK), lambda h, i: (h, i)), acc, acc] + [hbm] * ns,
        scratch_shapes=[pltpu.SemaphoreType.DMA((7 * ns,)), pltpu.SemaphoreType.DMA((7 * ns,)), pltpu.SemaphoreType.DMA((ns,))],
        compiler_params=_params(("arbitrary", "arbitrary")),
    )(qkv, k_st, k_t, v_st, do, rtot, *pieces)


def _tri_constants(nh, t):
    nb = t // LANES
    r = nh * nb
    li = np.arange(LANES)
    tri_in = (li[:, None] <= li[None, :])
    ri = np.arange(r)
    same = (ri[:, None] // nb) == (ri[None, :] // nb)
    blk = same & (ri[None, :] < ri[:, None])
    blk_rev = same & (ri[None, :] > ri[:, None])
    head_rows = (np.arange(max(8, nh))[:, None] == (ri[None, :] // nb))
    as_bf16 = lambda a: jnp.asarray(a.astype(np.float32), BF16)
    return as_bf16(tri_in), as_bf16(blk), as_bf16(tri_in.T), as_bf16(blk_rev), as_bf16(head_rows)


def kernel(x, c, w_ada, b_ada, g_attn, w_in, b_fgate, g_out_fox, g_out_sb, w_out, g_mlp, w_up, conv_w, conv_b, w_down, g_final, loss_target, m_w_ada, m_b_ada, m_g_attn, m_w_in, m_b_fgate, m_g_out_fox, m_g_out_sb, m_w_out, m_g_mlp, m_w_up, m_conv_w, m_conv_b, m_w_down, m_g_final, v_w_ada, v_b_ada, v_g_attn, v_w_in, v_b_fgate, v_g_out_fox, v_g_out_sb, v_w_out, v_g_mlp, v_w_up, v_conv_w, v_conv_b, v_w_down, v_g_final):
    t, d = x.shape[1], x.shape[2]
    dg = d // 2
    nh = dg // HEAD_DIM
    n_in = 6 * dg + nh
    dff = w_down.shape[1] * 4
    dfp = -(-dff // 256) * 256
    cf = 256
    tm = _tile(t, (512, 256, 128))
    nq = t // BLK
    xi, yi, ci = lax.axis_index("x"), lax.axis_index("y"), lax.axis_index("c")
    shard = 2 * xi + yi
    me = 4 * xi + 2 * yi + ci

    x2d, tg2d = x[0], loss_target[0]

    c_all = _all_gather8(jnp.pad(c, ((0, 7), (0, 0)))).reshape(8, 8, d)[:, 0, :]
    ada_cols = w_ada.shape[2]
    b_shard = lax.dynamic_slice(b_ada, (0, shard * ada_cols), (1, ada_cols))
    sc_all, mod_shard = _ada_fwd(c_all, w_ada[0], b_shard)
    mod_all = _all_gather8(mod_shard).reshape(4, 2, 8, ada_cols)
    mod_me = lax.dynamic_index_in_dim(mod_all[:, 0], me, axis=1, keepdims=False)
    mod8 = jnp.pad(mod_me.reshape(6, d), ((0, 2), (0, 0)))

    lane_pad = lambda a: jnp.pad(a, ((0, 0),) * (a.ndim - 1) + ((0, -a.shape[-1] % LANES),))
    (g_in,) = _gather_xy([lane_pad(w_in[0].astype(BF16))])
    later_shards = [w_out[0].astype(BF16), lane_pad(w_up[0].astype(BF16)), w_down[0].astype(BF16), lane_pad(conv_w[0])]
    w_in_full = jnp.transpose(g_in[:, :, :n_in // 4], (1, 0, 2)).reshape(d, n_in)
    w_qkv = w_in_full[:, :6 * dg]
    w_f = jnp.pad(w_in_full[:, 6 * dg:], ((0, 0), (0, LANES - nh)))

    qkv, fl, h1 = _in_proj_fwd(x2d, mod8, g_attn, w_qkv, w_f, tm)
    tri_in, tri_blk, tri_in_rev, tri_blk_rev, head_rows = _tri_constants(nh, t)
    fl2d = fl[:, :nh].T.reshape(nh * t // LANES, LANES)
    b_rows = jnp.repeat(b_fgate[0], t // LANES)[:, None]
    f2d = _fgate_fwd(fl2d, b_rows, tri_in, tri_blk)
    fcol = f2d.reshape(nh, 1, t)
    pairs = nh // 2
    qx, kx, kx_t, vf_st, vf_t, ks_st, ks_t, vs_st, vs_t = _attn_operands(qkv, fcol, dg)
    o_fox_t, lse, g_out, g_up, g_down, g_cw = _fox_fwd(qx, kx, vf_t, dg, later_shards)
    g_up, g_cw = g_up[:, :, :dff // 2], g_cw[:, :, :dff // 2]
    w_out_full = g_out.reshape(2 * dg, d)
    w_up_full = jnp.transpose(g_up, (1, 0, 2)).reshape(d, 2 * dff)
    padc = ((0, 0), (0, dfp - dff))
    wg, wv = jnp.pad(w_up_full[:, :dff], padc), jnp.pad(w_up_full[:, dff:], padc)
    wd = jnp.pad(g_down.reshape(dff, d), ((0, dfp - dff), (0, 0)))
    cw_full = jnp.transpose(g_cw, (1, 0, 2)).reshape(3, 2 * dff)
    cw4 = jnp.concatenate([cw_full, conv_b], axis=0)
    cwg = jnp.pad(cw4[:, :dff], ((0, 4), (0, dfp - dff)))
    cwv = jnp.pad(cw4[:, dff:], ((0, 4), (0, dfp - dff)))
    o_sb_t, rtot = _sb_fwd(qkv, ks_st, vs_t, dg)
    o_fox, o_sb = o_fox_t, o_sb_t
    li = np.arange(dg)
    bd = jnp.asarray((li[:, None] // HEAD_DIM == li[None, :] // HEAD_DIM).astype(np.float32), BF16)
    hsel = jnp.asarray((li[:, None] // HEAD_DIM == np.arange(LANES)[None, :]).astype(np.float32), BF16)
    x2, mix, mix_t = _attn_out_fwd(x2d, o_fox, o_sb, g_out_fox, g_out_sb, w_out_full, mod8, bd, tm)
    g_final2 = g_final[None, :]
    dx3, h2, part_f = _ffn_fwd(x2, tg2d, mod8, g_mlp, g_final2, wg, wv, cwg, cwv, wd, tm, cf)

    tm_ffn_bwd = _tile(t, (256, 128))
    dx2, dupg, dupv, act, dxg3, part_b, pcg, pcv = _ffn_bwd(x2, dx3, mod8, g_mlp, wg, wv, cwg, cwv, wd, tm_ffn_bwd, cf)
    do_fox, do_sb, delta, dxg2, part_o = _attn_out_bwd(dx2, mix, o_fox, o_sb, g_out_fox, g_out_sb, w_out_full, mod8, bd, hsel, tm)
    drow = delta[:, :nh].T.reshape(nh, 1, t)

    def col_pieces(g):
        r, cc = g.shape
        return jnp.transpose(g.reshape(2, r // 2, 4, cc // 4), (2, 0, 1, 3)).reshape(8, r // 2, cc // 4)

    def row_pieces(g):
        r, cc = g.shape
        return g.reshape(8, r // 8, cc)

    gw_out = _matmul_tn(mix_t, dxg2, "grad_w_out")
    gw_upg = _matmul_tn(h2, dupg, "grad_w_up_gate")
    gw_upv = _matmul_tn(h2, dupv, "grad_w_up_val")
    gw_up = jnp.concatenate([gw_upg[:, :dff], gw_upv[:, :dff]], axis=1)
    gw_down = _matmul_tn(act, dxg3, "grad_w_down")[:dff]
    early = (row_pieces(gw_out), lane_pad(col_pieces(gw_up)), row_pieces(gw_down))
    early = [_to_bf16(pc, "pieces_bf16_" + nm) for pc, nm in zip(early, ("w_out", "w_up", "w_down"))]

    qxb = _fox_q_bwd(qkv, fcol, lse, dg)
    dq_s_t, dk_s, dv_s, *recv_early = _sb_bwd(qkv, ks_st, ks_t, vs_st, do_sb, rtot, dg, early)
    halves_early = [_sum_leading(rv, nm) for rv, nm in zip(recv_early, ("sum_w_out", "sum_w_up", "sum_w_down"))]
    dq_f_t, dft, dk_f, dv_f, dkx, *swapped_early = _fox_bwd(qxb, kx, kx_t, vf_st, do_fox, drow, dg, halves_early)
    dq_f, dq_s = dq_f_t, dq_s_t
    f2d_shape = (nh * t // LANES, LANES)
    dfs = jnp.transpose(dkx.reshape(t, pairs, LANES)[:, :, :2], (1, 2, 0))
    dfl2d, gb8 = _fgate_bwd(fl2d, b_rows, dft.reshape(f2d_shape), dfs.reshape(f2d_shape), tri_in_rev, tri_blk_rev, head_rows)
    dfl = jnp.pad(dfl2d.reshape(nh, t).T, ((0, 0), (0, LANES - nh)))
    grad_x, dproj, dflb, part_i = _in_proj_bwd([dq_f, dk_f, dv_f, dq_s, dk_s, dv_s], dfl, w_qkv, w_f, x2d, dx2, mod8, g_attn, tm)

    gw_qkv = _matmul_tn(h1, dproj, "grad_w_qkv")
    gw_f = _matmul_tn(h1, dflb, "grad_w_f")
    gw_in = jnp.concatenate([gw_qkv, gw_f[:, :nh]], axis=1)

    sf = _sum_leading(part_f, "sum_part_ffn_fwd")
    sb_ = _sum_leading(part_b, "sum_part_ffn_bwd")
    so = _sum_leading(part_o, "sum_part_attn_out")
    si = _sum_leading(part_i, "sum_part_in_proj")
    scg = _sum_leading(pcg, "sum_part_conv_gate")
    scv = _sum_leading(pcv, "sum_part_conv_val")
    gb_f = gb8[:nh, 0]
    dmod = jnp.concatenate([si[0], si[1], so[0], sb_[0], sb_[1], sf[1]])
    g_conv_w = jnp.concatenate([scg[0:3, :dff], scv[0:3, :dff]], axis=1).reshape(-1)
    g_conv_b = jnp.concatenate([scg[3, :dff], scv[3, :dff]])
    loss_part = jnp.sum(sf[2])
    fields = [dmod, si[2], gb_f, so[1, :dg], so[1, dg:], sb_[2], g_conv_b, sf[0], g_conv_w, loss_part[None]]
    sizes = [int(f.shape[0]) for f in fields]
    n_pack = sum(sizes)
    lanes_pack = -(-n_pack // (8 * LANES)) * LANES
    pack = jnp.pad(jnp.concatenate(fields), (0, 8 * lanes_pack - n_pack)).reshape(8, lanes_pack)
    gathered = _all_gather8(pack)
    tot = _sum_leading(gathered.reshape(8, 8, lanes_pack), "sum_pack").reshape(-1)
    offs = np.concatenate([[0], np.cumsum(sizes)])
    take = lambda k: tot[int(offs[k]):int(offs[k + 1])]
    g_b_ada, g_g_attn, g_b_fgate, g_g_fox, g_g_sb, g_g_mlp, g_cb, g_g_final, g_cw_full, loss_v = [take(k) for k in range(10)]
    loss = loss_v[0]
    dmod_all = gathered.reshape(8, 8 * lanes_pack)[:, :6 * d]
    dmod_cols = lax.dynamic_slice(dmod_all, (0, shard * ada_cols), (8, ada_cols))
    g_w_ada = _ada_bwd(sc_all.T, dmod_cols)

    (recv_in,) = _scatter8([_to_bf16(lane_pad(col_pieces(gw_in)), "pieces_bf16_w_in")])
    (swapped_in,) = _swap_halves([_sum_leading(recv_in, "sum_w_in")])
    swapped = [swapped_in] + swapped_early
    shard_cols = (n_in // 4, d, dff // 2, d)
    g_w_in, g_w_out, g_w_up, g_w_down = [s.reshape(2 * s.shape[1], s.shape[2])[:, :cc] for s, cc in zip(swapped, shard_cols)]
    g_conv_w_shard = lax.dynamic_slice(g_cw_full.reshape(3, 2 * dff), (0, shard * (dff // 2)), (3, dff // 2))

    grads, deltas, new_m, new_v = {}, {}, {}, {}

    def step(name, w, g, m, v):
        shape = w.shape
        as2d = lambda a: a.reshape(-1, shape[-1])
        dl, nm, nv = _adamw(as2d(w), as2d(g), as2d(m), as2d(v), "adamw_" + name)
        grads[name], deltas[name], new_m[name], new_v[name] = g.reshape(shape), dl.reshape(shape), nm.reshape(shape), nv.reshape(shape)

    step("w_ada", w_ada, g_w_ada, m_w_ada, v_w_ada)
    step("w_in", w_in, g_w_in, m_w_in, v_w_in)
    step("w_out", w_out, g_w_out, m_w_out, v_w_out)
    step("w_up", w_up, g_w_up, m_w_up, v_w_up)
    step("conv_w", conv_w, g_conv_w_shard, m_conv_w, v_conv_w)
    step("w_down", w_down, g_w_down, m_w_down, v_w_down)

    small = [("b_ada", b_ada, g_b_ada, m_b_ada, v_b_ada), ("g_attn", g_attn, g_g_attn, m_g_attn, v_g_attn),
             ("b_fgate", b_fgate, g_b_fgate, m_b_fgate, v_b_fgate), ("g_out_fox", g_out_fox, g_g_fox, m_g_out_fox, v_g_out_fox),
             ("g_out_sb", g_out_sb, g_g_sb, m_g_out_sb, v_g_out_sb), ("g_mlp", g_mlp, g_g_mlp, m_g_mlp, v_g_mlp),
             ("conv_b", conv_b, g_cb, m_conv_b, v_conv_b), ("g_final", g_final, g_g_final, m_g_final, v_g_final)]
    ssz = [int(np.prod(s[1].shape)) for s in small]
    n_small = sum(ssz)
    lanes_small = -(-n_small // (8 * LANES)) * LANES
    packs = [jnp.pad(jnp.concatenate([s[k].reshape(-1) for s in small]), (0, 8 * lanes_small - n_small)).reshape(8, lanes_small)
             for k in (1, 2, 3, 4)]
    dl_s, nm_s, nv_s = _adamw(*packs, "adamw_small")
    so_ = np.concatenate([[0], np.cumsum(ssz)])
    for k, s in enumerate(small):
        cut = lambda a: a.reshape(-1)[int(so_[k]):int(so_[k + 1])].reshape(s[1].shape)
        grads[s[0]], deltas[s[0]], new_m[s[0]], new_v[s[0]] = s[2].reshape(s[1].shape), cut(dl_s), cut(nm_s), cut(nv_s)

    order = ["w_ada", "b_ada", "g_attn", "w_in", "b_fgate", "g_out_fox", "g_out_sb", "w_out", "g_mlp", "w_up",
             "conv_w", "conv_b", "w_down", "g_final"]
    return (loss, grad_x[None], *[grads[n] for n in order], *[deltas[n] for n in order],
            *[new_m[n] for n in order], *[new_v[n] for n in order])
```

```python
import functools

import numpy as np
import jax
import jax.numpy as jnp
from jax import lax
from jax.experimental import pallas as pl
from jax.experimental.pallas import tpu as pltpu

F32 = jnp.float32
BF16 = jnp.bfloat16
MESH = pl.DeviceIdType.MESH

HEAD_DIM = 64
LANES = 128
EPS = 1e-6
NEG = -1e30
ADAM_LR, ADAM_B1, ADAM_B2, ADAM_EPS, ADAM_WD, ADAM_STEP = 0.001, 0.9, 0.999, 1e-08, 0.01, 10
V7X_VMEM_BYTES = 64 * 1024 * 1024
VMEM_LIMIT = V7X_VMEM_BYTES - 12 * 1024 * 1024
NT_DIMS = (((1,), (1,)), ((), ()))
TN_DIMS = (((0,), (0,)), ((), ()))


def _pcall(body, **kw):
    return pl.pallas_call(body, **kw)


def _params(sem=None, **kw):
    return pltpu.CompilerParams(dimension_semantics=sem, vmem_limit_bytes=VMEM_LIMIT, **kw)


def _split_dot(x, m, passes):
    acc = None
    for _ in range(passes):
        part = x.astype(BF16)
        d = jnp.dot(part, m, preferred_element_type=F32)
        acc = d if acc is None else acc + d
        x = x - part.astype(F32)
    return acc


def _tile(n, candidates):
    for t in candidates:
        if n % t == 0:
            return t
    return n


def _rows_tile(rows, row_bytes, budget=2 * 1024 * 1024):
    best = None
    for t in range(8, rows + 1, 8):
        if rows % t == 0 and t * row_bytes <= budget:
            best = t
    return best if best is not None else rows


def _all_gather8(v):
    m_per, n = v.shape

    def body(x_ref, out_ref, send_sems, recv_sems, local_sem):
        x, y, c = lax.axis_index("x"), lax.axis_index("y"), lax.axis_index("c")
        me, sibling = (x, y, c), (x, y, 1 - c)
        chips = [(1 - x, y), (x, 1 - y), (1 - x, 1 - y)]

        def rows(px, py, pc):
            return out_ref.at[pl.ds((4 * px + 2 * py + pc) * m_per, m_per), :]

        def copy(k, block, to, src=None):
            return pltpu.make_async_remote_copy(
                src_ref=rows(*block) if src is None else src, dst_ref=rows(*block),
                send_sem=send_sems.at[k], recv_sem=recv_sems.at[k], device_id=to, device_id_type=MESH)

        mine = pltpu.make_async_copy(x_ref, rows(*me), local_sem)
        mine.start()
        first = [copy(0, me, sibling, src=x_ref)]
        first += [copy(1 + j, me, (*chip, c), src=x_ref) for j, chip in enumerate(chips)]
        for cp in first:
            cp.start()
        passed = [copy(4 + j, (*chip, c), sibling) for j, chip in enumerate(chips)]
        for j, chip in enumerate(chips):
            copy(1 + j, (*chip, c), me).wait_recv()
            passed[j].start()
        copy(0, sibling, me).wait_recv()
        for j, chip in enumerate(chips):
            copy(4 + j, (*chip, 1 - c), me).wait_recv()
        for cp in first + passed:
            cp.wait_send()
        mine.wait()

    return _pcall(
        body, name="all_gather8",
        out_shape=jax.ShapeDtypeStruct((8 * m_per, n), v.dtype),
        in_specs=[pl.BlockSpec(memory_space=pltpu.VMEM)],
        out_specs=pl.BlockSpec(memory_space=pltpu.VMEM),
        scratch_shapes=[pltpu.SemaphoreType.DMA((7,)), pltpu.SemaphoreType.DMA((7,)), pltpu.SemaphoreType.DMA],
        compiler_params=pltpu.CompilerParams(vmem_limit_bytes=VMEM_LIMIT),
    )(v)


def _gather_xy(shards):
    n = len(shards)

    def body(*refs):
        ins, outs = refs[:n], refs[n:2 * n]
        send_sems, recv_sems, local_sems = refs[2 * n:]
        x, y, c = lax.axis_index("x"), lax.axis_index("y"), lax.axis_index("c")
        chips = [(1 - x, y), (x, 1 - y), (1 - x, 1 - y)]
        mine = 2 * x + y
        local, remote = [], []
        for w in range(n):
            cp = pltpu.make_async_copy(ins[w], outs[w].at[mine], local_sems.at[w])
            cp.start()
            local.append(cp)
            for k, (px, py) in enumerate(chips):
                cp = pltpu.make_async_remote_copy(
                    src_ref=ins[w], dst_ref=outs[w].at[mine], send_sem=send_sems.at[3 * w + k],
                    recv_sem=recv_sems.at[3 * w + k], device_id=(px, py, c), device_id_type=MESH)
                cp.start()
                remote.append(cp)
        for cp in remote:
            cp.wait_recv()
        for cp in remote:
            cp.wait_send()
        for cp in local:
            cp.wait()

    hbm = pl.BlockSpec(memory_space=pltpu.HBM)
    return _pcall(
        body, name="gather_xy",
        out_shape=[jax.ShapeDtypeStruct((4,) + s.shape, s.dtype) for s in shards],
        in_specs=[hbm] * n, out_specs=[hbm] * n,
        scratch_shapes=[pltpu.SemaphoreType.DMA((3 * n,)), pltpu.SemaphoreType.DMA((3 * n,)),
                        pltpu.SemaphoreType.DMA((n,))],
        compiler_params=pltpu.CompilerParams(vmem_limit_bytes=VMEM_LIMIT),
    )(*shards)


def _scatter8(pieces):
    n = len(pieces)

    def body(*refs):
        ins, outs = refs[:n], refs[n:2 * n]
        send_sems, recv_sems, local_sems = refs[2 * n:]
        x, y, c = lax.axis_index("x"), lax.axis_index("y"), lax.axis_index("c")
        me = 4 * x + 2 * y + c
        local, remote = [], []
        for w in range(n):
            cp = pltpu.make_async_copy(ins[w].at[me], outs[w].at[me], local_sems.at[w])
            cp.start()
            local.append(cp)
            for f in range(1, 8):
                px = 1 - x if f & 4 else x
                py = 1 - y if f & 2 else y
                pc = 1 - c if f & 1 else c
                cp = pltpu.make_async_remote_copy(
                    src_ref=ins[w].at[4 * px + 2 * py + pc], dst_ref=outs[w].at[me],
                    send_sem=send_sems.at[7 * w + f - 1], recv_sem=recv_sems.at[7 * w + f - 1],
                    device_id=(px, py, pc), device_id_type=MESH)
                cp.start()
                remote.append(cp)
        for cp in remote:
            cp.wait_recv()
        for cp in remote:
            cp.wait_send()
        for cp in local:
            cp.wait()

    hbm = pl.BlockSpec(memory_space=pltpu.HBM)
    return _pcall(
        body, name="scatter8",
        out_shape=[jax.ShapeDtypeStruct(p.shape, p.dtype) for p in pieces],
        in_specs=[hbm] * n, out_specs=[hbm] * n,
        scratch_shapes=[pltpu.SemaphoreType.DMA((7 * n,)), pltpu.SemaphoreType.DMA((7 * n,)),
                        pltpu.SemaphoreType.DMA((n,))],
        compiler_params=pltpu.CompilerParams(vmem_limit_bytes=VMEM_LIMIT),
    )(*pieces)


def _swap_halves(halves):
    n = len(halves)
    chunks = 8
    n_chunks = [max(k for k in (chunks, 4, 2, 1) if h.shape[0] % (8 * k) == 0) for h in halves]

    def body(*refs):
        ins, outs = refs[:n], refs[n:2 * n]
        send_sems, recv_sems, local_sems = refs[2 * n:]
        x, y, c = lax.axis_index("x"), lax.axis_index("y"), lax.axis_index("c")
        local, remote = [], []
        for w in range(n):
            cp = pltpu.make_async_copy(ins[w], outs[w].at[c], local_sems.at[w])
            cp.start()
            local.append(cp)
            rows = ins[w].shape[0] // n_chunks[w]
            for k in range(n_chunks[w]):
                cp = pltpu.make_async_remote_copy(
                    src_ref=ins[w].at[pl.ds(k * rows, rows)], dst_ref=outs[w].at[c, pl.ds(k * rows, rows)],
                    send_sem=send_sems.at[chunks * w + k], recv_sem=recv_sems.at[chunks * w + k],
                    device_id=(x, y, 1 - c), device_id_type=MESH)
                cp.start()
                remote.append(cp)
        for cp in remote:
            cp.wait_recv()
        for cp in remote:
            cp.wait_send()
        for cp in local:
            cp.wait()

    hbm = pl.BlockSpec(memory_space=pltpu.HBM)
    return _pcall(
        body, name="swap_halves",
        out_shape=[jax.ShapeDtypeStruct((2,) + h.shape, h.dtype) for h in halves],
        in_specs=[hbm] * n, out_specs=[hbm] * n,
        scratch_shapes=[pltpu.SemaphoreType.DMA((chunks * n,)), pltpu.SemaphoreType.DMA((chunks * n,)),
                        pltpu.SemaphoreType.DMA((n,))],
        compiler_params=pltpu.CompilerParams(vmem_limit_bytes=VMEM_LIMIT),
    )(*halves)


def _sum_leading(a, name):
    n, r, c = a.shape
    tr = _rows_tile(r, n * c * 4, budget=6 * 1024 * 1024)
    if a.dtype == BF16 and tr % 16:
        tr = r

    def body(a_ref, o_ref):
        acc = a_ref[0].astype(F32)
        for k in range(1, n):
            acc = acc + a_ref[k].astype(F32)
        o_ref[...] = acc

    return _pcall(
        body, name=name, grid=(r // tr,),
        out_shape=jax.ShapeDtypeStruct((r, c), F32),
        in_specs=[pl.BlockSpec((n, tr, c), lambda i: (0, i, 0))],
        out_specs=pl.BlockSpec((tr, c), lambda i: (i, 0)),
        compiler_params=_params(("arbitrary",)),
    )(a)


def _to_bf16(a, name):
    n, r, c = a.shape

    def body(a_ref, o_ref):
        o_ref[...] = a_ref[...].astype(BF16)

    spec = pl.BlockSpec((1, r, c), lambda i: (i, 0, 0))
    return _pcall(
        body, name=name, grid=(n,), out_shape=jax.ShapeDtypeStruct(a.shape, BF16),
        in_specs=[spec], out_specs=spec, compiler_params=_params(("arbitrary",)),
    )(a)


def _adamw(w, g, m, v, name):
    r, c = w.shape
    tr = _rows_tile(r, c * 4, budget=1024 * 1024)
    c1 = 1.0 - ADAM_B1 ** ADAM_STEP
    c2 = 1.0 - ADAM_B2 ** ADAM_STEP

    def body(w_ref, g_ref, m_ref, v_ref, d_ref, nm_ref, nv_ref):
        gg = g_ref[...]
        nm = ADAM_B1 * m_ref[...] + (1.0 - ADAM_B1) * gg
        nv = ADAM_B2 * v_ref[...] + (1.0 - ADAM_B2) * (gg * gg)
        m_hat = nm / c1
        v_hat = nv / c2
        d_ref[...] = -ADAM_LR * (m_hat / (jnp.sqrt(v_hat) + ADAM_EPS) + ADAM_WD * w_ref[...])
        nm_ref[...] = nm
        nv_ref[...] = nv

    spec = pl.BlockSpec((tr, c), lambda i: (i, 0))
    return _pcall(
        body, name=name, grid=(r // tr,),
        out_shape=[jax.ShapeDtypeStruct((r, c), F32)] * 3,
        in_specs=[spec] * 4, out_specs=[spec] * 3,
        compiler_params=_params(("arbitrary",)),
    )(w, g, m, v)


def _ada_fwd(c_all, w_shard, b_shard):
    nb, d = c_all.shape
    cols = w_shard.shape[1]

    def body(c_ref, w_ref, b_ref, sc_ref, mod_ref):
        cv = c_ref[...]
        sc = cv * jax.nn.sigmoid(cv)
        sc_ref[...] = sc
        mod_ref[...] = jnp.dot(sc.astype(BF16), w_ref[...].astype(BF16), preferred_element_type=F32) + b_ref[...]

    return _pcall(
        body, name="ada_fwd",
        out_shape=[jax.ShapeDtypeStruct((nb, d), F32), jax.ShapeDtypeStruct((nb, cols), F32)],
        compiler_params=pltpu.CompilerParams(vmem_limit_bytes=VMEM_LIMIT),
    )(c_all, w_shard, b_shard)


def _ada_bwd(sc_t, dmod_cols):
    d, nb = sc_t.shape
    cols = dmod_cols.shape[1]
    tr = _rows_tile(d, cols * 4, budget=1024 * 1024)

    def body(s_ref, m_ref, o_ref):
        s = s_ref[...]
        m = m_ref[...]
        acc = s[:, 0:1] * m[0:1, :]
        for b in range(1, nb):
            acc = acc + s[:, b:b + 1] * m[b:b + 1, :]
        o_ref[...] = acc

    return _pcall(
        body, name="ada_bwd", grid=(d // tr,),
        out_shape=jax.ShapeDtypeStruct((d, cols), F32),
        in_specs=[pl.BlockSpec((tr, nb), lambda i: (i, 0)), pl.BlockSpec((nb, cols), lambda i: (0, 0))],
        out_specs=pl.BlockSpec((tr, cols), lambda i: (i, 0)),
        compiler_params=_params(("arbitrary",)),
    )(sc_t, dmod_cols)


def _log_sigmoid(x):
    return jnp.minimum(x, 0.0) - jnp.log1p(jnp.exp(-jnp.abs(x)))


def _fgate_fwd(fl2d, b_rows, tri_in, tri_blk):
    r = fl2d.shape[0]

    def body(x_ref, b_ref, u_ref, l_ref, f_ref):
        lf = _log_sigmoid(x_ref[...] + b_ref[...])
        c1 = _split_dot(lf, u_ref[...], 3)
        tot = jnp.broadcast_to(c1[:, LANES - 1:LANES], (r, LANES))
        acc = None
        for _ in range(3):
            part = tot.astype(BF16)
            dd = jnp.dot(l_ref[...], part, preferred_element_type=F32)
            acc = dd if acc is None else acc + dd
            tot = tot - part.astype(F32)
        f_ref[...] = c1 + acc

    return _pcall(
        body, name="fgate_fwd", out_shape=jax.ShapeDtypeStruct((r, LANES), F32),
        compiler_params=pltpu.CompilerParams(vmem_limit_bytes=VMEM_LIMIT),
    )(fl2d, b_rows, tri_in, tri_blk)


def _fgate_bwd(fl2d, b_rows, df_query, df_key, tri_in_rev, tri_blk_rev, head_rows):
    r = fl2d.shape[0]
    nhp = head_rows.shape[0]

    def body(x_ref, b_ref, dq_ref, dk_ref, u_ref, l_ref, hr_ref, o_ref, gb_ref):
        c1 = _split_dot(dq_ref[...] + dk_ref[...], u_ref[...], 3)
        tot = jnp.broadcast_to(c1[:, 0:1], (r, LANES))
        acc = None
        for _ in range(3):
            part = tot.astype(BF16)
            dd = jnp.dot(l_ref[...], part, preferred_element_type=F32)
            acc = dd if acc is None else acc + dd
            tot = tot - part.astype(F32)
        x = x_ref[...] + b_ref[...]
        e = jnp.exp(-jnp.abs(x))
        dfl = (c1 + acc) * (jnp.where(x >= 0, e, 1.0) / (1.0 + e))
        o_ref[...] = dfl
        rs = jnp.broadcast_to(jnp.sum(dfl, axis=1, keepdims=True), (r, LANES))
        gb = None
        for _ in range(3):
            part = rs.astype(BF16)
            dd = jnp.dot(hr_ref[...], part, preferred_element_type=F32)
            gb = dd if gb is None else gb + dd
            rs = rs - part.astype(F32)
        gb_ref[...] = gb

    return _pcall(
        body, name="fgate_bwd",
        out_shape=[jax.ShapeDtypeStruct((r, LANES), F32), jax.ShapeDtypeStruct((nhp, LANES), F32)],
        compiler_params=pltpu.CompilerParams(vmem_limit_bytes=VMEM_LIMIT),
    )(fl2d, b_rows, df_query, df_key, tri_in_rev, tri_blk_rev, head_rows)


def _norm_mod(x, g, scale, shift):
    r = lax.rsqrt(jnp.mean(x * x, axis=-1, keepdims=True) + EPS)
    return (x * r * g) * (1.0 + scale) + shift


def _norm_mod_bwd(x, dh, g, scale):
    r = lax.rsqrt(jnp.mean(x * x, axis=-1, keepdims=True) + EPS)
    xn = x * r
    dshift = jnp.sum(dh, axis=0, keepdims=True)
    dscale = jnp.sum(dh * (xn * g), axis=0, keepdims=True)
    dxn_g = dh * (1.0 + scale)
    dg = jnp.sum(dxn_g * xn, axis=0, keepdims=True)
    dxn = dxn_g * g
    dx = r * (dxn - xn * jnp.mean(dxn * xn, axis=-1, keepdims=True))
    return dx, dshift, dscale, dg


def _in_proj_fwd(x, mod8, g_attn, w_qkv, w_f, tm):
    t, d = x.shape
    dg = w_qkv.shape[1] // 6

    def body(x_ref, mod_ref, g_ref, w_ref, wf_ref, qkv_ref, fl_ref, h1_ref, h_sc):
        j = pl.program_id(1)

        @pl.when(j == 0)
        def _():
            h = _norm_mod(x_ref[...], g_ref[...], mod_ref[1:2, :], mod_ref[0:1, :]).astype(BF16)
            h_sc[...] = h
            h1_ref[...] = _transposed(h)
            fl_ref[...] = jnp.dot(h, wf_ref[...], preferred_element_type=F32)

        s = jnp.where((j == 0) | (j == 3), HEAD_DIM ** -0.5, 1.0)
        qkv_ref[...] = (jnp.dot(h_sc[...], w_ref[...], preferred_element_type=F32) * s).astype(BF16)

    return _pcall(
        body, name="in_proj_fwd", grid=(t // tm, 6),
        out_shape=[jax.ShapeDtypeStruct((t, 6 * dg), BF16), jax.ShapeDtypeStruct((t, LANES), F32),
                   jax.ShapeDtypeStruct((d, t), BF16)],
        in_specs=[pl.BlockSpec((tm, d), lambda i, j: (i, 0)), pl.BlockSpec((8, d), lambda i, j: (0, 0)),
                  pl.BlockSpec((1, d), lambda i, j: (0, 0)), pl.BlockSpec((d, dg), lambda i, j: (0, j)),
                  pl.BlockSpec((d, LANES), lambda i, j: (0, 0))],
        out_specs=[pl.BlockSpec((tm, dg), lambda i, j: (i, j)), pl.BlockSpec((tm, LANES), lambda i, j: (i, 0)),
                   pl.BlockSpec((d, tm), lambda i, j: (0, i))],
        scratch_shapes=[pltpu.VMEM((tm, d), BF16)],
        compiler_params=_params(("arbitrary", "arbitrary")),
    )(x, mod8, g_attn, w_qkv, w_f)


def _head_rstd(o, bd):
    return lax.rsqrt(_split_dot(o * o, bd, 3) * (1.0 / HEAD_DIM) + EPS)


def _attn_out_fwd(x, o_fox, o_sb, g_fox, g_sb, w_out, mod8, bd, tm):
    t, d = x.shape
    dg = o_fox.shape[0]

    def body(x_ref, of_ref, os_ref, gf_ref, gs_ref, w_ref, mod_ref, bd_ref, x2_ref, mix_ref, mixt_ref):
        of, osb = of_ref[...].T, os_ref[...].T
        mf = (of * _head_rstd(of, bd_ref[...]) * gf_ref[...]).astype(BF16)
        ms = (osb * _head_rstd(osb, bd_ref[...]) * gs_ref[...]).astype(BF16)
        mix_ref[:, :dg] = mf
        mix_ref[:, dg:] = ms
        mixt_ref[:dg, :] = _transposed(mf)
        mixt_ref[dg:, :] = _transposed(ms)
        y = jnp.dot(mf, w_ref[:dg, :], preferred_element_type=F32) + jnp.dot(ms, w_ref[dg:, :], preferred_element_type=F32)
        x2_ref[...] = x_ref[...] + mod_ref[2:3, :] * y

    row = lambda w: pl.BlockSpec((tm, w), lambda i: (i, 0))
    full = lambda a: pl.BlockSpec(a.shape, lambda i: (0,) * a.ndim)
    return _pcall(
        body, name="attn_out_fwd", grid=(t // tm,),
        out_shape=[jax.ShapeDtypeStruct((t, d), F32), jax.ShapeDtypeStruct((t, 2 * dg), BF16),
                   jax.ShapeDtypeStruct((2 * dg, t), BF16)],
        in_specs=[row(d), pl.BlockSpec((dg, tm), lambda i: (0, i)), pl.BlockSpec((dg, tm), lambda i: (0, i)),
                  full(g_fox), full(g_sb), full(w_out), full(mod8), full(bd)],
        out_specs=[row(d), row(2 * dg), pl.BlockSpec((2 * dg, tm), lambda i: (0, i))],
        compiler_params=_params(("arbitrary",)),
    )(x, o_fox, o_sb, g_fox, g_sb, w_out, mod8, bd)


def _attn_out_bwd(dx2, mix, o_fox, o_sb, g_fox, g_sb, w_out, mod8, bd, hsel, tm):
    t, d = dx2.shape
    dg = o_fox.shape[0]

    def body(dx_ref, mix_ref, of_ref, os_ref, gf_ref, gs_ref, w_ref, mod_ref, bd_ref, hs_ref,
             dof_ref, dos_ref, dlt_ref, dxg_ref, part_ref):
        dx = dx_ref[...]
        gate = mod_ref[2:3, :]
        dxg = (dx * gate).astype(BF16)
        dxg_ref[...] = dxg
        mixv = mix_ref[...]
        y = jnp.dot(mixv[:, :dg], w_ref[:dg, :], preferred_element_type=F32)
        y = y + jnp.dot(mixv[:, dg:], w_ref[dg:, :], preferred_element_type=F32)
        part_ref[0] = jnp.zeros((8, d), F32)
        part_ref[0, 0:1, :] = jnp.sum(dx * y, axis=0, keepdims=True)
        for grp, (o_ref, g_ref, do_ref) in enumerate(((of_ref, gf_ref, dof_ref), (os_ref, gs_ref, dos_ref))):
            dmix = lax.dot_general(dxg, w_ref[grp * dg:(grp + 1) * dg, :], NT_DIMS, preferred_element_type=F32)
            o = o_ref[...].T
            r = _head_rstd(o, bd_ref[...])
            n = o * r
            part_ref[0, 1:2, grp * dg:(grp + 1) * dg] = jnp.sum(dmix * n, axis=0, keepdims=True)
            dn = dmix * g_ref[...]
            mh = _split_dot(dn * n, bd_ref[...], 3) * (1.0 / HEAD_DIM)
            do = r * (dn - n * mh)
            do_ref[...] = do.astype(BF16)
            if grp == 0:
                dlt_ref[...] = _split_dot(do * o, hs_ref[...], 3)

    row = lambda w: pl.BlockSpec((tm, w), lambda i: (i, 0))
    full = lambda a: pl.BlockSpec(a.shape, lambda i: (0,) * a.ndim)
    nt = t // tm
    return _pcall(
        body, name="attn_out_bwd", grid=(nt,),
        out_shape=[jax.ShapeDtypeStruct((t, dg), BF16), jax.ShapeDtypeStruct((t, dg), BF16),
                   jax.ShapeDtypeStruct((t, LANES), F32), jax.ShapeDtypeStruct((t, d), BF16),
                   jax.ShapeDtypeStruct((nt, 8, d), F32)],
        in_specs=[row(d), row(2 * dg), pl.BlockSpec((dg, tm), lambda i: (0, i)), pl.BlockSpec((dg, tm), lambda i: (0, i)),
                  full(g_fox), full(g_sb), full(w_out), full(mod8),
                  full(bd), full(hsel)],
        out_specs=[row(dg), row(dg), row(LANES), row(d), pl.BlockSpec((1, 8, d), lambda i: (i, 0, 0))],
        compiler_params=_params(("arbitrary",)),
    )(dx2, mix, o_fox, o_sb, g_fox, g_sb, w_out, mod8, bd, hsel)


def _in_proj_bwd(dparts, dfl, w_qkv, w_f, x, dx2, mod8, g_attn, tm):
    t, d = x.shape
    dg = dparts[1].shape[1]

    def body(*refs):
        d_refs = refs[:6]
        dfl_ref, w_ref, wf_ref, x_ref, dx2_ref, mod_ref, g_ref, gx_ref, dp_ref, dflb_ref, part_ref = refs[6:]
        dh = None
        for k in range(6):
            dk = d_refs[k][...].T if k in (0, 3) else d_refs[k][...]
            if k in (0, 3):
                dk = dk * HEAD_DIM ** -0.5
            db = dk.astype(BF16)
            dp_ref[:, k * dg:(k + 1) * dg] = db
            term = lax.dot_general(db, w_ref[:, k * dg:(k + 1) * dg], NT_DIMS, preferred_element_type=F32)
            dh = term if dh is None else dh + term
        dfb = dfl_ref[...].astype(BF16)
        dflb_ref[...] = dfb
        dh = dh + lax.dot_general(dfb, wf_ref[...], NT_DIMS, preferred_element_type=F32)
        dx, dshift, dscale, dgn = _norm_mod_bwd(x_ref[...], dh, g_ref[...], mod_ref[1:2, :])
        gx_ref[...] = dx2_ref[...] + dx
        part_ref[0] = jnp.zeros((8, d), F32)
        part_ref[0, 0:1, :] = dshift
        part_ref[0, 1:2, :] = dscale
        part_ref[0, 2:3, :] = dgn

    row = lambda w: pl.BlockSpec((tm, w), lambda i: (i, 0))
    full = lambda a: pl.BlockSpec(a.shape, lambda i: (0,) * a.ndim)
    nt = t // tm
    return _pcall(
        body, name="in_proj_bwd", grid=(nt,),
        out_shape=[jax.ShapeDtypeStruct((t, d), F32), jax.ShapeDtypeStruct((t, 6 * dg), BF16),
                   jax.ShapeDtypeStruct((t, LANES), BF16), jax.ShapeDtypeStruct((nt, 8, d), F32)],
        in_specs=[pl.BlockSpec((dg, tm), lambda i: (0, i)), row(dg), row(dg)] * 2
        + [row(LANES), full(w_qkv), full(w_f), row(d), row(d), full(mod8), full(g_attn)],
        out_specs=[row(d), row(6 * dg), row(LANES), pl.BlockSpec((1, 8, d), lambda i: (i, 0, 0))],
        compiler_params=_params(("arbitrary",)),
    )(*dparts, dfl, w_qkv, w_f, x, dx2, mod8, g_attn)


def _matmul_tn(a_t, b, name):
    m, t = a_t.shape
    n = b.shape[1]
    tm_ = _tile(m, (512, 256, 128))
    tn_ = _tile(n, (1024, 512, 256, 128))
    tk = _tile(t, (2048, 1024, 512, 256, 128))
    nk = t // tk

    def body(a_ref, b_ref, o_ref):
        k = pl.program_id(2)

        @pl.when(k == 0)
        def _():
            o_ref[...] = jnp.zeros_like(o_ref)

        o_ref[...] += jnp.dot(a_ref[...], b_ref[...], preferred_element_type=F32)

    return _pcall(
        body, name=name, grid=(m // tm_, n // tn_, nk),
        out_shape=jax.ShapeDtypeStruct((m, n), F32),
        in_specs=[pl.BlockSpec((tm_, tk), lambda i, j, k: (i, k)), pl.BlockSpec((tk, tn_), lambda i, j, k: (k, j))],
        out_specs=pl.BlockSpec((tm_, tn_), lambda i, j, k: (i, j)),
        compiler_params=_params(("arbitrary", "arbitrary", "arbitrary")),
    )(a_t, b)


HALO = 16


def _conv_taps(up_ext, cw, lo, rows):
    s1 = pltpu.roll(up_ext, 1, 0)
    s2 = pltpu.roll(up_ext, 2, 0)
    u = cw[2:3, :] * up_ext[lo:lo + rows] + cw[1:2, :] * s1[lo:lo + rows] + cw[0:1, :] * s2[lo:lo + rows] + cw[3:4, :]
    return u, s1, s2


def _ffn_fwd(x2, target, mod8, g_mlp, g_final, wg, wv, cwg, cwv, wd, tm, cf):
    t, d = x2.shape
    dfp = wg.shape[1]
    nt, nc = t // tm, dfp // cf
    hb = tm // HALO

    def body(x_ref, xp_ref, tg_ref, mod_ref, g_ref, gf_ref, wg_ref, wv_ref, cg_ref, cv_ref, wd_ref,
             dx3_ref, h2_ref, part_ref, h_sc, acc_sc):
        i, j = pl.program_id(0), pl.program_id(1)

        @pl.when(j == 0)
        def _():
            xe = jnp.concatenate([xp_ref[...], x_ref[...]], axis=0)
            h = _norm_mod(xe, g_ref[...], mod_ref[4:5, :], mod_ref[3:4, :]).astype(BF16)
            h_sc[...] = h
            h2_ref[...] = h[HALO:]
            acc_sc[...] = jnp.zeros_like(acc_sc)

        rowi = lax.broadcasted_iota(jnp.int32, (tm + HALO, 1), 0)
        keep = (rowi >= HALO) | (i > 0)
        hv = h_sc[...]
        upg = jnp.where(keep, jnp.dot(hv, wg_ref[...], preferred_element_type=F32), 0.0)
        upv = jnp.where(keep, jnp.dot(hv, wv_ref[...], preferred_element_type=F32), 0.0)
        ug, _, _ = _conv_taps(upg, cg_ref[...], HALO, tm)
        uv, _, _ = _conv_taps(upv, cv_ref[...], HALO, tm)
        act = (ug * jax.nn.sigmoid(ug) * uv).astype(BF16)
        acc_sc[...] += jnp.dot(act, wd_ref[...], preferred_element_type=F32)

        @pl.when(j == nc - 1)
        def _():
            y_ffn = acc_sc[...]
            x3 = x_ref[...] + mod_ref[5:6, :] * y_ffn
            r3 = lax.rsqrt(jnp.mean(x3 * x3, axis=-1, keepdims=True) + EPS)
            xn = x3 * r3
            gf = gf_ref[...]
            diff = xn * gf - tg_ref[...]
            dy = diff * (1.0 / d)
            dxn = dy * gf
            dx3 = r3 * (dxn - xn * jnp.mean(dxn * xn, axis=-1, keepdims=True))
            dx3_ref[...] = dx3
            part_ref[0] = jnp.zeros((8, d), F32)
            part_ref[0, 0:1, :] = jnp.sum(dy * xn, axis=0, keepdims=True)
            part_ref[0, 1:2, :] = jnp.sum(dx3 * y_ffn, axis=0, keepdims=True)
            part_ref[0, 2:3, :] = jnp.sum(diff * diff, axis=0, keepdims=True) * (0.5 / d)

    row = lambda w: pl.BlockSpec((tm, w), lambda i, j: (i, 0))
    full = lambda a: pl.BlockSpec(a.shape, lambda i, j: (0,) * a.ndim)
    return _pcall(
        body, name="ffn_fwd", grid=(nt, nc),
        out_shape=[jax.ShapeDtypeStruct((t, d), F32), jax.ShapeDtypeStruct((t, d), BF16),
                   jax.ShapeDtypeStruct((nt, 8, d), F32)],
        in_specs=[row(d), pl.BlockSpec((HALO, d), lambda i, j: (jnp.maximum(i * hb - 1, 0), 0)), row(d),
                  full(mod8), full(g_mlp), full(g_final),
                  pl.BlockSpec((d, cf), lambda i, j: (0, j)), pl.BlockSpec((d, cf), lambda i, j: (0, j)),
                  pl.BlockSpec((8, cf), lambda i, j: (0, j)), pl.BlockSpec((8, cf), lambda i, j: (0, j)),
                  pl.BlockSpec((cf, d), lambda i, j: (j, 0))],
        out_specs=[row(d), row(d), pl.BlockSpec((1, 8, d), lambda i, j: (i, 0, 0))],
        scratch_shapes=[pltpu.VMEM((tm + HALO, d), BF16), pltpu.VMEM((tm, d), F32)],
        compiler_params=_params(("arbitrary", "arbitrary")),
    )(x2, x2, target, mod8, g_mlp, g_final, wg, wv, cwg, cwv, wd)


def _chunk_major(w, cf):
    d, n = w.shape[0], w.shape[1] // cf
    return jnp.transpose(w.reshape(d, n, cf), (1, 0, 2))


def _ffn_fwd(x2, target, mod8, g_mlp, g_final, wg, wv, cwg, cwv, wd, tm, cf):
    t, d = x2.shape
    dfp = wg.shape[1]
    nt, nc = t // tm, dfp // cf
    hb = tm // HALO
    wg_c, wv_c = _chunk_major(wg, cf), _chunk_major(wv, cf)

    def body(x_ref, xp_ref, tg_ref, mod_ref, g_ref, gf_ref, wg_ref, wv_ref, cg_ref, cv_ref, wd_ref,
             dx3_ref, h2_ref, part_ref, act_sc):
        i = pl.program_id(0)
        xe = jnp.concatenate([xp_ref[...], x_ref[...]], axis=0)
        h = _norm_mod(xe, g_ref[...], mod_ref[4:5, :], mod_ref[3:4, :]).astype(BF16)
        h2_ref[...] = _transposed(h[HALO:])
        first = jnp.where(i > 0, h[:HALO], jnp.zeros_like(h[:HALO]))
        h = jnp.concatenate([first, h[HALO:]], axis=0)

        def up(c):
            return (jnp.dot(h, wg_ref[c], preferred_element_type=F32), jnp.dot(h, wv_ref[c], preferred_element_type=F32))

        def activation(c, ups):
            cols = slice(c * cf, (c + 1) * cf)
            ug, _, _ = _conv_taps(ups[0], cg_ref[:, cols], HALO, tm)
            uv, _, _ = _conv_taps(ups[1], cv_ref[:, cols], HALO, tm)
            act_sc[:, cols] = (ug * jax.nn.sigmoid(ug) * uv).astype(BF16)

        for c0 in range(0, nc, 2):
            group = list(range(c0, min(c0 + 2, nc)))
            ups = [up(c) for c in group]
            for c, u in zip(group, ups):
                activation(c, u)

        y_ffn = jnp.dot(act_sc[...], wd_ref[...], preferred_element_type=F32)
        x3 = x_ref[...] + mod_ref[5:6, :] * y_ffn
        r3 = lax.rsqrt(jnp.mean(x3 * x3, axis=-1, keepdims=True) + EPS)
        xn = x3 * r3
        gf = gf_ref[...]
        diff = xn * gf - tg_ref[...]
        dy = diff * (1.0 / d)
        dxn = dy * gf
        dx3 = r3 * (dxn - xn * jnp.mean(dxn * xn, axis=-1, keepdims=True))
        dx3_ref[...] = dx3
        part_ref[0] = jnp.zeros((8, d), F32)
        part_ref[0, 0:1, :] = jnp.sum(dy * xn, axis=0, keepdims=True)
        part_ref[0, 1:2, :] = jnp.sum(dx3 * y_ffn, axis=0, keepdims=True)
        part_ref[0, 2:3, :] = jnp.sum(diff * diff, axis=0, keepdims=True) * (0.5 / d)

    row = lambda w: pl.BlockSpec((tm, w), lambda i: (i, 0))
    full = lambda a: pl.BlockSpec(a.shape, lambda i: (0,) * a.ndim)
    once = lambda a: pl.BlockSpec(a.shape, lambda i: (0,) * a.ndim, pipeline_mode=pl.Buffered(1))
    return _pcall(
        body, name="ffn_fwd", grid=(nt,),
        out_shape=[jax.ShapeDtypeStruct((t, d), F32), jax.ShapeDtypeStruct((d, t), BF16),
                   jax.ShapeDtypeStruct((nt, 8, d), F32)],
        in_specs=[row(d), pl.BlockSpec((HALO, d), lambda i: (jnp.maximum(i * hb - 1, 0), 0)), row(d),
                  full(mod8), full(g_mlp), full(g_final), once(wg_c), once(wv_c), once(cwg), once(cwv), once(wd)],
        out_specs=[row(d), pl.BlockSpec((d, tm), lambda i: (0, i)), pl.BlockSpec((1, 8, d), lambda i: (i, 0, 0))],
        scratch_shapes=[pltpu.VMEM((tm, dfp), BF16)],
        compiler_params=_params(("arbitrary",)),
    )(x2, x2, target, mod8, g_mlp, g_final, wg_c, wv_c, cwg, cwv, wd)


def _ffn_bwd(x2, dx3, mod8, g_mlp, wg, wv, cwg, cwv, wd, tm, cf):
    t, d = x2.shape
    dfp = wg.shape[1]
    nt, nc = t // tm, dfp // cf
    hb = tm // HALO
    nhb = t // HALO
    ext = tm + 2 * HALO

    def body(x_ref, xp_ref, xn_ref, dx_ref, dxn_ref, mod_ref, g_ref, wg_ref, wv_ref, cg_ref, cv_ref, wd_ref,
             dx2_ref, dug_ref, duv_ref, act_ref, dxg_ref, part_ref, pcg_ref, pcv_ref, h_sc, dg_sc, dh_sc):
        i, j = pl.program_id(0), pl.program_id(1)

        @pl.when(j == 0)
        def _():
            xe = jnp.concatenate([xp_ref[...], x_ref[...], xn_ref[...]], axis=0)
            h_sc[...] = _norm_mod(xe, g_ref[...], mod_ref[4:5, :], mod_ref[3:4, :]).astype(BF16)
            de = (jnp.concatenate([dx_ref[...], dxn_ref[...]], axis=0) * mod_ref[5:6, :]).astype(BF16)
            dg_sc[...] = de
            dxg_ref[...] = de[:tm]
            dh_sc[...] = jnp.zeros_like(dh_sc)

        rowe = lax.broadcasted_iota(jnp.int32, (ext, 1), 0)
        keep_up = (rowe >= HALO) | (i > 0)
        rowu = lax.broadcasted_iota(jnp.int32, (tm + HALO, 1), 0)
        keep_du = (rowu < tm) | (i < nt - 1)
        hv = h_sc[...]
        upg = jnp.where(keep_up, jnp.dot(hv, wg_ref[...], preferred_element_type=F32), 0.0)
        upv = jnp.where(keep_up, jnp.dot(hv, wv_ref[...], preferred_element_type=F32), 0.0)
        cg, cv = cg_ref[...], cv_ref[...]
        ug, g1, g2 = _conv_taps(upg, cg, HALO, tm + HALO)
        uv, v1, v2 = _conv_taps(upv, cv, HALO, tm + HALO)
        dact = lax.dot_general(dg_sc[...], wd_ref[...], NT_DIMS, preferred_element_type=F32)
        sg = jax.nn.sigmoid(ug)
        sil = ug * sg
        act_ref[...] = (sil * uv)[:tm].astype(BF16)
        duv = jnp.where(keep_du, dact * sil, 0.0)
        dug = jnp.where(keep_du, dact * uv * (sg * (1.0 + ug * (1.0 - sg))), 0.0)

        def back(du, cw, up, s1, s2, pc_ref):
            n = tm + HALO
            dup = (cw[2:3, :] * du + cw[1:2, :] * pltpu.roll(du, n - 1, 0) + cw[0:1, :] * pltpu.roll(du, n - 2, 0))[:tm]
            dut = du[:tm]
            pc_ref[0] = jnp.zeros((8, cf), F32)
            pc_ref[0, 0:1, :] = jnp.sum(dut * s2[HALO:HALO + tm], axis=0, keepdims=True)
            pc_ref[0, 1:2, :] = jnp.sum(dut * s1[HALO:HALO + tm], axis=0, keepdims=True)
            pc_ref[0, 2:3, :] = jnp.sum(dut * up[HALO:HALO + tm], axis=0, keepdims=True)
            pc_ref[0, 3:4, :] = jnp.sum(dut, axis=0, keepdims=True)
            return dup.astype(BF16)

        dupg = back(dug, cg, upg, g1, g2, pcg_ref)
        dupv = back(duv, cv, upv, v1, v2, pcv_ref)
        dug_ref[...] = dupg
        duv_ref[...] = dupv
        dh_sc[...] += (lax.dot_general(dupg, wg_ref[...], NT_DIMS, preferred_element_type=F32)
                       + lax.dot_general(dupv, wv_ref[...], NT_DIMS, preferred_element_type=F32))

        @pl.when(j == nc - 1)
        def _():
            dx, dshift, dscale, dgn = _norm_mod_bwd(x_ref[...], dh_sc[...], g_ref[...], mod_ref[4:5, :])
            dx2_ref[...] = dx_ref[...] + dx
            part_ref[0] = jnp.zeros((8, d), F32)
            part_ref[0, 0:1, :] = dshift
            part_ref[0, 1:2, :] = dscale
            part_ref[0, 2:3, :] = dgn

    row = lambda w: pl.BlockSpec((tm, w), lambda i, j: (i, 0))
    prev = pl.BlockSpec((HALO, d), lambda i, j: (jnp.maximum(i * hb - 1, 0), 0))
    nxt = pl.BlockSpec((HALO, d), lambda i, j: (jnp.minimum((i + 1) * hb, nhb - 1), 0))
    full = lambda a: pl.BlockSpec(a.shape, lambda i, j: (0,) * a.ndim)
    chunk = pl.BlockSpec((tm, cf), lambda i, j: (i, j))
    pchunk = pl.BlockSpec((1, 8, cf), lambda i, j: (i, 0, j))
    return _pcall(
        body, name="ffn_bwd", grid=(nt, nc),
        out_shape=[jax.ShapeDtypeStruct((t, d), F32), jax.ShapeDtypeStruct((t, dfp), BF16),
                   jax.ShapeDtypeStruct((t, dfp), BF16), jax.ShapeDtypeStruct((t, dfp), BF16),
                   jax.ShapeDtypeStruct((t, d), BF16), jax.ShapeDtypeStruct((nt, 8, d), F32),
                   jax.ShapeDtypeStruct((nt, 8, dfp), F32), jax.ShapeDtypeStruct((nt, 8, dfp), F32)],
        in_specs=[row(d), prev, nxt, row(d), nxt, full(mod8), full(g_mlp),
                  pl.BlockSpec((d, cf), lambda i, j: (0, j)), pl.BlockSpec((d, cf), lambda i, j: (0, j)),
                  pl.BlockSpec((8, cf), lambda i, j: (0, j)), pl.BlockSpec((8, cf), lambda i, j: (0, j)),
                  pl.BlockSpec((cf, d), lambda i, j: (j, 0))],
        out_specs=[row(d), chunk, chunk, chunk, row(d), pl.BlockSpec((1, 8, d), lambda i, j: (i, 0, 0)), pchunk, pchunk],
        scratch_shapes=[pltpu.VMEM((ext, d), BF16), pltpu.VMEM((tm + HALO, d), BF16), pltpu.VMEM((tm, d), F32)],
        compiler_params=_params(("arbitrary", "arbitrary")),
    )(x2, x2, x2, dx3, dx3, mod8, g_mlp, wg, wv, cwg, cwv, wd)


def _ffn_bwd(x2, dx3, mod8, g_mlp, wg, wv, cwg, cwv, wd, tm, cf):
    t, d = x2.shape
    dfp = wg.shape[1]
    nt, nc = t // tm, dfp // cf
    hb = tm // HALO
    nhb = t // HALO
    n = tm + HALO

    def body(x_ref, xp_ref, xn_ref, dx_ref, dxn_ref, mod_ref, g_ref, wg_ref, wv_ref, cg_ref, cv_ref, wd_ref,
             dx2_ref, dug_ref, duv_ref, act_ref, dxg_ref, part_ref, pcg_ref, pcv_ref):
        i = pl.program_id(0)
        xe = jnp.concatenate([xp_ref[...], x_ref[...], xn_ref[...]], axis=0)
        h = _norm_mod(xe, g_ref[...], mod_ref[4:5, :], mod_ref[3:4, :]).astype(BF16)
        h = jnp.concatenate([jnp.where(i > 0, h[:HALO], jnp.zeros_like(h[:HALO])), h[HALO:]], axis=0)
        dx = dx_ref[...] * mod_ref[5:6, :]
        dxn = jnp.where(i < nt - 1, dxn_ref[...] * mod_ref[5:6, :], 0.0)
        de = jnp.concatenate([dx, dxn], axis=0).astype(BF16)
        dxg_ref[...] = de[:tm]
        pcg_ref[0] = jnp.zeros((8, dfp), F32)
        pcv_ref[0] = jnp.zeros((8, dfp), F32)

        def products(c):
            cols = slice(c * cf, (c + 1) * cf)
            return (jnp.dot(h, wg_ref[:, cols], preferred_element_type=F32), jnp.dot(h, wv_ref[:, cols], preferred_element_type=F32),
                    lax.dot_general(de, wd_ref[cols, :], NT_DIMS, preferred_element_type=F32))

        def back(du, cw, up, s1, s2, pc_ref, cols):
            dup = (cw[2:3, :] * du + cw[1:2, :] * pltpu.roll(du, n - 1, 0) + cw[0:1, :] * pltpu.roll(du, n - 2, 0))[:tm]
            dut = du[:tm]
            pc_ref[0, 0:1, cols] = jnp.sum(dut * s2[HALO:HALO + tm], axis=0, keepdims=True)
            pc_ref[0, 1:2, cols] = jnp.sum(dut * s1[HALO:HALO + tm], axis=0, keepdims=True)
            pc_ref[0, 2:3, cols] = jnp.sum(dut * up[HALO:HALO + tm], axis=0, keepdims=True)
            pc_ref[0, 3:4, cols] = jnp.sum(dut, axis=0, keepdims=True)
            return dup.astype(BF16)

        def chunk(c, prods):
            cols = slice(c * cf, (c + 1) * cf)
            upg, upv, dact = prods
            cg, cv = cg_ref[:, cols], cv_ref[:, cols]
            ug, g1, g2 = _conv_taps(upg, cg, HALO, n)
            uv, v1, v2 = _conv_taps(upv, cv, HALO, n)
            sg = jax.nn.sigmoid(ug)
            sil = ug * sg
            act_ref[cols, :] = _transposed((sil * uv)[:tm].astype(BF16))
            dug_ref[:, cols] = back(dact * uv * (sg * (1.0 + ug * (1.0 - sg))), cg, upg, g1, g2, pcg_ref, cols)
            duv_ref[:, cols] = back(dact * sil, cv, upv, v1, v2, pcv_ref, cols)

        for c0 in range(0, nc, 2):
            group = list(range(c0, min(c0 + 2, nc)))
            prods = [products(c) for c in group]
            for c, pr in zip(group, prods):
                chunk(c, pr)

        dh = (lax.dot_general(dug_ref[...], wg_ref[...], NT_DIMS, preferred_element_type=F32)
              + lax.dot_general(duv_ref[...], wv_ref[...], NT_DIMS, preferred_element_type=F32))
        dxt, dshift, dscale, dgn = _norm_mod_bwd(x_ref[...], dh, g_ref[...], mod_ref[4:5, :])
        dx2_ref[...] = dx_ref[...] + dxt
        part_ref[0] = jnp.zeros((8, d), F32)
        part_ref[0, 0:1, :] = dshift
        part_ref[0, 1:2, :] = dscale
        part_ref[0, 2:3, :] = dgn

    row = lambda w: pl.BlockSpec((tm, w), lambda i: (i, 0))
    prev = pl.BlockSpec((HALO, d), lambda i: (jnp.maximum(i * hb - 1, 0), 0))
    nxt = pl.BlockSpec((HALO, d), lambda i: (jnp.minimum((i + 1) * hb, nhb - 1), 0))
    full = lambda a: pl.BlockSpec(a.shape, lambda i: (0,) * a.ndim)
    once = lambda a: pl.BlockSpec(a.shape, lambda i: (0,) * a.ndim, pipeline_mode=pl.Buffered(1))
    part = lambda w: pl.BlockSpec((1, 8, w), lambda i: (i, 0, 0))
    return _pcall(
        body, name="ffn_bwd", grid=(nt,),
        out_shape=[jax.ShapeDtypeStruct((t, d), F32), jax.ShapeDtypeStruct((t, dfp), BF16),
                   jax.ShapeDtypeStruct((t, dfp), BF16), jax.ShapeDtypeStruct((dfp, t), BF16),
                   jax.ShapeDtypeStruct((t, d), BF16), jax.ShapeDtypeStruct((nt, 8, d), F32),
                   jax.ShapeDtypeStruct((nt, 8, dfp), F32), jax.ShapeDtypeStruct((nt, 8, dfp), F32)],
        in_specs=[row(d), prev, nxt, row(d), nxt, full(mod8), full(g_mlp), once(wg), once(wv), once(cwg), once(cwv), once(wd)],
        out_specs=[row(d), row(dfp), row(dfp), pl.BlockSpec((dfp, tm), lambda i: (0, i)), row(d), part(d), part(dfp), part(dfp)],
        compiler_params=_params(("arbitrary",)),
    )(x2, x2, x2, dx3, dx3, mod8, g_mlp, wg, wv, cwg, cwv, wd)


def _head_masks():
    lane = lax.broadcasted_iota(jnp.int32, (1, LANES), 1)
    in_a = lane < HEAD_DIM
    return in_a, jnp.logical_not(in_a)


BLK = 2 * LANES


def _stack_heads(qkv, dg):
    t = qkv.shape[0]
    p = dg // LANES
    rows = _tile(t, (512, 256, 128))
    sub = rows // BLK

    def body(kf_ref, vf_ref, ks_ref, vs_ref, okf, ovf, oks, ovs):
        in_a, in_b = _head_masks()
        for src, dst in ((kf_ref, okf), (vf_ref, ovf), (ks_ref, oks), (vs_ref, ovs)):
            v = src[...]
            zero = jnp.zeros_like(v)
            va, vb = jnp.where(in_a, v, zero), jnp.where(in_b, v, zero)
            for s in range(sub):
                dst[0, s, :BLK, :] = va[s * BLK:(s + 1) * BLK]
                dst[0, s, BLK:, :] = vb[s * BLK:(s + 1) * BLK]

    col = lambda base: pl.BlockSpec((rows, LANES), lambda h, j: (j, base * p + h))
    out = pl.BlockSpec((1, sub, 2 * BLK, LANES), lambda h, j: (h, j, 0, 0))
    shape = jax.ShapeDtypeStruct((p, t // BLK, 2 * BLK, LANES), BF16)
    return _pcall(
        body, name="stack_heads", grid=(p, t // rows),
        out_shape=[shape] * 4, in_specs=[col(1), col(2), col(4), col(5)], out_specs=[out] * 4,
        compiler_params=_params(("arbitrary", "arbitrary")),
    )(qkv, qkv, qkv, qkv)


def _tile_masks():
    rowi = lax.broadcasted_iota(jnp.int32, (BLK, BLK), 0)
    coli = lax.broadcasted_iota(jnp.int32, (BLK, BLK), 1)
    return coli <= rowi, coli < rowi


def _pair_triangle(suffix):
    r = lax.broadcasted_iota(jnp.int32, (BLK, BLK), 0)
    c = lax.broadcasted_iota(jnp.int32, (BLK, BLK), 1)
    return ((r >= c) if suffix else (r <= c)).astype(BF16)


def _pair_cumsum(x2, tri, passes):
    return jnp.concatenate([_split_dot(x2[:, :BLK], tri, passes), _split_dot(x2[:, BLK:], tri, passes)], axis=1)


def _pair_specs(t, dg, base):
    p = dg // LANES
    q = pl.BlockSpec((BLK, LANES), lambda h, i: (i, base * p + h))
    kv = pl.BlockSpec((1, t // BLK, 2 * BLK, LANES), lambda h, i: (h, 0, 0, 0))
    return q, kv


def _fox_fwd(qkv, kst, vst, fcol, frow2, dg):
    t = qkv.shape[0]
    p, nq = dg // LANES, t // BLK
    nh = 2 * p

    def body(q_ref, k_ref, v_ref, ft_ref, fs_ref, o_ref, lse_ref):
        i = pl.program_id(1)
        in_a, _ = _head_masks()
        causal, _ = _tile_masks()
        q2 = q_ref[...]
        ft = tuple(jnp.broadcast_to(ft_ref[a], (BLK, BLK)) for a in range(2))

        def tile(j, carry, masked):
            m, l, acc = carry
            kb, vb = k_ref[0, j], v_ref[0, j]
            s2 = lax.dot_general(q2, kb, NT_DIMS, preferred_element_type=F32)
            fs = fs_ref[0, j]
            m_new, l_new, alpha, pr = [], [], [], []
            for a in range(2):
                sl = slice(a * BLK, (a + 1) * BLK)
                s = (s2[:, sl] + ft[a]) - fs[:, sl]
                if masked:
                    s = jnp.where(causal, s, NEG)
                mn = jnp.maximum(m[a], jnp.max(s, axis=1, keepdims=True))
                pa = jnp.exp(s - mn)
                al = jnp.exp(m[a] - mn)
                m_new.append(mn)
                alpha.append(al)
                l_new.append(al * l[a] + jnp.sum(pa, axis=1, keepdims=True))
                pr.append(pa.astype(BF16))
            acc = jnp.where(in_a, alpha[0], alpha[1]) * acc + jnp.dot(
                jnp.concatenate(pr, axis=1), vb, preferred_element_type=F32)
            return tuple(m_new), tuple(l_new), acc

        neg, zero = jnp.full((BLK, 1), NEG, F32), jnp.zeros((BLK, 1), F32)
        carry = lax.fori_loop(0, i, functools.partial(tile, masked=False), ((neg, neg), (zero, zero), jnp.zeros((BLK, LANES), F32)))
        m, l, acc = tile(i, carry, True)
        o_ref[...] = acc / jnp.where(in_a, l[0], l[1])
        lse_ref[0] = m[0] + jnp.log(l[0])
        lse_ref[1] = m[1] + jnp.log(l[1])

    qs, kv = _pair_specs(t, dg, 0)
    col = pl.BlockSpec((2, BLK, 1), lambda h, i: (h, i, 0))
    return _pcall(
        body, name="fox_fwd", grid=(p, nq),
        out_shape=[jax.ShapeDtypeStruct((t, dg), F32), jax.ShapeDtypeStruct((nh, t, 1), F32)],
        in_specs=[qs, kv, kv, col, pl.BlockSpec((1, nq, 1, 2 * BLK), lambda h, i: (h, 0, 0, 0))],
        out_specs=[pl.BlockSpec((BLK, LANES), lambda h, i: (i, h)), col],
        compiler_params=_params(("arbitrary", "arbitrary")),
    )(qkv, kst, vst, fcol, frow2)


def _fold_heads(stacked, in_a):
    return jnp.where(in_a, stacked[:BLK], stacked[BLK:])


def _fox_bwd(qkv, kst, vst, do, fcol, frow2, lse, delta, dg):
    t = qkv.shape[0]
    p, nq = dg // LANES, t // BLK
    nh = 2 * p

    def body(q_ref, k_ref, v_ref, do_ref, ft_ref, fs_ref, lse_ref, dl_ref, dq_ref, dk_ref, dv_ref, dfs_ref, dft_ref):
        i = pl.program_id(1)

        @pl.when(i == 0)
        def _():
            dk_ref[...] = jnp.zeros_like(dk_ref)
            dv_ref[...] = jnp.zeros_like(dv_ref)
            dfs_ref[...] = jnp.zeros_like(dfs_ref)

        in_a, _ = _head_masks()
        causal, _ = _tile_masks()
        q2, do2 = q_ref[...], do_ref[...]
        ft = tuple(jnp.broadcast_to(ft_ref[a] - lse_ref[a], (BLK, BLK)) for a in range(2))
        dl = tuple(jnp.broadcast_to(dl_ref[a], (BLK, BLK)) for a in range(2))

        def tile(j, carry, masked):
            dq, dft = carry
            kb, vb = k_ref[0, j], v_ref[0, j]
            s2 = lax.dot_general(q2, kb, NT_DIMS, preferred_element_type=F32)
            dp2 = lax.dot_general(do2, vb, NT_DIMS, preferred_element_type=F32)
            fs = fs_ref[0, j]
            pr, ds, dft_new = [], [], []
            for a in range(2):
                sl = slice(a * BLK, (a + 1) * BLK)
                s = (s2[:, sl] + ft[a]) - fs[:, sl]
                if masked:
                    s = jnp.where(causal, s, NEG)
                pa = jnp.exp(s)
                dsa = pa * (dp2[:, sl] - dl[a])
                pr.append(pa.astype(BF16))
                ds.append(dsa)
                dft_new.append(dft[a] + jnp.sum(dsa, axis=1, keepdims=True))
            ds2 = jnp.concatenate(ds, axis=1)
            dsb = ds2.astype(BF16)
            off = pl.multiple_of(j * BLK, BLK)
            dk_ref[pl.ds(off, BLK), :] += _fold_heads(lax.dot_general(dsb, q2, TN_DIMS, preferred_element_type=F32), in_a)
            dv_ref[pl.ds(off, BLK), :] += _fold_heads(
                lax.dot_general(jnp.concatenate(pr, axis=1), do2, TN_DIMS, preferred_element_type=F32), in_a)
            dfs_ref[0, j] += -jnp.sum(ds2, axis=0, keepdims=True)
            return dq + jnp.dot(dsb, kb, preferred_element_type=F32), tuple(dft_new)

        zero = jnp.zeros((BLK, 1), F32)
        carry = lax.fori_loop(0, i, functools.partial(tile, masked=False), (jnp.zeros((BLK, LANES), F32), (zero, zero)))
        dq, dft = tile(i, carry, True)
        dq_ref[...] = dq
        dft_ref[0] = dft[0]
        dft_ref[1] = dft[1]

    qs, kv = _pair_specs(t, dg, 0)
    col = pl.BlockSpec((2, BLK, 1), lambda h, i: (h, i, 0))
    rowspec = pl.BlockSpec((1, nq, 1, 2 * BLK), lambda h, i: (h, 0, 0, 0))
    blk = pl.BlockSpec((BLK, LANES), lambda h, i: (i, h))
    acc = pl.BlockSpec((t, LANES), lambda h, i: (0, h))
    return _pcall(
        body, name="fox_bwd", grid=(p, nq),
        out_shape=[jax.ShapeDtypeStruct((t, dg), F32)] * 3 + [jax.ShapeDtypeStruct((p, nq, 1, 2 * BLK), F32),
                                                              jax.ShapeDtypeStruct((nh, t, 1), F32)],
        in_specs=[qs, kv, kv, blk, col, rowspec, col, col],
        out_specs=[blk, acc, acc, rowspec, col],
        compiler_params=_params(("arbitrary", "arbitrary")),
    )(qkv, kst, vst, do, fcol, frow2, lse, delta)


def _softplus_parts(z):
    e = jnp.exp(-jnp.abs(z))
    return jnp.maximum(z, 0.0) + jnp.log(1.0 + e), e


def _sigmoid_from(z, e):
    d = 1.0 + e
    r = pl.reciprocal(d, approx=True)
    r = r * (2.0 - d * r)
    return jnp.where(z >= 0, 1.0, e) * r


def _sb_fwd(qkv, kst, vst, dg):
    t = qkv.shape[0]
    p, nq = dg // LANES, t // BLK
    nh = 2 * p

    def body(q_ref, k_ref, v_ref, o_ref, rt_ref):
        i = pl.program_id(1)
        _, strict = _tile_masks()
        strict2 = jnp.concatenate([strict, strict], axis=1)
        suffix = _pair_triangle(True)
        q2 = q_ref[...]

        def tile(j, carry, masked):
            rest, acc = carry
            kb, vb = k_ref[0, j], v_ref[0, j]
            z = lax.dot_general(q2, kb, NT_DIMS, preferred_element_type=F32)
            sp, _ = _softplus_parts(z)
            if masked:
                sp = jnp.where(strict2, sp, 0.0)
            cs = _pair_cumsum(sp, suffix, 2)
            w, rest_new = [], []
            for a in range(2):
                sl = slice(a * BLK, (a + 1) * BLK)
                wa = jnp.exp(z[:, sl] - cs[:, sl] - rest[a])
                if masked:
                    wa = jnp.where(strict, wa, 0.0)
                w.append(wa.astype(BF16))
                rest_new.append(rest[a] + cs[:, a * BLK:a * BLK + 1])
            acc = acc + jnp.dot(jnp.concatenate(w, axis=1), vb, preferred_element_type=F32)
            return tuple(rest_new), acc

        zero = jnp.zeros((BLK, 1), F32)
        carry = tile(i, ((zero, zero), jnp.zeros((BLK, LANES), F32)), True)
        rest, acc = lax.fori_loop(0, i, lambda jj, c: tile(i - 1 - jj, c, False), carry)
        o_ref[...] = acc
        rt_ref[0] = rest[0]
        rt_ref[1] = rest[1]

    qs, kv = _pair_specs(t, dg, 3)
    col = pl.BlockSpec((2, BLK, 1), lambda h, i: (h, i, 0))
    return _pcall(
        body, name="sb_fwd", grid=(p, nq),
        out_shape=[jax.ShapeDtypeStruct((t, dg), F32), jax.ShapeDtypeStruct((nh, t, 1), F32)],
        in_specs=[qs, kv, kv],
        out_specs=[pl.BlockSpec((BLK, LANES), lambda h, i: (i, h)), col],
        compiler_params=_params(("arbitrary", "arbitrary")),
    )(qkv, kst, vst)


def _sb_bwd(qkv, kst, vst, do, rtot, dg):
    t = qkv.shape[0]
    p, nq = dg // LANES, t // BLK

    def body(q_ref, k_ref, v_ref, do_ref, rt_ref, dq_ref, dk_ref, dv_ref):
        i = pl.program_id(1)

        @pl.when(i == 0)
        def _():
            dk_ref[...] = jnp.zeros_like(dk_ref)
            dv_ref[...] = jnp.zeros_like(dv_ref)

        in_a, _ = _head_masks()
        _, strict = _tile_masks()
        strict2 = jnp.concatenate([strict, strict], axis=1)
        prefix = _pair_triangle(False)
        q2, do2 = q_ref[...], do_ref[...]
        rt = (rt_ref[0], rt_ref[1])

        def tile(j, carry, masked):
            before, gbefore, dq = carry
            kb, vb = k_ref[0, j], v_ref[0, j]
            z = lax.dot_general(q2, kb, NT_DIMS, preferred_element_type=F32)
            da = lax.dot_general(do2, vb, NT_DIMS, preferred_element_type=F32)
            sp, e = _softplus_parts(z)
            sig = _sigmoid_from(z, e)
            if masked:
                sp = jnp.where(strict2, sp, 0.0)
            pre = _pair_cumsum(sp, prefix, 2)
            w = []
            for a in range(2):
                sl = slice(a * BLK, (a + 1) * BLK)
                wa = jnp.exp(z[:, sl] + (before[a] - rt[a]) + pre[:, sl] - sp[:, sl])
                if masked:
                    wa = jnp.where(strict, wa, 0.0)
                w.append(wa)
            w2 = jnp.concatenate(w, axis=1)
            g = w2 * da
            preg = _pair_cumsum(g, prefix, 1)
            dz = []
            for a in range(2):
                sl = slice(a * BLK, (a + 1) * BLK)
                dza = g[:, sl] * (1.0 - sig[:, sl]) - sig[:, sl] * (gbefore[a] + preg[:, sl] - g[:, sl])
                if masked:
                    dza = jnp.where(strict, dza, 0.0)
                dz.append(dza.astype(BF16))
            dzb = jnp.concatenate(dz, axis=1)
            off = pl.multiple_of(j * BLK, BLK)
            dk_ref[pl.ds(off, BLK), :] += _fold_heads(lax.dot_general(dzb, q2, TN_DIMS, preferred_element_type=F32), in_a)
            dv_ref[pl.ds(off, BLK), :] += _fold_heads(
                lax.dot_general(w2.astype(BF16), do2, TN_DIMS, preferred_element_type=F32), in_a)
            last = lambda x, a: x[:, (a + 1) * BLK - 1:(a + 1) * BLK]
            return (tuple(before[a] + last(pre, a) for a in range(2)),
                    tuple(gbefore[a] + last(preg, a) for a in range(2)),
                    dq + jnp.dot(dzb, kb, preferred_element_type=F32))

        zero = jnp.zeros((BLK, 1), F32)
        carry = lax.fori_loop(0, i, functools.partial(tile, masked=False), ((zero, zero), (zero, zero), jnp.zeros((BLK, LANES), F32)))
        dq_ref[...] = tile(i, carry, True)[2]

    qs, kv = _pair_specs(t, dg, 3)
    col = pl.BlockSpec((2, BLK, 1), lambda h, i: (h, i, 0))
    blk = pl.BlockSpec((BLK, LANES), lambda h, i: (i, h))
    acc = pl.BlockSpec((t, LANES), lambda h, i: (0, h))
    return _pcall(
        body, name="sb_bwd", grid=(p, nq),
        out_shape=[jax.ShapeDtypeStruct((t, dg), F32)] * 3,
        in_specs=[qs, kv, kv, blk, col],
        out_specs=[blk, acc, acc],
        compiler_params=_params(("arbitrary", "arbitrary")),
    )(qkv, kst, vst, do, rtot)


XROWS = 144
LANE_FS, LANE_FT_A, LANE_FT_B = 0, 3, 6


def _pieces3(x):
    hi = x.astype(BF16).astype(F32)
    r = x - hi
    mid = r.astype(BF16).astype(F32)
    return hi, mid, (r - mid).astype(BF16).astype(F32)


def _bias_lanes(rows, entries):
    sub = lax.broadcasted_iota(jnp.int32, (16, 1), 0)
    out = jnp.zeros((16, rows), F32)
    for l, v in entries:
        out = jnp.where(sub == l, v, out)
    return jnp.concatenate([out, jnp.zeros((LANES - 16, rows), F32)], axis=0).T


def _three(first, values):
    return [(first + k, v) for k, v in enumerate(values)]


def _stack_rows(x, in_a, in_b):
    zero = jnp.zeros_like(x)
    return jnp.concatenate([jnp.where(in_a, x, zero), jnp.where(in_b, x, zero)], axis=0)


def _transposed(x):
    return x.astype(F32).T.astype(BF16)


def _attn_operands(qkv, fcol, dg):
    t = qkv.shape[0]
    p, nk = dg // LANES, t // BLK

    def body(qf_ref, kf_ref, vf_ref, ks_ref, vs_ref, f_ref, qx_ref, kx_ref, kxt_ref, vf_o, vft_o, ks_o, kst_o, vs_o, vst_o):
        in_a, in_b = _head_masks()
        fa, fb = _pieces3(f_ref[0]), _pieces3(f_ref[1])
        qx_ref[0, :, :LANES] = qf_ref[...]
        qx_ref[0, :, LANES:] = _bias_lanes(
            BLK, _three(LANE_FS, (-1.0,) * 3) + _three(LANE_FT_A, fa) + _three(LANE_FT_B, fb)).astype(BF16)
        kf = kf_ref[...]
        zero = jnp.zeros_like(kf)
        top = jnp.concatenate([jnp.where(in_a, kf, zero), _bias_lanes(
            BLK, _three(LANE_FS, fa) + _three(LANE_FT_A, (1.0,) * 3)).astype(BF16)], axis=1)
        bot = jnp.concatenate([jnp.where(in_b, kf, zero), _bias_lanes(
            BLK, _three(LANE_FS, fb) + _three(LANE_FT_B, (1.0,) * 3)).astype(BF16)], axis=1)
        kx = jnp.concatenate([top, bot], axis=0)
        kx_ref[0, 0] = kx
        kxt_ref[0, 0] = _transposed(kx)[:XROWS]
        for src, dst, dst_t in ((vf_ref, vf_o, vft_o), (ks_ref, ks_o, kst_o), (vs_ref, vs_o, vst_o)):
            st = _stack_rows(src[...], in_a, in_b)
            dst[0, 0] = st
            dst_t[0, 0] = _transposed(st)

    col = lambda base: pl.BlockSpec((BLK, LANES), lambda h, j: (j, base * p + h))
    blk4 = lambda r, c: pl.BlockSpec((1, 1, r, c), lambda h, j: (h, j, 0, 0))
    shp4 = lambda r, c: jax.ShapeDtypeStruct((p, nk, r, c), BF16)
    return _pcall(
        body, name="attn_operands", grid=(p, nk),
        out_shape=[jax.ShapeDtypeStruct((p, t, 2 * LANES), BF16), shp4(2 * BLK, 2 * LANES), shp4(XROWS, 2 * BLK)]
        + [shp4(2 * BLK, LANES), shp4(LANES, 2 * BLK)] * 3,
        in_specs=[col(0), col(1), col(2), col(4), col(5), pl.BlockSpec((2, 1, BLK), lambda h, j: (h, 0, j))],
        out_specs=[pl.BlockSpec((1, BLK, 2 * LANES), lambda h, j: (h, j, 0)), blk4(2 * BLK, 2 * LANES), blk4(XROWS, 2 * BLK)]
        + [blk4(2 * BLK, LANES), blk4(LANES, 2 * BLK)] * 3,
        compiler_params=_params(("arbitrary", "arbitrary")),
    )(qkv, qkv, qkv, qkv, qkv, fcol)


def _fox_q_bwd(qkv, fcol, lse_col, dg):
    t = qkv.shape[0]
    p = dg // LANES

    def body(q_ref, f_ref, l_ref, qx_ref):
        fa, fb = _pieces3(f_ref[0] - l_ref[0]), _pieces3(f_ref[1] - l_ref[1])
        qx_ref[0, :, :LANES] = q_ref[...]
        qx_ref[0, :, LANES:] = _bias_lanes(
            BLK, _three(LANE_FS, (-1.0,) * 3) + _three(LANE_FT_A, fa) + _three(LANE_FT_B, fb)).astype(BF16)

    row = pl.BlockSpec((2, 1, BLK), lambda h, j: (h, 0, j))
    return _pcall(
        body, name="fox_q_bwd", grid=(p, t // BLK),
        out_shape=jax.ShapeDtypeStruct((p, t, 2 * LANES), BF16),
        in_specs=[pl.BlockSpec((BLK, LANES), lambda h, j: (j, h)), row, row],
        out_specs=pl.BlockSpec((1, BLK, 2 * LANES), lambda h, j: (h, j, 0)),
        compiler_params=_params(("arbitrary", "arbitrary")),
    )(qkv, fcol, lse_col)


def _key_query_masks():
    key = lax.broadcasted_iota(jnp.int32, (BLK, BLK), 0)
    qry = lax.broadcasted_iota(jnp.int32, (BLK, BLK), 1)
    return key <= qry, key < qry


def _key_triangle(kind):
    s = lax.broadcasted_iota(jnp.int32, (BLK, BLK), 0)
    j = lax.broadcasted_iota(jnp.int32, (BLK, BLK), 1)
    return {"suffix": j >= s, "prefix": j <= s, "before": j < s}[kind].astype(BF16)


def _tri_dot(tri, x, passes):
    acc = None
    for _ in range(passes):
        part = x.astype(BF16)
        d = jnp.dot(tri, part, preferred_element_type=F32)
        acc = d if acc is None else acc + d
        x = x - part.astype(F32)
    return acc


GROUPS = (4, 2, 1)


def _loop_blocks(n, tiles, carry, descending=False, groups=GROUPS):
    at = (lambda k: n - 1 - k) if descending else (lambda k: k)
    done = 0
    for g in groups:
        left = n - done
        carry = lax.fori_loop(0, left // g, lambda h, c, g=g, done=done: tiles([at(done + g * h + k) for k in range(g)], c), carry)
        done = done + (left // g) * g
    return carry


def _resident(shape):
    return pl.BlockSpec((1,) + shape, lambda h, i: (h,) + (0,) * len(shape), pipeline_mode=pl.Buffered(1))


def _rows_per_head(a, b):
    return jnp.concatenate([jnp.broadcast_to(a, (HEAD_DIM, BLK)), jnp.broadcast_to(b, (HEAD_DIM, BLK))], axis=0)


def _fold_heads(stacked, in_a):
    return jnp.where(in_a, stacked[:BLK], stacked[BLK:])


def _xy_gather_copies(ins, outs, send_sems, recv_sems, local_sems):
    x, y, c = lax.axis_index("x"), lax.axis_index("y"), lax.axis_index("c")
    chips = [(1 - x, y), (x, 1 - y), (1 - x, 1 - y)]
    mine = 2 * x + y
    local, remote = [], []
    for w in range(len(ins)):
        local.append(pltpu.make_async_copy(ins[w], outs[w].at[mine], local_sems.at[w]))
        for k, (px, py) in enumerate(chips):
            remote.append(pltpu.make_async_remote_copy(
                src_ref=ins[w], dst_ref=outs[w].at[mine], send_sem=send_sems.at[3 * w + k],
                recv_sem=recv_sems.at[3 * w + k], device_id=(px, py, c), device_id_type=MESH))
    return local, remote


def _fox_fwd(qx, kx, v_t, dg, shards):
    p, t = qx.shape[0], qx.shape[1]
    nq = t // BLK
    nh = 2 * p
    ns = len(shards)

    def body(q_ref, k_ref, vt_ref, *rest):
        shard_refs, (o_ref, lse_ref), gathered = rest[:ns], rest[ns:ns + 2], rest[ns + 2:2 * ns + 2]
        local, remote = _xy_gather_copies(shard_refs, gathered, *rest[2 * ns + 2:])
        i = pl.program_id(1)

        @pl.when((pl.program_id(0) == 0) & (i == 0))
        def _():
            for cp in local + remote:
                cp.start()

        causal, _ = _key_query_masks()
        q = q_ref[0]

        def scores(j, masked):
            s2 = lax.dot_general(k_ref[0, j], q, NT_DIMS, preferred_element_type=F32)
            s = [s2[a * BLK:(a + 1) * BLK] for a in range(2)]
            return [jnp.where(causal, x, NEG) for x in s] if masked else s

        def update(blocks, carry):
            m, l, acc = list(carry[0]), list(carry[1]), carry[2]
            for j, s in blocks:
                alpha, pr = [], []
                for a in range(2):
                    mn = jnp.maximum(m[a], jnp.max(s[a], axis=0, keepdims=True))
                    pa = jnp.exp(s[a] - mn)
                    al = jnp.exp(m[a] - mn)
                    l[a] = al * l[a] + jnp.sum(pa, axis=0, keepdims=True)
                    m[a] = mn
                    alpha.append(al)
                    pr.append(pa.astype(BF16))
                acc = _rows_per_head(*alpha) * acc + jnp.dot(vt_ref[0, j], jnp.concatenate(pr, axis=0), preferred_element_type=F32)
            return tuple(m), tuple(l), acc

        tiles = lambda js, c: update([(j, scores(j, False)) for j in js], c)
        neg, zero = jnp.full((1, BLK), NEG, F32), jnp.zeros((1, BLK), F32)
        carry = _loop_blocks(i, tiles, ((neg, neg), (zero, zero), jnp.zeros((LANES, BLK), F32)), groups=(8, 4, 2, 1))
        m, l, acc = update([(i, scores(i, True))], carry)
        o_ref[...] = acc / _rows_per_head(*l)
        lse_ref[0] = m[0] + jnp.log(l[0])
        lse_ref[1] = m[1] + jnp.log(l[1])

        @pl.when((pl.program_id(0) == p - 1) & (i == nq - 1))
        def _():
            for cp in remote:
                cp.wait_recv()
            for cp in remote:
                cp.wait_send()
            for cp in local:
                cp.wait()

    row = pl.BlockSpec((2, 1, BLK), lambda h, i: (h, 0, i))
    hbm = pl.BlockSpec(memory_space=pltpu.HBM)
    return _pcall(
        body, name="fox_fwd", grid=(p, nq),
        out_shape=[jax.ShapeDtypeStruct((dg, t), F32), jax.ShapeDtypeStruct((nh, 1, t), F32)]
        + [jax.ShapeDtypeStruct((4,) + s.shape, s.dtype) for s in shards],
        in_specs=[pl.BlockSpec((1, BLK, 2 * LANES), lambda h, i: (h, i, 0)), _resident((nq, 2 * BLK, 2 * LANES)),
                  _resident((nq, LANES, 2 * BLK))] + [hbm] * ns,
        out_specs=[pl.BlockSpec((LANES, BLK), lambda h, i: (h, i)), row] + [hbm] * ns,
        scratch_shapes=[pltpu.SemaphoreType.DMA((3 * ns,)), pltpu.SemaphoreType.DMA((3 * ns,)), pltpu.SemaphoreType.DMA((ns,))],
        compiler_params=_params(("arbitrary", "arbitrary")),
    )(qx, kx, v_t, *shards)


def _swap_copies(ins, outs, send_sems, recv_sems, local_sems):
    x, y, c = lax.axis_index("x"), lax.axis_index("y"), lax.axis_index("c")
    local, remote = [], []
    for w in range(len(ins)):
        local.append(pltpu.make_async_copy(ins[w], outs[w].at[c], local_sems.at[w]))
        remote.append(pltpu.make_async_remote_copy(
            src_ref=ins[w], dst_ref=outs[w].at[c], send_sem=send_sems.at[w], recv_sem=recv_sems.at[w],
            device_id=(x, y, 1 - c), device_id_type=MESH))
    return local, remote


def _fox_bwd(qxb, kx, kx_t, v_st, do, delta, dg, halves):
    p, t = qxb.shape[0], qxb.shape[1]
    nq = t // BLK
    nh = 2 * p
    ns = len(halves)

    def body(q_ref, k_ref, kt_ref, v_ref, do_ref, dl_ref, *rest):
        half_refs, (dq_ref, dft_ref, dk_ref, dv_ref, dkx_ref), both_refs = rest[:ns], rest[ns:ns + 5], rest[ns + 5:2 * ns + 5]
        local, remote = _swap_copies(half_refs, both_refs, *rest[2 * ns + 5:])
        i = pl.program_id(1)

        @pl.when((pl.program_id(0) == 0) & (i == 0))
        def _():
            for cp in local + remote:
                cp.start()

        @pl.when(i == 0)
        def _():
            dk_ref[...] = jnp.zeros_like(dk_ref)
            dv_ref[...] = jnp.zeros_like(dv_ref)
            dkx_ref[...] = jnp.zeros_like(dkx_ref)

        in_a, _ = _head_masks()
        first_lane = lax.broadcasted_iota(jnp.int32, (1, LANES), 1) == 0
        causal, _ = _key_query_masks()
        q, do2 = q_ref[0], do_ref[...]
        dl = (dl_ref[0], dl_ref[1])

        def products(j):
            return (lax.dot_general(k_ref[0, j], q, NT_DIMS, preferred_element_type=F32),
                    lax.dot_general(v_ref[0, j], do2, NT_DIMS, preferred_element_type=F32))

        def dscores(prod, masked):
            s2, dp2 = prod
            pr, ds = [], []
            for a in range(2):
                s = s2[a * BLK:(a + 1) * BLK]
                if masked:
                    s = jnp.where(causal, s, NEG)
                pa = jnp.exp(s)
                ds.append((pa * (dp2[a * BLK:(a + 1) * BLK] - dl[a])).astype(BF16))
                pr.append(pa.astype(BF16))
            return jnp.concatenate(ds, axis=0), jnp.concatenate(pr, axis=0)

        def accumulate(j, dsb, prb, dq):
            off = pl.multiple_of(j * BLK, BLK)
            dk_full = jnp.dot(dsb, q, preferred_element_type=F32)
            dk_ref[pl.ds(off, BLK), :] += _fold_heads(dk_full[:, :LANES], in_a)
            dkx_ref[pl.ds(off, BLK), :] += jnp.where(first_lane, dk_full[:BLK, LANES:], dk_full[BLK:, LANES:])
            dv_ref[pl.ds(off, BLK), :] += _fold_heads(jnp.dot(prb, do2, preferred_element_type=F32), in_a)
            return dq + jnp.dot(kt_ref[0, j], dsb, preferred_element_type=F32)

        def tiles(js, dq, masked=False):
            prods = [products(j) for j in js]
            grads = [dscores(pr, masked) for pr in prods]
            for j, (dsb, prb) in zip(js, grads):
                dq = accumulate(j, dsb, prb, dq)
            return dq

        dq = _loop_blocks(i, tiles, jnp.zeros((XROWS, BLK), F32))
        dq = tiles([i], dq, True)
        dq_ref[...] = dq[:LANES]
        dft_ref[0] = dq[LANES + LANE_FT_A:LANES + LANE_FT_A + 1]
        dft_ref[1] = dq[LANES + LANE_FT_B:LANES + LANE_FT_B + 1]

        @pl.when((pl.program_id(0) == p - 1) & (i == nq - 1))
        def _():
            for cp in remote:
                cp.wait_recv()
            for cp in remote:
                cp.wait_send()
            for cp in local:
                cp.wait()

    row = pl.BlockSpec((2, 1, BLK), lambda h, i: (h, 0, i))
    acc = pl.BlockSpec((t, LANES), lambda h, i: (0, h))
    hbm = pl.BlockSpec(memory_space=pltpu.HBM)
    return _pcall(
        body, name="fox_bwd", grid=(p, nq),
        out_shape=[jax.ShapeDtypeStruct((dg, t), F32), jax.ShapeDtypeStruct((nh, 1, t), F32)] + [jax.ShapeDtypeStruct((t, dg), F32)] * 3
        + [jax.ShapeDtypeStruct((2,) + h.shape, h.dtype) for h in halves],
        in_specs=[pl.BlockSpec((1, BLK, 2 * LANES), lambda h, i: (h, i, 0)), _resident((nq, 2 * BLK, 2 * LANES)),
                  _resident((nq, XROWS, 2 * BLK)), _resident((nq, 2 * BLK, LANES)),
                  pl.BlockSpec((BLK, LANES), lambda h, i: (i, h)), row] + [hbm] * ns,
        out_specs=[pl.BlockSpec((LANES, BLK), lambda h, i: (h, i)), row, acc, acc, acc] + [hbm] * ns,
        scratch_shapes=[pltpu.SemaphoreType.DMA((ns,)), pltpu.SemaphoreType.DMA((ns,)), pltpu.SemaphoreType.DMA((ns,))],
        compiler_params=_params(("arbitrary", "arbitrary")),
    )(qxb, kx, kx_t, v_st, do, delta, *halves)


def _softplus_of(z):
    return jnp.maximum(z, 0.0) + jnp.log(1.0 + jnp.exp(-jnp.abs(z)))


def _sb_fwd(qkv, k_st, v_t, dg):
    t = qkv.shape[0]
    p, nq = dg // LANES, t // BLK
    nh = 2 * p

    def body(q_ref, k_ref, vt_ref, o_ref, rt_ref):
        i = pl.program_id(1)
        _, strict = _key_query_masks()
        suffix = _key_triangle("suffix")
        q = q_ref[...]

        def scores(j):
            z2 = lax.dot_general(k_ref[0, j], q, NT_DIMS, preferred_element_type=F32)
            return [z2[a * BLK:(a + 1) * BLK] for a in range(2)]

        def suffix_sums(z, masked):
            out = []
            for a in range(2):
                sp = _softplus_of(z[a])
                if masked:
                    sp = jnp.where(strict, sp, 0.0)
                out.append(_tri_dot(suffix, sp, 2))
            return out

        def weights(z, cs, rest, masked):
            w, rest_new = [], []
            for a in range(2):
                wa = jnp.exp(z[a] - cs[a] - rest[a])
                if masked:
                    wa = jnp.where(strict, wa, 0.0)
                w.append(wa.astype(BF16))
                rest_new.append(rest[a] + cs[a][0:1])
            return jnp.concatenate(w, axis=0), tuple(rest_new)

        def tiles(js, carry, masked=False):
            rest, acc = carry
            zs = [scores(j) for j in js]
            css = [suffix_sums(z, masked) for z in zs]
            ws = []
            for z, cs in zip(zs, css):
                w2, rest = weights(z, cs, rest, masked)
                ws.append(w2)
            for j, w2 in zip(js, ws):
                acc = acc + jnp.dot(vt_ref[0, j], w2, preferred_element_type=F32)
            return rest, acc

        zero = jnp.zeros((1, BLK), F32)
        carry = tiles([i], ((zero, zero), jnp.zeros((LANES, BLK), F32)), True)
        rest, acc = _loop_blocks(i, tiles, carry, descending=True, groups=(8, 4, 2, 1))
        o_ref[...] = acc
        rt_ref[0] = rest[0]
        rt_ref[1] = rest[1]

    return _pcall(
        body, name="sb_fwd", grid=(p, nq),
        out_shape=[jax.ShapeDtypeStruct((dg, t), F32), jax.ShapeDtypeStruct((nh, 1, t), F32)],
        in_specs=[pl.BlockSpec((BLK, LANES), lambda h, i: (i, 3 * p + h)), _resident((nq, 2 * BLK, LANES)),
                  _resident((nq, LANES, 2 * BLK))],
        out_specs=[pl.BlockSpec((LANES, BLK), lambda h, i: (h, i)), pl.BlockSpec((2, 1, BLK), lambda h, i: (h, 0, i))],
        compiler_params=_params(("arbitrary", "arbitrary")),
    )(qkv, k_st, v_t)


def _scatter8_copies(ins, outs, send_sems, recv_sems, local_sems):
    x, y, c = lax.axis_index("x"), lax.axis_index("y"), lax.axis_index("c")
    me = 4 * x + 2 * y + c
    local, remote = [], []
    for w in range(len(ins)):
        local.append(pltpu.make_async_copy(ins[w].at[me], outs[w].at[me], local_sems.at[w]))
        for f in range(1, 8):
            px = 1 - x if f & 4 else x
            py = 1 - y if f & 2 else y
            pc = 1 - c if f & 1 else c
            remote.append(pltpu.make_async_remote_copy(
                src_ref=ins[w].at[4 * px + 2 * py + pc], dst_ref=outs[w].at[me],
                send_sem=send_sems.at[7 * w + f - 1], recv_sem=recv_sems.at[7 * w + f - 1],
                device_id=(px, py, pc), device_id_type=MESH))
    return local, remote


def _sb_bwd(qkv, k_st, k_t, v_st, do, rtot, dg, pieces):
    t = qkv.shape[0]
    p, nq = dg // LANES, t // BLK
    ns = len(pieces)

    def body(q_ref, k_ref, kt_ref, v_ref, do_ref, rt_ref, *rest):
        piece_refs, (dq_ref, dk_ref, dv_ref), recv_refs = rest[:ns], rest[ns:ns + 3], rest[ns + 3:2 * ns + 3]
        local, remote = _scatter8_copies(piece_refs, recv_refs, *rest[2 * ns + 3:])
        i = pl.program_id(1)

        @pl.when((pl.program_id(0) == 0) & (i == 0))
        def _():
            for cp in local + remote:
                cp.start()

        @pl.when(i == 0)
        def _():
            dk_ref[...] = jnp.zeros_like(dk_ref)
            dv_ref[...] = jnp.zeros_like(dv_ref)

        in_a, _ = _head_masks()
        _, strict = _key_query_masks()
        before_m, prefix_m = _key_triangle("before"), _key_triangle("prefix")
        q, do2 = q_ref[...], do_ref[...]
        rt = (rt_ref[0], rt_ref[1])

        def products(j):
            z2 = lax.dot_general(k_ref[0, j], q, NT_DIMS, preferred_element_type=F32)
            da2 = lax.dot_general(v_ref[0, j], do2, NT_DIMS, preferred_element_type=F32)
            return [z2[a * BLK:(a + 1) * BLK] for a in range(2)], [da2[a * BLK:(a + 1) * BLK] for a in range(2)]

        def softplus_sums(z, masked):
            sp = [_softplus_of(x) for x in z]
            if masked:
                sp = [jnp.where(strict, x, 0.0) for x in sp]
            return sp, [_tri_dot(before_m, x, 2) for x in sp]

        def weight_grads(z, da, sp, pre, before, masked):
            w, g, pg, before_new = [], [], [], []
            for a in range(2):
                wa = jnp.exp(z[a] + (before[a] - rt[a]) + pre[a])
                if masked:
                    wa = jnp.where(strict, wa, 0.0)
                ga = wa * da[a]
                w.append(wa.astype(BF16))
                g.append(ga)
                pg.append(jnp.dot(prefix_m, ga.astype(BF16), preferred_element_type=F32))
                before_new.append(before[a] + pre[a][BLK - 1:BLK] + sp[a][BLK - 1:BLK])
            return jnp.concatenate(w, axis=0), g, pg, tuple(before_new)

        def dlogits(sp, g, pg, gbefore, masked):
            dz, gbefore_new = [], []
            for a in range(2):
                s_incl = gbefore[a] + pg[a]
                dza = (g[a] - s_incl) + jnp.exp(-sp[a]) * s_incl
                if masked:
                    dza = jnp.where(strict, dza, 0.0)
                dz.append(dza.astype(BF16))
                gbefore_new.append(s_incl[BLK - 1:BLK])
            return jnp.concatenate(dz, axis=0), tuple(gbefore_new)

        def accumulate(j, dzb, wb, dq):
            off = pl.multiple_of(j * BLK, BLK)
            dk_ref[pl.ds(off, BLK), :] += _fold_heads(jnp.dot(dzb, q, preferred_element_type=F32), in_a)
            dv_ref[pl.ds(off, BLK), :] += _fold_heads(jnp.dot(wb, do2, preferred_element_type=F32), in_a)
            return dq + jnp.dot(kt_ref[0, j], dzb, preferred_element_type=F32)

        def tiles(js, carry, masked=False):
            before, gbefore, dq = carry
            prods = [products(j) for j in js]
            sums = [softplus_sums(z, masked) for z, _ in prods]
            grads = []
            for (z, da), (sp, pre) in zip(prods, sums):
                wb, g, pg, before = weight_grads(z, da, sp, pre, before, masked)
                grads.append((wb, g, pg))
            for j, (sp, _), (wb, g, pg) in zip(js, sums, grads):
                dzb, gbefore = dlogits(sp, g, pg, gbefore, masked)
                dq = accumulate(j, dzb, wb, dq)
            return before, gbefore, dq

        zero = jnp.zeros((1, BLK), F32)
        carry = _loop_blocks(i, tiles, ((zero, zero), (zero, zero), jnp.zeros((LANES, BLK), F32)), groups=(2, 1))
        dq_ref[...] = tiles([i], carry, True)[2]

        @pl.when((pl.program_id(0) == p - 1) & (i == nq - 1))
        def _():
            for cp in remote:
                cp.wait_recv()
            for cp in remote:
                cp.wait_send()
            for cp in local:
                cp.wait()

    acc = pl.BlockSpec((t, LANES), lambda h, i: (0, h))
    hbm = pl.BlockSpec(memory_space=pltpu.HBM)
    return _pcall(
        body, name="sb_bwd", grid=(p, nq),
        out_shape=[jax.ShapeDtypeStruct((dg, t), F32)] + [jax.ShapeDtypeStruct((t, dg), F32)] * 2
        + [jax.ShapeDtypeStruct(pc.shape, pc.dtype) for pc in pieces],
        in_specs=[pl.BlockSpec((BLK, LANES), lambda h, i: (i, 3 * p + h)), _resident((nq, 2 * BLK, LANES)),
                  _resident((nq, LANES, 2 * BLK)), _resident((nq, 2 * BLK, LANES)),
                  pl.BlockSpec((BLK, LANES), lambda h, i: (i, h)), pl.BlockSpec((2, 1, BLK), lambda h, i: (h, 0, i))] + [hbm] * ns,
        out_specs=[pl.BlockSpec((LANES, BLK), lambda h, i: (h, i)), acc, acc] + [hbm] * ns,
        scratch_shapes=[pltpu.SemaphoreType.DMA((7 * ns,)), pltpu.SemaphoreType.DMA((7 * ns,)), pltpu.SemaphoreType.DMA((ns,))],
        compiler_params=_params(("arbitrary", "arbitrary")),
    )(qkv, k_st, k_t, v_st, do, rtot, *pieces)


def _tri_constants(nh, t):
    nb = t // LANES
    r = nh * nb
    li = np.arange(LANES)
    tri_in = (li[:, None] <= li[None, :])
    ri = np.arange(r)
    same = (ri[:, None] // nb) == (ri[None, :] // nb)
    blk = same & (ri[None, :] < ri[:, None])
    blk_rev = same & (ri[None, :] > ri[:, None])
    head_rows = (np.arange(max(8, nh))[:, None] == (ri[None, :] // nb))
    as_bf16 = lambda a: jnp.asarray(a.astype(np.float32), BF16)
    return as_bf16(tri_in), as_bf16(blk), as_bf16(tri_in.T), as_bf16(blk_rev), as_bf16(head_rows)


def kernel(x, c, w_ada, b_ada, g_attn, w_in, b_fgate, g_out_fox, g_out_sb, w_out, g_mlp, w_up, conv_w, conv_b, w_down, g_final, loss_target, m_w_ada, m_b_ada, m_g_attn, m_w_in, m_b_fgate, m_g_out_fox, m_g_out_sb, m_w_out, m_g_mlp, m_w_up, m_conv_w, m_conv_b, m_w_down, m_g_final, v_w_ada, v_b_ada, v_g_attn, v_w_in, v_b_fgate, v_g_out_fox, v_g_out_sb, v_w_out, v_g_mlp, v_w_up, v_conv_w, v_conv_b, v_w_down, v_g_final):
    t, d = x.shape[1], x.shape[2]
    dg = d // 2
    nh = dg // HEAD_DIM
    n_in = 6 * dg + nh
    dff = w_down.shape[1] * 4
    dfp = -(-dff // 256) * 256
    cf = 256
    tm = _tile(t, (512, 256, 128))
    nq = t // BLK
    xi, yi, ci = lax.axis_index("x"), lax.axis_index("y"), lax.axis_index("c")
    shard = 2 * xi + yi
    me = 4 * xi + 2 * yi + ci

    x2d, tg2d = x[0], loss_target[0]

    c_all = _all_gather8(jnp.pad(c, ((0, 7), (0, 0)))).reshape(8, 8, d)[:, 0, :]
    ada_cols = w_ada.shape[2]
    b_shard = lax.dynamic_slice(b_ada, (0, shard * ada_cols), (1, ada_cols))
    sc_all, mod_shard = _ada_fwd(c_all, w_ada[0], b_shard)
    mod_all = _all_gather8(mod_shard).reshape(4, 2, 8, ada_cols)
    mod_me = lax.dynamic_index_in_dim(mod_all[:, 0], me, axis=1, keepdims=False)
    mod8 = jnp.pad(mod_me.reshape(6, d), ((0, 2), (0, 0)))

    lane_pad = lambda a: jnp.pad(a, ((0, 0),) * (a.ndim - 1) + ((0, -a.shape[-1] % LANES),))
    (g_in,) = _gather_xy([lane_pad(w_in[0].astype(BF16))])
    later_shards = [w_out[0].astype(BF16), lane_pad(w_up[0].astype(BF16)), w_down[0].astype(BF16), lane_pad(conv_w[0])]
    w_in_full = jnp.transpose(g_in[:, :, :n_in // 4], (1, 0, 2)).reshape(d, n_in)
    w_qkv = w_in_full[:, :6 * dg]
    w_f = jnp.pad(w_in_full[:, 6 * dg:], ((0, 0), (0, LANES - nh)))

    qkv, fl, h1 = _in_proj_fwd(x2d, mod8, g_attn, w_qkv, w_f, tm)
    tri_in, tri_blk, tri_in_rev, tri_blk_rev, head_rows = _tri_constants(nh, t)
    fl2d = fl[:, :nh].T.reshape(nh * t // LANES, LANES)
    b_rows = jnp.repeat(b_fgate[0], t // LANES)[:, None]
    f2d = _fgate_fwd(fl2d, b_rows, tri_in, tri_blk)
    fcol = f2d.reshape(nh, 1, t)
    pairs = nh // 2
    qx, kx, kx_t, vf_st, vf_t, ks_st, ks_t, vs_st, vs_t = _attn_operands(qkv, fcol, dg)
    o_fox_t, lse, g_out, g_up, g_down, g_cw = _fox_fwd(qx, kx, vf_t, dg, later_shards)
    g_up, g_cw = g_up[:, :, :dff // 2], g_cw[:, :, :dff // 2]
    w_out_full = g_out.reshape(2 * dg, d)
    w_up_full = jnp.transpose(g_up, (1, 0, 2)).reshape(d, 2 * dff)
    padc = ((0, 0), (0, dfp - dff))
    wg, wv = jnp.pad(w_up_full[:, :dff], padc), jnp.pad(w_up_full[:, dff:], padc)
    wd = jnp.pad(g_down.reshape(dff, d), ((0, dfp - dff), (0, 0)))
    cw_full = jnp.transpose(g_cw, (1, 0, 2)).reshape(3, 2 * dff)
    cw4 = jnp.concatenate([cw_full, conv_b], axis=0)
    cwg = jnp.pad(cw4[:, :dff], ((0, 4), (0, dfp - dff)))
    cwv = jnp.pad(cw4[:, dff:], ((0, 4), (0, dfp - dff)))
    o_sb_t, rtot = _sb_fwd(qkv, ks_st, vs_t, dg)
    o_fox, o_sb = o_fox_t, o_sb_t
    li = np.arange(dg)
    bd = jnp.asarray((li[:, None] // HEAD_DIM == li[None, :] // HEAD_DIM).astype(np.float32), BF16)
    hsel = jnp.asarray((li[:, None] // HEAD_DIM == np.arange(LANES)[None, :]).astype(np.float32), BF16)
    x2, mix, mix_t = _attn_out_fwd(x2d, o_fox, o_sb, g_out_fox, g_out_sb, w_out_full, mod8, bd, tm)
    g_final2 = g_final[None, :]
    dx3, h2, part_f = _ffn_fwd(x2, tg2d, mod8, g_mlp, g_final2, wg, wv, cwg, cwv, wd, tm, cf)

    tm_ffn_bwd = _tile(t, (256, 128))
    dx2, dupg, dupv, act, dxg3, part_b, pcg, pcv = _ffn_bwd(x2, dx3, mod8, g_mlp, wg, wv, cwg, cwv, wd, tm_ffn_bwd, cf)
    do_fox, do_sb, delta, dxg2, part_o = _attn_out_bwd(dx2, mix, o_fox, o_sb, g_out_fox, g_out_sb, w_out_full, mod8, bd, hsel, tm)
    drow = delta[:, :nh].T.reshape(nh, 1, t)

    def col_pieces(g):
        r, cc = g.shape
        return jnp.transpose(g.reshape(2, r // 2, 4, cc // 4), (2, 0, 1, 3)).reshape(8, r // 2, cc // 4)

    def row_pieces(g):
        r, cc = g.shape
        return g.reshape(8, r // 8, cc)

    gw_out = _matmul_tn(mix_t, dxg2, "grad_w_out")
    gw_upg = _matmul_tn(h2, dupg, "grad_w_up_gate")
    gw_upv = _matmul_tn(h2, dupv, "grad_w_up_val")
    gw_up = jnp.concatenate([gw_upg[:, :dff], gw_upv[:, :dff]], axis=1)
    gw_down = _matmul_tn(act, dxg3, "grad_w_down")[:dff]
    early = (row_pieces(gw_out), lane_pad(col_pieces(gw_up)), row_pieces(gw_down))
    early = [_to_bf16(pc, "pieces_bf16_" + nm) for pc, nm in zip(early, ("w_out", "w_up", "w_down"))]

    qxb = _fox_q_bwd(qkv, fcol, lse, dg)
    dq_s_t, dk_s, dv_s, *recv_early = _sb_bwd(qkv, ks_st, ks_t, vs_st, do_sb, rtot, dg, early)
    halves_early = [_sum_leading(rv, nm) for rv, nm in zip(recv_early, ("sum_w_out", "sum_w_up", "sum_w_down"))]
    dq_f_t, dft, dk_f, dv_f, dkx, *swapped_early = _fox_bwd(qxb, kx, kx_t, vf_st, do_fox, drow, dg, halves_early)
    dq_f, dq_s = dq_f_t, dq_s_t
    f2d_shape = (nh * t // LANES, LANES)
    dfs = jnp.transpose(dkx.reshape(t, pairs, LANES)[:, :, :2], (1, 2, 0))
    dfl2d, gb8 = _fgate_bwd(fl2d, b_rows, dft.reshape(f2d_shape), dfs.reshape(f2d_shape), tri_in_rev, tri_blk_rev, head_rows)
    dfl = jnp.pad(dfl2d.reshape(nh, t).T, ((0, 0), (0, LANES - nh)))
    grad_x, dproj, dflb, part_i = _in_proj_bwd([dq_f, dk_f, dv_f, dq_s, dk_s, dv_s], dfl, w_qkv, w_f, x2d, dx2, mod8, g_attn, tm)

    gw_qkv = _matmul_tn(h1, dproj, "grad_w_qkv")
    gw_f = _matmul_tn(h1, dflb, "grad_w_f")
    gw_in = jnp.concatenate([gw_qkv, gw_f[:, :nh]], axis=1)

    sf = _sum_leading(part_f, "sum_part_ffn_fwd")
    sb_ = _sum_leading(part_b, "sum_part_ffn_bwd")
    so = _sum_leading(part_o, "sum_part_attn_out")
    si = _sum_leading(part_i, "sum_part_in_proj")
    scg = _sum_leading(pcg, "sum_part_conv_gate")
    scv = _sum_leading(pcv, "sum_part_conv_val")
    gb_f = gb8[:nh, 0]
    dmod = jnp.concatenate([si[0], si[1], so[0], sb_[0], sb_[1], sf[1]])
    g_conv_w = jnp.concatenate([scg[0:3, :dff], scv[0:3, :dff]], axis=1).reshape(-1)
    g_conv_b = jnp.concatenate([scg[3, :dff], scv[3, :dff]])
    loss_part = jnp.sum(sf[2])
    fields = [dmod, si[2], gb_f, so[1, :dg], so[1, dg:], sb_[2], g_conv_b, sf[0], g_conv_w, loss_part[None]]
    sizes = [int(f.shape[0]) for f in fields]
    n_pack = sum(sizes)
    lanes_pack = -(-n_pack // (8 * LANES)) * LANES
    pack = jnp.pad(jnp.concatenate(fields), (0, 8 * lanes_pack - n_pack)).reshape(8, lanes_pack)
    gathered = _all_gather8(pack)
    tot = _sum_leading(gathered.reshape(8, 8, lanes_pack), "sum_pack").reshape(-1)
    offs = np.concatenate([[0], np.cumsum(sizes)])
    take = lambda k: tot[int(offs[k]):int(offs[k + 1])]
    g_b_ada, g_g_attn, g_b_fgate, g_g_fox, g_g_sb, g_g_mlp, g_cb, g_g_final, g_cw_full, loss_v = [take(k) for k in range(10)]
    loss = loss_v[0]
    dmod_all = gathered.reshape(8, 8 * lanes_pack)[:, :6 * d]
    dmod_cols = lax.dynamic_slice(dmod_all, (0, shard * ada_cols), (8, ada_cols))
    g_w_ada = _ada_bwd(sc_all.T, dmod_cols)

    (recv_in,) = _scatter8([_to_bf16(lane_pad(col_pieces(gw_in)), "pieces_bf16_w_in")])
    (swapped_in,) = _swap_halves([_sum_leading(recv_in, "sum_w_in")])
    swapped = [swapped_in] + swapped_early
    shard_cols = (n_in // 4, d, dff // 2, d)
    g_w_in, g_w_out, g_w_up, g_w_down = [s.reshape(2 * s.shape[1], s.shape[2])[:, :cc] for s, cc in zip(swapped, shard_cols)]
    g_conv_w_shard = lax.dynamic_slice(g_cw_full.reshape(3, 2 * dff), (0, shard * (dff // 2)), (3, dff // 2))

    grads, deltas, new_m, new_v = {}, {}, {}, {}

    def step(name, w, g, m, v):
        shape = w.shape
        as2d = lambda a: a.reshape(-1, shape[-1])
        dl, nm, nv = _adamw(as2d(w), as2d(g), as2d(m), as2d(v), "adamw_" + name)
        grads[name], deltas[name], new_m[name], new_v[name] = g.reshape(shape), dl.reshape(shape), nm.reshape(shape), nv.reshape(shape)

    step("w_ada", w_ada, g_w_ada, m_w_ada, v_w_ada)
    step("w_in", w_in, g_w_in, m_w_in, v_w_in)
    step("w_out", w_out, g_w_out, m_w_out, v_w_out)
    step("w_up", w_up, g_w_up, m_w_up, v_w_up)
    step("conv_w", conv_w, g_conv_w_shard, m_conv_w, v_conv_w)
    step("w_down", w_down, g_w_down, m_w_down, v_w_down)

    small = [("b_ada", b_ada, g_b_ada, m_b_ada, v_b_ada), ("g_attn", g_attn, g_g_attn, m_g_attn, v_g_attn),
             ("b_fgate", b_fgate, g_b_fgate, m_b_fgate, v_b_fgate), ("g_out_fox", g_out_fox, g_g_fox, m_g_out_fox, v_g_out_fox),
             ("g_out_sb", g_out_sb, g_g_sb, m_g_out_sb, v_g_out_sb), ("g_mlp", g_mlp, g_g_mlp, m_g_mlp, v_g_mlp),
             ("conv_b", conv_b, g_cb, m_conv_b, v_conv_b), ("g_final", g_final, g_g_final, m_g_final, v_g_final)]
    ssz = [int(np.prod(s[1].shape)) for s in small]
    n_small = sum(ssz)
    lanes_small = -(-n_small // (8 * LANES)) * LANES
    packs = [jnp.pad(jnp.concatenate([s[k].reshape(-1) for s in small]), (0, 8 * lanes_small - n_small)).reshape(8, lanes_small)
             for k in (1, 2, 3, 4)]
    dl_s, nm_s, nv_s = _adamw(*packs, "adamw_small")
    so_ = np.concatenate([[0], np.cumsum(ssz)])
    for k, s in enumerate(small):
        cut = lambda a: a.reshape(-1)[int(so_[k]):int(so_[k + 1])].reshape(s[1].shape)
        grads[s[0]], deltas[s[0]], new_m[s[0]], new_v[s[0]] = s[2].reshape(s[1].shape), cut(dl_s), cut(nm_s), cut(nv_s)

    order = ["w_ada", "b_ada", "g_attn", "w_in", "b_fgate", "g_out_fox", "g_out_sb", "w_out", "g_mlp", "w_up",
             "conv_w", "conv_b", "w_down", "g_final"]
    return (loss, grad_x[None], *[grads[n] for n in order], *[deltas[n] for n in order],
            *[new_m[n] for n in order], *[new_v[n] for n in order])
```

```python
import functools

import numpy as np
import jax
import jax.numpy as jnp
from jax import lax
from jax.experimental import pallas as pl
from jax.experimental.pallas import tpu as pltpu

F32 = jnp.float32
BF16 = jnp.bfloat16
MESH = pl.DeviceIdType.MESH

HEAD_DIM = 64
LANES = 128
EPS = 1e-6
NEG = -1e30
ADAM_LR, ADAM_B1, ADAM_B2, ADAM_EPS, ADAM_WD, ADAM_STEP = 0.001, 0.9, 0.999, 1e-08, 0.01, 10
V7X_VMEM_BYTES = 64 * 1024 * 1024
VMEM_LIMIT = V7X_VMEM_BYTES - 12 * 1024 * 1024
NT_DIMS = (((1,), (1,)), ((), ()))


def _pcall(body, **kw):
    return pl.pallas_call(body, **kw)


def _params(sem=None, **kw):
    return pltpu.CompilerParams(dimension_semantics=sem, vmem_limit_bytes=VMEM_LIMIT, **kw)


def _split_dot(x, m, passes):
    acc = None
    for _ in range(passes):
        part = x.astype(BF16)
        d = jnp.dot(part, m, preferred_element_type=F32)
        acc = d if acc is None else acc + d
        x = x - part.astype(F32)
    return acc


def _tile(n, candidates):
    for t in candidates:
        if n % t == 0:
            return t
    return n


def _rows_tile(rows, row_bytes, budget=2 * 1024 * 1024):
    best = None
    for t in range(8, rows + 1, 8):
        if rows % t == 0 and t * row_bytes <= budget:
            best = t
    return best if best is not None else rows


def _all_gather8(v):
    m_per, n = v.shape

    def body(x_ref, out_ref, send_sems, recv_sems, local_sem):
        x, y, c = lax.axis_index("x"), lax.axis_index("y"), lax.axis_index("c")
        me, sibling = (x, y, c), (x, y, 1 - c)
        chips = [(1 - x, y), (x, 1 - y), (1 - x, 1 - y)]

        def rows(px, py, pc):
            return out_ref.at[pl.ds((4 * px + 2 * py + pc) * m_per, m_per), :]

        def copy(k, block, to, src=None):
            return pltpu.make_async_remote_copy(
                src_ref=rows(*block) if src is None else src, dst_ref=rows(*block),
                send_sem=send_sems.at[k], recv_sem=recv_sems.at[k], device_id=to, device_id_type=MESH)

        mine = pltpu.make_async_copy(x_ref, rows(*me), local_sem)
        mine.start()
        first = [copy(0, me, sibling, src=x_ref)]
        first += [copy(1 + j, me, (*chip, c), src=x_ref) for j, chip in enumerate(chips)]
        for cp in first:
            cp.start()
        passed = [copy(4 + j, (*chip, c), sibling) for j, chip in enumerate(chips)]
        for j, chip in enumerate(chips):
            copy(1 + j, (*chip, c), me).wait_recv()
            passed[j].start()
        copy(0, sibling, me).wait_recv()
        for j, chip in enumerate(chips):
            copy(4 + j, (*chip, 1 - c), me).wait_recv()
        for cp in first + passed:
            cp.wait_send()
        mine.wait()

    return _pcall(
        body, name="all_gather8",
        out_shape=jax.ShapeDtypeStruct((8 * m_per, n), v.dtype),
        in_specs=[pl.BlockSpec(memory_space=pltpu.VMEM)],
        out_specs=pl.BlockSpec(memory_space=pltpu.VMEM),
        scratch_shapes=[pltpu.SemaphoreType.DMA((7,)), pltpu.SemaphoreType.DMA((7,)), pltpu.SemaphoreType.DMA],
        compiler_params=pltpu.CompilerParams(vmem_limit_bytes=VMEM_LIMIT),
    )(v)


def _gather_xy(shards):
    n = len(shards)

    def body(*refs):
        ins, outs = refs[:n], refs[n:2 * n]
        send_sems, recv_sems, local_sems = refs[2 * n:]
        x, y, c = lax.axis_index("x"), lax.axis_index("y"), lax.axis_index("c")
        chips = [(1 - x, y), (x, 1 - y), (1 - x, 1 - y)]
        mine = 2 * x + y
        local, remote = [], []
        for w in range(n):
            cp = pltpu.make_async_copy(ins[w], outs[w].at[mine], local_sems.at[w])
            cp.start()
            local.append(cp)
            for k, (px, py) in enumerate(chips):
                cp = pltpu.make_async_remote_copy(
                    src_ref=ins[w], dst_ref=outs[w].at[mine], send_sem=send_sems.at[3 * w + k],
                    recv_sem=recv_sems.at[3 * w + k], device_id=(px, py, c), device_id_type=MESH)
                cp.start()
                remote.append(cp)
        for cp in remote:
            cp.wait_recv()
        for cp in remote:
            cp.wait_send()
        for cp in local:
            cp.wait()

    hbm = pl.BlockSpec(memory_space=pltpu.HBM)
    return _pcall(
        body, name="gather_xy",
        out_shape=[jax.ShapeDtypeStruct((4,) + s.shape, s.dtype) for s in shards],
        in_specs=[hbm] * n, out_specs=[hbm] * n,
        scratch_shapes=[pltpu.SemaphoreType.DMA((3 * n,)), pltpu.SemaphoreType.DMA((3 * n,)),
                        pltpu.SemaphoreType.DMA((n,))],
        compiler_params=pltpu.CompilerParams(vmem_limit_bytes=VMEM_LIMIT),
    )(*shards)


def _scatter8(pieces):
    n = len(pieces)

    def body(*refs):
        ins, outs = refs[:n], refs[n:2 * n]
        send_sems, recv_sems, local_sems = refs[2 * n:]
        x, y, c = lax.axis_index("x"), lax.axis_index("y"), lax.axis_index("c")
        me = 4 * x + 2 * y + c
        local, remote = [], []
        for w in range(n):
            cp = pltpu.make_async_copy(ins[w].at[me], outs[w].at[me], local_sems.at[w])
            cp.start()
            local.append(cp)
            for f in range(1, 8):
                px = 1 - x if f & 4 else x
                py = 1 - y if f & 2 else y
                pc = 1 - c if f & 1 else c
                cp = pltpu.make_async_remote_copy(
                    src_ref=ins[w].at[4 * px + 2 * py + pc], dst_ref=outs[w].at[me],
                    send_sem=send_sems.at[7 * w + f - 1], recv_sem=recv_sems.at[7 * w + f - 1],
                    device_id=(px, py, pc), device_id_type=MESH)
                cp.start()
                remote.append(cp)
        for cp in remote:
            cp.wait_recv()
        for cp in remote:
            cp.wait_send()
        for cp in local:
            cp.wait()

    hbm = pl.BlockSpec(memory_space=pltpu.HBM)
    return _pcall(
        body, name="scatter8",
        out_shape=[jax.ShapeDtypeStruct(p.shape, p.dtype) for p in pieces],
        in_specs=[hbm] * n, out_specs=[hbm] * n,
        scratch_shapes=[pltpu.SemaphoreType.DMA((7 * n,)), pltpu.SemaphoreType.DMA((7 * n,)),
                        pltpu.SemaphoreType.DMA((n,))],
        compiler_params=pltpu.CompilerParams(vmem_limit_bytes=VMEM_LIMIT),
    )(*pieces)


def _swap_halves(halves):
    n = len(halves)
    chunks = 8
    n_chunks = [max(k for k in (chunks, 4, 2, 1) if h.shape[0] % (8 * k) == 0) for h in halves]

    def body(*refs):
        ins, outs = refs[:n], refs[n:2 * n]
        send_sems, recv_sems, local_sems = refs[2 * n:]
        x, y, c = lax.axis_index("x"), lax.axis_index("y"), lax.axis_index("c")
        local, remote = [], []
        for w in range(n):
            cp = pltpu.make_async_copy(ins[w], outs[w].at[c], local_sems.at[w])
            cp.start()
            local.append(cp)
            rows = ins[w].shape[0] // n_chunks[w]
            for k in range(n_chunks[w]):
                cp = pltpu.make_async_remote_copy(
                    src_ref=ins[w].at[pl.ds(k * rows, rows)], dst_ref=outs[w].at[c, pl.ds(k * rows, rows)],
                    send_sem=send_sems.at[chunks * w + k], recv_sem=recv_sems.at[chunks * w + k],
                    device_id=(x, y, 1 - c), device_id_type=MESH)
                cp.start()
                remote.append(cp)
        for cp in remote:
            cp.wait_recv()
        for cp in remote:
            cp.wait_send()
        for cp in local:
            cp.wait()

    hbm = pl.BlockSpec(memory_space=pltpu.HBM)
    return _pcall(
        body, name="swap_halves",
        out_shape=[jax.ShapeDtypeStruct((2,) + h.shape, h.dtype) for h in halves],
        in_specs=[hbm] * n, out_specs=[hbm] * n,
        scratch_shapes=[pltpu.SemaphoreType.DMA((chunks * n,)), pltpu.SemaphoreType.DMA((chunks * n,)),
                        pltpu.SemaphoreType.DMA((n,))],
        compiler_params=pltpu.CompilerParams(vmem_limit_bytes=VMEM_LIMIT),
    )(*halves)


def _sum_leading(a, name):
    n, r, c = a.shape
    tr = _rows_tile(r, n * c * 4, budget=6 * 1024 * 1024)
    if a.dtype == BF16 and tr % 16:
        tr = r

    def body(a_ref, o_ref):
        acc = a_ref[0].astype(F32)
        for k in range(1, n):
            acc = acc + a_ref[k].astype(F32)
        o_ref[...] = acc

    return _pcall(
        body, name=name, grid=(r // tr,),
        out_shape=jax.ShapeDtypeStruct((r, c), F32),
        in_specs=[pl.BlockSpec((n, tr, c), lambda i: (0, i, 0))],
        out_specs=pl.BlockSpec((tr, c), lambda i: (i, 0)),
        compiler_params=_params(("arbitrary",)),
    )(a)


def _to_bf16(a, name):
    n, r, c = a.shape

    def body(a_ref, o_ref):
        o_ref[...] = a_ref[...].astype(BF16)

    spec = pl.BlockSpec((1, r, c), lambda i: (i, 0, 0))
    return _pcall(
        body, name=name, grid=(n,), out_shape=jax.ShapeDtypeStruct(a.shape, BF16),
        in_specs=[spec], out_specs=spec, compiler_params=_params(("arbitrary",)),
    )(a)


def _adamw(w, g, m, v, name):
    r, c = w.shape
    tr = _rows_tile(r, c * 4, budget=1024 * 1024)
    c1 = 1.0 - ADAM_B1 ** ADAM_STEP
    c2 = 1.0 - ADAM_B2 ** ADAM_STEP

    def body(w_ref, g_ref, m_ref, v_ref, d_ref, nm_ref, nv_ref):
        gg = g_ref[...]
        nm = ADAM_B1 * m_ref[...] + (1.0 - ADAM_B1) * gg
        nv = ADAM_B2 * v_ref[...] + (1.0 - ADAM_B2) * (gg * gg)
        m_hat = nm / c1
        v_hat = nv / c2
        d_ref[...] = -ADAM_LR * (m_hat / (jnp.sqrt(v_hat) + ADAM_EPS) + ADAM_WD * w_ref[...])
        nm_ref[...] = nm
        nv_ref[...] = nv

    spec = pl.BlockSpec((tr, c), lambda i: (i, 0))
    return _pcall(
        body, name=name, grid=(r // tr,),
        out_shape=[jax.ShapeDtypeStruct((r, c), F32)] * 3,
        in_specs=[spec] * 4, out_specs=[spec] * 3,
        compiler_params=_params(("arbitrary",)),
    )(w, g, m, v)


def _ada_fwd(c_all, w_shard, b_shard):
    nb, d = c_all.shape
    cols = w_shard.shape[1]

    def body(c_ref, w_ref, b_ref, sc_ref, mod_ref):
        cv = c_ref[...]
        sc = cv * jax.nn.sigmoid(cv)
        sc_ref[...] = sc
        mod_ref[...] = jnp.dot(sc.astype(BF16), w_ref[...].astype(BF16), preferred_element_type=F32) + b_ref[...]

    return _pcall(
        body, name="ada_fwd",
        out_shape=[jax.ShapeDtypeStruct((nb, d), F32), jax.ShapeDtypeStruct((nb, cols), F32)],
        compiler_params=pltpu.CompilerParams(vmem_limit_bytes=VMEM_LIMIT),
    )(c_all, w_shard, b_shard)


def _ada_bwd(sc_t, dmod_cols):
    d, nb = sc_t.shape
    cols = dmod_cols.shape[1]
    tr = _rows_tile(d, cols * 4, budget=1024 * 1024)

    def body(s_ref, m_ref, o_ref):
        s = s_ref[...]
        m = m_ref[...]
        acc = s[:, 0:1] * m[0:1, :]
        for b in range(1, nb):
            acc = acc + s[:, b:b + 1] * m[b:b + 1, :]
        o_ref[...] = acc

    return _pcall(
        body, name="ada_bwd", grid=(d // tr,),
        out_shape=jax.ShapeDtypeStruct((d, cols), F32),
        in_specs=[pl.BlockSpec((tr, nb), lambda i: (i, 0)), pl.BlockSpec((nb, cols), lambda i: (0, 0))],
        out_specs=pl.BlockSpec((tr, cols), lambda i: (i, 0)),
        compiler_params=_params(("arbitrary",)),
    )(sc_t, dmod_cols)


def _log_sigmoid(x):
    return jnp.minimum(x, 0.0) - jnp.log1p(jnp.exp(-jnp.abs(x)))


def _fgate_fwd(fl2d, b_rows, tri_in, tri_blk):
    r = fl2d.shape[0]

    def body(x_ref, b_ref, u_ref, l_ref, f_ref):
        lf = _log_sigmoid(x_ref[...] + b_ref[...])
        c1 = _split_dot(lf, u_ref[...], 3)
        tot = jnp.broadcast_to(c1[:, LANES - 1:LANES], (r, LANES))
        acc = None
        for _ in range(3):
            part = tot.astype(BF16)
            dd = jnp.dot(l_ref[...], part, preferred_element_type=F32)
            acc = dd if acc is None else acc + dd
            tot = tot - part.astype(F32)
        f_ref[...] = c1 + acc

    return _pcall(
        body, name="fgate_fwd", out_shape=jax.ShapeDtypeStruct((r, LANES), F32),
        compiler_params=pltpu.CompilerParams(vmem_limit_bytes=VMEM_LIMIT),
    )(fl2d, b_rows, tri_in, tri_blk)


def _fgate_bwd(fl2d, b_rows, df_query, df_key, tri_in_rev, tri_blk_rev, head_rows):
    r = fl2d.shape[0]
    nhp = head_rows.shape[0]

    def body(x_ref, b_ref, dq_ref, dk_ref, u_ref, l_ref, hr_ref, o_ref, gb_ref):
        c1 = _split_dot(dq_ref[...] + dk_ref[...], u_ref[...], 3)
        tot = jnp.broadcast_to(c1[:, 0:1], (r, LANES))
        acc = None
        for _ in range(3):
            part = tot.astype(BF16)
            dd = jnp.dot(l_ref[...], part, preferred_element_type=F32)
            acc = dd if acc is None else acc + dd
            tot = tot - part.astype(F32)
        x = x_ref[...] + b_ref[...]
        e = jnp.exp(-jnp.abs(x))
        dfl = (c1 + acc) * (jnp.where(x >= 0, e, 1.0) / (1.0 + e))
        o_ref[...] = dfl
        rs = jnp.broadcast_to(jnp.sum(dfl, axis=1, keepdims=True), (r, LANES))
        gb = None
        for _ in range(3):
            part = rs.astype(BF16)
            dd = jnp.dot(hr_ref[...], part, preferred_element_type=F32)
            gb = dd if gb is None else gb + dd
            rs = rs - part.astype(F32)
        gb_ref[...] = gb

    return _pcall(
        body, name="fgate_bwd",
        out_shape=[jax.ShapeDtypeStruct((r, LANES), F32), jax.ShapeDtypeStruct((nhp, LANES), F32)],
        compiler_params=pltpu.CompilerParams(vmem_limit_bytes=VMEM_LIMIT),
    )(fl2d, b_rows, df_query, df_key, tri_in_rev, tri_blk_rev, head_rows)


def _norm_mod(x, g, scale, shift):
    r = lax.rsqrt(jnp.mean(x * x, axis=-1, keepdims=True) + EPS)
    return (x * r * g) * (1.0 + scale) + shift


def _norm_mod_bwd(x, dh, g, scale):
    r = lax.rsqrt(jnp.mean(x * x, axis=-1, keepdims=True) + EPS)
    xn = x * r
    dshift = jnp.sum(dh, axis=0, keepdims=True)
    dscale = jnp.sum(dh * (xn * g), axis=0, keepdims=True)
    dxn_g = dh * (1.0 + scale)
    dg = jnp.sum(dxn_g * xn, axis=0, keepdims=True)
    dxn = dxn_g * g
    dx = r * (dxn - xn * jnp.mean(dxn * xn, axis=-1, keepdims=True))
    return dx, dshift, dscale, dg


def _in_proj_fwd(x, mod8, g_attn, w_qkv, w_f, tm):
    t, d = x.shape
    dg = w_qkv.shape[1] // 6

    def body(x_ref, mod_ref, g_ref, w_ref, wf_ref, qkv_ref, fl_ref, h1_ref, h_sc):
        j = pl.program_id(1)

        @pl.when(j == 0)
        def _():
            h = _norm_mod(x_ref[...], g_ref[...], mod_ref[1:2, :], mod_ref[0:1, :]).astype(BF16)
            h_sc[...] = h
            h1_ref[...] = _transposed(h)
            fl_ref[...] = jnp.dot(h, wf_ref[...], preferred_element_type=F32)

        s = jnp.where((j == 0) | (j == 3), HEAD_DIM ** -0.5, 1.0)
        qkv_ref[...] = (jnp.dot(h_sc[...], w_ref[...], preferred_element_type=F32) * s).astype(BF16)

    return _pcall(
        body, name="in_proj_fwd", grid=(t // tm, 6),
        out_shape=[jax.ShapeDtypeStruct((t, 6 * dg), BF16), jax.ShapeDtypeStruct((t, LANES), F32),
                   jax.ShapeDtypeStruct((d, t), BF16)],
        in_specs=[pl.BlockSpec((tm, d), lambda i, j: (i, 0)), pl.BlockSpec((8, d), lambda i, j: (0, 0)),
                  pl.BlockSpec((1, d), lambda i, j: (0, 0)), pl.BlockSpec((d, dg), lambda i, j: (0, j)),
                  pl.BlockSpec((d, LANES), lambda i, j: (0, 0))],
        out_specs=[pl.BlockSpec((tm, dg), lambda i, j: (i, j)), pl.BlockSpec((tm, LANES), lambda i, j: (i, 0)),
                   pl.BlockSpec((d, tm), lambda i, j: (0, i))],
        scratch_shapes=[pltpu.VMEM((tm, d), BF16)],
        compiler_params=_params(("arbitrary", "arbitrary")),
    )(x, mod8, g_attn, w_qkv, w_f)


def _head_rstd(o, bd):
    return lax.rsqrt(_split_dot(o * o, bd, 3) * (1.0 / HEAD_DIM) + EPS)


def _attn_out_fwd(x, o_fox, o_sb, g_fox, g_sb, w_out, mod8, bd, tm):
    t, d = x.shape
    dg = o_fox.shape[0]

    def body(x_ref, of_ref, os_ref, gf_ref, gs_ref, w_ref, mod_ref, bd_ref, x2_ref, mix_ref, mixt_ref):
        of, osb = of_ref[...].T, os_ref[...].T
        mf = (of * _head_rstd(of, bd_ref[...]) * gf_ref[...]).astype(BF16)
        ms = (osb * _head_rstd(osb, bd_ref[...]) * gs_ref[...]).astype(BF16)
        mix_ref[:, :dg] = mf
        mix_ref[:, dg:] = ms
        mixt_ref[:dg, :] = _transposed(mf)
        mixt_ref[dg:, :] = _transposed(ms)
        y = jnp.dot(mf, w_ref[:dg, :], preferred_element_type=F32) + jnp.dot(ms, w_ref[dg:, :], preferred_element_type=F32)
        x2_ref[...] = x_ref[...] + mod_ref[2:3, :] * y

    row = lambda w: pl.BlockSpec((tm, w), lambda i: (i, 0))
    full = lambda a: pl.BlockSpec(a.shape, lambda i: (0,) * a.ndim)
    return _pcall(
        body, name="attn_out_fwd", grid=(t // tm,),
        out_shape=[jax.ShapeDtypeStruct((t, d), F32), jax.ShapeDtypeStruct((t, 2 * dg), BF16),
                   jax.ShapeDtypeStruct((2 * dg, t), BF16)],
        in_specs=[row(d), pl.BlockSpec((dg, tm), lambda i: (0, i)), pl.BlockSpec((dg, tm), lambda i: (0, i)),
                  full(g_fox), full(g_sb), full(w_out), full(mod8), full(bd)],
        out_specs=[row(d), row(2 * dg), pl.BlockSpec((2 * dg, tm), lambda i: (0, i))],
        compiler_params=_params(("arbitrary",)),
    )(x, o_fox, o_sb, g_fox, g_sb, w_out, mod8, bd)


def _attn_out_bwd(dx2, mix, o_fox, o_sb, g_fox, g_sb, w_out, mod8, bd, hsel, tm):
    t, d = dx2.shape
    dg = o_fox.shape[0]

    def body(dx_ref, mix_ref, of_ref, os_ref, gf_ref, gs_ref, w_ref, mod_ref, bd_ref, hs_ref,
             dof_ref, dos_ref, dlt_ref, dxg_ref, part_ref):
        dx = dx_ref[...]
        gate = mod_ref[2:3, :]
        dxg = (dx * gate).astype(BF16)
        dxg_ref[...] = dxg
        mixv = mix_ref[...]
        y = jnp.dot(mixv[:, :dg], w_ref[:dg, :], preferred_element_type=F32)
        y = y + jnp.dot(mixv[:, dg:], w_ref[dg:, :], preferred_element_type=F32)
        part_ref[0] = jnp.zeros((8, d), F32)
        part_ref[0, 0:1, :] = jnp.sum(dx * y, axis=0, keepdims=True)
        for grp, (o_ref, g_ref, do_ref) in enumerate(((of_ref, gf_ref, dof_ref), (os_ref, gs_ref, dos_ref))):
            dmix = lax.dot_general(dxg, w_ref[grp * dg:(grp + 1) * dg, :], NT_DIMS, preferred_element_type=F32)
            o = o_ref[...].T
            r = _head_rstd(o, bd_ref[...])
            n = o * r
            part_ref[0, 1:2, grp * dg:(grp + 1) * dg] = jnp.sum(dmix * n, axis=0, keepdims=True)
            dn = dmix * g_ref[...]
            mh = _split_dot(dn * n, bd_ref[...], 3) * (1.0 / HEAD_DIM)
            do = r * (dn - n * mh)
            do_ref[...] = do.astype(BF16)
            if grp == 0:
                dlt_ref[...] = _split_dot(do * o, hs_ref[...], 3)

    row = lambda w: pl.BlockSpec((tm, w), lambda i: (i, 0))
    full = lambda a: pl.BlockSpec(a.shape, lambda i: (0,) * a.ndim)
    nt = t // tm
    return _pcall(
        body, name="attn_out_bwd", grid=(nt,),
        out_shape=[jax.ShapeDtypeStruct((t, dg), BF16), jax.ShapeDtypeStruct((t, dg), BF16),
                   jax.ShapeDtypeStruct((t, LANES), F32), jax.ShapeDtypeStruct((t, d), BF16),
                   jax.ShapeDtypeStruct((nt, 8, d), F32)],
        in_specs=[row(d), row(2 * dg), pl.BlockSpec((dg, tm), lambda i: (0, i)), pl.BlockSpec((dg, tm), lambda i: (0, i)),
                  full(g_fox), full(g_sb), full(w_out), full(mod8),
                  full(bd), full(hsel)],
        out_specs=[row(dg), row(dg), row(LANES), row(d), pl.BlockSpec((1, 8, d), lambda i: (i, 0, 0))],
        compiler_params=_params(("arbitrary",)),
    )(dx2, mix, o_fox, o_sb, g_fox, g_sb, w_out, mod8, bd, hsel)


def _in_proj_bwd(dparts, dfl, w_qkv, w_f, x, dx2, mod8, g_attn, tm):
    t, d = x.shape
    dg = dparts[1].shape[1]

    def body(*refs):
        d_refs = refs[:6]
        dfl_ref, w_ref, wf_ref, x_ref, dx2_ref, mod_ref, g_ref, gx_ref, dp_ref, dflb_ref, part_ref = refs[6:]
        dh = None
        for k in range(6):
            dk = d_refs[k][...].T if k in (0, 3) else d_refs[k][...]
            if k in (0, 3):
                dk = dk * HEAD_DIM ** -0.5
            db = dk.astype(BF16)
            dp_ref[:, k * dg:(k + 1) * dg] = db
            term = lax.dot_general(db, w_ref[:, k * dg:(k + 1) * dg], NT_DIMS, preferred_element_type=F32)
            dh = term if dh is None else dh + term
        dfb = dfl_ref[...].astype(BF16)
        dflb_ref[...] = dfb
        dh = dh + lax.dot_general(dfb, wf_ref[...], NT_DIMS, preferred_element_type=F32)
        dx, dshift, dscale, dgn = _norm_mod_bwd(x_ref[...], dh, g_ref[...], mod_ref[1:2, :])
        gx_ref[...] = dx2_ref[...] + dx
        part_ref[0] = jnp.zeros((8, d), F32)
        part_ref[0, 0:1, :] = dshift
        part_ref[0, 1:2, :] = dscale
        part_ref[0, 2:3, :] = dgn

    row = lambda w: pl.BlockSpec((tm, w), lambda i: (i, 0))
    full = lambda a: pl.BlockSpec(a.shape, lambda i: (0,) * a.ndim)
    nt = t // tm
    return _pcall(
        body, name="in_proj_bwd", grid=(nt,),
        out_shape=[jax.ShapeDtypeStruct((t, d), F32), jax.ShapeDtypeStruct((t, 6 * dg), BF16),
                   jax.ShapeDtypeStruct((t, LANES), BF16), jax.ShapeDtypeStruct((nt, 8, d), F32)],
        in_specs=[pl.BlockSpec((dg, tm), lambda i: (0, i)), row(dg), row(dg)] * 2
        + [row(LANES), full(w_qkv), full(w_f), row(d), row(d), full(mod8), full(g_attn)],
        out_specs=[row(d), row(6 * dg), row(LANES), pl.BlockSpec((1, 8, d), lambda i: (i, 0, 0))],
        compiler_params=_params(("arbitrary",)),
    )(*dparts, dfl, w_qkv, w_f, x, dx2, mod8, g_attn)


def _matmul_tn(a_t, b, name):
    m, t = a_t.shape
    n = b.shape[1]
    tm_ = _tile(m, (1408, 1024, 512, 256, 128))
    tn_ = _tile(n, (1408, 1024, 512, 256, 128))
    tk = _tile(t, (1024, 512, 256, 128))
    nk = t // tk

    def body(a_ref, b_ref, o_ref):
        k = pl.program_id(2)

        @pl.when(k == 0)
        def _():
            o_ref[...] = jnp.zeros_like(o_ref)

        o_ref[...] += jnp.dot(a_ref[...], b_ref[...], preferred_element_type=F32)

    return _pcall(
        body, name=name, grid=(m // tm_, n // tn_, nk),
        out_shape=jax.ShapeDtypeStruct((m, n), F32),
        in_specs=[pl.BlockSpec((tm_, tk), lambda i, j, k: (i, k)), pl.BlockSpec((tk, tn_), lambda i, j, k: (k, j))],
        out_specs=pl.BlockSpec((tm_, tn_), lambda i, j, k: (i, j)),
        compiler_params=_params(("arbitrary", "arbitrary", "arbitrary")),
    )(a_t, b)


HALO = 16


def _conv_taps(up_ext, cw, lo, rows):
    s1 = pltpu.roll(up_ext, 1, 0)
    s2 = pltpu.roll(up_ext, 2, 0)
    u = cw[2:3, :] * up_ext[lo:lo + rows] + cw[1:2, :] * s1[lo:lo + rows] + cw[0:1, :] * s2[lo:lo + rows] + cw[3:4, :]
    return u, s1, s2


def _chunk_major(w, cf):
    d, n = w.shape[0], w.shape[1] // cf
    return jnp.transpose(w.reshape(d, n, cf), (1, 0, 2))


def _ffn_fwd(x2, target, mod8, g_mlp, g_final, wg, wv, cwg, cwv, wd, tm, cf):
    t, d = x2.shape
    dfp = wg.shape[1]
    nt, nc = t // tm, dfp // cf
    hb = tm // HALO
    wg_c, wv_c = _chunk_major(wg, cf), _chunk_major(wv, cf)

    def body(x_ref, xp_ref, tg_ref, mod_ref, g_ref, gf_ref, wg_ref, wv_ref, cg_ref, cv_ref, wd_ref,
             dx3_ref, h2_ref, part_ref, act_sc):
        i = pl.program_id(0)
        xe = jnp.concatenate([xp_ref[...], x_ref[...]], axis=0)
        h = _norm_mod(xe, g_ref[...], mod_ref[4:5, :], mod_ref[3:4, :]).astype(BF16)
        h2_ref[...] = _transposed(h[HALO:])
        first = jnp.where(i > 0, h[:HALO], jnp.zeros_like(h[:HALO]))
        h = jnp.concatenate([first, h[HALO:]], axis=0)

        def up(c):
            return (jnp.dot(h, wg_ref[c], preferred_element_type=F32), jnp.dot(h, wv_ref[c], preferred_element_type=F32))

        def activation(c, ups):
            cols = slice(c * cf, (c + 1) * cf)
            ug, _, _ = _conv_taps(ups[0], cg_ref[:, cols], HALO, tm)
            uv, _, _ = _conv_taps(ups[1], cv_ref[:, cols], HALO, tm)
            act_sc[:, cols] = (ug * jax.nn.sigmoid(ug) * uv).astype(BF16)

        for c0 in range(0, nc, 2):
            group = list(range(c0, min(c0 + 2, nc)))
            ups = [up(c) for c in group]
            for c, u in zip(group, ups):
                activation(c, u)

        y_ffn = jnp.dot(act_sc[...], wd_ref[...], preferred_element_type=F32)
        x3 = x_ref[...] + mod_ref[5:6, :] * y_ffn
        r3 = lax.rsqrt(jnp.mean(x3 * x3, axis=-1, keepdims=True) + EPS)
        xn = x3 * r3
        gf = gf_ref[...]
        diff = xn * gf - tg_ref[...]
        dy = diff * (1.0 / d)
        dxn = dy * gf
        dx3 = r3 * (dxn - xn * jnp.mean(dxn * xn, axis=-1, keepdims=True))
        dx3_ref[...] = dx3
        part_ref[0] = jnp.zeros((8, d), F32)
        part_ref[0, 0:1, :] = jnp.sum(dy * xn, axis=0, keepdims=True)
        part_ref[0, 1:2, :] = jnp.sum(dx3 * y_ffn, axis=0, keepdims=True)
        part_ref[0, 2:3, :] = jnp.sum(diff * diff, axis=0, keepdims=True) * (0.5 / d)

    row = lambda w: pl.BlockSpec((tm, w), lambda i: (i, 0))
    full = lambda a: pl.BlockSpec(a.shape, lambda i: (0,) * a.ndim)
    once = lambda a: pl.BlockSpec(a.shape, lambda i: (0,) * a.ndim, pipeline_mode=pl.Buffered(1))
    return _pcall(
        body, name="ffn_fwd", grid=(nt,),
        out_shape=[jax.ShapeDtypeStruct((t, d), F32), jax.ShapeDtypeStruct((d, t), BF16),
                   jax.ShapeDtypeStruct((nt, 8, d), F32)],
        in_specs=[row(d), pl.BlockSpec((HALO, d), lambda i: (jnp.maximum(i * hb - 1, 0), 0)), row(d),
                  full(mod8), full(g_mlp), full(g_final), once(wg_c), once(wv_c), once(cwg), once(cwv), once(wd)],
        out_specs=[row(d), pl.BlockSpec((d, tm), lambda i: (0, i)), pl.BlockSpec((1, 8, d), lambda i: (i, 0, 0))],
        scratch_shapes=[pltpu.VMEM((tm, dfp), BF16)],
        compiler_params=_params(("arbitrary",)),
    )(x2, x2, target, mod8, g_mlp, g_final, wg_c, wv_c, cwg, cwv, wd)


def _ffn_bwd(x2, dx3, mod8, g_mlp, wg, wv, cwg, cwv, wd, tm, cf):
    t, d = x2.shape
    dfp = wg.shape[1]
    nt, nc = t // tm, dfp // cf
    hb = tm // HALO
    nhb = t // HALO
    n = tm + HALO

    def body(x_ref, xp_ref, xn_ref, dx_ref, dxn_ref, mod_ref, g_ref, wg_ref, wv_ref, cg_ref, cv_ref, wd_ref,
             dx2_ref, dug_ref, duv_ref, act_ref, dxg_ref, part_ref, pcg_ref, pcv_ref):
        i = pl.program_id(0)
        xe = jnp.concatenate([xp_ref[...], x_ref[...], xn_ref[...]], axis=0)
        h = _norm_mod(xe, g_ref[...], mod_ref[4:5, :], mod_ref[3:4, :]).astype(BF16)
        h = jnp.concatenate([jnp.where(i > 0, h[:HALO], jnp.zeros_like(h[:HALO])), h[HALO:]], axis=0)
        dx = dx_ref[...] * mod_ref[5:6, :]
        dxn = jnp.where(i < nt - 1, dxn_ref[...] * mod_ref[5:6, :], 0.0)
        de = jnp.concatenate([dx, dxn], axis=0).astype(BF16)
        dxg_ref[...] = de[:tm]
        pcg_ref[0] = jnp.zeros((8, dfp), F32)
        pcv_ref[0] = jnp.zeros((8, dfp), F32)

        def products(c):
            cols = slice(c * cf, (c + 1) * cf)
            return (jnp.dot(h, wg_ref[:, cols], preferred_element_type=F32), jnp.dot(h, wv_ref[:, cols], preferred_element_type=F32),
                    lax.dot_general(de, wd_ref[cols, :], NT_DIMS, preferred_element_type=F32))

        def back(du, cw, up, s1, s2, pc_ref, cols):
            dup = (cw[2:3, :] * du + cw[1:2, :] * pltpu.roll(du, n - 1, 0) + cw[0:1, :] * pltpu.roll(du, n - 2, 0))[:tm]
            dut = du[:tm]
            pc_ref[0, 0:1, cols] = jnp.sum(dut * s2[HALO:HALO + tm], axis=0, keepdims=True)
            pc_ref[0, 1:2, cols] = jnp.sum(dut * s1[HALO:HALO + tm], axis=0, keepdims=True)
            pc_ref[0, 2:3, cols] = jnp.sum(dut * up[HALO:HALO + tm], axis=0, keepdims=True)
            pc_ref[0, 3:4, cols] = jnp.sum(dut, axis=0, keepdims=True)
            return dup.astype(BF16)

        def chunk(c, prods):
            cols = slice(c * cf, (c + 1) * cf)
            upg, upv, dact = prods
            cg, cv = cg_ref[:, cols], cv_ref[:, cols]
            ug, g1, g2 = _conv_taps(upg, cg, HALO, n)
            uv, v1, v2 = _conv_taps(upv, cv, HALO, n)
            sg = jax.nn.sigmoid(ug)
            sil = ug * sg
            act_ref[cols, :] = _transposed((sil * uv)[:tm].astype(BF16))
            dug_ref[:, cols] = back(dact * uv * (sg * (1.0 + ug * (1.0 - sg))), cg, upg, g1, g2, pcg_ref, cols)
            duv_ref[:, cols] = back(dact * sil, cv, upv, v1, v2, pcv_ref, cols)

        for c0 in range(0, nc, 2):
            group = list(range(c0, min(c0 + 2, nc)))
            prods = [products(c) for c in group]
            for c, pr in zip(group, prods):
                chunk(c, pr)

        dh = (lax.dot_general(dug_ref[...], wg_ref[...], NT_DIMS, preferred_element_type=F32)
              + lax.dot_general(duv_ref[...], wv_ref[...], NT_DIMS, preferred_element_type=F32))
        dxt, dshift, dscale, dgn = _norm_mod_bwd(x_ref[...], dh, g_ref[...], mod_ref[4:5, :])
        dx2_ref[...] = dx_ref[...] + dxt
        part_ref[0] = jnp.zeros((8, d), F32)
        part_ref[0, 0:1, :] = dshift
        part_ref[0, 1:2, :] = dscale
        part_ref[0, 2:3, :] = dgn

    row = lambda w: pl.BlockSpec((tm, w), lambda i: (i, 0))
    prev = pl.BlockSpec((HALO, d), lambda i: (jnp.maximum(i * hb - 1, 0), 0))
    nxt = pl.BlockSpec((HALO, d), lambda i: (jnp.minimum((i + 1) * hb, nhb - 1), 0))
    full = lambda a: pl.BlockSpec(a.shape, lambda i: (0,) * a.ndim)
    once = lambda a: pl.BlockSpec(a.shape, lambda i: (0,) * a.ndim, pipeline_mode=pl.Buffered(1))
    part = lambda w: pl.BlockSpec((1, 8, w), lambda i: (i, 0, 0))
    return _pcall(
        body, name="ffn_bwd", grid=(nt,),
        out_shape=[jax.ShapeDtypeStruct((t, d), F32), jax.ShapeDtypeStruct((t, dfp), BF16),
                   jax.ShapeDtypeStruct((t, dfp), BF16), jax.ShapeDtypeStruct((dfp, t), BF16),
                   jax.ShapeDtypeStruct((t, d), BF16), jax.ShapeDtypeStruct((nt, 8, d), F32),
                   jax.ShapeDtypeStruct((nt, 8, dfp), F32), jax.ShapeDtypeStruct((nt, 8, dfp), F32)],
        in_specs=[row(d), prev, nxt, row(d), nxt, full(mod8), full(g_mlp), once(wg), once(wv), once(cwg), once(cwv), once(wd)],
        out_specs=[row(d), row(dfp), row(dfp), pl.BlockSpec((dfp, tm), lambda i: (0, i)), row(d), part(d), part(dfp), part(dfp)],
        compiler_params=_params(("arbitrary",)),
    )(x2, x2, x2, dx3, dx3, mod8, g_mlp, wg, wv, cwg, cwv, wd)


BLK = 2 * LANES
XROWS = 144
LANE_FS, LANE_FT_A, LANE_FT_B = 0, 3, 6


def _head_masks():
    lane = lax.broadcasted_iota(jnp.int32, (1, LANES), 1)
    in_a = lane < HEAD_DIM
    return in_a, jnp.logical_not(in_a)


def _pieces3(x):
    hi = x.astype(BF16).astype(F32)
    r = x - hi
    mid = r.astype(BF16).astype(F32)
    return hi, mid, (r - mid).astype(BF16).astype(F32)


def _bias_lanes(rows, entries):
    sub = lax.broadcasted_iota(jnp.int32, (16, 1), 0)
    out = jnp.zeros((16, rows), F32)
    for l, v in entries:
        out = jnp.where(sub == l, v, out)
    return jnp.concatenate([out, jnp.zeros((LANES - 16, rows), F32)], axis=0).T


def _three(first, values):
    return [(first + k, v) for k, v in enumerate(values)]


def _stack_rows(x, in_a, in_b):
    zero = jnp.zeros_like(x)
    return jnp.concatenate([jnp.where(in_a, x, zero), jnp.where(in_b, x, zero)], axis=0)


def _transposed(x):
    return x.astype(F32).T.astype(BF16)


def _attn_operands(qkv, frow, dg):
    t = qkv.shape[0]
    p, nk = dg // LANES, t // BLK

    def body(qf_ref, kf_ref, vf_ref, ks_ref, vs_ref, f_ref, qx_ref, kx_ref, kxt_ref, vf_o, vft_o, ks_o, kst_o, vs_o, vst_o):
        in_a, in_b = _head_masks()
        fa, fb = _pieces3(f_ref[0]), _pieces3(f_ref[1])
        qx_ref[0, :, :LANES] = qf_ref[...]
        qx_ref[0, :, LANES:] = _bias_lanes(
            BLK, _three(LANE_FS, (-1.0,) * 3) + _three(LANE_FT_A, fa) + _three(LANE_FT_B, fb)).astype(BF16)
        kf = kf_ref[...]
        zero = jnp.zeros_like(kf)
        top = jnp.concatenate([jnp.where(in_a, kf, zero), _bias_lanes(
            BLK, _three(LANE_FS, fa) + _three(LANE_FT_A, (1.0,) * 3)).astype(BF16)], axis=1)
        bot = jnp.concatenate([jnp.where(in_b, kf, zero), _bias_lanes(
            BLK, _three(LANE_FS, fb) + _three(LANE_FT_B, (1.0,) * 3)).astype(BF16)], axis=1)
        kx = jnp.concatenate([top, bot], axis=0)
        kx_ref[0, 0] = kx
        kxt_ref[0, 0] = _transposed(kx)[:XROWS]
        for src, dst, dst_t in ((vf_ref, vf_o, vft_o), (ks_ref, ks_o, kst_o), (vs_ref, vs_o, vst_o)):
            st = _stack_rows(src[...], in_a, in_b)
            dst[0, 0] = st
            dst_t[0, 0] = _transposed(st)

    col = lambda base: pl.BlockSpec((BLK, LANES), lambda h, j: (j, base * p + h))
    blk4 = lambda r, c: pl.BlockSpec((1, 1, r, c), lambda h, j: (h, j, 0, 0))
    shp4 = lambda r, c: jax.ShapeDtypeStruct((p, nk, r, c), BF16)
    return _pcall(
        body, name="attn_operands", grid=(p, nk),
        out_shape=[jax.ShapeDtypeStruct((p, t, 2 * LANES), BF16), shp4(2 * BLK, 2 * LANES), shp4(XROWS, 2 * BLK)]
        + [shp4(2 * BLK, LANES), shp4(LANES, 2 * BLK)] * 3,
        in_specs=[col(0), col(1), col(2), col(4), col(5), pl.BlockSpec((2, 1, BLK), lambda h, j: (h, 0, j))],
        out_specs=[pl.BlockSpec((1, BLK, 2 * LANES), lambda h, j: (h, j, 0)), blk4(2 * BLK, 2 * LANES), blk4(XROWS, 2 * BLK)]
        + [blk4(2 * BLK, LANES), blk4(LANES, 2 * BLK)] * 3,
        compiler_params=_params(("arbitrary", "arbitrary")),
    )(qkv, qkv, qkv, qkv, qkv, frow)


def _fox_q_bwd(qkv, frow, lse_row, dg):
    t = qkv.shape[0]
    p = dg // LANES

    def body(q_ref, f_ref, l_ref, qx_ref):
        fa, fb = _pieces3(f_ref[0] - l_ref[0]), _pieces3(f_ref[1] - l_ref[1])
        qx_ref[0, :, :LANES] = q_ref[...]
        qx_ref[0, :, LANES:] = _bias_lanes(
            BLK, _three(LANE_FS, (-1.0,) * 3) + _three(LANE_FT_A, fa) + _three(LANE_FT_B, fb)).astype(BF16)

    row = pl.BlockSpec((2, 1, BLK), lambda h, j: (h, 0, j))
    return _pcall(
        body, name="fox_q_bwd", grid=(p, t // BLK),
        out_shape=jax.ShapeDtypeStruct((p, t, 2 * LANES), BF16),
        in_specs=[pl.BlockSpec((BLK, LANES), lambda h, j: (j, h)), row, row],
        out_specs=pl.BlockSpec((1, BLK, 2 * LANES), lambda h, j: (h, j, 0)),
        compiler_params=_params(("arbitrary", "arbitrary")),
    )(qkv, frow, lse_row)


def _key_query_masks():
    key = lax.broadcasted_iota(jnp.int32, (BLK, BLK), 0)
    qry = lax.broadcasted_iota(jnp.int32, (BLK, BLK), 1)
    return key <= qry, key < qry


def _key_triangle(kind):
    s = lax.broadcasted_iota(jnp.int32, (BLK, BLK), 0)
    j = lax.broadcasted_iota(jnp.int32, (BLK, BLK), 1)
    return {"suffix": j >= s, "prefix": j <= s, "before": j < s}[kind].astype(BF16)


def _tri_dot(tri, x, passes):
    acc = None
    for _ in range(passes):
        part = x.astype(BF16)
        d = jnp.dot(tri, part, preferred_element_type=F32)
        acc = d if acc is None else acc + d
        x = x - part.astype(F32)
    return acc


GROUPS = (4, 2, 1)


def _loop_blocks(n, tiles, carry, descending=False, groups=GROUPS):
    at = (lambda k: n - 1 - k) if descending else (lambda k: k)
    done = 0
    for g in groups:
        left = n - done
        carry = lax.fori_loop(0, left // g, lambda h, c, g=g, done=done: tiles([at(done + g * h + k) for k in range(g)], c), carry)
        done = done + (left // g) * g
    return carry


def _resident(shape):
    return pl.BlockSpec((1,) + shape, lambda h, i: (h,) + (0,) * len(shape), pipeline_mode=pl.Buffered(1))


def _rows_per_head(a, b):
    return jnp.concatenate([jnp.broadcast_to(a, (HEAD_DIM, BLK)), jnp.broadcast_to(b, (HEAD_DIM, BLK))], axis=0)


def _fold_heads(stacked, in_a):
    return jnp.where(in_a, stacked[:BLK], stacked[BLK:])


def _xy_gather_copies(ins, outs, send_sems, recv_sems, local_sems):
    x, y, c = lax.axis_index("x"), lax.axis_index("y"), lax.axis_index("c")
    chips = [(1 - x, y), (x, 1 - y), (1 - x, 1 - y)]
    mine = 2 * x + y
    local, remote = [], []
    for w in range(len(ins)):
        local.append(pltpu.make_async_copy(ins[w], outs[w].at[mine], local_sems.at[w]))
        for k, (px, py) in enumerate(chips):
            remote.append(pltpu.make_async_remote_copy(
                src_ref=ins[w], dst_ref=outs[w].at[mine], send_sem=send_sems.at[3 * w + k],
                recv_sem=recv_sems.at[3 * w + k], device_id=(px, py, c), device_id_type=MESH))
    return local, remote


def _fox_fwd(qx, kx, v_t, dg, shards):
    p, t = qx.shape[0], qx.shape[1]
    nq = t // BLK
    nh = 2 * p
    ns = len(shards)

    def body(q_ref, k_ref, vt_ref, *rest):
        shard_refs, (o_ref, lse_ref), gathered = rest[:ns], rest[ns:ns + 2], rest[ns + 2:2 * ns + 2]
        local, remote = _xy_gather_copies(shard_refs, gathered, *rest[2 * ns + 2:])
        i = pl.program_id(1)

        @pl.when((pl.program_id(0) == 0) & (i == 0))
        def _():
            for cp in local + remote:
                cp.start()

        causal, _ = _key_query_masks()
        q = q_ref[0]

        def scores(j, masked):
            s2 = lax.dot_general(k_ref[0, j], q, NT_DIMS, preferred_element_type=F32)
            s = [s2[a * BLK:(a + 1) * BLK] for a in range(2)]
            return [jnp.where(causal, x, NEG) for x in s] if masked else s

        def update(blocks, carry):
            m, l, acc = list(carry[0]), list(carry[1]), carry[2]
            for j, s in blocks:
                alpha, pr = [], []
                for a in range(2):
                    mn = jnp.maximum(m[a], jnp.max(s[a], axis=0, keepdims=True))
                    pa = jnp.exp(s[a] - mn)
                    al = jnp.exp(m[a] - mn)
                    l[a] = al * l[a] + jnp.sum(pa, axis=0, keepdims=True)
                    m[a] = mn
                    alpha.append(al)
                    pr.append(pa.astype(BF16))
                acc = _rows_per_head(*alpha) * acc + jnp.dot(vt_ref[0, j], jnp.concatenate(pr, axis=0), preferred_element_type=F32)
            return tuple(m), tuple(l), acc

        tiles = lambda js, c: update([(j, scores(j, False)) for j in js], c)
        neg, zero = jnp.full((1, BLK), NEG, F32), jnp.zeros((1, BLK), F32)
        carry = _loop_blocks(i, tiles, ((neg, neg), (zero, zero), jnp.zeros((LANES, BLK), F32)), groups=(8, 4, 2, 1))
        m, l, acc = update([(i, scores(i, True))], carry)
        o_ref[...] = acc / _rows_per_head(*l)
        lse_ref[0] = m[0] + jnp.log(l[0])
        lse_ref[1] = m[1] + jnp.log(l[1])

        @pl.when((pl.program_id(0) == p - 1) & (i == nq - 1))
        def _():
            for cp in remote:
                cp.wait_recv()
            for cp in remote:
                cp.wait_send()
            for cp in local:
                cp.wait()

    row = pl.BlockSpec((2, 1, BLK), lambda h, i: (h, 0, i))
    hbm = pl.BlockSpec(memory_space=pltpu.HBM)
    return _pcall(
        body, name="fox_fwd", grid=(p, nq),
        out_shape=[jax.ShapeDtypeStruct((dg, t), F32), jax.ShapeDtypeStruct((nh, 1, t), F32)]
        + [jax.ShapeDtypeStruct((4,) + s.shape, s.dtype) for s in shards],
        in_specs=[pl.BlockSpec((1, BLK, 2 * LANES), lambda h, i: (h, i, 0)), _resident((nq, 2 * BLK, 2 * LANES)),
                  _resident((nq, LANES, 2 * BLK))] + [hbm] * ns,
        out_specs=[pl.BlockSpec((LANES, BLK), lambda h, i: (h, i)), row] + [hbm] * ns,
        scratch_shapes=[pltpu.SemaphoreType.DMA((3 * ns,)), pltpu.SemaphoreType.DMA((3 * ns,)), pltpu.SemaphoreType.DMA((ns,))],
        compiler_params=_params(("arbitrary", "arbitrary")),
    )(qx, kx, v_t, *shards)


def _swap_copies(ins, outs, send_sems, recv_sems, local_sems):
    x, y, c = lax.axis_index("x"), lax.axis_index("y"), lax.axis_index("c")
    local, remote = [], []
    for w in range(len(ins)):
        local.append(pltpu.make_async_copy(ins[w], outs[w].at[c], local_sems.at[w]))
        remote.append(pltpu.make_async_remote_copy(
            src_ref=ins[w], dst_ref=outs[w].at[c], send_sem=send_sems.at[w], recv_sem=recv_sems.at[w],
            device_id=(x, y, 1 - c), device_id_type=MESH))
    return local, remote


def _fox_bwd(qxb, kx, kx_t, v_st, do, delta, dg, halves):
    p, t = qxb.shape[0], qxb.shape[1]
    nq = t // BLK
    nh = 2 * p
    ns = len(halves)

    def body(q_ref, k_ref, kt_ref, v_ref, do_ref, dl_ref, *rest):
        half_refs, (dq_ref, dft_ref, dk_ref, dv_ref, dkx_ref), both_refs = rest[:ns], rest[ns:ns + 5], rest[ns + 5:2 * ns + 5]
        local, remote = _swap_copies(half_refs, both_refs, *rest[2 * ns + 5:])
        i = pl.program_id(1)

        @pl.when((pl.program_id(0) == 0) & (i == 0))
        def _():
            for cp in local + remote:
                cp.start()

        @pl.when(i == 0)
        def _():
            dk_ref[...] = jnp.zeros_like(dk_ref)
            dv_ref[...] = jnp.zeros_like(dv_ref)
            dkx_ref[...] = jnp.zeros_like(dkx_ref)

        in_a, _ = _head_masks()
        first_lane = lax.broadcasted_iota(jnp.int32, (1, LANES), 1) == 0
        causal, _ = _key_query_masks()
        q, do2 = q_ref[0], do_ref[...]
        dl = (dl_ref[0], dl_ref[1])

        def products(j):
            return (lax.dot_general(k_ref[0, j], q, NT_DIMS, preferred_element_type=F32),
                    lax.dot_general(v_ref[0, j], do2, NT_DIMS, preferred_element_type=F32))

        def dscores(prod, masked):
            s2, dp2 = prod
            pr, ds = [], []
            for a in range(2):
                s = s2[a * BLK:(a + 1) * BLK]
                if masked:
                    s = jnp.where(causal, s, NEG)
                pa = jnp.exp(s)
                ds.append((pa * (dp2[a * BLK:(a + 1) * BLK] - dl[a])).astype(BF16))
                pr.append(pa.astype(BF16))
            return jnp.concatenate(ds, axis=0), jnp.concatenate(pr, axis=0)

        def accumulate(j, dsb, prb, dq):
            off = pl.multiple_of(j * BLK, BLK)
            dk_full = jnp.dot(dsb, q, preferred_element_type=F32)
            dk_ref[pl.ds(off, BLK), :] += _fold_heads(dk_full[:, :LANES], in_a)
            dkx_ref[pl.ds(off, BLK), :] += jnp.where(first_lane, dk_full[:BLK, LANES:], dk_full[BLK:, LANES:])
            dv_ref[pl.ds(off, BLK), :] += _fold_heads(jnp.dot(prb, do2, preferred_element_type=F32), in_a)
            return dq + jnp.dot(kt_ref[0, j], dsb, preferred_element_type=F32)

        def tiles(js, dq, masked=False):
            prods = [products(j) for j in js]
            grads = [dscores(pr, masked) for pr in prods]
            for j, (dsb, prb) in zip(js, grads):
                dq = accumulate(j, dsb, prb, dq)
            return dq

        dq = _loop_blocks(i, tiles, jnp.zeros((XROWS, BLK), F32))
        dq = tiles([i], dq, True)
        dq_ref[...] = dq[:LANES]
        dft_ref[0] = dq[LANES + LANE_FT_A:LANES + LANE_FT_A + 1]
        dft_ref[1] = dq[LANES + LANE_FT_B:LANES + LANE_FT_B + 1]

        @pl.when((pl.program_id(0) == p - 1) & (i == nq - 1))
        def _():
            for cp in remote:
                cp.wait_recv()
            for cp in remote:
                cp.wait_send()
            for cp in local:
                cp.wait()

    row = pl.BlockSpec((2, 1, BLK), lambda h, i: (h, 0, i))
    acc = pl.BlockSpec((t, LANES), lambda h, i: (0, h))
    hbm = pl.BlockSpec(memory_space=pltpu.HBM)
    return _pcall(
        body, name="fox_bwd", grid=(p, nq),
        out_shape=[jax.ShapeDtypeStruct((dg, t), F32), jax.ShapeDtypeStruct((nh, 1, t), F32)] + [jax.ShapeDtypeStruct((t, dg), F32)] * 3
        + [jax.ShapeDtypeStruct((2,) + h.shape, h.dtype) for h in halves],
        in_specs=[pl.BlockSpec((1, BLK, 2 * LANES), lambda h, i: (h, i, 0)), _resident((nq, 2 * BLK, 2 * LANES)),
                  _resident((nq, XROWS, 2 * BLK)), _resident((nq, 2 * BLK, LANES)),
                  pl.BlockSpec((BLK, LANES), lambda h, i: (i, h)), row] + [hbm] * ns,
        out_specs=[pl.BlockSpec((LANES, BLK), lambda h, i: (h, i)), row, acc, acc, acc] + [hbm] * ns,
        scratch_shapes=[pltpu.SemaphoreType.DMA((ns,)), pltpu.SemaphoreType.DMA((ns,)), pltpu.SemaphoreType.DMA((ns,))],
        compiler_params=_params(("arbitrary", "arbitrary")),
    )(qxb, kx, kx_t, v_st, do, delta, *halves)


def _softplus_of(z):
    return jnp.maximum(z, 0.0) + jnp.log(1.0 + jnp.exp(-jnp.abs(z)))


def _sb_fwd(qkv, k_st, v_t, dg):
    t = qkv.shape[0]
    p, nq = dg // LANES, t // BLK
    nh = 2 * p

    def body(q_ref, k_ref, vt_ref, o_ref, rt_ref):
        i = pl.program_id(1)
        _, strict = _key_query_masks()
        suffix = _key_triangle("suffix")
        q = q_ref[...]

        def scores(j):
            z2 = lax.dot_general(k_ref[0, j], q, NT_DIMS, preferred_element_type=F32)
            return [z2[a * BLK:(a + 1) * BLK] for a in range(2)]

        def suffix_sums(z, masked):
            out = []
            for a in range(2):
                sp = _softplus_of(z[a])
                if masked:
                    sp = jnp.where(strict, sp, 0.0)
                out.append(_tri_dot(suffix, sp, 2))
            return out

        def weights(z, cs, rest, masked):
            w, rest_new = [], []
            for a in range(2):
                wa = jnp.exp(z[a] - cs[a] - rest[a])
                if masked:
                    wa = jnp.where(strict, wa, 0.0)
                w.append(wa.astype(BF16))
                rest_new.append(rest[a] + cs[a][0:1])
            return jnp.concatenate(w, axis=0), tuple(rest_new)

        def tiles(js, carry, masked=False):
            rest, acc = carry
            zs = [scores(j) for j in js]
            css = [suffix_sums(z, masked) for z in zs]
            ws = []
            for z, cs in zip(zs, css):
                w2, rest = weights(z, cs, rest, masked)
                ws.append(w2)
            for j, w2 in zip(js, ws):
                acc = acc + jnp.dot(vt_ref[0, j], w2, preferred_element_type=F32)
            return rest, acc

        zero = jnp.zeros((1, BLK), F32)
        carry = tiles([i], ((zero, zero), jnp.zeros((LANES, BLK), F32)), True)
        rest, acc = _loop_blocks(i, tiles, carry, descending=True, groups=(8, 4, 2, 1))
        o_ref[...] = acc
        rt_ref[0] = rest[0]
        rt_ref[1] = rest[1]

    return _pcall(
        body, name="sb_fwd", grid=(p, nq),
        out_shape=[jax.ShapeDtypeStruct((dg, t), F32), jax.ShapeDtypeStruct((nh, 1, t), F32)],
        in_specs=[pl.BlockSpec((BLK, LANES), lambda h, i: (i, 3 * p + h)), _resident((nq, 2 * BLK, LANES)),
                  _resident((nq, LANES, 2 * BLK))],
        out_specs=[pl.BlockSpec((LANES, BLK), lambda h, i: (h, i)), pl.BlockSpec((2, 1, BLK), lambda h, i: (h, 0, i))],
        compiler_params=_params(("arbitrary", "arbitrary")),
    )(qkv, k_st, v_t)


def _scatter8_copies(ins, outs, send_sems, recv_sems, local_sems):
    x, y, c = lax.axis_index("x"), lax.axis_index("y"), lax.axis_index("c")
    me = 4 * x + 2 * y + c
    local, remote = [], []
    for w in range(len(ins)):
        local.append(pltpu.make_async_copy(ins[w].at[me], outs[w].at[me], local_sems.at[w]))
        for f in range(1, 8):
            px = 1 - x if f & 4 else x
            py = 1 - y if f & 2 else y
            pc = 1 - c if f & 1 else c
            remote.append(pltpu.make_async_remote_copy(
                src_ref=ins[w].at[4 * px + 2 * py + pc], dst_ref=outs[w].at[me],
                send_sem=send_sems.at[7 * w + f - 1], recv_sem=recv_sems.at[7 * w + f - 1],
                device_id=(px, py, pc), device_id_type=MESH))
    return local, remote


def _sb_bwd(qkv, k_st, k_t, v_st, do, rtot, dg, pieces):
    t = qkv.shape[0]
    p, nq = dg // LANES, t // BLK
    ns = len(pieces)

    def body(q_ref, k_ref, kt_ref, v_ref, do_ref, rt_ref, *rest):
        piece_refs, (dq_ref, dk_ref, dv_ref), recv_refs = rest[:ns], rest[ns:ns + 3], rest[ns + 3:2 * ns + 3]
        local, remote = _scatter8_copies(piece_refs, recv_refs, *rest[2 * ns + 3:])
        i = pl.program_id(1)

        @pl.when((pl.program_id(0) == 0) & (i == 0))
        def _():
            for cp in local + remote:
                cp.start()

        @pl.when(i == 0)
        def _():
            dk_ref[...] = jnp.zeros_like(dk_ref)
            dv_ref[...] = jnp.zeros_like(dv_ref)

        in_a, _ = _head_masks()
        _, strict = _key_query_masks()
        before_m, prefix_m = _key_triangle("before"), _key_triangle("prefix")
        q, do2 = q_ref[...], do_ref[...]
        rt = (rt_ref[0], rt_ref[1])

        def products(j):
            z2 = lax.dot_general(k_ref[0, j], q, NT_DIMS, preferred_element_type=F32)
            da2 = lax.dot_general(v_ref[0, j], do2, NT_DIMS, preferred_element_type=F32)
            return [z2[a * BLK:(a + 1) * BLK] for a in range(2)], [da2[a * BLK:(a + 1) * BLK] for a in range(2)]

        def softplus_sums(z, masked):
            sp = [_softplus_of(x) for x in z]
            if masked:
                sp = [jnp.where(strict, x, 0.0) for x in sp]
            return sp, [_tri_dot(before_m, x, 2) for x in sp]

        def weight_grads(z, da, sp, pre, before, masked):
            w, g, pg, before_new = [], [], [], []
            for a in range(2):
                wa = jnp.exp(z[a] + (before[a] - rt[a]) + pre[a])
                if masked:
                    wa = jnp.where(strict, wa, 0.0)
                ga = wa * da[a]
                w.append(wa.astype(BF16))
                g.append(ga)
                pg.append(jnp.dot(prefix_m, ga.astype(BF16), preferred_element_type=F32))
                before_new.append(before[a] + pre[a][BLK - 1:BLK] + sp[a][BLK - 1:BLK])
            return jnp.concatenate(w, axis=0), g, pg, tuple(before_new)

        def dlogits(sp, g, pg, gbefore, masked):
            dz, gbefore_new = [], []
            for a in range(2):
                s_incl = gbefore[a] + pg[a]
                dza = (g[a] - s_incl) + jnp.exp(-sp[a]) * s_incl
                if masked:
                    dza = jnp.where(strict, dza, 0.0)
                dz.append(dza.astype(BF16))
                gbefore_new.append(s_incl[BLK - 1:BLK])
            return jnp.concatenate(dz, axis=0), tuple(gbefore_new)

        def accumulate(j, dzb, wb, dq):
            off = pl.multiple_of(j * BLK, BLK)
            dk_ref[pl.ds(off, BLK), :] += _fold_heads(jnp.dot(dzb, q, preferred_element_type=F32), in_a)
            dv_ref[pl.ds(off, BLK), :] += _fold_heads(jnp.dot(wb, do2, preferred_element_type=F32), in_a)
            return dq + jnp.dot(kt_ref[0, j], dzb, preferred_element_type=F32)

        def tiles(js, carry, masked=False):
            before, gbefore, dq = carry
            prods = [products(j) for j in js]
            sums = [softplus_sums(z, masked) for z, _ in prods]
            grads = []
            for (z, da), (sp, pre) in zip(prods, sums):
                wb, g, pg, before = weight_grads(z, da, sp, pre, before, masked)
                grads.append((wb, g, pg))
            for j, (sp, _), (wb, g, pg) in zip(js, sums, grads):
                dzb, gbefore = dlogits(sp, g, pg, gbefore, masked)
                dq = accumulate(j, dzb, wb, dq)
            return before, gbefore, dq

        zero = jnp.zeros((1, BLK), F32)
        carry = _loop_blocks(i, tiles, ((zero, zero), (zero, zero), jnp.zeros((LANES, BLK), F32)), groups=(2, 1))
        dq_ref[...] = tiles([i], carry, True)[2]

        @pl.when((pl.program_id(0) == p - 1) & (i == nq - 1))
        def _():
            for cp in remote:
                cp.wait_recv()
            for cp in remote:
                cp.wait_send()
            for cp in local:
                cp.wait()

    acc = pl.BlockSpec((t, LANES), lambda h, i: (0, h))
    hbm = pl.BlockSpec(memory_space=pltpu.HBM)
    return _pcall(
        body, name="sb_bwd", grid=(p, nq),
        out_shape=[jax.ShapeDtypeStruct((dg, t), F32)] + [jax.ShapeDtypeStruct((t, dg), F32)] * 2
        + [jax.ShapeDtypeStruct(pc.shape, pc.dtype) for pc in pieces],
        in_specs=[pl.BlockSpec((BLK, LANES), lambda h, i: (i, 3 * p + h)), _resident((nq, 2 * BLK, LANES)),
                  _resident((nq, LANES, 2 * BLK)), _resident((nq, 2 * BLK, LANES)),
                  pl.BlockSpec((BLK, LANES), lambda h, i: (i, h)), pl.BlockSpec((2, 1, BLK), lambda h, i: (h, 0, i))] + [hbm] * ns,
        out_specs=[pl.BlockSpec((LANES, BLK), lambda h, i: (h, i)), acc, acc] + [hbm] * ns,
        scratch_shapes=[pltpu.SemaphoreType.DMA((7 * ns,)), pltpu.SemaphoreType.DMA((7 * ns,)), pltpu.SemaphoreType.DMA((ns,))],
        compiler_params=_params(("arbitrary", "arbitrary")),
    )(qkv, k_st, k_t, v_st, do, rtot, *pieces)


def _tri_constants(nh, t):
    nb = t // LANES
    r = nh * nb
    li = np.arange(LANES)
    tri_in = (li[:, None] <= li[None, :])
    ri = np.arange(r)
    same = (ri[:, None] // nb) == (ri[None, :] // nb)
    blk = same & (ri[None, :] < ri[:, None])
    blk_rev = same & (ri[None, :] > ri[:, None])
    head_rows = (np.arange(max(8, nh))[:, None] == (ri[None, :] // nb))
    as_bf16 = lambda a: jnp.asarray(a.astype(np.float32), BF16)
    return as_bf16(tri_in), as_bf16(blk), as_bf16(tri_in.T), as_bf16(blk_rev), as_bf16(head_rows)


def kernel(x, c, w_ada, b_ada, g_attn, w_in, b_fgate, g_out_fox, g_out_sb, w_out, g_mlp, w_up, conv_w, conv_b, w_down, g_final, loss_target, m_w_ada, m_b_ada, m_g_attn, m_w_in, m_b_fgate, m_g_out_fox, m_g_out_sb, m_w_out, m_g_mlp, m_w_up, m_conv_w, m_conv_b, m_w_down, m_g_final, v_w_ada, v_b_ada, v_g_attn, v_w_in, v_b_fgate, v_g_out_fox, v_g_out_sb, v_w_out, v_g_mlp, v_w_up, v_conv_w, v_conv_b, v_w_down, v_g_final):
    t, d = x.shape[1], x.shape[2]
    dg = d // 2
    nh = dg // HEAD_DIM
    n_in = 6 * dg + nh
    dff = w_down.shape[1] * 4
    dfp = -(-dff // 256) * 256
    cf = 256
    tm = _tile(t, (512, 256, 128))
    nq = t // BLK
    xi, yi, ci = lax.axis_index("x"), lax.axis_index("y"), lax.axis_index("c")
    shard = 2 * xi + yi
    me = 4 * xi + 2 * yi + ci

    x2d, tg2d = x[0], loss_target[0]

    c_all = _all_gather8(jnp.pad(c, ((0, 7), (0, 0)))).reshape(8, 8, d)[:, 0, :]
    ada_cols = w_ada.shape[2]
    b_shard = lax.dynamic_slice(b_ada, (0, shard * ada_cols), (1, ada_cols))
    sc_all, mod_shard = _ada_fwd(c_all, w_ada[0], b_shard)
    mod_all = _all_gather8(mod_shard).reshape(4, 2, 8, ada_cols)
    mod_me = lax.dynamic_index_in_dim(mod_all[:, 0], me, axis=1, keepdims=False)
    mod8 = jnp.pad(mod_me.reshape(6, d), ((0, 2), (0, 0)))

    lane_pad = lambda a: jnp.pad(a, ((0, 0),) * (a.ndim - 1) + ((0, -a.shape[-1] % LANES),))
    (g_in,) = _gather_xy([lane_pad(w_in[0].astype(BF16))])
    later_shards = [w_out[0].astype(BF16), lane_pad(w_up[0].astype(BF16)), w_down[0].astype(BF16), lane_pad(conv_w[0])]
    w_in_full = jnp.transpose(g_in[:, :, :n_in // 4], (1, 0, 2)).reshape(d, n_in)
    w_qkv = w_in_full[:, :6 * dg]
    w_f = jnp.pad(w_in_full[:, 6 * dg:], ((0, 0), (0, LANES - nh)))

    qkv, fl, h1 = _in_proj_fwd(x2d, mod8, g_attn, w_qkv, w_f, tm)
    tri_in, tri_blk, tri_in_rev, tri_blk_rev, head_rows = _tri_constants(nh, t)
    fl2d = fl[:, :nh].T.reshape(nh * t // LANES, LANES)
    b_rows = jnp.repeat(b_fgate[0], t // LANES)[:, None]
    f2d = _fgate_fwd(fl2d, b_rows, tri_in, tri_blk)
    frow = f2d.reshape(nh, 1, t)
    pairs = nh // 2
    qx, kx, kx_t, vf_st, vf_t, ks_st, ks_t, vs_st, vs_t = _attn_operands(qkv, frow, dg)
    o_fox, lse, g_out, g_up, g_down, g_cw = _fox_fwd(qx, kx, vf_t, dg, later_shards)
    g_up, g_cw = g_up[:, :, :dff // 2], g_cw[:, :, :dff // 2]
    w_out_full = g_out.reshape(2 * dg, d)
    w_up_full = jnp.transpose(g_up, (1, 0, 2)).reshape(d, 2 * dff)
    padc = ((0, 0), (0, dfp - dff))
    wg, wv = jnp.pad(w_up_full[:, :dff], padc), jnp.pad(w_up_full[:, dff:], padc)
    wd = jnp.pad(g_down.reshape(dff, d), ((0, dfp - dff), (0, 0)))
    cw_full = jnp.transpose(g_cw, (1, 0, 2)).reshape(3, 2 * dff)
    cw4 = jnp.concatenate([cw_full, conv_b], axis=0)
    cwg = jnp.pad(cw4[:, :dff], ((0, 4), (0, dfp - dff)))
    cwv = jnp.pad(cw4[:, dff:], ((0, 4), (0, dfp - dff)))
    o_sb, rtot = _sb_fwd(qkv, ks_st, vs_t, dg)
    li = np.arange(dg)
    bd = jnp.asarray((li[:, None] // HEAD_DIM == li[None, :] // HEAD_DIM).astype(np.float32), BF16)
    hsel = jnp.asarray((li[:, None] // HEAD_DIM == np.arange(LANES)[None, :]).astype(np.float32), BF16)
    x2, mix, mix_t = _attn_out_fwd(x2d, o_fox, o_sb, g_out_fox, g_out_sb, w_out_full, mod8, bd, tm)
    g_final2 = g_final[None, :]
    dx3, h2, part_f = _ffn_fwd(x2, tg2d, mod8, g_mlp, g_final2, wg, wv, cwg, cwv, wd, tm, cf)

    tm_ffn_bwd = _tile(t, (256, 128))
    dx2, dupg, dupv, act, dxg3, part_b, pcg, pcv = _ffn_bwd(x2, dx3, mod8, g_mlp, wg, wv, cwg, cwv, wd, tm_ffn_bwd, cf)
    do_fox, do_sb, delta, dxg2, part_o = _attn_out_bwd(dx2, mix, o_fox, o_sb, g_out_fox, g_out_sb, w_out_full, mod8, bd, hsel, tm)
    drow = delta[:, :nh].T.reshape(nh, 1, t)

    def col_pieces(g):
        r, cc = g.shape
        return jnp.transpose(g.reshape(2, r // 2, 4, cc // 4), (2, 0, 1, 3)).reshape(8, r // 2, cc // 4)

    def row_pieces(g):
        r, cc = g.shape
        return g.reshape(8, r // 8, cc)

    gw_out = _matmul_tn(mix_t, dxg2, "grad_w_out")
    gw_upg = _matmul_tn(h2, dupg, "grad_w_up_gate")
    gw_upv = _matmul_tn(h2, dupv, "grad_w_up_val")
    gw_up = jnp.concatenate([gw_upg[:, :dff], gw_upv[:, :dff]], axis=1)
    gw_down = _matmul_tn(act, dxg3, "grad_w_down")[:dff]
    early = (row_pieces(gw_out), lane_pad(col_pieces(gw_up)), row_pieces(gw_down))
    early = [_to_bf16(pc, "pieces_bf16_" + nm) for pc, nm in zip(early, ("w_out", "w_up", "w_down"))]

    qxb = _fox_q_bwd(qkv, frow, lse, dg)
    dq_s, dk_s, dv_s, *recv_early = _sb_bwd(qkv, ks_st, ks_t, vs_st, do_sb, rtot, dg, early)
    halves_early = [_sum_leading(rv, nm) for rv, nm in zip(recv_early, ("sum_w_out", "sum_w_up", "sum_w_down"))]
    dq_f, dft, dk_f, dv_f, dkx, *swapped_early = _fox_bwd(qxb, kx, kx_t, vf_st, do_fox, drow, dg, halves_early)
    f2d_shape = (nh * t // LANES, LANES)
    dfs = jnp.transpose(dkx.reshape(t, pairs, LANES)[:, :, :2], (1, 2, 0))
    dfl2d, gb8 = _fgate_bwd(fl2d, b_rows, dft.reshape(f2d_shape), dfs.reshape(f2d_shape), tri_in_rev, tri_blk_rev, head_rows)
    dfl = jnp.pad(dfl2d.reshape(nh, t).T, ((0, 0), (0, LANES - nh)))
    grad_x, dproj, dflb, part_i = _in_proj_bwd([dq_f, dk_f, dv_f, dq_s, dk_s, dv_s], dfl, w_qkv, w_f, x2d, dx2, mod8, g_attn, tm)

    gw_qkv = _matmul_tn(h1, dproj, "grad_w_qkv")
    gw_f = _matmul_tn(h1, dflb, "grad_w_f")
    gw_in = jnp.concatenate([gw_qkv, gw_f[:, :nh]], axis=1)

    sf = _sum_leading(part_f, "sum_part_ffn_fwd")
    sb_ = _sum_leading(part_b, "sum_part_ffn_bwd")
    so = _sum_leading(part_o, "sum_part_attn_out")
    si = _sum_leading(part_i, "sum_part_in_proj")
    scg = _sum_leading(pcg, "sum_part_conv_gate")
    scv = _sum_leading(pcv, "sum_part_conv_val")
    gb_f = gb8[:nh, 0]
    dmod = jnp.concatenate([si[0], si[1], so[0], sb_[0], sb_[1], sf[1]])
    g_conv_w = jnp.concatenate([scg[0:3, :dff], scv[0:3, :dff]], axis=1).reshape(-1)
    g_conv_b = jnp.concatenate([scg[3, :dff], scv[3, :dff]])
    loss_part = jnp.sum(sf[2])
    fields = [dmod, si[2], gb_f, so[1, :dg], so[1, dg:], sb_[2], g_conv_b, sf[0], g_conv_w, loss_part[None]]
    sizes = [int(f.shape[0]) for f in fields]
    n_pack = sum(sizes)
    lanes_pack = -(-n_pack // (8 * LANES)) * LANES
    pack = jnp.pad(jnp.concatenate(fields), (0, 8 * lanes_pack - n_pack)).reshape(8, lanes_pack)
    gathered = _all_gather8(pack)
    tot = _sum_leading(gathered.reshape(8, 8, lanes_pack), "sum_pack").reshape(-1)
    offs = np.concatenate([[0], np.cumsum(sizes)])
    take = lambda k: tot[int(offs[k]):int(offs[k + 1])]
    g_b_ada, g_g_attn, g_b_fgate, g_g_fox, g_g_sb, g_g_mlp, g_cb, g_g_final, g_cw_full, loss_v = [take(k) for k in range(10)]
    loss = loss_v[0]
    dmod_all = gathered.reshape(8, 8 * lanes_pack)[:, :6 * d]
    dmod_cols = lax.dynamic_slice(dmod_all, (0, shard * ada_cols), (8, ada_cols))
    g_w_ada = _ada_bwd(sc_all.T, dmod_cols)

    (recv_in,) = _scatter8([_to_bf16(lane_pad(col_pieces(gw_in)), "pieces_bf16_w_in")])
    (swapped_in,) = _swap_halves([_sum_leading(recv_in, "sum_w_in")])
    swapped = [swapped_in] + swapped_early
    shard_cols = (n_in // 4, d, dff // 2, d)
    g_w_in, g_w_out, g_w_up, g_w_down = [s.reshape(2 * s.shape[1], s.shape[2])[:, :cc] for s, cc in zip(swapped, shard_cols)]
    g_conv_w_shard = lax.dynamic_slice(g_cw_full.reshape(3, 2 * dff), (0, shard * (dff // 2)), (3, dff // 2))

    grads, deltas, new_m, new_v = {}, {}, {}, {}

    def step(name, w, g, m, v):
        shape = w.shape
        as2d = lambda a: a.reshape(-1, shape[-1])
        dl, nm, nv = _adamw(as2d(w), as2d(g), as2d(m), as2d(v), "adamw_" + name)
        grads[name], deltas[name], new_m[name], new_v[name] = g.reshape(shape), dl.reshape(shape), nm.reshape(shape), nv.reshape(shape)

    step("w_ada", w_ada, g_w_ada, m_w_ada, v_w_ada)
    step("w_in", w_in, g_w_in, m_w_in, v_w_in)
    step("w_out", w_out, g_w_out, m_w_out, v_w_out)
    step("w_up", w_up, g_w_up, m_w_up, v_w_up)
    step("conv_w", conv_w, g_conv_w_shard, m_conv_w, v_conv_w)
    step("w_down", w_down, g_w_down, m_w_down, v_w_down)

    small = [("b_ada", b_ada, g_b_ada, m_b_ada, v_b_ada), ("g_attn", g_attn, g_g_attn, m_g_attn, v_g_attn),
             ("b_fgate", b_fgate, g_b_fgate, m_b_fgate, v_b_fgate), ("g_out_fox", g_out_fox, g_g_fox, m_g_out_fox, v_g_out_fox),
             ("g_out_sb", g_out_sb, g_g_sb, m_g_out_sb, v_g_out_sb), ("g_mlp", g_mlp, g_g_mlp, m_g_mlp, v_g_mlp),
             ("conv_b", conv_b, g_cb, m_conv_b, v_conv_b), ("g_final", g_final, g_g_final, m_g_final, v_g_final)]
    ssz = [int(np.prod(s[1].shape)) for s in small]
    n_small = sum(ssz)
    lanes_small = -(-n_small // (8 * LANES)) * LANES
    packs = [jnp.pad(jnp.concatenate([s[k].reshape(-1) for s in small]), (0, 8 * lanes_small - n_small)).reshape(8, lanes_small)
             for k in (1, 2, 3, 4)]
    dl_s, nm_s, nv_s = _adamw(*packs, "adamw_small")
    so_ = np.concatenate([[0], np.cumsum(ssz)])
    for k, s in enumerate(small):
        cut = lambda a: a.reshape(-1)[int(so_[k]):int(so_[k + 1])].reshape(s[1].shape)
        grads[s[0]], deltas[s[0]], new_m[s[0]], new_v[s[0]] = s[2].reshape(s[1].shape), cut(dl_s), cut(nm_s), cut(nv_s)

    order = ["w_ada", "b_ada", "g_attn", "w_in", "b_fgate", "g_out_fox", "g_out_sb", "w_out", "g_mlp", "w_up",
             "conv_w", "conv_b", "w_down", "g_final"]
    return (loss, grad_x[None], *[grads[n] for n in order], *[deltas[n] for n in order],
            *[new_m[n] for n in order], *[new_v[n] for n in order])
```

```python
import functools

import numpy as np
import jax
import jax.numpy as jnp
from jax import lax
from jax.experimental import pallas as pl
from jax.experimental.pallas import tpu as pltpu

F32 = jnp.float32
BF16 = jnp.bfloat16
MESH = pl.DeviceIdType.MESH

HEAD_DIM = 64
LANES = 128
EPS = 1e-6
NEG = -1e30
ADAM_LR, ADAM_B1, ADAM_B2, ADAM_EPS, ADAM_WD, ADAM_STEP = 0.001, 0.9, 0.999, 1e-08, 0.01, 10
V7X_VMEM_BYTES = 64 * 1024 * 1024
VMEM_LIMIT = V7X_VMEM_BYTES - 12 * 1024 * 1024
NT_DIMS = (((1,), (1,)), ((), ()))


def _pcall(body, **kw):
    return pl.pallas_call(body, **kw)


def _params(sem=None, **kw):
    return pltpu.CompilerParams(dimension_semantics=sem, vmem_limit_bytes=VMEM_LIMIT, **kw)


def _split_dot(x, m, passes):
    acc = None
    for _ in range(passes):
        part = x.astype(BF16)
        d = jnp.dot(part, m, preferred_element_type=F32)
        acc = d if acc is None else acc + d
        x = x - part.astype(F32)
    return acc


def _tile(n, candidates):
    for t in candidates:
        if n % t == 0:
            return t
    return n


def _rows_tile(rows, row_bytes, budget=2 * 1024 * 1024):
    best = None
    for t in range(8, rows + 1, 8):
        if rows % t == 0 and t * row_bytes <= budget:
            best = t
    return best if best is not None else rows


def _all_gather8(v):
    m_per, n = v.shape

    def body(x_ref, out_ref, send_sems, recv_sems, local_sem):
        x, y, c = lax.axis_index("x"), lax.axis_index("y"), lax.axis_index("c")
        me, sibling = (x, y, c), (x, y, 1 - c)
        chips = [(1 - x, y), (x, 1 - y), (1 - x, 1 - y)]

        def rows(px, py, pc):
            return out_ref.at[pl.ds((4 * px + 2 * py + pc) * m_per, m_per), :]

        def copy(k, block, to, src=None):
            return pltpu.make_async_remote_copy(
                src_ref=rows(*block) if src is None else src, dst_ref=rows(*block),
                send_sem=send_sems.at[k], recv_sem=recv_sems.at[k], device_id=to, device_id_type=MESH)

        mine = pltpu.make_async_copy(x_ref, rows(*me), local_sem)
        mine.start()
        first = [copy(0, me, sibling, src=x_ref)]
        first += [copy(1 + j, me, (*chip, c), src=x_ref) for j, chip in enumerate(chips)]
        for cp in first:
            cp.start()
        passed = [copy(4 + j, (*chip, c), sibling) for j, chip in enumerate(chips)]
        for j, chip in enumerate(chips):
            copy(1 + j, (*chip, c), me).wait_recv()
            passed[j].start()
        copy(0, sibling, me).wait_recv()
        for j, chip in enumerate(chips):
            copy(4 + j, (*chip, 1 - c), me).wait_recv()
        for cp in first + passed:
            cp.wait_send()
        mine.wait()

    return _pcall(
        body, name="all_gather8",
        out_shape=jax.ShapeDtypeStruct((8 * m_per, n), v.dtype),
        in_specs=[pl.BlockSpec(memory_space=pltpu.VMEM)],
        out_specs=pl.BlockSpec(memory_space=pltpu.VMEM),
        scratch_shapes=[pltpu.SemaphoreType.DMA((7,)), pltpu.SemaphoreType.DMA((7,)), pltpu.SemaphoreType.DMA],
        compiler_params=pltpu.CompilerParams(vmem_limit_bytes=VMEM_LIMIT),
    )(v)


def _gather_xy(shards):
    n = len(shards)

    def body(*refs):
        ins, outs = refs[:n], refs[n:2 * n]
        send_sems, recv_sems, local_sems = refs[2 * n:]
        x, y, c = lax.axis_index("x"), lax.axis_index("y"), lax.axis_index("c")
        chips = [(1 - x, y), (x, 1 - y), (1 - x, 1 - y)]
        mine = 2 * x + y
        local, remote = [], []
        for w in range(n):
            cp = pltpu.make_async_copy(ins[w], outs[w].at[mine], local_sems.at[w])
            cp.start()
            local.append(cp)
            for k, (px, py) in enumerate(chips):
                cp = pltpu.make_async_remote_copy(
                    src_ref=ins[w], dst_ref=outs[w].at[mine], send_sem=send_sems.at[3 * w + k],
                    recv_sem=recv_sems.at[3 * w + k], device_id=(px, py, c), device_id_type=MESH)
                cp.start()
                remote.append(cp)
        for cp in remote:
            cp.wait_recv()
        for cp in remote:
            cp.wait_send()
        for cp in local:
            cp.wait()

    hbm = pl.BlockSpec(memory_space=pltpu.HBM)
    return _pcall(
        body, name="gather_xy",
        out_shape=[jax.ShapeDtypeStruct((4,) + s.shape, s.dtype) for s in shards],
        in_specs=[hbm] * n, out_specs=[hbm] * n,
        scratch_shapes=[pltpu.SemaphoreType.DMA((3 * n,)), pltpu.SemaphoreType.DMA((3 * n,)),
                        pltpu.SemaphoreType.DMA((n,))],
        compiler_params=pltpu.CompilerParams(vmem_limit_bytes=VMEM_LIMIT),
    )(*shards)


def _scatter8(pieces):
    n = len(pieces)

    def body(*refs):
        ins, outs = refs[:n], refs[n:2 * n]
        send_sems, recv_sems, local_sems = refs[2 * n:]
        x, y, c = lax.axis_index("x"), lax.axis_index("y"), lax.axis_index("c")
        me = 4 * x + 2 * y + c
        local, remote = [], []
        for w in range(n):
            cp = pltpu.make_async_copy(ins[w].at[me], outs[w].at[me], local_sems.at[w])
            cp.start()
            local.append(cp)
            for f in range(1, 8):
                px = 1 - x if f & 4 else x
                py = 1 - y if f & 2 else y
                pc = 1 - c if f & 1 else c
                cp = pltpu.make_async_remote_copy(
                    src_ref=ins[w].at[4 * px + 2 * py + pc], dst_ref=outs[w].at[me],
                    send_sem=send_sems.at[7 * w + f - 1], recv_sem=recv_sems.at[7 * w + f - 1],
                    device_id=(px, py, pc), device_id_type=MESH)
                cp.start()
                remote.append(cp)
        for cp in remote:
            cp.wait_recv()
        for cp in remote:
            cp.wait_send()
        for cp in local:
            cp.wait()

    hbm = pl.BlockSpec(memory_space=pltpu.HBM)
    return _pcall(
        body, name="scatter8",
        out_shape=[jax.ShapeDtypeStruct(p.shape, p.dtype) for p in pieces],
        in_specs=[hbm] * n, out_specs=[hbm] * n,
        scratch_shapes=[pltpu.SemaphoreType.DMA((7 * n,)), pltpu.SemaphoreType.DMA((7 * n,)),
                        pltpu.SemaphoreType.DMA((n,))],
        compiler_params=pltpu.CompilerParams(vmem_limit_bytes=VMEM_LIMIT),
    )(*pieces)


def _swap_halves(halves):
    n = len(halves)
    chunks = 8
    n_chunks = [max(k for k in (chunks, 4, 2, 1) if h.shape[0] % (8 * k) == 0) for h in halves]

    def body(*refs):
        ins, outs = refs[:n], refs[n:2 * n]
        send_sems, recv_sems, local_sems = refs[2 * n:]
        x, y, c = lax.axis_index("x"), lax.axis_index("y"), lax.axis_index("c")
        local, remote = [], []
        for w in range(n):
            cp = pltpu.make_async_copy(ins[w], outs[w].at[c], local_sems.at[w])
            cp.start()
            local.append(cp)
            rows = ins[w].shape[0] // n_chunks[w]
            for k in range(n_chunks[w]):
                cp = pltpu.make_async_remote_copy(
                    src_ref=ins[w].at[pl.ds(k * rows, rows)], dst_ref=outs[w].at[c, pl.ds(k * rows, rows)],
                    send_sem=send_sems.at[chunks * w + k], recv_sem=recv_sems.at[chunks * w + k],
                    device_id=(x, y, 1 - c), device_id_type=MESH)
                cp.start()
                remote.append(cp)
        for cp in remote:
            cp.wait_recv()
        for cp in remote:
            cp.wait_send()
        for cp in local:
            cp.wait()

    hbm = pl.BlockSpec(memory_space=pltpu.HBM)
    return _pcall(
        body, name="swap_halves",
        out_shape=[jax.ShapeDtypeStruct((2,) + h.shape, h.dtype) for h in halves],
        in_specs=[hbm] * n, out_specs=[hbm] * n,
        scratch_shapes=[pltpu.SemaphoreType.DMA((chunks * n,)), pltpu.SemaphoreType.DMA((chunks * n,)),
                        pltpu.SemaphoreType.DMA((n,))],
        compiler_params=pltpu.CompilerParams(vmem_limit_bytes=VMEM_LIMIT),
    )(*halves)


def _sum_leading(a, name):
    n, r, c = a.shape
    tr = _rows_tile(r, n * c * 4, budget=6 * 1024 * 1024)
    if a.dtype == BF16 and tr % 16:
        tr = r

    def body(a_ref, o_ref):
        acc = a_ref[0].astype(F32)
        for k in range(1, n):
            acc = acc + a_ref[k].astype(F32)
        o_ref[...] = acc

    return _pcall(
        body, name=name, grid=(r // tr,),
        out_shape=jax.ShapeDtypeStruct((r, c), F32),
        in_specs=[pl.BlockSpec((n, tr, c), lambda i: (0, i, 0))],
        out_specs=pl.BlockSpec((tr, c), lambda i: (i, 0)),
        compiler_params=_params(("arbitrary",)),
    )(a)


def _to_bf16(a, name):
    n, r, c = a.shape

    def body(a_ref, o_ref):
        o_ref[...] = a_ref[...].astype(BF16)

    spec = pl.BlockSpec((1, r, c), lambda i: (i, 0, 0))
    return _pcall(
        body, name=name, grid=(n,), out_shape=jax.ShapeDtypeStruct(a.shape, BF16),
        in_specs=[spec], out_specs=spec, compiler_params=_params(("arbitrary",)),
    )(a)


def _adamw(w, g, m, v, name):
    r, c = w.shape
    tr = _rows_tile(r, c * 4, budget=1024 * 1024)
    c1 = 1.0 - ADAM_B1 ** ADAM_STEP
    c2 = 1.0 - ADAM_B2 ** ADAM_STEP

    def body(w_ref, g_ref, m_ref, v_ref, d_ref, nm_ref, nv_ref):
        gg = g_ref[...]
        nm = ADAM_B1 * m_ref[...] + (1.0 - ADAM_B1) * gg
        nv = ADAM_B2 * v_ref[...] + (1.0 - ADAM_B2) * (gg * gg)
        m_hat = nm / c1
        v_hat = nv / c2
        d_ref[...] = -ADAM_LR * (m_hat / (jnp.sqrt(v_hat) + ADAM_EPS) + ADAM_WD * w_ref[...])
        nm_ref[...] = nm
        nv_ref[...] = nv

    spec = pl.BlockSpec((tr, c), lambda i: (i, 0))
    return _pcall(
        body, name=name, grid=(r // tr,),
        out_shape=[jax.ShapeDtypeStruct((r, c), F32)] * 3,
        in_specs=[spec] * 4, out_specs=[spec] * 3,
        compiler_params=_params(("arbitrary",)),
    )(w, g, m, v)


def _ada_fwd(c_all, w_shard, b_shard):
    nb, d = c_all.shape
    cols = w_shard.shape[1]

    def body(c_ref, w_ref, b_ref, sc_ref, mod_ref):
        cv = c_ref[...]
        sc = cv * jax.nn.sigmoid(cv)
        sc_ref[...] = sc
        mod_ref[...] = jnp.dot(sc.astype(BF16), w_ref[...].astype(BF16), preferred_element_type=F32) + b_ref[...]

    return _pcall(
        body, name="ada_fwd",
        out_shape=[jax.ShapeDtypeStruct((nb, d), F32), jax.ShapeDtypeStruct((nb, cols), F32)],
        compiler_params=pltpu.CompilerParams(vmem_limit_bytes=VMEM_LIMIT),
    )(c_all, w_shard, b_shard)


def _ada_bwd(sc_t, dmod_cols):
    d, nb = sc_t.shape
    cols = dmod_cols.shape[1]
    tr = _rows_tile(d, cols * 4, budget=1024 * 1024)

    def body(s_ref, m_ref, o_ref):
        s = s_ref[...]
        m = m_ref[...]
        acc = s[:, 0:1] * m[0:1, :]
        for b in range(1, nb):
            acc = acc + s[:, b:b + 1] * m[b:b + 1, :]
        o_ref[...] = acc

    return _pcall(
        body, name="ada_bwd", grid=(d // tr,),
        out_shape=jax.ShapeDtypeStruct((d, cols), F32),
        in_specs=[pl.BlockSpec((tr, nb), lambda i: (i, 0)), pl.BlockSpec((nb, cols), lambda i: (0, 0))],
        out_specs=pl.BlockSpec((tr, cols), lambda i: (i, 0)),
        compiler_params=_params(("arbitrary",)),
    )(sc_t, dmod_cols)


def _log_sigmoid(x):
    return jnp.minimum(x, 0.0) - jnp.log1p(jnp.exp(-jnp.abs(x)))


def _fgate_fwd(fl2d, b_rows, tri_in, tri_blk):
    r = fl2d.shape[0]

    def body(x_ref, b_ref, u_ref, l_ref, f_ref):
        lf = _log_sigmoid(x_ref[...] + b_ref[...])
        c1 = _split_dot(lf, u_ref[...], 3)
        tot = jnp.broadcast_to(c1[:, LANES - 1:LANES], (r, LANES))
        acc = None
        for _ in range(3):
            part = tot.astype(BF16)
            dd = jnp.dot(l_ref[...], part, preferred_element_type=F32)
            acc = dd if acc is None else acc + dd
            tot = tot - part.astype(F32)
        f_ref[...] = c1 + acc

    return _pcall(
        body, name="fgate_fwd", out_shape=jax.ShapeDtypeStruct((r, LANES), F32),
        compiler_params=pltpu.CompilerParams(vmem_limit_bytes=VMEM_LIMIT),
    )(fl2d, b_rows, tri_in, tri_blk)


def _fgate_bwd(fl2d, b_rows, df_query, df_key, tri_in_rev, tri_blk_rev, head_rows):
    r = fl2d.shape[0]
    nhp = head_rows.shape[0]

    def body(x_ref, b_ref, dq_ref, dk_ref, u_ref, l_ref, hr_ref, o_ref, gb_ref):
        c1 = _split_dot(dq_ref[...] + dk_ref[...], u_ref[...], 3)
        tot = jnp.broadcast_to(c1[:, 0:1], (r, LANES))
        acc = None
        for _ in range(3):
            part = tot.astype(BF16)
            dd = jnp.dot(l_ref[...], part, preferred_element_type=F32)
            acc = dd if acc is None else acc + dd
            tot = tot - part.astype(F32)
        x = x_ref[...] + b_ref[...]
        e = jnp.exp(-jnp.abs(x))
        dfl = (c1 + acc) * (jnp.where(x >= 0, e, 1.0) / (1.0 + e))
        o_ref[...] = dfl
        rs = jnp.broadcast_to(jnp.sum(dfl, axis=1, keepdims=True), (r, LANES))
        gb = None
        for _ in range(3):
            part = rs.astype(BF16)
            dd = jnp.dot(hr_ref[...], part, preferred_element_type=F32)
            gb = dd if gb is None else gb + dd
            rs = rs - part.astype(F32)
        gb_ref[...] = gb

    return _pcall(
        body, name="fgate_bwd",
        out_shape=[jax.ShapeDtypeStruct((r, LANES), F32), jax.ShapeDtypeStruct((nhp, LANES), F32)],
        compiler_params=pltpu.CompilerParams(vmem_limit_bytes=VMEM_LIMIT),
    )(fl2d, b_rows, df_query, df_key, tri_in_rev, tri_blk_rev, head_rows)


def _norm_mod(x, g, scale, shift):
    r = lax.rsqrt(jnp.mean(x * x, axis=-1, keepdims=True) + EPS)
    return (x * r * g) * (1.0 + scale) + shift


def _norm_mod_bwd(x, dh, g, scale):
    r = lax.rsqrt(jnp.mean(x * x, axis=-1, keepdims=True) + EPS)
    xn = x * r
    dshift = jnp.sum(dh, axis=0, keepdims=True)
    dscale = jnp.sum(dh * (xn * g), axis=0, keepdims=True)
    dxn_g = dh * (1.0 + scale)
    dg = jnp.sum(dxn_g * xn, axis=0, keepdims=True)
    dxn = dxn_g * g
    dx = r * (dxn - xn * jnp.mean(dxn * xn, axis=-1, keepdims=True))
    return dx, dshift, dscale, dg


def _in_proj_fwd(x, mod8, g_attn, w_qkv, w_f, tm):
    t, d = x.shape
    dg = w_qkv.shape[1] // 6

    def body(x_ref, mod_ref, g_ref, w_ref, wf_ref, qkv_ref, fl_ref, h1_ref, h_sc):
        j = pl.program_id(1)

        @pl.when(j == 0)
        def _():
            h = _norm_mod(x_ref[...], g_ref[...], mod_ref[1:2, :], mod_ref[0:1, :]).astype(BF16)
            h_sc[...] = h
            h1_ref[...] = _transposed(h)
            fl_ref[...] = lax.dot_general(wf_ref[...], h, NT_DIMS, preferred_element_type=F32)

        s = jnp.where((j == 0) | (j == 3), HEAD_DIM ** -0.5, 1.0)
        qkv_ref[...] = (jnp.dot(h_sc[...], w_ref[...], preferred_element_type=F32) * s).astype(BF16)

    return _pcall(
        body, name="in_proj_fwd", grid=(t // tm, 6),
        out_shape=[jax.ShapeDtypeStruct((t, 6 * dg), BF16), jax.ShapeDtypeStruct((LANES, t), F32),
                   jax.ShapeDtypeStruct((d, t), BF16)],
        in_specs=[pl.BlockSpec((tm, d), lambda i, j: (i, 0)), pl.BlockSpec((8, d), lambda i, j: (0, 0)),
                  pl.BlockSpec((1, d), lambda i, j: (0, 0)), pl.BlockSpec((d, dg), lambda i, j: (0, j)),
                  pl.BlockSpec((LANES, d), lambda i, j: (0, 0))],
        out_specs=[pl.BlockSpec((tm, dg), lambda i, j: (i, j)), pl.BlockSpec((LANES, tm), lambda i, j: (0, i)),
                   pl.BlockSpec((d, tm), lambda i, j: (0, i))],
        scratch_shapes=[pltpu.VMEM((tm, d), BF16)],
        compiler_params=_params(("arbitrary", "arbitrary")),
    )(x, mod8, g_attn, w_qkv, w_f)


def _head_rstd(o, bd):
    return lax.rsqrt(_split_dot(o * o, bd, 3) * (1.0 / HEAD_DIM) + EPS)


def _attn_out_fwd(x, o_fox, o_sb, g_fox, g_sb, w_out, mod8, bd, tm):
    t, d = x.shape
    dg = o_fox.shape[0]

    def body(x_ref, of_ref, os_ref, gf_ref, gs_ref, w_ref, mod_ref, bd_ref, x2_ref, mix_ref, mixt_ref):
        of, osb = of_ref[...].T, os_ref[...].T
        mf = (of * _head_rstd(of, bd_ref[...]) * gf_ref[...]).astype(BF16)
        ms = (osb * _head_rstd(osb, bd_ref[...]) * gs_ref[...]).astype(BF16)
        mix_ref[:, :dg] = mf
        mix_ref[:, dg:] = ms
        mixt_ref[:dg, :] = _transposed(mf)
        mixt_ref[dg:, :] = _transposed(ms)
        y = jnp.dot(mf, w_ref[:dg, :], preferred_element_type=F32) + jnp.dot(ms, w_ref[dg:, :], preferred_element_type=F32)
        x2_ref[...] = x_ref[...] + mod_ref[2:3, :] * y

    row = lambda w: pl.BlockSpec((tm, w), lambda i: (i, 0))
    full = lambda a: pl.BlockSpec(a.shape, lambda i: (0,) * a.ndim)
    return _pcall(
        body, name="attn_out_fwd", grid=(t // tm,),
        out_shape=[jax.ShapeDtypeStruct((t, d), F32), jax.ShapeDtypeStruct((t, 2 * dg), BF16),
                   jax.ShapeDtypeStruct((2 * dg, t), BF16)],
        in_specs=[row(d), pl.BlockSpec((dg, tm), lambda i: (0, i)), pl.BlockSpec((dg, tm), lambda i: (0, i)),
                  full(g_fox), full(g_sb), full(w_out), full(mod8), full(bd)],
        out_specs=[row(d), row(2 * dg), pl.BlockSpec((2 * dg, tm), lambda i: (0, i))],
        compiler_params=_params(("arbitrary",)),
    )(x, o_fox, o_sb, g_fox, g_sb, w_out, mod8, bd)


def _attn_out_bwd(dx2, mix, o_fox, o_sb, g_fox, g_sb, w_out, mod8, bd, hsel, tm):
    t, d = dx2.shape
    dg = o_fox.shape[0]

    def body(dx_ref, mix_ref, of_ref, os_ref, gf_ref, gs_ref, w_ref, mod_ref, bd_ref, hs_ref,
             dof_ref, dos_ref, dlt_ref, dxg_ref, part_ref):
        dx = dx_ref[...]
        gate = mod_ref[2:3, :]
        dxg = (dx * gate).astype(BF16)
        dxg_ref[...] = dxg
        mixv = mix_ref[...]
        y = jnp.dot(mixv[:, :dg], w_ref[:dg, :], preferred_element_type=F32)
        y = y + jnp.dot(mixv[:, dg:], w_ref[dg:, :], preferred_element_type=F32)
        part_ref[0] = jnp.zeros((8, d), F32)
        part_ref[0, 0:1, :] = jnp.sum(dx * y, axis=0, keepdims=True)
        for grp, (o_ref, g_ref, do_ref) in enumerate(((of_ref, gf_ref, dof_ref), (os_ref, gs_ref, dos_ref))):
            dmix = lax.dot_general(dxg, w_ref[grp * dg:(grp + 1) * dg, :], NT_DIMS, preferred_element_type=F32)
            o = o_ref[...].T
            r = _head_rstd(o, bd_ref[...])
            n = o * r
            part_ref[0, 1:2, grp * dg:(grp + 1) * dg] = jnp.sum(dmix * n, axis=0, keepdims=True)
            dn = dmix * g_ref[...]
            mh = _split_dot(dn * n, bd_ref[...], 3) * (1.0 / HEAD_DIM)
            do = r * (dn - n * mh)
            do_ref[...] = do.astype(BF16)
            if grp == 0:
                prod, dlt = do * o, None
                for _ in range(3):
                    part = prod.astype(BF16)
                    term = lax.dot_general(hs_ref[...], part, NT_DIMS, preferred_element_type=F32)
                    dlt = term if dlt is None else dlt + term
                    prod = prod - part.astype(F32)
                dlt_ref[...] = dlt

    row = lambda w: pl.BlockSpec((tm, w), lambda i: (i, 0))
    full = lambda a: pl.BlockSpec(a.shape, lambda i: (0,) * a.ndim)
    nt = t // tm
    return _pcall(
        body, name="attn_out_bwd", grid=(nt,),
        out_shape=[jax.ShapeDtypeStruct((t, dg), BF16), jax.ShapeDtypeStruct((t, dg), BF16),
                   jax.ShapeDtypeStruct((LANES, t), F32), jax.ShapeDtypeStruct((t, d), BF16),
                   jax.ShapeDtypeStruct((nt, 8, d), F32)],
        in_specs=[row(d), row(2 * dg), pl.BlockSpec((dg, tm), lambda i: (0, i)), pl.BlockSpec((dg, tm), lambda i: (0, i)),
                  full(g_fox), full(g_sb), full(w_out), full(mod8),
                  full(bd), full(hsel)],
        out_specs=[row(dg), row(dg), pl.BlockSpec((LANES, tm), lambda i: (0, i)), row(d), pl.BlockSpec((1, 8, d), lambda i: (i, 0, 0))],
        compiler_params=_params(("arbitrary",)),
    )(dx2, mix, o_fox, o_sb, g_fox, g_sb, w_out, mod8, bd, hsel)


def _in_proj_bwd(dparts, dfl, w_qkv, w_f, x, dx2, mod8, g_attn, tm):
    t, d = x.shape
    dg = dparts[1].shape[1]

    def body(*refs):
        d_refs = refs[:6]
        dfl_ref, w_ref, wf_ref, x_ref, dx2_ref, mod_ref, g_ref, gx_ref, dp_ref, dflb_ref, part_ref = refs[6:]
        dh = None
        for k in range(6):
            dk = d_refs[k][...].T if k in (0, 3) else d_refs[k][...]
            if k in (0, 3):
                dk = dk * HEAD_DIM ** -0.5
            db = dk.astype(BF16)
            dp_ref[:, k * dg:(k + 1) * dg] = db
            term = lax.dot_general(db, w_ref[:, k * dg:(k + 1) * dg], NT_DIMS, preferred_element_type=F32)
            dh = term if dh is None else dh + term
        dfb = dfl_ref[...].T.astype(BF16)
        dflb_ref[...] = dfb
        dh = dh + jnp.dot(dfb, wf_ref[...], preferred_element_type=F32)
        dx, dshift, dscale, dgn = _norm_mod_bwd(x_ref[...], dh, g_ref[...], mod_ref[1:2, :])
        gx_ref[...] = dx2_ref[...] + dx
        part_ref[0] = jnp.zeros((8, d), F32)
        part_ref[0, 0:1, :] = dshift
        part_ref[0, 1:2, :] = dscale
        part_ref[0, 2:3, :] = dgn

    row = lambda w: pl.BlockSpec((tm, w), lambda i: (i, 0))
    full = lambda a: pl.BlockSpec(a.shape, lambda i: (0,) * a.ndim)
    nt = t // tm
    return _pcall(
        body, name="in_proj_bwd", grid=(nt,),
        out_shape=[jax.ShapeDtypeStruct((t, d), F32), jax.ShapeDtypeStruct((t, 6 * dg), BF16),
                   jax.ShapeDtypeStruct((t, LANES), BF16), jax.ShapeDtypeStruct((nt, 8, d), F32)],
        in_specs=[pl.BlockSpec((dg, tm), lambda i: (0, i)), row(dg), row(dg)] * 2
        + [pl.BlockSpec((LANES, tm), lambda i: (0, i)), full(w_qkv), full(w_f), row(d), row(d), full(mod8), full(g_attn)],
        out_specs=[row(d), row(6 * dg), row(LANES), pl.BlockSpec((1, 8, d), lambda i: (i, 0, 0))],
        compiler_params=_params(("arbitrary",)),
    )(*dparts, dfl, w_qkv, w_f, x, dx2, mod8, g_attn)


def _matmul_tn(a_t, b, name):
    m, t = a_t.shape
    n = b.shape[1]
    tm_ = _tile(m, (1408, 1024, 512, 256, 128))
    tn_ = _tile(n, (1408, 1024, 512, 256, 128))
    tk = _tile(t, (1024, 512, 256, 128))
    nk = t // tk

    def body(a_ref, b_ref, o_ref):
        k = pl.program_id(2)

        @pl.when(k == 0)
        def _():
            o_ref[...] = jnp.zeros_like(o_ref)

        o_ref[...] += jnp.dot(a_ref[...], b_ref[...], preferred_element_type=F32)

    return _pcall(
        body, name=name, grid=(m // tm_, n // tn_, nk),
        out_shape=jax.ShapeDtypeStruct((m, n), F32),
        in_specs=[pl.BlockSpec((tm_, tk), lambda i, j, k: (i, k)), pl.BlockSpec((tk, tn_), lambda i, j, k: (k, j))],
        out_specs=pl.BlockSpec((tm_, tn_), lambda i, j, k: (i, j)),
        compiler_params=_params(("arbitrary", "arbitrary", "arbitrary")),
    )(a_t, b)


HALO = 16


def _conv_taps(up_ext, cw, lo, rows):
    s1 = pltpu.roll(up_ext, 1, 0)
    s2 = pltpu.roll(up_ext, 2, 0)
    u = cw[2:3, :] * up_ext[lo:lo + rows] + cw[1:2, :] * s1[lo:lo + rows] + cw[0:1, :] * s2[lo:lo + rows] + cw[3:4, :]
    return u, s1, s2


def _chunk_major(w, cf):
    d, n = w.shape[0], w.shape[1] // cf
    return jnp.transpose(w.reshape(d, n, cf), (1, 0, 2))


def _ffn_fwd(x2, target, mod8, g_mlp, g_final, wg, wv, cwg, cwv, wd, tm, cf):
    t, d = x2.shape
    dfp = wg.shape[1]
    nt, nc = t // tm, dfp // cf
    hb = tm // HALO
    wg_c, wv_c = _chunk_major(wg, cf), _chunk_major(wv, cf)

    def body(x_ref, xp_ref, tg_ref, mod_ref, g_ref, gf_ref, wg_ref, wv_ref, cg_ref, cv_ref, wd_ref,
             dx3_ref, h2_ref, part_ref, act_sc):
        i = pl.program_id(0)
        xe = jnp.concatenate([xp_ref[...], x_ref[...]], axis=0)
        h = _norm_mod(xe, g_ref[...], mod_ref[4:5, :], mod_ref[3:4, :]).astype(BF16)
        h2_ref[...] = _transposed(h[HALO:])
        first = jnp.where(i > 0, h[:HALO], jnp.zeros_like(h[:HALO]))
        h = jnp.concatenate([first, h[HALO:]], axis=0)

        def up(c):
            return (jnp.dot(h, wg_ref[c], preferred_element_type=F32), jnp.dot(h, wv_ref[c], preferred_element_type=F32))

        def activation(c, ups):
            cols = slice(c * cf, (c + 1) * cf)
            ug, _, _ = _conv_taps(ups[0], cg_ref[:, cols], HALO, tm)
            uv, _, _ = _conv_taps(ups[1], cv_ref[:, cols], HALO, tm)
            act_sc[:, cols] = (ug * jax.nn.sigmoid(ug) * uv).astype(BF16)

        for c0 in range(0, nc, 2):
            group = list(range(c0, min(c0 + 2, nc)))
            ups = [up(c) for c in group]
            for c, u in zip(group, ups):
                activation(c, u)

        y_ffn = jnp.dot(act_sc[...], wd_ref[...], preferred_element_type=F32)
        x3 = x_ref[...] + mod_ref[5:6, :] * y_ffn
        r3 = lax.rsqrt(jnp.mean(x3 * x3, axis=-1, keepdims=True) + EPS)
        xn = x3 * r3
        gf = gf_ref[...]
        diff = xn * gf - tg_ref[...]
        dy = diff * (1.0 / d)
        dxn = dy * gf
        dx3 = r3 * (dxn - xn * jnp.mean(dxn * xn, axis=-1, keepdims=True))
        dx3_ref[...] = dx3
        part_ref[0] = jnp.zeros((8, d), F32)
        part_ref[0, 0:1, :] = jnp.sum(dy * xn, axis=0, keepdims=True)
        part_ref[0, 1:2, :] = jnp.sum(dx3 * y_ffn, axis=0, keepdims=True)
        part_ref[0, 2:3, :] = jnp.sum(diff * diff, axis=0, keepdims=True) * (0.5 / d)

    row = lambda w: pl.BlockSpec((tm, w), lambda i: (i, 0))
    full = lambda a: pl.BlockSpec(a.shape, lambda i: (0,) * a.ndim)
    once = lambda a: pl.BlockSpec(a.shape, lambda i: (0,) * a.ndim, pipeline_mode=pl.Buffered(1))
    return _pcall(
        body, name="ffn_fwd", grid=(nt,),
        out_shape=[jax.ShapeDtypeStruct((t, d), F32), jax.ShapeDtypeStruct((d, t), BF16),
                   jax.ShapeDtypeStruct((nt, 8, d), F32)],
        in_specs=[row(d), pl.BlockSpec((HALO, d), lambda i: (jnp.maximum(i * hb - 1, 0), 0)), row(d),
                  full(mod8), full(g_mlp), full(g_final), once(wg_c), once(wv_c), once(cwg), once(cwv), once(wd)],
        out_specs=[row(d), pl.BlockSpec((d, tm), lambda i: (0, i)), pl.BlockSpec((1, 8, d), lambda i: (i, 0, 0))],
        scratch_shapes=[pltpu.VMEM((tm, dfp), BF16)],
        compiler_params=_params(("arbitrary",)),
    )(x2, x2, target, mod8, g_mlp, g_final, wg_c, wv_c, cwg, cwv, wd)


def _ffn_bwd(x2, dx3, mod8, g_mlp, wg, wv, cwg, cwv, wd, tm, cf):
    t, d = x2.shape
    dfp = wg.shape[1]
    nt, nc = t // tm, dfp // cf
    hb = tm // HALO
    nhb = t // HALO
    n = tm + HALO

    def body(x_ref, xp_ref, xn_ref, dx_ref, dxn_ref, mod_ref, g_ref, wg_ref, wv_ref, cg_ref, cv_ref, wd_ref,
             dx2_ref, dug_ref, duv_ref, act_ref, dxg_ref, part_ref, pcg_ref, pcv_ref):
        i = pl.program_id(0)
        xe = jnp.concatenate([xp_ref[...], x_ref[...], xn_ref[...]], axis=0)
        h = _norm_mod(xe, g_ref[...], mod_ref[4:5, :], mod_ref[3:4, :]).astype(BF16)
        h = jnp.concatenate([jnp.where(i > 0, h[:HALO], jnp.zeros_like(h[:HALO])), h[HALO:]], axis=0)
        dx = dx_ref[...] * mod_ref[5:6, :]
        dxn = jnp.where(i < nt - 1, dxn_ref[...] * mod_ref[5:6, :], 0.0)
        de = jnp.concatenate([dx, dxn], axis=0).astype(BF16)
        dxg_ref[...] = de[:tm]
        pcg_ref[0] = jnp.zeros((8, dfp), F32)
        pcv_ref[0] = jnp.zeros((8, dfp), F32)

        def products(c):
            cols = slice(c * cf, (c + 1) * cf)
            return (jnp.dot(h, wg_ref[:, cols], preferred_element_type=F32), jnp.dot(h, wv_ref[:, cols], preferred_element_type=F32),
                    lax.dot_general(de, wd_ref[cols, :], NT_DIMS, preferred_element_type=F32))

        def back(du, cw, up, s1, s2, pc_ref, cols):
            dup = (cw[2:3, :] * du + cw[1:2, :] * pltpu.roll(du, n - 1, 0) + cw[0:1, :] * pltpu.roll(du, n - 2, 0))[:tm]
            dut = du[:tm]
            pc_ref[0, 0:1, cols] = jnp.sum(dut * s2[HALO:HALO + tm], axis=0, keepdims=True)
            pc_ref[0, 1:2, cols] = jnp.sum(dut * s1[HALO:HALO + tm], axis=0, keepdims=True)
            pc_ref[0, 2:3, cols] = jnp.sum(dut * up[HALO:HALO + tm], axis=0, keepdims=True)
            pc_ref[0, 3:4, cols] = jnp.sum(dut, axis=0, keepdims=True)
            return dup.astype(BF16)

        def chunk(c, prods):
            cols = slice(c * cf, (c + 1) * cf)
            upg, upv, dact = prods
            cg, cv = cg_ref[:, cols], cv_ref[:, cols]
            ug, g1, g2 = _conv_taps(upg, cg, HALO, n)
            uv, v1, v2 = _conv_taps(upv, cv, HALO, n)
            sg = jax.nn.sigmoid(ug)
            sil = ug * sg
            act_ref[cols, :] = _transposed((sil * uv)[:tm].astype(BF16))
            dug_ref[:, cols] = back(dact * uv * (sg * (1.0 + ug * (1.0 - sg))), cg, upg, g1, g2, pcg_ref, cols)
            duv_ref[:, cols] = back(dact * sil, cv, upv, v1, v2, pcv_ref, cols)

        for c0 in range(0, nc, 2):
            group = list(range(c0, min(c0 + 2, nc)))
            prods = [products(c) for c in group]
            for c, pr in zip(group, prods):
                chunk(c, pr)

        dh = (lax.dot_general(dug_ref[...], wg_ref[...], NT_DIMS, preferred_element_type=F32)
              + lax.dot_general(duv_ref[...], wv_ref[...], NT_DIMS, preferred_element_type=F32))
        dxt, dshift, dscale, dgn = _norm_mod_bwd(x_ref[...], dh, g_ref[...], mod_ref[4:5, :])
        dx2_ref[...] = dx_ref[...] + dxt
        part_ref[0] = jnp.zeros((8, d), F32)
        part_ref[0, 0:1, :] = dshift
        part_ref[0, 1:2, :] = dscale
        part_ref[0, 2:3, :] = dgn

    row = lambda w: pl.BlockSpec((tm, w), lambda i: (i, 0))
    prev = pl.BlockSpec((HALO, d), lambda i: (jnp.maximum(i * hb - 1, 0), 0))
    nxt = pl.BlockSpec((HALO, d), lambda i: (jnp.minimum((i + 1) * hb, nhb - 1), 0))
    full = lambda a: pl.BlockSpec(a.shape, lambda i: (0,) * a.ndim)
    once = lambda a: pl.BlockSpec(a.shape, lambda i: (0,) * a.ndim, pipeline_mode=pl.Buffered(1))
    part = lambda w: pl.BlockSpec((1, 8, w), lambda i: (i, 0, 0))
    return _pcall(
        body, name="ffn_bwd", grid=(nt,),
        out_shape=[jax.ShapeDtypeStruct((t, d), F32), jax.ShapeDtypeStruct((t, dfp), BF16),
                   jax.ShapeDtypeStruct((t, dfp), BF16), jax.ShapeDtypeStruct((dfp, t), BF16),
                   jax.ShapeDtypeStruct((t, d), BF16), jax.ShapeDtypeStruct((nt, 8, d), F32),
                   jax.ShapeDtypeStruct((nt, 8, dfp), F32), jax.ShapeDtypeStruct((nt, 8, dfp), F32)],
        in_specs=[row(d), prev, nxt, row(d), nxt, full(mod8), full(g_mlp), once(wg), once(wv), once(cwg), once(cwv), once(wd)],
        out_specs=[row(d), row(dfp), row(dfp), pl.BlockSpec((dfp, tm), lambda i: (0, i)), row(d), part(d), part(dfp), part(dfp)],
        compiler_params=_params(("arbitrary",)),
    )(x2, x2, x2, dx3, dx3, mod8, g_mlp, wg, wv, cwg, cwv, wd)


BLK = 2 * LANES
XROWS = 144
LANE_FS, LANE_FT_A, LANE_FT_B = 0, 3, 6


def _head_masks():
    lane = lax.broadcasted_iota(jnp.int32, (1, LANES), 1)
    in_a = lane < HEAD_DIM
    return in_a, jnp.logical_not(in_a)


def _pieces3(x):
    hi = x.astype(BF16).astype(F32)
    r = x - hi
    mid = r.astype(BF16).astype(F32)
    return hi, mid, (r - mid).astype(BF16).astype(F32)


def _bias_lanes(rows, entries):
    sub = lax.broadcasted_iota(jnp.int32, (16, 1), 0)
    out = jnp.zeros((16, rows), F32)
    for l, v in entries:
        out = jnp.where(sub == l, v, out)
    return jnp.concatenate([out, jnp.zeros((LANES - 16, rows), F32)], axis=0).T


def _three(first, values):
    return [(first + k, v) for k, v in enumerate(values)]


def _stack_rows(x, in_a, in_b):
    zero = jnp.zeros_like(x)
    return jnp.concatenate([jnp.where(in_a, x, zero), jnp.where(in_b, x, zero)], axis=0)


def _transposed(x):
    return x.astype(F32).T.astype(BF16)


def _attn_operands(qkv, frow, dg):
    t = qkv.shape[0]
    p, nk = dg // LANES, t // BLK

    def body(qf_ref, kf_ref, vf_ref, ks_ref, vs_ref, f_ref, qx_ref, kx_ref, kxt_ref, vf_o, vft_o, ks_o, kst_o, vs_o, vst_o):
        in_a, in_b = _head_masks()
        fa, fb = _pieces3(f_ref[0]), _pieces3(f_ref[1])
        qx_ref[0, :, :LANES] = qf_ref[...]
        qx_ref[0, :, LANES:] = _bias_lanes(
            BLK, _three(LANE_FS, (-1.0,) * 3) + _three(LANE_FT_A, fa) + _three(LANE_FT_B, fb)).astype(BF16)
        kf = kf_ref[...]
        zero = jnp.zeros_like(kf)
        top = jnp.concatenate([jnp.where(in_a, kf, zero), _bias_lanes(
            BLK, _three(LANE_FS, fa) + _three(LANE_FT_A, (1.0,) * 3)).astype(BF16)], axis=1)
        bot = jnp.concatenate([jnp.where(in_b, kf, zero), _bias_lanes(
            BLK, _three(LANE_FS, fb) + _three(LANE_FT_B, (1.0,) * 3)).astype(BF16)], axis=1)
        kx = jnp.concatenate([top, bot], axis=0)
        kx_ref[0, 0] = kx
        kxt_ref[0, 0] = _transposed(kx)[:XROWS]
        for src, dst, dst_t in ((vf_ref, vf_o, vft_o), (ks_ref, ks_o, kst_o), (vs_ref, vs_o, vst_o)):
            st = _stack_rows(src[...], in_a, in_b)
            dst[0, 0] = st
            dst_t[0, 0] = _transposed(st)

    col = lambda base: pl.BlockSpec((BLK, LANES), lambda h, j: (j, base * p + h))
    blk4 = lambda r, c: pl.BlockSpec((1, 1, r, c), lambda h, j: (h, j, 0, 0))
    shp4 = lambda r, c: jax.ShapeDtypeStruct((p, nk, r, c), BF16)
    return _pcall(
        body, name="attn_operands", grid=(p, nk),
        out_shape=[jax.ShapeDtypeStruct((p, t, 2 * LANES), BF16), shp4(2 * BLK, 2 * LANES), shp4(XROWS, 2 * BLK)]
        + [shp4(2 * BLK, LANES), shp4(LANES, 2 * BLK)] * 3,
        in_specs=[col(0), col(1), col(2), col(4), col(5), pl.BlockSpec((2, 1, BLK), lambda h, j: (h, 0, j))],
        out_specs=[pl.BlockSpec((1, BLK, 2 * LANES), lambda h, j: (h, j, 0)), blk4(2 * BLK, 2 * LANES), blk4(XROWS, 2 * BLK)]
        + [blk4(2 * BLK, LANES), blk4(LANES, 2 * BLK)] * 3,
        compiler_params=_params(("arbitrary", "arbitrary")),
    )(qkv, qkv, qkv, qkv, qkv, frow)


def _key_query_masks():
    key = lax.broadcasted_iota(jnp.int32, (BLK, BLK), 0)
    qry = lax.broadcasted_iota(jnp.int32, (BLK, BLK), 1)
    return key <= qry, key < qry


def _key_triangle(kind):
    s = lax.broadcasted_iota(jnp.int32, (BLK, BLK), 0)
    j = lax.broadcasted_iota(jnp.int32, (BLK, BLK), 1)
    return {"suffix": j >= s, "prefix": j <= s, "before": j < s}[kind].astype(BF16)


def _tri_dot(tri, x, passes):
    acc = None
    for _ in range(passes):
        part = x.astype(BF16)
        d = jnp.dot(tri, part, preferred_element_type=F32)
        acc = d if acc is None else acc + d
        x = x - part.astype(F32)
    return acc


GROUPS = (4, 2, 1)


def _loop_blocks(n, tiles, carry, descending=False, groups=GROUPS):
    at = (lambda k: n - 1 - k) if descending else (lambda k: k)
    done = 0
    for g in groups:
        left = n - done
        carry = lax.fori_loop(0, left // g, lambda h, c, g=g, done=done: tiles([at(done + g * h + k) for k in range(g)], c), carry)
        done = done + (left // g) * g
    return carry


def _resident(shape):
    return pl.BlockSpec((1,) + shape, lambda h, i: (h,) + (0,) * len(shape), pipeline_mode=pl.Buffered(1))


def _rows_per_head(a, b):
    return jnp.concatenate([jnp.broadcast_to(a, (HEAD_DIM, BLK)), jnp.broadcast_to(b, (HEAD_DIM, BLK))], axis=0)


def _fold_heads(stacked, in_a):
    return jnp.where(in_a, stacked[:BLK], stacked[BLK:])


def _xy_gather_copies(ins, outs, send_sems, recv_sems, local_sems):
    x, y, c = lax.axis_index("x"), lax.axis_index("y"), lax.axis_index("c")
    chips = [(1 - x, y), (x, 1 - y), (1 - x, 1 - y)]
    mine = 2 * x + y
    local, remote = [], []
    for w in range(len(ins)):
        local.append(pltpu.make_async_copy(ins[w], outs[w].at[mine], local_sems.at[w]))
        for k, (px, py) in enumerate(chips):
            remote.append(pltpu.make_async_remote_copy(
                src_ref=ins[w], dst_ref=outs[w].at[mine], send_sem=send_sems.at[3 * w + k],
                recv_sem=recv_sems.at[3 * w + k], device_id=(px, py, c), device_id_type=MESH))
    return local, remote


def _fox_fwd(qx, kx, v_t, dg, shards):
    p, t = qx.shape[0], qx.shape[1]
    nq = t // BLK
    nh = 2 * p
    ns = len(shards)

    def body(q_ref, k_ref, vt_ref, *rest):
        shard_refs, (o_ref, lse_ref), gathered = rest[:ns], rest[ns:ns + 2], rest[ns + 2:2 * ns + 2]
        local, remote = _xy_gather_copies(shard_refs, gathered, *rest[2 * ns + 2:])
        i = pl.program_id(1)

        @pl.when((pl.program_id(0) == 0) & (i == 0))
        def _():
            for cp in local + remote:
                cp.start()

        causal, _ = _key_query_masks()
        q = q_ref[0]

        def scores(j, masked):
            s2 = lax.dot_general(k_ref[0, j], q, NT_DIMS, preferred_element_type=F32)
            s = [s2[a * BLK:(a + 1) * BLK] for a in range(2)]
            return [jnp.where(causal, x, NEG) for x in s] if masked else s

        def update(blocks, carry):
            m, l, acc = list(carry[0]), list(carry[1]), carry[2]
            for j, s in blocks:
                alpha, pr = [], []
                for a in range(2):
                    mn = jnp.maximum(m[a], jnp.max(s[a], axis=0, keepdims=True))
                    pa = jnp.exp(s[a] - mn)
                    al = jnp.exp(m[a] - mn)
                    l[a] = al * l[a] + jnp.sum(pa, axis=0, keepdims=True)
                    m[a] = mn
                    alpha.append(al)
                    pr.append(pa.astype(BF16))
                acc = _rows_per_head(*alpha) * acc + jnp.dot(vt_ref[0, j], jnp.concatenate(pr, axis=0), preferred_element_type=F32)
            return tuple(m), tuple(l), acc

        tiles = lambda js, c: update([(j, scores(j, False)) for j in js], c)
        neg, zero = jnp.full((1, BLK), NEG, F32), jnp.zeros((1, BLK), F32)
        carry = _loop_blocks(i, tiles, ((neg, neg), (zero, zero), jnp.zeros((LANES, BLK), F32)), groups=(8, 4, 2, 1))
        m, l, acc = update([(i, scores(i, True))], carry)
        o_ref[...] = acc / _rows_per_head(*l)
        lse_ref[0] = m[0] + jnp.log(l[0])
        lse_ref[1] = m[1] + jnp.log(l[1])

        @pl.when((pl.program_id(0) == p - 1) & (i == nq - 1))
        def _():
            for cp in remote:
                cp.wait_recv()
            for cp in remote:
                cp.wait_send()
            for cp in local:
                cp.wait()

    row = pl.BlockSpec((2, 1, BLK), lambda h, i: (h, 0, i))
    hbm = pl.BlockSpec(memory_space=pltpu.HBM)
    return _pcall(
        body, name="fox_fwd", grid=(p, nq),
        out_shape=[jax.ShapeDtypeStruct((dg, t), F32), jax.ShapeDtypeStruct((nh, 1, t), F32)]
        + [jax.ShapeDtypeStruct((4,) + s.shape, s.dtype) for s in shards],
        in_specs=[pl.BlockSpec((1, BLK, 2 * LANES), lambda h, i: (h, i, 0)), _resident((nq, 2 * BLK, 2 * LANES)),
                  _resident((nq, LANES, 2 * BLK))] + [hbm] * ns,
        out_specs=[pl.BlockSpec((LANES, BLK), lambda h, i: (h, i)), row] + [hbm] * ns,
        scratch_shapes=[pltpu.SemaphoreType.DMA((3 * ns,)), pltpu.SemaphoreType.DMA((3 * ns,)), pltpu.SemaphoreType.DMA((ns,))],
        compiler_params=_params(("arbitrary", "arbitrary")),
    )(qx, kx, v_t, *shards)


def _swap_copies(ins, outs, send_sems, recv_sems, local_sems):
    x, y, c = lax.axis_index("x"), lax.axis_index("y"), lax.axis_index("c")
    local, remote = [], []
    for w in range(len(ins)):
        local.append(pltpu.make_async_copy(ins[w], outs[w].at[c], local_sems.at[w]))
        remote.append(pltpu.make_async_remote_copy(
            src_ref=ins[w], dst_ref=outs[w].at[c], send_sem=send_sems.at[w], recv_sem=recv_sems.at[w],
            device_id=(x, y, 1 - c), device_id_type=MESH))
    return local, remote


def _fox_bwd(qx, kx, kx_t, v_st, do, delta, lse, dg, halves):
    p, t = qx.shape[0], qx.shape[1]
    nq = t // BLK
    nh = 2 * p
    ns = len(halves)

    def body(q_ref, k_ref, kt_ref, v_ref, do_ref, dl_ref, lse_ref, *rest):
        half_refs, (dq_ref, dft_ref, dk_ref, dv_ref, dkx_ref), both_refs = rest[:ns], rest[ns:ns + 5], rest[ns + 5:2 * ns + 5]
        local, remote = _swap_copies(half_refs, both_refs, *rest[2 * ns + 5:])
        i = pl.program_id(1)

        @pl.when((pl.program_id(0) == 0) & (i == 0))
        def _():
            for cp in local + remote:
                cp.start()

        @pl.when(i == 0)
        def _():
            dk_ref[...] = jnp.zeros_like(dk_ref)
            dv_ref[...] = jnp.zeros_like(dv_ref)
            dkx_ref[...] = jnp.zeros_like(dkx_ref)

        in_a, _ = _head_masks()
        first_lane = lax.broadcasted_iota(jnp.int32, (1, LANES), 1) == 0
        causal, _ = _key_query_masks()
        q, do2 = q_ref[0], do_ref[...]
        dl = (dl_ref[0], dl_ref[1])
        lse = (lse_ref[0], lse_ref[1])

        def products(j):
            return (lax.dot_general(k_ref[0, j], q, NT_DIMS, preferred_element_type=F32),
                    lax.dot_general(v_ref[0, j], do2, NT_DIMS, preferred_element_type=F32))

        def dscores(prod, masked):
            s2, dp2 = prod
            pr, ds = [], []
            for a in range(2):
                s = s2[a * BLK:(a + 1) * BLK]
                if masked:
                    s = jnp.where(causal, s, NEG)
                pa = jnp.exp(s - lse[a])
                ds.append((pa * (dp2[a * BLK:(a + 1) * BLK] - dl[a])).astype(BF16))
                pr.append(pa.astype(BF16))
            return jnp.concatenate(ds, axis=0), jnp.concatenate(pr, axis=0)

        def accumulate(j, dsb, prb, dq):
            off = pl.multiple_of(j * BLK, BLK)
            dk_full = jnp.dot(dsb, q, preferred_element_type=F32)
            dk_ref[pl.ds(off, BLK), :] += _fold_heads(dk_full[:, :LANES], in_a)
            dkx_ref[pl.ds(off, BLK), :] += jnp.where(first_lane, dk_full[:BLK, LANES:], dk_full[BLK:, LANES:])
            dv_ref[pl.ds(off, BLK), :] += _fold_heads(jnp.dot(prb, do2, preferred_element_type=F32), in_a)
            return dq + jnp.dot(kt_ref[0, j], dsb, preferred_element_type=F32)

        def tiles(js, dq, masked=False):
            prods = [products(j) for j in js]
            grads = [dscores(pr, masked) for pr in prods]
            for j, (dsb, prb) in zip(js, grads):
                dq = accumulate(j, dsb, prb, dq)
            return dq

        dq = _loop_blocks(i, tiles, jnp.zeros((XROWS, BLK), F32))
        dq = tiles([i], dq, True)
        dq_ref[...] = dq[:LANES]
        dft_ref[0] = dq[LANES + LANE_FT_A:LANES + LANE_FT_A + 1]
        dft_ref[1] = dq[LANES + LANE_FT_B:LANES + LANE_FT_B + 1]

        @pl.when((pl.program_id(0) == p - 1) & (i == nq - 1))
        def _():
            for cp in remote:
                cp.wait_recv()
            for cp in remote:
                cp.wait_send()
            for cp in local:
                cp.wait()

    row = pl.BlockSpec((2, 1, BLK), lambda h, i: (h, 0, i))
    acc = pl.BlockSpec((t, LANES), lambda h, i: (0, h))
    hbm = pl.BlockSpec(memory_space=pltpu.HBM)
    return _pcall(
        body, name="fox_bwd", grid=(p, nq),
        out_shape=[jax.ShapeDtypeStruct((dg, t), F32), jax.ShapeDtypeStruct((nh, 1, t), F32)] + [jax.ShapeDtypeStruct((t, dg), F32)] * 3
        + [jax.ShapeDtypeStruct((2,) + h.shape, h.dtype) for h in halves],
        in_specs=[pl.BlockSpec((1, BLK, 2 * LANES), lambda h, i: (h, i, 0)), _resident((nq, 2 * BLK, 2 * LANES)),
                  _resident((nq, XROWS, 2 * BLK)), _resident((nq, 2 * BLK, LANES)),
                  pl.BlockSpec((BLK, LANES), lambda h, i: (i, h)), row, row] + [hbm] * ns,
        out_specs=[pl.BlockSpec((LANES, BLK), lambda h, i: (h, i)), row, acc, acc, acc] + [hbm] * ns,
        scratch_shapes=[pltpu.SemaphoreType.DMA((ns,)), pltpu.SemaphoreType.DMA((ns,)), pltpu.SemaphoreType.DMA((ns,))],
        compiler_params=_params(("arbitrary", "arbitrary")),
    )(qx, kx, kx_t, v_st, do, delta, lse, *halves)


def _softplus_of(z):
    return jnp.maximum(z, 0.0) + jnp.log(1.0 + jnp.exp(-jnp.abs(z)))


def _sb_fwd(qkv, k_st, v_t, dg):
    t = qkv.shape[0]
    p, nq = dg // LANES, t // BLK
    nh = 2 * p

    def body(q_ref, k_ref, vt_ref, o_ref, rt_ref):
        i = pl.program_id(1)
        _, strict = _key_query_masks()
        suffix = _key_triangle("suffix")
        q = q_ref[...]

        def scores(j):
            z2 = lax.dot_general(k_ref[0, j], q, NT_DIMS, preferred_element_type=F32)
            return [z2[a * BLK:(a + 1) * BLK] for a in range(2)]

        def suffix_sums(z, masked):
            out = []
            for a in range(2):
                sp = _softplus_of(z[a])
                if masked:
                    sp = jnp.where(strict, sp, 0.0)
                out.append(_tri_dot(suffix, sp, 2))
            return out

        def weights(z, cs, rest, masked):
            w, rest_new = [], []
            for a in range(2):
                wa = jnp.exp(z[a] - cs[a] - rest[a])
                if masked:
                    wa = jnp.where(strict, wa, 0.0)
                w.append(wa.astype(BF16))
                rest_new.append(rest[a] + cs[a][0:1])
            return jnp.concatenate(w, axis=0), tuple(rest_new)

        def tiles(js, carry, masked=False):
            rest, acc = carry
            zs = [scores(j) for j in js]
            css = [suffix_sums(z, masked) for z in zs]
            ws = []
            for z, cs in zip(zs, css):
                w2, rest = weights(z, cs, rest, masked)
                ws.append(w2)
            for j, w2 in zip(js, ws):
                acc = acc + jnp.dot(vt_ref[0, j], w2, preferred_element_type=F32)
            return rest, acc

        zero = jnp.zeros((1, BLK), F32)
        carry = tiles([i], ((zero, zero), jnp.zeros((LANES, BLK), F32)), True)
        rest, acc = _loop_blocks(i, tiles, carry, descending=True, groups=(8, 4, 2, 1))
        o_ref[...] = acc
        rt_ref[0] = rest[0]
        rt_ref[1] = rest[1]

    return _pcall(
        body, name="sb_fwd", grid=(p, nq),
        out_shape=[jax.ShapeDtypeStruct((dg, t), F32), jax.ShapeDtypeStruct((nh, 1, t), F32)],
        in_specs=[pl.BlockSpec((BLK, LANES), lambda h, i: (i, 3 * p + h)), _resident((nq, 2 * BLK, LANES)),
                  _resident((nq, LANES, 2 * BLK))],
        out_specs=[pl.BlockSpec((LANES, BLK), lambda h, i: (h, i)), pl.BlockSpec((2, 1, BLK), lambda h, i: (h, 0, i))],
        compiler_params=_params(("arbitrary", "arbitrary")),
    )(qkv, k_st, v_t)


def _scatter8_copies(ins, outs, send_sems, recv_sems, local_sems):
    x, y, c = lax.axis_index("x"), lax.axis_index("y"), lax.axis_index("c")
    me = 4 * x + 2 * y + c
    local, remote = [], []
    for w in range(len(ins)):
        local.append(pltpu.make_async_copy(ins[w].at[me], outs[w].at[me], local_sems.at[w]))
        for f in range(1, 8):
            px = 1 - x if f & 4 else x
            py = 1 - y if f & 2 else y
            pc = 1 - c if f & 1 else c
            remote.append(pltpu.make_async_remote_copy(
                src_ref=ins[w].at[4 * px + 2 * py + pc], dst_ref=outs[w].at[me],
                send_sem=send_sems.at[7 * w + f - 1], recv_sem=recv_sems.at[7 * w + f - 1],
                device_id=(px, py, pc), device_id_type=MESH))
    return local, remote


def _sb_bwd(qkv, k_st, k_t, v_st, do, rtot, dg, pieces):
    t = qkv.shape[0]
    p, nq = dg // LANES, t // BLK
    ns = len(pieces)

    def body(q_ref, k_ref, kt_ref, v_ref, do_ref, rt_ref, *rest):
        piece_refs, (dq_ref, dk_ref, dv_ref), recv_refs = rest[:ns], rest[ns:ns + 3], rest[ns + 3:2 * ns + 3]
        local, remote = _scatter8_copies(piece_refs, recv_refs, *rest[2 * ns + 3:])
        i = pl.program_id(1)

        @pl.when((pl.program_id(0) == 0) & (i == 0))
        def _():
            for cp in local + remote:
                cp.start()

        @pl.when(i == 0)
        def _():
            dk_ref[...] = jnp.zeros_like(dk_ref)
            dv_ref[...] = jnp.zeros_like(dv_ref)

        in_a, _ = _head_masks()
        _, strict = _key_query_masks()
        before_m, prefix_m = _key_triangle("before"), _key_triangle("prefix")
        q, do2 = q_ref[...], do_ref[...]
        rt = (rt_ref[0], rt_ref[1])

        def products(j):
            z2 = lax.dot_general(k_ref[0, j], q, NT_DIMS, preferred_element_type=F32)
            da2 = lax.dot_general(v_ref[0, j], do2, NT_DIMS, preferred_element_type=F32)
            return [z2[a * BLK:(a + 1) * BLK] for a in range(2)], [da2[a * BLK:(a + 1) * BLK] for a in range(2)]

        def softplus_sums(z, masked):
            sp = [_softplus_of(x) for x in z]
            if masked:
                sp = [jnp.where(strict, x, 0.0) for x in sp]
            return sp, [_tri_dot(before_m, x, 2) for x in sp]

        def weight_grads(z, da, sp, pre, before, masked):
            w, g, pg, before_new = [], [], [], []
            for a in range(2):
                wa = jnp.exp(z[a] + (before[a] - rt[a]) + pre[a])
                if masked:
                    wa = jnp.where(strict, wa, 0.0)
                ga = wa * da[a]
                w.append(wa.astype(BF16))
                g.append(ga)
                pg.append(jnp.dot(prefix_m, ga.astype(BF16), preferred_element_type=F32))
                before_new.append(before[a] + pre[a][BLK - 1:BLK] + sp[a][BLK - 1:BLK])
            return jnp.concatenate(w, axis=0), g, pg, tuple(before_new)

        def dlogits(sp, g, pg, gbefore, masked):
            dz, gbefore_new = [], []
            for a in range(2):
                s_incl = gbefore[a] + pg[a]
                dza = (g[a] - s_incl) + jnp.exp(-sp[a]) * s_incl
                if masked:
                    dza = jnp.where(strict, dza, 0.0)
                dz.append(dza.astype(BF16))
                gbefore_new.append(s_incl[BLK - 1:BLK])
            return jnp.concatenate(dz, axis=0), tuple(gbefore_new)

        def accumulate(j, dzb, wb, dq):
            off = pl.multiple_of(j * BLK, BLK)
            dk_ref[pl.ds(off, BLK), :] += _fold_heads(jnp.dot(dzb, q, preferred_element_type=F32), in_a)
            dv_ref[pl.ds(off, BLK), :] += _fold_heads(jnp.dot(wb, do2, preferred_element_type=F32), in_a)
            return dq + jnp.dot(kt_ref[0, j], dzb, preferred_element_type=F32)

        def tiles(js, carry, masked=False):
            before, gbefore, dq = carry
            prods = [products(j) for j in js]
            sums = [softplus_sums(z, masked) for z, _ in prods]
            grads = []
            for (z, da), (sp, pre) in zip(prods, sums):
                wb, g, pg, before = weight_grads(z, da, sp, pre, before, masked)
                grads.append((wb, g, pg))
            for j, (sp, _), (wb, g, pg) in zip(js, sums, grads):
                dzb, gbefore = dlogits(sp, g, pg, gbefore, masked)
                dq = accumulate(j, dzb, wb, dq)
            return before, gbefore, dq

        zero = jnp.zeros((1, BLK), F32)
        carry = _loop_blocks(i, tiles, ((zero, zero), (zero, zero), jnp.zeros((LANES, BLK), F32)), groups=(2, 1))
        dq_ref[...] = tiles([i], carry, True)[2]

        @pl.when((pl.program_id(0) == p - 1) & (i == nq - 1))
        def _():
            for cp in remote:
                cp.wait_recv()
            for cp in remote:
                cp.wait_send()
            for cp in local:
                cp.wait()

    acc = pl.BlockSpec((t, LANES), lambda h, i: (0, h))
    hbm = pl.BlockSpec(memory_space=pltpu.HBM)
    return _pcall(
        body, name="sb_bwd", grid=(p, nq),
        out_shape=[jax.ShapeDtypeStruct((dg, t), F32)] + [jax.ShapeDtypeStruct((t, dg), F32)] * 2
        + [jax.ShapeDtypeStruct(pc.shape, pc.dtype) for pc in pieces],
        in_specs=[pl.BlockSpec((BLK, LANES), lambda h, i: (i, 3 * p + h)), _resident((nq, 2 * BLK, LANES)),
                  _resident((nq, LANES, 2 * BLK)), _resident((nq, 2 * BLK, LANES)),
                  pl.BlockSpec((BLK, LANES), lambda h, i: (i, h)), pl.BlockSpec((2, 1, BLK), lambda h, i: (h, 0, i))] + [hbm] * ns,
        out_specs=[pl.BlockSpec((LANES, BLK), lambda h, i: (h, i)), acc, acc] + [hbm] * ns,
        scratch_shapes=[pltpu.SemaphoreType.DMA((7 * ns,)), pltpu.SemaphoreType.DMA((7 * ns,)), pltpu.SemaphoreType.DMA((ns,))],
        compiler_params=_params(("arbitrary", "arbitrary")),
    )(qkv, k_st, k_t, v_st, do, rtot, *pieces)


def _tri_constants(nh, t):
    nb = t // LANES
    r = nh * nb
    li = np.arange(LANES)
    tri_in = (li[:, None] <= li[None, :])
    ri = np.arange(r)
    same = (ri[:, None] // nb) == (ri[None, :] // nb)
    blk = same & (ri[None, :] < ri[:, None])
    blk_rev = same & (ri[None, :] > ri[:, None])
    head_rows = (np.arange(max(8, nh))[:, None] == (ri[None, :] // nb))
    as_bf16 = lambda a: jnp.asarray(a.astype(np.float32), BF16)
    return as_bf16(tri_in), as_bf16(blk), as_bf16(tri_in.T), as_bf16(blk_rev), as_bf16(head_rows)


def kernel(x, c, w_ada, b_ada, g_attn, w_in, b_fgate, g_out_fox, g_out_sb, w_out, g_mlp, w_up, conv_w, conv_b, w_down, g_final, loss_target, m_w_ada, m_b_ada, m_g_attn, m_w_in, m_b_fgate, m_g_out_fox, m_g_out_sb, m_w_out, m_g_mlp, m_w_up, m_conv_w, m_conv_b, m_w_down, m_g_final, v_w_ada, v_b_ada, v_g_attn, v_w_in, v_b_fgate, v_g_out_fox, v_g_out_sb, v_w_out, v_g_mlp, v_w_up, v_conv_w, v_conv_b, v_w_down, v_g_final):
    t, d = x.shape[1], x.shape[2]
    dg = d // 2
    nh = dg // HEAD_DIM
    n_in = 6 * dg + nh
    dff = w_down.shape[1] * 4
    dfp = -(-dff // 256) * 256
    cf = 256
    tm = _tile(t, (512, 256, 128))
    nq = t // BLK
    xi, yi, ci = lax.axis_index("x"), lax.axis_index("y"), lax.axis_index("c")
    shard = 2 * xi + yi
    me = 4 * xi + 2 * yi + ci

    x2d, tg2d = x[0], loss_target[0]

    c_all = _all_gather8(jnp.pad(c, ((0, 7), (0, 0)))).reshape(8, 8, d)[:, 0, :]
    ada_cols = w_ada.shape[2]
    b_shard = lax.dynamic_slice(b_ada, (0, shard * ada_cols), (1, ada_cols))
    sc_all, mod_shard = _ada_fwd(c_all, w_ada[0], b_shard)
    mod_all = _all_gather8(mod_shard).reshape(4, 2, 8, ada_cols)
    mod_me = lax.dynamic_index_in_dim(mod_all[:, 0], me, axis=1, keepdims=False)
    mod8 = jnp.pad(mod_me.reshape(6, d), ((0, 2), (0, 0)))

    lane_pad = lambda a: jnp.pad(a, ((0, 0),) * (a.ndim - 1) + ((0, -a.shape[-1] % LANES),))
    (g_in,) = _gather_xy([lane_pad(w_in[0].astype(BF16))])
    later_shards = [w_out[0].astype(BF16), lane_pad(w_up[0].astype(BF16)), w_down[0].astype(BF16), lane_pad(conv_w[0])]
    w_in_full = jnp.transpose(g_in[:, :, :n_in // 4], (1, 0, 2)).reshape(d, n_in)
    w_qkv = w_in_full[:, :6 * dg]
    w_f = jnp.pad(w_in_full[:, 6 * dg:].T, ((0, LANES - nh), (0, 0)))

    qkv, fl, h1 = _in_proj_fwd(x2d, mod8, g_attn, w_qkv, w_f, tm)
    tri_in, tri_blk, tri_in_rev, tri_blk_rev, head_rows = _tri_constants(nh, t)
    fl2d = fl[:nh].reshape(nh * t // LANES, LANES)
    b_rows = jnp.repeat(b_fgate[0], t // LANES)[:, None]
    f2d = _fgate_fwd(fl2d, b_rows, tri_in, tri_blk)
    frow = f2d.reshape(nh, 1, t)
    pairs = nh // 2
    qx, kx, kx_t, vf_st, vf_t, ks_st, ks_t, vs_st, vs_t = _attn_operands(qkv, frow, dg)
    o_fox, lse, g_out, g_up, g_down, g_cw = _fox_fwd(qx, kx, vf_t, dg, later_shards)
    g_up, g_cw = g_up[:, :, :dff // 2], g_cw[:, :, :dff // 2]
    w_out_full = g_out.reshape(2 * dg, d)
    w_up_full = jnp.transpose(g_up, (1, 0, 2)).reshape(d, 2 * dff)
    padc = ((0, 0), (0, dfp - dff))
    wg, wv = jnp.pad(w_up_full[:, :dff], padc), jnp.pad(w_up_full[:, dff:], padc)
    wd = jnp.pad(g_down.reshape(dff, d), ((0, dfp - dff), (0, 0)))
    cw_full = jnp.transpose(g_cw, (1, 0, 2)).reshape(3, 2 * dff)
    cw4 = jnp.concatenate([cw_full, conv_b], axis=0)
    cwg = jnp.pad(cw4[:, :dff], ((0, 4), (0, dfp - dff)))
    cwv = jnp.pad(cw4[:, dff:], ((0, 4), (0, dfp - dff)))
    o_sb, rtot = _sb_fwd(qkv, ks_st, vs_t, dg)
    li = np.arange(dg)
    bd = jnp.asarray((li[:, None] // HEAD_DIM == li[None, :] // HEAD_DIM).astype(np.float32), BF16)
    hsel = jnp.asarray((np.arange(LANES)[:, None] == li[None, :] // HEAD_DIM).astype(np.float32), BF16)
    x2, mix, mix_t = _attn_out_fwd(x2d, o_fox, o_sb, g_out_fox, g_out_sb, w_out_full, mod8, bd, tm)
    g_final2 = g_final[None, :]
    dx3, h2, part_f = _ffn_fwd(x2, tg2d, mod8, g_mlp, g_final2, wg, wv, cwg, cwv, wd, tm, cf)

    tm_ffn_bwd = _tile(t, (256, 128))
    dx2, dupg, dupv, act, dxg3, part_b, pcg, pcv = _ffn_bwd(x2, dx3, mod8, g_mlp, wg, wv, cwg, cwv, wd, tm_ffn_bwd, cf)
    do_fox, do_sb, delta, dxg2, part_o = _attn_out_bwd(dx2, mix, o_fox, o_sb, g_out_fox, g_out_sb, w_out_full, mod8, bd, hsel, tm)
    drow = delta[:nh].reshape(nh, 1, t)

    def col_pieces(g):
        r, cc = g.shape
        return jnp.transpose(g.reshape(2, r // 2, 4, cc // 4), (2, 0, 1, 3)).reshape(8, r // 2, cc // 4)

    def row_pieces(g):
        r, cc = g.shape
        return g.reshape(8, r // 8, cc)

    gw_out = _matmul_tn(mix_t, dxg2, "grad_w_out")
    gw_upg = _matmul_tn(h2, dupg, "grad_w_up_gate")
    gw_upv = _matmul_tn(h2, dupv, "grad_w_up_val")
    gw_up = jnp.concatenate([gw_upg[:, :dff], gw_upv[:, :dff]], axis=1)
    gw_down = _matmul_tn(act, dxg3, "grad_w_down")[:dff]
    early = (row_pieces(gw_out), lane_pad(col_pieces(gw_up)), row_pieces(gw_down))
    early = [_to_bf16(pc, "pieces_bf16_" + nm) for pc, nm in zip(early, ("w_out", "w_up", "w_down"))]

    dq_s, dk_s, dv_s, *recv_early = _sb_bwd(qkv, ks_st, ks_t, vs_st, do_sb, rtot, dg, early)
    halves_early = [_sum_leading(rv, nm) for rv, nm in zip(recv_early, ("sum_w_out", "sum_w_up", "sum_w_down"))]
    dq_f, dft, dk_f, dv_f, dkx, *swapped_early = _fox_bwd(qx, kx, kx_t, vf_st, do_fox, drow, lse, dg, halves_early)
    f2d_shape = (nh * t // LANES, LANES)
    dfs = jnp.transpose(dkx.reshape(t, pairs, LANES)[:, :, :2], (1, 2, 0))
    dfl2d, gb8 = _fgate_bwd(fl2d, b_rows, dft.reshape(f2d_shape), dfs.reshape(f2d_shape), tri_in_rev, tri_blk_rev, head_rows)
    dfl = jnp.pad(dfl2d.reshape(nh, t), ((0, LANES - nh), (0, 0)))
    grad_x, dproj, dflb, part_i = _in_proj_bwd([dq_f, dk_f, dv_f, dq_s, dk_s, dv_s], dfl, w_qkv, w_f, x2d, dx2, mod8, g_attn, tm)

    gw_qkv = _matmul_tn(h1, dproj, "grad_w_qkv")
    gw_f = _matmul_tn(h1, dflb, "grad_w_f")
    gw_in = jnp.concatenate([gw_qkv, gw_f[:, :nh]], axis=1)

    sf = _sum_leading(part_f, "sum_part_ffn_fwd")
    sb_ = _sum_leading(part_b, "sum_part_ffn_bwd")
    so = _sum_leading(part_o, "sum_part_attn_out")
    si = _sum_leading(part_i, "sum_part_in_proj")
    scg = _sum_leading(pcg, "sum_part_conv_gate")
    scv = _sum_leading(pcv, "sum_part_conv_val")
    gb_f = gb8[:nh, 0]
    dmod = jnp.concatenate([si[0], si[1], so[0], sb_[0], sb_[1], sf[1]])
    g_conv_w = jnp.concatenate([scg[0:3, :dff], scv[0:3, :dff]], axis=1).reshape(-1)
    g_conv_b = jnp.concatenate([scg[3, :dff], scv[3, :dff]])
    loss_part = jnp.sum(sf[2])
    fields = [dmod, si[2], gb_f, so[1, :dg], so[1, dg:], sb_[2], g_conv_b, sf[0], g_conv_w, loss_part[None]]
    sizes = [int(f.shape[0]) for f in fields]
    n_pack = sum(sizes)
    lanes_pack = -(-n_pack // (8 * LANES)) * LANES
    pack = jnp.pad(jnp.concatenate(fields), (0, 8 * lanes_pack - n_pack)).reshape(8, lanes_pack)
    gathered = _all_gather8(pack)
    tot = _sum_leading(gathered.reshape(8, 8, lanes_pack), "sum_pack").reshape(-1)
    offs = np.concatenate([[0], np.cumsum(sizes)])
    take = lambda k: tot[int(offs[k]):int(offs[k + 1])]
    g_b_ada, g_g_attn, g_b_fgate, g_g_fox, g_g_sb, g_g_mlp, g_cb, g_g_final, g_cw_full, loss_v = [take(k) for k in range(10)]
    loss = loss_v[0]
    dmod_all = gathered.reshape(8, 8 * lanes_pack)[:, :6 * d]
    dmod_cols = lax.dynamic_slice(dmod_all, (0, shard * ada_cols), (8, ada_cols))
    g_w_ada = _ada_bwd(sc_all.T, dmod_cols)

    (recv_in,) = _scatter8([_to_bf16(lane_pad(col_pieces(gw_in)), "pieces_bf16_w_in")])
    (swapped_in,) = _swap_halves([_sum_leading(recv_in, "sum_w_in")])
    swapped = [swapped_in] + swapped_early
    shard_cols = (n_in // 4, d, dff // 2, d)
    g_w_in, g_w_out, g_w_up, g_w_down = [s.reshape(2 * s.shape[1], s.shape[2])[:, :cc] for s, cc in zip(swapped, shard_cols)]
    g_conv_w_shard = lax.dynamic_slice(g_cw_full.reshape(3, 2 * dff), (0, shard * (dff // 2)), (3, dff // 2))

    grads, deltas, new_m, new_v = {}, {}, {}, {}

    def step(name, w, g, m, v):
        shape = w.shape
        as2d = lambda a: a.reshape(-1, shape[-1])
        dl, nm, nv = _adamw(as2d(w), as2d(g), as2d(m), as2d(v), "adamw_" + name)
        grads[name], deltas[name], new_m[name], new_v[name] = g.reshape(shape), dl.reshape(shape), nm.reshape(shape), nv.reshape(shape)

    step("w_ada", w_ada, g_w_ada, m_w_ada, v_w_ada)
    step("w_in", w_in, g_w_in, m_w_in, v_w_in)
    step("w_out", w_out, g_w_out, m_w_out, v_w_out)
    step("w_up", w_up, g_w_up, m_w_up, v_w_up)
    step("conv_w", conv_w, g_conv_w_shard, m_conv_w, v_conv_w)
    step("w_down", w_down, g_w_down, m_w_down, v_w_down)

    small = [("b_ada", b_ada, g_b_ada, m_b_ada, v_b_ada), ("g_attn", g_attn, g_g_attn, m_g_attn, v_g_attn),
             ("b_fgate", b_fgate, g_b_fgate, m_b_fgate, v_b_fgate), ("g_out_fox", g_out_fox, g_g_fox, m_g_out_fox, v_g_out_fox),
             ("g_out_sb", g_out_sb, g_g_sb, m_g_out_sb, v_g_out_sb), ("g_mlp", g_mlp, g_g_mlp, m_g_mlp, v_g_mlp),
             ("conv_b", conv_b, g_cb, m_conv_b, v_conv_b), ("g_final", g_final, g_g_final, m_g_final, v_g_final)]
    ssz = [int(np.prod(s[1].shape)) for s in small]
    n_small = sum(ssz)
    lanes_small = -(-n_small // (8 * LANES)) * LANES
    packs = [jnp.pad(jnp.concatenate([s[k].reshape(-1) for s in small]), (0, 8 * lanes_small - n_small)).reshape(8, lanes_small)
             for k in (1, 2, 3, 4)]
    dl_s, nm_s, nv_s = _adamw(*packs, "adamw_small")
    so_ = np.concatenate([[0], np.cumsum(ssz)])
    for k, s in enumerate(small):
        cut = lambda a: a.reshape(-1)[int(so_[k]):int(so_[k + 1])].reshape(s[1].shape)
        grads[s[0]], deltas[s[0]], new_m[s[0]], new_v[s[0]] = s[2].reshape(s[1].shape), cut(dl_s), cut(nm_s), cut(nv_s)

    order = ["w_ada", "b_ada", "g_attn", "w_in", "b_fgate", "g_out_fox", "g_out_sb", "w_out", "g_mlp", "w_up",
             "conv_w", "conv_b", "w_down", "g_final"]
    return (loss, grad_x[None], *[grads[n] for n in order], *[deltas[n] for n in order],
            *[new_m[n] for n in order], *[new_v[n] for n in order])
```

```python
import functools

import numpy as np
import jax
import jax.numpy as jnp
from jax import lax
from jax.experimental import pallas as pl
from jax.experimental.pallas import tpu as pltpu

F32 = jnp.float32
BF16 = jnp.bfloat16
MESH = pl.DeviceIdType.MESH

HEAD_DIM = 64
LANES = 128
EPS = 1e-6
NEG = -1e30
ADAM_LR, ADAM_B1, ADAM_B2, ADAM_EPS, ADAM_WD, ADAM_STEP = 0.001, 0.9, 0.999, 1e-08, 0.01, 10
V7X_VMEM_BYTES = 64 * 1024 * 1024
VMEM_LIMIT = V7X_VMEM_BYTES - 12 * 1024 * 1024
NT_DIMS = (((1,), (1,)), ((), ()))


def _pcall(body, **kw):
    return pl.pallas_call(body, **kw)


def _params(sem=None, **kw):
    return pltpu.CompilerParams(dimension_semantics=sem, vmem_limit_bytes=VMEM_LIMIT, **kw)


def _split_dot(x, m, passes):
    acc = None
    for _ in range(passes):
        part = x.astype(BF16)
        d = jnp.dot(part, m, preferred_element_type=F32)
        acc = d if acc is None else acc + d
        x = x - part.astype(F32)
    return acc


def _tile(n, candidates):
    for t in candidates:
        if n % t == 0:
            return t
    return n


def _rows_tile(rows, row_bytes, budget=2 * 1024 * 1024):
    best = None
    for t in range(8, rows + 1, 8):
        if rows % t == 0 and t * row_bytes <= budget:
            best = t
    return best if best is not None else rows


def _all_gather8(v):
    m_per, n = v.shape

    def body(x_ref, out_ref, send_sems, recv_sems, local_sem):
        x, y, c = lax.axis_index("x"), lax.axis_index("y"), lax.axis_index("c")
        me, sibling = (x, y, c), (x, y, 1 - c)
        chips = [(1 - x, y), (x, 1 - y), (1 - x, 1 - y)]

        def rows(px, py, pc):
            return out_ref.at[pl.ds((4 * px + 2 * py + pc) * m_per, m_per), :]

        def copy(k, block, to, src=None):
            return pltpu.make_async_remote_copy(
                src_ref=rows(*block) if src is None else src, dst_ref=rows(*block),
                send_sem=send_sems.at[k], recv_sem=recv_sems.at[k], device_id=to, device_id_type=MESH)

        mine = pltpu.make_async_copy(x_ref, rows(*me), local_sem)
        mine.start()
        first = [copy(0, me, sibling, src=x_ref)]
        first += [copy(1 + j, me, (*chip, c), src=x_ref) for j, chip in enumerate(chips)]
        for cp in first:
            cp.start()
        passed = [copy(4 + j, (*chip, c), sibling) for j, chip in enumerate(chips)]
        for j, chip in enumerate(chips):
            copy(1 + j, (*chip, c), me).wait_recv()
            passed[j].start()
        copy(0, sibling, me).wait_recv()
        for j, chip in enumerate(chips):
            copy(4 + j, (*chip, 1 - c), me).wait_recv()
        for cp in first + passed:
            cp.wait_send()
        mine.wait()

    return _pcall(
        body, name="all_gather8",
        out_shape=jax.ShapeDtypeStruct((8 * m_per, n), v.dtype),
        in_specs=[pl.BlockSpec(memory_space=pltpu.VMEM)],
        out_specs=pl.BlockSpec(memory_space=pltpu.VMEM),
        scratch_shapes=[pltpu.SemaphoreType.DMA((7,)), pltpu.SemaphoreType.DMA((7,)), pltpu.SemaphoreType.DMA],
        compiler_params=pltpu.CompilerParams(vmem_limit_bytes=VMEM_LIMIT),
    )(v)


def _gather_xy(shards):
    n = len(shards)

    def body(*refs):
        ins, outs = refs[:n], refs[n:2 * n]
        send_sems, recv_sems, local_sems = refs[2 * n:]
        x, y, c = lax.axis_index("x"), lax.axis_index("y"), lax.axis_index("c")
        chips = [(1 - x, y), (x, 1 - y), (1 - x, 1 - y)]
        mine = 2 * x + y
        local, remote = [], []
        for w in range(n):
            cp = pltpu.make_async_copy(ins[w], outs[w].at[mine], local_sems.at[w])
            cp.start()
            local.append(cp)
            for k, (px, py) in enumerate(chips):
                cp = pltpu.make_async_remote_copy(
                    src_ref=ins[w], dst_ref=outs[w].at[mine], send_sem=send_sems.at[3 * w + k],
                    recv_sem=recv_sems.at[3 * w + k], device_id=(px, py, c), device_id_type=MESH)
                cp.start()
                remote.append(cp)
        for cp in remote:
            cp.wait_recv()
        for cp in remote:
            cp.wait_send()
        for cp in local:
            cp.wait()

    hbm = pl.BlockSpec(memory_space=pltpu.HBM)
    return _pcall(
        body, name="gather_xy",
        out_shape=[jax.ShapeDtypeStruct((4,) + s.shape, s.dtype) for s in shards],
        in_specs=[hbm] * n, out_specs=[hbm] * n,
        scratch_shapes=[pltpu.SemaphoreType.DMA((3 * n,)), pltpu.SemaphoreType.DMA((3 * n,)),
                        pltpu.SemaphoreType.DMA((n,))],
        compiler_params=pltpu.CompilerParams(vmem_limit_bytes=VMEM_LIMIT),
    )(*shards)


def _scatter8(pieces):
    n = len(pieces)

    def body(*refs):
        ins, outs = refs[:n], refs[n:2 * n]
        send_sems, recv_sems, local_sems = refs[2 * n:]
        x, y, c = lax.axis_index("x"), lax.axis_index("y"), lax.axis_index("c")
        me = 4 * x + 2 * y + c
        local, remote = [], []
        for w in range(n):
            cp = pltpu.make_async_copy(ins[w].at[me], outs[w].at[me], local_sems.at[w])
            cp.start()
            local.append(cp)
            for f in range(1, 8):
                px = 1 - x if f & 4 else x
                py = 1 - y if f & 2 else y
                pc = 1 - c if f & 1 else c
                cp = pltpu.make_async_remote_copy(
                    src_ref=ins[w].at[4 * px + 2 * py + pc], dst_ref=outs[w].at[me],
                    send_sem=send_sems.at[7 * w + f - 1], recv_sem=recv_sems.at[7 * w + f - 1],
                    device_id=(px, py, pc), device_id_type=MESH)
                cp.start()
                remote.append(cp)
        for cp in remote:
            cp.wait_recv()
        for cp in remote:
            cp.wait_send()
        for cp in local:
            cp.wait()

    hbm = pl.BlockSpec(memory_space=pltpu.HBM)
    return _pcall(
        body, name="scatter8",
        out_shape=[jax.ShapeDtypeStruct(p.shape, p.dtype) for p in pieces],
        in_specs=[hbm] * n, out_specs=[hbm] * n,
        scratch_shapes=[pltpu.SemaphoreType.DMA((7 * n,)), pltpu.SemaphoreType.DMA((7 * n,)),
                        pltpu.SemaphoreType.DMA((n,))],
        compiler_params=pltpu.CompilerParams(vmem_limit_bytes=VMEM_LIMIT),
    )(*pieces)


def _swap_halves(halves):
    n = len(halves)
    chunks = 8
    n_chunks = [max(k for k in (chunks, 4, 2, 1) if h.shape[0] % (8 * k) == 0) for h in halves]

    def body(*refs):
        ins, outs = refs[:n], refs[n:2 * n]
        send_sems, recv_sems, local_sems = refs[2 * n:]
        x, y, c = lax.axis_index("x"), lax.axis_index("y"), lax.axis_index("c")
        local, remote = [], []
        for w in range(n):
            cp = pltpu.make_async_copy(ins[w], outs[w].at[c], local_sems.at[w])
            cp.start()
            local.append(cp)
            rows = ins[w].shape[0] // n_chunks[w]
            for k in range(n_chunks[w]):
                cp = pltpu.make_async_remote_copy(
                    src_ref=ins[w].at[pl.ds(k * rows, rows)], dst_ref=outs[w].at[c, pl.ds(k * rows, rows)],
                    send_sem=send_sems.at[chunks * w + k], recv_sem=recv_sems.at[chunks * w + k],
                    device_id=(x, y, 1 - c), device_id_type=MESH)
                cp.start()
                remote.append(cp)
        for cp in remote:
            cp.wait_recv()
        for cp in remote:
            cp.wait_send()
        for cp in local:
            cp.wait()

    hbm = pl.BlockSpec(memory_space=pltpu.HBM)
    return _pcall(
        body, name="swap_halves",
        out_shape=[jax.ShapeDtypeStruct((2,) + h.shape, h.dtype) for h in halves],
        in_specs=[hbm] * n, out_specs=[hbm] * n,
        scratch_shapes=[pltpu.SemaphoreType.DMA((chunks * n,)), pltpu.SemaphoreType.DMA((chunks * n,)),
                        pltpu.SemaphoreType.DMA((n,))],
        compiler_params=pltpu.CompilerParams(vmem_limit_bytes=VMEM_LIMIT),
    )(*halves)


def _sum_leading(a, name):
    n, r, c = a.shape
    tr = _rows_tile(r, n * c * 4, budget=6 * 1024 * 1024)
    if a.dtype == BF16 and tr % 16:
        tr = r

    def body(a_ref, o_ref):
        acc = a_ref[0].astype(F32)
        for k in range(1, n):
            acc = acc + a_ref[k].astype(F32)
        o_ref[...] = acc

    return _pcall(
        body, name=name, grid=(r // tr,),
        out_shape=jax.ShapeDtypeStruct((r, c), F32),
        in_specs=[pl.BlockSpec((n, tr, c), lambda i: (0, i, 0))],
        out_specs=pl.BlockSpec((tr, c), lambda i: (i, 0)),
        compiler_params=_params(("arbitrary",)),
    )(a)


def _to_bf16(a, name):
    n, r, c = a.shape

    def body(a_ref, o_ref):
        o_ref[...] = a_ref[...].astype(BF16)

    spec = pl.BlockSpec((1, r, c), lambda i: (i, 0, 0))
    return _pcall(
        body, name=name, grid=(n,), out_shape=jax.ShapeDtypeStruct(a.shape, BF16),
        in_specs=[spec], out_specs=spec, compiler_params=_params(("arbitrary",)),
    )(a)


def _adamw(w, g, m, v, name):
    r, c = w.shape
    tr = _rows_tile(r, c * 4, budget=1024 * 1024)
    c1 = 1.0 - ADAM_B1 ** ADAM_STEP
    c2 = 1.0 - ADAM_B2 ** ADAM_STEP

    def body(w_ref, g_ref, m_ref, v_ref, d_ref, nm_ref, nv_ref):
        gg = g_ref[...]
        nm = ADAM_B1 * m_ref[...] + (1.0 - ADAM_B1) * gg
        nv = ADAM_B2 * v_ref[...] + (1.0 - ADAM_B2) * (gg * gg)
        m_hat = nm / c1
        v_hat = nv / c2
        d_ref[...] = -ADAM_LR * (m_hat / (jnp.sqrt(v_hat) + ADAM_EPS) + ADAM_WD * w_ref[...])
        nm_ref[...] = nm
        nv_ref[...] = nv

    spec = pl.BlockSpec((tr, c), lambda i: (i, 0))
    return _pcall(
        body, name=name, grid=(r // tr,),
        out_shape=[jax.ShapeDtypeStruct((r, c), F32)] * 3,
        in_specs=[spec] * 4, out_specs=[spec] * 3,
        compiler_params=_params(("arbitrary",)),
    )(w, g, m, v)


def _ada_fwd(c_all, w_shard, b_shard):
    nb, d = c_all.shape
    cols = w_shard.shape[1]

    def body(c_ref, w_ref, b_ref, sc_ref, mod_ref):
        cv = c_ref[...]
        sc = cv * jax.nn.sigmoid(cv)
        sc_ref[...] = sc
        mod_ref[...] = jnp.dot(sc.astype(BF16), w_ref[...].astype(BF16), preferred_element_type=F32) + b_ref[...]

    return _pcall(
        body, name="ada_fwd",
        out_shape=[jax.ShapeDtypeStruct((nb, d), F32), jax.ShapeDtypeStruct((nb, cols), F32)],
        compiler_params=pltpu.CompilerParams(vmem_limit_bytes=VMEM_LIMIT),
    )(c_all, w_shard, b_shard)


def _ada_bwd(sc_t, dmod_cols):
    d, nb = sc_t.shape
    cols = dmod_cols.shape[1]
    tr = _rows_tile(d, cols * 4, budget=1024 * 1024)

    def body(s_ref, m_ref, o_ref):
        s = s_ref[...]
        m = m_ref[...]
        acc = s[:, 0:1] * m[0:1, :]
        for b in range(1, nb):
            acc = acc + s[:, b:b + 1] * m[b:b + 1, :]
        o_ref[...] = acc

    return _pcall(
        body, name="ada_bwd", grid=(d // tr,),
        out_shape=jax.ShapeDtypeStruct((d, cols), F32),
        in_specs=[pl.BlockSpec((tr, nb), lambda i: (i, 0)), pl.BlockSpec((nb, cols), lambda i: (0, 0))],
        out_specs=pl.BlockSpec((tr, cols), lambda i: (i, 0)),
        compiler_params=_params(("arbitrary",)),
    )(sc_t, dmod_cols)


def _log_sigmoid(x):
    return jnp.minimum(x, 0.0) - jnp.log1p(jnp.exp(-jnp.abs(x)))


def _fgate_fwd(fl2d, b_rows, tri_in, tri_blk):
    r = fl2d.shape[0]

    def body(x_ref, b_ref, u_ref, l_ref, f_ref):
        lf = _log_sigmoid(x_ref[...] + b_ref[...])
        c1 = _split_dot(lf, u_ref[...], 3)
        tot = jnp.broadcast_to(c1[:, LANES - 1:LANES], (r, LANES))
        acc = None
        for _ in range(3):
            part = tot.astype(BF16)
            dd = jnp.dot(l_ref[...], part, preferred_element_type=F32)
            acc = dd if acc is None else acc + dd
            tot = tot - part.astype(F32)
        f_ref[...] = c1 + acc

    return _pcall(
        body, name="fgate_fwd", out_shape=jax.ShapeDtypeStruct((r, LANES), F32),
        compiler_params=pltpu.CompilerParams(vmem_limit_bytes=VMEM_LIMIT),
    )(fl2d, b_rows, tri_in, tri_blk)


def _fgate_bwd(fl2d, b_rows, df_query, df_key, tri_in_rev, tri_blk_rev, head_rows):
    r = fl2d.shape[0]
    nhp = head_rows.shape[0]

    def body(x_ref, b_ref, dq_ref, dk_ref, u_ref, l_ref, hr_ref, o_ref, gb_ref):
        c1 = _split_dot(dq_ref[...] + dk_ref[...], u_ref[...], 3)
        tot = jnp.broadcast_to(c1[:, 0:1], (r, LANES))
        acc = None
        for _ in range(3):
            part = tot.astype(BF16)
            dd = jnp.dot(l_ref[...], part, preferred_element_type=F32)
            acc = dd if acc is None else acc + dd
            tot = tot - part.astype(F32)
        x = x_ref[...] + b_ref[...]
        e = jnp.exp(-jnp.abs(x))
        dfl = (c1 + acc) * (jnp.where(x >= 0, e, 1.0) / (1.0 + e))
        o_ref[...] = dfl
        rs = jnp.broadcast_to(jnp.sum(dfl, axis=1, keepdims=True), (r, LANES))
        gb = None
        for _ in range(3):
            part = rs.astype(BF16)
            dd = jnp.dot(hr_ref[...], part, preferred_element_type=F32)
            gb = dd if gb is None else gb + dd
            rs = rs - part.astype(F32)
        gb_ref[...] = gb

    return _pcall(
        body, name="fgate_bwd",
        out_shape=[jax.ShapeDtypeStruct((r, LANES), F32), jax.ShapeDtypeStruct((nhp, LANES), F32)],
        compiler_params=pltpu.CompilerParams(vmem_limit_bytes=VMEM_LIMIT),
    )(fl2d, b_rows, df_query, df_key, tri_in_rev, tri_blk_rev, head_rows)


def _norm_mod(x, g, scale, shift):
    r = lax.rsqrt(jnp.mean(x * x, axis=-1, keepdims=True) + EPS)
    return (x * r * g) * (1.0 + scale) + shift


def _norm_mod_bwd(x, dh, g, scale):
    r = lax.rsqrt(jnp.mean(x * x, axis=-1, keepdims=True) + EPS)
    xn = x * r
    dshift = jnp.sum(dh, axis=0, keepdims=True)
    dscale = jnp.sum(dh * (xn * g), axis=0, keepdims=True)
    dxn_g = dh * (1.0 + scale)
    dg = jnp.sum(dxn_g * xn, axis=0, keepdims=True)
    dxn = dxn_g * g
    dx = r * (dxn - xn * jnp.mean(dxn * xn, axis=-1, keepdims=True))
    return dx, dshift, dscale, dg


def _in_proj_fwd(x, mod8, g_attn, w_qkv, w_f, tm):
    t, d = x.shape
    dg = w_qkv.shape[1] // 6

    def body(x_ref, mod_ref, g_ref, w_ref, wf_ref, qkv_ref, fl_ref, h1_ref):
        h = _norm_mod(x_ref[...], g_ref[...], mod_ref[1:2, :], mod_ref[0:1, :]).astype(BF16)
        h1_ref[...] = _transposed(h)
        fl_ref[...] = lax.dot_general(wf_ref[...], h, NT_DIMS, preferred_element_type=F32)
        for k in range(6):
            cols = slice(k * dg, (k + 1) * dg)
            y = jnp.dot(h, w_ref[:, cols], preferred_element_type=F32)
            qkv_ref[:, cols] = (y * HEAD_DIM ** -0.5 if k in (0, 3) else y).astype(BF16)

    once = lambda a: pl.BlockSpec(a.shape, lambda i: (0,) * a.ndim, pipeline_mode=pl.Buffered(1))
    return _pcall(
        body, name="in_proj_fwd", grid=(t // tm,),
        out_shape=[jax.ShapeDtypeStruct((t, 6 * dg), BF16), jax.ShapeDtypeStruct((LANES, t), F32),
                   jax.ShapeDtypeStruct((d, t), BF16)],
        in_specs=[pl.BlockSpec((tm, d), lambda i: (i, 0)), pl.BlockSpec((8, d), lambda i: (0, 0)),
                  pl.BlockSpec((1, d), lambda i: (0, 0)), once(w_qkv), once(w_f)],
        out_specs=[pl.BlockSpec((tm, 6 * dg), lambda i: (i, 0)), pl.BlockSpec((LANES, tm), lambda i: (0, i)),
                   pl.BlockSpec((d, tm), lambda i: (0, i))],
        compiler_params=_params(("arbitrary",)),
    )(x, mod8, g_attn, w_qkv, w_f)


def _head_rstd(o, bd):
    return lax.rsqrt(_split_dot(o * o, bd, 3) * (1.0 / HEAD_DIM) + EPS)


def _attn_out_fwd(x, o_fox, o_sb, g_fox, g_sb, w_out, mod8, bd, tm):
    t, d = x.shape
    dg = o_fox.shape[0]

    def body(x_ref, of_ref, os_ref, gf_ref, gs_ref, w_ref, mod_ref, bd_ref, x2_ref, mix_ref, mixt_ref):
        of, osb = of_ref[...].T, os_ref[...].T
        mf = (of * _head_rstd(of, bd_ref[...]) * gf_ref[...]).astype(BF16)
        ms = (osb * _head_rstd(osb, bd_ref[...]) * gs_ref[...]).astype(BF16)
        mix_ref[:, :dg] = mf
        mix_ref[:, dg:] = ms
        mixt_ref[:dg, :] = _transposed(mf)
        mixt_ref[dg:, :] = _transposed(ms)
        y = jnp.dot(mf, w_ref[:dg, :], preferred_element_type=F32) + jnp.dot(ms, w_ref[dg:, :], preferred_element_type=F32)
        x2_ref[...] = x_ref[...] + mod_ref[2:3, :] * y

    row = lambda w: pl.BlockSpec((tm, w), lambda i: (i, 0))
    full = lambda a: pl.BlockSpec(a.shape, lambda i: (0,) * a.ndim)
    return _pcall(
        body, name="attn_out_fwd", grid=(t // tm,),
        out_shape=[jax.ShapeDtypeStruct((t, d), F32), jax.ShapeDtypeStruct((t, 2 * dg), BF16),
                   jax.ShapeDtypeStruct((2 * dg, t), BF16)],
        in_specs=[row(d), pl.BlockSpec((dg, tm), lambda i: (0, i)), pl.BlockSpec((dg, tm), lambda i: (0, i)),
                  full(g_fox), full(g_sb), full(w_out), full(mod8), full(bd)],
        out_specs=[row(d), row(2 * dg), pl.BlockSpec((2 * dg, tm), lambda i: (0, i))],
        compiler_params=_params(("arbitrary",)),
    )(x, o_fox, o_sb, g_fox, g_sb, w_out, mod8, bd)


def _attn_out_bwd(dx2, mix, o_fox, o_sb, g_fox, g_sb, w_out, mod8, bd, hsel, tm):
    t, d = dx2.shape
    dg = o_fox.shape[0]

    def body(dx_ref, mix_ref, of_ref, os_ref, gf_ref, gs_ref, w_ref, mod_ref, bd_ref, hs_ref,
             dof_ref, dos_ref, dlt_ref, dxg_ref, part_ref):
        dx = dx_ref[...]
        gate = mod_ref[2:3, :]
        dxg = (dx * gate).astype(BF16)
        dxg_ref[...] = dxg
        mixv = mix_ref[...]
        y = jnp.dot(mixv[:, :dg], w_ref[:dg, :], preferred_element_type=F32)
        y = y + jnp.dot(mixv[:, dg:], w_ref[dg:, :], preferred_element_type=F32)
        part_ref[0] = jnp.zeros((8, d), F32)
        part_ref[0, 0:1, :] = jnp.sum(dx * y, axis=0, keepdims=True)
        for grp, (o_ref, g_ref, do_ref) in enumerate(((of_ref, gf_ref, dof_ref), (os_ref, gs_ref, dos_ref))):
            dmix = lax.dot_general(dxg, w_ref[grp * dg:(grp + 1) * dg, :], NT_DIMS, preferred_element_type=F32)
            o = o_ref[...].T
            r = _head_rstd(o, bd_ref[...])
            n = o * r
            part_ref[0, 1:2, grp * dg:(grp + 1) * dg] = jnp.sum(dmix * n, axis=0, keepdims=True)
            dn = dmix * g_ref[...]
            mh = _split_dot(dn * n, bd_ref[...], 3) * (1.0 / HEAD_DIM)
            do = r * (dn - n * mh)
            do_ref[...] = do.astype(BF16)
            if grp == 0:
                prod, dlt = do * o, None
                for _ in range(3):
                    part = prod.astype(BF16)
                    term = lax.dot_general(hs_ref[...], part, NT_DIMS, preferred_element_type=F32)
                    dlt = term if dlt is None else dlt + term
                    prod = prod - part.astype(F32)
                dlt_ref[...] = dlt

    row = lambda w: pl.BlockSpec((tm, w), lambda i: (i, 0))
    full = lambda a: pl.BlockSpec(a.shape, lambda i: (0,) * a.ndim)
    nt = t // tm
    return _pcall(
        body, name="attn_out_bwd", grid=(nt,),
        out_shape=[jax.ShapeDtypeStruct((t, dg), BF16), jax.ShapeDtypeStruct((t, dg), BF16),
                   jax.ShapeDtypeStruct((LANES, t), F32), jax.ShapeDtypeStruct((t, d), BF16),
                   jax.ShapeDtypeStruct((nt, 8, d), F32)],
        in_specs=[row(d), row(2 * dg), pl.BlockSpec((dg, tm), lambda i: (0, i)), pl.BlockSpec((dg, tm), lambda i: (0, i)),
                  full(g_fox), full(g_sb), full(w_out), full(mod8),
                  full(bd), full(hsel)],
        out_specs=[row(dg), row(dg), pl.BlockSpec((LANES, tm), lambda i: (0, i)), row(d), pl.BlockSpec((1, 8, d), lambda i: (i, 0, 0))],
        compiler_params=_params(("arbitrary",)),
    )(dx2, mix, o_fox, o_sb, g_fox, g_sb, w_out, mod8, bd, hsel)


def _in_proj_bwd(dparts, dfl, w_qkv, w_f, x, dx2, mod8, g_attn, tm):
    t, d = x.shape
    dg = dparts[1].shape[1]

    def body(*refs):
        d_refs = refs[:6]
        dfl_ref, w_ref, wf_ref, x_ref, dx2_ref, mod_ref, g_ref, gx_ref, dp_ref, dflb_ref, part_ref = refs[6:]
        dh = None
        for k in range(6):
            dk = d_refs[k][...].T if k in (0, 3) else d_refs[k][...]
            if k in (0, 3):
                dk = dk * HEAD_DIM ** -0.5
            db = dk.astype(BF16)
            dp_ref[:, k * dg:(k + 1) * dg] = db
            term = lax.dot_general(db, w_ref[:, k * dg:(k + 1) * dg], NT_DIMS, preferred_element_type=F32)
            dh = term if dh is None else dh + term
        dfb = dfl_ref[...].T.astype(BF16)
        dflb_ref[...] = dfb
        dh = dh + jnp.dot(dfb, wf_ref[...], preferred_element_type=F32)
        dx, dshift, dscale, dgn = _norm_mod_bwd(x_ref[...], dh, g_ref[...], mod_ref[1:2, :])
        gx_ref[...] = dx2_ref[...] + dx
        part_ref[0] = jnp.zeros((8, d), F32)
        part_ref[0, 0:1, :] = dshift
        part_ref[0, 1:2, :] = dscale
        part_ref[0, 2:3, :] = dgn

    row = lambda w: pl.BlockSpec((tm, w), lambda i: (i, 0))
    full = lambda a: pl.BlockSpec(a.shape, lambda i: (0,) * a.ndim)
    nt = t // tm
    return _pcall(
        body, name="in_proj_bwd", grid=(nt,),
        out_shape=[jax.ShapeDtypeStruct((t, d), F32), jax.ShapeDtypeStruct((t, 6 * dg), BF16),
                   jax.ShapeDtypeStruct((t, LANES), BF16), jax.ShapeDtypeStruct((nt, 8, d), F32)],
        in_specs=[pl.BlockSpec((dg, tm), lambda i: (0, i)), row(dg), row(dg)] * 2
        + [pl.BlockSpec((LANES, tm), lambda i: (0, i)), full(w_qkv), full(w_f), row(d), row(d), full(mod8), full(g_attn)],
        out_specs=[row(d), row(6 * dg), row(LANES), pl.BlockSpec((1, 8, d), lambda i: (i, 0, 0))],
        compiler_params=_params(("arbitrary",)),
    )(*dparts, dfl, w_qkv, w_f, x, dx2, mod8, g_attn)


def _matmul_tn(a_t, b, name):
    m, t = a_t.shape
    n = b.shape[1]
    tm_ = _tile(m, (1408, 1024, 512, 256, 128))
    tn_ = _tile(n, (1408, 1024, 512, 256, 128))
    tk = _tile(t, (1024, 512, 256, 128))
    nk = t // tk

    def body(a_ref, b_ref, o_ref):
        k = pl.program_id(2)

        @pl.when(k == 0)
        def _():
            o_ref[...] = jnp.zeros_like(o_ref)

        o_ref[...] += jnp.dot(a_ref[...], b_ref[...], preferred_element_type=F32)

    return _pcall(
        body, name=name, grid=(m // tm_, n // tn_, nk),
        out_shape=jax.ShapeDtypeStruct((m, n), F32),
        in_specs=[pl.BlockSpec((tm_, tk), lambda i, j, k: (i, k)), pl.BlockSpec((tk, tn_), lambda i, j, k: (k, j))],
        out_specs=pl.BlockSpec((tm_, tn_), lambda i, j, k: (i, j)),
        compiler_params=_params(("arbitrary", "arbitrary", "arbitrary")),
    )(a_t, b)


HALO = 16


def _conv_taps(up_ext, cw, lo, rows):
    s1 = pltpu.roll(up_ext, 1, 0)
    s2 = pltpu.roll(up_ext, 2, 0)
    u = cw[2:3, :] * up_ext[lo:lo + rows] + cw[1:2, :] * s1[lo:lo + rows] + cw[0:1, :] * s2[lo:lo + rows] + cw[3:4, :]
    return u, s1, s2


def _chunk_major(w, cf):
    d, n = w.shape[0], w.shape[1] // cf
    return jnp.transpose(w.reshape(d, n, cf), (1, 0, 2))


def _ffn_fwd(x2, target, mod8, g_mlp, g_final, wg, wv, cwg, cwv, wd, tm, cf):
    t, d = x2.shape
    dfp = wg.shape[1]
    nt, nc = t // tm, dfp // cf
    hb = tm // HALO
    wg_c, wv_c = _chunk_major(wg, cf), _chunk_major(wv, cf)

    def body(x_ref, xp_ref, tg_ref, mod_ref, g_ref, gf_ref, wg_ref, wv_ref, cg_ref, cv_ref, wd_ref,
             dx3_ref, h2_ref, part_ref, act_sc):
        i = pl.program_id(0)
        xe = jnp.concatenate([xp_ref[...], x_ref[...]], axis=0)
        h = _norm_mod(xe, g_ref[...], mod_ref[4:5, :], mod_ref[3:4, :]).astype(BF16)
        h2_ref[...] = _transposed(h[HALO:])
        first = jnp.where(i > 0, h[:HALO], jnp.zeros_like(h[:HALO]))
        h = jnp.concatenate([first, h[HALO:]], axis=0)

        def up(c):
            return (jnp.dot(h, wg_ref[c], preferred_element_type=F32), jnp.dot(h, wv_ref[c], preferred_element_type=F32))

        def activation(c, ups):
            cols = slice(c * cf, (c + 1) * cf)
            ug, _, _ = _conv_taps(ups[0], cg_ref[:, cols], HALO, tm)
            uv, _, _ = _conv_taps(ups[1], cv_ref[:, cols], HALO, tm)
            act_sc[:, cols] = (ug * jax.nn.sigmoid(ug) * uv).astype(BF16)

        for c0 in range(0, nc, 2):
            group = list(range(c0, min(c0 + 2, nc)))
            ups = [up(c) for c in group]
            for c, u in zip(group, ups):
                activation(c, u)

        y_ffn = jnp.dot(act_sc[...], wd_ref[...], preferred_element_type=F32)
        x3 = x_ref[...] + mod_ref[5:6, :] * y_ffn
        r3 = lax.rsqrt(jnp.mean(x3 * x3, axis=-1, keepdims=True) + EPS)
        xn = x3 * r3
        gf = gf_ref[...]
        diff = xn * gf - tg_ref[...]
        dy = diff * (1.0 / d)
        dxn = dy * gf
        dx3 = r3 * (dxn - xn * jnp.mean(dxn * xn, axis=-1, keepdims=True))
        dx3_ref[...] = dx3
        part_ref[0] = jnp.zeros((8, d), F32)
        part_ref[0, 0:1, :] = jnp.sum(dy * xn, axis=0, keepdims=True)
        part_ref[0, 1:2, :] = jnp.sum(dx3 * y_ffn, axis=0, keepdims=True)
        part_ref[0, 2:3, :] = jnp.sum(diff * diff, axis=0, keepdims=True) * (0.5 / d)

    row = lambda w: pl.BlockSpec((tm, w), lambda i: (i, 0))
    full = lambda a: pl.BlockSpec(a.shape, lambda i: (0,) * a.ndim)
    once = lambda a: pl.BlockSpec(a.shape, lambda i: (0,) * a.ndim, pipeline_mode=pl.Buffered(1))
    return _pcall(
        body, name="ffn_fwd", grid=(nt,),
        out_shape=[jax.ShapeDtypeStruct((t, d), F32), jax.ShapeDtypeStruct((d, t), BF16),
                   jax.ShapeDtypeStruct((nt, 8, d), F32)],
        in_specs=[row(d), pl.BlockSpec((HALO, d), lambda i: (jnp.maximum(i * hb - 1, 0), 0)), row(d),
                  full(mod8), full(g_mlp), full(g_final), once(wg_c), once(wv_c), once(cwg), once(cwv), once(wd)],
        out_specs=[row(d), pl.BlockSpec((d, tm), lambda i: (0, i)), pl.BlockSpec((1, 8, d), lambda i: (i, 0, 0))],
        scratch_shapes=[pltpu.VMEM((tm, dfp), BF16)],
        compiler_params=_params(("arbitrary",)),
    )(x2, x2, target, mod8, g_mlp, g_final, wg_c, wv_c, cwg, cwv, wd)


def _ffn_bwd(x2, dx3, mod8, g_mlp, wg, wv, cwg, cwv, wd, tm, cf):
    t, d = x2.shape
    dfp = wg.shape[1]
    nt, nc = t // tm, dfp // cf
    hb = tm // HALO
    nhb = t // HALO
    n = tm + HALO

    def body(x_ref, xp_ref, xn_ref, dx_ref, dxn_ref, mod_ref, g_ref, wg_ref, wv_ref, cg_ref, cv_ref, wd_ref,
             dx2_ref, dug_ref, duv_ref, act_ref, dxg_ref, part_ref, pcg_ref, pcv_ref):
        i = pl.program_id(0)
        xe = jnp.concatenate([xp_ref[...], x_ref[...], xn_ref[...]], axis=0)
        h = _norm_mod(xe, g_ref[...], mod_ref[4:5, :], mod_ref[3:4, :]).astype(BF16)
        h = jnp.concatenate([jnp.where(i > 0, h[:HALO], jnp.zeros_like(h[:HALO])), h[HALO:]], axis=0)
        dx = dx_ref[...] * mod_ref[5:6, :]
        dxn = jnp.where(i < nt - 1, dxn_ref[...] * mod_ref[5:6, :], 0.0)
        de = jnp.concatenate([dx, dxn], axis=0).astype(BF16)
        dxg_ref[...] = de[:tm]
        pcg_ref[0] = jnp.zeros((8, dfp), F32)
        pcv_ref[0] = jnp.zeros((8, dfp), F32)

        def products(c):
            cols = slice(c * cf, (c + 1) * cf)
            return (jnp.dot(h, wg_ref[:, cols], preferred_element_type=F32), jnp.dot(h, wv_ref[:, cols], preferred_element_type=F32),
                    lax.dot_general(de, wd_ref[cols, :], NT_DIMS, preferred_element_type=F32))

        def back(du, cw, up, s1, s2, pc_ref, cols):
            dup = (cw[2:3, :] * du + cw[1:2, :] * pltpu.roll(du, n - 1, 0) + cw[0:1, :] * pltpu.roll(du, n - 2, 0))[:tm]
            dut = du[:tm]
            pc_ref[0, 0:1, cols] = jnp.sum(dut * s2[HALO:HALO + tm], axis=0, keepdims=True)
            pc_ref[0, 1:2, cols] = jnp.sum(dut * s1[HALO:HALO + tm], axis=0, keepdims=True)
            pc_ref[0, 2:3, cols] = jnp.sum(dut * up[HALO:HALO + tm], axis=0, keepdims=True)
            pc_ref[0, 3:4, cols] = jnp.sum(dut, axis=0, keepdims=True)
            return dup.astype(BF16)

        def chunk(c, prods):
            cols = slice(c * cf, (c + 1) * cf)
            upg, upv, dact = prods
            cg, cv = cg_ref[:, cols], cv_ref[:, cols]
            ug, g1, g2 = _conv_taps(upg, cg, HALO, n)
            uv, v1, v2 = _conv_taps(upv, cv, HALO, n)
            sg = jax.nn.sigmoid(ug)
            sil = ug * sg
            act_ref[cols, :] = _transposed((sil * uv)[:tm].astype(BF16))
            dug_ref[:, cols] = back(dact * uv * (sg * (1.0 + ug * (1.0 - sg))), cg, upg, g1, g2, pcg_ref, cols)
            duv_ref[:, cols] = back(dact * sil, cv, upv, v1, v2, pcv_ref, cols)

        for c0 in range(0, nc, 2):
            group = list(range(c0, min(c0 + 2, nc)))
            prods = [products(c) for c in group]
            for c, pr in zip(group, prods):
                chunk(c, pr)

        dh = (lax.dot_general(dug_ref[...], wg_ref[...], NT_DIMS, preferred_element_type=F32)
              + lax.dot_general(duv_ref[...], wv_ref[...], NT_DIMS, preferred_element_type=F32))
        dxt, dshift, dscale, dgn = _norm_mod_bwd(x_ref[...], dh, g_ref[...], mod_ref[4:5, :])
        dx2_ref[...] = dx_ref[...] + dxt
        part_ref[0] = jnp.zeros((8, d), F32)
        part_ref[0, 0:1, :] = dshift
        part_ref[0, 1:2, :] = dscale
        part_ref[0, 2:3, :] = dgn

    row = lambda w: pl.BlockSpec((tm, w), lambda i: (i, 0))
    prev = pl.BlockSpec((HALO, d), lambda i: (jnp.maximum(i * hb - 1, 0), 0))
    nxt = pl.BlockSpec((HALO, d), lambda i: (jnp.minimum((i + 1) * hb, nhb - 1), 0))
    full = lambda a: pl.BlockSpec(a.shape, lambda i: (0,) * a.ndim)
    once = lambda a: pl.BlockSpec(a.shape, lambda i: (0,) * a.ndim, pipeline_mode=pl.Buffered(1))
    part = lambda w: pl.BlockSpec((1, 8, w), lambda i: (i, 0, 0))
    return _pcall(
        body, name="ffn_bwd", grid=(nt,),
        out_shape=[jax.ShapeDtypeStruct((t, d), F32), jax.ShapeDtypeStruct((t, dfp), BF16),
                   jax.ShapeDtypeStruct((t, dfp), BF16), jax.ShapeDtypeStruct((dfp, t), BF16),
                   jax.ShapeDtypeStruct((t, d), BF16), jax.ShapeDtypeStruct((nt, 8, d), F32),
                   jax.ShapeDtypeStruct((nt, 8, dfp), F32), jax.ShapeDtypeStruct((nt, 8, dfp), F32)],
        in_specs=[row(d), prev, nxt, row(d), nxt, full(mod8), full(g_mlp), once(wg), once(wv), once(cwg), once(cwv), once(wd)],
        out_specs=[row(d), row(dfp), row(dfp), pl.BlockSpec((dfp, tm), lambda i: (0, i)), row(d), part(d), part(dfp), part(dfp)],
        compiler_params=_params(("arbitrary",)),
    )(x2, x2, x2, dx3, dx3, mod8, g_mlp, wg, wv, cwg, cwv, wd)


BLK = 2 * LANES
XROWS = 144
LANE_FS, LANE_FT_A, LANE_FT_B = 0, 3, 6


def _head_masks():
    lane = lax.broadcasted_iota(jnp.int32, (1, LANES), 1)
    in_a = lane < HEAD_DIM
    return in_a, jnp.logical_not(in_a)


def _pieces3(x):
    hi = x.astype(BF16).astype(F32)
    r = x - hi
    mid = r.astype(BF16).astype(F32)
    return hi, mid, (r - mid).astype(BF16).astype(F32)


def _bias_lanes(rows, entries):
    sub = lax.broadcasted_iota(jnp.int32, (16, 1), 0)
    out = jnp.zeros((16, rows), F32)
    for l, v in entries:
        out = jnp.where(sub == l, v, out)
    return jnp.concatenate([out, jnp.zeros((LANES - 16, rows), F32)], axis=0).T


def _three(first, values):
    return [(first + k, v) for k, v in enumerate(values)]


def _stack_rows(x, in_a, in_b):
    zero = jnp.zeros_like(x)
    return jnp.concatenate([jnp.where(in_a, x, zero), jnp.where(in_b, x, zero)], axis=0)


def _transposed(x):
    return x.T


def _attn_operands(qkv, frow, dg):
    t = qkv.shape[0]
    p, nk = dg // LANES, t // BLK

    def body(qf_ref, kf_ref, vf_ref, ks_ref, vs_ref, f_ref, qx_ref, kx_ref, kxt_ref, vf_o, vft_o, ks_o, kst_o, vs_o, vst_o):
        in_a, in_b = _head_masks()
        fa, fb = _pieces3(f_ref[0]), _pieces3(f_ref[1])
        qx_ref[0, :, :LANES] = qf_ref[...]
        qx_ref[0, :, LANES:] = _bias_lanes(
            BLK, _three(LANE_FS, (-1.0,) * 3) + _three(LANE_FT_A, fa) + _three(LANE_FT_B, fb)).astype(BF16)
        kf = kf_ref[...]
        zero = jnp.zeros_like(kf)
        top = jnp.concatenate([jnp.where(in_a, kf, zero), _bias_lanes(
            BLK, _three(LANE_FS, fa) + _three(LANE_FT_A, (1.0,) * 3)).astype(BF16)], axis=1)
        bot = jnp.concatenate([jnp.where(in_b, kf, zero), _bias_lanes(
            BLK, _three(LANE_FS, fb) + _three(LANE_FT_B, (1.0,) * 3)).astype(BF16)], axis=1)
        kx = jnp.concatenate([top, bot], axis=0)
        kx_ref[0, 0] = kx
        kxt_ref[0, 0] = _transposed(kx)[:XROWS]
        for src, dst, dst_t in ((vf_ref, vf_o, vft_o), (ks_ref, ks_o, kst_o), (vs_ref, vs_o, vst_o)):
            st = _stack_rows(src[...], in_a, in_b)
            dst[0, 0] = st
            dst_t[0, 0] = _transposed(st)

    col = lambda base: pl.BlockSpec((BLK, LANES), lambda h, j: (j, base * p + h))
    blk4 = lambda r, c: pl.BlockSpec((1, 1, r, c), lambda h, j: (h, j, 0, 0))
    shp4 = lambda r, c: jax.ShapeDtypeStruct((p, nk, r, c), BF16)
    return _pcall(
        body, name="attn_operands", grid=(p, nk),
        out_shape=[jax.ShapeDtypeStruct((p, t, 2 * LANES), BF16), shp4(2 * BLK, 2 * LANES), shp4(XROWS, 2 * BLK)]
        + [shp4(2 * BLK, LANES), shp4(LANES, 2 * BLK)] * 3,
        in_specs=[col(0), col(1), col(2), col(4), col(5), pl.BlockSpec((2, 1, BLK), lambda h, j: (h, 0, j))],
        out_specs=[pl.BlockSpec((1, BLK, 2 * LANES), lambda h, j: (h, j, 0)), blk4(2 * BLK, 2 * LANES), blk4(XROWS, 2 * BLK)]
        + [blk4(2 * BLK, LANES), blk4(LANES, 2 * BLK)] * 3,
        compiler_params=_params(("arbitrary", "arbitrary")),
    )(qkv, qkv, qkv, qkv, qkv, frow)


def _key_query_masks():
    key = lax.broadcasted_iota(jnp.int32, (BLK, BLK), 0)
    qry = lax.broadcasted_iota(jnp.int32, (BLK, BLK), 1)
    return key <= qry, key < qry


def _key_triangle(kind):
    s = lax.broadcasted_iota(jnp.int32, (BLK, BLK), 0)
    j = lax.broadcasted_iota(jnp.int32, (BLK, BLK), 1)
    return {"suffix": j >= s, "prefix": j <= s, "before": j < s}[kind].astype(BF16)


def _tri_dot(tri, x, passes):
    acc = None
    for _ in range(passes):
        part = x.astype(BF16)
        d = jnp.dot(tri, part, preferred_element_type=F32)
        acc = d if acc is None else acc + d
        x = x - part.astype(F32)
    return acc


GROUPS = (4, 2, 1)


def _loop_blocks(n, tiles, carry, descending=False, groups=GROUPS):
    at = (lambda k: n - 1 - k) if descending else (lambda k: k)
    done = 0
    for g in groups:
        left = n - done
        carry = lax.fori_loop(0, left // g, lambda h, c, g=g, done=done: tiles([at(done + g * h + k) for k in range(g)], c), carry)
        done = done + (left // g) * g
    return carry


def _resident(shape):
    return pl.BlockSpec((1,) + shape, lambda h, i: (h,) + (0,) * len(shape), pipeline_mode=pl.Buffered(1))


def _rows_per_head(a, b):
    return jnp.concatenate([jnp.broadcast_to(a, (HEAD_DIM, BLK)), jnp.broadcast_to(b, (HEAD_DIM, BLK))], axis=0)


def _fold_heads(stacked, in_a):
    return jnp.where(in_a, stacked[:BLK], stacked[BLK:])


def _xy_gather_copies(ins, outs, send_sems, recv_sems, local_sems):
    x, y, c = lax.axis_index("x"), lax.axis_index("y"), lax.axis_index("c")
    chips = [(1 - x, y), (x, 1 - y), (1 - x, 1 - y)]
    mine = 2 * x + y
    local, remote = [], []
    for w in range(len(ins)):
        local.append(pltpu.make_async_copy(ins[w], outs[w].at[mine], local_sems.at[w]))
        for k, (px, py) in enumerate(chips):
            remote.append(pltpu.make_async_remote_copy(
                src_ref=ins[w], dst_ref=outs[w].at[mine], send_sem=send_sems.at[3 * w + k],
                recv_sem=recv_sems.at[3 * w + k], device_id=(px, py, c), device_id_type=MESH))
    return local, remote


def _fox_fwd(qx, kx, v_t, dg, shards):
    p, t = qx.shape[0], qx.shape[1]
    nq = t // BLK
    nh = 2 * p
    ns = len(shards)

    def body(q_ref, k_ref, vt_ref, *rest):
        shard_refs, (o_ref, lse_ref), gathered = rest[:ns], rest[ns:ns + 2], rest[ns + 2:2 * ns + 2]
        local, remote = _xy_gather_copies(shard_refs, gathered, *rest[2 * ns + 2:])
        i = pl.program_id(1)

        @pl.when((pl.program_id(0) == 0) & (i == 0))
        def _():
            for cp in local + remote:
                cp.start()

        causal, _ = _key_query_masks()
        q = q_ref[0]

        def scores(j, masked):
            s2 = lax.dot_general(k_ref[0, j], q, NT_DIMS, preferred_element_type=F32)
            s = [s2[a * BLK:(a + 1) * BLK] for a in range(2)]
            return [jnp.where(causal, x, NEG) for x in s] if masked else s

        def update(blocks, carry):
            m, l, acc = list(carry[0]), list(carry[1]), carry[2]
            for j, s in blocks:
                alpha, pr = [], []
                for a in range(2):
                    mn = jnp.maximum(m[a], jnp.max(s[a], axis=0, keepdims=True))
                    pa = jnp.exp(s[a] - mn)
                    al = jnp.exp(m[a] - mn)
                    l[a] = al * l[a] + jnp.sum(pa, axis=0, keepdims=True)
                    m[a] = mn
                    alpha.append(al)
                    pr.append(pa.astype(BF16))
                acc = _rows_per_head(*alpha) * acc + jnp.dot(vt_ref[0, j], jnp.concatenate(pr, axis=0), preferred_element_type=F32)
            return tuple(m), tuple(l), acc

        tiles = lambda js, c: update([(j, scores(j, False)) for j in js], c)
        neg, zero = jnp.full((1, BLK), NEG, F32), jnp.zeros((1, BLK), F32)
        carry = _loop_blocks(i, tiles, ((neg, neg), (zero, zero), jnp.zeros((LANES, BLK), F32)), groups=(8, 4, 2, 1))
        m, l, acc = update([(i, scores(i, True))], carry)
        o_ref[...] = acc / _rows_per_head(*l)
        lse_ref[0] = m[0] + jnp.log(l[0])
        lse_ref[1] = m[1] + jnp.log(l[1])

        @pl.when((pl.program_id(0) == p - 1) & (i == nq - 1))
        def _():
            for cp in remote:
                cp.wait_recv()
            for cp in remote:
                cp.wait_send()
            for cp in local:
                cp.wait()

    row = pl.BlockSpec((2, 1, BLK), lambda h, i: (h, 0, i))
    hbm = pl.BlockSpec(memory_space=pltpu.HBM)
    return _pcall(
        body, name="fox_fwd", grid=(p, nq),
        out_shape=[jax.ShapeDtypeStruct((dg, t), F32), jax.ShapeDtypeStruct((nh, 1, t), F32)]
        + [jax.ShapeDtypeStruct((4,) + s.shape, s.dtype) for s in shards],
        in_specs=[pl.BlockSpec((1, BLK, 2 * LANES), lambda h, i: (h, i, 0)), _resident((nq, 2 * BLK, 2 * LANES)),
                  _resident((nq, LANES, 2 * BLK))] + [hbm] * ns,
        out_specs=[pl.BlockSpec((LANES, BLK), lambda h, i: (h, i)), row] + [hbm] * ns,
        scratch_shapes=[pltpu.SemaphoreType.DMA((3 * ns,)), pltpu.SemaphoreType.DMA((3 * ns,)), pltpu.SemaphoreType.DMA((ns,))],
        compiler_params=_params(("arbitrary", "arbitrary")),
    )(qx, kx, v_t, *shards)


def _swap_copies(ins, outs, send_sems, recv_sems, local_sems):
    x, y, c = lax.axis_index("x"), lax.axis_index("y"), lax.axis_index("c")
    local, remote = [], []
    for w in range(len(ins)):
        local.append(pltpu.make_async_copy(ins[w], outs[w].at[c], local_sems.at[w]))
        remote.append(pltpu.make_async_remote_copy(
            src_ref=ins[w], dst_ref=outs[w].at[c], send_sem=send_sems.at[w], recv_sem=recv_sems.at[w],
            device_id=(x, y, 1 - c), device_id_type=MESH))
    return local, remote


def _fox_bwd(qx, kx, kx_t, v_st, do, delta, lse, dg, halves):
    p, t = qx.shape[0], qx.shape[1]
    nq = t // BLK
    nh = 2 * p
    ns = len(halves)

    def body(q_ref, k_ref, kt_ref, v_ref, do_ref, dl_ref, lse_ref, *rest):
        half_refs, (dq_ref, dft_ref, dk_ref, dv_ref, dkx_ref), both_refs = rest[:ns], rest[ns:ns + 5], rest[ns + 5:2 * ns + 5]
        local, remote = _swap_copies(half_refs, both_refs, *rest[2 * ns + 5:])
        i = pl.program_id(1)

        @pl.when((pl.program_id(0) == 0) & (i == 0))
        def _():
            for cp in local + remote:
                cp.start()

        @pl.when(i == 0)
        def _():
            dk_ref[...] = jnp.zeros_like(dk_ref)
            dv_ref[...] = jnp.zeros_like(dv_ref)
            dkx_ref[...] = jnp.zeros_like(dkx_ref)

        in_a, _ = _head_masks()
        first_lane = lax.broadcasted_iota(jnp.int32, (1, LANES), 1) == 0
        causal, _ = _key_query_masks()
        q, do2 = q_ref[0], do_ref[...]
        dl = (dl_ref[0], dl_ref[1])
        lse = (lse_ref[0], lse_ref[1])

        def products(j):
            return (lax.dot_general(k_ref[0, j], q, NT_DIMS, preferred_element_type=F32),
                    lax.dot_general(v_ref[0, j], do2, NT_DIMS, preferred_element_type=F32))

        def dscores(prod, masked):
            s2, dp2 = prod
            pr, ds = [], []
            for a in range(2):
                s = s2[a * BLK:(a + 1) * BLK]
                if masked:
                    s = jnp.where(causal, s, NEG)
                pa = jnp.exp(s - lse[a])
                ds.append((pa * (dp2[a * BLK:(a + 1) * BLK] - dl[a])).astype(BF16))
                pr.append(pa.astype(BF16))
            return jnp.concatenate(ds, axis=0), jnp.concatenate(pr, axis=0)

        def accumulate(j, dsb, prb, dq):
            off = pl.multiple_of(j * BLK, BLK)
            dk_full = jnp.dot(dsb, q, preferred_element_type=F32)
            dk_ref[pl.ds(off, BLK), :] += _fold_heads(dk_full[:, :LANES], in_a)
            dkx_ref[pl.ds(off, BLK), :] += jnp.where(first_lane, dk_full[:BLK, LANES:], dk_full[BLK:, LANES:])
            dv_ref[pl.ds(off, BLK), :] += _fold_heads(jnp.dot(prb, do2, preferred_element_type=F32), in_a)
            return dq + jnp.dot(kt_ref[0, j], dsb, preferred_element_type=F32)

        def tiles(js, dq, masked=False):
            prods = [products(j) for j in js]
            grads = [dscores(pr, masked) for pr in prods]
            for j, (dsb, prb) in zip(js, grads):
                dq = accumulate(j, dsb, prb, dq)
            return dq

        dq = _loop_blocks(i, tiles, jnp.zeros((XROWS, BLK), F32))
        dq = tiles([i], dq, True)
        dq_ref[...] = dq[:LANES]
        dft_ref[0] = dq[LANES + LANE_FT_A:LANES + LANE_FT_A + 1]
        dft_ref[1] = dq[LANES + LANE_FT_B:LANES + LANE_FT_B + 1]

        @pl.when((pl.program_id(0) == p - 1) & (i == nq - 1))
        def _():
            for cp in remote:
                cp.wait_recv()
            for cp in remote:
                cp.wait_send()
            for cp in local:
                cp.wait()

    row = pl.BlockSpec((2, 1, BLK), lambda h, i: (h, 0, i))
    acc = pl.BlockSpec((t, LANES), lambda h, i: (0, h))
    hbm = pl.BlockSpec(memory_space=pltpu.HBM)
    return _pcall(
        body, name="fox_bwd", grid=(p, nq),
        out_shape=[jax.ShapeDtypeStruct((dg, t), F32), jax.ShapeDtypeStruct((nh, 1, t), F32)] + [jax.ShapeDtypeStruct((t, dg), F32)] * 3
        + [jax.ShapeDtypeStruct((2,) + h.shape, h.dtype) for h in halves],
        in_specs=[pl.BlockSpec((1, BLK, 2 * LANES), lambda h, i: (h, i, 0)), _resident((nq, 2 * BLK, 2 * LANES)),
                  _resident((nq, XROWS, 2 * BLK)), _resident((nq, 2 * BLK, LANES)),
                  pl.BlockSpec((BLK, LANES), lambda h, i: (i, h)), row, row] + [hbm] * ns,
        out_specs=[pl.BlockSpec((LANES, BLK), lambda h, i: (h, i)), row, acc, acc, acc] + [hbm] * ns,
        scratch_shapes=[pltpu.SemaphoreType.DMA((ns,)), pltpu.SemaphoreType.DMA((ns,)), pltpu.SemaphoreType.DMA((ns,))],
        compiler_params=_params(("arbitrary", "arbitrary")),
    )(qx, kx, kx_t, v_st, do, delta, lse, *halves)


def _softplus_of(z):
    return jnp.maximum(z, 0.0) + jnp.log(1.0 + jnp.exp(-jnp.abs(z)))


def _sb_fwd(qkv, k_st, v_t, dg):
    t = qkv.shape[0]
    p, nq = dg // LANES, t // BLK
    nh = 2 * p

    def body(q_ref, k_ref, vt_ref, o_ref, rt_ref):
        i = pl.program_id(1)
        _, strict = _key_query_masks()
        suffix = _key_triangle("suffix")
        q = q_ref[...]

        def scores(j):
            z2 = lax.dot_general(k_ref[0, j], q, NT_DIMS, preferred_element_type=F32)
            return [z2[a * BLK:(a + 1) * BLK] for a in range(2)]

        def suffix_sums(z, masked):
            out = []
            for a in range(2):
                sp = _softplus_of(z[a])
                if masked:
                    sp = jnp.where(strict, sp, 0.0)
                out.append(_tri_dot(suffix, sp, 2))
            return out

        def weights(z, cs, rest, masked):
            w, rest_new = [], []
            for a in range(2):
                wa = jnp.exp(z[a] - cs[a] - rest[a])
                if masked:
                    wa = jnp.where(strict, wa, 0.0)
                w.append(wa.astype(BF16))
                rest_new.append(rest[a] + cs[a][0:1])
            return jnp.concatenate(w, axis=0), tuple(rest_new)

        def tiles(js, carry, masked=False):
            rest, acc = carry
            zs = [scores(j) for j in js]
            css = [suffix_sums(z, masked) for z in zs]
            ws = []
            for z, cs in zip(zs, css):
                w2, rest = weights(z, cs, rest, masked)
                ws.append(w2)
            for j, w2 in zip(js, ws):
                acc = acc + jnp.dot(vt_ref[0, j], w2, preferred_element_type=F32)
            return rest, acc

        zero = jnp.zeros((1, BLK), F32)
        carry = tiles([i], ((zero, zero), jnp.zeros((LANES, BLK), F32)), True)
        rest, acc = _loop_blocks(i, tiles, carry, descending=True, groups=(8, 4, 2, 1))
        o_ref[...] = acc
        rt_ref[0] = rest[0]
        rt_ref[1] = rest[1]

    return _pcall(
        body, name="sb_fwd", grid=(p, nq),
        out_shape=[jax.ShapeDtypeStruct((dg, t), F32), jax.ShapeDtypeStruct((nh, 1, t), F32)],
        in_specs=[pl.BlockSpec((BLK, LANES), lambda h, i: (i, 3 * p + h)), _resident((nq, 2 * BLK, LANES)),
                  _resident((nq, LANES, 2 * BLK))],
        out_specs=[pl.BlockSpec((LANES, BLK), lambda h, i: (h, i)), pl.BlockSpec((2, 1, BLK), lambda h, i: (h, 0, i))],
        compiler_params=_params(("arbitrary", "arbitrary")),
    )(qkv, k_st, v_t)


def _scatter8_copies(ins, outs, send_sems, recv_sems, local_sems):
    x, y, c = lax.axis_index("x"), lax.axis_index("y"), lax.axis_index("c")
    me = 4 * x + 2 * y + c
    local, remote = [], []
    for w in range(len(ins)):
        local.append(pltpu.make_async_copy(ins[w].at[me], outs[w].at[me], local_sems.at[w]))
        for f in range(1, 8):
            px = 1 - x if f & 4 else x
            py = 1 - y if f & 2 else y
            pc = 1 - c if f & 1 else c
            remote.append(pltpu.make_async_remote_copy(
                src_ref=ins[w].at[4 * px + 2 * py + pc], dst_ref=outs[w].at[me],
                send_sem=send_sems.at[7 * w + f - 1], recv_sem=recv_sems.at[7 * w + f - 1],
                device_id=(px, py, pc), device_id_type=MESH))
    return local, remote


def _sb_bwd(qkv, k_st, k_t, v_st, do, rtot, dg, pieces):
    t = qkv.shape[0]
    p, nq = dg // LANES, t // BLK
    ns = len(pieces)

    def body(q_ref, k_ref, kt_ref, v_ref, do_ref, rt_ref, *rest):
        piece_refs, (dq_ref, dk_ref, dv_ref), recv_refs = rest[:ns], rest[ns:ns + 3], rest[ns + 3:2 * ns + 3]
        local, remote = _scatter8_copies(piece_refs, recv_refs, *rest[2 * ns + 3:])
        i = pl.program_id(1)

        @pl.when((pl.program_id(0) == 0) & (i == 0))
        def _():
            for cp in local + remote:
                cp.start()

        @pl.when(i == 0)
        def _():
            dk_ref[...] = jnp.zeros_like(dk_ref)
            dv_ref[...] = jnp.zeros_like(dv_ref)

        in_a, _ = _head_masks()
        _, strict = _key_query_masks()
        before_m, prefix_m = _key_triangle("before"), _key_triangle("prefix")
        q, do2 = q_ref[...], do_ref[...]
        rt = (rt_ref[0], rt_ref[1])

        def products(j):
            z2 = lax.dot_general(k_ref[0, j], q, NT_DIMS, preferred_element_type=F32)
            da2 = lax.dot_general(v_ref[0, j], do2, NT_DIMS, preferred_element_type=F32)
            return [z2[a * BLK:(a + 1) * BLK] for a in range(2)], [da2[a * BLK:(a + 1) * BLK] for a in range(2)]

        def softplus_sums(z, masked):
            sp = [_softplus_of(x) for x in z]
            if masked:
                sp = [jnp.where(strict, x, 0.0) for x in sp]
            return sp, [_tri_dot(before_m, x, 2) for x in sp]

        def weight_grads(z, da, sp, pre, before, masked):
            w, g, pg, before_new = [], [], [], []
            for a in range(2):
                wa = jnp.exp(z[a] + (before[a] - rt[a]) + pre[a])
                if masked:
                    wa = jnp.where(strict, wa, 0.0)
                ga = wa * da[a]
                w.append(wa.astype(BF16))
                g.append(ga)
                pg.append(jnp.dot(prefix_m, ga.astype(BF16), preferred_element_type=F32))
                before_new.append(before[a] + pre[a][BLK - 1:BLK] + sp[a][BLK - 1:BLK])
            return jnp.concatenate(w, axis=0), g, pg, tuple(before_new)

        def dlogits(sp, g, pg, gbefore, masked):
            dz, gbefore_new = [], []
            for a in range(2):
                s_incl = gbefore[a] + pg[a]
                dza = (g[a] - s_incl) + jnp.exp(-sp[a]) * s_incl
                if masked:
                    dza = jnp.where(strict, dza, 0.0)
                dz.append(dza.astype(BF16))
                gbefore_new.append(s_incl[BLK - 1:BLK])
            return jnp.concatenate(dz, axis=0), tuple(gbefore_new)

        def accumulate(j, dzb, wb, dq):
            off = pl.multiple_of(j * BLK, BLK)
            dk_ref[pl.ds(off, BLK), :] += _fold_heads(jnp.dot(dzb, q, preferred_element_type=F32), in_a)
            dv_ref[pl.ds(off, BLK), :] += _fold_heads(jnp.dot(wb, do2, preferred_element_type=F32), in_a)
            return dq + jnp.dot(kt_ref[0, j], dzb, preferred_element_type=F32)

        def tiles(js, carry, masked=False):
            before, gbefore, dq = carry
            prods = [products(j) for j in js]
            sums = [softplus_sums(z, masked) for z, _ in prods]
            grads = []
            for (z, da), (sp, pre) in zip(prods, sums):
                wb, g, pg, before = weight_grads(z, da, sp, pre, before, masked)
                grads.append((wb, g, pg))
            for j, (sp, _), (wb, g, pg) in zip(js, sums, grads):
                dzb, gbefore = dlogits(sp, g, pg, gbefore, masked)
                dq = accumulate(j, dzb, wb, dq)
            return before, gbefore, dq

        zero = jnp.zeros((1, BLK), F32)
        carry = _loop_blocks(i, tiles, ((zero, zero), (zero, zero), jnp.zeros((LANES, BLK), F32)), groups=(2, 1))
        dq_ref[...] = tiles([i], carry, True)[2]

        @pl.when((pl.program_id(0) == p - 1) & (i == nq - 1))
        def _():
            for cp in remote:
                cp.wait_recv()
            for cp in remote:
                cp.wait_send()
            for cp in local:
                cp.wait()

    acc = pl.BlockSpec((t, LANES), lambda h, i: (0, h))
    hbm = pl.BlockSpec(memory_space=pltpu.HBM)
    return _pcall(
        body, name="sb_bwd", grid=(p, nq),
        out_shape=[jax.ShapeDtypeStruct((dg, t), F32)] + [jax.ShapeDtypeStruct((t, dg), F32)] * 2
        + [jax.ShapeDtypeStruct(pc.shape, pc.dtype) for pc in pieces],
        in_specs=[pl.BlockSpec((BLK, LANES), lambda h, i: (i, 3 * p + h)), _resident((nq, 2 * BLK, LANES)),
                  _resident((nq, LANES, 2 * BLK)), _resident((nq, 2 * BLK, LANES)),
                  pl.BlockSpec((BLK, LANES), lambda h, i: (i, h)), pl.BlockSpec((2, 1, BLK), lambda h, i: (h, 0, i))] + [hbm] * ns,
        out_specs=[pl.BlockSpec((LANES, BLK), lambda h, i: (h, i)), acc, acc] + [hbm] * ns,
        scratch_shapes=[pltpu.SemaphoreType.DMA((7 * ns,)), pltpu.SemaphoreType.DMA((7 * ns,)), pltpu.SemaphoreType.DMA((ns,))],
        compiler_params=_params(("arbitrary", "arbitrary")),
    )(qkv, k_st, k_t, v_st, do, rtot, *pieces)


def _tri_constants(nh, t):
    nb = t // LANES
    r = nh * nb
    li = np.arange(LANES)
    tri_in = (li[:, None] <= li[None, :])
    ri = np.arange(r)
    same = (ri[:, None] // nb) == (ri[None, :] // nb)
    blk = same & (ri[None, :] < ri[:, None])
    blk_rev = same & (ri[None, :] > ri[:, None])
    head_rows = (np.arange(max(8, nh))[:, None] == (ri[None, :] // nb))
    as_bf16 = lambda a: jnp.asarray(a.astype(np.float32), BF16)
    return as_bf16(tri_in), as_bf16(blk), as_bf16(tri_in.T), as_bf16(blk_rev), as_bf16(head_rows)


def kernel(x, c, w_ada, b_ada, g_attn, w_in, b_fgate, g_out_fox, g_out_sb, w_out, g_mlp, w_up, conv_w, conv_b, w_down, g_final, loss_target, m_w_ada, m_b_ada, m_g_attn, m_w_in, m_b_fgate, m_g_out_fox, m_g_out_sb, m_w_out, m_g_mlp, m_w_up, m_conv_w, m_conv_b, m_w_down, m_g_final, v_w_ada, v_b_ada, v_g_attn, v_w_in, v_b_fgate, v_g_out_fox, v_g_out_sb, v_w_out, v_g_mlp, v_w_up, v_conv_w, v_conv_b, v_w_down, v_g_final):
    t, d = x.shape[1], x.shape[2]
    dg = d // 2
    nh = dg // HEAD_DIM
    n_in = 6 * dg + nh
    dff = w_down.shape[1] * 4
    dfp = -(-dff // 256) * 256
    cf = 256
    tm = _tile(t, (512, 256, 128))
    nq = t // BLK
    xi, yi, ci = lax.axis_index("x"), lax.axis_index("y"), lax.axis_index("c")
    shard = 2 * xi + yi
    me = 4 * xi + 2 * yi + ci

    x2d, tg2d = x[0], loss_target[0]

    c_all = _all_gather8(jnp.pad(c, ((0, 7), (0, 0)))).reshape(8, 8, d)[:, 0, :]
    ada_cols = w_ada.shape[2]
    b_shard = lax.dynamic_slice(b_ada, (0, shard * ada_cols), (1, ada_cols))
    sc_all, mod_shard = _ada_fwd(c_all, w_ada[0], b_shard)
    mod_all = _all_gather8(mod_shard).reshape(4, 2, 8, ada_cols)
    mod_me = lax.dynamic_index_in_dim(mod_all[:, 0], me, axis=1, keepdims=False)
    mod8 = jnp.pad(mod_me.reshape(6, d), ((0, 2), (0, 0)))

    lane_pad = lambda a: jnp.pad(a, ((0, 0),) * (a.ndim - 1) + ((0, -a.shape[-1] % LANES),))
    (g_in,) = _gather_xy([lane_pad(w_in[0].astype(BF16))])
    later_shards = [w_out[0].astype(BF16), lane_pad(w_up[0].astype(BF16)), w_down[0].astype(BF16), lane_pad(conv_w[0])]
    w_in_full = jnp.transpose(g_in[:, :, :n_in // 4], (1, 0, 2)).reshape(d, n_in)
    w_qkv = w_in_full[:, :6 * dg]
    w_f = jnp.pad(w_in_full[:, 6 * dg:].T, ((0, LANES - nh), (0, 0)))

    qkv, fl, h1 = _in_proj_fwd(x2d, mod8, g_attn, w_qkv, w_f, tm)
    tri_in, tri_blk, tri_in_rev, tri_blk_rev, head_rows = _tri_constants(nh, t)
    fl2d = fl[:nh].reshape(nh * t // LANES, LANES)
    b_rows = jnp.repeat(b_fgate[0], t // LANES)[:, None]
    f2d = _fgate_fwd(fl2d, b_rows, tri_in, tri_blk)
    frow = f2d.reshape(nh, 1, t)
    pairs = nh // 2
    qx, kx, kx_t, vf_st, vf_t, ks_st, ks_t, vs_st, vs_t = _attn_operands(qkv, frow, dg)
    o_fox, lse, g_out, g_up, g_down, g_cw = _fox_fwd(qx, kx, vf_t, dg, later_shards)
    g_up, g_cw = g_up[:, :, :dff // 2], g_cw[:, :, :dff // 2]
    w_out_full = g_out.reshape(2 * dg, d)
    w_up_full = jnp.transpose(g_up, (1, 0, 2)).reshape(d, 2 * dff)
    padc = ((0, 0), (0, dfp - dff))
    wg, wv = jnp.pad(w_up_full[:, :dff], padc), jnp.pad(w_up_full[:, dff:], padc)
    wd = jnp.pad(g_down.reshape(dff, d), ((0, dfp - dff), (0, 0)))
    cw_full = jnp.transpose(g_cw, (1, 0, 2)).reshape(3, 2 * dff)
    cw4 = jnp.concatenate([cw_full, conv_b], axis=0)
    cwg = jnp.pad(cw4[:, :dff], ((0, 4), (0, dfp - dff)))
    cwv = jnp.pad(cw4[:, dff:], ((0, 4), (0, dfp - dff)))
    o_sb, rtot = _sb_fwd(qkv, ks_st, vs_t, dg)
    li = np.arange(dg)
    bd = jnp.asarray((li[:, None] // HEAD_DIM == li[None, :] // HEAD_DIM).astype(np.float32), BF16)
    hsel = jnp.asarray((np.arange(LANES)[:, None] == li[None, :] // HEAD_DIM).astype(np.float32), BF16)
    x2, mix, mix_t = _attn_out_fwd(x2d, o_fox, o_sb, g_out_fox, g_out_sb, w_out_full, mod8, bd, tm)
    g_final2 = g_final[None, :]
    dx3, h2, part_f = _ffn_fwd(x2, tg2d, mod8, g_mlp, g_final2, wg, wv, cwg, cwv, wd, tm, cf)

    tm_ffn_bwd = _tile(t, (256, 128))
    dx2, dupg, dupv, act, dxg3, part_b, pcg, pcv = _ffn_bwd(x2, dx3, mod8, g_mlp, wg, wv, cwg, cwv, wd, tm_ffn_bwd, cf)
    do_fox, do_sb, delta, dxg2, part_o = _attn_out_bwd(dx2, mix, o_fox, o_sb, g_out_fox, g_out_sb, w_out_full, mod8, bd, hsel, tm)
    drow = delta[:nh].reshape(nh, 1, t)

    def col_pieces(g):
        r, cc = g.shape
        return jnp.transpose(g.reshape(2, r // 2, 4, cc // 4), (2, 0, 1, 3)).reshape(8, r // 2, cc // 4)

    def row_pieces(g):
        r, cc = g.shape
        return g.reshape(8, r // 8, cc)

    gw_out = _matmul_tn(mix_t, dxg2, "grad_w_out")
    gw_upg = _matmul_tn(h2, dupg, "grad_w_up_gate")
    gw_upv = _matmul_tn(h2, dupv, "grad_w_up_val")
    gw_up = jnp.concatenate([gw_upg[:, :dff], gw_upv[:, :dff]], axis=1)
    gw_down = _matmul_tn(act, dxg3, "grad_w_down")[:dff]
    early = (row_pieces(gw_out), lane_pad(col_pieces(gw_up)), row_pieces(gw_down))
    early = [_to_bf16(pc, "pieces_bf16_" + nm) for pc, nm in zip(early, ("w_out", "w_up", "w_down"))]

    dq_s, dk_s, dv_s, *recv_early = _sb_bwd(qkv, ks_st, ks_t, vs_st, do_sb, rtot, dg, early)
    halves_early = [_sum_leading(rv, nm) for rv, nm in zip(recv_early, ("sum_w_out", "sum_w_up", "sum_w_down"))]
    dq_f, dft, dk_f, dv_f, dkx, *swapped_early = _fox_bwd(qx, kx, kx_t, vf_st, do_fox, drow, lse, dg, halves_early)
    f2d_shape = (nh * t // LANES, LANES)
    dfs = jnp.transpose(dkx.reshape(t, pairs, LANES)[:, :, :2], (1, 2, 0))
    dfl2d, gb8 = _fgate_bwd(fl2d, b_rows, dft.reshape(f2d_shape), dfs.reshape(f2d_shape), tri_in_rev, tri_blk_rev, head_rows)
    dfl = jnp.pad(dfl2d.reshape(nh, t), ((0, LANES - nh), (0, 0)))
    grad_x, dproj, dflb, part_i = _in_proj_bwd([dq_f, dk_f, dv_f, dq_s, dk_s, dv_s], dfl, w_qkv, w_f, x2d, dx2, mod8, g_attn, tm)

    gw_qkv = _matmul_tn(h1, dproj, "grad_w_qkv")
    gw_f = _matmul_tn(h1, dflb, "grad_w_f")
    gw_in = jnp.concatenate([gw_qkv, gw_f[:, :nh]], axis=1)

    sf = _sum_leading(part_f, "sum_part_ffn_fwd")
    sb_ = _sum_leading(part_b, "sum_part_ffn_bwd")
    so = _sum_leading(part_o, "sum_part_attn_out")
    si = _sum_leading(part_i, "sum_part_in_proj")
    scg = _sum_leading(pcg, "sum_part_conv_gate")
    scv = _sum_leading(pcv, "sum_part_conv_val")
    gb_f = gb8[:nh, 0]
    dmod = jnp.concatenate([si[0], si[1], so[0], sb_[0], sb_[1], sf[1]])
    g_conv_w = jnp.concatenate([scg[0:3, :dff], scv[0:3, :dff]], axis=1).reshape(-1)
    g_conv_b = jnp.concatenate([scg[3, :dff], scv[3, :dff]])
    loss_part = jnp.sum(sf[2])
    fields = [dmod, si[2], gb_f, so[1, :dg], so[1, dg:], sb_[2], g_conv_b, sf[0], g_conv_w, loss_part[None]]
    sizes = [int(f.shape[0]) for f in fields]
    n_pack = sum(sizes)
    lanes_pack = -(-n_pack // (8 * LANES)) * LANES
    pack = jnp.pad(jnp.concatenate(fields), (0, 8 * lanes_pack - n_pack)).reshape(8, lanes_pack)
    gathered = _all_gather8(pack)
    tot = _sum_leading(gathered.reshape(8, 8, lanes_pack), "sum_pack").reshape(-1)
    offs = np.concatenate([[0], np.cumsum(sizes)])
    take = lambda k: tot[int(offs[k]):int(offs[k + 1])]
    g_b_ada, g_g_attn, g_b_fgate, g_g_fox, g_g_sb, g_g_mlp, g_cb, g_g_final, g_cw_full, loss_v = [take(k) for k in range(10)]
    loss = loss_v[0]
    dmod_all = gathered.reshape(8, 8 * lanes_pack)[:, :6 * d]
    dmod_cols = lax.dynamic_slice(dmod_all, (0, shard * ada_cols), (8, ada_cols))
    g_w_ada = _ada_bwd(sc_all.T, dmod_cols)

    (recv_in,) = _scatter8([_to_bf16(lane_pad(col_pieces(gw_in)), "pieces_bf16_w_in")])
    (swapped_in,) = _swap_halves([_sum_leading(recv_in, "sum_w_in")])
    swapped = [swapped_in] + swapped_early
    shard_cols = (n_in // 4, d, dff // 2, d)
    g_w_in, g_w_out, g_w_up, g_w_down = [s.reshape(2 * s.shape[1], s.shape[2])[:, :cc] for s, cc in zip(swapped, shard_cols)]
    g_conv_w_shard = lax.dynamic_slice(g_cw_full.reshape(3, 2 * dff), (0, shard * (dff // 2)), (3, dff // 2))

    grads, deltas, new_m, new_v = {}, {}, {}, {}

    def step(name, w, g, m, v):
        shape = w.shape
        as2d = lambda a: a.reshape(-1, shape[-1])
        dl, nm, nv = _adamw(as2d(w), as2d(g), as2d(m), as2d(v), "adamw_" + name)
        grads[name], deltas[name], new_m[name], new_v[name] = g.reshape(shape), dl.reshape(shape), nm.reshape(shape), nv.reshape(shape)

    step("w_ada", w_ada, g_w_ada, m_w_ada, v_w_ada)
    step("w_in", w_in, g_w_in, m_w_in, v_w_in)
    step("w_out", w_out, g_w_out, m_w_out, v_w_out)
    step("w_up", w_up, g_w_up, m_w_up, v_w_up)
    step("conv_w", conv_w, g_conv_w_shard, m_conv_w, v_conv_w)
    step("w_down", w_down, g_w_down, m_w_down, v_w_down)

    small = [("b_ada", b_ada, g_b_ada, m_b_ada, v_b_ada), ("g_attn", g_attn, g_g_attn, m_g_attn, v_g_attn),
             ("b_fgate", b_fgate, g_b_fgate, m_b_fgate, v_b_fgate), ("g_out_fox", g_out_fox, g_g_fox, m_g_out_fox, v_g_out_fox),
             ("g_out_sb", g_out_sb, g_g_sb, m_g_out_sb, v_g_out_sb), ("g_mlp", g_mlp, g_g_mlp, m_g_mlp, v_g_mlp),
             ("conv_b", conv_b, g_cb, m_conv_b, v_conv_b), ("g_final", g_final, g_g_final, m_g_final, v_g_final)]
    ssz = [int(np.prod(s[1].shape)) for s in small]
    n_small = sum(ssz)
    lanes_small = -(-n_small // (8 * LANES)) * LANES
    packs = [jnp.pad(jnp.concatenate([s[k].reshape(-1) for s in small]), (0, 8 * lanes_small - n_small)).reshape(8, lanes_small)
             for k in (1, 2, 3, 4)]
    dl_s, nm_s, nv_s = _adamw(*packs, "adamw_small")
    so_ = np.concatenate([[0], np.cumsum(ssz)])
    for k, s in enumerate(small):
        cut = lambda a: a.reshape(-1)[int(so_[k]):int(so_[k + 1])].reshape(s[1].shape)
        grads[s[0]], deltas[s[0]], new_m[s[0]], new_v[s[0]] = s[2].reshape(s[1].shape), cut(dl_s), cut(nm_s), cut(nv_s)

    order = ["w_ada", "b_ada", "g_attn", "w_in", "b_fgate", "g_out_fox", "g_out_sb", "w_out", "g_mlp", "w_up",
             "conv_w", "conv_b", "w_down", "g_final"]
    return (loss, grad_x[None], *[grads[n] for n in order], *[deltas[n] for n in order],
            *[new_m[n] for n in order], *[new_v[n] for n in order])
```

```python
import functools

import numpy as np
import jax
import jax.numpy as jnp
from jax import lax
from jax.experimental import pallas as pl
from jax.experimental.pallas import tpu as pltpu

F32 = jnp.float32
BF16 = jnp.bfloat16
MESH = pl.DeviceIdType.MESH

HEAD_DIM = 64
LANES = 128
EPS = 1e-6
NEG = -1e30
ADAM_LR, ADAM_B1, ADAM_B2, ADAM_EPS, ADAM_WD, ADAM_STEP = 0.001, 0.9, 0.999, 1e-08, 0.01, 10
V7X_VMEM_BYTES = 64 * 1024 * 1024
VMEM_LIMIT = V7X_VMEM_BYTES - 12 * 1024 * 1024
NT_DIMS = (((1,), (1,)), ((), ()))


def _pcall(body, **kw):
    return pl.pallas_call(body, **kw)


def _params(sem=None, **kw):
    return pltpu.CompilerParams(dimension_semantics=sem, vmem_limit_bytes=VMEM_LIMIT, **kw)


def _split_dot(x, m, passes):
    acc = None
    for _ in range(passes):
        part = x.astype(BF16)
        d = jnp.dot(part, m, preferred_element_type=F32)
        acc = d if acc is None else acc + d
        x = x - part.astype(F32)
    return acc


def _tile(n, candidates):
    for t in candidates:
        if n % t == 0:
            return t
    return n


def _rows_tile(rows, row_bytes, budget=2 * 1024 * 1024):
    best = None
    for t in range(8, rows + 1, 8):
        if rows % t == 0 and t * row_bytes <= budget:
            best = t
    return best if best is not None else rows


def _all_gather8(v):
    m_per, n = v.shape

    def body(x_ref, out_ref, send_sems, recv_sems, local_sem):
        x, y, c = lax.axis_index("x"), lax.axis_index("y"), lax.axis_index("c")
        me, sibling = (x, y, c), (x, y, 1 - c)
        chips = [(1 - x, y), (x, 1 - y), (1 - x, 1 - y)]

        def rows(px, py, pc):
            return out_ref.at[pl.ds((4 * px + 2 * py + pc) * m_per, m_per), :]

        def copy(k, block, to, src=None):
            return pltpu.make_async_remote_copy(
                src_ref=rows(*block) if src is None else src, dst_ref=rows(*block),
                send_sem=send_sems.at[k], recv_sem=recv_sems.at[k], device_id=to, device_id_type=MESH)

        mine = pltpu.make_async_copy(x_ref, rows(*me), local_sem)
        mine.start()
        first = [copy(0, me, sibling, src=x_ref)]
        first += [copy(1 + j, me, (*chip, c), src=x_ref) for j, chip in enumerate(chips)]
        for cp in first:
            cp.start()
        passed = [copy(4 + j, (*chip, c), sibling) for j, chip in enumerate(chips)]
        for j, chip in enumerate(chips):
            copy(1 + j, (*chip, c), me).wait_recv()
            passed[j].start()
        copy(0, sibling, me).wait_recv()
        for j, chip in enumerate(chips):
            copy(4 + j, (*chip, 1 - c), me).wait_recv()
        for cp in first + passed:
            cp.wait_send()
        mine.wait()

    return _pcall(
        body, name="all_gather8",
        out_shape=jax.ShapeDtypeStruct((8 * m_per, n), v.dtype),
        in_specs=[pl.BlockSpec(memory_space=pltpu.VMEM)],
        out_specs=pl.BlockSpec(memory_space=pltpu.VMEM),
        scratch_shapes=[pltpu.SemaphoreType.DMA((7,)), pltpu.SemaphoreType.DMA((7,)), pltpu.SemaphoreType.DMA],
        compiler_params=pltpu.CompilerParams(vmem_limit_bytes=VMEM_LIMIT),
    )(v)


def _gather_xy(shards):
    n = len(shards)

    def body(*refs):
        ins, outs = refs[:n], refs[n:2 * n]
        send_sems, recv_sems, local_sems = refs[2 * n:]
        x, y, c = lax.axis_index("x"), lax.axis_index("y"), lax.axis_index("c")
        chips = [(1 - x, y), (x, 1 - y), (1 - x, 1 - y)]
        mine = 2 * x + y
        local, remote = [], []
        for w in range(n):
            cp = pltpu.make_async_copy(ins[w], outs[w].at[mine], local_sems.at[w])
            cp.start()
            local.append(cp)
            for k, (px, py) in enumerate(chips):
                cp = pltpu.make_async_remote_copy(
                    src_ref=ins[w], dst_ref=outs[w].at[mine], send_sem=send_sems.at[3 * w + k],
                    recv_sem=recv_sems.at[3 * w + k], device_id=(px, py, c), device_id_type=MESH)
                cp.start()
                remote.append(cp)
        for cp in remote:
            cp.wait_recv()
        for cp in remote:
            cp.wait_send()
        for cp in local:
            cp.wait()

    hbm = pl.BlockSpec(memory_space=pltpu.HBM)
    return _pcall(
        body, name="gather_xy",
        out_shape=[jax.ShapeDtypeStruct((4,) + s.shape, s.dtype) for s in shards],
        in_specs=[hbm] * n, out_specs=[hbm] * n,
        scratch_shapes=[pltpu.SemaphoreType.DMA((3 * n,)), pltpu.SemaphoreType.DMA((3 * n,)),
                        pltpu.SemaphoreType.DMA((n,))],
        compiler_params=pltpu.CompilerParams(vmem_limit_bytes=VMEM_LIMIT),
    )(*shards)


def _scatter8(pieces):
    n = len(pieces)

    def body(*refs):
        ins, outs = refs[:n], refs[n:2 * n]
        send_sems, recv_sems, local_sems = refs[2 * n:]
        x, y, c = lax.axis_index("x"), lax.axis_index("y"), lax.axis_index("c")
        me = 4 * x + 2 * y + c
        local, remote = [], []
        for w in range(n):
            cp = pltpu.make_async_copy(ins[w].at[me], outs[w].at[me], local_sems.at[w])
            cp.start()
            local.append(cp)
            for f in range(1, 8):
                px = 1 - x if f & 4 else x
                py = 1 - y if f & 2 else y
                pc = 1 - c if f & 1 else c
                cp = pltpu.make_async_remote_copy(
                    src_ref=ins[w].at[4 * px + 2 * py + pc], dst_ref=outs[w].at[me],
                    send_sem=send_sems.at[7 * w + f - 1], recv_sem=recv_sems.at[7 * w + f - 1],
                    device_id=(px, py, pc), device_id_type=MESH)
                cp.start()
                remote.append(cp)
        for cp in remote:
            cp.wait_recv()
        for cp in remote:
            cp.wait_send()
        for cp in local:
            cp.wait()

    hbm = pl.BlockSpec(memory_space=pltpu.HBM)
    return _pcall(
        body, name="scatter8",
        out_shape=[jax.ShapeDtypeStruct(p.shape, p.dtype) for p in pieces],
        in_specs=[hbm] * n, out_specs=[hbm] * n,
        scratch_shapes=[pltpu.SemaphoreType.DMA((7 * n,)), pltpu.SemaphoreType.DMA((7 * n,)),
                        pltpu.SemaphoreType.DMA((n,))],
        compiler_params=pltpu.CompilerParams(vmem_limit_bytes=VMEM_LIMIT),
    )(*pieces)


def _swap_halves(halves):
    n = len(halves)
    chunks = 8
    n_chunks = [max(k for k in (chunks, 4, 2, 1) if h.shape[0] % (8 * k) == 0) for h in halves]

    def body(*refs):
        ins, outs = refs[:n], refs[n:2 * n]
        send_sems, recv_sems, local_sems = refs[2 * n:]
        x, y, c = lax.axis_index("x"), lax.axis_index("y"), lax.axis_index("c")
        local, remote = [], []
        for w in range(n):
            cp = pltpu.make_async_copy(ins[w], outs[w].at[c], local_sems.at[w])
            cp.start()
            local.append(cp)
            rows = ins[w].shape[0] // n_chunks[w]
            for k in range(n_chunks[w]):
                cp = pltpu.make_async_remote_copy(
                    src_ref=ins[w].at[pl.ds(k * rows, rows)], dst_ref=outs[w].at[c, pl.ds(k * rows, rows)],
                    send_sem=send_sems.at[chunks * w + k], recv_sem=recv_sems.at[chunks * w + k],
                    device_id=(x, y, 1 - c), device_id_type=MESH)
                cp.start()
                remote.append(cp)
        for cp in remote:
            cp.wait_recv()
        for cp in remote:
            cp.wait_send()
        for cp in local:
            cp.wait()

    hbm = pl.BlockSpec(memory_space=pltpu.HBM)
    return _pcall(
        body, name="swap_halves",
        out_shape=[jax.ShapeDtypeStruct((2,) + h.shape, h.dtype) for h in halves],
        in_specs=[hbm] * n, out_specs=[hbm] * n,
        scratch_shapes=[pltpu.SemaphoreType.DMA((chunks * n,)), pltpu.SemaphoreType.DMA((chunks * n,)),
                        pltpu.SemaphoreType.DMA((n,))],
        compiler_params=pltpu.CompilerParams(vmem_limit_bytes=VMEM_LIMIT),
    )(*halves)


def _sum_leading(a, name):
    n, r, c = a.shape
    tr = _rows_tile(r, n * c * 4, budget=6 * 1024 * 1024)
    if a.dtype == BF16 and tr % 16:
        tr = r

    def body(a_ref, o_ref):
        acc = a_ref[0].astype(F32)
        for k in range(1, n):
            acc = acc + a_ref[k].astype(F32)
        o_ref[...] = acc

    return _pcall(
        body, name=name, grid=(r // tr,),
        out_shape=jax.ShapeDtypeStruct((r, c), F32),
        in_specs=[pl.BlockSpec((n, tr, c), lambda i: (0, i, 0))],
        out_specs=pl.BlockSpec((tr, c), lambda i: (i, 0)),
        compiler_params=_params(("arbitrary",)),
    )(a)


def _to_bf16(a, name):
    n, r, c = a.shape

    def body(a_ref, o_ref):
        o_ref[...] = a_ref[...].astype(BF16)

    spec = pl.BlockSpec((1, r, c), lambda i: (i, 0, 0))
    return _pcall(
        body, name=name, grid=(n,), out_shape=jax.ShapeDtypeStruct(a.shape, BF16),
        in_specs=[spec], out_specs=spec, compiler_params=_params(("arbitrary",)),
    )(a)


def _adamw(w, g, m, v, name):
    r, c = w.shape
    tr = _rows_tile(r, c * 4, budget=1024 * 1024)
    c1 = 1.0 - ADAM_B1 ** ADAM_STEP
    c2 = 1.0 - ADAM_B2 ** ADAM_STEP

    def body(w_ref, g_ref, m_ref, v_ref, d_ref, nm_ref, nv_ref):
        gg = g_ref[...]
        nm = ADAM_B1 * m_ref[...] + (1.0 - ADAM_B1) * gg
        nv = ADAM_B2 * v_ref[...] + (1.0 - ADAM_B2) * (gg * gg)
        m_hat = nm / c1
        v_hat = nv / c2
        d_ref[...] = -ADAM_LR * (m_hat / (jnp.sqrt(v_hat) + ADAM_EPS) + ADAM_WD * w_ref[...])
        nm_ref[...] = nm
        nv_ref[...] = nv

    spec = pl.BlockSpec((tr, c), lambda i: (i, 0))
    return _pcall(
        body, name=name, grid=(r // tr,),
        out_shape=[jax.ShapeDtypeStruct((r, c), F32)] * 3,
        in_specs=[spec] * 4, out_specs=[spec] * 3,
        compiler_params=_params(("arbitrary",)),
    )(w, g, m, v)


def _ada_fwd(c_all, w_shard, b_shard):
    nb, d = c_all.shape
    cols = w_shard.shape[1]

    def body(c_ref, w_ref, b_ref, sc_ref, mod_ref):
        cv = c_ref[...]
        sc = cv * jax.nn.sigmoid(cv)
        sc_ref[...] = sc
        mod_ref[...] = jnp.dot(sc.astype(BF16), w_ref[...].astype(BF16), preferred_element_type=F32) + b_ref[...]

    return _pcall(
        body, name="ada_fwd",
        out_shape=[jax.ShapeDtypeStruct((nb, d), F32), jax.ShapeDtypeStruct((nb, cols), F32)],
        compiler_params=pltpu.CompilerParams(vmem_limit_bytes=VMEM_LIMIT),
    )(c_all, w_shard, b_shard)


def _ada_bwd(sc_t, dmod_cols):
    d, nb = sc_t.shape
    cols = dmod_cols.shape[1]
    tr = _rows_tile(d, cols * 4, budget=1024 * 1024)

    def body(s_ref, m_ref, o_ref):
        s = s_ref[...]
        m = m_ref[...]
        acc = s[:, 0:1] * m[0:1, :]
        for b in range(1, nb):
            acc = acc + s[:, b:b + 1] * m[b:b + 1, :]
        o_ref[...] = acc

    return _pcall(
        body, name="ada_bwd", grid=(d // tr,),
        out_shape=jax.ShapeDtypeStruct((d, cols), F32),
        in_specs=[pl.BlockSpec((tr, nb), lambda i: (i, 0)), pl.BlockSpec((nb, cols), lambda i: (0, 0))],
        out_specs=pl.BlockSpec((tr, cols), lambda i: (i, 0)),
        compiler_params=_params(("arbitrary",)),
    )(sc_t, dmod_cols)


def _log_sigmoid(x):
    return jnp.minimum(x, 0.0) - jnp.log1p(jnp.exp(-jnp.abs(x)))


def _fgate_fwd(fl2d, b_rows, tri_in, tri_blk):
    r = fl2d.shape[0]

    def body(x_ref, b_ref, u_ref, l_ref, f_ref):
        lf = _log_sigmoid(x_ref[...] + b_ref[...])
        c1 = _split_dot(lf, u_ref[...], 3)
        tot = jnp.broadcast_to(c1[:, LANES - 1:LANES], (r, LANES))
        acc = None
        for _ in range(3):
            part = tot.astype(BF16)
            dd = jnp.dot(l_ref[...], part, preferred_element_type=F32)
            acc = dd if acc is None else acc + dd
            tot = tot - part.astype(F32)
        f_ref[...] = c1 + acc

    return _pcall(
        body, name="fgate_fwd", out_shape=jax.ShapeDtypeStruct((r, LANES), F32),
        compiler_params=pltpu.CompilerParams(vmem_limit_bytes=VMEM_LIMIT),
    )(fl2d, b_rows, tri_in, tri_blk)


def _fgate_bwd(fl2d, b_rows, df_query, df_key, tri_in_rev, tri_blk_rev, head_rows):
    r = fl2d.shape[0]
    nhp = head_rows.shape[0]

    def body(x_ref, b_ref, dq_ref, dk_ref, u_ref, l_ref, hr_ref, o_ref, gb_ref):
        c1 = _split_dot(dq_ref[...] + dk_ref[...], u_ref[...], 3)
        tot = jnp.broadcast_to(c1[:, 0:1], (r, LANES))
        acc = None
        for _ in range(3):
            part = tot.astype(BF16)
            dd = jnp.dot(l_ref[...], part, preferred_element_type=F32)
            acc = dd if acc is None else acc + dd
            tot = tot - part.astype(F32)
        x = x_ref[...] + b_ref[...]
        e = jnp.exp(-jnp.abs(x))
        dfl = (c1 + acc) * (jnp.where(x >= 0, e, 1.0) / (1.0 + e))
        o_ref[...] = dfl
        rs = jnp.broadcast_to(jnp.sum(dfl, axis=1, keepdims=True), (r, LANES))
        gb = None
        for _ in range(3):
            part = rs.astype(BF16)
            dd = jnp.dot(hr_ref[...], part, preferred_element_type=F32)
            gb = dd if gb is None else gb + dd
            rs = rs - part.astype(F32)
        gb_ref[...] = gb

    return _pcall(
        body, name="fgate_bwd",
        out_shape=[jax.ShapeDtypeStruct((r, LANES), F32), jax.ShapeDtypeStruct((nhp, LANES), F32)],
        compiler_params=pltpu.CompilerParams(vmem_limit_bytes=VMEM_LIMIT),
    )(fl2d, b_rows, df_query, df_key, tri_in_rev, tri_blk_rev, head_rows)


def _norm_mod(x, g, scale, shift):
    r = lax.rsqrt(jnp.mean(x * x, axis=-1, keepdims=True) + EPS)
    return (x * r * g) * (1.0 + scale) + shift


def _norm_mod_bwd(x, dh, g, scale):
    r = lax.rsqrt(jnp.mean(x * x, axis=-1, keepdims=True) + EPS)
    xn = x * r
    dshift = jnp.sum(dh, axis=0, keepdims=True)
    dscale = jnp.sum(dh * (xn * g), axis=0, keepdims=True)
    dxn_g = dh * (1.0 + scale)
    dg = jnp.sum(dxn_g * xn, axis=0, keepdims=True)
    dxn = dxn_g * g
    dx = r * (dxn - xn * jnp.mean(dxn * xn, axis=-1, keepdims=True))
    return dx, dshift, dscale, dg


def _in_proj_fwd(x, mod8, g_attn, w_qkv, w_f, tm):
    t, d = x.shape
    dg = w_qkv.shape[1] // 6

    def body(x_ref, mod_ref, g_ref, w_ref, wf_ref, qkv_ref, fl_ref, h1_ref):
        h = _norm_mod(x_ref[...], g_ref[...], mod_ref[1:2, :], mod_ref[0:1, :]).astype(BF16)
        h1_ref[...] = _transposed(h)
        fl_ref[...] = lax.dot_general(wf_ref[...], h, NT_DIMS, preferred_element_type=F32)
        for k in range(6):
            cols = slice(k * dg, (k + 1) * dg)
            y = jnp.dot(h, w_ref[:, cols], preferred_element_type=F32)
            qkv_ref[:, cols] = (y * HEAD_DIM ** -0.5 if k in (0, 3) else y).astype(BF16)

    once = lambda a: pl.BlockSpec(a.shape, lambda i: (0,) * a.ndim, pipeline_mode=pl.Buffered(1))
    return _pcall(
        body, name="in_proj_fwd", grid=(t // tm,),
        out_shape=[jax.ShapeDtypeStruct((t, 6 * dg), BF16), jax.ShapeDtypeStruct((LANES, t), F32),
                   jax.ShapeDtypeStruct((d, t), BF16)],
        in_specs=[pl.BlockSpec((tm, d), lambda i: (i, 0)), pl.BlockSpec((8, d), lambda i: (0, 0)),
                  pl.BlockSpec((1, d), lambda i: (0, 0)), once(w_qkv), once(w_f)],
        out_specs=[pl.BlockSpec((tm, 6 * dg), lambda i: (i, 0)), pl.BlockSpec((LANES, tm), lambda i: (0, i)),
                   pl.BlockSpec((d, tm), lambda i: (0, i))],
        compiler_params=_params(("arbitrary",)),
    )(x, mod8, g_attn, w_qkv, w_f)


def _head_rstd(o, bd):
    return lax.rsqrt(_split_dot(o * o, bd, 3) * (1.0 / HEAD_DIM) + EPS)


def _attn_out_fwd(x, o_fox, o_sb, g_fox, g_sb, w_out, mod8, bd, tm):
    t, d = x.shape
    dg = o_fox.shape[0]

    def body(x_ref, of_ref, os_ref, gf_ref, gs_ref, w_ref, mod_ref, bd_ref, x2_ref, mix_ref, mixt_ref):
        of, osb = of_ref[...].T, os_ref[...].T
        mf = (of * _head_rstd(of, bd_ref[...]) * gf_ref[...]).astype(BF16)
        ms = (osb * _head_rstd(osb, bd_ref[...]) * gs_ref[...]).astype(BF16)
        mix_ref[:, :dg] = mf
        mix_ref[:, dg:] = ms
        mixt_ref[:dg, :] = _transposed(mf)
        mixt_ref[dg:, :] = _transposed(ms)
        y = jnp.dot(mf, w_ref[:dg, :], preferred_element_type=F32) + jnp.dot(ms, w_ref[dg:, :], preferred_element_type=F32)
        x2_ref[...] = x_ref[...] + mod_ref[2:3, :] * y

    row = lambda w: pl.BlockSpec((tm, w), lambda i: (i, 0))
    full = lambda a: pl.BlockSpec(a.shape, lambda i: (0,) * a.ndim)
    return _pcall(
        body, name="attn_out_fwd", grid=(t // tm,),
        out_shape=[jax.ShapeDtypeStruct((t, d), F32), jax.ShapeDtypeStruct((t, 2 * dg), BF16),
                   jax.ShapeDtypeStruct((2 * dg, t), BF16)],
        in_specs=[row(d), pl.BlockSpec((dg, tm), lambda i: (0, i)), pl.BlockSpec((dg, tm), lambda i: (0, i)),
                  full(g_fox), full(g_sb), full(w_out), full(mod8), full(bd)],
        out_specs=[row(d), row(2 * dg), pl.BlockSpec((2 * dg, tm), lambda i: (0, i))],
        compiler_params=_params(("arbitrary",)),
    )(x, o_fox, o_sb, g_fox, g_sb, w_out, mod8, bd)


def _attn_out_bwd(dx2, mix, o_fox, o_sb, g_fox, g_sb, w_out, mod8, bd, hsel, tm):
    t, d = dx2.shape
    dg = o_fox.shape[0]

    def body(dx_ref, mix_ref, of_ref, os_ref, gf_ref, gs_ref, w_ref, mod_ref, bd_ref, hs_ref,
             dof_ref, dos_ref, dlt_ref, dxg_ref, part_ref):
        dx = dx_ref[...]
        gate = mod_ref[2:3, :]
        dxg = (dx * gate).astype(BF16)
        dxg_ref[...] = dxg
        mixv = mix_ref[...]
        y = jnp.dot(mixv[:, :dg], w_ref[:dg, :], preferred_element_type=F32)
        y = y + jnp.dot(mixv[:, dg:], w_ref[dg:, :], preferred_element_type=F32)
        part_ref[0] = jnp.zeros((8, d), F32)
        part_ref[0, 0:1, :] = jnp.sum(dx * y, axis=0, keepdims=True)
        for grp, (o_ref, g_ref, do_ref) in enumerate(((of_ref, gf_ref, dof_ref), (os_ref, gs_ref, dos_ref))):
            dmix = lax.dot_general(dxg, w_ref[grp * dg:(grp + 1) * dg, :], NT_DIMS, preferred_element_type=F32)
            o = o_ref[...].T
            r = _head_rstd(o, bd_ref[...])
            n = o * r
            part_ref[0, 1:2, grp * dg:(grp + 1) * dg] = jnp.sum(dmix * n, axis=0, keepdims=True)
            dn = dmix * g_ref[...]
            mh = _split_dot(dn * n, bd_ref[...], 3) * (1.0 / HEAD_DIM)
            do = r * (dn - n * mh)
            do_ref[...] = do.astype(BF16)
            if grp == 0:
                prod, dlt = do * o, None
                for _ in range(3):
                    part = prod.astype(BF16)
                    term = lax.dot_general(hs_ref[...], part, NT_DIMS, preferred_element_type=F32)
                    dlt = term if dlt is None else dlt + term
                    prod = prod - part.astype(F32)
                dlt_ref[...] = dlt

    row = lambda w: pl.BlockSpec((tm, w), lambda i: (i, 0))
    full = lambda a: pl.BlockSpec(a.shape, lambda i: (0,) * a.ndim)
    nt = t // tm
    return _pcall(
        body, name="attn_out_bwd", grid=(nt,),
        out_shape=[jax.ShapeDtypeStruct((t, dg), BF16), jax.ShapeDtypeStruct((t, dg), BF16),
                   jax.ShapeDtypeStruct((LANES, t), F32), jax.ShapeDtypeStruct((t, d), BF16),
                   jax.ShapeDtypeStruct((nt, 8, d), F32)],
        in_specs=[row(d), row(2 * dg), pl.BlockSpec((dg, tm), lambda i: (0, i)), pl.BlockSpec((dg, tm), lambda i: (0, i)),
                  full(g_fox), full(g_sb), full(w_out), full(mod8),
                  full(bd), full(hsel)],
        out_specs=[row(dg), row(dg), pl.BlockSpec((LANES, tm), lambda i: (0, i)), row(d), pl.BlockSpec((1, 8, d), lambda i: (i, 0, 0))],
        compiler_params=_params(("arbitrary",)),
    )(dx2, mix, o_fox, o_sb, g_fox, g_sb, w_out, mod8, bd, hsel)


def _in_proj_bwd(dparts, dfl, w_qkv, w_f, x, dx2, mod8, g_attn, tm):
    t, d = x.shape
    dg = dparts[1].shape[1]

    def body(*refs):
        d_refs = refs[:6]
        dfl_ref, w_ref, wf_ref, x_ref, dx2_ref, mod_ref, g_ref, gx_ref, dp_ref, dflb_ref, part_ref = refs[6:]
        dh = None
        for k in range(6):
            dk = d_refs[k][...].T if k in (0, 3) else d_refs[k][...]
            if k in (0, 3):
                dk = dk * HEAD_DIM ** -0.5
            db = dk.astype(BF16)
            dp_ref[:, k * dg:(k + 1) * dg] = db
            term = lax.dot_general(db, w_ref[:, k * dg:(k + 1) * dg], NT_DIMS, preferred_element_type=F32)
            dh = term if dh is None else dh + term
        dfb = dfl_ref[...].T.astype(BF16)
        dflb_ref[...] = dfb
        dh = dh + jnp.dot(dfb, wf_ref[...], preferred_element_type=F32)
        dx, dshift, dscale, dgn = _norm_mod_bwd(x_ref[...], dh, g_ref[...], mod_ref[1:2, :])
        gx_ref[...] = dx2_ref[...] + dx
        part_ref[0] = jnp.zeros((8, d), F32)
        part_ref[0, 0:1, :] = dshift
        part_ref[0, 1:2, :] = dscale
        part_ref[0, 2:3, :] = dgn

    row = lambda w: pl.BlockSpec((tm, w), lambda i: (i, 0))
    full = lambda a: pl.BlockSpec(a.shape, lambda i: (0,) * a.ndim)
    nt = t // tm
    return _pcall(
        body, name="in_proj_bwd", grid=(nt,),
        out_shape=[jax.ShapeDtypeStruct((t, d), F32), jax.ShapeDtypeStruct((t, 6 * dg), BF16),
                   jax.ShapeDtypeStruct((t, LANES), BF16), jax.ShapeDtypeStruct((nt, 8, d), F32)],
        in_specs=[pl.BlockSpec((dg, tm), lambda i: (0, i)), row(dg), row(dg)] * 2
        + [pl.BlockSpec((LANES, tm), lambda i: (0, i)), full(w_qkv), full(w_f), row(d), row(d), full(mod8), full(g_attn)],
        out_specs=[row(d), row(6 * dg), row(LANES), pl.BlockSpec((1, 8, d), lambda i: (i, 0, 0))],
        compiler_params=_params(("arbitrary",)),
    )(*dparts, dfl, w_qkv, w_f, x, dx2, mod8, g_attn)


def _matmul_tn(a_t, b, name):
    m, t = a_t.shape
    n = b.shape[1]
    tm_ = _tile(m, (1408, 1024, 512, 256, 128))
    tn_ = _tile(n, (1408, 1024, 512, 256, 128))
    tk = _tile(t, (1024, 512, 256, 128))
    nk = t // tk

    def body(a_ref, b_ref, o_ref):
        k = pl.program_id(2)

        @pl.when(k == 0)
        def _():
            o_ref[...] = jnp.zeros_like(o_ref)

        o_ref[...] += jnp.dot(a_ref[...], b_ref[...], preferred_element_type=F32)

    return _pcall(
        body, name=name, grid=(m // tm_, n // tn_, nk),
        out_shape=jax.ShapeDtypeStruct((m, n), F32),
        in_specs=[pl.BlockSpec((tm_, tk), lambda i, j, k: (i, k)), pl.BlockSpec((tk, tn_), lambda i, j, k: (k, j))],
        out_specs=pl.BlockSpec((tm_, tn_), lambda i, j, k: (i, j)),
        compiler_params=_params(("arbitrary", "arbitrary", "arbitrary")),
    )(a_t, b)


HALO = 16


def _conv_taps(up_ext, cw, lo, rows):
    s1 = pltpu.roll(up_ext, 1, 0)
    s2 = pltpu.roll(up_ext, 2, 0)
    u = cw[2:3, :] * up_ext[lo:lo + rows] + cw[1:2, :] * s1[lo:lo + rows] + cw[0:1, :] * s2[lo:lo + rows] + cw[3:4, :]
    return u, s1, s2


def _chunk_major(w, cf):
    d, n = w.shape[0], w.shape[1] // cf
    return jnp.transpose(w.reshape(d, n, cf), (1, 0, 2))


def _ffn_fwd(x2, target, mod8, g_mlp, g_final, wg, wv, cwg, cwv, wd, tm, cf):
    t, d = x2.shape
    dfp = wg.shape[1]
    nt, nc = t // tm, dfp // cf
    hb = tm // HALO
    wg_c, wv_c = _chunk_major(wg, cf), _chunk_major(wv, cf)

    def body(x_ref, xp_ref, tg_ref, mod_ref, g_ref, gf_ref, wg_ref, wv_ref, cg_ref, cv_ref, wd_ref,
             dx3_ref, h2_ref, part_ref, act_sc):
        i = pl.program_id(0)
        xe = jnp.concatenate([xp_ref[...], x_ref[...]], axis=0)
        h = _norm_mod(xe, g_ref[...], mod_ref[4:5, :], mod_ref[3:4, :]).astype(BF16)
        h2_ref[...] = _transposed(h[HALO:])
        first = jnp.where(i > 0, h[:HALO], jnp.zeros_like(h[:HALO]))
        h = jnp.concatenate([first, h[HALO:]], axis=0)

        def up(c):
            return (jnp.dot(h, wg_ref[c], preferred_element_type=F32), jnp.dot(h, wv_ref[c], preferred_element_type=F32))

        def activation(c, ups):
            cols = slice(c * cf, (c + 1) * cf)
            ug, _, _ = _conv_taps(ups[0], cg_ref[:, cols], HALO, tm)
            uv, _, _ = _conv_taps(ups[1], cv_ref[:, cols], HALO, tm)
            act_sc[:, cols] = (ug * jax.nn.sigmoid(ug) * uv).astype(BF16)

        for c0 in range(0, nc, 2):
            group = list(range(c0, min(c0 + 2, nc)))
            ups = [up(c) for c in group]
            for c, u in zip(group, ups):
                activation(c, u)

        y_ffn = jnp.dot(act_sc[...], wd_ref[...], preferred_element_type=F32)
        x3 = x_ref[...] + mod_ref[5:6, :] * y_ffn
        r3 = lax.rsqrt(jnp.mean(x3 * x3, axis=-1, keepdims=True) + EPS)
        xn = x3 * r3
        gf = gf_ref[...]
        diff = xn * gf - tg_ref[...]
        dy = diff * (1.0 / d)
        dxn = dy * gf
        dx3 = r3 * (dxn - xn * jnp.mean(dxn * xn, axis=-1, keepdims=True))
        dx3_ref[...] = dx3
        part_ref[0] = jnp.zeros((8, d), F32)
        part_ref[0, 0:1, :] = jnp.sum(dy * xn, axis=0, keepdims=True)
        part_ref[0, 1:2, :] = jnp.sum(dx3 * y_ffn, axis=0, keepdims=True)
        part_ref[0, 2:3, :] = jnp.sum(diff * diff, axis=0, keepdims=True) * (0.5 / d)

    row = lambda w: pl.BlockSpec((tm, w), lambda i: (i, 0))
    full = lambda a: pl.BlockSpec(a.shape, lambda i: (0,) * a.ndim)
    once = lambda a: pl.BlockSpec(a.shape, lambda i: (0,) * a.ndim, pipeline_mode=pl.Buffered(1))
    return _pcall(
        body, name="ffn_fwd", grid=(nt,),
        out_shape=[jax.ShapeDtypeStruct((t, d), F32), jax.ShapeDtypeStruct((d, t), BF16),
                   jax.ShapeDtypeStruct((nt, 8, d), F32)],
        in_specs=[row(d), pl.BlockSpec((HALO, d), lambda i: (jnp.maximum(i * hb - 1, 0), 0)), row(d),
                  full(mod8), full(g_mlp), full(g_final), once(wg_c), once(wv_c), once(cwg), once(cwv), once(wd)],
        out_specs=[row(d), pl.BlockSpec((d, tm), lambda i: (0, i)), pl.BlockSpec((1, 8, d), lambda i: (i, 0, 0))],
        scratch_shapes=[pltpu.VMEM((tm, dfp), BF16)],
        compiler_params=_params(("arbitrary",)),
    )(x2, x2, target, mod8, g_mlp, g_final, wg_c, wv_c, cwg, cwv, wd)


def _ffn_bwd(x2, dx3, mod8, g_mlp, wg, wv, cwg, cwv, wd, tm, cf):
    t, d = x2.shape
    dfp = wg.shape[1]
    nt, nc = t // tm, dfp // cf
    hb = tm // HALO
    nhb = t // HALO
    n = tm + HALO

    def body(x_ref, xp_ref, xn_ref, dx_ref, dxn_ref, mod_ref, g_ref, wg_ref, wv_ref, cg_ref, cv_ref, wd_ref,
             dx2_ref, dug_ref, duv_ref, act_ref, dxg_ref, part_ref, pcg_ref, pcv_ref):
        i = pl.program_id(0)
        xe = jnp.concatenate([xp_ref[...], x_ref[...], xn_ref[...]], axis=0)
        h = _norm_mod(xe, g_ref[...], mod_ref[4:5, :], mod_ref[3:4, :]).astype(BF16)
        h = jnp.concatenate([jnp.where(i > 0, h[:HALO], jnp.zeros_like(h[:HALO])), h[HALO:]], axis=0)
        dx = dx_ref[...] * mod_ref[5:6, :]
        dxn = jnp.where(i < nt - 1, dxn_ref[...] * mod_ref[5:6, :], 0.0)
        de = jnp.concatenate([dx, dxn], axis=0).astype(BF16)
        dxg_ref[...] = de[:tm]
        pcg_ref[0] = jnp.zeros((8, dfp), F32)
        pcv_ref[0] = jnp.zeros((8, dfp), F32)

        def products(c):
            cols = slice(c * cf, (c + 1) * cf)
            return (jnp.dot(h, wg_ref[:, cols], preferred_element_type=F32), jnp.dot(h, wv_ref[:, cols], preferred_element_type=F32),
                    lax.dot_general(de, wd_ref[cols, :], NT_DIMS, preferred_element_type=F32))

        def back(du, cw, up, s1, s2, pc_ref, cols):
            dup = (cw[2:3, :] * du + cw[1:2, :] * pltpu.roll(du, n - 1, 0) + cw[0:1, :] * pltpu.roll(du, n - 2, 0))[:tm]
            dut = du[:tm]
            pc_ref[0, 0:1, cols] = jnp.sum(dut * s2[HALO:HALO + tm], axis=0, keepdims=True)
            pc_ref[0, 1:2, cols] = jnp.sum(dut * s1[HALO:HALO + tm], axis=0, keepdims=True)
            pc_ref[0, 2:3, cols] = jnp.sum(dut * up[HALO:HALO + tm], axis=0, keepdims=True)
            pc_ref[0, 3:4, cols] = jnp.sum(dut, axis=0, keepdims=True)
            return dup.astype(BF16)

        def chunk(c, prods):
            cols = slice(c * cf, (c + 1) * cf)
            upg, upv, dact = prods
            cg, cv = cg_ref[:, cols], cv_ref[:, cols]
            ug, g1, g2 = _conv_taps(upg, cg, HALO, n)
            uv, v1, v2 = _conv_taps(upv, cv, HALO, n)
            sg = jax.nn.sigmoid(ug)
            sil = ug * sg
            act_ref[cols, :] = _transposed((sil * uv)[:tm].astype(BF16))
            dug_ref[:, cols] = back(dact * uv * (sg * (1.0 + ug * (1.0 - sg))), cg, upg, g1, g2, pcg_ref, cols)
            duv_ref[:, cols] = back(dact * sil, cv, upv, v1, v2, pcv_ref, cols)

        for c0 in range(0, nc, 2):
            group = list(range(c0, min(c0 + 2, nc)))
            prods = [products(c) for c in group]
            for c, pr in zip(group, prods):
                chunk(c, pr)

        dh = (lax.dot_general(dug_ref[...], wg_ref[...], NT_DIMS, preferred_element_type=F32)
              + lax.dot_general(duv_ref[...], wv_ref[...], NT_DIMS, preferred_element_type=F32))
        dxt, dshift, dscale, dgn = _norm_mod_bwd(x_ref[...], dh, g_ref[...], mod_ref[4:5, :])
        dx2_ref[...] = dx_ref[...] + dxt
        part_ref[0] = jnp.zeros((8, d), F32)
        part_ref[0, 0:1, :] = dshift
        part_ref[0, 1:2, :] = dscale
        part_ref[0, 2:3, :] = dgn

    row = lambda w: pl.BlockSpec((tm, w), lambda i: (i, 0))
    prev = pl.BlockSpec((HALO, d), lambda i: (jnp.maximum(i * hb - 1, 0), 0))
    nxt = pl.BlockSpec((HALO, d), lambda i: (jnp.minimum((i + 1) * hb, nhb - 1), 0))
    full = lambda a: pl.BlockSpec(a.shape, lambda i: (0,) * a.ndim)
    once = lambda a: pl.BlockSpec(a.shape, lambda i: (0,) * a.ndim, pipeline_mode=pl.Buffered(1))
    part = lambda w: pl.BlockSpec((1, 8, w), lambda i: (i, 0, 0))
    return _pcall(
        body, name="ffn_bwd", grid=(nt,),
        out_shape=[jax.ShapeDtypeStruct((t, d), F32), jax.ShapeDtypeStruct((t, dfp), BF16),
                   jax.ShapeDtypeStruct((t, dfp), BF16), jax.ShapeDtypeStruct((dfp, t), BF16),
                   jax.ShapeDtypeStruct((t, d), BF16), jax.ShapeDtypeStruct((nt, 8, d), F32),
                   jax.ShapeDtypeStruct((nt, 8, dfp), F32), jax.ShapeDtypeStruct((nt, 8, dfp), F32)],
        in_specs=[row(d), prev, nxt, row(d), nxt, full(mod8), full(g_mlp), once(wg), once(wv), once(cwg), once(cwv), once(wd)],
        out_specs=[row(d), row(dfp), row(dfp), pl.BlockSpec((dfp, tm), lambda i: (0, i)), row(d), part(d), part(dfp), part(dfp)],
        compiler_params=_params(("arbitrary",)),
    )(x2, x2, x2, dx3, dx3, mod8, g_mlp, wg, wv, cwg, cwv, wd)


BLK = 2 * LANES
XROWS = 144
LANE_FS, LANE_FT_A, LANE_FT_B = 0, 3, 6


def _head_masks():
    lane = lax.broadcasted_iota(jnp.int32, (1, LANES), 1)
    in_a = lane < HEAD_DIM
    return in_a, jnp.logical_not(in_a)


def _pieces3(x):
    hi = x.astype(BF16).astype(F32)
    r = x - hi
    mid = r.astype(BF16).astype(F32)
    return hi, mid, (r - mid).astype(BF16).astype(F32)


def _bias_lanes(rows, entries):
    sub = lax.broadcasted_iota(jnp.int32, (16, 1), 0)
    out = jnp.zeros((16, rows), F32)
    for l, v in entries:
        out = jnp.where(sub == l, v, out)
    return jnp.concatenate([out, jnp.zeros((LANES - 16, rows), F32)], axis=0).T


def _three(first, values):
    return [(first + k, v) for k, v in enumerate(values)]


def _stack_rows(x, in_a, in_b):
    zero = jnp.zeros_like(x)
    return jnp.concatenate([jnp.where(in_a, x, zero), jnp.where(in_b, x, zero)], axis=0)


def _transposed(x):
    return x.astype(F32).T.astype(BF16)


def _attn_operands(qkv, frow, dg):
    t = qkv.shape[0]
    p, nk = dg // LANES, t // BLK

    def body(qf_ref, kf_ref, vf_ref, ks_ref, vs_ref, f_ref, qx_ref, kx_ref, kxt_ref, vf_o, vft_o, ks_o, kst_o, vs_o, vst_o):
        in_a, in_b = _head_masks()
        fa, fb = _pieces3(f_ref[0]), _pieces3(f_ref[1])
        qx_ref[0, :, :LANES] = qf_ref[...]
        qx_ref[0, :, LANES:] = _bias_lanes(
            BLK, _three(LANE_FS, (-1.0,) * 3) + _three(LANE_FT_A, fa) + _three(LANE_FT_B, fb)).astype(BF16)
        kf = kf_ref[...]
        zero = jnp.zeros_like(kf)
        top = jnp.concatenate([jnp.where(in_a, kf, zero), _bias_lanes(
            BLK, _three(LANE_FS, fa) + _three(LANE_FT_A, (1.0,) * 3)).astype(BF16)], axis=1)
        bot = jnp.concatenate([jnp.where(in_b, kf, zero), _bias_lanes(
            BLK, _three(LANE_FS, fb) + _three(LANE_FT_B, (1.0,) * 3)).astype(BF16)], axis=1)
        kx = jnp.concatenate([top, bot], axis=0)
        kx_ref[0, 0] = kx
        kxt_ref[0, 0] = _transposed(kx)[:XROWS]
        for src, dst, dst_t in ((vf_ref, vf_o, vft_o), (ks_ref, ks_o, kst_o), (vs_ref, vs_o, vst_o)):
            st = _stack_rows(src[...], in_a, in_b)
            dst[0, 0] = st
            dst_t[0, 0] = _transposed(st)

    col = lambda base: pl.BlockSpec((BLK, LANES), lambda h, j: (j, base * p + h))
    blk4 = lambda r, c: pl.BlockSpec((1, 1, r, c), lambda h, j: (h, j, 0, 0))
    shp4 = lambda r, c: jax.ShapeDtypeStruct((p, nk, r, c), BF16)
    return _pcall(
        body, name="attn_operands", grid=(p, nk),
        out_shape=[jax.ShapeDtypeStruct((p, t, 2 * LANES), BF16), shp4(2 * BLK, 2 * LANES), shp4(XROWS, 2 * BLK)]
        + [shp4(2 * BLK, LANES), shp4(LANES, 2 * BLK)] * 3,
        in_specs=[col(0), col(1), col(2), col(4), col(5), pl.BlockSpec((2, 1, BLK), lambda h, j: (h, 0, j))],
        out_specs=[pl.BlockSpec((1, BLK, 2 * LANES), lambda h, j: (h, j, 0)), blk4(2 * BLK, 2 * LANES), blk4(XROWS, 2 * BLK)]
        + [blk4(2 * BLK, LANES), blk4(LANES, 2 * BLK)] * 3,
        compiler_params=_params(("arbitrary", "arbitrary")),
    )(qkv, qkv, qkv, qkv, qkv, frow)


def _key_query_masks():
    key = lax.broadcasted_iota(jnp.int32, (BLK, BLK), 0)
    qry = lax.broadcasted_iota(jnp.int32, (BLK, BLK), 1)
    return key <= qry, key < qry


def _key_triangle(kind):
    s = lax.broadcasted_iota(jnp.int32, (BLK, BLK), 0)
    j = lax.broadcasted_iota(jnp.int32, (BLK, BLK), 1)
    return {"suffix": j >= s, "prefix": j <= s, "before": j < s}[kind].astype(BF16)


def _tri_dot(tri, x, passes):
    acc = None
    for _ in range(passes):
        part = x.astype(BF16)
        d = jnp.dot(tri, part, preferred_element_type=F32)
        acc = d if acc is None else acc + d
        x = x - part.astype(F32)
    return acc


GROUPS = (4, 2, 1)


def _loop_blocks(n, tiles, carry, descending=False, groups=GROUPS):
    at = (lambda k: n - 1 - k) if descending else (lambda k: k)
    done = 0
    for g in groups:
        left = n - done
        carry = lax.fori_loop(0, left // g, lambda h, c, g=g, done=done: tiles([at(done + g * h + k) for k in range(g)], c), carry)
        done = done + (left // g) * g
    return carry


def _resident(shape):
    return pl.BlockSpec((1,) + shape, lambda h, i: (h,) + (0,) * len(shape), pipeline_mode=pl.Buffered(1))


def _rows_per_head(a, b):
    return jnp.concatenate([jnp.broadcast_to(a, (HEAD_DIM, BLK)), jnp.broadcast_to(b, (HEAD_DIM, BLK))], axis=0)


def _fold_heads(stacked, in_a):
    return jnp.where(in_a, stacked[:BLK], stacked[BLK:])


def _xy_gather_copies(ins, outs, send_sems, recv_sems, local_sems):
    x, y, c = lax.axis_index("x"), lax.axis_index("y"), lax.axis_index("c")
    chips = [(1 - x, y), (x, 1 - y), (1 - x, 1 - y)]
    mine = 2 * x + y
    local, remote = [], []
    for w in range(len(ins)):
        local.append(pltpu.make_async_copy(ins[w], outs[w].at[mine], local_sems.at[w]))
        for k, (px, py) in enumerate(chips):
            remote.append(pltpu.make_async_remote_copy(
                src_ref=ins[w], dst_ref=outs[w].at[mine], send_sem=send_sems.at[3 * w + k],
                recv_sem=recv_sems.at[3 * w + k], device_id=(px, py, c), device_id_type=MESH))
    return local, remote


def _fox_fwd(qx, kx, v_t, dg, shards):
    p, t = qx.shape[0], qx.shape[1]
    nq = t // BLK
    nh = 2 * p
    ns = len(shards)

    def body(q_ref, k_ref, vt_ref, *rest):
        shard_refs, (o_ref, lse_ref), gathered = rest[:ns], rest[ns:ns + 2], rest[ns + 2:2 * ns + 2]
        local, remote = _xy_gather_copies(shard_refs, gathered, *rest[2 * ns + 2:])
        i = pl.program_id(1)

        @pl.when((pl.program_id(0) == 0) & (i == 0))
        def _():
            for cp in local + remote:
                cp.start()

        causal, _ = _key_query_masks()
        q = q_ref[0]

        def scores(j, masked):
            s2 = lax.dot_general(k_ref[0, j], q, NT_DIMS, preferred_element_type=F32)
            s = [s2[a * BLK:(a + 1) * BLK] for a in range(2)]
            return [jnp.where(causal, x, NEG) for x in s] if masked else s

        def update(blocks, carry):
            m, l, acc = list(carry[0]), list(carry[1]), carry[2]
            for j, s in blocks:
                alpha, pr = [], []
                for a in range(2):
                    mn = jnp.maximum(m[a], jnp.max(s[a], axis=0, keepdims=True))
                    pa = jnp.exp(s[a] - mn)
                    al = jnp.exp(m[a] - mn)
                    l[a] = al * l[a] + jnp.sum(pa, axis=0, keepdims=True)
                    m[a] = mn
                    alpha.append(al)
                    pr.append(pa.astype(BF16))
                acc = _rows_per_head(*alpha) * acc + jnp.dot(vt_ref[0, j], jnp.concatenate(pr, axis=0), preferred_element_type=F32)
            return tuple(m), tuple(l), acc

        tiles = lambda js, c: update([(j, scores(j, False)) for j in js], c)
        neg, zero = jnp.full((1, BLK), NEG, F32), jnp.zeros((1, BLK), F32)
        carry = _loop_blocks(i, tiles, ((neg, neg), (zero, zero), jnp.zeros((LANES, BLK), F32)), groups=(8, 4, 2, 1))
        m, l, acc = update([(i, scores(i, True))], carry)
        o_ref[...] = acc / _rows_per_head(*l)
        lse_ref[0] = m[0] + jnp.log(l[0])
        lse_ref[1] = m[1] + jnp.log(l[1])

        @pl.when((pl.program_id(0) == p - 1) & (i == nq - 1))
        def _():
            for cp in remote:
                cp.wait_recv()
            for cp in remote:
                cp.wait_send()
            for cp in local:
                cp.wait()

    row = pl.BlockSpec((2, 1, BLK), lambda h, i: (h, 0, i))
    hbm = pl.BlockSpec(memory_space=pltpu.HBM)
    return _pcall(
        body, name="fox_fwd", grid=(p, nq),
        out_shape=[jax.ShapeDtypeStruct((dg, t), F32), jax.ShapeDtypeStruct((nh, 1, t), F32)]
        + [jax.ShapeDtypeStruct((4,) + s.shape, s.dtype) for s in shards],
        in_specs=[pl.BlockSpec((1, BLK, 2 * LANES), lambda h, i: (h, i, 0)), _resident((nq, 2 * BLK, 2 * LANES)),
                  _resident((nq, LANES, 2 * BLK))] + [hbm] * ns,
        out_specs=[pl.BlockSpec((LANES, BLK), lambda h, i: (h, i)), row] + [hbm] * ns,
        scratch_shapes=[pltpu.SemaphoreType.DMA((3 * ns,)), pltpu.SemaphoreType.DMA((3 * ns,)), pltpu.SemaphoreType.DMA((ns,))],
        compiler_params=_params(("arbitrary", "arbitrary")),
    )(qx, kx, v_t, *shards)


def _swap_copies(ins, outs, send_sems, recv_sems, local_sems):
    x, y, c = lax.axis_index("x"), lax.axis_index("y"), lax.axis_index("c")
    local, remote = [], []
    for w in range(len(ins)):
        local.append(pltpu.make_async_copy(ins[w], outs[w].at[c], local_sems.at[w]))
        remote.append(pltpu.make_async_remote_copy(
            src_ref=ins[w], dst_ref=outs[w].at[c], send_sem=send_sems.at[w], recv_sem=recv_sems.at[w],
            device_id=(x, y, 1 - c), device_id_type=MESH))
    return local, remote


def _fox_bwd(qx, kx, kx_t, v_st, do, delta, lse, dg, halves):
    p, t = qx.shape[0], qx.shape[1]
    nq = t // BLK
    nh = 2 * p
    ns = len(halves)

    def body(q_ref, k_ref, kt_ref, v_ref, do_ref, dl_ref, lse_ref, *rest):
        half_refs, (dq_ref, dft_ref, dk_ref, dv_ref, dkx_ref), both_refs = rest[:ns], rest[ns:ns + 5], rest[ns + 5:2 * ns + 5]
        local, remote = _swap_copies(half_refs, both_refs, *rest[2 * ns + 5:])
        i = pl.program_id(1)

        @pl.when((pl.program_id(0) == 0) & (i == 0))
        def _():
            for cp in local + remote:
                cp.start()

        @pl.when(i == 0)
        def _():
            dk_ref[...] = jnp.zeros_like(dk_ref)
            dv_ref[...] = jnp.zeros_like(dv_ref)
            dkx_ref[...] = jnp.zeros_like(dkx_ref)

        in_a, _ = _head_masks()
        first_lane = lax.broadcasted_iota(jnp.int32, (1, LANES), 1) == 0
        causal, _ = _key_query_masks()
        q, do2 = q_ref[0], do_ref[...]
        dl = (dl_ref[0], dl_ref[1])
        lse = (lse_ref[0], lse_ref[1])

        def products(j):
            return (lax.dot_general(k_ref[0, j], q, NT_DIMS, preferred_element_type=F32),
                    lax.dot_general(v_ref[0, j], do2, NT_DIMS, preferred_element_type=F32))

        def dscores(prod, masked):
            s2, dp2 = prod
            pr, ds = [], []
            for a in range(2):
                s = s2[a * BLK:(a + 1) * BLK]
                if masked:
                    s = jnp.where(causal, s, NEG)
                pa = jnp.exp(s - lse[a])
                ds.append((pa * (dp2[a * BLK:(a + 1) * BLK] - dl[a])).astype(BF16))
                pr.append(pa.astype(BF16))
            return jnp.concatenate(ds, axis=0), jnp.concatenate(pr, axis=0)

        def accumulate(j, dsb, prb, dq):
            off = pl.multiple_of(j * BLK, BLK)
            dk_full = jnp.dot(dsb, q, preferred_element_type=F32)
            dk_ref[pl.ds(off, BLK), :] += _fold_heads(dk_full[:, :LANES], in_a)
            dkx_ref[pl.ds(off, BLK), :] += jnp.where(first_lane, dk_full[:BLK, LANES:], dk_full[BLK:, LANES:])
            dv_ref[pl.ds(off, BLK), :] += _fold_heads(jnp.dot(prb, do2, preferred_element_type=F32), in_a)
            return dq + jnp.dot(kt_ref[0, j], dsb, preferred_element_type=F32)

        def tiles(js, dq, masked=False):
            prods = [products(j) for j in js]
            grads = [dscores(pr, masked) for pr in prods]
            for j, (dsb, prb) in zip(js, grads):
                dq = accumulate(j, dsb, prb, dq)
            return dq

        dq = _loop_blocks(i, tiles, jnp.zeros((XROWS, BLK), F32))
        dq = tiles([i], dq, True)
        dq_ref[...] = dq[:LANES]
        dft_ref[0] = dq[LANES + LANE_FT_A:LANES + LANE_FT_A + 1]
        dft_ref[1] = dq[LANES + LANE_FT_B:LANES + LANE_FT_B + 1]

        @pl.when((pl.program_id(0) == p - 1) & (i == nq - 1))
        def _():
            for cp in remote:
                cp.wait_recv()
            for cp in remote:
                cp.wait_send()
            for cp in local:
                cp.wait()

    row = pl.BlockSpec((2, 1, BLK), lambda h, i: (h, 0, i))
    acc = pl.BlockSpec((t, LANES), lambda h, i: (0, h))
    hbm = pl.BlockSpec(memory_space=pltpu.HBM)
    return _pcall(
        body, name="fox_bwd", grid=(p, nq),
        out_shape=[jax.ShapeDtypeStruct((dg, t), F32), jax.ShapeDtypeStruct((nh, 1, t), F32)] + [jax.ShapeDtypeStruct((t, dg), F32)] * 3
        + [jax.ShapeDtypeStruct((2,) + h.shape, h.dtype) for h in halves],
        in_specs=[pl.BlockSpec((1, BLK, 2 * LANES), lambda h, i: (h, i, 0)), _resident((nq, 2 * BLK, 2 * LANES)),
                  _resident((nq, XROWS, 2 * BLK)), _resident((nq, 2 * BLK, LANES)),
                  pl.BlockSpec((BLK, LANES), lambda h, i: (i, h)), row, row] + [hbm] * ns,
        out_specs=[pl.BlockSpec((LANES, BLK), lambda h, i: (h, i)), row, acc, acc, acc] + [hbm] * ns,
        scratch_shapes=[pltpu.SemaphoreType.DMA((ns,)), pltpu.SemaphoreType.DMA((ns,)), pltpu.SemaphoreType.DMA((ns,))],
        compiler_params=_params(("arbitrary", "arbitrary")),
    )(qx, kx, kx_t, v_st, do, delta, lse, *halves)


def _softplus_of(z):
    return jnp.maximum(z, 0.0) + jnp.log(1.0 + jnp.exp(-jnp.abs(z)))


def _sb_fwd(qkv, k_st, v_t, dg):
    t = qkv.shape[0]
    p, nq = dg // LANES, t // BLK
    nh = 2 * p

    def body(q_ref, k_ref, vt_ref, o_ref, rt_ref):
        i = pl.program_id(1)
        _, strict = _key_query_masks()
        suffix = _key_triangle("suffix")
        q = q_ref[...]

        def scores(j):
            z2 = lax.dot_general(k_ref[0, j], q, NT_DIMS, preferred_element_type=F32)
            return [z2[a * BLK:(a + 1) * BLK] for a in range(2)]

        def suffix_sums(z, masked):
            out = []
            for a in range(2):
                sp = _softplus_of(z[a])
                if masked:
                    sp = jnp.where(strict, sp, 0.0)
                out.append(_tri_dot(suffix, sp, 2))
            return out

        def weights(z, cs, rest, masked):
            w, rest_new = [], []
            for a in range(2):
                wa = jnp.exp(z[a] - cs[a] - rest[a])
                if masked:
                    wa = jnp.where(strict, wa, 0.0)
                w.append(wa.astype(BF16))
                rest_new.append(rest[a] + cs[a][0:1])
            return jnp.concatenate(w, axis=0), tuple(rest_new)

        def tiles(js, carry, masked=False):
            rest, acc = carry
            zs = [scores(j) for j in js]
            css = [suffix_sums(z, masked) for z in zs]
            ws = []
            for z, cs in zip(zs, css):
                w2, rest = weights(z, cs, rest, masked)
                ws.append(w2)
            for j, w2 in zip(js, ws):
                acc = acc + jnp.dot(vt_ref[0, j], w2, preferred_element_type=F32)
            return rest, acc

        zero = jnp.zeros((1, BLK), F32)
        carry = tiles([i], ((zero, zero), jnp.zeros((LANES, BLK), F32)), True)
        rest, acc = _loop_blocks(i, tiles, carry, descending=True, groups=(8, 4, 2, 1))
        o_ref[...] = acc
        rt_ref[0] = rest[0]
        rt_ref[1] = rest[1]

    return _pcall(
        body, name="sb_fwd", grid=(p, nq),
        out_shape=[jax.ShapeDtypeStruct((dg, t), F32), jax.ShapeDtypeStruct((nh, 1, t), F32)],
        in_specs=[pl.BlockSpec((BLK, LANES), lambda h, i: (i, 3 * p + h)), _resident((nq, 2 * BLK, LANES)),
                  _resident((nq, LANES, 2 * BLK))],
        out_specs=[pl.BlockSpec((LANES, BLK), lambda h, i: (h, i)), pl.BlockSpec((2, 1, BLK), lambda h, i: (h, 0, i))],
        compiler_params=_params(("arbitrary", "arbitrary")),
    )(qkv, k_st, v_t)


def _scatter8_copies(ins, outs, send_sems, recv_sems, local_sems):
    x, y, c = lax.axis_index("x"), lax.axis_index("y"), lax.axis_index("c")
    me = 4 * x + 2 * y + c
    local, remote = [], []
    for w in range(len(ins)):
        local.append(pltpu.make_async_copy(ins[w].at[me], outs[w].at[me], local_sems.at[w]))
        for f in range(1, 8):
            px = 1 - x if f & 4 else x
            py = 1 - y if f & 2 else y
            pc = 1 - c if f & 1 else c
            remote.append(pltpu.make_async_remote_copy(
                src_ref=ins[w].at[4 * px + 2 * py + pc], dst_ref=outs[w].at[me],
                send_sem=send_sems.at[7 * w + f - 1], recv_sem=recv_sems.at[7 * w + f - 1],
                device_id=(px, py, pc), device_id_type=MESH))
    return local, remote


def _sb_bwd(qkv, k_st, k_t, v_st, do, rtot, dg, pieces):
    t = qkv.shape[0]
    p, nq = dg // LANES, t // BLK
    ns = len(pieces)

    def body(q_ref, k_ref, kt_ref, v_ref, do_ref, rt_ref, *rest):
        piece_refs, (dq_ref, dk_ref, dv_ref), recv_refs = rest[:ns], rest[ns:ns + 3], rest[ns + 3:2 * ns + 3]
        local, remote = _scatter8_copies(piece_refs, recv_refs, *rest[2 * ns + 3:])
        i = pl.program_id(1)

        @pl.when((pl.program_id(0) == 0) & (i == 0))
        def _():
            for cp in local + remote:
                cp.start()

        @pl.when(i == 0)
        def _():
            dk_ref[...] = jnp.zeros_like(dk_ref)
            dv_ref[...] = jnp.zeros_like(dv_ref)

        in_a, _ = _head_masks()
        _, strict = _key_query_masks()
        before_m, prefix_m = _key_triangle("before"), _key_triangle("prefix")
        q, do2 = q_ref[...], do_ref[...]
        rt = (rt_ref[0], rt_ref[1])

        def products(j):
            z2 = lax.dot_general(k_ref[0, j], q, NT_DIMS, preferred_element_type=F32)
            da2 = lax.dot_general(v_ref[0, j], do2, NT_DIMS, preferred_element_type=F32)
            return [z2[a * BLK:(a + 1) * BLK] for a in range(2)], [da2[a * BLK:(a + 1) * BLK] for a in range(2)]

        def softplus_sums(z, masked):
            sp = [_softplus_of(x) for x in z]
            if masked:
                sp = [jnp.where(strict, x, 0.0) for x in sp]
            return sp, [_tri_dot(before_m, x, 2) for x in sp]

        def weight_grads(z, da, sp, pre, before, masked):
            w, g, pg, before_new = [], [], [], []
            for a in range(2):
                wa = jnp.exp(z[a] + (before[a] - rt[a]) + pre[a])
                if masked:
                    wa = jnp.where(strict, wa, 0.0)
                ga = wa * da[a]
                w.append(wa.astype(BF16))
                g.append(ga)
                pg.append(jnp.dot(prefix_m, ga.astype(BF16), preferred_element_type=F32))
                before_new.append(before[a] + pre[a][BLK - 1:BLK] + sp[a][BLK - 1:BLK])
            return jnp.concatenate(w, axis=0), g, pg, tuple(before_new)

        def dlogits(sp, g, pg, gbefore, masked):
            dz, gbefore_new = [], []
            for a in range(2):
                s_incl = gbefore[a] + pg[a]
                dza = (g[a] - s_incl) + jnp.exp(-sp[a]) * s_incl
                if masked:
                    dza = jnp.where(strict, dza, 0.0)
                dz.append(dza.astype(BF16))
                gbefore_new.append(s_incl[BLK - 1:BLK])
            return jnp.concatenate(dz, axis=0), tuple(gbefore_new)

        def accumulate(j, dzb, wb, dq):
            off = pl.multiple_of(j * BLK, BLK)
            dk_ref[pl.ds(off, BLK), :] += _fold_heads(jnp.dot(dzb, q, preferred_element_type=F32), in_a)
            dv_ref[pl.ds(off, BLK), :] += _fold_heads(jnp.dot(wb, do2, preferred_element_type=F32), in_a)
            return dq + jnp.dot(kt_ref[0, j], dzb, preferred_element_type=F32)

        def tiles(js, carry, masked=False):
            before, gbefore, dq = carry
            prods = [products(j) for j in js]
            sums = [softplus_sums(z, masked) for z, _ in prods]
            grads = []
            for (z, da), (sp, pre) in zip(prods, sums):
                wb, g, pg, before = weight_grads(z, da, sp, pre, before, masked)
                grads.append((wb, g, pg))
            for j, (sp, _), (wb, g, pg) in zip(js, sums, grads):
                dzb, gbefore = dlogits(sp, g, pg, gbefore, masked)
                dq = accumulate(j, dzb, wb, dq)
            return before, gbefore, dq

        zero = jnp.zeros((1, BLK), F32)
        carry = _loop_blocks(i, tiles, ((zero, zero), (zero, zero), jnp.zeros((LANES, BLK), F32)), groups=(2, 1))
        dq_ref[...] = tiles([i], carry, True)[2]

        @pl.when((pl.program_id(0) == p - 1) & (i == nq - 1))
        def _():
            for cp in remote:
                cp.wait_recv()
            for cp in remote:
                cp.wait_send()
            for cp in local:
                cp.wait()

    acc = pl.BlockSpec((t, LANES), lambda h, i: (0, h))
    hbm = pl.BlockSpec(memory_space=pltpu.HBM)
    return _pcall(
        body, name="sb_bwd", grid=(p, nq),
        out_shape=[jax.ShapeDtypeStruct((dg, t), F32)] + [jax.ShapeDtypeStruct((t, dg), F32)] * 2
        + [jax.ShapeDtypeStruct(pc.shape, pc.dtype) for pc in pieces],
        in_specs=[pl.BlockSpec((BLK, LANES), lambda h, i: (i, 3 * p + h)), _resident((nq, 2 * BLK, LANES)),
                  _resident((nq, LANES, 2 * BLK)), _resident((nq, 2 * BLK, LANES)),
                  pl.BlockSpec((BLK, LANES), lambda h, i: (i, h)), pl.BlockSpec((2, 1, BLK), lambda h, i: (h, 0, i))] + [hbm] * ns,
        out_specs=[pl.BlockSpec((LANES, BLK), lambda h, i: (h, i)), acc, acc] + [hbm] * ns,
        scratch_shapes=[pltpu.SemaphoreType.DMA((7 * ns,)), pltpu.SemaphoreType.DMA((7 * ns,)), pltpu.SemaphoreType.DMA((ns,))],
        compiler_params=_params(("arbitrary", "arbitrary")),
    )(qkv, k_st, k_t, v_st, do, rtot, *pieces)


def _tri_constants(nh, t):
    nb = t // LANES
    r = nh * nb
    li = np.arange(LANES)
    tri_in = (li[:, None] <= li[None, :])
    ri = np.arange(r)
    same = (ri[:, None] // nb) == (ri[None, :] // nb)
    blk = same & (ri[None, :] < ri[:, None])
    blk_rev = same & (ri[None, :] > ri[:, None])
    head_rows = (np.arange(max(8, nh))[:, None] == (ri[None, :] // nb))
    as_bf16 = lambda a: jnp.asarray(a.astype(np.float32), BF16)
    return as_bf16(tri_in), as_bf16(blk), as_bf16(tri_in.T), as_bf16(blk_rev), as_bf16(head_rows)


def kernel(x, c, w_ada, b_ada, g_attn, w_in, b_fgate, g_out_fox, g_out_sb, w_out, g_mlp, w_up, conv_w, conv_b, w_down, g_final, loss_target, m_w_ada, m_b_ada, m_g_attn, m_w_in, m_b_fgate, m_g_out_fox, m_g_out_sb, m_w_out, m_g_mlp, m_w_up, m_conv_w, m_conv_b, m_w_down, m_g_final, v_w_ada, v_b_ada, v_g_attn, v_w_in, v_b_fgate, v_g_out_fox, v_g_out_sb, v_w_out, v_g_mlp, v_w_up, v_conv_w, v_conv_b, v_w_down, v_g_final):
    t, d = x.shape[1], x.shape[2]
    dg = d // 2
    nh = dg // HEAD_DIM
    n_in = 6 * dg + nh
    dff = w_down.shape[1] * 4
    dfp = -(-dff // 256) * 256
    cf = 256
    tm = _tile(t, (512, 256, 128))
    nq = t // BLK
    xi, yi, ci = lax.axis_index("x"), lax.axis_index("y"), lax.axis_index("c")
    shard = 2 * xi + yi
    me = 4 * xi + 2 * yi + ci

    x2d, tg2d = x[0], loss_target[0]

    c_all = _all_gather8(jnp.pad(c, ((0, 7), (0, 0)))).reshape(8, 8, d)[:, 0, :]
    ada_cols = w_ada.shape[2]
    b_shard = lax.dynamic_slice(b_ada, (0, shard * ada_cols), (1, ada_cols))
    sc_all, mod_shard = _ada_fwd(c_all, w_ada[0], b_shard)
    mod_all = _all_gather8(mod_shard).reshape(4, 2, 8, ada_cols)
    mod_me = lax.dynamic_index_in_dim(mod_all[:, 0], me, axis=1, keepdims=False)
    mod8 = jnp.pad(mod_me.reshape(6, d), ((0, 2), (0, 0)))

    (g_in,) = _gather_xy([w_in[0].astype(BF16)])
    later_shards = [w_out[0].astype(BF16), w_up[0].astype(BF16), w_down[0].astype(BF16), conv_w[0]]
    w_in_full = jnp.transpose(g_in, (1, 0, 2)).reshape(d, n_in)
    w_qkv = w_in_full[:, :6 * dg]
    w_f = jnp.pad(w_in_full[:, 6 * dg:].T, ((0, LANES - nh), (0, 0)))

    qkv, fl, h1 = _in_proj_fwd(x2d, mod8, g_attn, w_qkv, w_f, tm)
    tri_in, tri_blk, tri_in_rev, tri_blk_rev, head_rows = _tri_constants(nh, t)
    fl2d = fl[:nh].reshape(nh * t // LANES, LANES)
    b_rows = jnp.repeat(b_fgate[0], t // LANES)[:, None]
    f2d = _fgate_fwd(fl2d, b_rows, tri_in, tri_blk)
    frow = f2d.reshape(nh, 1, t)
    pairs = nh // 2
    qx, kx, kx_t, vf_st, vf_t, ks_st, ks_t, vs_st, vs_t = _attn_operands(qkv, frow, dg)
    o_fox, lse, g_out, g_up, g_down, g_cw = _fox_fwd(qx, kx, vf_t, dg, later_shards)
    w_out_full = g_out.reshape(2 * dg, d)
    w_up_full = jnp.transpose(g_up, (1, 0, 2)).reshape(d, 2 * dff)
    padc = ((0, 0), (0, dfp - dff))
    wg, wv = jnp.pad(w_up_full[:, :dff], padc), jnp.pad(w_up_full[:, dff:], padc)
    wd = jnp.pad(g_down.reshape(dff, d), ((0, dfp - dff), (0, 0)))
    cw_full = jnp.transpose(g_cw, (1, 0, 2)).reshape(3, 2 * dff)
    cw4 = jnp.concatenate([cw_full, conv_b], axis=0)
    cwg = jnp.pad(cw4[:, :dff], ((0, 4), (0, dfp - dff)))
    cwv = jnp.pad(cw4[:, dff:], ((0, 4), (0, dfp - dff)))
    o_sb, rtot = _sb_fwd(qkv, ks_st, vs_t, dg)
    li = np.arange(dg)
    bd = jnp.asarray((li[:, None] // HEAD_DIM == li[None, :] // HEAD_DIM).astype(np.float32), BF16)
    hsel = jnp.asarray((np.arange(LANES)[:, None] == li[None, :] // HEAD_DIM).astype(np.float32), BF16)
    x2, mix, mix_t = _attn_out_fwd(x2d, o_fox, o_sb, g_out_fox, g_out_sb, w_out_full, mod8, bd, tm)
    g_final2 = g_final[None, :]
    dx3, h2, part_f = _ffn_fwd(x2, tg2d, mod8, g_mlp, g_final2, wg, wv, cwg, cwv, wd, tm, cf)

    tm_ffn_bwd = _tile(t, (256, 128))
    dx2, dupg, dupv, act, dxg3, part_b, pcg, pcv = _ffn_bwd(x2, dx3, mod8, g_mlp, wg, wv, cwg, cwv, wd, tm_ffn_bwd, cf)
    do_fox, do_sb, delta, dxg2, part_o = _attn_out_bwd(dx2, mix, o_fox, o_sb, g_out_fox, g_out_sb, w_out_full, mod8, bd, hsel, tm)
    drow = delta[:nh].reshape(nh, 1, t)

    def col_pieces(g):
        r, cc = g.shape
        return jnp.transpose(g.reshape(2, r // 2, 4, cc // 4), (2, 0, 1, 3)).reshape(8, r // 2, cc // 4)

    def row_pieces(g):
        r, cc = g.shape
        return g.reshape(8, r // 8, cc)

    gw_out = _matmul_tn(mix_t, dxg2, "grad_w_out")
    gw_upg = _matmul_tn(h2, dupg, "grad_w_up_gate")
    gw_upv = _matmul_tn(h2, dupv, "grad_w_up_val")
    gw_up = jnp.concatenate([gw_upg[:, :dff], gw_upv[:, :dff]], axis=1)
    gw_down = _matmul_tn(act, dxg3, "grad_w_down")[:dff]
    early = (row_pieces(gw_out), col_pieces(gw_up), row_pieces(gw_down))
    early = [_to_bf16(pc, "pieces_bf16_" + nm) for pc, nm in zip(early, ("w_out", "w_up", "w_down"))]

    dq_s, dk_s, dv_s, *recv_early = _sb_bwd(qkv, ks_st, ks_t, vs_st, do_sb, rtot, dg, early)
    halves_early = [_sum_leading(rv, nm) for rv, nm in zip(recv_early, ("sum_w_out", "sum_w_up", "sum_w_down"))]
    dq_f, dft, dk_f, dv_f, dkx, *swapped_early = _fox_bwd(qx, kx, kx_t, vf_st, do_fox, drow, lse, dg, halves_early)
    f2d_shape = (nh * t // LANES, LANES)
    dfs = jnp.transpose(dkx.reshape(t, pairs, LANES)[:, :, :2], (1, 2, 0))
    dfl2d, gb8 = _fgate_bwd(fl2d, b_rows, dft.reshape(f2d_shape), dfs.reshape(f2d_shape), tri_in_rev, tri_blk_rev, head_rows)
    dfl = jnp.pad(dfl2d.reshape(nh, t), ((0, LANES - nh), (0, 0)))
    grad_x, dproj, dflb, part_i = _in_proj_bwd([dq_f, dk_f, dv_f, dq_s, dk_s, dv_s], dfl, w_qkv, w_f, x2d, dx2, mod8, g_attn, tm)

    gw_qkv = _matmul_tn(h1, dproj, "grad_w_qkv")
    gw_f = _matmul_tn(h1, dflb, "grad_w_f")
    gw_in = jnp.concatenate([gw_qkv, gw_f[:, :nh]], axis=1)

    sf = _sum_leading(part_f, "sum_part_ffn_fwd")
    sb_ = _sum_leading(part_b, "sum_part_ffn_bwd")
    so = _sum_leading(part_o, "sum_part_attn_out")
    si = _sum_leading(part_i, "sum_part_in_proj")
    scg = _sum_leading(pcg, "sum_part_conv_gate")
    scv = _sum_leading(pcv, "sum_part_conv_val")
    gb_f = gb8[:nh, 0]
    dmod = jnp.concatenate([si[0], si[1], so[0], sb_[0], sb_[1], sf[1]])
    g_conv_w = jnp.concatenate([scg[0:3, :dff], scv[0:3, :dff]], axis=1).reshape(-1)
    g_conv_b = jnp.concatenate([scg[3, :dff], scv[3, :dff]])
    loss_part = jnp.sum(sf[2])
    fields = [dmod, si[2], gb_f, so[1, :dg], so[1, dg:], sb_[2], g_conv_b, sf[0], g_conv_w, loss_part[None]]
    sizes = [int(f.shape[0]) for f in fields]
    n_pack = sum(sizes)
    lanes_pack = -(-n_pack // (8 * LANES)) * LANES
    pack = jnp.pad(jnp.concatenate(fields), (0, 8 * lanes_pack - n_pack)).reshape(8, lanes_pack)
    gathered = _all_gather8(pack)
    tot = _sum_leading(gathered.reshape(8, 8, lanes_pack), "sum_pack").reshape(-1)
    offs = np.concatenate([[0], np.cumsum(sizes)])
    take = lambda k: tot[int(offs[k]):int(offs[k + 1])]
    g_b_ada, g_g_attn, g_b_fgate, g_g_fox, g_g_sb, g_g_mlp, g_cb, g_g_final, g_cw_full, loss_v = [take(k) for k in range(10)]
    loss = loss_v[0]
    dmod_all = gathered.reshape(8, 8 * lanes_pack)[:, :6 * d]
    dmod_cols = lax.dynamic_slice(dmod_all, (0, shard * ada_cols), (8, ada_cols))
    g_w_ada = _ada_bwd(sc_all.T, dmod_cols)

    (recv_in,) = _scatter8([_to_bf16(col_pieces(gw_in), "pieces_bf16_w_in")])
    (swapped_in,) = _swap_halves([_sum_leading(recv_in, "sum_w_in")])
    swapped = [swapped_in] + swapped_early
    g_w_in, g_w_out, g_w_up, g_w_down = [s.reshape(2 * s.shape[1], s.shape[2]) for s in swapped]
    g_conv_w_shard = lax.dynamic_slice(g_cw_full.reshape(3, 2 * dff), (0, shard * (dff // 2)), (3, dff // 2))

    grads, deltas, new_m, new_v = {}, {}, {}, {}

    def step(name, w, g, m, v):
        shape = w.shape
        as2d = lambda a: a.reshape(-1, shape[-1])
        dl, nm, nv = _adamw(as2d(w), as2d(g), as2d(m), as2d(v), "adamw_" + name)
        grads[name], deltas[name], new_m[name], new_v[name] = g.reshape(shape), dl.reshape(shape), nm.reshape(shape), nv.reshape(shape)

    step("w_ada", w_ada, g_w_ada, m_w_ada, v_w_ada)
    step("w_in", w_in, g_w_in, m_w_in, v_w_in)
    step("w_out", w_out, g_w_out, m_w_out, v_w_out)
    step("w_up", w_up, g_w_up, m_w_up, v_w_up)
    step("conv_w", conv_w, g_conv_w_shard, m_conv_w, v_conv_w)
    step("w_down", w_down, g_w_down, m_w_down, v_w_down)

    small = [("b_ada", b_ada, g_b_ada, m_b_ada, v_b_ada), ("g_attn", g_attn, g_g_attn, m_g_attn, v_g_attn),
             ("b_fgate", b_fgate, g_b_fgate, m_b_fgate, v_b_fgate), ("g_out_fox", g_out_fox, g_g_fox, m_g_out_fox, v_g_out_fox),
             ("g_out_sb", g_out_sb, g_g_sb, m_g_out_sb, v_g_out_sb), ("g_mlp", g_mlp, g_g_mlp, m_g_mlp, v_g_mlp),
             ("conv_b", conv_b, g_cb, m_conv_b, v_conv_b), ("g_final", g_final, g_g_final, m_g_final, v_g_final)]
    ssz = [int(np.prod(s[1].shape)) for s in small]
    n_small = sum(ssz)
    lanes_small = -(-n_small // (8 * LANES)) * LANES
    packs = [jnp.pad(jnp.concatenate([s[k].reshape(-1) for s in small]), (0, 8 * lanes_small - n_small)).reshape(8, lanes_small)
             for k in (1, 2, 3, 4)]
    dl_s, nm_s, nv_s = _adamw(*packs, "adamw_small")
    so_ = np.concatenate([[0], np.cumsum(ssz)])
    for k, s in enumerate(small):
        cut = lambda a: a.reshape(-1)[int(so_[k]):int(so_[k + 1])].reshape(s[1].shape)
        grads[s[0]], deltas[s[0]], new_m[s[0]], new_v[s[0]] = s[2].reshape(s[1].shape), cut(dl_s), cut(nm_s), cut(nv_s)

    order = ["w_ada", "b_ada", "g_attn", "w_in", "b_fgate", "g_out_fox", "g_out_sb", "w_out", "g_mlp", "w_up",
             "conv_w", "conv_b", "w_down", "g_final"]
    return (loss, grad_x[None], *[grads[n] for n in order], *[deltas[n] for n in order],
            *[new_m[n] for n in order], *[new_v[n] for n in order])
```

```python
import functools

import numpy as np
import jax
import jax.numpy as jnp
from jax import lax
from jax.experimental import pallas as pl
from jax.experimental.pallas import tpu as pltpu

F32 = jnp.float32
BF16 = jnp.bfloat16
MESH = pl.DeviceIdType.MESH

HEAD_DIM = 64
LANES = 128
EPS = 1e-6
NEG = -1e30
ADAM_LR, ADAM_B1, ADAM_B2, ADAM_EPS, ADAM_WD, ADAM_STEP = 0.001, 0.9, 0.999, 1e-08, 0.01, 10
V7X_VMEM_BYTES = 64 * 1024 * 1024
VMEM_LIMIT = V7X_VMEM_BYTES - 12 * 1024 * 1024
NT_DIMS = (((1,), (1,)), ((), ()))
LOG2_E = 1.4426950408889634


def _pcall(body, **kw):
    return pl.pallas_call(body, **kw)


def _params(sem=None, **kw):
    return pltpu.CompilerParams(dimension_semantics=sem, vmem_limit_bytes=VMEM_LIMIT, **kw)


def _split_dot(x, m, passes):
    acc = None
    for _ in range(passes):
        part = x.astype(BF16)
        d = jnp.dot(part, m, preferred_element_type=F32)
        acc = d if acc is None else acc + d
        x = x - part.astype(F32)
    return acc


def _tile(n, candidates):
    for t in candidates:
        if n % t == 0:
            return t
    return n


def _rows_tile(rows, row_bytes, budget=2 * 1024 * 1024):
    best = None
    for t in range(8, rows + 1, 8):
        if rows % t == 0 and t * row_bytes <= budget:
            best = t
    return best if best is not None else rows


def _all_gather8(v):
    m_per, n = v.shape

    def body(x_ref, out_ref, send_sems, recv_sems, local_sem):
        x, y, c = lax.axis_index("x"), lax.axis_index("y"), lax.axis_index("c")
        me, sibling = (x, y, c), (x, y, 1 - c)
        chips = [(1 - x, y), (x, 1 - y), (1 - x, 1 - y)]

        def rows(px, py, pc):
            return out_ref.at[pl.ds((4 * px + 2 * py + pc) * m_per, m_per), :]

        def copy(k, block, to, src=None):
            return pltpu.make_async_remote_copy(
                src_ref=rows(*block) if src is None else src, dst_ref=rows(*block),
                send_sem=send_sems.at[k], recv_sem=recv_sems.at[k], device_id=to, device_id_type=MESH)

        mine = pltpu.make_async_copy(x_ref, rows(*me), local_sem)
        mine.start()
        first = [copy(0, me, sibling, src=x_ref)]
        first += [copy(1 + j, me, (*chip, c), src=x_ref) for j, chip in enumerate(chips)]
        for cp in first:
            cp.start()
        passed = [copy(4 + j, (*chip, c), sibling) for j, chip in enumerate(chips)]
        for j, chip in enumerate(chips):
            copy(1 + j, (*chip, c), me).wait_recv()
            passed[j].start()
        copy(0, sibling, me).wait_recv()
        for j, chip in enumerate(chips):
            copy(4 + j, (*chip, 1 - c), me).wait_recv()
        for cp in first + passed:
            cp.wait_send()
        mine.wait()

    return _pcall(
        body, name="all_gather8",
        out_shape=jax.ShapeDtypeStruct((8 * m_per, n), v.dtype),
        in_specs=[pl.BlockSpec(memory_space=pltpu.VMEM)],
        out_specs=pl.BlockSpec(memory_space=pltpu.VMEM),
        scratch_shapes=[pltpu.SemaphoreType.DMA((7,)), pltpu.SemaphoreType.DMA((7,)), pltpu.SemaphoreType.DMA],
        compiler_params=pltpu.CompilerParams(vmem_limit_bytes=VMEM_LIMIT),
    )(v)


def _gather_xy(shards):
    n = len(shards)

    def body(*refs):
        ins, outs = refs[:n], refs[n:2 * n]
        send_sems, recv_sems, local_sems = refs[2 * n:]
        x, y, c = lax.axis_index("x"), lax.axis_index("y"), lax.axis_index("c")
        chips = [(1 - x, y), (x, 1 - y), (1 - x, 1 - y)]
        mine = 2 * x + y
        local, remote = [], []
        for w in range(n):
            cp = pltpu.make_async_copy(ins[w], outs[w].at[mine], local_sems.at[w])
            cp.start()
            local.append(cp)
            for k, (px, py) in enumerate(chips):
                cp = pltpu.make_async_remote_copy(
                    src_ref=ins[w], dst_ref=outs[w].at[mine], send_sem=send_sems.at[3 * w + k],
                    recv_sem=recv_sems.at[3 * w + k], device_id=(px, py, c), device_id_type=MESH)
                cp.start()
                remote.append(cp)
        for cp in remote:
            cp.wait_recv()
        for cp in remote:
            cp.wait_send()
        for cp in local:
            cp.wait()

    hbm = pl.BlockSpec(memory_space=pltpu.HBM)
    return _pcall(
        body, name="gather_xy",
        out_shape=[jax.ShapeDtypeStruct((4,) + s.shape, s.dtype) for s in shards],
        in_specs=[hbm] * n, out_specs=[hbm] * n,
        scratch_shapes=[pltpu.SemaphoreType.DMA((3 * n,)), pltpu.SemaphoreType.DMA((3 * n,)),
                        pltpu.SemaphoreType.DMA((n,))],
        compiler_params=pltpu.CompilerParams(vmem_limit_bytes=VMEM_LIMIT),
    )(*shards)


def _scatter8(pieces):
    n = len(pieces)

    def body(*refs):
        ins, outs = refs[:n], refs[n:2 * n]
        send_sems, recv_sems, local_sems = refs[2 * n:]
        x, y, c = lax.axis_index("x"), lax.axis_index("y"), lax.axis_index("c")
        me = 4 * x + 2 * y + c
        local, remote = [], []
        for w in range(n):
            cp = pltpu.make_async_copy(ins[w].at[me], outs[w].at[me], local_sems.at[w])
            cp.start()
            local.append(cp)
            for f in range(1, 8):
                px = 1 - x if f & 4 else x
                py = 1 - y if f & 2 else y
                pc = 1 - c if f & 1 else c
                cp = pltpu.make_async_remote_copy(
                    src_ref=ins[w].at[4 * px + 2 * py + pc], dst_ref=outs[w].at[me],
                    send_sem=send_sems.at[7 * w + f - 1], recv_sem=recv_sems.at[7 * w + f - 1],
                    device_id=(px, py, pc), device_id_type=MESH)
                cp.start()
                remote.append(cp)
        for cp in remote:
            cp.wait_recv()
        for cp in remote:
            cp.wait_send()
        for cp in local:
            cp.wait()

    hbm = pl.BlockSpec(memory_space=pltpu.HBM)
    return _pcall(
        body, name="scatter8",
        out_shape=[jax.ShapeDtypeStruct(p.shape, p.dtype) for p in pieces],
        in_specs=[hbm] * n, out_specs=[hbm] * n,
        scratch_shapes=[pltpu.SemaphoreType.DMA((7 * n,)), pltpu.SemaphoreType.DMA((7 * n,)),
                        pltpu.SemaphoreType.DMA((n,))],
        compiler_params=pltpu.CompilerParams(vmem_limit_bytes=VMEM_LIMIT),
    )(*pieces)


def _swap_halves(halves):
    n = len(halves)
    chunks = 8
    n_chunks = [max(k for k in (chunks, 4, 2, 1) if h.shape[0] % (8 * k) == 0) for h in halves]

    def body(*refs):
        ins, outs = refs[:n], refs[n:2 * n]
        send_sems, recv_sems, local_sems = refs[2 * n:]
        x, y, c = lax.axis_index("x"), lax.axis_index("y"), lax.axis_index("c")
        local, remote = [], []
        for w in range(n):
            cp = pltpu.make_async_copy(ins[w], outs[w].at[c], local_sems.at[w])
            cp.start()
            local.append(cp)
            rows = ins[w].shape[0] // n_chunks[w]
            for k in range(n_chunks[w]):
                cp = pltpu.make_async_remote_copy(
                    src_ref=ins[w].at[pl.ds(k * rows, rows)], dst_ref=outs[w].at[c, pl.ds(k * rows, rows)],
                    send_sem=send_sems.at[chunks * w + k], recv_sem=recv_sems.at[chunks * w + k],
                    device_id=(x, y, 1 - c), device_id_type=MESH)
                cp.start()
                remote.append(cp)
        for cp in remote:
            cp.wait_recv()
        for cp in remote:
            cp.wait_send()
        for cp in local:
            cp.wait()

    hbm = pl.BlockSpec(memory_space=pltpu.HBM)
    return _pcall(
        body, name="swap_halves",
        out_shape=[jax.ShapeDtypeStruct((2,) + h.shape, h.dtype) for h in halves],
        in_specs=[hbm] * n, out_specs=[hbm] * n,
        scratch_shapes=[pltpu.SemaphoreType.DMA((chunks * n,)), pltpu.SemaphoreType.DMA((chunks * n,)),
                        pltpu.SemaphoreType.DMA((n,))],
        compiler_params=pltpu.CompilerParams(vmem_limit_bytes=VMEM_LIMIT),
    )(*halves)


def _sum_leading(a, name):
    n, r, c = a.shape
    tr = _rows_tile(r, n * c * 4, budget=6 * 1024 * 1024)
    if a.dtype == BF16 and tr % 16:
        tr = r

    def body(a_ref, o_ref):
        acc = a_ref[0].astype(F32)
        for k in range(1, n):
            acc = acc + a_ref[k].astype(F32)
        o_ref[...] = acc

    return _pcall(
        body, name=name, grid=(r // tr,),
        out_shape=jax.ShapeDtypeStruct((r, c), F32),
        in_specs=[pl.BlockSpec((n, tr, c), lambda i: (0, i, 0))],
        out_specs=pl.BlockSpec((tr, c), lambda i: (i, 0)),
        compiler_params=_params(("arbitrary",)),
    )(a)


def _to_bf16(a, name):
    n, r, c = a.shape

    def body(a_ref, o_ref):
        o_ref[...] = a_ref[...].astype(BF16)

    spec = pl.BlockSpec((1, r, c), lambda i: (i, 0, 0))
    return _pcall(
        body, name=name, grid=(n,), out_shape=jax.ShapeDtypeStruct(a.shape, BF16),
        in_specs=[spec], out_specs=spec, compiler_params=_params(("arbitrary",)),
    )(a)


def _adamw(w, g, m, v, name):
    r, c = w.shape
    tr = _rows_tile(r, c * 4, budget=1024 * 1024)
    c1 = 1.0 - ADAM_B1 ** ADAM_STEP
    c2 = 1.0 - ADAM_B2 ** ADAM_STEP

    def body(w_ref, g_ref, m_ref, v_ref, d_ref, nm_ref, nv_ref):
        gg = g_ref[...]
        nm = ADAM_B1 * m_ref[...] + (1.0 - ADAM_B1) * gg
        nv = ADAM_B2 * v_ref[...] + (1.0 - ADAM_B2) * (gg * gg)
        m_hat = nm / c1
        v_hat = nv / c2
        d_ref[...] = -ADAM_LR * (m_hat / (jnp.sqrt(v_hat) + ADAM_EPS) + ADAM_WD * w_ref[...])
        nm_ref[...] = nm
        nv_ref[...] = nv

    spec = pl.BlockSpec((tr, c), lambda i: (i, 0))
    return _pcall(
        body, name=name, grid=(r // tr,),
        out_shape=[jax.ShapeDtypeStruct((r, c), F32)] * 3,
        in_specs=[spec] * 4, out_specs=[spec] * 3,
        compiler_params=_params(("arbitrary",)),
    )(w, g, m, v)


def _ada_fwd(c_all, w_shard, b_shard):
    nb, d = c_all.shape
    cols = w_shard.shape[1]

    def body(c_ref, w_ref, b_ref, sc_ref, mod_ref):
        cv = c_ref[...]
        sc = cv * jax.nn.sigmoid(cv)
        sc_ref[...] = sc
        mod_ref[...] = jnp.dot(sc.astype(BF16), w_ref[...].astype(BF16), preferred_element_type=F32) + b_ref[...]

    return _pcall(
        body, name="ada_fwd",
        out_shape=[jax.ShapeDtypeStruct((nb, d), F32), jax.ShapeDtypeStruct((nb, cols), F32)],
        compiler_params=pltpu.CompilerParams(vmem_limit_bytes=VMEM_LIMIT),
    )(c_all, w_shard, b_shard)


def _ada_bwd(sc_t, dmod_cols):
    d, nb = sc_t.shape
    cols = dmod_cols.shape[1]
    tr = _rows_tile(d, cols * 4, budget=1024 * 1024)

    def body(s_ref, m_ref, o_ref):
        s = s_ref[...]
        m = m_ref[...]
        acc = s[:, 0:1] * m[0:1, :]
        for b in range(1, nb):
            acc = acc + s[:, b:b + 1] * m[b:b + 1, :]
        o_ref[...] = acc

    return _pcall(
        body, name="ada_bwd", grid=(d // tr,),
        out_shape=jax.ShapeDtypeStruct((d, cols), F32),
        in_specs=[pl.BlockSpec((tr, nb), lambda i: (i, 0)), pl.BlockSpec((nb, cols), lambda i: (0, 0))],
        out_specs=pl.BlockSpec((tr, cols), lambda i: (i, 0)),
        compiler_params=_params(("arbitrary",)),
    )(sc_t, dmod_cols)


def _log_sigmoid(x):
    return jnp.minimum(x, 0.0) - jnp.log1p(jnp.exp(-jnp.abs(x)))


def _fgate_fwd(fl2d, b_rows, tri_in, tri_blk):
    r = fl2d.shape[0]

    def body(x_ref, b_ref, u_ref, l_ref, f_ref):
        lf = _log_sigmoid(x_ref[...] + b_ref[...])
        c1 = _split_dot(lf, u_ref[...], 3)
        tot = jnp.broadcast_to(c1[:, LANES - 1:LANES], (r, LANES))
        acc = None
        for _ in range(3):
            part = tot.astype(BF16)
            dd = jnp.dot(l_ref[...], part, preferred_element_type=F32)
            acc = dd if acc is None else acc + dd
            tot = tot - part.astype(F32)
        f_ref[...] = c1 + acc

    return _pcall(
        body, name="fgate_fwd", out_shape=jax.ShapeDtypeStruct((r, LANES), F32),
        compiler_params=pltpu.CompilerParams(vmem_limit_bytes=VMEM_LIMIT),
    )(fl2d, b_rows, tri_in, tri_blk)


def _fgate_bwd(fl2d, b_rows, df_query, df_key, tri_in_rev, tri_blk_rev, head_rows):
    r = fl2d.shape[0]
    nhp = head_rows.shape[0]

    def body(x_ref, b_ref, dq_ref, dk_ref, u_ref, l_ref, hr_ref, o_ref, gb_ref):
        c1 = _split_dot(dq_ref[...] + dk_ref[...], u_ref[...], 3)
        tot = jnp.broadcast_to(c1[:, 0:1], (r, LANES))
        acc = None
        for _ in range(3):
            part = tot.astype(BF16)
            dd = jnp.dot(l_ref[...], part, preferred_element_type=F32)
            acc = dd if acc is None else acc + dd
            tot = tot - part.astype(F32)
        x = x_ref[...] + b_ref[...]
        e = jnp.exp(-jnp.abs(x))
        dfl = (c1 + acc) * (jnp.where(x >= 0, e, 1.0) / (1.0 + e))
        o_ref[...] = dfl
        rs = jnp.broadcast_to(jnp.sum(dfl, axis=1, keepdims=True), (r, LANES))
        gb = None
        for _ in range(3):
            part = rs.astype(BF16)
            dd = jnp.dot(hr_ref[...], part, preferred_element_type=F32)
            gb = dd if gb is None else gb + dd
            rs = rs - part.astype(F32)
        gb_ref[...] = gb

    return _pcall(
        body, name="fgate_bwd",
        out_shape=[jax.ShapeDtypeStruct((r, LANES), F32), jax.ShapeDtypeStruct((nhp, LANES), F32)],
        compiler_params=pltpu.CompilerParams(vmem_limit_bytes=VMEM_LIMIT),
    )(fl2d, b_rows, df_query, df_key, tri_in_rev, tri_blk_rev, head_rows)


def _norm_mod(x, g, scale, shift):
    r = lax.rsqrt(jnp.mean(x * x, axis=-1, keepdims=True) + EPS)
    return (x * r * g) * (1.0 + scale) + shift


def _norm_mod_bwd(x, dh, g, scale):
    r = lax.rsqrt(jnp.mean(x * x, axis=-1, keepdims=True) + EPS)
    xn = x * r
    dshift = jnp.sum(dh, axis=0, keepdims=True)
    dscale = jnp.sum(dh * (xn * g), axis=0, keepdims=True)
    dxn_g = dh * (1.0 + scale)
    dg = jnp.sum(dxn_g * xn, axis=0, keepdims=True)
    dxn = dxn_g * g
    dx = r * (dxn - xn * jnp.mean(dxn * xn, axis=-1, keepdims=True))
    return dx, dshift, dscale, dg


def _in_proj_fwd(x, mod8, g_attn, w_qkv, w_f, tm):
    t, d = x.shape
    dg = w_qkv.shape[1] // 6

    def body(x_ref, mod_ref, g_ref, w_ref, wf_ref, qkv_ref, fl_ref, h1_ref):
        h = _norm_mod(x_ref[...], g_ref[...], mod_ref[1:2, :], mod_ref[0:1, :]).astype(BF16)
        h1_ref[...] = _transposed(h)
        fl_ref[...] = lax.dot_general(wf_ref[...], h, NT_DIMS, preferred_element_type=F32)
        for k in range(6):
            cols = slice(k * dg, (k + 1) * dg)
            y = jnp.dot(h, w_ref[:, cols], preferred_element_type=F32)
            qkv_ref[:, cols] = (y * HEAD_DIM ** -0.5 if k in (0, 3) else y).astype(BF16)

    once = lambda a: pl.BlockSpec(a.shape, lambda i: (0,) * a.ndim, pipeline_mode=pl.Buffered(1))
    return _pcall(
        body, name="in_proj_fwd", grid=(t // tm,),
        out_shape=[jax.ShapeDtypeStruct((t, 6 * dg), BF16), jax.ShapeDtypeStruct((LANES, t), F32),
                   jax.ShapeDtypeStruct((d, t), BF16)],
        in_specs=[pl.BlockSpec((tm, d), lambda i: (i, 0)), pl.BlockSpec((8, d), lambda i: (0, 0)),
                  pl.BlockSpec((1, d), lambda i: (0, 0)), once(w_qkv), once(w_f)],
        out_specs=[pl.BlockSpec((tm, 6 * dg), lambda i: (i, 0)), pl.BlockSpec((LANES, tm), lambda i: (0, i)),
                   pl.BlockSpec((d, tm), lambda i: (0, i))],
        compiler_params=_params(("arbitrary",)),
    )(x, mod8, g_attn, w_qkv, w_f)


def _head_rstd(o, bd):
    return lax.rsqrt(_split_dot(o * o, bd, 3) * (1.0 / HEAD_DIM) + EPS)


def _attn_out_fwd(x, o_fox, o_sb, g_fox, g_sb, w_out, mod8, bd, tm):
    t, d = x.shape
    dg = o_fox.shape[0]

    def body(x_ref, of_ref, os_ref, gf_ref, gs_ref, w_ref, mod_ref, bd_ref, x2_ref, mix_ref, mixt_ref):
        of, osb = of_ref[...].T, os_ref[...].T
        mf = (of * _head_rstd(of, bd_ref[...]) * gf_ref[...]).astype(BF16)
        ms = (osb * _head_rstd(osb, bd_ref[...]) * gs_ref[...]).astype(BF16)
        mix_ref[:, :dg] = mf
        mix_ref[:, dg:] = ms
        mixt_ref[:dg, :] = _transposed(mf)
        mixt_ref[dg:, :] = _transposed(ms)
        y = jnp.dot(mf, w_ref[:dg, :], preferred_element_type=F32) + jnp.dot(ms, w_ref[dg:, :], preferred_element_type=F32)
        x2_ref[...] = x_ref[...] + mod_ref[2:3, :] * y

    row = lambda w: pl.BlockSpec((tm, w), lambda i: (i, 0))
    full = lambda a: pl.BlockSpec(a.shape, lambda i: (0,) * a.ndim)
    return _pcall(
        body, name="attn_out_fwd", grid=(t // tm,),
        out_shape=[jax.ShapeDtypeStruct((t, d), F32), jax.ShapeDtypeStruct((t, 2 * dg), BF16),
                   jax.ShapeDtypeStruct((2 * dg, t), BF16)],
        in_specs=[row(d), pl.BlockSpec((dg, tm), lambda i: (0, i)), pl.BlockSpec((dg, tm), lambda i: (0, i)),
                  full(g_fox), full(g_sb), full(w_out), full(mod8), full(bd)],
        out_specs=[row(d), row(2 * dg), pl.BlockSpec((2 * dg, tm), lambda i: (0, i))],
        compiler_params=_params(("arbitrary",)),
    )(x, o_fox, o_sb, g_fox, g_sb, w_out, mod8, bd)


def _attn_out_bwd(dx2, mix, o_fox, o_sb, g_fox, g_sb, w_out, mod8, bd, hsel, tm):
    t, d = dx2.shape
    dg = o_fox.shape[0]

    def body(dx_ref, mix_ref, of_ref, os_ref, gf_ref, gs_ref, w_ref, mod_ref, bd_ref, hs_ref,
             dof_ref, dos_ref, dlt_ref, dxg_ref, part_ref):
        dx = dx_ref[...]
        gate = mod_ref[2:3, :]
        dxg = (dx * gate).astype(BF16)
        dxg_ref[...] = dxg
        mixv = mix_ref[...]
        y = jnp.dot(mixv[:, :dg], w_ref[:dg, :], preferred_element_type=F32)
        y = y + jnp.dot(mixv[:, dg:], w_ref[dg:, :], preferred_element_type=F32)
        part_ref[0] = jnp.zeros((8, d), F32)
        part_ref[0, 0:1, :] = jnp.sum(dx * y, axis=0, keepdims=True)
        for grp, (o_ref, g_ref, do_ref) in enumerate(((of_ref, gf_ref, dof_ref), (os_ref, gs_ref, dos_ref))):
            dmix = lax.dot_general(dxg, w_ref[grp * dg:(grp + 1) * dg, :], NT_DIMS, preferred_element_type=F32)
            o = o_ref[...].T
            r = _head_rstd(o, bd_ref[...])
            n = o * r
            part_ref[0, 1:2, grp * dg:(grp + 1) * dg] = jnp.sum(dmix * n, axis=0, keepdims=True)
            dn = dmix * g_ref[...]
            mh = _split_dot(dn * n, bd_ref[...], 3) * (1.0 / HEAD_DIM)
            do = r * (dn - n * mh)
            do_ref[...] = do.astype(BF16)
            if grp == 0:
                prod, dlt = do * o, None
                for _ in range(3):
                    part = prod.astype(BF16)
                    term = lax.dot_general(hs_ref[...], part, NT_DIMS, preferred_element_type=F32)
                    dlt = term if dlt is None else dlt + term
                    prod = prod - part.astype(F32)
                dlt_ref[...] = dlt

    row = lambda w: pl.BlockSpec((tm, w), lambda i: (i, 0))
    full = lambda a: pl.BlockSpec(a.shape, lambda i: (0,) * a.ndim)
    nt = t // tm
    return _pcall(
        body, name="attn_out_bwd", grid=(nt,),
        out_shape=[jax.ShapeDtypeStruct((t, dg), BF16), jax.ShapeDtypeStruct((t, dg), BF16),
                   jax.ShapeDtypeStruct((LANES, t), F32), jax.ShapeDtypeStruct((t, d), BF16),
                   jax.ShapeDtypeStruct((nt, 8, d), F32)],
        in_specs=[row(d), row(2 * dg), pl.BlockSpec((dg, tm), lambda i: (0, i)), pl.BlockSpec((dg, tm), lambda i: (0, i)),
                  full(g_fox), full(g_sb), full(w_out), full(mod8),
                  full(bd), full(hsel)],
        out_specs=[row(dg), row(dg), pl.BlockSpec((LANES, tm), lambda i: (0, i)), row(d), pl.BlockSpec((1, 8, d), lambda i: (i, 0, 0))],
        compiler_params=_params(("arbitrary",)),
    )(dx2, mix, o_fox, o_sb, g_fox, g_sb, w_out, mod8, bd, hsel)


def _in_proj_bwd(dparts, dfl, w_qkv, w_f, x, dx2, mod8, g_attn, tm):
    t, d = x.shape
    dg = dparts[1].shape[1]

    def body(*refs):
        d_refs = refs[:6]
        dfl_ref, w_ref, wf_ref, x_ref, dx2_ref, mod_ref, g_ref, gx_ref, dp_ref, dflb_ref, part_ref = refs[6:]
        dh = None
        for k in range(6):
            dk = d_refs[k][...].T if k in (0, 3) else d_refs[k][...]
            if k in (0, 3):
                dk = dk * HEAD_DIM ** -0.5
            db = dk.astype(BF16)
            dp_ref[:, k * dg:(k + 1) * dg] = db
            term = lax.dot_general(db, w_ref[:, k * dg:(k + 1) * dg], NT_DIMS, preferred_element_type=F32)
            dh = term if dh is None else dh + term
        dfb = dfl_ref[...].T.astype(BF16)
        dflb_ref[...] = dfb
        dh = dh + jnp.dot(dfb, wf_ref[...], preferred_element_type=F32)
        dx, dshift, dscale, dgn = _norm_mod_bwd(x_ref[...], dh, g_ref[...], mod_ref[1:2, :])
        gx_ref[...] = dx2_ref[...] + dx
        part_ref[0] = jnp.zeros((8, d), F32)
        part_ref[0, 0:1, :] = dshift
        part_ref[0, 1:2, :] = dscale
        part_ref[0, 2:3, :] = dgn

    row = lambda w: pl.BlockSpec((tm, w), lambda i: (i, 0))
    full = lambda a: pl.BlockSpec(a.shape, lambda i: (0,) * a.ndim)
    nt = t // tm
    return _pcall(
        body, name="in_proj_bwd", grid=(nt,),
        out_shape=[jax.ShapeDtypeStruct((t, d), F32), jax.ShapeDtypeStruct((t, 6 * dg), BF16),
                   jax.ShapeDtypeStruct((t, LANES), BF16), jax.ShapeDtypeStruct((nt, 8, d), F32)],
        in_specs=[pl.BlockSpec((dg, tm), lambda i: (0, i)), row(dg), row(dg)] * 2
        + [pl.BlockSpec((LANES, tm), lambda i: (0, i)), full(w_qkv), full(w_f), row(d), row(d), full(mod8), full(g_attn)],
        out_specs=[row(d), row(6 * dg), row(LANES), pl.BlockSpec((1, 8, d), lambda i: (i, 0, 0))],
        compiler_params=_params(("arbitrary",)),
    )(*dparts, dfl, w_qkv, w_f, x, dx2, mod8, g_attn)


def _matmul_tn(a_t, b, name):
    m, t = a_t.shape
    n = b.shape[1]
    tm_ = _tile(m, (1408, 1024, 512, 256, 128))
    tn_ = _tile(n, (1408, 1024, 512, 256, 128))
    tk = _tile(t, (1024, 512, 256, 128))
    nk = t // tk

    def body(a_ref, b_ref, o_ref):
        k = pl.program_id(2)

        @pl.when(k == 0)
        def _():
            o_ref[...] = jnp.zeros_like(o_ref)

        o_ref[...] += jnp.dot(a_ref[...], b_ref[...], preferred_element_type=F32)

    return _pcall(
        body, name=name, grid=(m // tm_, n // tn_, nk),
        out_shape=jax.ShapeDtypeStruct((m, n), F32),
        in_specs=[pl.BlockSpec((tm_, tk), lambda i, j, k: (i, k)), pl.BlockSpec((tk, tn_), lambda i, j, k: (k, j))],
        out_specs=pl.BlockSpec((tm_, tn_), lambda i, j, k: (i, j)),
        compiler_params=_params(("arbitrary", "arbitrary", "arbitrary")),
    )(a_t, b)


HALO = 16


def _conv_taps(up_ext, cw, lo, rows):
    s1 = pltpu.roll(up_ext, 1, 0)
    s2 = pltpu.roll(up_ext, 2, 0)
    u = cw[2:3, :] * up_ext[lo:lo + rows] + cw[1:2, :] * s1[lo:lo + rows] + cw[0:1, :] * s2[lo:lo + rows] + cw[3:4, :]
    return u, s1, s2


def _chunk_major(w, cf):
    d, n = w.shape[0], w.shape[1] // cf
    return jnp.transpose(w.reshape(d, n, cf), (1, 0, 2))


def _ffn_fwd(x2, target, mod8, g_mlp, g_final, wg, wv, cwg, cwv, wd, tm, cf):
    t, d = x2.shape
    dfp = wg.shape[1]
    nt, nc = t // tm, dfp // cf
    hb = tm // HALO
    wg_c, wv_c = _chunk_major(wg, cf), _chunk_major(wv, cf)

    def body(x_ref, xp_ref, tg_ref, mod_ref, g_ref, gf_ref, wg_ref, wv_ref, cg_ref, cv_ref, wd_ref,
             dx3_ref, h2_ref, part_ref, act_sc):
        i = pl.program_id(0)
        xe = jnp.concatenate([xp_ref[...], x_ref[...]], axis=0)
        h = _norm_mod(xe, g_ref[...], mod_ref[4:5, :], mod_ref[3:4, :]).astype(BF16)
        h2_ref[...] = _transposed(h[HALO:])
        first = jnp.where(i > 0, h[:HALO], jnp.zeros_like(h[:HALO]))
        h = jnp.concatenate([first, h[HALO:]], axis=0)

        def up(c):
            return (jnp.dot(h, wg_ref[c], preferred_element_type=F32), jnp.dot(h, wv_ref[c], preferred_element_type=F32))

        def activation(c, ups):
            cols = slice(c * cf, (c + 1) * cf)
            ug, _, _ = _conv_taps(ups[0], cg_ref[:, cols], HALO, tm)
            uv, _, _ = _conv_taps(ups[1], cv_ref[:, cols], HALO, tm)
            act_sc[:, cols] = (ug * jax.nn.sigmoid(ug) * uv).astype(BF16)

        for c0 in range(0, nc, 2):
            group = list(range(c0, min(c0 + 2, nc)))
            ups = [up(c) for c in group]
            for c, u in zip(group, ups):
                activation(c, u)

        y_ffn = jnp.dot(act_sc[...], wd_ref[...], preferred_element_type=F32)
        x3 = x_ref[...] + mod_ref[5:6, :] * y_ffn
        r3 = lax.rsqrt(jnp.mean(x3 * x3, axis=-1, keepdims=True) + EPS)
        xn = x3 * r3
        gf = gf_ref[...]
        diff = xn * gf - tg_ref[...]
        dy = diff * (1.0 / d)
        dxn = dy * gf
        dx3 = r3 * (dxn - xn * jnp.mean(dxn * xn, axis=-1, keepdims=True))
        dx3_ref[...] = dx3
        part_ref[0] = jnp.zeros((8, d), F32)
        part_ref[0, 0:1, :] = jnp.sum(dy * xn, axis=0, keepdims=True)
        part_ref[0, 1:2, :] = jnp.sum(dx3 * y_ffn, axis=0, keepdims=True)
        part_ref[0, 2:3, :] = jnp.sum(diff * diff, axis=0, keepdims=True) * (0.5 / d)

    row = lambda w: pl.BlockSpec((tm, w), lambda i: (i, 0))
    full = lambda a: pl.BlockSpec(a.shape, lambda i: (0,) * a.ndim)
    once = lambda a: pl.BlockSpec(a.shape, lambda i: (0,) * a.ndim, pipeline_mode=pl.Buffered(1))
    return _pcall(
        body, name="ffn_fwd", grid=(nt,),
        out_shape=[jax.ShapeDtypeStruct((t, d), F32), jax.ShapeDtypeStruct((d, t), BF16),
                   jax.ShapeDtypeStruct((nt, 8, d), F32)],
        in_specs=[row(d), pl.BlockSpec((HALO, d), lambda i: (jnp.maximum(i * hb - 1, 0), 0)), row(d),
                  full(mod8), full(g_mlp), full(g_final), once(wg_c), once(wv_c), once(cwg), once(cwv), once(wd)],
        out_specs=[row(d), pl.BlockSpec((d, tm), lambda i: (0, i)), pl.BlockSpec((1, 8, d), lambda i: (i, 0, 0))],
        scratch_shapes=[pltpu.VMEM((tm, dfp), BF16)],
        compiler_params=_params(("arbitrary",)),
    )(x2, x2, target, mod8, g_mlp, g_final, wg_c, wv_c, cwg, cwv, wd)


def _ffn_bwd(x2, dx3, mod8, g_mlp, wg, wv, cwg, cwv, wd, tm, cf):
    t, d = x2.shape
    dfp = wg.shape[1]
    nt, nc = t // tm, dfp // cf
    hb = tm // HALO
    nhb = t // HALO
    n = tm + HALO

    def body(x_ref, xp_ref, xn_ref, dx_ref, dxn_ref, mod_ref, g_ref, wg_ref, wv_ref, cg_ref, cv_ref, wd_ref,
             dx2_ref, dug_ref, duv_ref, act_ref, dxg_ref, part_ref, pcg_ref, pcv_ref):
        i = pl.program_id(0)
        xe = jnp.concatenate([xp_ref[...], x_ref[...], xn_ref[...]], axis=0)
        h = _norm_mod(xe, g_ref[...], mod_ref[4:5, :], mod_ref[3:4, :]).astype(BF16)
        h = jnp.concatenate([jnp.where(i > 0, h[:HALO], jnp.zeros_like(h[:HALO])), h[HALO:]], axis=0)
        dx = dx_ref[...] * mod_ref[5:6, :]
        dxn = jnp.where(i < nt - 1, dxn_ref[...] * mod_ref[5:6, :], 0.0)
        de = jnp.concatenate([dx, dxn], axis=0).astype(BF16)
        dxg_ref[...] = de[:tm]
        pcg_ref[0] = jnp.zeros((8, dfp), F32)
        pcv_ref[0] = jnp.zeros((8, dfp), F32)

        def products(c):
            cols = slice(c * cf, (c + 1) * cf)
            return (jnp.dot(h, wg_ref[:, cols], preferred_element_type=F32), jnp.dot(h, wv_ref[:, cols], preferred_element_type=F32),
                    lax.dot_general(de, wd_ref[cols, :], NT_DIMS, preferred_element_type=F32))

        def back(du, cw, up, s1, s2, pc_ref, cols):
            dup = (cw[2:3, :] * du + cw[1:2, :] * pltpu.roll(du, n - 1, 0) + cw[0:1, :] * pltpu.roll(du, n - 2, 0))[:tm]
            dut = du[:tm]
            pc_ref[0, 0:1, cols] = jnp.sum(dut * s2[HALO:HALO + tm], axis=0, keepdims=True)
            pc_ref[0, 1:2, cols] = jnp.sum(dut * s1[HALO:HALO + tm], axis=0, keepdims=True)
            pc_ref[0, 2:3, cols] = jnp.sum(dut * up[HALO:HALO + tm], axis=0, keepdims=True)
            pc_ref[0, 3:4, cols] = jnp.sum(dut, axis=0, keepdims=True)
            return dup.astype(BF16)

        def chunk(c, prods):
            cols = slice(c * cf, (c + 1) * cf)
            upg, upv, dact = prods
            cg, cv = cg_ref[:, cols], cv_ref[:, cols]
            ug, g1, g2 = _conv_taps(upg, cg, HALO, n)
            uv, v1, v2 = _conv_taps(upv, cv, HALO, n)
            sg = jax.nn.sigmoid(ug)
            sil = ug * sg
            act_ref[cols, :] = _transposed((sil * uv)[:tm].astype(BF16))
            dug_ref[:, cols] = back(dact * uv * (sg * (1.0 + ug * (1.0 - sg))), cg, upg, g1, g2, pcg_ref, cols)
            duv_ref[:, cols] = back(dact * sil, cv, upv, v1, v2, pcv_ref, cols)

        for c0 in range(0, nc, 2):
            group = list(range(c0, min(c0 + 2, nc)))
            prods = [products(c) for c in group]
            for c, pr in zip(group, prods):
                chunk(c, pr)

        dh = (lax.dot_general(dug_ref[...], wg_ref[...], NT_DIMS, preferred_element_type=F32)
              + lax.dot_general(duv_ref[...], wv_ref[...], NT_DIMS, preferred_element_type=F32))
        dxt, dshift, dscale, dgn = _norm_mod_bwd(x_ref[...], dh, g_ref[...], mod_ref[4:5, :])
        dx2_ref[...] = dx_ref[...] + dxt
        part_ref[0] = jnp.zeros((8, d), F32)
        part_ref[0, 0:1, :] = dshift
        part_ref[0, 1:2, :] = dscale
        part_ref[0, 2:3, :] = dgn

    row = lambda w: pl.BlockSpec((tm, w), lambda i: (i, 0))
    prev = pl.BlockSpec((HALO, d), lambda i: (jnp.maximum(i * hb - 1, 0), 0))
    nxt = pl.BlockSpec((HALO, d), lambda i: (jnp.minimum((i + 1) * hb, nhb - 1), 0))
    full = lambda a: pl.BlockSpec(a.shape, lambda i: (0,) * a.ndim)
    once = lambda a: pl.BlockSpec(a.shape, lambda i: (0,) * a.ndim, pipeline_mode=pl.Buffered(1))
    part = lambda w: pl.BlockSpec((1, 8, w), lambda i: (i, 0, 0))
    return _pcall(
        body, name="ffn_bwd", grid=(nt,),
        out_shape=[jax.ShapeDtypeStruct((t, d), F32), jax.ShapeDtypeStruct((t, dfp), BF16),
                   jax.ShapeDtypeStruct((t, dfp), BF16), jax.ShapeDtypeStruct((dfp, t), BF16),
                   jax.ShapeDtypeStruct((t, d), BF16), jax.ShapeDtypeStruct((nt, 8, d), F32),
                   jax.ShapeDtypeStruct((nt, 8, dfp), F32), jax.ShapeDtypeStruct((nt, 8, dfp), F32)],
        in_specs=[row(d), prev, nxt, row(d), nxt, full(mod8), full(g_mlp), once(wg), once(wv), once(cwg), once(cwv), once(wd)],
        out_specs=[row(d), row(dfp), row(dfp), pl.BlockSpec((dfp, tm), lambda i: (0, i)), row(d), part(d), part(dfp), part(dfp)],
        compiler_params=_params(("arbitrary",)),
    )(x2, x2, x2, dx3, dx3, mod8, g_mlp, wg, wv, cwg, cwv, wd)


BLK = 2 * LANES
XROWS = 144
LANE_FS, LANE_FT_A, LANE_FT_B = 0, 3, 6


def _head_masks():
    lane = lax.broadcasted_iota(jnp.int32, (1, LANES), 1)
    in_a = lane < HEAD_DIM
    return in_a, jnp.logical_not(in_a)


def _pieces3(x):
    hi = x.astype(BF16).astype(F32)
    r = x - hi
    mid = r.astype(BF16).astype(F32)
    return hi, mid, (r - mid).astype(BF16).astype(F32)


def _bias_lanes(rows, entries):
    sub = lax.broadcasted_iota(jnp.int32, (16, 1), 0)
    out = jnp.zeros((16, rows), F32)
    for l, v in entries:
        out = jnp.where(sub == l, v, out)
    return jnp.concatenate([out, jnp.zeros((LANES - 16, rows), F32)], axis=0).T


def _three(first, values):
    return [(first + k, v) for k, v in enumerate(values)]


def _stack_rows(x, in_a, in_b):
    zero = jnp.zeros_like(x)
    return jnp.concatenate([jnp.where(in_a, x, zero), jnp.where(in_b, x, zero)], axis=0)


def _transposed(x):
    return x.astype(F32).T.astype(BF16)


def _attn_operands(qkv, frow, dg):
    t = qkv.shape[0]
    p, nk = dg // LANES, t // BLK

    def body(qf_ref, kf_ref, vf_ref, ks_ref, vs_ref, f_ref, qx_ref, kx_ref, kxt_ref, vf_o, vft_o, ks_o, kst_o, vs_o, vst_o):
        in_a, in_b = _head_masks()
        fa, fb = _pieces3(f_ref[0]), _pieces3(f_ref[1])
        qx_ref[0, :, :LANES] = qf_ref[...]
        qx_ref[0, :, LANES:] = _bias_lanes(
            BLK, _three(LANE_FS, (-1.0,) * 3) + _three(LANE_FT_A, fa) + _three(LANE_FT_B, fb)).astype(BF16)
        kf = kf_ref[...]
        zero = jnp.zeros_like(kf)
        top = jnp.concatenate([jnp.where(in_a, kf, zero), _bias_lanes(
            BLK, _three(LANE_FS, fa) + _three(LANE_FT_A, (1.0,) * 3)).astype(BF16)], axis=1)
        bot = jnp.concatenate([jnp.where(in_b, kf, zero), _bias_lanes(
            BLK, _three(LANE_FS, fb) + _three(LANE_FT_B, (1.0,) * 3)).astype(BF16)], axis=1)
        kx = jnp.concatenate([top, bot], axis=0)
        kx_ref[0, 0] = kx
        kxt_ref[0, 0] = _transposed(kx)[:XROWS]
        for src, dst, dst_t in ((vf_ref, vf_o, vft_o), (ks_ref, ks_o, kst_o), (vs_ref, vs_o, vst_o)):
            st = _stack_rows(src[...], in_a, in_b)
            dst[0, 0] = st
            dst_t[0, 0] = _transposed(st)

    col = lambda base: pl.BlockSpec((BLK, LANES), lambda h, j: (j, base * p + h))
    blk4 = lambda r, c: pl.BlockSpec((1, 1, r, c), lambda h, j: (h, j, 0, 0))
    shp4 = lambda r, c: jax.ShapeDtypeStruct((p, nk, r, c), BF16)
    return _pcall(
        body, name="attn_operands", grid=(p, nk),
        out_shape=[jax.ShapeDtypeStruct((p, t, 2 * LANES), BF16), shp4(2 * BLK, 2 * LANES), shp4(XROWS, 2 * BLK)]
        + [shp4(2 * BLK, LANES), shp4(LANES, 2 * BLK)] * 3,
        in_specs=[col(0), col(1), col(2), col(4), col(5), pl.BlockSpec((2, 1, BLK), lambda h, j: (h, 0, j))],
        out_specs=[pl.BlockSpec((1, BLK, 2 * LANES), lambda h, j: (h, j, 0)), blk4(2 * BLK, 2 * LANES), blk4(XROWS, 2 * BLK)]
        + [blk4(2 * BLK, LANES), blk4(LANES, 2 * BLK)] * 3,
        compiler_params=_params(("arbitrary", "arbitrary")),
    )(qkv, qkv, qkv, qkv, qkv, frow)


def _key_query_masks():
    key = lax.broadcasted_iota(jnp.int32, (BLK, BLK), 0)
    qry = lax.broadcasted_iota(jnp.int32, (BLK, BLK), 1)
    return key <= qry, key < qry


def _key_triangle(kind):
    s = lax.broadcasted_iota(jnp.int32, (BLK, BLK), 0)
    j = lax.broadcasted_iota(jnp.int32, (BLK, BLK), 1)
    return {"suffix": j >= s, "prefix": j <= s, "before": j < s}[kind].astype(BF16)


def _tri_dot(tri, x, passes):
    acc = None
    for _ in range(passes):
        part = x.astype(BF16)
        d = jnp.dot(tri, part, preferred_element_type=F32)
        acc = d if acc is None else acc + d
        x = x - part.astype(F32)
    return acc


GROUPS = (4, 2, 1)


def _loop_blocks(n, tiles, carry, descending=False, groups=GROUPS):
    at = (lambda k: n - 1 - k) if descending else (lambda k: k)
    done = 0
    for g in groups:
        left = n - done
        carry = lax.fori_loop(0, left // g, lambda h, c, g=g, done=done: tiles([at(done + g * h + k) for k in range(g)], c), carry)
        done = done + (left // g) * g
    return carry


def _resident(shape):
    return pl.BlockSpec((1,) + shape, lambda h, i: (h,) + (0,) * len(shape), pipeline_mode=pl.Buffered(1))


def _rows_per_head(a, b):
    return jnp.concatenate([jnp.broadcast_to(a, (HEAD_DIM, BLK)), jnp.broadcast_to(b, (HEAD_DIM, BLK))], axis=0)


def _fold_heads(stacked, in_a):
    return jnp.where(in_a, stacked[:BLK], stacked[BLK:])


def _xy_gather_copies(ins, outs, send_sems, recv_sems, local_sems):
    x, y, c = lax.axis_index("x"), lax.axis_index("y"), lax.axis_index("c")
    chips = [(1 - x, y), (x, 1 - y), (1 - x, 1 - y)]
    mine = 2 * x + y
    local, remote = [], []
    for w in range(len(ins)):
        local.append(pltpu.make_async_copy(ins[w], outs[w].at[mine], local_sems.at[w]))
        for k, (px, py) in enumerate(chips):
            remote.append(pltpu.make_async_remote_copy(
                src_ref=ins[w], dst_ref=outs[w].at[mine], send_sem=send_sems.at[3 * w + k],
                recv_sem=recv_sems.at[3 * w + k], device_id=(px, py, c), device_id_type=MESH))
    return local, remote


def _fox_fwd(qx, kx, v_t, dg, shards):
    p, t = qx.shape[0], qx.shape[1]
    nq = t // BLK
    nh = 2 * p
    ns = len(shards)

    def body(q_ref, k_ref, vt_ref, *rest):
        shard_refs, (o_ref, lse_ref), gathered = rest[:ns], rest[ns:ns + 2], rest[ns + 2:2 * ns + 2]
        local, remote = _xy_gather_copies(shard_refs, gathered, *rest[2 * ns + 2:])
        i = pl.program_id(1)

        @pl.when((pl.program_id(0) == 0) & (i == 0))
        def _():
            for cp in local + remote:
                cp.start()

        causal, _ = _key_query_masks()
        q = q_ref[0]

        def scores(j, masked):
            s2 = lax.dot_general(k_ref[0, j], q, NT_DIMS, preferred_element_type=F32)
            s = [s2[a * BLK:(a + 1) * BLK] for a in range(2)]
            return [jnp.where(causal, x, NEG) for x in s] if masked else s

        def update(blocks, carry):
            m, l, acc = list(carry[0]), list(carry[1]), carry[2]
            for j, s in blocks:
                alpha, pr = [], []
                for a in range(2):
                    mn = jnp.maximum(m[a], jnp.max(s[a], axis=0, keepdims=True))
                    pa = jnp.exp(s[a] - mn)
                    al = jnp.exp(m[a] - mn)
                    l[a] = al * l[a] + jnp.sum(pa, axis=0, keepdims=True)
                    m[a] = mn
                    alpha.append(al)
                    pr.append(pa.astype(BF16))
                acc = _rows_per_head(*alpha) * acc + jnp.dot(vt_ref[0, j], jnp.concatenate(pr, axis=0), preferred_element_type=F32)
            return tuple(m), tuple(l), acc

        tiles = lambda js, c: update([(j, scores(j, False)) for j in js], c)
        neg, zero = jnp.full((1, BLK), NEG, F32), jnp.zeros((1, BLK), F32)
        carry = _loop_blocks(i, tiles, ((neg, neg), (zero, zero), jnp.zeros((LANES, BLK), F32)), groups=(8, 4, 2, 1))
        m, l, acc = update([(i, scores(i, True))], carry)
        o_ref[...] = acc / _rows_per_head(*l)
        lse_ref[0] = m[0] + jnp.log(l[0])
        lse_ref[1] = m[1] + jnp.log(l[1])

        @pl.when((pl.program_id(0) == p - 1) & (i == nq - 1))
        def _():
            for cp in remote:
                cp.wait_recv()
            for cp in remote:
                cp.wait_send()
            for cp in local:
                cp.wait()

    row = pl.BlockSpec((2, 1, BLK), lambda h, i: (h, 0, i))
    hbm = pl.BlockSpec(memory_space=pltpu.HBM)
    return _pcall(
        body, name="fox_fwd", grid=(p, nq),
        out_shape=[jax.ShapeDtypeStruct((dg, t), F32), jax.ShapeDtypeStruct((nh, 1, t), F32)]
        + [jax.ShapeDtypeStruct((4,) + s.shape, s.dtype) for s in shards],
        in_specs=[pl.BlockSpec((1, BLK, 2 * LANES), lambda h, i: (h, i, 0)), _resident((nq, 2 * BLK, 2 * LANES)),
                  _resident((nq, LANES, 2 * BLK))] + [hbm] * ns,
        out_specs=[pl.BlockSpec((LANES, BLK), lambda h, i: (h, i)), row] + [hbm] * ns,
        scratch_shapes=[pltpu.SemaphoreType.DMA((3 * ns,)), pltpu.SemaphoreType.DMA((3 * ns,)), pltpu.SemaphoreType.DMA((ns,))],
        compiler_params=_params(("arbitrary", "arbitrary")),
    )(qx, kx, v_t, *shards)


def _swap_copies(ins, outs, send_sems, recv_sems, local_sems):
    x, y, c = lax.axis_index("x"), lax.axis_index("y"), lax.axis_index("c")
    local, remote = [], []
    for w in range(len(ins)):
        local.append(pltpu.make_async_copy(ins[w], outs[w].at[c], local_sems.at[w]))
        remote.append(pltpu.make_async_remote_copy(
            src_ref=ins[w], dst_ref=outs[w].at[c], send_sem=send_sems.at[w], recv_sem=recv_sems.at[w],
            device_id=(x, y, 1 - c), device_id_type=MESH))
    return local, remote


def _fox_bwd(qx, kx, kx_t, v_st, do, delta, lse, dg, halves):
    p, t = qx.shape[0], qx.shape[1]
    nq = t // BLK
    nh = 2 * p
    ns = len(halves)

    def body(q_ref, k_ref, kt_ref, v_ref, do_ref, dl_ref, lse_ref, *rest):
        half_refs, (dq_ref, dft_ref, dk_ref, dv_ref, dkx_ref), both_refs = rest[:ns], rest[ns:ns + 5], rest[ns + 5:2 * ns + 5]
        local, remote = _swap_copies(half_refs, both_refs, *rest[2 * ns + 5:])
        i = pl.program_id(1)

        @pl.when((pl.program_id(0) == 0) & (i == 0))
        def _():
            for cp in local + remote:
                cp.start()

        @pl.when(i == 0)
        def _():
            dk_ref[...] = jnp.zeros_like(dk_ref)
            dv_ref[...] = jnp.zeros_like(dv_ref)
            dkx_ref[...] = jnp.zeros_like(dkx_ref)

        in_a, _ = _head_masks()
        first_lane = lax.broadcasted_iota(jnp.int32, (1, LANES), 1) == 0
        causal, _ = _key_query_masks()
        q, do2 = q_ref[0], do_ref[...]
        dl = (dl_ref[0], dl_ref[1])
        lse = (lse_ref[0], lse_ref[1])

        def products(j):
            return (lax.dot_general(k_ref[0, j], q, NT_DIMS, preferred_element_type=F32),
                    lax.dot_general(v_ref[0, j], do2, NT_DIMS, preferred_element_type=F32))

        def dscores(prod, masked):
            s2, dp2 = prod
            pr, ds = [], []
            for a in range(2):
                s = s2[a * BLK:(a + 1) * BLK]
                if masked:
                    s = jnp.where(causal, s, NEG)
                pa = jnp.exp(s - lse[a])
                ds.append((pa * (dp2[a * BLK:(a + 1) * BLK] - dl[a])).astype(BF16))
                pr.append(pa.astype(BF16))
            return jnp.concatenate(ds, axis=0), jnp.concatenate(pr, axis=0)

        def accumulate(j, dsb, prb, dq):
            off = pl.multiple_of(j * BLK, BLK)
            dk_full = jnp.dot(dsb, q, preferred_element_type=F32)
            dk_ref[pl.ds(off, BLK), :] += _fold_heads(dk_full[:, :LANES], in_a)
            dkx_ref[pl.ds(off, BLK), :] += jnp.where(first_lane, dk_full[:BLK, LANES:], dk_full[BLK:, LANES:])
            dv_ref[pl.ds(off, BLK), :] += _fold_heads(jnp.dot(prb, do2, preferred_element_type=F32), in_a)
            return dq + jnp.dot(kt_ref[0, j], dsb, preferred_element_type=F32)

        def tiles(js, dq, masked=False):
            prods = [products(j) for j in js]
            grads = [dscores(pr, masked) for pr in prods]
            for j, (dsb, prb) in zip(js, grads):
                dq = accumulate(j, dsb, prb, dq)
            return dq

        dq = _loop_blocks(i, tiles, jnp.zeros((XROWS, BLK), F32))
        dq = tiles([i], dq, True)
        dq_ref[...] = dq[:LANES]
        dft_ref[0] = dq[LANES + LANE_FT_A:LANES + LANE_FT_A + 1]
        dft_ref[1] = dq[LANES + LANE_FT_B:LANES + LANE_FT_B + 1]

        @pl.when((pl.program_id(0) == p - 1) & (i == nq - 1))
        def _():
            for cp in remote:
                cp.wait_recv()
            for cp in remote:
                cp.wait_send()
            for cp in local:
                cp.wait()

    row = pl.BlockSpec((2, 1, BLK), lambda h, i: (h, 0, i))
    acc = pl.BlockSpec((t, LANES), lambda h, i: (0, h))
    hbm = pl.BlockSpec(memory_space=pltpu.HBM)
    return _pcall(
        body, name="fox_bwd", grid=(p, nq),
        out_shape=[jax.ShapeDtypeStruct((dg, t), F32), jax.ShapeDtypeStruct((nh, 1, t), F32)] + [jax.ShapeDtypeStruct((t, dg), F32)] * 3
        + [jax.ShapeDtypeStruct((2,) + h.shape, h.dtype) for h in halves],
        in_specs=[pl.BlockSpec((1, BLK, 2 * LANES), lambda h, i: (h, i, 0)), _resident((nq, 2 * BLK, 2 * LANES)),
                  _resident((nq, XROWS, 2 * BLK)), _resident((nq, 2 * BLK, LANES)),
                  pl.BlockSpec((BLK, LANES), lambda h, i: (i, h)), row, row] + [hbm] * ns,
        out_specs=[pl.BlockSpec((LANES, BLK), lambda h, i: (h, i)), row, acc, acc, acc] + [hbm] * ns,
        scratch_shapes=[pltpu.SemaphoreType.DMA((ns,)), pltpu.SemaphoreType.DMA((ns,)), pltpu.SemaphoreType.DMA((ns,))],
        compiler_params=_params(("arbitrary", "arbitrary")),
    )(qx, kx, kx_t, v_st, do, delta, lse, *halves)


def _softplus_of(z):
    return jnp.maximum(z, 0.0) + jnp.log(1.0 + jnp.exp2(jnp.abs(z) * (-LOG2_E)))


def _sb_fwd(qkv, k_st, v_t, dg):
    t = qkv.shape[0]
    p, nq = dg // LANES, t // BLK
    nh = 2 * p

    def body(q_ref, k_ref, vt_ref, o_ref, rt_ref):
        i = pl.program_id(1)
        _, strict = _key_query_masks()
        suffix = _key_triangle("suffix")
        q = q_ref[...]

        def scores(j):
            z2 = lax.dot_general(k_ref[0, j], q, NT_DIMS, preferred_element_type=F32)
            return [z2[a * BLK:(a + 1) * BLK] for a in range(2)]

        def suffix_sums(z, masked):
            out = []
            for a in range(2):
                sp = _softplus_of(z[a])
                if masked:
                    sp = jnp.where(strict, sp, 0.0)
                out.append(_tri_dot(suffix, sp, 2))
            return out

        def weights(z, cs, rest, masked):
            w, rest_new = [], []
            for a in range(2):
                wa = jnp.exp(z[a] - cs[a] - rest[a])
                if masked:
                    wa = jnp.where(strict, wa, 0.0)
                w.append(wa.astype(BF16))
                rest_new.append(rest[a] + cs[a][0:1])
            return jnp.concatenate(w, axis=0), tuple(rest_new)

        def tiles(js, carry, masked=False):
            rest, acc = carry
            zs = [scores(j) for j in js]
            css = [suffix_sums(z, masked) for z in zs]
            ws = []
            for z, cs in zip(zs, css):
                w2, rest = weights(z, cs, rest, masked)
                ws.append(w2)
            for j, w2 in zip(js, ws):
                acc = acc + jnp.dot(vt_ref[0, j], w2, preferred_element_type=F32)
            return rest, acc

        zero = jnp.zeros((1, BLK), F32)
        carry = tiles([i], ((zero, zero), jnp.zeros((LANES, BLK), F32)), True)
        rest, acc = _loop_blocks(i, tiles, carry, descending=True, groups=(8, 4, 2, 1))
        o_ref[...] = acc
        rt_ref[0] = rest[0]
        rt_ref[1] = rest[1]

    return _pcall(
        body, name="sb_fwd", grid=(p, nq),
        out_shape=[jax.ShapeDtypeStruct((dg, t), F32), jax.ShapeDtypeStruct((nh, 1, t), F32)],
        in_specs=[pl.BlockSpec((BLK, LANES), lambda h, i: (i, 3 * p + h)), _resident((nq, 2 * BLK, LANES)),
                  _resident((nq, LANES, 2 * BLK))],
        out_specs=[pl.BlockSpec((LANES, BLK), lambda h, i: (h, i)), pl.BlockSpec((2, 1, BLK), lambda h, i: (h, 0, i))],
        compiler_params=_params(("arbitrary", "arbitrary")),
    )(qkv, k_st, v_t)


def _scatter8_copies(ins, outs, send_sems, recv_sems, local_sems):
    x, y, c = lax.axis_index("x"), lax.axis_index("y"), lax.axis_index("c")
    me = 4 * x + 2 * y + c
    local, remote = [], []
    for w in range(len(ins)):
        local.append(pltpu.make_async_copy(ins[w].at[me], outs[w].at[me], local_sems.at[w]))
        for f in range(1, 8):
            px = 1 - x if f & 4 else x
            py = 1 - y if f & 2 else y
            pc = 1 - c if f & 1 else c
            remote.append(pltpu.make_async_remote_copy(
                src_ref=ins[w].at[4 * px + 2 * py + pc], dst_ref=outs[w].at[me],
                send_sem=send_sems.at[7 * w + f - 1], recv_sem=recv_sems.at[7 * w + f - 1],
                device_id=(px, py, pc), device_id_type=MESH))
    return local, remote


def _sb_bwd(qkv, k_st, k_t, v_st, do, rtot, dg, pieces):
    t = qkv.shape[0]
    p, nq = dg // LANES, t // BLK
    ns = len(pieces)

    def body(q_ref, k_ref, kt_ref, v_ref, do_ref, rt_ref, *rest):
        piece_refs, (dq_ref, dk_ref, dv_ref), recv_refs = rest[:ns], rest[ns:ns + 3], rest[ns + 3:2 * ns + 3]
        local, remote = _scatter8_copies(piece_refs, recv_refs, *rest[2 * ns + 3:])
        i = pl.program_id(1)

        @pl.when((pl.program_id(0) == 0) & (i == 0))
        def _():
            for cp in local + remote:
                cp.start()

        @pl.when(i == 0)
        def _():
            dk_ref[...] = jnp.zeros_like(dk_ref)
            dv_ref[...] = jnp.zeros_like(dv_ref)

        in_a, _ = _head_masks()
        _, strict = _key_query_masks()
        before_m, prefix_m = _key_triangle("before"), _key_triangle("prefix")
        q, do2 = q_ref[...], do_ref[...]
        rt = (rt_ref[0], rt_ref[1])

        def products(j):
            z2 = lax.dot_general(k_ref[0, j], q, NT_DIMS, preferred_element_type=F32)
            da2 = lax.dot_general(v_ref[0, j], do2, NT_DIMS, preferred_element_type=F32)
            return [z2[a * BLK:(a + 1) * BLK] for a in range(2)], [da2[a * BLK:(a + 1) * BLK] for a in range(2)]

        def softplus_sums(z, masked):
            sp = [_softplus_of(x) for x in z]
            if masked:
                sp = [jnp.where(strict, x, 0.0) for x in sp]
            return sp, [_tri_dot(before_m, x, 2) for x in sp]

        def weight_grads(z, da, sp, pre, before, masked):
            w, g, pg, before_new = [], [], [], []
            for a in range(2):
                wa = jnp.exp(z[a] + (before[a] - rt[a]) + pre[a])
                if masked:
                    wa = jnp.where(strict, wa, 0.0)
                ga = wa * da[a]
                w.append(wa.astype(BF16))
                g.append(ga)
                pg.append(jnp.dot(prefix_m, ga.astype(BF16), preferred_element_type=F32))
                before_new.append(before[a] + pre[a][BLK - 1:BLK] + sp[a][BLK - 1:BLK])
            return jnp.concatenate(w, axis=0), g, pg, tuple(before_new)

        def dlogits(sp, g, pg, gbefore, masked):
            dz, gbefore_new = [], []
            for a in range(2):
                s_incl = gbefore[a] + pg[a]
                dza = (g[a] - s_incl) + jnp.exp2(sp[a] * (-LOG2_E)) * s_incl
                if masked:
                    dza = jnp.where(strict, dza, 0.0)
                dz.append(dza.astype(BF16))
                gbefore_new.append(s_incl[BLK - 1:BLK])
            return jnp.concatenate(dz, axis=0), tuple(gbefore_new)

        def accumulate(j, dzb, wb, dq):
            off = pl.multiple_of(j * BLK, BLK)
            dk_ref[pl.ds(off, BLK), :] += _fold_heads(jnp.dot(dzb, q, preferred_element_type=F32), in_a)
            dv_ref[pl.ds(off, BLK), :] += _fold_heads(jnp.dot(wb, do2, preferred_element_type=F32), in_a)
            return dq + jnp.dot(kt_ref[0, j], dzb, preferred_element_type=F32)

        def tiles(js, carry, masked=False):
            before, gbefore, dq = carry
            prods = [products(j) for j in js]
            sums = [softplus_sums(z, masked) for z, _ in prods]
            grads = []
            for (z, da), (sp, pre) in zip(prods, sums):
                wb, g, pg, before = weight_grads(z, da, sp, pre, before, masked)
                grads.append((wb, g, pg))
            for j, (sp, _), (wb, g, pg) in zip(js, sums, grads):
                dzb, gbefore = dlogits(sp, g, pg, gbefore, masked)
                dq = accumulate(j, dzb, wb, dq)
            return before, gbefore, dq

        zero = jnp.zeros((1, BLK), F32)
        carry = _loop_blocks(i, tiles, ((zero, zero), (zero, zero), jnp.zeros((LANES, BLK), F32)), groups=(3, 2, 1))
        dq_ref[...] = tiles([i], carry, True)[2]

        @pl.when((pl.program_id(0) == p - 1) & (i == nq - 1))
        def _():
            for cp in remote:
                cp.wait_recv()
            for cp in remote:
                cp.wait_send()
            for cp in local:
                cp.wait()

    acc = pl.BlockSpec((t, LANES), lambda h, i: (0, h))
    hbm = pl.BlockSpec(memory_space=pltpu.HBM)
    return _pcall(
        body, name="sb_bwd", grid=(p, nq),
        out_shape=[jax.ShapeDtypeStruct((dg, t), F32)] + [jax.ShapeDtypeStruct((t, dg), F32)] * 2
        + [jax.ShapeDtypeStruct(pc.shape, pc.dtype) for pc in pieces],
        in_specs=[pl.BlockSpec((BLK, LANES), lambda h, i: (i, 3 * p + h)), _resident((nq, 2 * BLK, LANES)),
                  _resident((nq, LANES, 2 * BLK)), _resident((nq, 2 * BLK, LANES)),
                  pl.BlockSpec((BLK, LANES), lambda h, i: (i, h)), pl.BlockSpec((2, 1, BLK), lambda h, i: (h, 0, i))] + [hbm] * ns,
        out_specs=[pl.BlockSpec((LANES, BLK), lambda h, i: (h, i)), acc, acc] + [hbm] * ns,
        scratch_shapes=[pltpu.SemaphoreType.DMA((7 * ns,)), pltpu.SemaphoreType.DMA((7 * ns,)), pltpu.SemaphoreType.DMA((ns,))],
        compiler_params=_params(("arbitrary", "arbitrary")),
    )(qkv, k_st, k_t, v_st, do, rtot, *pieces)


def _tri_constants(nh, t):
    nb = t // LANES
    r = nh * nb
    li = np.arange(LANES)
    tri_in = (li[:, None] <= li[None, :])
    ri = np.arange(r)
    same = (ri[:, None] // nb) == (ri[None, :] // nb)
    blk = same & (ri[None, :] < ri[:, None])
    blk_rev = same & (ri[None, :] > ri[:, None])
    head_rows = (np.arange(max(8, nh))[:, None] == (ri[None, :] // nb))
    as_bf16 = lambda a: jnp.asarray(a.astype(np.float32), BF16)
    return as_bf16(tri_in), as_bf16(blk), as_bf16(tri_in.T), as_bf16(blk_rev), as_bf16(head_rows)


def kernel(x, c, w_ada, b_ada, g_attn, w_in, b_fgate, g_out_fox, g_out_sb, w_out, g_mlp, w_up, conv_w, conv_b, w_down, g_final, loss_target, m_w_ada, m_b_ada, m_g_attn, m_w_in, m_b_fgate, m_g_out_fox, m_g_out_sb, m_w_out, m_g_mlp, m_w_up, m_conv_w, m_conv_b, m_w_down, m_g_final, v_w_ada, v_b_ada, v_g_attn, v_w_in, v_b_fgate, v_g_out_fox, v_g_out_sb, v_w_out, v_g_mlp, v_w_up, v_conv_w, v_conv_b, v_w_down, v_g_final):
    t, d = x.shape[1], x.shape[2]
    dg = d // 2
    nh = dg // HEAD_DIM
    n_in = 6 * dg + nh
    dff = w_down.shape[1] * 4
    dfp = -(-dff // 256) * 256
    cf = 256
    tm = _tile(t, (512, 256, 128))
    nq = t // BLK
    xi, yi, ci = lax.axis_index("x"), lax.axis_index("y"), lax.axis_index("c")
    shard = 2 * xi + yi
    me = 4 * xi + 2 * yi + ci

    x2d, tg2d = x[0], loss_target[0]

    c_all = _all_gather8(jnp.pad(c, ((0, 7), (0, 0)))).reshape(8, 8, d)[:, 0, :]
    ada_cols = w_ada.shape[2]
    b_shard = lax.dynamic_slice(b_ada, (0, shard * ada_cols), (1, ada_cols))
    sc_all, mod_shard = _ada_fwd(c_all, w_ada[0], b_shard)
    mod_all = _all_gather8(mod_shard).reshape(4, 2, 8, ada_cols)
    mod_me = lax.dynamic_index_in_dim(mod_all[:, 0], me, axis=1, keepdims=False)
    mod8 = jnp.pad(mod_me.reshape(6, d), ((0, 2), (0, 0)))

    (g_in,) = _gather_xy([w_in[0].astype(BF16)])
    later_shards = [w_out[0].astype(BF16), w_up[0].astype(BF16), w_down[0].astype(BF16), conv_w[0]]
    w_in_full = jnp.transpose(g_in, (1, 0, 2)).reshape(d, n_in)
    w_qkv = w_in_full[:, :6 * dg]
    w_f = jnp.pad(w_in_full[:, 6 * dg:].T, ((0, LANES - nh), (0, 0)))

    qkv, fl, h1 = _in_proj_fwd(x2d, mod8, g_attn, w_qkv, w_f, tm)
    tri_in, tri_blk, tri_in_rev, tri_blk_rev, head_rows = _tri_constants(nh, t)
    fl2d = fl[:nh].reshape(nh * t // LANES, LANES)
    b_rows = jnp.repeat(b_fgate[0], t // LANES)[:, None]
    f2d = _fgate_fwd(fl2d, b_rows, tri_in, tri_blk)
    frow = f2d.reshape(nh, 1, t)
    pairs = nh // 2
    qx, kx, kx_t, vf_st, vf_t, ks_st, ks_t, vs_st, vs_t = _attn_operands(qkv, frow, dg)
    o_fox, lse, g_out, g_up, g_down, g_cw = _fox_fwd(qx, kx, vf_t, dg, later_shards)
    w_out_full = g_out.reshape(2 * dg, d)
    w_up_full = jnp.transpose(g_up, (1, 0, 2)).reshape(d, 2 * dff)
    padc = ((0, 0), (0, dfp - dff))
    wg, wv = jnp.pad(w_up_full[:, :dff], padc), jnp.pad(w_up_full[:, dff:], padc)
    wd = jnp.pad(g_down.reshape(dff, d), ((0, dfp - dff), (0, 0)))
    cw_full = jnp.transpose(g_cw, (1, 0, 2)).reshape(3, 2 * dff)
    cw4 = jnp.concatenate([cw_full, conv_b], axis=0)
    cwg = jnp.pad(cw4[:, :dff], ((0, 4), (0, dfp - dff)))
    cwv = jnp.pad(cw4[:, dff:], ((0, 4), (0, dfp - dff)))
    o_sb, rtot = _sb_fwd(qkv, ks_st, vs_t, dg)
    li = np.arange(dg)
    bd = jnp.asarray((li[:, None] // HEAD_DIM == li[None, :] // HEAD_DIM).astype(np.float32), BF16)
    hsel = jnp.asarray((np.arange(LANES)[:, None] == li[None, :] // HEAD_DIM).astype(np.float32), BF16)
    x2, mix, mix_t = _attn_out_fwd(x2d, o_fox, o_sb, g_out_fox, g_out_sb, w_out_full, mod8, bd, tm)
    g_final2 = g_final[None, :]
    dx3, h2, part_f = _ffn_fwd(x2, tg2d, mod8, g_mlp, g_final2, wg, wv, cwg, cwv, wd, tm, cf)

    tm_ffn_bwd = _tile(t, (256, 128))
    dx2, dupg, dupv, act, dxg3, part_b, pcg, pcv = _ffn_bwd(x2, dx3, mod8, g_mlp, wg, wv, cwg, cwv, wd, tm_ffn_bwd, cf)
    do_fox, do_sb, delta, dxg2, part_o = _attn_out_bwd(dx2, mix, o_fox, o_sb, g_out_fox, g_out_sb, w_out_full, mod8, bd, hsel, tm)
    drow = delta[:nh].reshape(nh, 1, t)

    def col_pieces(g):
        r, cc = g.shape
        return jnp.transpose(g.reshape(2, r // 2, 4, cc // 4), (2, 0, 1, 3)).reshape(8, r // 2, cc // 4)

    def row_pieces(g):
        r, cc = g.shape
        return g.reshape(8, r // 8, cc)

    gw_out = _matmul_tn(mix_t, dxg2, "grad_w_out")
    gw_upg = _matmul_tn(h2, dupg, "grad_w_up_gate")
    gw_upv = _matmul_tn(h2, dupv, "grad_w_up_val")
    gw_up = jnp.concatenate([gw_upg[:, :dff], gw_upv[:, :dff]], axis=1)
    gw_down = _matmul_tn(act, dxg3, "grad_w_down")[:dff]
    early = (row_pieces(gw_out), col_pieces(gw_up), row_pieces(gw_down))
    early = [_to_bf16(pc, "pieces_bf16_" + nm) for pc, nm in zip(early, ("w_out", "w_up", "w_down"))]

    dq_s, dk_s, dv_s, *recv_early = _sb_bwd(qkv, ks_st, ks_t, vs_st, do_sb, rtot, dg, early)
    halves_early = [_sum_leading(rv, nm) for rv, nm in zip(recv_early, ("sum_w_out", "sum_w_up", "sum_w_down"))]
    dq_f, dft, dk_f, dv_f, dkx, *swapped_early = _fox_bwd(qx, kx, kx_t, vf_st, do_fox, drow, lse, dg, halves_early)
    f2d_shape = (nh * t // LANES, LANES)
    dfs = jnp.transpose(dkx.reshape(t, pairs, LANES)[:, :, :2], (1, 2, 0))
    dfl2d, gb8 = _fgate_bwd(fl2d, b_rows, dft.reshape(f2d_shape), dfs.reshape(f2d_shape), tri_in_rev, tri_blk_rev, head_rows)
    dfl = jnp.pad(dfl2d.reshape(nh, t), ((0, LANES - nh), (0, 0)))
    grad_x, dproj, dflb, part_i = _in_proj_bwd([dq_f, dk_f, dv_f, dq_s, dk_s, dv_s], dfl, w_qkv, w_f, x2d, dx2, mod8, g_attn, tm)

    gw_qkv = _matmul_tn(h1, dproj, "grad_w_qkv")
    gw_f = _matmul_tn(h1, dflb, "grad_w_f")
    gw_in = jnp.concatenate([gw_qkv, gw_f[:, :nh]], axis=1)

    sf = _sum_leading(part_f, "sum_part_ffn_fwd")
    sb_ = _sum_leading(part_b, "sum_part_ffn_bwd")
    so = _sum_leading(part_o, "sum_part_attn_out")
    si = _sum_leading(part_i, "sum_part_in_proj")
    scg = _sum_leading(pcg, "sum_part_conv_gate")
    scv = _sum_leading(pcv, "sum_part_conv_val")
    gb_f = gb8[:nh, 0]
    dmod = jnp.concatenate([si[0], si[1], so[0], sb_[0], sb_[1], sf[1]])
    g_conv_w = jnp.concatenate([scg[0:3, :dff], scv[0:3, :dff]], axis=1).reshape(-1)
    g_conv_b = jnp.concatenate([scg[3, :dff], scv[3, :dff]])
    loss_part = jnp.sum(sf[2])
    fields = [dmod, si[2], gb_f, so[1, :dg], so[1, dg:], sb_[2], g_conv_b, sf[0], g_conv_w, loss_part[None]]
    sizes = [int(f.shape[0]) for f in fields]
    n_pack = sum(sizes)
    lanes_pack = -(-n_pack // (8 * LANES)) * LANES
    pack = jnp.pad(jnp.concatenate(fields), (0, 8 * lanes_pack - n_pack)).reshape(8, lanes_pack)
    gathered = _all_gather8(pack)
    tot = _sum_leading(gathered.reshape(8, 8, lanes_pack), "sum_pack").reshape(-1)
    offs = np.concatenate([[0], np.cumsum(sizes)])
    take = lambda k: tot[int(offs[k]):int(offs[k + 1])]
    g_b_ada, g_g_attn, g_b_fgate, g_g_fox, g_g_sb, g_g_mlp, g_cb, g_g_final, g_cw_full, loss_v = [take(k) for k in range(10)]
    loss = loss_v[0]
    dmod_all = gathered.reshape(8, 8 * lanes_pack)[:, :6 * d]
    dmod_cols = lax.dynamic_slice(dmod_all, (0, shard * ada_cols), (8, ada_cols))
    g_w_ada = _ada_bwd(sc_all.T, dmod_cols)

    (recv_in,) = _scatter8([_to_bf16(col_pieces(gw_in), "pieces_bf16_w_in")])
    (swapped_in,) = _swap_halves([_sum_leading(recv_in, "sum_w_in")])
    swapped = [swapped_in] + swapped_early
    g_w_in, g_w_out, g_w_up, g_w_down = [s.reshape(2 * s.shape[1], s.shape[2]) for s in swapped]
    g_conv_w_shard = lax.dynamic_slice(g_cw_full.reshape(3, 2 * dff), (0, shard * (dff // 2)), (3, dff // 2))

    grads, deltas, new_m, new_v = {}, {}, {}, {}

    def step(name, w, g, m, v):
        shape = w.shape
        as2d = lambda a: a.reshape(-1, shape[-1])
        dl, nm, nv = _adamw(as2d(w), as2d(g), as2d(m), as2d(v), "adamw_" + name)
        grads[name], deltas[name], new_m[name], new_v[name] = g.reshape(shape), dl.reshape(shape), nm.reshape(shape), nv.reshape(shape)

    step("w_ada", w_ada, g_w_ada, m_w_ada, v_w_ada)
    step("w_in", w_in, g_w_in, m_w_in, v_w_in)
    step("w_out", w_out, g_w_out, m_w_out, v_w_out)
    step("w_up", w_up, g_w_up, m_w_up, v_w_up)
    step("conv_w", conv_w, g_conv_w_shard, m_conv_w, v_conv_w)
    step("w_down", w_down, g_w_down, m_w_down, v_w_down)

    small = [("b_ada", b_ada, g_b_ada, m_b_ada, v_b_ada), ("g_attn", g_attn, g_g_attn, m_g_attn, v_g_attn),
             ("b_fgate", b_fgate, g_b_fgate, m_b_fgate, v_b_fgate), ("g_out_fox", g_out_fox, g_g_fox, m_g_out_fox, v_g_out_fox),
             ("g_out_sb", g_out_sb, g_g_sb, m_g_out_sb, v_g_out_sb), ("g_mlp", g_mlp, g_g_mlp, m_g_mlp, v_g_mlp),
             ("conv_b", conv_b, g_cb, m_conv_b, v_conv_b), ("g_final", g_final, g_g_final, m_g_final, v_g_final)]
    ssz = [int(np.prod(s[1].shape)) for s in small]
    n_small = sum(ssz)
    lanes_small = -(-n_small // (8 * LANES)) * LANES
    packs = [jnp.pad(jnp.concatenate([s[k].reshape(-1) for s in small]), (0, 8 * lanes_small - n_small)).reshape(8, lanes_small)
             for k in (1, 2, 3, 4)]
    dl_s, nm_s, nv_s = _adamw(*packs, "adamw_small")
    so_ = np.concatenate([[0], np.cumsum(ssz)])
    for k, s in enumerate(small):
        cut = lambda a: a.reshape(-1)[int(so_[k]):int(so_[k + 1])].reshape(s[1].shape)
        grads[s[0]], deltas[s[0]], new_m[s[0]], new_v[s[0]] = s[2].reshape(s[1].shape), cut(dl_s), cut(nm_s), cut(nv_s)

    order = ["w_ada", "b_ada", "g_attn", "w_in", "b_fgate", "g_out_fox", "g_out_sb", "w_out", "g_mlp", "w_up",
             "conv_w", "conv_b", "w_down", "g_final"]
    return (loss, grad_x[None], *[grads[n] for n in order], *[deltas[n] for n in order],
            *[new_m[n] for n in order], *[new_v[n] for n in order])
```

```python
import functools

import numpy as np
import jax
import jax.numpy as jnp
from jax import lax
from jax.experimental import pallas as pl
from jax.experimental.pallas import tpu as pltpu

F32 = jnp.float32
BF16 = jnp.bfloat16
MESH = pl.DeviceIdType.MESH

HEAD_DIM = 64
LANES = 128
EPS = 1e-6
NEG = -1e30
ADAM_LR, ADAM_B1, ADAM_B2, ADAM_EPS, ADAM_WD, ADAM_STEP = 0.001, 0.9, 0.999, 1e-08, 0.01, 10
V7X_VMEM_BYTES = 64 * 1024 * 1024
VMEM_LIMIT = V7X_VMEM_BYTES - 12 * 1024 * 1024
NT_DIMS = (((1,), (1,)), ((), ()))
LOG2_E = 1.4426950408889634


def _pcall(body, **kw):
    return pl.pallas_call(body, **kw)


def _params(sem=None, **kw):
    return pltpu.CompilerParams(dimension_semantics=sem, vmem_limit_bytes=VMEM_LIMIT, **kw)


def _split_dot(x, m, passes):
    acc = None
    for _ in range(passes):
        part = x.astype(BF16)
        d = jnp.dot(part, m, preferred_element_type=F32)
        acc = d if acc is None else acc + d
        x = x - part.astype(F32)
    return acc


def _tile(n, candidates):
    for t in candidates:
        if n % t == 0:
            return t
    return n


def _rows_tile(rows, row_bytes, budget=2 * 1024 * 1024):
    best = None
    for t in range(8, rows + 1, 8):
        if rows % t == 0 and t * row_bytes <= budget:
            best = t
    return best if best is not None else rows


def _all_gather8(v):
    m_per, n = v.shape

    def body(x_ref, out_ref, send_sems, recv_sems, local_sem):
        x, y, c = lax.axis_index("x"), lax.axis_index("y"), lax.axis_index("c")
        me, sibling = (x, y, c), (x, y, 1 - c)
        chips = [(1 - x, y), (x, 1 - y), (1 - x, 1 - y)]

        def rows(px, py, pc):
            return out_ref.at[pl.ds((4 * px + 2 * py + pc) * m_per, m_per), :]

        def copy(k, block, to, src=None):
            return pltpu.make_async_remote_copy(
                src_ref=rows(*block) if src is None else src, dst_ref=rows(*block),
                send_sem=send_sems.at[k], recv_sem=recv_sems.at[k], device_id=to, device_id_type=MESH)

        mine = pltpu.make_async_copy(x_ref, rows(*me), local_sem)
        mine.start()
        first = [copy(0, me, sibling, src=x_ref)]
        first += [copy(1 + j, me, (*chip, c), src=x_ref) for j, chip in enumerate(chips)]
        for cp in first:
            cp.start()
        passed = [copy(4 + j, (*chip, c), sibling) for j, chip in enumerate(chips)]
        for j, chip in enumerate(chips):
            copy(1 + j, (*chip, c), me).wait_recv()
            passed[j].start()
        copy(0, sibling, me).wait_recv()
        for j, chip in enumerate(chips):
            copy(4 + j, (*chip, 1 - c), me).wait_recv()
        for cp in first + passed:
            cp.wait_send()
        mine.wait()

    return _pcall(
        body, name="all_gather8",
        out_shape=jax.ShapeDtypeStruct((8 * m_per, n), v.dtype),
        in_specs=[pl.BlockSpec(memory_space=pltpu.VMEM)],
        out_specs=pl.BlockSpec(memory_space=pltpu.VMEM),
        scratch_shapes=[pltpu.SemaphoreType.DMA((7,)), pltpu.SemaphoreType.DMA((7,)), pltpu.SemaphoreType.DMA],
        compiler_params=pltpu.CompilerParams(vmem_limit_bytes=VMEM_LIMIT),
    )(v)


def _gather_xy(shards):
    n = len(shards)

    def body(*refs):
        ins, outs = refs[:n], refs[n:2 * n]
        send_sems, recv_sems, local_sems = refs[2 * n:]
        x, y, c = lax.axis_index("x"), lax.axis_index("y"), lax.axis_index("c")
        chips = [(1 - x, y), (x, 1 - y), (1 - x, 1 - y)]
        mine = 2 * x + y
        local, remote = [], []
        for w in range(n):
            cp = pltpu.make_async_copy(ins[w], outs[w].at[mine], local_sems.at[w])
            cp.start()
            local.append(cp)
            for k, (px, py) in enumerate(chips):
                cp = pltpu.make_async_remote_copy(
                    src_ref=ins[w], dst_ref=outs[w].at[mine], send_sem=send_sems.at[3 * w + k],
                    recv_sem=recv_sems.at[3 * w + k], device_id=(px, py, c), device_id_type=MESH)
                cp.start()
                remote.append(cp)
        for cp in remote:
            cp.wait_recv()
        for cp in remote:
            cp.wait_send()
        for cp in local:
            cp.wait()

    hbm = pl.BlockSpec(memory_space=pltpu.HBM)
    return _pcall(
        body, name="gather_xy",
        out_shape=[jax.ShapeDtypeStruct((4,) + s.shape, s.dtype) for s in shards],
        in_specs=[hbm] * n, out_specs=[hbm] * n,
        scratch_shapes=[pltpu.SemaphoreType.DMA((3 * n,)), pltpu.SemaphoreType.DMA((3 * n,)),
                        pltpu.SemaphoreType.DMA((n,))],
        compiler_params=pltpu.CompilerParams(vmem_limit_bytes=VMEM_LIMIT),
    )(*shards)


def _scatter8(pieces):
    n = len(pieces)

    def body(*refs):
        ins, outs = refs[:n], refs[n:2 * n]
        send_sems, recv_sems, local_sems = refs[2 * n:]
        x, y, c = lax.axis_index("x"), lax.axis_index("y"), lax.axis_index("c")
        me = 4 * x + 2 * y + c
        local, remote = [], []
        for w in range(n):
            cp = pltpu.make_async_copy(ins[w].at[me], outs[w].at[me], local_sems.at[w])
            cp.start()
            local.append(cp)
            for f in range(1, 8):
                px = 1 - x if f & 4 else x
                py = 1 - y if f & 2 else y
                pc = 1 - c if f & 1 else c
                cp = pltpu.make_async_remote_copy(
                    src_ref=ins[w].at[4 * px + 2 * py + pc], dst_ref=outs[w].at[me],
                    send_sem=send_sems.at[7 * w + f - 1], recv_sem=recv_sems.at[7 * w + f - 1],
                    device_id=(px, py, pc), device_id_type=MESH)
                cp.start()
                remote.append(cp)
        for cp in remote:
            cp.wait_recv()
        for cp in remote:
            cp.wait_send()
        for cp in local:
            cp.wait()

    hbm = pl.BlockSpec(memory_space=pltpu.HBM)
    return _pcall(
        body, name="scatter8",
        out_shape=[jax.ShapeDtypeStruct(p.shape, p.dtype) for p in pieces],
        in_specs=[hbm] * n, out_specs=[hbm] * n,
        scratch_shapes=[pltpu.SemaphoreType.DMA((7 * n,)), pltpu.SemaphoreType.DMA((7 * n,)),
                        pltpu.SemaphoreType.DMA((n,))],
        compiler_params=pltpu.CompilerParams(vmem_limit_bytes=VMEM_LIMIT),
    )(*pieces)


def _swap_halves(halves):
    n = len(halves)
    chunks = 8
    n_chunks = [max(k for k in (chunks, 4, 2, 1) if h.shape[0] % (8 * k) == 0) for h in halves]

    def body(*refs):
        ins, outs = refs[:n], refs[n:2 * n]
        send_sems, recv_sems, local_sems = refs[2 * n:]
        x, y, c = lax.axis_index("x"), lax.axis_index("y"), lax.axis_index("c")
        local, remote = [], []
        for w in range(n):
            cp = pltpu.make_async_copy(ins[w], outs[w].at[c], local_sems.at[w])
            cp.start()
            local.append(cp)
            rows = ins[w].shape[0] // n_chunks[w]
            for k in range(n_chunks[w]):
                cp = pltpu.make_async_remote_copy(
                    src_ref=ins[w].at[pl.ds(k * rows, rows)], dst_ref=outs[w].at[c, pl.ds(k * rows, rows)],
                    send_sem=send_sems.at[chunks * w + k], recv_sem=recv_sems.at[chunks * w + k],
                    device_id=(x, y, 1 - c), device_id_type=MESH)
                cp.start()
                remote.append(cp)
        for cp in remote:
            cp.wait_recv()
        for cp in remote:
            cp.wait_send()
        for cp in local:
            cp.wait()

    hbm = pl.BlockSpec(memory_space=pltpu.HBM)
    return _pcall(
        body, name="swap_halves",
        out_shape=[jax.ShapeDtypeStruct((2,) + h.shape, h.dtype) for h in halves],
        in_specs=[hbm] * n, out_specs=[hbm] * n,
        scratch_shapes=[pltpu.SemaphoreType.DMA((chunks * n,)), pltpu.SemaphoreType.DMA((chunks * n,)),
                        pltpu.SemaphoreType.DMA((n,))],
        compiler_params=pltpu.CompilerParams(vmem_limit_bytes=VMEM_LIMIT),
    )(*halves)


def _sum_leading(a, name):
    n, r, c = a.shape
    tr = _rows_tile(r, n * c * 4, budget=6 * 1024 * 1024)
    if a.dtype == BF16 and tr % 16:
        tr = r

    def body(a_ref, o_ref):
        acc = a_ref[0].astype(F32)
        for k in range(1, n):
            acc = acc + a_ref[k].astype(F32)
        o_ref[...] = acc

    return _pcall(
        body, name=name, grid=(r // tr,),
        out_shape=jax.ShapeDtypeStruct((r, c), F32),
        in_specs=[pl.BlockSpec((n, tr, c), lambda i: (0, i, 0))],
        out_specs=pl.BlockSpec((tr, c), lambda i: (i, 0)),
        compiler_params=_params(("arbitrary",)),
    )(a)


def _to_bf16(a, name):
    n, r, c = a.shape

    def body(a_ref, o_ref):
        o_ref[...] = a_ref[...].astype(BF16)

    spec = pl.BlockSpec((1, r, c), lambda i: (i, 0, 0))
    return _pcall(
        body, name=name, grid=(n,), out_shape=jax.ShapeDtypeStruct(a.shape, BF16),
        in_specs=[spec], out_specs=spec, compiler_params=_params(("arbitrary",)),
    )(a)


def _adamw(w, g, m, v, name):
    r, c = w.shape
    tr = _rows_tile(r, c * 4, budget=1024 * 1024)
    c1 = 1.0 - ADAM_B1 ** ADAM_STEP
    c2 = 1.0 - ADAM_B2 ** ADAM_STEP

    def body(w_ref, g_ref, m_ref, v_ref, d_ref, nm_ref, nv_ref):
        gg = g_ref[...]
        nm = ADAM_B1 * m_ref[...] + (1.0 - ADAM_B1) * gg
        nv = ADAM_B2 * v_ref[...] + (1.0 - ADAM_B2) * (gg * gg)
        m_hat = nm / c1
        v_hat = nv / c2
        d_ref[...] = -ADAM_LR * (m_hat / (jnp.sqrt(v_hat) + ADAM_EPS) + ADAM_WD * w_ref[...])
        nm_ref[...] = nm
        nv_ref[...] = nv

    spec = pl.BlockSpec((tr, c), lambda i: (i, 0))
    return _pcall(
        body, name=name, grid=(r // tr,),
        out_shape=[jax.ShapeDtypeStruct((r, c), F32)] * 3,
        in_specs=[spec] * 4, out_specs=[spec] * 3,
        compiler_params=_params(("arbitrary",)),
    )(w, g, m, v)


def _ada_fwd(c_all, w_shard, b_shard):
    nb, d = c_all.shape
    cols = w_shard.shape[1]

    def body(c_ref, w_ref, b_ref, sc_ref, mod_ref):
        cv = c_ref[...]
        sc = cv * jax.nn.sigmoid(cv)
        sc_ref[...] = sc
        mod_ref[...] = jnp.dot(sc.astype(BF16), w_ref[...].astype(BF16), preferred_element_type=F32) + b_ref[...]

    return _pcall(
        body, name="ada_fwd",
        out_shape=[jax.ShapeDtypeStruct((nb, d), F32), jax.ShapeDtypeStruct((nb, cols), F32)],
        compiler_params=pltpu.CompilerParams(vmem_limit_bytes=VMEM_LIMIT),
    )(c_all, w_shard, b_shard)


def _ada_bwd(sc_t, dmod_cols):
    d, nb = sc_t.shape
    cols = dmod_cols.shape[1]
    tr = _rows_tile(d, cols * 4, budget=1024 * 1024)

    def body(s_ref, m_ref, o_ref):
        s = s_ref[...]
        m = m_ref[...]
        acc = s[:, 0:1] * m[0:1, :]
        for b in range(1, nb):
            acc = acc + s[:, b:b + 1] * m[b:b + 1, :]
        o_ref[...] = acc

    return _pcall(
        body, name="ada_bwd", grid=(d // tr,),
        out_shape=jax.ShapeDtypeStruct((d, cols), F32),
        in_specs=[pl.BlockSpec((tr, nb), lambda i: (i, 0)), pl.BlockSpec((nb, cols), lambda i: (0, 0))],
        out_specs=pl.BlockSpec((tr, cols), lambda i: (i, 0)),
        compiler_params=_params(("arbitrary",)),
    )(sc_t, dmod_cols)


def _log_sigmoid(x):
    return jnp.minimum(x, 0.0) - jnp.log1p(jnp.exp(-jnp.abs(x)))


def _fgate_fwd(fl2d, b_rows, tri_in, tri_blk):
    r = fl2d.shape[0]

    def body(x_ref, b_ref, u_ref, l_ref, f_ref):
        lf = _log_sigmoid(x_ref[...] + b_ref[...])
        c1 = _split_dot(lf, u_ref[...], 3)
        tot = jnp.broadcast_to(c1[:, LANES - 1:LANES], (r, LANES))
        acc = None
        for _ in range(3):
            part = tot.astype(BF16)
            dd = jnp.dot(l_ref[...], part, preferred_element_type=F32)
            acc = dd if acc is None else acc + dd
            tot = tot - part.astype(F32)
        f_ref[...] = c1 + acc

    return _pcall(
        body, name="fgate_fwd", out_shape=jax.ShapeDtypeStruct((r, LANES), F32),
        compiler_params=pltpu.CompilerParams(vmem_limit_bytes=VMEM_LIMIT),
    )(fl2d, b_rows, tri_in, tri_blk)


def _fgate_bwd(fl2d, b_rows, df_query, df_key, tri_in_rev, tri_blk_rev, head_rows):
    r = fl2d.shape[0]
    nhp = head_rows.shape[0]

    def body(x_ref, b_ref, dq_ref, dk_ref, u_ref, l_ref, hr_ref, o_ref, gb_ref):
        c1 = _split_dot(dq_ref[...] + dk_ref[...], u_ref[...], 3)
        tot = jnp.broadcast_to(c1[:, 0:1], (r, LANES))
        acc = None
        for _ in range(3):
            part = tot.astype(BF16)
            dd = jnp.dot(l_ref[...], part, preferred_element_type=F32)
            acc = dd if acc is None else acc + dd
            tot = tot - part.astype(F32)
        x = x_ref[...] + b_ref[...]
        e = jnp.exp(-jnp.abs(x))
        dfl = (c1 + acc) * (jnp.where(x >= 0, e, 1.0) / (1.0 + e))
        o_ref[...] = dfl
        rs = jnp.broadcast_to(jnp.sum(dfl, axis=1, keepdims=True), (r, LANES))
        gb = None
        for _ in range(3):
            part = rs.astype(BF16)
            dd = jnp.dot(hr_ref[...], part, preferred_element_type=F32)
            gb = dd if gb is None else gb + dd
            rs = rs - part.astype(F32)
        gb_ref[...] = gb

    return _pcall(
        body, name="fgate_bwd",
        out_shape=[jax.ShapeDtypeStruct((r, LANES), F32), jax.ShapeDtypeStruct((nhp, LANES), F32)],
        compiler_params=pltpu.CompilerParams(vmem_limit_bytes=VMEM_LIMIT),
    )(fl2d, b_rows, df_query, df_key, tri_in_rev, tri_blk_rev, head_rows)


def _norm_mod(x, g, scale, shift):
    r = lax.rsqrt(jnp.mean(x * x, axis=-1, keepdims=True) + EPS)
    return (x * r * g) * (1.0 + scale) + shift


def _norm_mod_bwd(x, dh, g, scale):
    r = lax.rsqrt(jnp.mean(x * x, axis=-1, keepdims=True) + EPS)
    xn = x * r
    dshift = jnp.sum(dh, axis=0, keepdims=True)
    dscale = jnp.sum(dh * (xn * g), axis=0, keepdims=True)
    dxn_g = dh * (1.0 + scale)
    dg = jnp.sum(dxn_g * xn, axis=0, keepdims=True)
    dxn = dxn_g * g
    dx = r * (dxn - xn * jnp.mean(dxn * xn, axis=-1, keepdims=True))
    return dx, dshift, dscale, dg


def _in_proj_fwd(x, mod8, g_attn, w_qkv, w_f, tm):
    t, d = x.shape
    dg = w_qkv.shape[1] // 6

    def body(x_ref, mod_ref, g_ref, w_ref, wf_ref, qkv_ref, fl_ref, h1_ref):
        h = _norm_mod(x_ref[...], g_ref[...], mod_ref[1:2, :], mod_ref[0:1, :]).astype(BF16)
        h1_ref[...] = _transposed(h)
        fl_ref[...] = lax.dot_general(wf_ref[...], h, NT_DIMS, preferred_element_type=F32)
        for k in range(6):
            cols = slice(k * dg, (k + 1) * dg)
            y = jnp.dot(h, w_ref[:, cols], preferred_element_type=F32)
            qkv_ref[:, cols] = (y * HEAD_DIM ** -0.5 if k in (0, 3) else y).astype(BF16)

    once = lambda a: pl.BlockSpec(a.shape, lambda i: (0,) * a.ndim, pipeline_mode=pl.Buffered(1))
    return _pcall(
        body, name="in_proj_fwd", grid=(t // tm,),
        out_shape=[jax.ShapeDtypeStruct((t, 6 * dg), BF16), jax.ShapeDtypeStruct((LANES, t), F32),
                   jax.ShapeDtypeStruct((d, t), BF16)],
        in_specs=[pl.BlockSpec((tm, d), lambda i: (i, 0)), pl.BlockSpec((8, d), lambda i: (0, 0)),
                  pl.BlockSpec((1, d), lambda i: (0, 0)), once(w_qkv), once(w_f)],
        out_specs=[pl.BlockSpec((tm, 6 * dg), lambda i: (i, 0)), pl.BlockSpec((LANES, tm), lambda i: (0, i)),
                   pl.BlockSpec((d, tm), lambda i: (0, i))],
        compiler_params=_params(("arbitrary",)),
    )(x, mod8, g_attn, w_qkv, w_f)


def _head_rstd(o, bd):
    return lax.rsqrt(_split_dot(o * o, bd, 3) * (1.0 / HEAD_DIM) + EPS)


def _attn_out_fwd(x, o_fox, o_sb, g_fox, g_sb, w_out, mod8, bd, tm):
    t, d = x.shape
    dg = o_fox.shape[0]

    def body(x_ref, of_ref, os_ref, gf_ref, gs_ref, w_ref, mod_ref, bd_ref, x2_ref, mix_ref, mixt_ref):
        of, osb = of_ref[...].T, os_ref[...].T
        mf = (of * _head_rstd(of, bd_ref[...]) * gf_ref[...]).astype(BF16)
        ms = (osb * _head_rstd(osb, bd_ref[...]) * gs_ref[...]).astype(BF16)
        mix_ref[:, :dg] = mf
        mix_ref[:, dg:] = ms
        mixt_ref[:dg, :] = _transposed(mf)
        mixt_ref[dg:, :] = _transposed(ms)
        y = jnp.dot(mf, w_ref[:dg, :], preferred_element_type=F32) + jnp.dot(ms, w_ref[dg:, :], preferred_element_type=F32)
        x2_ref[...] = x_ref[...] + mod_ref[2:3, :] * y

    row = lambda w: pl.BlockSpec((tm, w), lambda i: (i, 0))
    full = lambda a: pl.BlockSpec(a.shape, lambda i: (0,) * a.ndim)
    return _pcall(
        body, name="attn_out_fwd", grid=(t // tm,),
        out_shape=[jax.ShapeDtypeStruct((t, d), F32), jax.ShapeDtypeStruct((t, 2 * dg), BF16),
                   jax.ShapeDtypeStruct((2 * dg, t), BF16)],
        in_specs=[row(d), pl.BlockSpec((dg, tm), lambda i: (0, i)), pl.BlockSpec((dg, tm), lambda i: (0, i)),
                  full(g_fox), full(g_sb), full(w_out), full(mod8), full(bd)],
        out_specs=[row(d), row(2 * dg), pl.BlockSpec((2 * dg, tm), lambda i: (0, i))],
        compiler_params=_params(("arbitrary",)),
    )(x, o_fox, o_sb, g_fox, g_sb, w_out, mod8, bd)


def _attn_out_bwd(dx2, mix, o_fox, o_sb, g_fox, g_sb, w_out, mod8, bd, hsel, tm):
    t, d = dx2.shape
    dg = o_fox.shape[0]

    def body(dx_ref, mix_ref, of_ref, os_ref, gf_ref, gs_ref, w_ref, mod_ref, bd_ref, hs_ref,
             dof_ref, dos_ref, dlt_ref, dxg_ref, part_ref):
        dx = dx_ref[...]
        gate = mod_ref[2:3, :]
        dxg = (dx * gate).astype(BF16)
        dxg_ref[...] = dxg
        mixv = mix_ref[...]
        y = jnp.dot(mixv[:, :dg], w_ref[:dg, :], preferred_element_type=F32)
        y = y + jnp.dot(mixv[:, dg:], w_ref[dg:, :], preferred_element_type=F32)
        part_ref[0] = jnp.zeros((8, d), F32)
        part_ref[0, 0:1, :] = jnp.sum(dx * y, axis=0, keepdims=True)
        for grp, (o_ref, g_ref, do_ref) in enumerate(((of_ref, gf_ref, dof_ref), (os_ref, gs_ref, dos_ref))):
            dmix = lax.dot_general(dxg, w_ref[grp * dg:(grp + 1) * dg, :], NT_DIMS, preferred_element_type=F32)
            o = o_ref[...].T
            r = _head_rstd(o, bd_ref[...])
            n = o * r
            part_ref[0, 1:2, grp * dg:(grp + 1) * dg] = jnp.sum(dmix * n, axis=0, keepdims=True)
            dn = dmix * g_ref[...]
            mh = _split_dot(dn * n, bd_ref[...], 3) * (1.0 / HEAD_DIM)
            do = r * (dn - n * mh)
            do_ref[...] = do.astype(BF16)
            if grp == 0:
                prod, dlt = do * o, None
                for _ in range(3):
                    part = prod.astype(BF16)
                    term = lax.dot_general(hs_ref[...], part, NT_DIMS, preferred_element_type=F32)
                    dlt = term if dlt is None else dlt + term
                    prod = prod - part.astype(F32)
                dlt_ref[...] = dlt

    row = lambda w: pl.BlockSpec((tm, w), lambda i: (i, 0))
    full = lambda a: pl.BlockSpec(a.shape, lambda i: (0,) * a.ndim)
    nt = t // tm
    return _pcall(
        body, name="attn_out_bwd", grid=(nt,),
        out_shape=[jax.ShapeDtypeStruct((t, dg), BF16), jax.ShapeDtypeStruct((t, dg), BF16),
                   jax.ShapeDtypeStruct((LANES, t), F32), jax.ShapeDtypeStruct((t, d), BF16),
                   jax.ShapeDtypeStruct((nt, 8, d), F32)],
        in_specs=[row(d), row(2 * dg), pl.BlockSpec((dg, tm), lambda i: (0, i)), pl.BlockSpec((dg, tm), lambda i: (0, i)),
                  full(g_fox), full(g_sb), full(w_out), full(mod8),
                  full(bd), full(hsel)],
        out_specs=[row(dg), row(dg), pl.BlockSpec((LANES, tm), lambda i: (0, i)), row(d), pl.BlockSpec((1, 8, d), lambda i: (i, 0, 0))],
        compiler_params=_params(("arbitrary",)),
    )(dx2, mix, o_fox, o_sb, g_fox, g_sb, w_out, mod8, bd, hsel)


def _in_proj_bwd(dparts, dfl, w_qkv, w_f, x, dx2, mod8, g_attn, tm):
    t, d = x.shape
    dg = dparts[1].shape[1]

    def body(*refs):
        d_refs = refs[:6]
        dfl_ref, w_ref, wf_ref, x_ref, dx2_ref, mod_ref, g_ref, gx_ref, dp_ref, dflb_ref, part_ref = refs[6:]
        dh = None
        for k in range(6):
            dk = d_refs[k][...].T if k in (0, 3) else d_refs[k][...]
            if k in (0, 3):
                dk = dk * HEAD_DIM ** -0.5
            db = dk.astype(BF16)
            dp_ref[:, k * dg:(k + 1) * dg] = db
            term = lax.dot_general(db, w_ref[:, k * dg:(k + 1) * dg], NT_DIMS, preferred_element_type=F32)
            dh = term if dh is None else dh + term
        dfb = dfl_ref[...].T.astype(BF16)
        dflb_ref[...] = dfb
        dh = dh + jnp.dot(dfb, wf_ref[...], preferred_element_type=F32)
        dx, dshift, dscale, dgn = _norm_mod_bwd(x_ref[...], dh, g_ref[...], mod_ref[1:2, :])
        gx_ref[...] = dx2_ref[...] + dx
        part_ref[0] = jnp.zeros((8, d), F32)
        part_ref[0, 0:1, :] = dshift
        part_ref[0, 1:2, :] = dscale
        part_ref[0, 2:3, :] = dgn

    row = lambda w: pl.BlockSpec((tm, w), lambda i: (i, 0))
    full = lambda a: pl.BlockSpec(a.shape, lambda i: (0,) * a.ndim)
    nt = t // tm
    return _pcall(
        body, name="in_proj_bwd", grid=(nt,),
        out_shape=[jax.ShapeDtypeStruct((t, d), F32), jax.ShapeDtypeStruct((t, 6 * dg), BF16),
                   jax.ShapeDtypeStruct((t, LANES), BF16), jax.ShapeDtypeStruct((nt, 8, d), F32)],
        in_specs=[pl.BlockSpec((dg, tm), lambda i: (0, i)), row(dg), row(dg)] * 2
        + [pl.BlockSpec((LANES, tm), lambda i: (0, i)), full(w_qkv), full(w_f), row(d), row(d), full(mod8), full(g_attn)],
        out_specs=[row(d), row(6 * dg), row(LANES), pl.BlockSpec((1, 8, d), lambda i: (i, 0, 0))],
        compiler_params=_params(("arbitrary",)),
    )(*dparts, dfl, w_qkv, w_f, x, dx2, mod8, g_attn)


def _matmul_tn(a_t, b, name):
    m, t = a_t.shape
    n = b.shape[1]
    tm_ = _tile(m, (1408, 1024, 512, 256, 128))
    tn_ = _tile(n, (1408, 1024, 512, 256, 128))
    tk = _tile(t, (1024, 512, 256, 128))
    nk = t // tk

    def body(a_ref, b_ref, o_ref):
        k = pl.program_id(2)

        @pl.when(k == 0)
        def _():
            o_ref[...] = jnp.zeros_like(o_ref)

        o_ref[...] += jnp.dot(a_ref[...], b_ref[...], preferred_element_type=F32)

    return _pcall(
        body, name=name, grid=(m // tm_, n // tn_, nk),
        out_shape=jax.ShapeDtypeStruct((m, n), F32),
        in_specs=[pl.BlockSpec((tm_, tk), lambda i, j, k: (i, k)), pl.BlockSpec((tk, tn_), lambda i, j, k: (k, j))],
        out_specs=pl.BlockSpec((tm_, tn_), lambda i, j, k: (i, j)),
        compiler_params=_params(("arbitrary", "arbitrary", "arbitrary")),
    )(a_t, b)


HALO = 16


def _conv_taps(up_ext, cw, lo, rows):
    s1 = pltpu.roll(up_ext, 1, 0)
    s2 = pltpu.roll(up_ext, 2, 0)
    u = cw[2:3, :] * up_ext[lo:lo + rows] + cw[1:2, :] * s1[lo:lo + rows] + cw[0:1, :] * s2[lo:lo + rows] + cw[3:4, :]
    return u, s1, s2


def _chunk_major(w, cf):
    d, n = w.shape[0], w.shape[1] // cf
    return jnp.transpose(w.reshape(d, n, cf), (1, 0, 2))


def _ffn_fwd(x2, target, mod8, g_mlp, g_final, wg, wv, cwg, cwv, wd, tm, cf):
    t, d = x2.shape
    dfp = wg.shape[1]
    nt, nc = t // tm, dfp // cf
    hb = tm // HALO
    wg_c, wv_c = _chunk_major(wg, cf), _chunk_major(wv, cf)

    def body(x_ref, xp_ref, tg_ref, mod_ref, g_ref, gf_ref, wg_ref, wv_ref, cg_ref, cv_ref, wd_ref,
             dx3_ref, h2_ref, part_ref, act_sc):
        i = pl.program_id(0)
        xe = jnp.concatenate([xp_ref[...], x_ref[...]], axis=0)
        h = _norm_mod(xe, g_ref[...], mod_ref[4:5, :], mod_ref[3:4, :]).astype(BF16)
        h2_ref[...] = _transposed(h[HALO:])
        first = jnp.where(i > 0, h[:HALO], jnp.zeros_like(h[:HALO]))
        h = jnp.concatenate([first, h[HALO:]], axis=0)

        def up(c):
            return (jnp.dot(h, wg_ref[c], preferred_element_type=F32), jnp.dot(h, wv_ref[c], preferred_element_type=F32))

        def activation(c, ups):
            cols = slice(c * cf, (c + 1) * cf)
            ug, _, _ = _conv_taps(ups[0], cg_ref[:, cols], HALO, tm)
            uv, _, _ = _conv_taps(ups[1], cv_ref[:, cols], HALO, tm)
            act_sc[:, cols] = (ug * jax.nn.sigmoid(ug) * uv).astype(BF16)

        for c0 in range(0, nc, 2):
            group = list(range(c0, min(c0 + 2, nc)))
            ups = [up(c) for c in group]
            for c, u in zip(group, ups):
                activation(c, u)

        y_ffn = jnp.dot(act_sc[...], wd_ref[...], preferred_element_type=F32)
        x3 = x_ref[...] + mod_ref[5:6, :] * y_ffn
        r3 = lax.rsqrt(jnp.mean(x3 * x3, axis=-1, keepdims=True) + EPS)
        xn = x3 * r3
        gf = gf_ref[...]
        diff = xn * gf - tg_ref[...]
        dy = diff * (1.0 / d)
        dxn = dy * gf
        dx3 = r3 * (dxn - xn * jnp.mean(dxn * xn, axis=-1, keepdims=True))
        dx3_ref[...] = dx3
        part_ref[0] = jnp.zeros((8, d), F32)
        part_ref[0, 0:1, :] = jnp.sum(dy * xn, axis=0, keepdims=True)
        part_ref[0, 1:2, :] = jnp.sum(dx3 * y_ffn, axis=0, keepdims=True)
        part_ref[0, 2:3, :] = jnp.sum(diff * diff, axis=0, keepdims=True) * (0.5 / d)

    row = lambda w: pl.BlockSpec((tm, w), lambda i: (i, 0))
    full = lambda a: pl.BlockSpec(a.shape, lambda i: (0,) * a.ndim)
    once = lambda a: pl.BlockSpec(a.shape, lambda i: (0,) * a.ndim, pipeline_mode=pl.Buffered(1))
    return _pcall(
        body, name="ffn_fwd", grid=(nt,),
        out_shape=[jax.ShapeDtypeStruct((t, d), F32), jax.ShapeDtypeStruct((d, t), BF16),
                   jax.ShapeDtypeStruct((nt, 8, d), F32)],
        in_specs=[row(d), pl.BlockSpec((HALO, d), lambda i: (jnp.maximum(i * hb - 1, 0), 0)), row(d),
                  full(mod8), full(g_mlp), full(g_final), once(wg_c), once(wv_c), once(cwg), once(cwv), once(wd)],
        out_specs=[row(d), pl.BlockSpec((d, tm), lambda i: (0, i)), pl.BlockSpec((1, 8, d), lambda i: (i, 0, 0))],
        scratch_shapes=[pltpu.VMEM((tm, dfp), BF16)],
        compiler_params=_params(("arbitrary",)),
    )(x2, x2, target, mod8, g_mlp, g_final, wg_c, wv_c, cwg, cwv, wd)


def _ffn_bwd(x2, dx3, mod8, g_mlp, wg, wv, cwg, cwv, wd, tm, cf):
    t, d = x2.shape
    dfp = wg.shape[1]
    nt, nc = t // tm, dfp // cf
    hb = tm // HALO
    nhb = t // HALO
    n = tm + HALO

    def body(x_ref, xp_ref, xn_ref, dx_ref, dxn_ref, mod_ref, g_ref, wg_ref, wv_ref, cg_ref, cv_ref, wd_ref,
             dx2_ref, dug_ref, duv_ref, act_ref, dxg_ref, part_ref, pcg_ref, pcv_ref):
        i = pl.program_id(0)
        xe = jnp.concatenate([xp_ref[...], x_ref[...], xn_ref[...]], axis=0)
        h = _norm_mod(xe, g_ref[...], mod_ref[4:5, :], mod_ref[3:4, :]).astype(BF16)
        h = jnp.concatenate([jnp.where(i > 0, h[:HALO], jnp.zeros_like(h[:HALO])), h[HALO:]], axis=0)
        dx = dx_ref[...] * mod_ref[5:6, :]
        dxn = jnp.where(i < nt - 1, dxn_ref[...] * mod_ref[5:6, :], 0.0)
        de = jnp.concatenate([dx, dxn], axis=0).astype(BF16)
        dxg_ref[...] = de[:tm]
        pcg_ref[0] = jnp.zeros((8, dfp), F32)
        pcv_ref[0] = jnp.zeros((8, dfp), F32)

        def products(c):
            cols = slice(c * cf, (c + 1) * cf)
            return (jnp.dot(h, wg_ref[:, cols], preferred_element_type=F32), jnp.dot(h, wv_ref[:, cols], preferred_element_type=F32),
                    lax.dot_general(de, wd_ref[cols, :], NT_DIMS, preferred_element_type=F32))

        def back(du, cw, up, s1, s2, pc_ref, cols):
            dup = (cw[2:3, :] * du + cw[1:2, :] * pltpu.roll(du, n - 1, 0) + cw[0:1, :] * pltpu.roll(du, n - 2, 0))[:tm]
            dut = du[:tm]
            pc_ref[0, 0:1, cols] = jnp.sum(dut * s2[HALO:HALO + tm], axis=0, keepdims=True)
            pc_ref[0, 1:2, cols] = jnp.sum(dut * s1[HALO:HALO + tm], axis=0, keepdims=True)
            pc_ref[0, 2:3, cols] = jnp.sum(dut * up[HALO:HALO + tm], axis=0, keepdims=True)
            pc_ref[0, 3:4, cols] = jnp.sum(dut, axis=0, keepdims=True)
            return dup.astype(BF16)

        def chunk(c, prods):
            cols = slice(c * cf, (c + 1) * cf)
            upg, upv, dact = prods
            cg, cv = cg_ref[:, cols], cv_ref[:, cols]
            ug, g1, g2 = _conv_taps(upg, cg, HALO, n)
            uv, v1, v2 = _conv_taps(upv, cv, HALO, n)
            sg = jax.nn.sigmoid(ug)
            sil = ug * sg
            act_ref[cols, :] = _transposed((sil * uv)[:tm].astype(BF16))
            dug_ref[:, cols] = back(dact * uv * (sg * (1.0 + ug * (1.0 - sg))), cg, upg, g1, g2, pcg_ref, cols)
            duv_ref[:, cols] = back(dact * sil, cv, upv, v1, v2, pcv_ref, cols)

        for c0 in range(0, nc, 2):
            group = list(range(c0, min(c0 + 2, nc)))
            prods = [products(c) for c in group]
            for c, pr in zip(group, prods):
                chunk(c, pr)

        dh = (lax.dot_general(dug_ref[...], wg_ref[...], NT_DIMS, preferred_element_type=F32)
              + lax.dot_general(duv_ref[...], wv_ref[...], NT_DIMS, preferred_element_type=F32))
        dxt, dshift, dscale, dgn = _norm_mod_bwd(x_ref[...], dh, g_ref[...], mod_ref[4:5, :])
        dx2_ref[...] = dx_ref[...] + dxt
        part_ref[0] = jnp.zeros((8, d), F32)
        part_ref[0, 0:1, :] = dshift
        part_ref[0, 1:2, :] = dscale
        part_ref[0, 2:3, :] = dgn

    row = lambda w: pl.BlockSpec((tm, w), lambda i: (i, 0))
    prev = pl.BlockSpec((HALO, d), lambda i: (jnp.maximum(i * hb - 1, 0), 0))
    nxt = pl.BlockSpec((HALO, d), lambda i: (jnp.minimum((i + 1) * hb, nhb - 1), 0))
    full = lambda a: pl.BlockSpec(a.shape, lambda i: (0,) * a.ndim)
    once = lambda a: pl.BlockSpec(a.shape, lambda i: (0,) * a.ndim, pipeline_mode=pl.Buffered(1))
    part = lambda w: pl.BlockSpec((1, 8, w), lambda i: (i, 0, 0))
    return _pcall(
        body, name="ffn_bwd", grid=(nt,),
        out_shape=[jax.ShapeDtypeStruct((t, d), F32), jax.ShapeDtypeStruct((t, dfp), BF16),
                   jax.ShapeDtypeStruct((t, dfp), BF16), jax.ShapeDtypeStruct((dfp, t), BF16),
                   jax.ShapeDtypeStruct((t, d), BF16), jax.ShapeDtypeStruct((nt, 8, d), F32),
                   jax.ShapeDtypeStruct((nt, 8, dfp), F32), jax.ShapeDtypeStruct((nt, 8, dfp), F32)],
        in_specs=[row(d), prev, nxt, row(d), nxt, full(mod8), full(g_mlp), once(wg), once(wv), once(cwg), once(cwv), once(wd)],
        out_specs=[row(d), row(dfp), row(dfp), pl.BlockSpec((dfp, tm), lambda i: (0, i)), row(d), part(d), part(dfp), part(dfp)],
        compiler_params=_params(("arbitrary",)),
    )(x2, x2, x2, dx3, dx3, mod8, g_mlp, wg, wv, cwg, cwv, wd)


BLK = 2 * LANES
XROWS = 144
LANE_FS, LANE_FT_A, LANE_FT_B = 0, 3, 6


def _head_masks():
    lane = lax.broadcasted_iota(jnp.int32, (1, LANES), 1)
    in_a = lane < HEAD_DIM
    return in_a, jnp.logical_not(in_a)


def _pieces3(x):
    hi = x.astype(BF16).astype(F32)
    r = x - hi
    mid = r.astype(BF16).astype(F32)
    return hi, mid, (r - mid).astype(BF16).astype(F32)


def _bias_lanes(rows, entries):
    sub = lax.broadcasted_iota(jnp.int32, (16, 1), 0)
    out = jnp.zeros((16, rows), F32)
    for l, v in entries:
        out = jnp.where(sub == l, v, out)
    return jnp.concatenate([out, jnp.zeros((LANES - 16, rows), F32)], axis=0).T


def _three(first, values):
    return [(first + k, v) for k, v in enumerate(values)]


def _stack_rows(x, in_a, in_b):
    zero = jnp.zeros_like(x)
    return jnp.concatenate([jnp.where(in_a, x, zero), jnp.where(in_b, x, zero)], axis=0)


def _transposed(x):
    return x.astype(F32).T.astype(BF16)


def _attn_operands(qkv, frow, dg):
    t = qkv.shape[0]
    p, nk = dg // LANES, t // BLK

    def body(qf_ref, kf_ref, vf_ref, ks_ref, vs_ref, f_ref, qx_ref, kx_ref, kxt_ref, vf_o, vft_o, ks_o, kst_o, vs_o, vst_o):
        in_a, in_b = _head_masks()
        fa, fb = _pieces3(f_ref[0]), _pieces3(f_ref[1])
        qx_ref[0, :, :LANES] = qf_ref[...]
        qx_ref[0, :, LANES:] = _bias_lanes(
            BLK, _three(LANE_FS, (-1.0,) * 3) + _three(LANE_FT_A, fa) + _three(LANE_FT_B, fb)).astype(BF16)
        kf = kf_ref[...]
        zero = jnp.zeros_like(kf)
        top = jnp.concatenate([jnp.where(in_a, kf, zero), _bias_lanes(
            BLK, _three(LANE_FS, fa) + _three(LANE_FT_A, (1.0,) * 3)).astype(BF16)], axis=1)
        bot = jnp.concatenate([jnp.where(in_b, kf, zero), _bias_lanes(
            BLK, _three(LANE_FS, fb) + _three(LANE_FT_B, (1.0,) * 3)).astype(BF16)], axis=1)
        kx = jnp.concatenate([top, bot], axis=0)
        kx_ref[0, 0] = kx
        kxt_ref[0, 0] = _transposed(kx)[:XROWS]
        for src, dst, dst_t in ((vf_ref, vf_o, vft_o), (ks_ref, ks_o, kst_o), (vs_ref, vs_o, vst_o)):
            st = _stack_rows(src[...], in_a, in_b)
            dst[0, 0] = st
            dst_t[0, 0] = _transposed(st)

    col = lambda base: pl.BlockSpec((BLK, LANES), lambda h, j: (j, base * p + h))
    blk4 = lambda r, c: pl.BlockSpec((1, 1, r, c), lambda h, j: (h, j, 0, 0))
    shp4 = lambda r, c: jax.ShapeDtypeStruct((p, nk, r, c), BF16)
    return _pcall(
        body, name="attn_operands", grid=(p, nk),
        out_shape=[jax.ShapeDtypeStruct((p, t, 2 * LANES), BF16), shp4(2 * BLK, 2 * LANES), shp4(XROWS, 2 * BLK)]
        + [shp4(2 * BLK, LANES), shp4(LANES, 2 * BLK)] * 3,
        in_specs=[col(0), col(1), col(2), col(4), col(5), pl.BlockSpec((2, 1, BLK), lambda h, j: (h, 0, j))],
        out_specs=[pl.BlockSpec((1, BLK, 2 * LANES), lambda h, j: (h, j, 0)), blk4(2 * BLK, 2 * LANES), blk4(XROWS, 2 * BLK)]
        + [blk4(2 * BLK, LANES), blk4(LANES, 2 * BLK)] * 3,
        compiler_params=_params(("arbitrary", "arbitrary")),
    )(qkv, qkv, qkv, qkv, qkv, frow)


def _key_query_masks():
    key = lax.broadcasted_iota(jnp.int32, (BLK, BLK), 0)
    qry = lax.broadcasted_iota(jnp.int32, (BLK, BLK), 1)
    return key <= qry, key < qry


def _key_triangle(kind):
    s = lax.broadcasted_iota(jnp.int32, (BLK, BLK), 0)
    j = lax.broadcasted_iota(jnp.int32, (BLK, BLK), 1)
    return {"suffix": j >= s, "prefix": j <= s, "before": j < s}[kind].astype(BF16)


def _tri_dot(tri, x, passes):
    acc = None
    for _ in range(passes):
        part = x.astype(BF16)
        d = jnp.dot(tri, part, preferred_element_type=F32)
        acc = d if acc is None else acc + d
        x = x - part.astype(F32)
    return acc


GROUPS = (4, 2, 1)


def _loop_blocks(n, tiles, carry, descending=False, groups=GROUPS):
    at = (lambda k: n - 1 - k) if descending else (lambda k: k)
    done = 0
    for g in groups:
        left = n - done
        carry = lax.fori_loop(0, left // g, lambda h, c, g=g, done=done: tiles([at(done + g * h + k) for k in range(g)], c), carry)
        done = done + (left // g) * g
    return carry


def _resident(shape):
    return pl.BlockSpec((1,) + shape, lambda h, i: (h,) + (0,) * len(shape), pipeline_mode=pl.Buffered(1))


def _rows_per_head(a, b):
    return jnp.concatenate([jnp.broadcast_to(a, (HEAD_DIM, BLK)), jnp.broadcast_to(b, (HEAD_DIM, BLK))], axis=0)


def _fold_heads(stacked, in_a):
    return jnp.where(in_a, stacked[:BLK], stacked[BLK:])


def _xy_gather_copies(ins, outs, send_sems, recv_sems, local_sems):
    x, y, c = lax.axis_index("x"), lax.axis_index("y"), lax.axis_index("c")
    chips = [(1 - x, y), (x, 1 - y), (1 - x, 1 - y)]
    mine = 2 * x + y
    local, remote = [], []
    for w in range(len(ins)):
        local.append(pltpu.make_async_copy(ins[w], outs[w].at[mine], local_sems.at[w]))
        for k, (px, py) in enumerate(chips):
            remote.append(pltpu.make_async_remote_copy(
                src_ref=ins[w], dst_ref=outs[w].at[mine], send_sem=send_sems.at[3 * w + k],
                recv_sem=recv_sems.at[3 * w + k], device_id=(px, py, c), device_id_type=MESH))
    return local, remote


def _fox_fwd(qx, kx, v_t, dg, shards):
    p, t = qx.shape[0], qx.shape[1]
    nq = t // BLK
    nh = 2 * p
    ns = len(shards)

    def body(q_ref, k_ref, vt_ref, *rest):
        shard_refs, (o_ref, lse_ref), gathered = rest[:ns], rest[ns:ns + 2], rest[ns + 2:2 * ns + 2]
        local, remote = _xy_gather_copies(shard_refs, gathered, *rest[2 * ns + 2:])
        i = pl.program_id(1)

        @pl.when((pl.program_id(0) == 0) & (i == 0))
        def _():
            for cp in local + remote:
                cp.start()

        causal, _ = _key_query_masks()
        q = q_ref[0]

        def scores(j, masked):
            s2 = lax.dot_general(k_ref[0, j], q, NT_DIMS, preferred_element_type=F32)
            s = [s2[a * BLK:(a + 1) * BLK] for a in range(2)]
            return [jnp.where(causal, x, NEG) for x in s] if masked else s

        def update(blocks, carry):
            m, l, acc = list(carry[0]), list(carry[1]), carry[2]
            for j, s in blocks:
                alpha, pr = [], []
                for a in range(2):
                    mn = jnp.maximum(m[a], jnp.max(s[a], axis=0, keepdims=True))
                    pa = jnp.exp(s[a] - mn)
                    al = jnp.exp(m[a] - mn)
                    l[a] = al * l[a] + jnp.sum(pa, axis=0, keepdims=True)
                    m[a] = mn
                    alpha.append(al)
                    pr.append(pa.astype(BF16))
                acc = _rows_per_head(*alpha) * acc + jnp.dot(vt_ref[0, j], jnp.concatenate(pr, axis=0), preferred_element_type=F32)
            return tuple(m), tuple(l), acc

        tiles = lambda js, c: update([(j, scores(j, False)) for j in js], c)
        neg, zero = jnp.full((1, BLK), NEG, F32), jnp.zeros((1, BLK), F32)
        carry = _loop_blocks(i, tiles, ((neg, neg), (zero, zero), jnp.zeros((LANES, BLK), F32)), groups=(8, 4, 2, 1))
        m, l, acc = update([(i, scores(i, True))], carry)
        o_ref[...] = acc / _rows_per_head(*l)
        lse_ref[0] = m[0] + jnp.log(l[0])
        lse_ref[1] = m[1] + jnp.log(l[1])

        @pl.when((pl.program_id(0) == p - 1) & (i == nq - 1))
        def _():
            for cp in remote:
                cp.wait_recv()
            for cp in remote:
                cp.wait_send()
            for cp in local:
                cp.wait()

    row = pl.BlockSpec((2, 1, BLK), lambda h, i: (h, 0, i))
    hbm = pl.BlockSpec(memory_space=pltpu.HBM)
    return _pcall(
        body, name="fox_fwd", grid=(p, nq),
        out_shape=[jax.ShapeDtypeStruct((dg, t), F32), jax.ShapeDtypeStruct((nh, 1, t), F32)]
        + [jax.ShapeDtypeStruct((4,) + s.shape, s.dtype) for s in shards],
        in_specs=[pl.BlockSpec((1, BLK, 2 * LANES), lambda h, i: (h, i, 0)), _resident((nq, 2 * BLK, 2 * LANES)),
                  _resident((nq, LANES, 2 * BLK))] + [hbm] * ns,
        out_specs=[pl.BlockSpec((LANES, BLK), lambda h, i: (h, i)), row] + [hbm] * ns,
        scratch_shapes=[pltpu.SemaphoreType.DMA((3 * ns,)), pltpu.SemaphoreType.DMA((3 * ns,)), pltpu.SemaphoreType.DMA((ns,))],
        compiler_params=_params(("arbitrary", "arbitrary")),
    )(qx, kx, v_t, *shards)


def _swap_copies(ins, outs, send_sems, recv_sems, local_sems):
    x, y, c = lax.axis_index("x"), lax.axis_index("y"), lax.axis_index("c")
    local, remote = [], []
    for w in range(len(ins)):
        local.append(pltpu.make_async_copy(ins[w], outs[w].at[c], local_sems.at[w]))
        remote.append(pltpu.make_async_remote_copy(
            src_ref=ins[w], dst_ref=outs[w].at[c], send_sem=send_sems.at[w], recv_sem=recv_sems.at[w],
            device_id=(x, y, 1 - c), device_id_type=MESH))
    return local, remote


def _fox_bwd(qx, kx, kx_t, v_st, do, delta, lse, dg, halves):
    p, t = qx.shape[0], qx.shape[1]
    nq = t // BLK
    nh = 2 * p
    ns = len(halves)

    def body(q_ref, k_ref, kt_ref, v_ref, do_ref, dl_ref, lse_ref, *rest):
        half_refs, (dq_ref, dft_ref, dk_ref, dv_ref, dkx_ref), both_refs = rest[:ns], rest[ns:ns + 5], rest[ns + 5:2 * ns + 5]
        local, remote = _swap_copies(half_refs, both_refs, *rest[2 * ns + 5:])
        i = pl.program_id(1)

        @pl.when((pl.program_id(0) == 0) & (i == 0))
        def _():
            for cp in local + remote:
                cp.start()

        @pl.when(i == 0)
        def _():
            dk_ref[...] = jnp.zeros_like(dk_ref)
            dv_ref[...] = jnp.zeros_like(dv_ref)
            dkx_ref[...] = jnp.zeros_like(dkx_ref)

        in_a, _ = _head_masks()
        first_lane = lax.broadcasted_iota(jnp.int32, (1, LANES), 1) == 0
        causal, _ = _key_query_masks()
        q, do2 = q_ref[0], do_ref[...]
        dl = (dl_ref[0], dl_ref[1])
        lse = (lse_ref[0], lse_ref[1])

        def products(j):
            return (lax.dot_general(k_ref[0, j], q, NT_DIMS, preferred_element_type=F32),
                    lax.dot_general(v_ref[0, j], do2, NT_DIMS, preferred_element_type=F32))

        def dscores(prod, masked):
            s2, dp2 = prod
            pr, ds = [], []
            for a in range(2):
                s = s2[a * BLK:(a + 1) * BLK]
                if masked:
                    s = jnp.where(causal, s, NEG)
                pa = jnp.exp(s - lse[a])
                ds.append((pa * (dp2[a * BLK:(a + 1) * BLK] - dl[a])).astype(BF16))
                pr.append(pa.astype(BF16))
            return jnp.concatenate(ds, axis=0), jnp.concatenate(pr, axis=0)

        def accumulate(j, dsb, prb, dq):
            off = pl.multiple_of(j * BLK, BLK)
            dk_full = jnp.dot(dsb, q, preferred_element_type=F32)
            dk_ref[pl.ds(off, BLK), :] += _fold_heads(dk_full[:, :LANES], in_a)
            dkx_ref[pl.ds(off, BLK), :] += jnp.where(first_lane, dk_full[:BLK, LANES:], dk_full[BLK:, LANES:])
            dv_ref[pl.ds(off, BLK), :] += _fold_heads(jnp.dot(prb, do2, preferred_element_type=F32), in_a)
            return dq + jnp.dot(kt_ref[0, j], dsb, preferred_element_type=F32)

        def tiles(js, dq, masked=False):
            prods = [products(j) for j in js]
            grads = [dscores(pr, masked) for pr in prods]
            for j, (dsb, prb) in zip(js, grads):
                dq = accumulate(j, dsb, prb, dq)
            return dq

        dq = _loop_blocks(i, tiles, jnp.zeros((XROWS, BLK), F32))
        dq = tiles([i], dq, True)
        dq_ref[...] = dq[:LANES]
        dft_ref[0] = dq[LANES + LANE_FT_A:LANES + LANE_FT_A + 1]
        dft_ref[1] = dq[LANES + LANE_FT_B:LANES + LANE_FT_B + 1]

        @pl.when((pl.program_id(0) == p - 1) & (i == nq - 1))
        def _():
            for cp in remote:
                cp.wait_recv()
            for cp in remote:
                cp.wait_send()
            for cp in local:
                cp.wait()

    row = pl.BlockSpec((2, 1, BLK), lambda h, i: (h, 0, i))
    acc = pl.BlockSpec((t, LANES), lambda h, i: (0, h))
    hbm = pl.BlockSpec(memory_space=pltpu.HBM)
    return _pcall(
        body, name="fox_bwd", grid=(p, nq),
        out_shape=[jax.ShapeDtypeStruct((dg, t), F32), jax.ShapeDtypeStruct((nh, 1, t), F32)] + [jax.ShapeDtypeStruct((t, dg), F32)] * 3
        + [jax.ShapeDtypeStruct((2,) + h.shape, h.dtype) for h in halves],
        in_specs=[pl.BlockSpec((1, BLK, 2 * LANES), lambda h, i: (h, i, 0)), _resident((nq, 2 * BLK, 2 * LANES)),
                  _resident((nq, XROWS, 2 * BLK)), _resident((nq, 2 * BLK, LANES)),
                  pl.BlockSpec((BLK, LANES), lambda h, i: (i, h)), row, row] + [hbm] * ns,
        out_specs=[pl.BlockSpec((LANES, BLK), lambda h, i: (h, i)), row, acc, acc, acc] + [hbm] * ns,
        scratch_shapes=[pltpu.SemaphoreType.DMA((ns,)), pltpu.SemaphoreType.DMA((ns,)), pltpu.SemaphoreType.DMA((ns,))],
        compiler_params=_params(("arbitrary", "arbitrary")),
    )(qx, kx, kx_t, v_st, do, delta, lse, *halves)


def _softplus_of(z):
    return jnp.maximum(z, 0.0) + jnp.log(1.0 + jnp.exp2(jnp.abs(z) * (-LOG2_E)))


def _sb_fwd(qkv, k_st, v_t, dg):
    t = qkv.shape[0]
    p, nq = dg // LANES, t // BLK
    nh = 2 * p

    def body(q_ref, k_ref, vt_ref, o_ref, rt_ref):
        i = pl.program_id(1)
        _, strict = _key_query_masks()
        suffix = _key_triangle("suffix")
        q = q_ref[...]

        def scores(j):
            z2 = lax.dot_general(k_ref[0, j], q, NT_DIMS, preferred_element_type=F32)
            return [z2[a * BLK:(a + 1) * BLK] for a in range(2)]

        def suffix_sums(z, masked):
            out = []
            for a in range(2):
                sp = _softplus_of(z[a])
                if masked:
                    sp = jnp.where(strict, sp, 0.0)
                out.append(_tri_dot(suffix, sp, 2))
            return out

        def weights(z, cs, rest, masked):
            w, rest_new = [], []
            for a in range(2):
                wa = jnp.exp(z[a] - cs[a] - rest[a])
                if masked:
                    wa = jnp.where(strict, wa, 0.0)
                w.append(wa.astype(BF16))
                rest_new.append(rest[a] + cs[a][0:1])
            return jnp.concatenate(w, axis=0), tuple(rest_new)

        def tiles(js, carry, masked=False):
            flags = masked if isinstance(masked, tuple) else (masked,) * len(js)
            rest, acc = carry
            zs = [scores(j) for j in js]
            css = [suffix_sums(z, f) for z, f in zip(zs, flags)]
            ws = []
            for z, cs, f in zip(zs, css, flags):
                w2, rest = weights(z, cs, rest, f)
                ws.append(w2)
            for j, w2 in zip(js, ws):
                acc = acc + jnp.dot(vt_ref[0, j], w2, preferred_element_type=F32)
            return rest, acc

        zero = jnp.zeros((1, BLK), F32)
        carry = ((zero, zero), jnp.zeros((LANES, BLK), F32))
        paired = jnp.minimum(i, 1)
        carry = lax.fori_loop(0, paired, lambda _, c: tiles([i, i - 1], c, (True, False)), carry)
        carry = lax.fori_loop(0, 1 - paired, lambda _, c: tiles([i], c, True), carry)
        rest, acc = _loop_blocks(jnp.maximum(i - 1, 0), tiles, carry, descending=True, groups=(8, 4, 2, 1))
        o_ref[...] = acc
        rt_ref[0] = rest[0]
        rt_ref[1] = rest[1]

    return _pcall(
        body, name="sb_fwd", grid=(p, nq),
        out_shape=[jax.ShapeDtypeStruct((dg, t), F32), jax.ShapeDtypeStruct((nh, 1, t), F32)],
        in_specs=[pl.BlockSpec((BLK, LANES), lambda h, i: (i, 3 * p + h)), _resident((nq, 2 * BLK, LANES)),
                  _resident((nq, LANES, 2 * BLK))],
        out_specs=[pl.BlockSpec((LANES, BLK), lambda h, i: (h, i)), pl.BlockSpec((2, 1, BLK), lambda h, i: (h, 0, i))],
        compiler_params=_params(("arbitrary", "arbitrary")),
    )(qkv, k_st, v_t)


def _scatter8_copies(ins, outs, send_sems, recv_sems, local_sems):
    x, y, c = lax.axis_index("x"), lax.axis_index("y"), lax.axis_index("c")
    me = 4 * x + 2 * y + c
    local, remote = [], []
    for w in range(len(ins)):
        local.append(pltpu.make_async_copy(ins[w].at[me], outs[w].at[me], local_sems.at[w]))
        for f in range(1, 8):
            px = 1 - x if f & 4 else x
            py = 1 - y if f & 2 else y
            pc = 1 - c if f & 1 else c
            remote.append(pltpu.make_async_remote_copy(
                src_ref=ins[w].at[4 * px + 2 * py + pc], dst_ref=outs[w].at[me],
                send_sem=send_sems.at[7 * w + f - 1], recv_sem=recv_sems.at[7 * w + f - 1],
                device_id=(px, py, pc), device_id_type=MESH))
    return local, remote


def _sb_bwd(qkv, k_st, k_t, v_st, do, rtot, dg, pieces):
    t = qkv.shape[0]
    p, nq = dg // LANES, t // BLK
    ns = len(pieces)

    def body(q_ref, k_ref, kt_ref, v_ref, do_ref, rt_ref, *rest):
        piece_refs, (dq_ref, dk_ref, dv_ref), recv_refs = rest[:ns], rest[ns:ns + 3], rest[ns + 3:2 * ns + 3]
        local, remote = _scatter8_copies(piece_refs, recv_refs, *rest[2 * ns + 3:])
        i = pl.program_id(1)

        @pl.when((pl.program_id(0) == 0) & (i == 0))
        def _():
            for cp in local + remote:
                cp.start()

        @pl.when(i == 0)
        def _():
            dk_ref[...] = jnp.zeros_like(dk_ref)
            dv_ref[...] = jnp.zeros_like(dv_ref)

        in_a, _ = _head_masks()
        _, strict = _key_query_masks()
        before_m, prefix_m = _key_triangle("before"), _key_triangle("prefix")
        q, do2 = q_ref[...], do_ref[...]
        rt = (rt_ref[0], rt_ref[1])

        def products(j):
            z2 = lax.dot_general(k_ref[0, j], q, NT_DIMS, preferred_element_type=F32)
            da2 = lax.dot_general(v_ref[0, j], do2, NT_DIMS, preferred_element_type=F32)
            return [z2[a * BLK:(a + 1) * BLK] for a in range(2)], [da2[a * BLK:(a + 1) * BLK] for a in range(2)]

        def softplus_sums(z, masked):
            sp = [_softplus_of(x) for x in z]
            if masked:
                sp = [jnp.where(strict, x, 0.0) for x in sp]
            return sp, [_tri_dot(before_m, x, 2) for x in sp]

        def weight_grads(z, da, sp, pre, before, masked):
            w, g, pg, before_new = [], [], [], []
            for a in range(2):
                wa = jnp.exp(z[a] + (before[a] - rt[a]) + pre[a])
                if masked:
                    wa = jnp.where(strict, wa, 0.0)
                ga = wa * da[a]
                w.append(wa.astype(BF16))
                g.append(ga)
                pg.append(jnp.dot(prefix_m, ga.astype(BF16), preferred_element_type=F32))
                before_new.append(before[a] + pre[a][BLK - 1:BLK] + sp[a][BLK - 1:BLK])
            return jnp.concatenate(w, axis=0), g, pg, tuple(before_new)

        def dlogits(sp, g, pg, gbefore, masked):
            dz, gbefore_new = [], []
            for a in range(2):
                s_incl = gbefore[a] + pg[a]
                dza = (g[a] - s_incl) + jnp.exp2(sp[a] * (-LOG2_E)) * s_incl
                if masked:
                    dza = jnp.where(strict, dza, 0.0)
                dz.append(dza.astype(BF16))
                gbefore_new.append(s_incl[BLK - 1:BLK])
            return jnp.concatenate(dz, axis=0), tuple(gbefore_new)

        def accumulate(j, dzb, wb, dq):
            off = pl.multiple_of(j * BLK, BLK)
            dk_ref[pl.ds(off, BLK), :] += _fold_heads(jnp.dot(dzb, q, preferred_element_type=F32), in_a)
            dv_ref[pl.ds(off, BLK), :] += _fold_heads(jnp.dot(wb, do2, preferred_element_type=F32), in_a)
            return dq + jnp.dot(kt_ref[0, j], dzb, preferred_element_type=F32)

        def tiles(js, carry, masked=False):
            flags = masked if isinstance(masked, tuple) else (masked,) * len(js)
            before, gbefore, dq = carry
            prods = [products(j) for j in js]
            sums = [softplus_sums(z, f) for (z, _), f in zip(prods, flags)]
            grads = []
            for (z, da), (sp, pre), f in zip(prods, sums, flags):
                wb, g, pg, before = weight_grads(z, da, sp, pre, before, f)
                grads.append((wb, g, pg))
            for j, (sp, _), (wb, g, pg), f in zip(js, sums, grads, flags):
                dzb, gbefore = dlogits(sp, g, pg, gbefore, f)
                dq = accumulate(j, dzb, wb, dq)
            return before, gbefore, dq

        zero = jnp.zeros((1, BLK), F32)
        carry = _loop_blocks(jnp.maximum(i - 1, 0), tiles, ((zero, zero), (zero, zero), jnp.zeros((LANES, BLK), F32)),
                             groups=(3, 2, 1))
        paired = jnp.minimum(i, 1)
        carry = lax.fori_loop(0, paired, lambda _, c: tiles([i - 1, i], c, (False, True)), carry)
        carry = lax.fori_loop(0, 1 - paired, lambda _, c: tiles([i], c, True), carry)
        dq_ref[...] = carry[2]

        @pl.when((pl.program_id(0) == p - 1) & (i == nq - 1))
        def _():
            for cp in remote:
                cp.wait_recv()
            for cp in remote:
                cp.wait_send()
            for cp in local:
                cp.wait()

    acc = pl.BlockSpec((t, LANES), lambda h, i: (0, h))
    hbm = pl.BlockSpec(memory_space=pltpu.HBM)
    return _pcall(
        body, name="sb_bwd", grid=(p, nq),
        out_shape=[jax.ShapeDtypeStruct((dg, t), F32)] + [jax.ShapeDtypeStruct((t, dg), F32)] * 2
        + [jax.ShapeDtypeStruct(pc.shape, pc.dtype) for pc in pieces],
        in_specs=[pl.BlockSpec((BLK, LANES), lambda h, i: (i, 3 * p + h)), _resident((nq, 2 * BLK, LANES)),
                  _resident((nq, LANES, 2 * BLK)), _resident((nq, 2 * BLK, LANES)),
                  pl.BlockSpec((BLK, LANES), lambda h, i: (i, h)), pl.BlockSpec((2, 1, BLK), lambda h, i: (h, 0, i))] + [hbm] * ns,
        out_specs=[pl.BlockSpec((LANES, BLK), lambda h, i: (h, i)), acc, acc] + [hbm] * ns,
        scratch_shapes=[pltpu.SemaphoreType.DMA((7 * ns,)), pltpu.SemaphoreType.DMA((7 * ns,)), pltpu.SemaphoreType.DMA((ns,))],
        compiler_params=_params(("arbitrary", "arbitrary")),
    )(qkv, k_st, k_t, v_st, do, rtot, *pieces)


def _tri_constants(nh, t):
    nb = t // LANES
    r = nh * nb
    li = np.arange(LANES)
    tri_in = (li[:, None] <= li[None, :])
    ri = np.arange(r)
    same = (ri[:, None] // nb) == (ri[None, :] // nb)
    blk = same & (ri[None, :] < ri[:, None])
    blk_rev = same & (ri[None, :] > ri[:, None])
    head_rows = (np.arange(max(8, nh))[:, None] == (ri[None, :] // nb))
    as_bf16 = lambda a: jnp.asarray(a.astype(np.float32), BF16)
    return as_bf16(tri_in), as_bf16(blk), as_bf16(tri_in.T), as_bf16(blk_rev), as_bf16(head_rows)


def kernel(x, c, w_ada, b_ada, g_attn, w_in, b_fgate, g_out_fox, g_out_sb, w_out, g_mlp, w_up, conv_w, conv_b, w_down, g_final, loss_target, m_w_ada, m_b_ada, m_g_attn, m_w_in, m_b_fgate, m_g_out_fox, m_g_out_sb, m_w_out, m_g_mlp, m_w_up, m_conv_w, m_conv_b, m_w_down, m_g_final, v_w_ada, v_b_ada, v_g_attn, v_w_in, v_b_fgate, v_g_out_fox, v_g_out_sb, v_w_out, v_g_mlp, v_w_up, v_conv_w, v_conv_b, v_w_down, v_g_final):
    t, d = x.shape[1], x.shape[2]
    dg = d // 2
    nh = dg // HEAD_DIM
    n_in = 6 * dg + nh
    dff = w_down.shape[1] * 4
    dfp = -(-dff // 256) * 256
    cf = 256
    tm = _tile(t, (512, 256, 128))
    nq = t // BLK
    xi, yi, ci = lax.axis_index("x"), lax.axis_index("y"), lax.axis_index("c")
    shard = 2 * xi + yi
    me = 4 * xi + 2 * yi + ci

    x2d, tg2d = x[0], loss_target[0]

    c_all = _all_gather8(jnp.pad(c, ((0, 7), (0, 0)))).reshape(8, 8, d)[:, 0, :]
    ada_cols = w_ada.shape[2]
    b_shard = lax.dynamic_slice(b_ada, (0, shard * ada_cols), (1, ada_cols))
    sc_all, mod_shard = _ada_fwd(c_all, w_ada[0], b_shard)
    mod_all = _all_gather8(mod_shard).reshape(4, 2, 8, ada_cols)
    mod_me = lax.dynamic_index_in_dim(mod_all[:, 0], me, axis=1, keepdims=False)
    mod8 = jnp.pad(mod_me.reshape(6, d), ((0, 2), (0, 0)))

    (g_in,) = _gather_xy([w_in[0].astype(BF16)])
    later_shards = [w_out[0].astype(BF16), w_up[0].astype(BF16), w_down[0].astype(BF16), conv_w[0]]
    w_in_full = jnp.transpose(g_in, (1, 0, 2)).reshape(d, n_in)
    w_qkv = w_in_full[:, :6 * dg]
    w_f = jnp.pad(w_in_full[:, 6 * dg:].T, ((0, LANES - nh), (0, 0)))

    qkv, fl, h1 = _in_proj_fwd(x2d, mod8, g_attn, w_qkv, w_f, tm)
    tri_in, tri_blk, tri_in_rev, tri_blk_rev, head_rows = _tri_constants(nh, t)
    fl2d = fl[:nh].reshape(nh * t // LANES, LANES)
    b_rows = jnp.repeat(b_fgate[0], t // LANES)[:, None]
    f2d = _fgate_fwd(fl2d, b_rows, tri_in, tri_blk)
    frow = f2d.reshape(nh, 1, t)
    pairs = nh // 2
    qx, kx, kx_t, vf_st, vf_t, ks_st, ks_t, vs_st, vs_t = _attn_operands(qkv, frow, dg)
    o_fox, lse, g_out, g_up, g_down, g_cw = _fox_fwd(qx, kx, vf_t, dg, later_shards)
    w_out_full = g_out.reshape(2 * dg, d)
    w_up_full = jnp.transpose(g_up, (1, 0, 2)).reshape(d, 2 * dff)
    padc = ((0, 0), (0, dfp - dff))
    wg, wv = jnp.pad(w_up_full[:, :dff], padc), jnp.pad(w_up_full[:, dff:], padc)
    wd = jnp.pad(g_down.reshape(dff, d), ((0, dfp - dff), (0, 0)))
    cw_full = jnp.transpose(g_cw, (1, 0, 2)).reshape(3, 2 * dff)
    cw4 = jnp.concatenate([cw_full, conv_b], axis=0)
    cwg = jnp.pad(cw4[:, :dff], ((0, 4), (0, dfp - dff)))
    cwv = jnp.pad(cw4[:, dff:], ((0, 4), (0, dfp - dff)))
    o_sb, rtot = _sb_fwd(qkv, ks_st, vs_t, dg)
    li = np.arange(dg)
    bd = jnp.asarray((li[:, None] // HEAD_DIM == li[None, :] // HEAD_DIM).astype(np.float32), BF16)
    hsel = jnp.asarray((np.arange(LANES)[:, None] == li[None, :] // HEAD_DIM).astype(np.float32), BF16)
    x2, mix, mix_t = _attn_out_fwd(x2d, o_fox, o_sb, g_out_fox, g_out_sb, w_out_full, mod8, bd, tm)
    g_final2 = g_final[None, :]
    dx3, h2, part_f = _ffn_fwd(x2, tg2d, mod8, g_mlp, g_final2, wg, wv, cwg, cwv, wd, tm, cf)

    tm_ffn_bwd = _tile(t, (256, 128))
    dx2, dupg, dupv, act, dxg3, part_b, pcg, pcv = _ffn_bwd(x2, dx3, mod8, g_mlp, wg, wv, cwg, cwv, wd, tm_ffn_bwd, cf)
    do_fox, do_sb, delta, dxg2, part_o = _attn_out_bwd(dx2, mix, o_fox, o_sb, g_out_fox, g_out_sb, w_out_full, mod8, bd, hsel, tm)
    drow = delta[:nh].reshape(nh, 1, t)

    def col_pieces(g):
        r, cc = g.shape
        return jnp.transpose(g.reshape(2, r // 2, 4, cc // 4), (2, 0, 1, 3)).reshape(8, r // 2, cc // 4)

    def row_pieces(g):
        r, cc = g.shape
        return g.reshape(8, r // 8, cc)

    gw_out = _matmul_tn(mix_t, dxg2, "grad_w_out")
    gw_upg = _matmul_tn(h2, dupg, "grad_w_up_gate")
    gw_upv = _matmul_tn(h2, dupv, "grad_w_up_val")
    gw_up = jnp.concatenate([gw_upg[:, :dff], gw_upv[:, :dff]], axis=1)
    gw_down = _matmul_tn(act, dxg3, "grad_w_down")[:dff]
    early = (row_pieces(gw_out), col_pieces(gw_up), row_pieces(gw_down))
    early = [_to_bf16(pc, "pieces_bf16_" + nm) for pc, nm in zip(early, ("w_out", "w_up", "w_down"))]

    dq_s, dk_s, dv_s, *recv_early = _sb_bwd(qkv, ks_st, ks_t, vs_st, do_sb, rtot, dg, early)
    halves_early = [_sum_leading(rv, nm) for rv, nm in zip(recv_early, ("sum_w_out", "sum_w_up", "sum_w_down"))]
    dq_f, dft, dk_f, dv_f, dkx, *swapped_early = _fox_bwd(qx, kx, kx_t, vf_st, do_fox, drow, lse, dg, halves_early)
    f2d_shape = (nh * t // LANES, LANES)
    dfs = jnp.transpose(dkx.reshape(t, pairs, LANES)[:, :, :2], (1, 2, 0))
    dfl2d, gb8 = _fgate_bwd(fl2d, b_rows, dft.reshape(f2d_shape), dfs.reshape(f2d_shape), tri_in_rev, tri_blk_rev, head_rows)
    dfl = jnp.pad(dfl2d.reshape(nh, t), ((0, LANES - nh), (0, 0)))
    grad_x, dproj, dflb, part_i = _in_proj_bwd([dq_f, dk_f, dv_f, dq_s, dk_s, dv_s], dfl, w_qkv, w_f, x2d, dx2, mod8, g_attn, tm)

    gw_qkv = _matmul_tn(h1, dproj, "grad_w_qkv")
    gw_f = _matmul_tn(h1, dflb, "grad_w_f")
    gw_in = jnp.concatenate([gw_qkv, gw_f[:, :nh]], axis=1)

    sf = _sum_leading(part_f, "sum_part_ffn_fwd")
    sb_ = _sum_leading(part_b, "sum_part_ffn_bwd")
    so = _sum_leading(part_o, "sum_part_attn_out")
    si = _sum_leading(part_i, "sum_part_in_proj")
    scg = _sum_leading(pcg, "sum_part_conv_gate")
    scv = _sum_leading(pcv, "sum_part_conv_val")
    gb_f = gb8[:nh, 0]
    dmod = jnp.concatenate([si[0], si[1], so[0], sb_[0], sb_[1], sf[1]])
    g_conv_w = jnp.concatenate([scg[0:3, :dff], scv[0:3, :dff]], axis=1).reshape(-1)
    g_conv_b = jnp.concatenate([scg[3, :dff], scv[3, :dff]])
    loss_part = jnp.sum(sf[2])
    fields = [dmod, si[2], gb_f, so[1, :dg], so[1, dg:], sb_[2], g_conv_b, sf[0], g_conv_w, loss_part[None]]
    sizes = [int(f.shape[0]) for f in fields]
    n_pack = sum(sizes)
    lanes_pack = -(-n_pack // (8 * LANES)) * LANES
    pack = jnp.pad(jnp.concatenate(fields), (0, 8 * lanes_pack - n_pack)).reshape(8, lanes_pack)
    gathered = _all_gather8(pack)
    tot = _sum_leading(gathered.reshape(8, 8, lanes_pack), "sum_pack").reshape(-1)
    offs = np.concatenate([[0], np.cumsum(sizes)])
    take = lambda k: tot[int(offs[k]):int(offs[k + 1])]
    g_b_ada, g_g_attn, g_b_fgate, g_g_fox, g_g_sb, g_g_mlp, g_cb, g_g_final, g_cw_full, loss_v = [take(k) for k in range(10)]
    loss = loss_v[0]
    dmod_all = gathered.reshape(8, 8 * lanes_pack)[:, :6 * d]
    dmod_cols = lax.dynamic_slice(dmod_all, (0, shard * ada_cols), (8, ada_cols))
    g_w_ada = _ada_bwd(sc_all.T, dmod_cols)

    (recv_in,) = _scatter8([_to_bf16(col_pieces(gw_in), "pieces_bf16_w_in")])
    (swapped_in,) = _swap_halves([_sum_leading(recv_in, "sum_w_in")])
    swapped = [swapped_in] + swapped_early
    g_w_in, g_w_out, g_w_up, g_w_down = [s.reshape(2 * s.shape[1], s.shape[2]) for s in swapped]
    g_conv_w_shard = lax.dynamic_slice(g_cw_full.reshape(3, 2 * dff), (0, shard * (dff // 2)), (3, dff // 2))

    grads, deltas, new_m, new_v = {}, {}, {}, {}

    def step(name, w, g, m, v):
        shape = w.shape
        as2d = lambda a: a.reshape(-1, shape[-1])
        dl, nm, nv = _adamw(as2d(w), as2d(g), as2d(m), as2d(v), "adamw_" + name)
        grads[name], deltas[name], new_m[name], new_v[name] = g.reshape(shape), dl.reshape(shape), nm.reshape(shape), nv.reshape(shape)

    step("w_ada", w_ada, g_w_ada, m_w_ada, v_w_ada)
    step("w_in", w_in, g_w_in, m_w_in, v_w_in)
    step("w_out", w_out, g_w_out, m_w_out, v_w_out)
    step("w_up", w_up, g_w_up, m_w_up, v_w_up)
    step("conv_w", conv_w, g_conv_w_shard, m_conv_w, v_conv_w)
    step("w_down", w_down, g_w_down, m_w_down, v_w_down)

    small = [("b_ada", b_ada, g_b_ada, m_b_ada, v_b_ada), ("g_attn", g_attn, g_g_attn, m_g_attn, v_g_attn),
             ("b_fgate", b_fgate, g_b_fgate, m_b_fgate, v_b_fgate), ("g_out_fox", g_out_fox, g_g_fox, m_g_out_fox, v_g_out_fox),
             ("g_out_sb", g_out_sb, g_g_sb, m_g_out_sb, v_g_out_sb), ("g_mlp", g_mlp, g_g_mlp, m_g_mlp, v_g_mlp),
             ("conv_b", conv_b, g_cb, m_conv_b, v_conv_b), ("g_final", g_final, g_g_final, m_g_final, v_g_final)]
    ssz = [int(np.prod(s[1].shape)) for s in small]
    n_small = sum(ssz)
    lanes_small = -(-n_small // (8 * LANES)) * LANES
    packs = [jnp.pad(jnp.concatenate([s[k].reshape(-1) for s in small]), (0, 8 * lanes_small - n_small)).reshape(8, lanes_small)
             for k in (1, 2, 3, 4)]
    dl_s, nm_s, nv_s = _adamw(*packs, "adamw_small")
    so_ = np.concatenate([[0], np.cumsum(ssz)])
    for k, s in enumerate(small):
        cut = lambda a: a.reshape(-1)[int(so_[k]):int(so_[k + 1])].reshape(s[1].shape)
        grads[s[0]], deltas[s[0]], new_m[s[0]], new_v[s[0]] = s[2].reshape(s[1].shape), cut(dl_s), cut(nm_s), cut(nv_s)

    order = ["w_ada", "b_ada", "g_attn", "w_in", "b_fgate", "g_out_fox", "g_out_sb", "w_out", "g_mlp", "w_up",
             "conv_w", "conv_b", "w_down", "g_final"]
    return (loss, grad_x[None], *[grads[n] for n in order], *[deltas[n] for n in order],
            *[new_m[n] for n in order], *[new_v[n] for n in order])
```

```python
import functools

import numpy as np
import jax
import jax.numpy as jnp
from jax import lax
from jax.experimental import pallas as pl
from jax.experimental.pallas import tpu as pltpu

F32 = jnp.float32
BF16 = jnp.bfloat16
MESH = pl.DeviceIdType.MESH

HEAD_DIM = 64
LANES = 128
EPS = 1e-6
NEG = -1e30
ADAM_LR, ADAM_B1, ADAM_B2, ADAM_EPS, ADAM_WD, ADAM_STEP = 0.001, 0.9, 0.999, 1e-08, 0.01, 10
V7X_VMEM_BYTES = 64 * 1024 * 1024
VMEM_LIMIT = V7X_VMEM_BYTES - 12 * 1024 * 1024
NT_DIMS = (((1,), (1,)), ((), ()))
LOG2_E = 1.4426950408889634


def _pcall(body, **kw):
    return pl.pallas_call(body, **kw)


def _params(sem=None, **kw):
    return pltpu.CompilerParams(dimension_semantics=sem, vmem_limit_bytes=VMEM_LIMIT, **kw)


def _split_dot(x, m, passes):
    acc = None
    for _ in range(passes):
        part = x.astype(BF16)
        d = jnp.dot(part, m, preferred_element_type=F32)
        acc = d if acc is None else acc + d
        x = x - part.astype(F32)
    return acc


def _tile(n, candidates):
    for t in candidates:
        if n % t == 0:
            return t
    return n


def _rows_tile(rows, row_bytes, budget=2 * 1024 * 1024):
    best = None
    for t in range(8, rows + 1, 8):
        if rows % t == 0 and t * row_bytes <= budget:
            best = t
    return best if best is not None else rows


def _all_gather8(v):
    m_per, n = v.shape

    def body(x_ref, out_ref, send_sems, recv_sems, local_sem):
        x, y, c = lax.axis_index("x"), lax.axis_index("y"), lax.axis_index("c")
        me, sibling = (x, y, c), (x, y, 1 - c)
        chips = [(1 - x, y), (x, 1 - y), (1 - x, 1 - y)]

        def rows(px, py, pc):
            return out_ref.at[pl.ds((4 * px + 2 * py + pc) * m_per, m_per), :]

        def copy(k, block, to, src=None):
            return pltpu.make_async_remote_copy(
                src_ref=rows(*block) if src is None else src, dst_ref=rows(*block),
                send_sem=send_sems.at[k], recv_sem=recv_sems.at[k], device_id=to, device_id_type=MESH)

        mine = pltpu.make_async_copy(x_ref, rows(*me), local_sem)
        mine.start()
        first = [copy(0, me, sibling, src=x_ref)]
        first += [copy(1 + j, me, (*chip, c), src=x_ref) for j, chip in enumerate(chips)]
        for cp in first:
            cp.start()
        passed = [copy(4 + j, (*chip, c), sibling) for j, chip in enumerate(chips)]
        for j, chip in enumerate(chips):
            copy(1 + j, (*chip, c), me).wait_recv()
            passed[j].start()
        copy(0, sibling, me).wait_recv()
        for j, chip in enumerate(chips):
            copy(4 + j, (*chip, 1 - c), me).wait_recv()
        for cp in first + passed:
            cp.wait_send()
        mine.wait()

    return _pcall(
        body, name="all_gather8",
        out_shape=jax.ShapeDtypeStruct((8 * m_per, n), v.dtype),
        in_specs=[pl.BlockSpec(memory_space=pltpu.VMEM)],
        out_specs=pl.BlockSpec(memory_space=pltpu.VMEM),
        scratch_shapes=[pltpu.SemaphoreType.DMA((7,)), pltpu.SemaphoreType.DMA((7,)), pltpu.SemaphoreType.DMA],
        compiler_params=pltpu.CompilerParams(vmem_limit_bytes=VMEM_LIMIT),
    )(v)


def _gather_xy(shards):
    n = len(shards)

    def body(*refs):
        ins, outs = refs[:n], refs[n:2 * n]
        send_sems, recv_sems, local_sems = refs[2 * n:]
        x, y, c = lax.axis_index("x"), lax.axis_index("y"), lax.axis_index("c")
        chips = [(1 - x, y), (x, 1 - y), (1 - x, 1 - y)]
        mine = 2 * x + y
        local, remote = [], []
        for w in range(n):
            cp = pltpu.make_async_copy(ins[w], outs[w].at[mine], local_sems.at[w])
            cp.start()
            local.append(cp)
            for k, (px, py) in enumerate(chips):
                cp = pltpu.make_async_remote_copy(
                    src_ref=ins[w], dst_ref=outs[w].at[mine], send_sem=send_sems.at[3 * w + k],
                    recv_sem=recv_sems.at[3 * w + k], device_id=(px, py, c), device_id_type=MESH)
                cp.start()
                remote.append(cp)
        for cp in remote:
            cp.wait_recv()
        for cp in remote:
            cp.wait_send()
        for cp in local:
            cp.wait()

    hbm = pl.BlockSpec(memory_space=pltpu.HBM)
    return _pcall(
        body, name="gather_xy",
        out_shape=[jax.ShapeDtypeStruct((4,) + s.shape, s.dtype) for s in shards],
        in_specs=[hbm] * n, out_specs=[hbm] * n,
        scratch_shapes=[pltpu.SemaphoreType.DMA((3 * n,)), pltpu.SemaphoreType.DMA((3 * n,)),
                        pltpu.SemaphoreType.DMA((n,))],
        compiler_params=pltpu.CompilerParams(vmem_limit_bytes=VMEM_LIMIT),
    )(*shards)


def _scatter8(pieces):
    n = len(pieces)

    def body(*refs):
        ins, outs = refs[:n], refs[n:2 * n]
        send_sems, recv_sems, local_sems = refs[2 * n:]
        x, y, c = lax.axis_index("x"), lax.axis_index("y"), lax.axis_index("c")
        me = 4 * x + 2 * y + c
        local, remote = [], []
        for w in range(n):
            cp = pltpu.make_async_copy(ins[w].at[me], outs[w].at[me], local_sems.at[w])
            cp.start()
            local.append(cp)
            for f in range(1, 8):
                px = 1 - x if f & 4 else x
                py = 1 - y if f & 2 else y
                pc = 1 - c if f & 1 else c
                cp = pltpu.make_async_remote_copy(
                    src_ref=ins[w].at[4 * px + 2 * py + pc], dst_ref=outs[w].at[me],
                    send_sem=send_sems.at[7 * w + f - 1], recv_sem=recv_sems.at[7 * w + f - 1],
                    device_id=(px, py, pc), device_id_type=MESH)
                cp.start()
                remote.append(cp)
        for cp in remote:
            cp.wait_recv()
        for cp in remote:
            cp.wait_send()
        for cp in local:
            cp.wait()

    hbm = pl.BlockSpec(memory_space=pltpu.HBM)
    return _pcall(
        body, name="scatter8",
        out_shape=[jax.ShapeDtypeStruct(p.shape, p.dtype) for p in pieces],
        in_specs=[hbm] * n, out_specs=[hbm] * n,
        scratch_shapes=[pltpu.SemaphoreType.DMA((7 * n,)), pltpu.SemaphoreType.DMA((7 * n,)),
                        pltpu.SemaphoreType.DMA((n,))],
        compiler_params=pltpu.CompilerParams(vmem_limit_bytes=VMEM_LIMIT),
    )(*pieces)


def _swap_halves(halves):
    n = len(halves)
    chunks = 8
    n_chunks = [max(k for k in (chunks, 4, 2, 1) if h.shape[0] % (8 * k) == 0) for h in halves]

    def body(*refs):
        ins, outs = refs[:n], refs[n:2 * n]
        send_sems, recv_sems, local_sems = refs[2 * n:]
        x, y, c = lax.axis_index("x"), lax.axis_index("y"), lax.axis_index("c")
        local, remote = [], []
        for w in range(n):
            cp = pltpu.make_async_copy(ins[w], outs[w].at[c], local_sems.at[w])
            cp.start()
            local.append(cp)
            rows = ins[w].shape[0] // n_chunks[w]
            for k in range(n_chunks[w]):
                cp = pltpu.make_async_remote_copy(
                    src_ref=ins[w].at[pl.ds(k * rows, rows)], dst_ref=outs[w].at[c, pl.ds(k * rows, rows)],
                    send_sem=send_sems.at[chunks * w + k], recv_sem=recv_sems.at[chunks * w + k],
                    device_id=(x, y, 1 - c), device_id_type=MESH)
                cp.start()
                remote.append(cp)
        for cp in remote:
            cp.wait_recv()
        for cp in remote:
            cp.wait_send()
        for cp in local:
            cp.wait()

    hbm = pl.BlockSpec(memory_space=pltpu.HBM)
    return _pcall(
        body, name="swap_halves",
        out_shape=[jax.ShapeDtypeStruct((2,) + h.shape, h.dtype) for h in halves],
        in_specs=[hbm] * n, out_specs=[hbm] * n,
        scratch_shapes=[pltpu.SemaphoreType.DMA((chunks * n,)), pltpu.SemaphoreType.DMA((chunks * n,)),
                        pltpu.SemaphoreType.DMA((n,))],
        compiler_params=pltpu.CompilerParams(vmem_limit_bytes=VMEM_LIMIT),
    )(*halves)


def _sum_leading(a, name):
    n, r, c = a.shape
    tr = _rows_tile(r, n * c * 4, budget=6 * 1024 * 1024)
    if a.dtype == BF16 and tr % 16:
        tr = r

    def body(a_ref, o_ref):
        acc = a_ref[0].astype(F32)
        for k in range(1, n):
            acc = acc + a_ref[k].astype(F32)
        o_ref[...] = acc

    return _pcall(
        body, name=name, grid=(r // tr,),
        out_shape=jax.ShapeDtypeStruct((r, c), F32),
        in_specs=[pl.BlockSpec((n, tr, c), lambda i: (0, i, 0))],
        out_specs=pl.BlockSpec((tr, c), lambda i: (i, 0)),
        compiler_params=_params(("arbitrary",)),
    )(a)


def _to_bf16(a, name):
    n, r, c = a.shape

    def body(a_ref, o_ref):
        o_ref[...] = a_ref[...].astype(BF16)

    spec = pl.BlockSpec((1, r, c), lambda i: (i, 0, 0))
    return _pcall(
        body, name=name, grid=(n,), out_shape=jax.ShapeDtypeStruct(a.shape, BF16),
        in_specs=[spec], out_specs=spec, compiler_params=_params(("arbitrary",)),
    )(a)


def _adamw(w, g, m, v, name):
    r, c = w.shape
    tr = _rows_tile(r, c * 4, budget=1024 * 1024)
    c1 = 1.0 - ADAM_B1 ** ADAM_STEP
    c2 = 1.0 - ADAM_B2 ** ADAM_STEP

    def body(w_ref, g_ref, m_ref, v_ref, d_ref, nm_ref, nv_ref):
        gg = g_ref[...]
        nm = ADAM_B1 * m_ref[...] + (1.0 - ADAM_B1) * gg
        nv = ADAM_B2 * v_ref[...] + (1.0 - ADAM_B2) * (gg * gg)
        m_hat = nm / c1
        v_hat = nv / c2
        d_ref[...] = -ADAM_LR * (m_hat / (jnp.sqrt(v_hat) + ADAM_EPS) + ADAM_WD * w_ref[...])
        nm_ref[...] = nm
        nv_ref[...] = nv

    spec = pl.BlockSpec((tr, c), lambda i: (i, 0))
    return _pcall(
        body, name=name, grid=(r // tr,),
        out_shape=[jax.ShapeDtypeStruct((r, c), F32)] * 3,
        in_specs=[spec] * 4, out_specs=[spec] * 3,
        compiler_params=_params(("arbitrary",)),
    )(w, g, m, v)


def _ada_fwd(c_all, w_shard, b_shard):
    nb, d = c_all.shape
    cols = w_shard.shape[1]

    def body(c_ref, w_ref, b_ref, sc_ref, mod_ref):
        cv = c_ref[...]
        sc = cv * jax.nn.sigmoid(cv)
        sc_ref[...] = sc
        mod_ref[...] = jnp.dot(sc.astype(BF16), w_ref[...].astype(BF16), preferred_element_type=F32) + b_ref[...]

    return _pcall(
        body, name="ada_fwd",
        out_shape=[jax.ShapeDtypeStruct((nb, d), F32), jax.ShapeDtypeStruct((nb, cols), F32)],
        compiler_params=pltpu.CompilerParams(vmem_limit_bytes=VMEM_LIMIT),
    )(c_all, w_shard, b_shard)


def _ada_bwd(sc_t, dmod_cols):
    d, nb = sc_t.shape
    cols = dmod_cols.shape[1]
    tr = _rows_tile(d, cols * 4, budget=1024 * 1024)

    def body(s_ref, m_ref, o_ref):
        s = s_ref[...]
        m = m_ref[...]
        acc = s[:, 0:1] * m[0:1, :]
        for b in range(1, nb):
            acc = acc + s[:, b:b + 1] * m[b:b + 1, :]
        o_ref[...] = acc

    return _pcall(
        body, name="ada_bwd", grid=(d // tr,),
        out_shape=jax.ShapeDtypeStruct((d, cols), F32),
        in_specs=[pl.BlockSpec((tr, nb), lambda i: (i, 0)), pl.BlockSpec((nb, cols), lambda i: (0, 0))],
        out_specs=pl.BlockSpec((tr, cols), lambda i: (i, 0)),
        compiler_params=_params(("arbitrary",)),
    )(sc_t, dmod_cols)


def _log_sigmoid(x):
    return jnp.minimum(x, 0.0) - jnp.log1p(jnp.exp(-jnp.abs(x)))


def _fgate_fwd(fl2d, b_rows, tri_in, tri_blk):
    r = fl2d.shape[0]

    def body(x_ref, b_ref, u_ref, l_ref, f_ref):
        lf = _log_sigmoid(x_ref[...] + b_ref[...])
        c1 = _split_dot(lf, u_ref[...], 3)
        tot = jnp.broadcast_to(c1[:, LANES - 1:LANES], (r, LANES))
        acc = None
        for _ in range(3):
            part = tot.astype(BF16)
            dd = jnp.dot(l_ref[...], part, preferred_element_type=F32)
            acc = dd if acc is None else acc + dd
            tot = tot - part.astype(F32)
        f_ref[...] = c1 + acc

    return _pcall(
        body, name="fgate_fwd", out_shape=jax.ShapeDtypeStruct((r, LANES), F32),
        compiler_params=pltpu.CompilerParams(vmem_limit_bytes=VMEM_LIMIT),
    )(fl2d, b_rows, tri_in, tri_blk)


def _fgate_bwd(fl2d, b_rows, df_query, df_key, tri_in_rev, tri_blk_rev, head_rows):
    r = fl2d.shape[0]
    nhp = head_rows.shape[0]

    def body(x_ref, b_ref, dq_ref, dk_ref, u_ref, l_ref, hr_ref, o_ref, gb_ref):
        c1 = _split_dot(dq_ref[...] + dk_ref[...], u_ref[...], 3)
        tot = jnp.broadcast_to(c1[:, 0:1], (r, LANES))
        acc = None
        for _ in range(3):
            part = tot.astype(BF16)
            dd = jnp.dot(l_ref[...], part, preferred_element_type=F32)
            acc = dd if acc is None else acc + dd
            tot = tot - part.astype(F32)
        x = x_ref[...] + b_ref[...]
        e = jnp.exp(-jnp.abs(x))
        dfl = (c1 + acc) * (jnp.where(x >= 0, e, 1.0) / (1.0 + e))
        o_ref[...] = dfl
        rs = jnp.broadcast_to(jnp.sum(dfl, axis=1, keepdims=True), (r, LANES))
        gb = None
        for _ in range(3):
            part = rs.astype(BF16)
            dd = jnp.dot(hr_ref[...], part, preferred_element_type=F32)
            gb = dd if gb is None else gb + dd
            rs = rs - part.astype(F32)
        gb_ref[...] = gb

    return _pcall(
        body, name="fgate_bwd",
        out_shape=[jax.ShapeDtypeStruct((r, LANES), F32), jax.ShapeDtypeStruct((nhp, LANES), F32)],
        compiler_params=pltpu.CompilerParams(vmem_limit_bytes=VMEM_LIMIT),
    )(fl2d, b_rows, df_query, df_key, tri_in_rev, tri_blk_rev, head_rows)


def _norm_mod(x, g, scale, shift):
    r = lax.rsqrt(jnp.mean(x * x, axis=-1, keepdims=True) + EPS)
    return (x * r * g) * (1.0 + scale) + shift


def _norm_mod_bwd(x, dh, g, scale):
    r = lax.rsqrt(jnp.mean(x * x, axis=-1, keepdims=True) + EPS)
    xn = x * r
    dshift = jnp.sum(dh, axis=0, keepdims=True)
    dscale = jnp.sum(dh * (xn * g), axis=0, keepdims=True)
    dxn_g = dh * (1.0 + scale)
    dg = jnp.sum(dxn_g * xn, axis=0, keepdims=True)
    dxn = dxn_g * g
    dx = r * (dxn - xn * jnp.mean(dxn * xn, axis=-1, keepdims=True))
    return dx, dshift, dscale, dg


def _in_proj_fwd(x, mod8, g_attn, w_qkv, w_f, tm):
    t, d = x.shape
    dg = w_qkv.shape[1] // 6

    def body(x_ref, mod_ref, g_ref, w_ref, wf_ref, qkv_ref, fl_ref, h1_ref):
        h = _norm_mod(x_ref[...], g_ref[...], mod_ref[1:2, :], mod_ref[0:1, :]).astype(BF16)
        h1_ref[...] = _transposed(h)
        fl_ref[...] = lax.dot_general(wf_ref[...], h, NT_DIMS, preferred_element_type=F32)
        for k in range(6):
            cols = slice(k * dg, (k + 1) * dg)
            y = jnp.dot(h, w_ref[:, cols], preferred_element_type=F32)
            qkv_ref[:, cols] = (y * HEAD_DIM ** -0.5 if k in (0, 3) else y).astype(BF16)

    once = lambda a: pl.BlockSpec(a.shape, lambda i: (0,) * a.ndim, pipeline_mode=pl.Buffered(1))
    return _pcall(
        body, name="in_proj_fwd", grid=(t // tm,),
        out_shape=[jax.ShapeDtypeStruct((t, 6 * dg), BF16), jax.ShapeDtypeStruct((LANES, t), F32),
                   jax.ShapeDtypeStruct((d, t), BF16)],
        in_specs=[pl.BlockSpec((tm, d), lambda i: (i, 0)), pl.BlockSpec((8, d), lambda i: (0, 0)),
                  pl.BlockSpec((1, d), lambda i: (0, 0)), once(w_qkv), once(w_f)],
        out_specs=[pl.BlockSpec((tm, 6 * dg), lambda i: (i, 0)), pl.BlockSpec((LANES, tm), lambda i: (0, i)),
                   pl.BlockSpec((d, tm), lambda i: (0, i))],
        compiler_params=_params(("arbitrary",)),
    )(x, mod8, g_attn, w_qkv, w_f)


def _head_rstd(o, bd):
    return lax.rsqrt(_split_dot(o * o, bd, 3) * (1.0 / HEAD_DIM) + EPS)


def _attn_out_fwd(x, o_fox, o_sb, g_fox, g_sb, w_out, mod8, bd, tm):
    t, d = x.shape
    dg = o_fox.shape[0]

    def body(x_ref, of_ref, os_ref, gf_ref, gs_ref, w_ref, mod_ref, bd_ref, x2_ref, mix_ref, mixt_ref):
        of, osb = of_ref[...].T, os_ref[...].T
        mf = (of * _head_rstd(of, bd_ref[...]) * gf_ref[...]).astype(BF16)
        ms = (osb * _head_rstd(osb, bd_ref[...]) * gs_ref[...]).astype(BF16)
        mix_ref[:, :dg] = mf
        mix_ref[:, dg:] = ms
        mixt_ref[:dg, :] = _transposed(mf)
        mixt_ref[dg:, :] = _transposed(ms)
        y = jnp.dot(mf, w_ref[:dg, :], preferred_element_type=F32) + jnp.dot(ms, w_ref[dg:, :], preferred_element_type=F32)
        x2_ref[...] = x_ref[...] + mod_ref[2:3, :] * y

    row = lambda w: pl.BlockSpec((tm, w), lambda i: (i, 0))
    full = lambda a: pl.BlockSpec(a.shape, lambda i: (0,) * a.ndim)
    return _pcall(
        body, name="attn_out_fwd", grid=(t // tm,),
        out_shape=[jax.ShapeDtypeStruct((t, d), F32), jax.ShapeDtypeStruct((t, 2 * dg), BF16),
                   jax.ShapeDtypeStruct((2 * dg, t), BF16)],
        in_specs=[row(d), pl.BlockSpec((dg, tm), lambda i: (0, i)), pl.BlockSpec((dg, tm), lambda i: (0, i)),
                  full(g_fox), full(g_sb), full(w_out), full(mod8), full(bd)],
        out_specs=[row(d), row(2 * dg), pl.BlockSpec((2 * dg, tm), lambda i: (0, i))],
        compiler_params=_params(("arbitrary",)),
    )(x, o_fox, o_sb, g_fox, g_sb, w_out, mod8, bd)


def _attn_out_bwd(dx2, mix, o_fox, o_sb, g_fox, g_sb, w_out, mod8, bd, hsel, tm):
    t, d = dx2.shape
    dg = o_fox.shape[0]

    def body(dx_ref, mix_ref, of_ref, os_ref, gf_ref, gs_ref, w_ref, mod_ref, bd_ref, hs_ref,
             dof_ref, dos_ref, dlt_ref, dxg_ref, part_ref):
        dx = dx_ref[...]
        gate = mod_ref[2:3, :]
        dxg = (dx * gate).astype(BF16)
        dxg_ref[...] = dxg
        mixv = mix_ref[...]
        y = jnp.dot(mixv[:, :dg], w_ref[:dg, :], preferred_element_type=F32)
        y = y + jnp.dot(mixv[:, dg:], w_ref[dg:, :], preferred_element_type=F32)
        part_ref[0] = jnp.zeros((8, d), F32)
        part_ref[0, 0:1, :] = jnp.sum(dx * y, axis=0, keepdims=True)
        for grp, (o_ref, g_ref, do_ref) in enumerate(((of_ref, gf_ref, dof_ref), (os_ref, gs_ref, dos_ref))):
            dmix = lax.dot_general(dxg, w_ref[grp * dg:(grp + 1) * dg, :], NT_DIMS, preferred_element_type=F32)
            o = o_ref[...].T
            r = _head_rstd(o, bd_ref[...])
            n = o * r
            part_ref[0, 1:2, grp * dg:(grp + 1) * dg] = jnp.sum(dmix * n, axis=0, keepdims=True)
            dn = dmix * g_ref[...]
            mh = _split_dot(dn * n, bd_ref[...], 3) * (1.0 / HEAD_DIM)
            do = r * (dn - n * mh)
            do_ref[...] = do.astype(BF16)
            if grp == 0:
                prod, dlt = do * o, None
                for _ in range(3):
                    part = prod.astype(BF16)
                    term = lax.dot_general(hs_ref[...], part, NT_DIMS, preferred_element_type=F32)
                    dlt = term if dlt is None else dlt + term
                    prod = prod - part.astype(F32)
                dlt_ref[...] = dlt

    row = lambda w: pl.BlockSpec((tm, w), lambda i: (i, 0))
    full = lambda a: pl.BlockSpec(a.shape, lambda i: (0,) * a.ndim)
    nt = t // tm
    return _pcall(
        body, name="attn_out_bwd", grid=(nt,),
        out_shape=[jax.ShapeDtypeStruct((t, dg), BF16), jax.ShapeDtypeStruct((t, dg), BF16),
                   jax.ShapeDtypeStruct((LANES, t), F32), jax.ShapeDtypeStruct((t, d), BF16),
                   jax.ShapeDtypeStruct((nt, 8, d), F32)],
        in_specs=[row(d), row(2 * dg), pl.BlockSpec((dg, tm), lambda i: (0, i)), pl.BlockSpec((dg, tm), lambda i: (0, i)),
                  full(g_fox), full(g_sb), full(w_out), full(mod8),
                  full(bd), full(hsel)],
        out_specs=[row(dg), row(dg), pl.BlockSpec((LANES, tm), lambda i: (0, i)), row(d), pl.BlockSpec((1, 8, d), lambda i: (i, 0, 0))],
        compiler_params=_params(("arbitrary",)),
    )(dx2, mix, o_fox, o_sb, g_fox, g_sb, w_out, mod8, bd, hsel)


def _in_proj_bwd(dparts, dfl, w_qkv, w_f, x, dx2, mod8, g_attn, tm):
    t, d = x.shape
    dg = dparts[1].shape[1]

    def body(*refs):
        d_refs = refs[:6]
        dfl_ref, w_ref, wf_ref, x_ref, dx2_ref, mod_ref, g_ref, gx_ref, dp_ref, dflb_ref, part_ref = refs[6:]
        dh = None
        for k in range(6):
            dk = d_refs[k][...].T if k in (0, 3) else d_refs[k][...]
            if k in (0, 3):
                dk = dk * HEAD_DIM ** -0.5
            db = dk.astype(BF16)
            dp_ref[:, k * dg:(k + 1) * dg] = db
            term = lax.dot_general(db, w_ref[:, k * dg:(k + 1) * dg], NT_DIMS, preferred_element_type=F32)
            dh = term if dh is None else dh + term
        dfb = dfl_ref[...].T.astype(BF16)
        dflb_ref[...] = dfb
        dh = dh + jnp.dot(dfb, wf_ref[...], preferred_element_type=F32)
        dx, dshift, dscale, dgn = _norm_mod_bwd(x_ref[...], dh, g_ref[...], mod_ref[1:2, :])
        gx_ref[...] = dx2_ref[...] + dx
        part_ref[0] = jnp.zeros((8, d), F32)
        part_ref[0, 0:1, :] = dshift
        part_ref[0, 1:2, :] = dscale
        part_ref[0, 2:3, :] = dgn

    row = lambda w: pl.BlockSpec((tm, w), lambda i: (i, 0))
    full = lambda a: pl.BlockSpec(a.shape, lambda i: (0,) * a.ndim)
    nt = t // tm
    return _pcall(
        body, name="in_proj_bwd", grid=(nt,),
        out_shape=[jax.ShapeDtypeStruct((t, d), F32), jax.ShapeDtypeStruct((t, 6 * dg), BF16),
                   jax.ShapeDtypeStruct((t, LANES), BF16), jax.ShapeDtypeStruct((nt, 8, d), F32)],
        in_specs=[pl.BlockSpec((dg, tm), lambda i: (0, i)), row(dg), row(dg)] * 2
        + [pl.BlockSpec((LANES, tm), lambda i: (0, i)), full(w_qkv), full(w_f), row(d), row(d), full(mod8), full(g_attn)],
        out_specs=[row(d), row(6 * dg), row(LANES), pl.BlockSpec((1, 8, d), lambda i: (i, 0, 0))],
        compiler_params=_params(("arbitrary",)),
    )(*dparts, dfl, w_qkv, w_f, x, dx2, mod8, g_attn)


def _matmul_tn(a_t, b, name):
    m, t = a_t.shape
    n = b.shape[1]
    tm_ = _tile(m, (1408, 1024, 512, 256, 128))
    tn_ = _tile(n, (1408, 1024, 512, 256, 128))
    tk = _tile(t, (1024, 512, 256, 128))
    nk = t // tk

    def body(a_ref, b_ref, o_ref):
        k = pl.program_id(2)

        @pl.when(k == 0)
        def _():
            o_ref[...] = jnp.zeros_like(o_ref)

        o_ref[...] += jnp.dot(a_ref[...], b_ref[...], preferred_element_type=F32)

    return _pcall(
        body, name=name, grid=(m // tm_, n // tn_, nk),
        out_shape=jax.ShapeDtypeStruct((m, n), F32),
        in_specs=[pl.BlockSpec((tm_, tk), lambda i, j, k: (i, k)), pl.BlockSpec((tk, tn_), lambda i, j, k: (k, j))],
        out_specs=pl.BlockSpec((tm_, tn_), lambda i, j, k: (i, j)),
        compiler_params=_params(("arbitrary", "arbitrary", "arbitrary")),
    )(a_t, b)


HALO = 16


def _conv_taps(up_ext, cw, lo, rows):
    s1 = pltpu.roll(up_ext, 1, 0)
    s2 = pltpu.roll(up_ext, 2, 0)
    u = cw[2:3, :] * up_ext[lo:lo + rows] + cw[1:2, :] * s1[lo:lo + rows] + cw[0:1, :] * s2[lo:lo + rows] + cw[3:4, :]
    return u, s1, s2


def _chunk_major(w, cf):
    d, n = w.shape[0], w.shape[1] // cf
    return jnp.transpose(w.reshape(d, n, cf), (1, 0, 2))


def _ffn_fwd(x2, target, mod8, g_mlp, g_final, wg, wv, cwg, cwv, wd, tm, cf):
    t, d = x2.shape
    dfp = wg.shape[1]
    nt, nc = t // tm, dfp // cf
    hb = tm // HALO
    wg_c, wv_c = _chunk_major(wg, cf), _chunk_major(wv, cf)

    def body(x_ref, xp_ref, tg_ref, mod_ref, g_ref, gf_ref, wg_ref, wv_ref, cg_ref, cv_ref, wd_ref,
             dx3_ref, h2_ref, part_ref, act_sc):
        i = pl.program_id(0)
        xe = jnp.concatenate([xp_ref[...], x_ref[...]], axis=0)
        h = _norm_mod(xe, g_ref[...], mod_ref[4:5, :], mod_ref[3:4, :]).astype(BF16)
        h2_ref[...] = _transposed(h[HALO:])
        first = jnp.where(i > 0, h[:HALO], jnp.zeros_like(h[:HALO]))
        h = jnp.concatenate([first, h[HALO:]], axis=0)

        def up(c):
            return (jnp.dot(h, wg_ref[c], preferred_element_type=F32), jnp.dot(h, wv_ref[c], preferred_element_type=F32))

        def activation(c, ups):
            cols = slice(c * cf, (c + 1) * cf)
            ug, _, _ = _conv_taps(ups[0], cg_ref[:, cols], HALO, tm)
            uv, _, _ = _conv_taps(ups[1], cv_ref[:, cols], HALO, tm)
            act_sc[:, cols] = (ug * jax.nn.sigmoid(ug) * uv).astype(BF16)

        for c0 in range(0, nc, 2):
            group = list(range(c0, min(c0 + 2, nc)))
            ups = [up(c) for c in group]
            for c, u in zip(group, ups):
                activation(c, u)

        y_ffn = jnp.dot(act_sc[...], wd_ref[...], preferred_element_type=F32)
        x3 = x_ref[...] + mod_ref[5:6, :] * y_ffn
        r3 = lax.rsqrt(jnp.mean(x3 * x3, axis=-1, keepdims=True) + EPS)
        xn = x3 * r3
        gf = gf_ref[...]
        diff = xn * gf - tg_ref[...]
        dy = diff * (1.0 / d)
        dxn = dy * gf
        dx3 = r3 * (dxn - xn * jnp.mean(dxn * xn, axis=-1, keepdims=True))
        dx3_ref[...] = dx3
        part_ref[0] = jnp.zeros((8, d), F32)
        part_ref[0, 0:1, :] = jnp.sum(dy * xn, axis=0, keepdims=True)
        part_ref[0, 1:2, :] = jnp.sum(dx3 * y_ffn, axis=0, keepdims=True)
        part_ref[0, 2:3, :] = jnp.sum(diff * diff, axis=0, keepdims=True) * (0.5 / d)

    row = lambda w: pl.BlockSpec((tm, w), lambda i: (i, 0))
    full = lambda a: pl.BlockSpec(a.shape, lambda i: (0,) * a.ndim)
    once = lambda a: pl.BlockSpec(a.shape, lambda i: (0,) * a.ndim, pipeline_mode=pl.Buffered(1))
    return _pcall(
        body, name="ffn_fwd", grid=(nt,),
        out_shape=[jax.ShapeDtypeStruct((t, d), F32), jax.ShapeDtypeStruct((d, t), BF16),
                   jax.ShapeDtypeStruct((nt, 8, d), F32)],
        in_specs=[row(d), pl.BlockSpec((HALO, d), lambda i: (jnp.maximum(i * hb - 1, 0), 0)), row(d),
                  full(mod8), full(g_mlp), full(g_final), once(wg_c), once(wv_c), once(cwg), once(cwv), once(wd)],
        out_specs=[row(d), pl.BlockSpec((d, tm), lambda i: (0, i)), pl.BlockSpec((1, 8, d), lambda i: (i, 0, 0))],
        scratch_shapes=[pltpu.VMEM((tm, dfp), BF16)],
        compiler_params=_params(("arbitrary",)),
    )(x2, x2, target, mod8, g_mlp, g_final, wg_c, wv_c, cwg, cwv, wd)


def _ffn_bwd(x2, dx3, mod8, g_mlp, wg, wv, cwg, cwv, wd, tm, cf):
    t, d = x2.shape
    dfp = wg.shape[1]
    nt, nc = t // tm, dfp // cf
    hb = tm // HALO
    nhb = t // HALO
    n = tm + HALO

    def body(x_ref, xp_ref, xn_ref, dx_ref, dxn_ref, mod_ref, g_ref, wg_ref, wv_ref, cg_ref, cv_ref, wd_ref,
             dx2_ref, dug_ref, duv_ref, act_ref, dxg_ref, part_ref, pcg_ref, pcv_ref):
        i = pl.program_id(0)
        xe = jnp.concatenate([xp_ref[...], x_ref[...], xn_ref[...]], axis=0)
        h = _norm_mod(xe, g_ref[...], mod_ref[4:5, :], mod_ref[3:4, :]).astype(BF16)
        h = jnp.concatenate([jnp.where(i > 0, h[:HALO], jnp.zeros_like(h[:HALO])), h[HALO:]], axis=0)
        dx = dx_ref[...] * mod_ref[5:6, :]
        dxn = jnp.where(i < nt - 1, dxn_ref[...] * mod_ref[5:6, :], 0.0)
        de = jnp.concatenate([dx, dxn], axis=0).astype(BF16)
        dxg_ref[...] = de[:tm]
        pcg_ref[0] = jnp.zeros((8, dfp), F32)
        pcv_ref[0] = jnp.zeros((8, dfp), F32)

        def products(c):
            cols = slice(c * cf, (c + 1) * cf)
            return (jnp.dot(h, wg_ref[:, cols], preferred_element_type=F32), jnp.dot(h, wv_ref[:, cols], preferred_element_type=F32),
                    lax.dot_general(de, wd_ref[cols, :], NT_DIMS, preferred_element_type=F32))

        def back(du, cw, up, s1, s2, pc_ref, cols):
            dup = (cw[2:3, :] * du + cw[1:2, :] * pltpu.roll(du, n - 1, 0) + cw[0:1, :] * pltpu.roll(du, n - 2, 0))[:tm]
            dut = du[:tm]
            pc_ref[0, 0:1, cols] = jnp.sum(dut * s2[HALO:HALO + tm], axis=0, keepdims=True)
            pc_ref[0, 1:2, cols] = jnp.sum(dut * s1[HALO:HALO + tm], axis=0, keepdims=True)
            pc_ref[0, 2:3, cols] = jnp.sum(dut * up[HALO:HALO + tm], axis=0, keepdims=True)
            pc_ref[0, 3:4, cols] = jnp.sum(dut, axis=0, keepdims=True)
            return dup.astype(BF16)

        def chunk(c, prods):
            cols = slice(c * cf, (c + 1) * cf)
            upg, upv, dact = prods
            cg, cv = cg_ref[:, cols], cv_ref[:, cols]
            ug, g1, g2 = _conv_taps(upg, cg, HALO, n)
            uv, v1, v2 = _conv_taps(upv, cv, HALO, n)
            sg = jax.nn.sigmoid(ug)
            sil = ug * sg
            act_ref[cols, :] = _transposed((sil * uv)[:tm].astype(BF16))
            dug_ref[:, cols] = back(dact * uv * (sg * (1.0 + ug * (1.0 - sg))), cg, upg, g1, g2, pcg_ref, cols)
            duv_ref[:, cols] = back(dact * sil, cv, upv, v1, v2, pcv_ref, cols)

        for c0 in range(0, nc, 2):
            group = list(range(c0, min(c0 + 2, nc)))
            prods = [products(c) for c in group]
            for c, pr in zip(group, prods):
                chunk(c, pr)

        dh = (lax.dot_general(dug_ref[...], wg_ref[...], NT_DIMS, preferred_element_type=F32)
              + lax.dot_general(duv_ref[...], wv_ref[...], NT_DIMS, preferred_element_type=F32))
        dxt, dshift, dscale, dgn = _norm_mod_bwd(x_ref[...], dh, g_ref[...], mod_ref[4:5, :])
        dx2_ref[...] = dx_ref[...] + dxt
        part_ref[0] = jnp.zeros((8, d), F32)
        part_ref[0, 0:1, :] = dshift
        part_ref[0, 1:2, :] = dscale
        part_ref[0, 2:3, :] = dgn

    row = lambda w: pl.BlockSpec((tm, w), lambda i: (i, 0))
    prev = pl.BlockSpec((HALO, d), lambda i: (jnp.maximum(i * hb - 1, 0), 0))
    nxt = pl.BlockSpec((HALO, d), lambda i: (jnp.minimum((i + 1) * hb, nhb - 1), 0))
    full = lambda a: pl.BlockSpec(a.shape, lambda i: (0,) * a.ndim)
    once = lambda a: pl.BlockSpec(a.shape, lambda i: (0,) * a.ndim, pipeline_mode=pl.Buffered(1))
    part = lambda w: pl.BlockSpec((1, 8, w), lambda i: (i, 0, 0))
    return _pcall(
        body, name="ffn_bwd", grid=(nt,),
        out_shape=[jax.ShapeDtypeStruct((t, d), F32), jax.ShapeDtypeStruct((t, dfp), BF16),
                   jax.ShapeDtypeStruct((t, dfp), BF16), jax.ShapeDtypeStruct((dfp, t), BF16),
                   jax.ShapeDtypeStruct((t, d), BF16), jax.ShapeDtypeStruct((nt, 8, d), F32),
                   jax.ShapeDtypeStruct((nt, 8, dfp), F32), jax.ShapeDtypeStruct((nt, 8, dfp), F32)],
        in_specs=[row(d), prev, nxt, row(d), nxt, full(mod8), full(g_mlp), once(wg), once(wv), once(cwg), once(cwv), once(wd)],
        out_specs=[row(d), row(dfp), row(dfp), pl.BlockSpec((dfp, tm), lambda i: (0, i)), row(d), part(d), part(dfp), part(dfp)],
        compiler_params=_params(("arbitrary",)),
    )(x2, x2, x2, dx3, dx3, mod8, g_mlp, wg, wv, cwg, cwv, wd)


BLK = 2 * LANES
XROWS = 144
LANE_FS, LANE_FT_A, LANE_FT_B = 0, 3, 6


def _head_masks():
    lane = lax.broadcasted_iota(jnp.int32, (1, LANES), 1)
    in_a = lane < HEAD_DIM
    return in_a, jnp.logical_not(in_a)


def _pieces3(x):
    hi = x.astype(BF16).astype(F32)
    r = x - hi
    mid = r.astype(BF16).astype(F32)
    return hi, mid, (r - mid).astype(BF16).astype(F32)


def _bias_lanes(rows, entries):
    sub = lax.broadcasted_iota(jnp.int32, (16, 1), 0)
    out = jnp.zeros((16, rows), F32)
    for l, v in entries:
        out = jnp.where(sub == l, v, out)
    return jnp.concatenate([out, jnp.zeros((LANES - 16, rows), F32)], axis=0).T


def _three(first, values):
    return [(first + k, v) for k, v in enumerate(values)]


def _stack_rows(x, in_a, in_b):
    zero = jnp.zeros_like(x)
    return jnp.concatenate([jnp.where(in_a, x, zero), jnp.where(in_b, x, zero)], axis=0)


def _transposed(x):
    return x.astype(F32).T.astype(BF16)


def _attn_operands(qkv, frow, dg):
    t = qkv.shape[0]
    p, nk = dg // LANES, t // BLK

    def body(qf_ref, kf_ref, vf_ref, ks_ref, vs_ref, f_ref, qx_ref, kx_ref, kxt_ref, vf_o, vft_o, ks_o, kst_o, vs_o, vst_o):
        in_a, in_b = _head_masks()
        fa, fb = _pieces3(f_ref[0]), _pieces3(f_ref[1])
        qx_ref[0, :, :LANES] = qf_ref[...]
        qx_ref[0, :, LANES:] = _bias_lanes(
            BLK, _three(LANE_FS, (-1.0,) * 3) + _three(LANE_FT_A, fa) + _three(LANE_FT_B, fb)).astype(BF16)
        kf = kf_ref[...]
        zero = jnp.zeros_like(kf)
        top = jnp.concatenate([jnp.where(in_a, kf, zero), _bias_lanes(
            BLK, _three(LANE_FS, fa) + _three(LANE_FT_A, (1.0,) * 3)).astype(BF16)], axis=1)
        bot = jnp.concatenate([jnp.where(in_b, kf, zero), _bias_lanes(
            BLK, _three(LANE_FS, fb) + _three(LANE_FT_B, (1.0,) * 3)).astype(BF16)], axis=1)
        kx = jnp.concatenate([top, bot], axis=0)
        kx_ref[0, 0] = kx
        kxt_ref[0, 0] = _transposed(kx)[:XROWS]
        for src, dst, dst_t in ((vf_ref, vf_o, vft_o), (ks_ref, ks_o, kst_o), (vs_ref, vs_o, vst_o)):
            st = _stack_rows(src[...], in_a, in_b)
            dst[0, 0] = st
            dst_t[0, 0] = _transposed(st)

    col = lambda base: pl.BlockSpec((BLK, LANES), lambda h, j: (j, base * p + h))
    blk4 = lambda r, c: pl.BlockSpec((1, 1, r, c), lambda h, j: (h, j, 0, 0))
    shp4 = lambda r, c: jax.ShapeDtypeStruct((p, nk, r, c), BF16)
    return _pcall(
        body, name="attn_operands", grid=(p, nk),
        out_shape=[jax.ShapeDtypeStruct((p, t, 2 * LANES), BF16), shp4(2 * BLK, 2 * LANES), shp4(XROWS, 2 * BLK)]
        + [shp4(2 * BLK, LANES), shp4(LANES, 2 * BLK)] * 3,
        in_specs=[col(0), col(1), col(2), col(4), col(5), pl.BlockSpec((2, 1, BLK), lambda h, j: (h, 0, j))],
        out_specs=[pl.BlockSpec((1, BLK, 2 * LANES), lambda h, j: (h, j, 0)), blk4(2 * BLK, 2 * LANES), blk4(XROWS, 2 * BLK)]
        + [blk4(2 * BLK, LANES), blk4(LANES, 2 * BLK)] * 3,
        compiler_params=_params(("arbitrary", "arbitrary")),
    )(qkv, qkv, qkv, qkv, qkv, frow)


def _key_query_masks():
    key = lax.broadcasted_iota(jnp.int32, (BLK, BLK), 0)
    qry = lax.broadcasted_iota(jnp.int32, (BLK, BLK), 1)
    return key <= qry, key < qry


def _key_triangle(kind):
    s = lax.broadcasted_iota(jnp.int32, (BLK, BLK), 0)
    j = lax.broadcasted_iota(jnp.int32, (BLK, BLK), 1)
    return {"suffix": j >= s, "prefix": j <= s, "before": j < s}[kind].astype(BF16)


def _tri_dot(tri, x, passes):
    acc = None
    for _ in range(passes):
        part = x.astype(BF16)
        d = jnp.dot(tri, part, preferred_element_type=F32)
        acc = d if acc is None else acc + d
        x = x - part.astype(F32)
    return acc


GROUPS = (4, 2, 1)


def _loop_blocks(n, tiles, carry, descending=False, groups=GROUPS):
    at = (lambda k: n - 1 - k) if descending else (lambda k: k)
    done = 0
    for g in groups:
        left = n - done
        carry = lax.fori_loop(0, left // g, lambda h, c, g=g, done=done: tiles([at(done + g * h + k) for k in range(g)], c), carry)
        done = done + (left // g) * g
    return carry


def _resident(shape):
    return pl.BlockSpec((1,) + shape, lambda h, i: (h,) + (0,) * len(shape), pipeline_mode=pl.Buffered(1))


def _rows_per_head(a, b):
    return jnp.concatenate([jnp.broadcast_to(a, (HEAD_DIM, BLK)), jnp.broadcast_to(b, (HEAD_DIM, BLK))], axis=0)


def _fold_heads(stacked, in_a):
    return jnp.where(in_a, stacked[:BLK], stacked[BLK:])


def _xy_gather_copies(ins, outs, send_sems, recv_sems, local_sems):
    x, y, c = lax.axis_index("x"), lax.axis_index("y"), lax.axis_index("c")
    chips = [(1 - x, y), (x, 1 - y), (1 - x, 1 - y)]
    mine = 2 * x + y
    local, remote = [], []
    for w in range(len(ins)):
        local.append(pltpu.make_async_copy(ins[w], outs[w].at[mine], local_sems.at[w]))
        for k, (px, py) in enumerate(chips):
            remote.append(pltpu.make_async_remote_copy(
                src_ref=ins[w], dst_ref=outs[w].at[mine], send_sem=send_sems.at[3 * w + k],
                recv_sem=recv_sems.at[3 * w + k], device_id=(px, py, c), device_id_type=MESH))
    return local, remote


def _fox_fwd(qx, kx, v_t, dg, shards):
    p, t = qx.shape[0], qx.shape[1]
    nq = t // BLK
    nh = 2 * p
    ns = len(shards)

    def body(q_ref, k_ref, vt_ref, *rest):
        shard_refs, (o_ref, lse_ref), gathered = rest[:ns], rest[ns:ns + 2], rest[ns + 2:2 * ns + 2]
        local, remote = _xy_gather_copies(shard_refs, gathered, *rest[2 * ns + 2:])
        i = pl.program_id(1)

        @pl.when((pl.program_id(0) == 0) & (i == 0))
        def _():
            for cp in local + remote:
                cp.start()

        causal, _ = _key_query_masks()
        q = q_ref[0]

        def scores(j, masked):
            s2 = lax.dot_general(k_ref[0, j], q, NT_DIMS, preferred_element_type=F32)
            s = [s2[a * BLK:(a + 1) * BLK] for a in range(2)]
            return [jnp.where(causal, x, NEG) for x in s] if masked else s

        def update(blocks, carry):
            m, l, acc = list(carry[0]), list(carry[1]), carry[2]
            for j, s in blocks:
                alpha, pr = [], []
                for a in range(2):
                    mn = jnp.maximum(m[a], jnp.max(s[a], axis=0, keepdims=True))
                    pa = jnp.exp(s[a] - mn)
                    al = jnp.exp(m[a] - mn)
                    l[a] = al * l[a] + jnp.sum(pa, axis=0, keepdims=True)
                    m[a] = mn
                    alpha.append(al)
                    pr.append(pa.astype(BF16))
                acc = _rows_per_head(*alpha) * acc + jnp.dot(vt_ref[0, j], jnp.concatenate(pr, axis=0), preferred_element_type=F32)
            return tuple(m), tuple(l), acc

        tiles = lambda js, c: update([(j, scores(j, False)) for j in js], c)
        neg, zero = jnp.full((1, BLK), NEG, F32), jnp.zeros((1, BLK), F32)
        carry = _loop_blocks(jnp.maximum(i - 1, 0), tiles, ((neg, neg), (zero, zero), jnp.zeros((LANES, BLK), F32)),
                             groups=(8, 4, 2, 1))
        paired = jnp.minimum(i, 1)
        carry = lax.fori_loop(0, paired, lambda _, c: update([(i - 1, scores(i - 1, False)), (i, scores(i, True))], c), carry)
        m, l, acc = lax.fori_loop(0, 1 - paired, lambda _, c: update([(i, scores(i, True))], c), carry)
        o_ref[...] = acc / _rows_per_head(*l)
        lse_ref[0] = m[0] + jnp.log(l[0])
        lse_ref[1] = m[1] + jnp.log(l[1])

        @pl.when((pl.program_id(0) == p - 1) & (i == nq - 1))
        def _():
            for cp in remote:
                cp.wait_recv()
            for cp in remote:
                cp.wait_send()
            for cp in local:
                cp.wait()

    row = pl.BlockSpec((2, 1, BLK), lambda h, i: (h, 0, i))
    hbm = pl.BlockSpec(memory_space=pltpu.HBM)
    return _pcall(
        body, name="fox_fwd", grid=(p, nq),
        out_shape=[jax.ShapeDtypeStruct((dg, t), F32), jax.ShapeDtypeStruct((nh, 1, t), F32)]
        + [jax.ShapeDtypeStruct((4,) + s.shape, s.dtype) for s in shards],
        in_specs=[pl.BlockSpec((1, BLK, 2 * LANES), lambda h, i: (h, i, 0)), _resident((nq, 2 * BLK, 2 * LANES)),
                  _resident((nq, LANES, 2 * BLK))] + [hbm] * ns,
        out_specs=[pl.BlockSpec((LANES, BLK), lambda h, i: (h, i)), row] + [hbm] * ns,
        scratch_shapes=[pltpu.SemaphoreType.DMA((3 * ns,)), pltpu.SemaphoreType.DMA((3 * ns,)), pltpu.SemaphoreType.DMA((ns,))],
        compiler_params=_params(("arbitrary", "arbitrary")),
    )(qx, kx, v_t, *shards)


def _swap_copies(ins, outs, send_sems, recv_sems, local_sems):
    x, y, c = lax.axis_index("x"), lax.axis_index("y"), lax.axis_index("c")
    local, remote = [], []
    for w in range(len(ins)):
        local.append(pltpu.make_async_copy(ins[w], outs[w].at[c], local_sems.at[w]))
        remote.append(pltpu.make_async_remote_copy(
            src_ref=ins[w], dst_ref=outs[w].at[c], send_sem=send_sems.at[w], recv_sem=recv_sems.at[w],
            device_id=(x, y, 1 - c), device_id_type=MESH))
    return local, remote


def _fox_bwd(qx, kx, kx_t, v_st, do, delta, lse, dg, halves):
    p, t = qx.shape[0], qx.shape[1]
    nq = t // BLK
    nh = 2 * p
    ns = len(halves)

    def body(q_ref, k_ref, kt_ref, v_ref, do_ref, dl_ref, lse_ref, *rest):
        half_refs, (dq_ref, dft_ref, dk_ref, dv_ref, dkx_ref), both_refs = rest[:ns], rest[ns:ns + 5], rest[ns + 5:2 * ns + 5]
        local, remote = _swap_copies(half_refs, both_refs, *rest[2 * ns + 5:])
        i = pl.program_id(1)

        @pl.when((pl.program_id(0) == 0) & (i == 0))
        def _():
            for cp in local + remote:
                cp.start()

        @pl.when(i == 0)
        def _():
            dk_ref[...] = jnp.zeros_like(dk_ref)
            dv_ref[...] = jnp.zeros_like(dv_ref)
            dkx_ref[...] = jnp.zeros_like(dkx_ref)

        in_a, _ = _head_masks()
        first_lane = lax.broadcasted_iota(jnp.int32, (1, LANES), 1) == 0
        causal, _ = _key_query_masks()
        q, do2 = q_ref[0], do_ref[...]
        dl = (dl_ref[0], dl_ref[1])
        lse = (lse_ref[0], lse_ref[1])

        def products(j):
            return (lax.dot_general(k_ref[0, j], q, NT_DIMS, preferred_element_type=F32),
                    lax.dot_general(v_ref[0, j], do2, NT_DIMS, preferred_element_type=F32))

        def dscores(prod, masked):
            s2, dp2 = prod
            pr, ds = [], []
            for a in range(2):
                s = s2[a * BLK:(a + 1) * BLK]
                if masked:
                    s = jnp.where(causal, s, NEG)
                pa = jnp.exp(s - lse[a])
                ds.append((pa * (dp2[a * BLK:(a + 1) * BLK] - dl[a])).astype(BF16))
                pr.append(pa.astype(BF16))
            return jnp.concatenate(ds, axis=0), jnp.concatenate(pr, axis=0)

        def accumulate(j, dsb, prb, dq):
            off = pl.multiple_of(j * BLK, BLK)
            dk_full = jnp.dot(dsb, q, preferred_element_type=F32)
            dk_ref[pl.ds(off, BLK), :] += _fold_heads(dk_full[:, :LANES], in_a)
            dkx_ref[pl.ds(off, BLK), :] += jnp.where(first_lane, dk_full[:BLK, LANES:], dk_full[BLK:, LANES:])
            dv_ref[pl.ds(off, BLK), :] += _fold_heads(jnp.dot(prb, do2, preferred_element_type=F32), in_a)
            return dq + jnp.dot(kt_ref[0, j], dsb, preferred_element_type=F32)

        def tiles(js, dq, masked=False):
            prods = [products(j) for j in js]
            grads = [dscores(pr, masked) for pr in prods]
            for j, (dsb, prb) in zip(js, grads):
                dq = accumulate(j, dsb, prb, dq)
            return dq

        dq = _loop_blocks(i, tiles, jnp.zeros((XROWS, BLK), F32), groups=(6, 3, 1))
        dq = tiles([i], dq, True)
        dq_ref[...] = dq[:LANES]
        dft_ref[0] = dq[LANES + LANE_FT_A:LANES + LANE_FT_A + 1]
        dft_ref[1] = dq[LANES + LANE_FT_B:LANES + LANE_FT_B + 1]

        @pl.when((pl.program_id(0) == p - 1) & (i == nq - 1))
        def _():
            for cp in remote:
                cp.wait_recv()
            for cp in remote:
                cp.wait_send()
            for cp in local:
                cp.wait()

    row = pl.BlockSpec((2, 1, BLK), lambda h, i: (h, 0, i))
    acc = pl.BlockSpec((t, LANES), lambda h, i: (0, h))
    hbm = pl.BlockSpec(memory_space=pltpu.HBM)
    return _pcall(
        body, name="fox_bwd", grid=(p, nq),
        out_shape=[jax.ShapeDtypeStruct((dg, t), F32), jax.ShapeDtypeStruct((nh, 1, t), F32)] + [jax.ShapeDtypeStruct((t, dg), F32)] * 3
        + [jax.ShapeDtypeStruct((2,) + h.shape, h.dtype) for h in halves],
        in_specs=[pl.BlockSpec((1, BLK, 2 * LANES), lambda h, i: (h, i, 0)), _resident((nq, 2 * BLK, 2 * LANES)),
                  _resident((nq, XROWS, 2 * BLK)), _resident((nq, 2 * BLK, LANES)),
                  pl.BlockSpec((BLK, LANES), lambda h, i: (i, h)), row, row] + [hbm] * ns,
        out_specs=[pl.BlockSpec((LANES, BLK), lambda h, i: (h, i)), row, acc, acc, acc] + [hbm] * ns,
        scratch_shapes=[pltpu.SemaphoreType.DMA((ns,)), pltpu.SemaphoreType.DMA((ns,)), pltpu.SemaphoreType.DMA((ns,))],
        compiler_params=_params(("arbitrary", "arbitrary")),
    )(qx, kx, kx_t, v_st, do, delta, lse, *halves)


def _softplus_of(z):
    return jnp.maximum(z, 0.0) + jnp.log(1.0 + jnp.exp2(jnp.abs(z) * (-LOG2_E)))


def _sb_fwd(qkv, k_st, v_t, dg):
    t = qkv.shape[0]
    p, nq = dg // LANES, t // BLK
    nh = 2 * p

    def body(q_ref, k_ref, vt_ref, o_ref, rt_ref):
        i = pl.program_id(1)
        _, strict = _key_query_masks()
        suffix = _key_triangle("suffix")
        q = q_ref[...]

        def scores(j):
            z2 = lax.dot_general(k_ref[0, j], q, NT_DIMS, preferred_element_type=F32)
            return [z2[a * BLK:(a + 1) * BLK] for a in range(2)]

        def suffix_sums(z, masked):
            out = []
            for a in range(2):
                sp = _softplus_of(z[a])
                if masked:
                    sp = jnp.where(strict, sp, 0.0)
                out.append(_tri_dot(suffix, sp, 2))
            return out

        def weights(z, cs, rest, masked):
            w, rest_new = [], []
            for a in range(2):
                wa = jnp.exp(z[a] - cs[a] - rest[a])
                if masked:
                    wa = jnp.where(strict, wa, 0.0)
                w.append(wa.astype(BF16))
                rest_new.append(rest[a] + cs[a][0:1])
            return jnp.concatenate(w, axis=0), tuple(rest_new)

        def tiles(js, carry, masked=False):
            flags = masked if isinstance(masked, tuple) else (masked,) * len(js)
            rest, acc = carry
            zs = [scores(j) for j in js]
            css = [suffix_sums(z, f) for z, f in zip(zs, flags)]
            ws = []
            for z, cs, f in zip(zs, css, flags):
                w2, rest = weights(z, cs, rest, f)
                ws.append(w2)
            for j, w2 in zip(js, ws):
                acc = acc + jnp.dot(vt_ref[0, j], w2, preferred_element_type=F32)
            return rest, acc

        zero = jnp.zeros((1, BLK), F32)
        carry = ((zero, zero), jnp.zeros((LANES, BLK), F32))
        paired = jnp.minimum(i, 1)
        carry = lax.fori_loop(0, paired, lambda _, c: tiles([i, i - 1], c, (True, False)), carry)
        carry = lax.fori_loop(0, 1 - paired, lambda _, c: tiles([i], c, True), carry)
        rest, acc = _loop_blocks(jnp.maximum(i - 1, 0), tiles, carry, descending=True, groups=(8, 4, 2, 1))
        o_ref[...] = acc
        rt_ref[0] = rest[0]
        rt_ref[1] = rest[1]

    return _pcall(
        body, name="sb_fwd", grid=(p, nq),
        out_shape=[jax.ShapeDtypeStruct((dg, t), F32), jax.ShapeDtypeStruct((nh, 1, t), F32)],
        in_specs=[pl.BlockSpec((BLK, LANES), lambda h, i: (i, 3 * p + h)), _resident((nq, 2 * BLK, LANES)),
                  _resident((nq, LANES, 2 * BLK))],
        out_specs=[pl.BlockSpec((LANES, BLK), lambda h, i: (h, i)), pl.BlockSpec((2, 1, BLK), lambda h, i: (h, 0, i))],
        compiler_params=_params(("arbitrary", "arbitrary")),
    )(qkv, k_st, v_t)


def _scatter8_copies(ins, outs, send_sems, recv_sems, local_sems):
    x, y, c = lax.axis_index("x"), lax.axis_index("y"), lax.axis_index("c")
    me = 4 * x + 2 * y + c
    local, remote = [], []
    for w in range(len(ins)):
        local.append(pltpu.make_async_copy(ins[w].at[me], outs[w].at[me], local_sems.at[w]))
        for f in range(1, 8):
            px = 1 - x if f & 4 else x
            py = 1 - y if f & 2 else y
            pc = 1 - c if f & 1 else c
            remote.append(pltpu.make_async_remote_copy(
                src_ref=ins[w].at[4 * px + 2 * py + pc], dst_ref=outs[w].at[me],
                send_sem=send_sems.at[7 * w + f - 1], recv_sem=recv_sems.at[7 * w + f - 1],
                device_id=(px, py, pc), device_id_type=MESH))
    return local, remote


def _sb_bwd(qkv, k_st, k_t, v_st, do, rtot, dg, pieces):
    t = qkv.shape[0]
    p, nq = dg // LANES, t // BLK
    ns = len(pieces)

    def body(q_ref, k_ref, kt_ref, v_ref, do_ref, rt_ref, *rest):
        piece_refs, (dq_ref, dk_ref, dv_ref), recv_refs = rest[:ns], rest[ns:ns + 3], rest[ns + 3:2 * ns + 3]
        local, remote = _scatter8_copies(piece_refs, recv_refs, *rest[2 * ns + 3:])
        i = pl.program_id(1)

        @pl.when((pl.program_id(0) == 0) & (i == 0))
        def _():
            for cp in local + remote:
                cp.start()

        @pl.when(i == 0)
        def _():
            dk_ref[...] = jnp.zeros_like(dk_ref)
            dv_ref[...] = jnp.zeros_like(dv_ref)

        in_a, _ = _head_masks()
        _, strict = _key_query_masks()
        before_m, prefix_m = _key_triangle("before"), _key_triangle("prefix")
        q, do2 = q_ref[...], do_ref[...]
        rt = (rt_ref[0], rt_ref[1])

        def products(j):
            z2 = lax.dot_general(k_ref[0, j], q, NT_DIMS, preferred_element_type=F32)
            da2 = lax.dot_general(v_ref[0, j], do2, NT_DIMS, preferred_element_type=F32)
            return [z2[a * BLK:(a + 1) * BLK] for a in range(2)], [da2[a * BLK:(a + 1) * BLK] for a in range(2)]

        def softplus_sums(z, masked):
            sp = [_softplus_of(x) for x in z]
            if masked:
                sp = [jnp.where(strict, x, 0.0) for x in sp]
            return sp, [_tri_dot(before_m, x, 2) for x in sp]

        def weight_grads(z, da, sp, pre, before, masked):
            w, g, pg, before_new = [], [], [], []
            for a in range(2):
                wa = jnp.exp(z[a] + (before[a] - rt[a]) + pre[a])
                if masked:
                    wa = jnp.where(strict, wa, 0.0)
                ga = wa * da[a]
                w.append(wa.astype(BF16))
                g.append(ga)
                pg.append(jnp.dot(prefix_m, ga.astype(BF16), preferred_element_type=F32))
                before_new.append(before[a] + pre[a][BLK - 1:BLK] + sp[a][BLK - 1:BLK])
            return jnp.concatenate(w, axis=0), g, pg, tuple(before_new)

        def dlogits(sp, g, pg, gbefore, masked):
            dz, gbefore_new = [], []
            for a in range(2):
                s_incl = gbefore[a] + pg[a]
                dza = (g[a] - s_incl) + jnp.exp2(sp[a] * (-LOG2_E)) * s_incl
                if masked:
                    dza = jnp.where(strict, dza, 0.0)
                dz.append(dza.astype(BF16))
                gbefore_new.append(s_incl[BLK - 1:BLK])
            return jnp.concatenate(dz, axis=0), tuple(gbefore_new)

        def accumulate(j, dzb, wb, dq):
            off = pl.multiple_of(j * BLK, BLK)
            dk_ref[pl.ds(off, BLK), :] += _fold_heads(jnp.dot(dzb, q, preferred_element_type=F32), in_a)
            dv_ref[pl.ds(off, BLK), :] += _fold_heads(jnp.dot(wb, do2, preferred_element_type=F32), in_a)
            return dq + jnp.dot(kt_ref[0, j], dzb, preferred_element_type=F32)

        def tiles(js, carry, masked=False):
            flags = masked if isinstance(masked, tuple) else (masked,) * len(js)
            before, gbefore, dq = carry
            prods = [products(j) for j in js]
            sums = [softplus_sums(z, f) for (z, _), f in zip(prods, flags)]
            grads = []
            for (z, da), (sp, pre), f in zip(prods, sums, flags):
                wb, g, pg, before = weight_grads(z, da, sp, pre, before, f)
                grads.append((wb, g, pg))
            for j, (sp, _), (wb, g, pg), f in zip(js, sums, grads, flags):
                dzb, gbefore = dlogits(sp, g, pg, gbefore, f)
                dq = accumulate(j, dzb, wb, dq)
            return before, gbefore, dq

        zero = jnp.zeros((1, BLK), F32)
        carry = _loop_blocks(jnp.maximum(i - 1, 0), tiles, ((zero, zero), (zero, zero), jnp.zeros((LANES, BLK), F32)),
                             groups=(3, 2, 1))
        paired = jnp.minimum(i, 1)
        carry = lax.fori_loop(0, paired, lambda _, c: tiles([i - 1, i], c, (False, True)), carry)
        carry = lax.fori_loop(0, 1 - paired, lambda _, c: tiles([i], c, True), carry)
        dq_ref[...] = carry[2]

        @pl.when((pl.program_id(0) == p - 1) & (i == nq - 1))
        def _():
            for cp in remote:
                cp.wait_recv()
            for cp in remote:
                cp.wait_send()
            for cp in local:
                cp.wait()

    acc = pl.BlockSpec((t, LANES), lambda h, i: (0, h))
    hbm = pl.BlockSpec(memory_space=pltpu.HBM)
    return _pcall(
        body, name="sb_bwd", grid=(p, nq),
        out_shape=[jax.ShapeDtypeStruct((dg, t), F32)] + [jax.ShapeDtypeStruct((t, dg), F32)] * 2
        + [jax.ShapeDtypeStruct(pc.shape, pc.dtype) for pc in pieces],
        in_specs=[pl.BlockSpec((BLK, LANES), lambda h, i: (i, 3 * p + h)), _resident((nq, 2 * BLK, LANES)),
                  _resident((nq, LANES, 2 * BLK)), _resident((nq, 2 * BLK, LANES)),
                  pl.BlockSpec((BLK, LANES), lambda h, i: (i, h)), pl.BlockSpec((2, 1, BLK), lambda h, i: (h, 0, i))] + [hbm] * ns,
        out_specs=[pl.BlockSpec((LANES, BLK), lambda h, i: (h, i)), acc, acc] + [hbm] * ns,
        scratch_shapes=[pltpu.SemaphoreType.DMA((7 * ns,)), pltpu.SemaphoreType.DMA((7 * ns,)), pltpu.SemaphoreType.DMA((ns,))],
        compiler_params=_params(("arbitrary", "arbitrary")),
    )(qkv, k_st, k_t, v_st, do, rtot, *pieces)


def _tri_constants(nh, t):
    nb = t // LANES
    r = nh * nb
    li = np.arange(LANES)
    tri_in = (li[:, None] <= li[None, :])
    ri = np.arange(r)
    same = (ri[:, None] // nb) == (ri[None, :] // nb)
    blk = same & (ri[None, :] < ri[:, None])
    blk_rev = same & (ri[None, :] > ri[:, None])
    head_rows = (np.arange(max(8, nh))[:, None] == (ri[None, :] // nb))
    as_bf16 = lambda a: jnp.asarray(a.astype(np.float32), BF16)
    return as_bf16(tri_in), as_bf16(blk), as_bf16(tri_in.T), as_bf16(blk_rev), as_bf16(head_rows)


def kernel(x, c, w_ada, b_ada, g_attn, w_in, b_fgate, g_out_fox, g_out_sb, w_out, g_mlp, w_up, conv_w, conv_b, w_down, g_final, loss_target, m_w_ada, m_b_ada, m_g_attn, m_w_in, m_b_fgate, m_g_out_fox, m_g_out_sb, m_w_out, m_g_mlp, m_w_up, m_conv_w, m_conv_b, m_w_down, m_g_final, v_w_ada, v_b_ada, v_g_attn, v_w_in, v_b_fgate, v_g_out_fox, v_g_out_sb, v_w_out, v_g_mlp, v_w_up, v_conv_w, v_conv_b, v_w_down, v_g_final):
    t, d = x.shape[1], x.shape[2]
    dg = d // 2
    nh = dg // HEAD_DIM
    n_in = 6 * dg + nh
    dff = w_down.shape[1] * 4
    dfp = -(-dff // 256) * 256
    cf = 256
    tm = _tile(t, (512, 256, 128))
    nq = t // BLK
    xi, yi, ci = lax.axis_index("x"), lax.axis_index("y"), lax.axis_index("c")
    shard = 2 * xi + yi
    me = 4 * xi + 2 * yi + ci

    x2d, tg2d = x[0], loss_target[0]

    c_all = _all_gather8(jnp.pad(c, ((0, 7), (0, 0)))).reshape(8, 8, d)[:, 0, :]
    ada_cols = w_ada.shape[2]
    b_shard = lax.dynamic_slice(b_ada, (0, shard * ada_cols), (1, ada_cols))
    sc_all, mod_shard = _ada_fwd(c_all, w_ada[0], b_shard)
    mod_all = _all_gather8(mod_shard).reshape(4, 2, 8, ada_cols)
    mod_me = lax.dynamic_index_in_dim(mod_all[:, 0], me, axis=1, keepdims=False)
    mod8 = jnp.pad(mod_me.reshape(6, d), ((0, 2), (0, 0)))

    (g_in,) = _gather_xy([w_in[0].astype(BF16)])
    later_shards = [w_out[0].astype(BF16), w_up[0].astype(BF16), w_down[0].astype(BF16), conv_w[0]]
    w_in_full = jnp.transpose(g_in, (1, 0, 2)).reshape(d, n_in)
    w_qkv = w_in_full[:, :6 * dg]
    w_f = jnp.pad(w_in_full[:, 6 * dg:].T, ((0, LANES - nh), (0, 0)))

    qkv, fl, h1 = _in_proj_fwd(x2d, mod8, g_attn, w_qkv, w_f, tm)
    tri_in, tri_blk, tri_in_rev, tri_blk_rev, head_rows = _tri_constants(nh, t)
    fl2d = fl[:nh].reshape(nh * t // LANES, LANES)
    b_rows = jnp.repeat(b_fgate[0], t // LANES)[:, None]
    f2d = _fgate_fwd(fl2d, b_rows, tri_in, tri_blk)
    frow = f2d.reshape(nh, 1, t)
    pairs = nh // 2
    qx, kx, kx_t, vf_st, vf_t, ks_st, ks_t, vs_st, vs_t = _attn_operands(qkv, frow, dg)
    o_fox, lse, g_out, g_up, g_down, g_cw = _fox_fwd(qx, kx, vf_t, dg, later_shards)
    w_out_full = g_out.reshape(2 * dg, d)
    w_up_full = jnp.transpose(g_up, (1, 0, 2)).reshape(d, 2 * dff)
    padc = ((0, 0), (0, dfp - dff))
    wg, wv = jnp.pad(w_up_full[:, :dff], padc), jnp.pad(w_up_full[:, dff:], padc)
    wd = jnp.pad(g_down.reshape(dff, d), ((0, dfp - dff), (0, 0)))
    cw_full = jnp.transpose(g_cw, (1, 0, 2)).reshape(3, 2 * dff)
    cw4 = jnp.concatenate([cw_full, conv_b], axis=0)
    cwg = jnp.pad(cw4[:, :dff], ((0, 4), (0, dfp - dff)))
    cwv = jnp.pad(cw4[:, dff:], ((0, 4), (0, dfp - dff)))
    o_sb, rtot = _sb_fwd(qkv, ks_st, vs_t, dg)
    li = np.arange(dg)
    bd = jnp.asarray((li[:, None] // HEAD_DIM == li[None, :] // HEAD_DIM).astype(np.float32), BF16)
    hsel = jnp.asarray((np.arange(LANES)[:, None] == li[None, :] // HEAD_DIM).astype(np.float32), BF16)
    x2, mix, mix_t = _attn_out_fwd(x2d, o_fox, o_sb, g_out_fox, g_out_sb, w_out_full, mod8, bd, tm)
    g_final2 = g_final[None, :]
    dx3, h2, part_f = _ffn_fwd(x2, tg2d, mod8, g_mlp, g_final2, wg, wv, cwg, cwv, wd, tm, cf)

    tm_ffn_bwd = _tile(t, (256, 128))
    dx2, dupg, dupv, act, dxg3, part_b, pcg, pcv = _ffn_bwd(x2, dx3, mod8, g_mlp, wg, wv, cwg, cwv, wd, tm_ffn_bwd, cf)
    do_fox, do_sb, delta, dxg2, part_o = _attn_out_bwd(dx2, mix, o_fox, o_sb, g_out_fox, g_out_sb, w_out_full, mod8, bd, hsel, tm)
    drow = delta[:nh].reshape(nh, 1, t)

    def col_pieces(g):
        r, cc = g.shape
        return jnp.transpose(g.reshape(2, r // 2, 4, cc // 4), (2, 0, 1, 3)).reshape(8, r // 2, cc // 4)

    def row_pieces(g):
        r, cc = g.shape
        return g.reshape(8, r // 8, cc)

    gw_out = _matmul_tn(mix_t, dxg2, "grad_w_out")
    gw_upg = _matmul_tn(h2, dupg, "grad_w_up_gate")
    gw_upv = _matmul_tn(h2, dupv, "grad_w_up_val")
    gw_up = jnp.concatenate([gw_upg[:, :dff], gw_upv[:, :dff]], axis=1)
    gw_down = _matmul_tn(act, dxg3, "grad_w_down")[:dff]
    early = (row_pieces(gw_out), col_pieces(gw_up), row_pieces(gw_down))
    early = [_to_bf16(pc, "pieces_bf16_" + nm) for pc, nm in zip(early, ("w_out", "w_up", "w_down"))]

    dq_s, dk_s, dv_s, *recv_early = _sb_bwd(qkv, ks_st, ks_t, vs_st, do_sb, rtot, dg, early)
    halves_early = [_sum_leading(rv, nm) for rv, nm in zip(recv_early, ("sum_w_out", "sum_w_up", "sum_w_down"))]
    dq_f, dft, dk_f, dv_f, dkx, *swapped_early = _fox_bwd(qx, kx, kx_t, vf_st, do_fox, drow, lse, dg, halves_early)
    f2d_shape = (nh * t // LANES, LANES)
    dfs = jnp.transpose(dkx.reshape(t, pairs, LANES)[:, :, :2], (1, 2, 0))
    dfl2d, gb8 = _fgate_bwd(fl2d, b_rows, dft.reshape(f2d_shape), dfs.reshape(f2d_shape), tri_in_rev, tri_blk_rev, head_rows)
    dfl = jnp.pad(dfl2d.reshape(nh, t), ((0, LANES - nh), (0, 0)))
    grad_x, dproj, dflb, part_i = _in_proj_bwd([dq_f, dk_f, dv_f, dq_s, dk_s, dv_s], dfl, w_qkv, w_f, x2d, dx2, mod8, g_attn, tm)

    gw_qkv = _matmul_tn(h1, dproj, "grad_w_qkv")
    gw_f = _matmul_tn(h1, dflb, "grad_w_f")
    gw_in = jnp.concatenate([gw_qkv, gw_f[:, :nh]], axis=1)

    sf = _sum_leading(part_f, "sum_part_ffn_fwd")
    sb_ = _sum_leading(part_b, "sum_part_ffn_bwd")
    so = _sum_leading(part_o, "sum_part_attn_out")
    si = _sum_leading(part_i, "sum_part_in_proj")
    scg = _sum_leading(pcg, "sum_part_conv_gate")
    scv = _sum_leading(pcv, "sum_part_conv_val")
    gb_f = gb8[:nh, 0]
    dmod = jnp.concatenate([si[0], si[1], so[0], sb_[0], sb_[1], sf[1]])
    g_conv_w = jnp.concatenate([scg[0:3, :dff], scv[0:3, :dff]], axis=1).reshape(-1)
    g_conv_b = jnp.concatenate([scg[3, :dff], scv[3, :dff]])
    loss_part = jnp.sum(sf[2])
    fields = [dmod, si[2], gb_f, so[1, :dg], so[1, dg:], sb_[2], g_conv_b, sf[0], g_conv_w, loss_part[None]]
    sizes = [int(f.shape[0]) for f in fields]
    n_pack = sum(sizes)
    lanes_pack = -(-n_pack // (8 * LANES)) * LANES
    pack = jnp.pad(jnp.concatenate(fields), (0, 8 * lanes_pack - n_pack)).reshape(8, lanes_pack)
    gathered = _all_gather8(pack)
    tot = _sum_leading(gathered.reshape(8, 8, lanes_pack), "sum_pack").reshape(-1)
    offs = np.concatenate([[0], np.cumsum(sizes)])
    take = lambda k: tot[int(offs[k]):int(offs[k + 1])]
    g_b_ada, g_g_attn, g_b_fgate, g_g_fox, g_g_sb, g_g_mlp, g_cb, g_g_final, g_cw_full, loss_v = [take(k) for k in range(10)]
    loss = loss_v[0]
    dmod_all = gathered.reshape(8, 8 * lanes_pack)[:, :6 * d]
    dmod_cols = lax.dynamic_slice(dmod_all, (0, shard * ada_cols), (8, ada_cols))
    g_w_ada = _ada_bwd(sc_all.T, dmod_cols)

    (recv_in,) = _scatter8([_to_bf16(col_pieces(gw_in), "pieces_bf16_w_in")])
    (swapped_in,) = _swap_halves([_sum_leading(recv_in, "sum_w_in")])
    swapped = [swapped_in] + swapped_early
    g_w_in, g_w_out, g_w_up, g_w_down = [s.reshape(2 * s.shape[1], s.shape[2]) for s in swapped]
    g_conv_w_shard = lax.dynamic_slice(g_cw_full.reshape(3, 2 * dff), (0, shard * (dff // 2)), (3, dff // 2))

    grads, deltas, new_m, new_v = {}, {}, {}, {}

    def step(name, w, g, m, v):
        shape = w.shape
        as2d = lambda a: a.reshape(-1, shape[-1])
        dl, nm, nv = _adamw(as2d(w), as2d(g), as2d(m), as2d(v), "adamw_" + name)
        grads[name], deltas[name], new_m[name], new_v[name] = g.reshape(shape), dl.reshape(shape), nm.reshape(shape), nv.reshape(shape)

    step("w_ada", w_ada, g_w_ada, m_w_ada, v_w_ada)
    step("w_in", w_in, g_w_in, m_w_in, v_w_in)
    step("w_out", w_out, g_w_out, m_w_out, v_w_out)
    step("w_up", w_up, g_w_up, m_w_up, v_w_up)
    step("conv_w", conv_w, g_conv_w_shard, m_conv_w, v_conv_w)
    step("w_down", w_down, g_w_down, m_w_down, v_w_down)

    small = [("b_ada", b_ada, g_b_ada, m_b_ada, v_b_ada), ("g_attn", g_attn, g_g_attn, m_g_attn, v_g_attn),
             ("b_fgate", b_fgate, g_b_fgate, m_b_fgate, v_b_fgate), ("g_out_fox", g_out_fox, g_g_fox, m_g_out_fox, v_g_out_fox),
             ("g_out_sb", g_out_sb, g_g_sb, m_g_out_sb, v_g_out_sb), ("g_mlp", g_mlp, g_g_mlp, m_g_mlp, v_g_mlp),
             ("conv_b", conv_b, g_cb, m_conv_b, v_conv_b), ("g_final", g_final, g_g_final, m_g_final, v_g_final)]
    ssz = [int(np.prod(s[1].shape)) for s in small]
    n_small = sum(ssz)
    lanes_small = -(-n_small // (8 * LANES)) * LANES
    packs = [jnp.pad(jnp.concatenate([s[k].reshape(-1) for s in small]), (0, 8 * lanes_small - n_small)).reshape(8, lanes_small)
             for k in (1, 2, 3, 4)]
    dl_s, nm_s, nv_s = _adamw(*packs, "adamw_small")
    so_ = np.concatenate([[0], np.cumsum(ssz)])
    for k, s in enumerate(small):
        cut = lambda a: a.reshape(-1)[int(so_[k]):int(so_[k + 1])].reshape(s[1].shape)
        grads[s[0]], deltas[s[0]], new_m[s[0]], new_v[s[0]] = s[2].reshape(s[1].shape), cut(dl_s), cut(nm_s), cut(nv_s)

    order = ["w_ada", "b_ada", "g_attn", "w_in", "b_fgate", "g_out_fox", "g_out_sb", "w_out", "g_mlp", "w_up",
             "conv_w", "conv_b", "w_down", "g_final"]
    return (loss, grad_x[None], *[grads[n] for n in order], *[deltas[n] for n in order],
            *[new_m[n] for n in order], *[new_v[n] for n in order])
```

```python
import functools

import numpy as np
import jax
import jax.numpy as jnp
from jax import lax
from jax.experimental import pallas as pl
from jax.experimental.pallas import tpu as pltpu

F32 = jnp.float32
BF16 = jnp.bfloat16
MESH = pl.DeviceIdType.MESH

HEAD_DIM = 64
LANES = 128
EPS = 1e-6
NEG = -1e30
ADAM_LR, ADAM_B1, ADAM_B2, ADAM_EPS, ADAM_WD, ADAM_STEP = 0.001, 0.9, 0.999, 1e-08, 0.01, 10
V7X_VMEM_BYTES = 64 * 1024 * 1024
VMEM_LIMIT = V7X_VMEM_BYTES - 12 * 1024 * 1024
NT_DIMS = (((1,), (1,)), ((), ()))
LOG2_E = 1.4426950408889634


def _pcall(body, **kw):
    return pl.pallas_call(body, **kw)


def _params(sem=None, **kw):
    return pltpu.CompilerParams(dimension_semantics=sem, vmem_limit_bytes=VMEM_LIMIT, **kw)


def _split_dot(x, m, passes):
    acc = None
    for _ in range(passes):
        part = x.astype(BF16)
        d = jnp.dot(part, m, preferred_element_type=F32)
        acc = d if acc is None else acc + d
        x = x - part.astype(F32)
    return acc


def _tile(n, candidates):
    for t in candidates:
        if n % t == 0:
            return t
    return n


def _rows_tile(rows, row_bytes, budget=2 * 1024 * 1024):
    best = None
    for t in range(8, rows + 1, 8):
        if rows % t == 0 and t * row_bytes <= budget:
            best = t
    return best if best is not None else rows


def _all_gather8(v):
    m_per, n = v.shape

    def body(x_ref, out_ref, send_sems, recv_sems, local_sem):
        x, y, c = lax.axis_index("x"), lax.axis_index("y"), lax.axis_index("c")
        me, sibling = (x, y, c), (x, y, 1 - c)
        chips = [(1 - x, y), (x, 1 - y), (1 - x, 1 - y)]

        def rows(px, py, pc):
            return out_ref.at[pl.ds((4 * px + 2 * py + pc) * m_per, m_per), :]

        def copy(k, block, to, src=None):
            return pltpu.make_async_remote_copy(
                src_ref=rows(*block) if src is None else src, dst_ref=rows(*block),
                send_sem=send_sems.at[k], recv_sem=recv_sems.at[k], device_id=to, device_id_type=MESH)

        mine = pltpu.make_async_copy(x_ref, rows(*me), local_sem)
        mine.start()
        first = [copy(0, me, sibling, src=x_ref)]
        first += [copy(1 + j, me, (*chip, c), src=x_ref) for j, chip in enumerate(chips)]
        for cp in first:
            cp.start()
        passed = [copy(4 + j, (*chip, c), sibling) for j, chip in enumerate(chips)]
        for j, chip in enumerate(chips):
            copy(1 + j, (*chip, c), me).wait_recv()
            passed[j].start()
        copy(0, sibling, me).wait_recv()
        for j, chip in enumerate(chips):
            copy(4 + j, (*chip, 1 - c), me).wait_recv()
        for cp in first + passed:
            cp.wait_send()
        mine.wait()

    return _pcall(
        body, name="all_gather8",
        out_shape=jax.ShapeDtypeStruct((8 * m_per, n), v.dtype),
        in_specs=[pl.BlockSpec(memory_space=pltpu.VMEM)],
        out_specs=pl.BlockSpec(memory_space=pltpu.VMEM),
        scratch_shapes=[pltpu.SemaphoreType.DMA((7,)), pltpu.SemaphoreType.DMA((7,)), pltpu.SemaphoreType.DMA],
        compiler_params=pltpu.CompilerParams(vmem_limit_bytes=VMEM_LIMIT),
    )(v)


def _gather_xy(shards):
    n = len(shards)

    def body(*refs):
        ins, outs = refs[:n], refs[n:2 * n]
        send_sems, recv_sems, local_sems = refs[2 * n:]
        x, y, c = lax.axis_index("x"), lax.axis_index("y"), lax.axis_index("c")
        chips = [(1 - x, y), (x, 1 - y), (1 - x, 1 - y)]
        mine = 2 * x + y
        local, remote = [], []
        for w in range(n):
            cp = pltpu.make_async_copy(ins[w], outs[w].at[mine], local_sems.at[w])
            cp.start()
            local.append(cp)
            for k, (px, py) in enumerate(chips):
                cp = pltpu.make_async_remote_copy(
                    src_ref=ins[w], dst_ref=outs[w].at[mine], send_sem=send_sems.at[3 * w + k],
                    recv_sem=recv_sems.at[3 * w + k], device_id=(px, py, c), device_id_type=MESH)
                cp.start()
                remote.append(cp)
        for cp in remote:
            cp.wait_recv()
        for cp in remote:
            cp.wait_send()
        for cp in local:
            cp.wait()

    hbm = pl.BlockSpec(memory_space=pltpu.HBM)
    return _pcall(
        body, name="gather_xy",
        out_shape=[jax.ShapeDtypeStruct((4,) + s.shape, s.dtype) for s in shards],
        in_specs=[hbm] * n, out_specs=[hbm] * n,
        scratch_shapes=[pltpu.SemaphoreType.DMA((3 * n,)), pltpu.SemaphoreType.DMA((3 * n,)),
                        pltpu.SemaphoreType.DMA((n,))],
        compiler_params=pltpu.CompilerParams(vmem_limit_bytes=VMEM_LIMIT),
    )(*shards)


def _scatter8(pieces):
    n = len(pieces)

    def body(*refs):
        ins, outs = refs[:n], refs[n:2 * n]
        send_sems, recv_sems, local_sems = refs[2 * n:]
        x, y, c = lax.axis_index("x"), lax.axis_index("y"), lax.axis_index("c")
        me = 4 * x + 2 * y + c
        local, remote = [], []
        for w in range(n):
            cp = pltpu.make_async_copy(ins[w].at[me], outs[w].at[me], local_sems.at[w])
            cp.start()
            local.append(cp)
            for f in range(1, 8):
                px = 1 - x if f & 4 else x
                py = 1 - y if f & 2 else y
                pc = 1 - c if f & 1 else c
                cp = pltpu.make_async_remote_copy(
                    src_ref=ins[w].at[4 * px + 2 * py + pc], dst_ref=outs[w].at[me],
                    send_sem=send_sems.at[7 * w + f - 1], recv_sem=recv_sems.at[7 * w + f - 1],
                    device_id=(px, py, pc), device_id_type=MESH)
                cp.start()
                remote.append(cp)
        for cp in remote:
            cp.wait_recv()
        for cp in remote:
            cp.wait_send()
        for cp in local:
            cp.wait()

    hbm = pl.BlockSpec(memory_space=pltpu.HBM)
    return _pcall(
        body, name="scatter8",
        out_shape=[jax.ShapeDtypeStruct(p.shape, p.dtype) for p in pieces],
        in_specs=[hbm] * n, out_specs=[hbm] * n,
        scratch_shapes=[pltpu.SemaphoreType.DMA((7 * n,)), pltpu.SemaphoreType.DMA((7 * n,)),
                        pltpu.SemaphoreType.DMA((n,))],
        compiler_params=pltpu.CompilerParams(vmem_limit_bytes=VMEM_LIMIT),
    )(*pieces)


def _swap_halves(halves):
    n = len(halves)
    chunks = 8
    n_chunks = [max(k for k in (chunks, 4, 2, 1) if h.shape[0] % (8 * k) == 0) for h in halves]

    def body(*refs):
        ins, outs = refs[:n], refs[n:2 * n]
        send_sems, recv_sems, local_sems = refs[2 * n:]
        x, y, c = lax.axis_index("x"), lax.axis_index("y"), lax.axis_index("c")
        local, remote = [], []
        for w in range(n):
            cp = pltpu.make_async_copy(ins[w], outs[w].at[c], local_sems.at[w])
            cp.start()
            local.append(cp)
            rows = ins[w].shape[0] // n_chunks[w]
            for k in range(n_chunks[w]):
                cp = pltpu.make_async_remote_copy(
                    src_ref=ins[w].at[pl.ds(k * rows, rows)], dst_ref=outs[w].at[c, pl.ds(k * rows, rows)],
                    send_sem=send_sems.at[chunks * w + k], recv_sem=recv_sems.at[chunks * w + k],
                    device_id=(x, y, 1 - c), device_id_type=MESH)
                cp.start()
                remote.append(cp)
        for cp in remote:
            cp.wait_recv()
        for cp in remote:
            cp.wait_send()
        for cp in local:
            cp.wait()

    hbm = pl.BlockSpec(memory_space=pltpu.HBM)
    return _pcall(
        body, name="swap_halves",
        out_shape=[jax.ShapeDtypeStruct((2,) + h.shape, h.dtype) for h in halves],
        in_specs=[hbm] * n, out_specs=[hbm] * n,
        scratch_shapes=[pltpu.SemaphoreType.DMA((chunks * n,)), pltpu.SemaphoreType.DMA((chunks * n,)),
                        pltpu.SemaphoreType.DMA((n,))],
        compiler_params=pltpu.CompilerParams(vmem_limit_bytes=VMEM_LIMIT),
    )(*halves)


def _sum_leading(a, name):
    n, r, c = a.shape
    tr = _rows_tile(r, n * c * 4, budget=6 * 1024 * 1024)
    if a.dtype == BF16 and tr % 16:
        tr = r

    def body(a_ref, o_ref):
        acc = a_ref[0].astype(F32)
        for k in range(1, n):
            acc = acc + a_ref[k].astype(F32)
        o_ref[...] = acc

    return _pcall(
        body, name=name, grid=(r // tr,),
        out_shape=jax.ShapeDtypeStruct((r, c), F32),
        in_specs=[pl.BlockSpec((n, tr, c), lambda i: (0, i, 0))],
        out_specs=pl.BlockSpec((tr, c), lambda i: (i, 0)),
        compiler_params=_params(("arbitrary",)),
    )(a)


def _to_bf16(a, name):
    n, r, c = a.shape

    def body(a_ref, o_ref):
        o_ref[...] = a_ref[...].astype(BF16)

    spec = pl.BlockSpec((1, r, c), lambda i: (i, 0, 0))
    return _pcall(
        body, name=name, grid=(n,), out_shape=jax.ShapeDtypeStruct(a.shape, BF16),
        in_specs=[spec], out_specs=spec, compiler_params=_params(("arbitrary",)),
    )(a)


def _adamw(w, g, m, v, name, pieces=()):
    r, c = w.shape
    tr = _rows_tile(r, c * 4, budget=1024 * 1024)
    c1 = 1.0 - ADAM_B1 ** ADAM_STEP
    c2 = 1.0 - ADAM_B2 ** ADAM_STEP
    ns = len(pieces)
    steps = r // tr

    def body(w_ref, g_ref, m_ref, v_ref, *rest):
        piece_refs, (d_ref, nm_ref, nv_ref), recv_refs = rest[:ns], rest[ns:ns + 3], rest[ns + 3:2 * ns + 3]
        if ns:
            local, remote = _scatter8_copies(piece_refs, recv_refs, *rest[2 * ns + 3:])

            @pl.when(pl.program_id(0) == 0)
            def _():
                for cp in local + remote:
                    cp.start()

        gg = g_ref[...]
        nm = ADAM_B1 * m_ref[...] + (1.0 - ADAM_B1) * gg
        nv = ADAM_B2 * v_ref[...] + (1.0 - ADAM_B2) * (gg * gg)
        m_hat = nm / c1
        v_hat = nv / c2
        d_ref[...] = -ADAM_LR * (m_hat / (jnp.sqrt(v_hat) + ADAM_EPS) + ADAM_WD * w_ref[...])
        nm_ref[...] = nm
        nv_ref[...] = nv

        if ns:
            @pl.when(pl.program_id(0) == steps - 1)
            def _():
                for cp in remote:
                    cp.wait_recv()
                for cp in remote:
                    cp.wait_send()
                for cp in local:
                    cp.wait()

    spec = pl.BlockSpec((tr, c), lambda i: (i, 0))
    hbm = pl.BlockSpec(memory_space=pltpu.HBM)
    scratch = [pltpu.SemaphoreType.DMA((7 * ns,)), pltpu.SemaphoreType.DMA((7 * ns,)), pltpu.SemaphoreType.DMA((ns,))] if ns else []
    return _pcall(
        body, name=name, grid=(steps,),
        out_shape=[jax.ShapeDtypeStruct((r, c), F32)] * 3 + [jax.ShapeDtypeStruct(pc.shape, pc.dtype) for pc in pieces],
        in_specs=[spec] * 4 + [hbm] * ns, out_specs=[spec] * 3 + [hbm] * ns,
        scratch_shapes=scratch,
        compiler_params=_params(("arbitrary",)),
    )(w, g, m, v, *pieces)


def _ada_fwd(c_all, w_shard, b_shard):
    nb, d = c_all.shape
    cols = w_shard.shape[1]

    def body(c_ref, w_ref, b_ref, sc_ref, mod_ref):
        cv = c_ref[...]
        sc = cv * jax.nn.sigmoid(cv)
        sc_ref[...] = sc
        mod_ref[...] = jnp.dot(sc.astype(BF16), w_ref[...].astype(BF16), preferred_element_type=F32) + b_ref[...]

    return _pcall(
        body, name="ada_fwd",
        out_shape=[jax.ShapeDtypeStruct((nb, d), F32), jax.ShapeDtypeStruct((nb, cols), F32)],
        compiler_params=pltpu.CompilerParams(vmem_limit_bytes=VMEM_LIMIT),
    )(c_all, w_shard, b_shard)


def _ada_bwd(sc_t, dmod_cols):
    d, nb = sc_t.shape
    cols = dmod_cols.shape[1]
    tr = _rows_tile(d, cols * 4, budget=1024 * 1024)

    def body(s_ref, m_ref, o_ref):
        s = s_ref[...]
        m = m_ref[...]
        acc = s[:, 0:1] * m[0:1, :]
        for b in range(1, nb):
            acc = acc + s[:, b:b + 1] * m[b:b + 1, :]
        o_ref[...] = acc

    return _pcall(
        body, name="ada_bwd", grid=(d // tr,),
        out_shape=jax.ShapeDtypeStruct((d, cols), F32),
        in_specs=[pl.BlockSpec((tr, nb), lambda i: (i, 0)), pl.BlockSpec((nb, cols), lambda i: (0, 0))],
        out_specs=pl.BlockSpec((tr, cols), lambda i: (i, 0)),
        compiler_params=_params(("arbitrary",)),
    )(sc_t, dmod_cols)


def _log_sigmoid(x):
    return jnp.minimum(x, 0.0) - jnp.log1p(jnp.exp(-jnp.abs(x)))


def _fgate_fwd(fl2d, b_rows, tri_in, tri_blk):
    r = fl2d.shape[0]

    def body(x_ref, b_ref, u_ref, l_ref, f_ref):
        lf = _log_sigmoid(x_ref[...] + b_ref[...])
        c1 = _split_dot(lf, u_ref[...], 3)
        tot = jnp.broadcast_to(c1[:, LANES - 1:LANES], (r, LANES))
        acc = None
        for _ in range(3):
            part = tot.astype(BF16)
            dd = jnp.dot(l_ref[...], part, preferred_element_type=F32)
            acc = dd if acc is None else acc + dd
            tot = tot - part.astype(F32)
        f_ref[...] = c1 + acc

    return _pcall(
        body, name="fgate_fwd", out_shape=jax.ShapeDtypeStruct((r, LANES), F32),
        compiler_params=pltpu.CompilerParams(vmem_limit_bytes=VMEM_LIMIT),
    )(fl2d, b_rows, tri_in, tri_blk)


def _fgate_bwd(fl2d, b_rows, df_query, df_key, tri_in_rev, tri_blk_rev, head_rows):
    r = fl2d.shape[0]
    nhp = head_rows.shape[0]

    def body(x_ref, b_ref, dq_ref, dk_ref, u_ref, l_ref, hr_ref, o_ref, gb_ref):
        c1 = _split_dot(dq_ref[...] + dk_ref[...], u_ref[...], 3)
        tot = jnp.broadcast_to(c1[:, 0:1], (r, LANES))
        acc = None
        for _ in range(3):
            part = tot.astype(BF16)
            dd = jnp.dot(l_ref[...], part, preferred_element_type=F32)
            acc = dd if acc is None else acc + dd
            tot = tot - part.astype(F32)
        x = x_ref[...] + b_ref[...]
        e = jnp.exp(-jnp.abs(x))
        dfl = (c1 + acc) * (jnp.where(x >= 0, e, 1.0) / (1.0 + e))
        o_ref[...] = dfl
        rs = jnp.broadcast_to(jnp.sum(dfl, axis=1, keepdims=True), (r, LANES))
        gb = None
        for _ in range(3):
            part = rs.astype(BF16)
            dd = jnp.dot(hr_ref[...], part, preferred_element_type=F32)
            gb = dd if gb is None else gb + dd
            rs = rs - part.astype(F32)
        gb_ref[...] = gb

    return _pcall(
        body, name="fgate_bwd",
        out_shape=[jax.ShapeDtypeStruct((r, LANES), F32), jax.ShapeDtypeStruct((nhp, LANES), F32)],
        compiler_params=pltpu.CompilerParams(vmem_limit_bytes=VMEM_LIMIT),
    )(fl2d, b_rows, df_query, df_key, tri_in_rev, tri_blk_rev, head_rows)


def _norm_mod(x, g, scale, shift):
    r = lax.rsqrt(jnp.mean(x * x, axis=-1, keepdims=True) + EPS)
    return (x * r * g) * (1.0 + scale) + shift


def _norm_mod_bwd(x, dh, g, scale):
    r = lax.rsqrt(jnp.mean(x * x, axis=-1, keepdims=True) + EPS)
    xn = x * r
    dshift = jnp.sum(dh, axis=0, keepdims=True)
    dscale = jnp.sum(dh * (xn * g), axis=0, keepdims=True)
    dxn_g = dh * (1.0 + scale)
    dg = jnp.sum(dxn_g * xn, axis=0, keepdims=True)
    dxn = dxn_g * g
    dx = r * (dxn - xn * jnp.mean(dxn * xn, axis=-1, keepdims=True))
    return dx, dshift, dscale, dg


def _in_proj_fwd(x, mod8, g_attn, w_qkv, w_f, tm):
    t, d = x.shape
    dg = w_qkv.shape[1] // 6

    def body(x_ref, mod_ref, g_ref, w_ref, wf_ref, qkv_ref, fl_ref, h1_ref):
        h = _norm_mod(x_ref[...], g_ref[...], mod_ref[1:2, :], mod_ref[0:1, :]).astype(BF16)
        h1_ref[...] = _transposed(h)
        fl_ref[...] = lax.dot_general(wf_ref[...], h, NT_DIMS, preferred_element_type=F32)
        for k in range(6):
            cols = slice(k * dg, (k + 1) * dg)
            y = jnp.dot(h, w_ref[:, cols], preferred_element_type=F32)
            qkv_ref[:, cols] = (y * HEAD_DIM ** -0.5 if k in (0, 3) else y).astype(BF16)

    once = lambda a: pl.BlockSpec(a.shape, lambda i: (0,) * a.ndim, pipeline_mode=pl.Buffered(1))
    return _pcall(
        body, name="in_proj_fwd", grid=(t // tm,),
        out_shape=[jax.ShapeDtypeStruct((t, 6 * dg), BF16), jax.ShapeDtypeStruct((LANES, t), F32),
                   jax.ShapeDtypeStruct((d, t), BF16)],
        in_specs=[pl.BlockSpec((tm, d), lambda i: (i, 0)), pl.BlockSpec((8, d), lambda i: (0, 0)),
                  pl.BlockSpec((1, d), lambda i: (0, 0)), once(w_qkv), once(w_f)],
        out_specs=[pl.BlockSpec((tm, 6 * dg), lambda i: (i, 0)), pl.BlockSpec((LANES, tm), lambda i: (0, i)),
                   pl.BlockSpec((d, tm), lambda i: (0, i))],
        compiler_params=_params(("arbitrary",)),
    )(x, mod8, g_attn, w_qkv, w_f)


def _head_rstd(o, bd):
    return lax.rsqrt(_split_dot(o * o, bd, 3) * (1.0 / HEAD_DIM) + EPS)


def _attn_out_fwd(x, o_fox, o_sb, g_fox, g_sb, w_out, mod8, bd, tm):
    t, d = x.shape
    dg = o_fox.shape[0]

    def body(x_ref, of_ref, os_ref, gf_ref, gs_ref, w_ref, mod_ref, bd_ref, x2_ref, mix_ref, mixt_ref):
        of, osb = of_ref[...].T, os_ref[...].T
        mf = (of * _head_rstd(of, bd_ref[...]) * gf_ref[...]).astype(BF16)
        ms = (osb * _head_rstd(osb, bd_ref[...]) * gs_ref[...]).astype(BF16)
        mix_ref[:, :dg] = mf
        mix_ref[:, dg:] = ms
        mixt_ref[:dg, :] = _transposed(mf)
        mixt_ref[dg:, :] = _transposed(ms)
        y = jnp.dot(mf, w_ref[:dg, :], preferred_element_type=F32) + jnp.dot(ms, w_ref[dg:, :], preferred_element_type=F32)
        x2_ref[...] = x_ref[...] + mod_ref[2:3, :] * y

    row = lambda w: pl.BlockSpec((tm, w), lambda i: (i, 0))
    full = lambda a: pl.BlockSpec(a.shape, lambda i: (0,) * a.ndim)
    return _pcall(
        body, name="attn_out_fwd", grid=(t // tm,),
        out_shape=[jax.ShapeDtypeStruct((t, d), F32), jax.ShapeDtypeStruct((t, 2 * dg), BF16),
                   jax.ShapeDtypeStruct((2 * dg, t), BF16)],
        in_specs=[row(d), pl.BlockSpec((dg, tm), lambda i: (0, i)), pl.BlockSpec((dg, tm), lambda i: (0, i)),
                  full(g_fox), full(g_sb), full(w_out), full(mod8), full(bd)],
        out_specs=[row(d), row(2 * dg), pl.BlockSpec((2 * dg, tm), lambda i: (0, i))],
        compiler_params=_params(("arbitrary",)),
    )(x, o_fox, o_sb, g_fox, g_sb, w_out, mod8, bd)


def _attn_out_bwd(dx2, mix, o_fox, o_sb, g_fox, g_sb, w_out, mod8, bd, hsel, tm):
    t, d = dx2.shape
    dg = o_fox.shape[0]

    def body(dx_ref, mix_ref, of_ref, os_ref, gf_ref, gs_ref, w_ref, mod_ref, bd_ref, hs_ref,
             dof_ref, dos_ref, dlt_ref, dxg_ref, part_ref):
        dx = dx_ref[...]
        gate = mod_ref[2:3, :]
        dxg = (dx * gate).astype(BF16)
        dxg_ref[...] = dxg
        mixv = mix_ref[...]
        y = jnp.dot(mixv[:, :dg], w_ref[:dg, :], preferred_element_type=F32)
        y = y + jnp.dot(mixv[:, dg:], w_ref[dg:, :], preferred_element_type=F32)
        part_ref[0] = jnp.zeros((8, d), F32)
        part_ref[0, 0:1, :] = jnp.sum(dx * y, axis=0, keepdims=True)
        for grp, (o_ref, g_ref, do_ref) in enumerate(((of_ref, gf_ref, dof_ref), (os_ref, gs_ref, dos_ref))):
            dmix = lax.dot_general(dxg, w_ref[grp * dg:(grp + 1) * dg, :], NT_DIMS, preferred_element_type=F32)
            o = o_ref[...].T
            r = _head_rstd(o, bd_ref[...])
            n = o * r
            part_ref[0, 1:2, grp * dg:(grp + 1) * dg] = jnp.sum(dmix * n, axis=0, keepdims=True)
            dn = dmix * g_ref[...]
            mh = _split_dot(dn * n, bd_ref[...], 3) * (1.0 / HEAD_DIM)
            do = r * (dn - n * mh)
            do_ref[...] = do.astype(BF16)
            if grp == 0:
                prod, dlt = do * o, None
                for _ in range(3):
                    part = prod.astype(BF16)
                    term = lax.dot_general(hs_ref[...], part, NT_DIMS, preferred_element_type=F32)
                    dlt = term if dlt is None else dlt + term
                    prod = prod - part.astype(F32)
                dlt_ref[...] = dlt

    row = lambda w: pl.BlockSpec((tm, w), lambda i: (i, 0))
    full = lambda a: pl.BlockSpec(a.shape, lambda i: (0,) * a.ndim)
    nt = t // tm
    return _pcall(
        body, name="attn_out_bwd", grid=(nt,),
        out_shape=[jax.ShapeDtypeStruct((t, dg), BF16), jax.ShapeDtypeStruct((t, dg), BF16),
                   jax.ShapeDtypeStruct((LANES, t), F32), jax.ShapeDtypeStruct((t, d), BF16),
                   jax.ShapeDtypeStruct((nt, 8, d), F32)],
        in_specs=[row(d), row(2 * dg), pl.BlockSpec((dg, tm), lambda i: (0, i)), pl.BlockSpec((dg, tm), lambda i: (0, i)),
                  full(g_fox), full(g_sb), full(w_out), full(mod8),
                  full(bd), full(hsel)],
        out_specs=[row(dg), row(dg), pl.BlockSpec((LANES, tm), lambda i: (0, i)), row(d), pl.BlockSpec((1, 8, d), lambda i: (i, 0, 0))],
        compiler_params=_params(("arbitrary",)),
    )(dx2, mix, o_fox, o_sb, g_fox, g_sb, w_out, mod8, bd, hsel)


def _in_proj_bwd(dparts, dfl, w_qkv, w_f, x, dx2, mod8, g_attn, tm):
    t, d = x.shape
    dg = dparts[1].shape[1]

    def body(*refs):
        d_refs = refs[:6]
        dfl_ref, w_ref, wf_ref, x_ref, dx2_ref, mod_ref, g_ref, gx_ref, dp_ref, dflb_ref, part_ref = refs[6:]
        dh = None
        for k in range(6):
            dk = d_refs[k][...].T if k in (0, 3) else d_refs[k][...]
            if k in (0, 3):
                dk = dk * HEAD_DIM ** -0.5
            db = dk.astype(BF16)
            dp_ref[:, k * dg:(k + 1) * dg] = db
            term = lax.dot_general(db, w_ref[:, k * dg:(k + 1) * dg], NT_DIMS, preferred_element_type=F32)
            dh = term if dh is None else dh + term
        dfb = dfl_ref[...].T.astype(BF16)
        dflb_ref[...] = dfb
        dh = dh + jnp.dot(dfb, wf_ref[...], preferred_element_type=F32)
        dx, dshift, dscale, dgn = _norm_mod_bwd(x_ref[...], dh, g_ref[...], mod_ref[1:2, :])
        gx_ref[...] = dx2_ref[...] + dx
        part_ref[0] = jnp.zeros((8, d), F32)
        part_ref[0, 0:1, :] = dshift
        part_ref[0, 1:2, :] = dscale
        part_ref[0, 2:3, :] = dgn

    row = lambda w: pl.BlockSpec((tm, w), lambda i: (i, 0))
    full = lambda a: pl.BlockSpec(a.shape, lambda i: (0,) * a.ndim)
    nt = t // tm
    return _pcall(
        body, name="in_proj_bwd", grid=(nt,),
        out_shape=[jax.ShapeDtypeStruct((t, d), F32), jax.ShapeDtypeStruct((t, 6 * dg), BF16),
                   jax.ShapeDtypeStruct((t, LANES), BF16), jax.ShapeDtypeStruct((nt, 8, d), F32)],
        in_specs=[pl.BlockSpec((dg, tm), lambda i: (0, i)), row(dg), row(dg)] * 2
        + [pl.BlockSpec((LANES, tm), lambda i: (0, i)), full(w_qkv), full(w_f), row(d), row(d), full(mod8), full(g_attn)],
        out_specs=[row(d), row(6 * dg), row(LANES), pl.BlockSpec((1, 8, d), lambda i: (i, 0, 0))],
        compiler_params=_params(("arbitrary",)),
    )(*dparts, dfl, w_qkv, w_f, x, dx2, mod8, g_attn)


def _matmul_tn(a_t, b, name):
    m, t = a_t.shape
    n = b.shape[1]
    tm_ = _tile(m, (1408, 1024, 512, 256, 128))
    tn_ = _tile(n, (1408, 1024, 512, 256, 128))
    tk = _tile(t, (1024, 512, 256, 128))
    nk = t // tk

    def body(a_ref, b_ref, o_ref):
        k = pl.program_id(2)

        @pl.when(k == 0)
        def _():
            o_ref[...] = jnp.zeros_like(o_ref)

        o_ref[...] += jnp.dot(a_ref[...], b_ref[...], preferred_element_type=F32)

    return _pcall(
        body, name=name, grid=(m // tm_, n // tn_, nk),
        out_shape=jax.ShapeDtypeStruct((m, n), F32),
        in_specs=[pl.BlockSpec((tm_, tk), lambda i, j, k: (i, k)), pl.BlockSpec((tk, tn_), lambda i, j, k: (k, j))],
        out_specs=pl.BlockSpec((tm_, tn_), lambda i, j, k: (i, j)),
        compiler_params=_params(("arbitrary", "arbitrary", "arbitrary")),
    )(a_t, b)


HALO = 16


def _conv_taps(up_ext, cw, lo, rows):
    s1 = pltpu.roll(up_ext, 1, 0)
    s2 = pltpu.roll(up_ext, 2, 0)
    u = cw[2:3, :] * up_ext[lo:lo + rows] + cw[1:2, :] * s1[lo:lo + rows] + cw[0:1, :] * s2[lo:lo + rows] + cw[3:4, :]
    return u, s1, s2


def _chunk_major(w, cf):
    d, n = w.shape[0], w.shape[1] // cf
    return jnp.transpose(w.reshape(d, n, cf), (1, 0, 2))


def _ffn_fwd(x2, target, mod8, g_mlp, g_final, wg, wv, cwg, cwv, wd, tm, cf):
    t, d = x2.shape
    dfp = wg.shape[1]
    nt, nc = t // tm, dfp // cf
    hb = tm // HALO
    wg_c, wv_c = _chunk_major(wg, cf), _chunk_major(wv, cf)

    def body(x_ref, xp_ref, tg_ref, mod_ref, g_ref, gf_ref, wg_ref, wv_ref, cg_ref, cv_ref, wd_ref,
             dx3_ref, h2_ref, part_ref, act_sc):
        i = pl.program_id(0)
        xe = jnp.concatenate([xp_ref[...], x_ref[...]], axis=0)
        h = _norm_mod(xe, g_ref[...], mod_ref[4:5, :], mod_ref[3:4, :]).astype(BF16)
        h2_ref[...] = _transposed(h[HALO:])
        first = jnp.where(i > 0, h[:HALO], jnp.zeros_like(h[:HALO]))
        h = jnp.concatenate([first, h[HALO:]], axis=0)

        def up(c):
            return (jnp.dot(h, wg_ref[c], preferred_element_type=F32), jnp.dot(h, wv_ref[c], preferred_element_type=F32))

        def activation(c, ups):
            cols = slice(c * cf, (c + 1) * cf)
            ug, _, _ = _conv_taps(ups[0], cg_ref[:, cols], HALO, tm)
            uv, _, _ = _conv_taps(ups[1], cv_ref[:, cols], HALO, tm)
            act_sc[:, cols] = (ug * jax.nn.sigmoid(ug) * uv).astype(BF16)

        for c0 in range(0, nc, 2):
            group = list(range(c0, min(c0 + 2, nc)))
            ups = [up(c) for c in group]
            for c, u in zip(group, ups):
                activation(c, u)

        y_ffn = jnp.dot(act_sc[...], wd_ref[...], preferred_element_type=F32)
        x3 = x_ref[...] + mod_ref[5:6, :] * y_ffn
        r3 = lax.rsqrt(jnp.mean(x3 * x3, axis=-1, keepdims=True) + EPS)
        xn = x3 * r3
        gf = gf_ref[...]
        diff = xn * gf - tg_ref[...]
        dy = diff * (1.0 / d)
        dxn = dy * gf
        dx3 = r3 * (dxn - xn * jnp.mean(dxn * xn, axis=-1, keepdims=True))
        dx3_ref[...] = dx3
        part_ref[0] = jnp.zeros((8, d), F32)
        part_ref[0, 0:1, :] = jnp.sum(dy * xn, axis=0, keepdims=True)
        part_ref[0, 1:2, :] = jnp.sum(dx3 * y_ffn, axis=0, keepdims=True)
        part_ref[0, 2:3, :] = jnp.sum(diff * diff, axis=0, keepdims=True) * (0.5 / d)

    row = lambda w: pl.BlockSpec((tm, w), lambda i: (i, 0))
    full = lambda a: pl.BlockSpec(a.shape, lambda i: (0,) * a.ndim)
    once = lambda a: pl.BlockSpec(a.shape, lambda i: (0,) * a.ndim, pipeline_mode=pl.Buffered(1))
    return _pcall(
        body, name="ffn_fwd", grid=(nt,),
        out_shape=[jax.ShapeDtypeStruct((t, d), F32), jax.ShapeDtypeStruct((d, t), BF16),
                   jax.ShapeDtypeStruct((nt, 8, d), F32)],
        in_specs=[row(d), pl.BlockSpec((HALO, d), lambda i: (jnp.maximum(i * hb - 1, 0), 0)), row(d),
                  full(mod8), full(g_mlp), full(g_final), once(wg_c), once(wv_c), once(cwg), once(cwv), once(wd)],
        out_specs=[row(d), pl.BlockSpec((d, tm), lambda i: (0, i)), pl.BlockSpec((1, 8, d), lambda i: (i, 0, 0))],
        scratch_shapes=[pltpu.VMEM((tm, dfp), BF16)],
        compiler_params=_params(("arbitrary",)),
    )(x2, x2, target, mod8, g_mlp, g_final, wg_c, wv_c, cwg, cwv, wd)


def _ffn_bwd(x2, dx3, mod8, g_mlp, wg, wv, cwg, cwv, wd, tm, cf):
    t, d = x2.shape
    dfp = wg.shape[1]
    nt, nc = t // tm, dfp // cf
    hb = tm // HALO
    nhb = t // HALO
    n = tm + HALO

    def body(x_ref, xp_ref, xn_ref, dx_ref, dxn_ref, mod_ref, g_ref, wg_ref, wv_ref, cg_ref, cv_ref, wd_ref,
             dx2_ref, dug_ref, duv_ref, act_ref, dxg_ref, part_ref, pcg_ref, pcv_ref):
        i = pl.program_id(0)
        xe = jnp.concatenate([xp_ref[...], x_ref[...], xn_ref[...]], axis=0)
        h = _norm_mod(xe, g_ref[...], mod_ref[4:5, :], mod_ref[3:4, :]).astype(BF16)
        h = jnp.concatenate([jnp.where(i > 0, h[:HALO], jnp.zeros_like(h[:HALO])), h[HALO:]], axis=0)
        dx = dx_ref[...] * mod_ref[5:6, :]
        dxn = jnp.where(i < nt - 1, dxn_ref[...] * mod_ref[5:6, :], 0.0)
        de = jnp.concatenate([dx, dxn], axis=0).astype(BF16)
        dxg_ref[...] = de[:tm]
        pcg_ref[0] = jnp.zeros((8, dfp), F32)
        pcv_ref[0] = jnp.zeros((8, dfp), F32)

        def products(c):
            cols = slice(c * cf, (c + 1) * cf)
            return (jnp.dot(h, wg_ref[:, cols], preferred_element_type=F32), jnp.dot(h, wv_ref[:, cols], preferred_element_type=F32),
                    lax.dot_general(de, wd_ref[cols, :], NT_DIMS, preferred_element_type=F32))

        def back(du, cw, up, s1, s2, pc_ref, cols):
            dup = (cw[2:3, :] * du + cw[1:2, :] * pltpu.roll(du, n - 1, 0) + cw[0:1, :] * pltpu.roll(du, n - 2, 0))[:tm]
            dut = du[:tm]
            pc_ref[0, 0:1, cols] = jnp.sum(dut * s2[HALO:HALO + tm], axis=0, keepdims=True)
            pc_ref[0, 1:2, cols] = jnp.sum(dut * s1[HALO:HALO + tm], axis=0, keepdims=True)
            pc_ref[0, 2:3, cols] = jnp.sum(dut * up[HALO:HALO + tm], axis=0, keepdims=True)
            pc_ref[0, 3:4, cols] = jnp.sum(dut, axis=0, keepdims=True)
            return dup.astype(BF16)

        def chunk(c, prods):
            cols = slice(c * cf, (c + 1) * cf)
            upg, upv, dact = prods
            cg, cv = cg_ref[:, cols], cv_ref[:, cols]
            ug, g1, g2 = _conv_taps(upg, cg, HALO, n)
            uv, v1, v2 = _conv_taps(upv, cv, HALO, n)
            sg = jax.nn.sigmoid(ug)
            sil = ug * sg
            act_ref[cols, :] = _transposed((sil * uv)[:tm].astype(BF16))
            dug_ref[:, cols] = back(dact * uv * (sg * (1.0 + ug * (1.0 - sg))), cg, upg, g1, g2, pcg_ref, cols)
            duv_ref[:, cols] = back(dact * sil, cv, upv, v1, v2, pcv_ref, cols)

        for c0 in range(0, nc, 2):
            group = list(range(c0, min(c0 + 2, nc)))
            prods = [products(c) for c in group]
            for c, pr in zip(group, prods):
                chunk(c, pr)

        dh = (lax.dot_general(dug_ref[...], wg_ref[...], NT_DIMS, preferred_element_type=F32)
              + lax.dot_general(duv_ref[...], wv_ref[...], NT_DIMS, preferred_element_type=F32))
        dxt, dshift, dscale, dgn = _norm_mod_bwd(x_ref[...], dh, g_ref[...], mod_ref[4:5, :])
        dx2_ref[...] = dx_ref[...] + dxt
        part_ref[0] = jnp.zeros((8, d), F32)
        part_ref[0, 0:1, :] = dshift
        part_ref[0, 1:2, :] = dscale
        part_ref[0, 2:3, :] = dgn

    row = lambda w: pl.BlockSpec((tm, w), lambda i: (i, 0))
    prev = pl.BlockSpec((HALO, d), lambda i: (jnp.maximum(i * hb - 1, 0), 0))
    nxt = pl.BlockSpec((HALO, d), lambda i: (jnp.minimum((i + 1) * hb, nhb - 1), 0))
    full = lambda a: pl.BlockSpec(a.shape, lambda i: (0,) * a.ndim)
    once = lambda a: pl.BlockSpec(a.shape, lambda i: (0,) * a.ndim, pipeline_mode=pl.Buffered(1))
    part = lambda w: pl.BlockSpec((1, 8, w), lambda i: (i, 0, 0))
    return _pcall(
        body, name="ffn_bwd", grid=(nt,),
        out_shape=[jax.ShapeDtypeStruct((t, d), F32), jax.ShapeDtypeStruct((t, dfp), BF16),
                   jax.ShapeDtypeStruct((t, dfp), BF16), jax.ShapeDtypeStruct((dfp, t), BF16),
                   jax.ShapeDtypeStruct((t, d), BF16), jax.ShapeDtypeStruct((nt, 8, d), F32),
                   jax.ShapeDtypeStruct((nt, 8, dfp), F32), jax.ShapeDtypeStruct((nt, 8, dfp), F32)],
        in_specs=[row(d), prev, nxt, row(d), nxt, full(mod8), full(g_mlp), once(wg), once(wv), once(cwg), once(cwv), once(wd)],
        out_specs=[row(d), row(dfp), row(dfp), pl.BlockSpec((dfp, tm), lambda i: (0, i)), row(d), part(d), part(dfp), part(dfp)],
        compiler_params=_params(("arbitrary",)),
    )(x2, x2, x2, dx3, dx3, mod8, g_mlp, wg, wv, cwg, cwv, wd)


BLK = 2 * LANES
XROWS = 144
LANE_FS, LANE_FT_A, LANE_FT_B = 0, 3, 6


def _head_masks():
    lane = lax.broadcasted_iota(jnp.int32, (1, LANES), 1)
    in_a = lane < HEAD_DIM
    return in_a, jnp.logical_not(in_a)


def _pieces3(x):
    hi = x.astype(BF16).astype(F32)
    r = x - hi
    mid = r.astype(BF16).astype(F32)
    return hi, mid, (r - mid).astype(BF16).astype(F32)


def _bias_lanes(rows, entries):
    sub = lax.broadcasted_iota(jnp.int32, (16, 1), 0)
    out = jnp.zeros((16, rows), F32)
    for l, v in entries:
        out = jnp.where(sub == l, v, out)
    return jnp.concatenate([out, jnp.zeros((LANES - 16, rows), F32)], axis=0).T


def _three(first, values):
    return [(first + k, v) for k, v in enumerate(values)]


def _stack_rows(x, in_a, in_b):
    zero = jnp.zeros_like(x)
    return jnp.concatenate([jnp.where(in_a, x, zero), jnp.where(in_b, x, zero)], axis=0)


def _transposed(x):
    return x.astype(F32).T.astype(BF16)


def _attn_operands(qkv, frow, dg):
    t = qkv.shape[0]
    p, nk = dg // LANES, t // BLK

    def body(qf_ref, kf_ref, vf_ref, ks_ref, vs_ref, f_ref, qx_ref, kx_ref, kxt_ref, vf_o, vft_o, ks_o, kst_o, vs_o, vst_o):
        in_a, in_b = _head_masks()
        fa, fb = _pieces3(f_ref[0]), _pieces3(f_ref[1])
        qx_ref[0, :, :LANES] = qf_ref[...]
        qx_ref[0, :, LANES:] = _bias_lanes(
            BLK, _three(LANE_FS, (-1.0,) * 3) + _three(LANE_FT_A, fa) + _three(LANE_FT_B, fb)).astype(BF16)
        kf = kf_ref[...]
        zero = jnp.zeros_like(kf)
        top = jnp.concatenate([jnp.where(in_a, kf, zero), _bias_lanes(
            BLK, _three(LANE_FS, fa) + _three(LANE_FT_A, (1.0,) * 3)).astype(BF16)], axis=1)
        bot = jnp.concatenate([jnp.where(in_b, kf, zero), _bias_lanes(
            BLK, _three(LANE_FS, fb) + _three(LANE_FT_B, (1.0,) * 3)).astype(BF16)], axis=1)
        kx = jnp.concatenate([top, bot], axis=0)
        kx_ref[0, 0] = kx
        kxt_ref[0, 0] = _transposed(kx)[:XROWS]
        for src, dst, dst_t in ((vf_ref, vf_o, vft_o), (ks_ref, ks_o, kst_o), (vs_ref, vs_o, vst_o)):
            st = _stack_rows(src[...], in_a, in_b)
            dst[0, 0] = st
            dst_t[0, 0] = _transposed(st)

    col = lambda base: pl.BlockSpec((BLK, LANES), lambda h, j: (j, base * p + h))
    blk4 = lambda r, c: pl.BlockSpec((1, 1, r, c), lambda h, j: (h, j, 0, 0))
    shp4 = lambda r, c: jax.ShapeDtypeStruct((p, nk, r, c), BF16)
    return _pcall(
        body, name="attn_operands", grid=(p, nk),
        out_shape=[jax.ShapeDtypeStruct((p, t, 2 * LANES), BF16), shp4(2 * BLK, 2 * LANES), shp4(XROWS, 2 * BLK)]
        + [shp4(2 * BLK, LANES), shp4(LANES, 2 * BLK)] * 3,
        in_specs=[col(0), col(1), col(2), col(4), col(5), pl.BlockSpec((2, 1, BLK), lambda h, j: (h, 0, j))],
        out_specs=[pl.BlockSpec((1, BLK, 2 * LANES), lambda h, j: (h, j, 0)), blk4(2 * BLK, 2 * LANES), blk4(XROWS, 2 * BLK)]
        + [blk4(2 * BLK, LANES), blk4(LANES, 2 * BLK)] * 3,
        compiler_params=_params(("arbitrary", "arbitrary")),
    )(qkv, qkv, qkv, qkv, qkv, frow)


def _key_query_masks():
    key = lax.broadcasted_iota(jnp.int32, (BLK, BLK), 0)
    qry = lax.broadcasted_iota(jnp.int32, (BLK, BLK), 1)
    return key <= qry, key < qry


def _key_triangle(kind):
    s = lax.broadcasted_iota(jnp.int32, (BLK, BLK), 0)
    j = lax.broadcasted_iota(jnp.int32, (BLK, BLK), 1)
    return {"suffix": j >= s, "prefix": j <= s, "before": j < s}[kind].astype(BF16)


def _tri_dot(tri, x, passes):
    acc = None
    for _ in range(passes):
        part = x.astype(BF16)
        d = jnp.dot(tri, part, preferred_element_type=F32)
        acc = d if acc is None else acc + d
        x = x - part.astype(F32)
    return acc


GROUPS = (4, 2, 1)


def _loop_blocks(n, tiles, carry, descending=False, groups=GROUPS):
    at = (lambda k: n - 1 - k) if descending else (lambda k: k)
    done = 0
    for g in groups:
        left = n - done
        carry = lax.fori_loop(0, left // g, lambda h, c, g=g, done=done: tiles([at(done + g * h + k) for k in range(g)], c), carry)
        done = done + (left // g) * g
    return carry


def _resident(shape):
    return pl.BlockSpec((1,) + shape, lambda h, i: (h,) + (0,) * len(shape), pipeline_mode=pl.Buffered(1))


def _rows_per_head(a, b):
    return jnp.concatenate([jnp.broadcast_to(a, (HEAD_DIM, BLK)), jnp.broadcast_to(b, (HEAD_DIM, BLK))], axis=0)


def _fold_heads(stacked, in_a):
    return jnp.where(in_a, stacked[:BLK], stacked[BLK:])


def _xy_gather_copies(ins, outs, send_sems, recv_sems, local_sems):
    x, y, c = lax.axis_index("x"), lax.axis_index("y"), lax.axis_index("c")
    chips = [(1 - x, y), (x, 1 - y), (1 - x, 1 - y)]
    mine = 2 * x + y
    local, remote = [], []
    for w in range(len(ins)):
        local.append(pltpu.make_async_copy(ins[w], outs[w].at[mine], local_sems.at[w]))
        for k, (px, py) in enumerate(chips):
            remote.append(pltpu.make_async_remote_copy(
                src_ref=ins[w], dst_ref=outs[w].at[mine], send_sem=send_sems.at[3 * w + k],
                recv_sem=recv_sems.at[3 * w + k], device_id=(px, py, c), device_id_type=MESH))
    return local, remote


def _fox_fwd(qx, kx, v_t, dg, shards):
    p, t = qx.shape[0], qx.shape[1]
    nq = t // BLK
    nh = 2 * p
    ns = len(shards)

    def body(q_ref, k_ref, vt_ref, *rest):
        shard_refs, (o_ref, lse_ref), gathered = rest[:ns], rest[ns:ns + 2], rest[ns + 2:2 * ns + 2]
        local, remote = _xy_gather_copies(shard_refs, gathered, *rest[2 * ns + 2:])
        i = pl.program_id(1)

        @pl.when((pl.program_id(0) == 0) & (i == 0))
        def _():
            for cp in local + remote:
                cp.start()

        causal, _ = _key_query_masks()
        q = q_ref[0]

        def scores(j, masked):
            s2 = lax.dot_general(k_ref[0, j], q, NT_DIMS, preferred_element_type=F32)
            s = [s2[a * BLK:(a + 1) * BLK] for a in range(2)]
            return [jnp.where(causal, x, NEG) for x in s] if masked else s

        def update(blocks, carry):
            m, l, acc = list(carry[0]), list(carry[1]), carry[2]
            for j, s in blocks:
                alpha, pr = [], []
                for a in range(2):
                    mn = jnp.maximum(m[a], jnp.max(s[a], axis=0, keepdims=True))
                    pa = jnp.exp(s[a] - mn)
                    al = jnp.exp(m[a] - mn)
                    l[a] = al * l[a] + jnp.sum(pa, axis=0, keepdims=True)
                    m[a] = mn
                    alpha.append(al)
                    pr.append(pa.astype(BF16))
                acc = _rows_per_head(*alpha) * acc + jnp.dot(vt_ref[0, j], jnp.concatenate(pr, axis=0), preferred_element_type=F32)
            return tuple(m), tuple(l), acc

        tiles = lambda js, c: update([(j, scores(j, False)) for j in js], c)
        neg, zero = jnp.full((1, BLK), NEG, F32), jnp.zeros((1, BLK), F32)
        carry = _loop_blocks(jnp.maximum(i - 1, 0), tiles, ((neg, neg), (zero, zero), jnp.zeros((LANES, BLK), F32)),
                             groups=(8, 4, 2, 1))
        paired = jnp.minimum(i, 1)
        carry = lax.fori_loop(0, paired, lambda _, c: update([(i - 1, scores(i - 1, False)), (i, scores(i, True))], c), carry)
        m, l, acc = lax.fori_loop(0, 1 - paired, lambda _, c: update([(i, scores(i, True))], c), carry)
        o_ref[...] = acc / _rows_per_head(*l)
        lse_ref[0] = m[0] + jnp.log(l[0])
        lse_ref[1] = m[1] + jnp.log(l[1])

        @pl.when((pl.program_id(0) == p - 1) & (i == nq - 1))
        def _():
            for cp in remote:
                cp.wait_recv()
            for cp in remote:
                cp.wait_send()
            for cp in local:
                cp.wait()

    row = pl.BlockSpec((2, 1, BLK), lambda h, i: (h, 0, i))
    hbm = pl.BlockSpec(memory_space=pltpu.HBM)
    return _pcall(
        body, name="fox_fwd", grid=(p, nq),
        out_shape=[jax.ShapeDtypeStruct((dg, t), F32), jax.ShapeDtypeStruct((nh, 1, t), F32)]
        + [jax.ShapeDtypeStruct((4,) + s.shape, s.dtype) for s in shards],
        in_specs=[pl.BlockSpec((1, BLK, 2 * LANES), lambda h, i: (h, i, 0)), _resident((nq, 2 * BLK, 2 * LANES)),
                  _resident((nq, LANES, 2 * BLK))] + [hbm] * ns,
        out_specs=[pl.BlockSpec((LANES, BLK), lambda h, i: (h, i)), row] + [hbm] * ns,
        scratch_shapes=[pltpu.SemaphoreType.DMA((3 * ns,)), pltpu.SemaphoreType.DMA((3 * ns,)), pltpu.SemaphoreType.DMA((ns,))],
        compiler_params=_params(("arbitrary", "arbitrary")),
    )(qx, kx, v_t, *shards)


def _swap_copies(ins, outs, send_sems, recv_sems, local_sems):
    x, y, c = lax.axis_index("x"), lax.axis_index("y"), lax.axis_index("c")
    local, remote = [], []
    for w in range(len(ins)):
        local.append(pltpu.make_async_copy(ins[w], outs[w].at[c], local_sems.at[w]))
        remote.append(pltpu.make_async_remote_copy(
            src_ref=ins[w], dst_ref=outs[w].at[c], send_sem=send_sems.at[w], recv_sem=recv_sems.at[w],
            device_id=(x, y, 1 - c), device_id_type=MESH))
    return local, remote


def _fox_bwd(qx, kx, kx_t, v_st, do, delta, lse, dg, halves):
    p, t = qx.shape[0], qx.shape[1]
    nq = t // BLK
    nh = 2 * p
    ns = len(halves)

    def body(q_ref, k_ref, kt_ref, v_ref, do_ref, dl_ref, lse_ref, *rest):
        half_refs, (dq_ref, dft_ref, dk_ref, dv_ref, dkx_ref), both_refs = rest[:ns], rest[ns:ns + 5], rest[ns + 5:2 * ns + 5]
        local, remote = _swap_copies(half_refs, both_refs, *rest[2 * ns + 5:])
        i = pl.program_id(1)

        @pl.when((pl.program_id(0) == 0) & (i == 0))
        def _():
            for cp in local + remote:
                cp.start()

        @pl.when(i == 0)
        def _():
            dk_ref[...] = jnp.zeros_like(dk_ref)
            dv_ref[...] = jnp.zeros_like(dv_ref)
            dkx_ref[...] = jnp.zeros_like(dkx_ref)

        in_a, _ = _head_masks()
        first_lane = lax.broadcasted_iota(jnp.int32, (1, LANES), 1) == 0
        causal, _ = _key_query_masks()
        q, do2 = q_ref[0], do_ref[...]
        dl = (dl_ref[0], dl_ref[1])
        lse = (lse_ref[0], lse_ref[1])

        def products(j):
            return (lax.dot_general(k_ref[0, j], q, NT_DIMS, preferred_element_type=F32),
                    lax.dot_general(v_ref[0, j], do2, NT_DIMS, preferred_element_type=F32))

        def dscores(prod, masked):
            s2, dp2 = prod
            pr, ds = [], []
            for a in range(2):
                s = s2[a * BLK:(a + 1) * BLK]
                if masked:
                    s = jnp.where(causal, s, NEG)
                pa = jnp.exp(s - lse[a])
                ds.append((pa * (dp2[a * BLK:(a + 1) * BLK] - dl[a])).astype(BF16))
                pr.append(pa.astype(BF16))
            return jnp.concatenate(ds, axis=0), jnp.concatenate(pr, axis=0)

        def accumulate(j, dsb, prb, dq):
            off = pl.multiple_of(j * BLK, BLK)
            dk_full = jnp.dot(dsb, q, preferred_element_type=F32)
            dk_ref[pl.ds(off, BLK), :] += _fold_heads(dk_full[:, :LANES], in_a)
            dkx_ref[pl.ds(off, BLK), :] += jnp.where(first_lane, dk_full[:BLK, LANES:], dk_full[BLK:, LANES:])
            dv_ref[pl.ds(off, BLK), :] += _fold_heads(jnp.dot(prb, do2, preferred_element_type=F32), in_a)
            return dq + jnp.dot(kt_ref[0, j], dsb, preferred_element_type=F32)

        def tiles(js, dq, masked=False):
            prods = [products(j) for j in js]
            grads = [dscores(pr, masked) for pr in prods]
            for j, (dsb, prb) in zip(js, grads):
                dq = accumulate(j, dsb, prb, dq)
            return dq

        dq = _loop_blocks(i, tiles, jnp.zeros((XROWS, BLK), F32), groups=(6, 3, 1))
        dq = tiles([i], dq, True)
        dq_ref[...] = dq[:LANES]
        dft_ref[0] = dq[LANES + LANE_FT_A:LANES + LANE_FT_A + 1]
        dft_ref[1] = dq[LANES + LANE_FT_B:LANES + LANE_FT_B + 1]

        @pl.when((pl.program_id(0) == p - 1) & (i == nq - 1))
        def _():
            for cp in remote:
                cp.wait_recv()
            for cp in remote:
                cp.wait_send()
            for cp in local:
                cp.wait()

    row = pl.BlockSpec((2, 1, BLK), lambda h, i: (h, 0, i))
    acc = pl.BlockSpec((t, LANES), lambda h, i: (0, h))
    hbm = pl.BlockSpec(memory_space=pltpu.HBM)
    return _pcall(
        body, name="fox_bwd", grid=(p, nq),
        out_shape=[jax.ShapeDtypeStruct((dg, t), F32), jax.ShapeDtypeStruct((nh, 1, t), F32)] + [jax.ShapeDtypeStruct((t, dg), F32)] * 3
        + [jax.ShapeDtypeStruct((2,) + h.shape, h.dtype) for h in halves],
        in_specs=[pl.BlockSpec((1, BLK, 2 * LANES), lambda h, i: (h, i, 0)), _resident((nq, 2 * BLK, 2 * LANES)),
                  _resident((nq, XROWS, 2 * BLK)), _resident((nq, 2 * BLK, LANES)),
                  pl.BlockSpec((BLK, LANES), lambda h, i: (i, h)), row, row] + [hbm] * ns,
        out_specs=[pl.BlockSpec((LANES, BLK), lambda h, i: (h, i)), row, acc, acc, acc] + [hbm] * ns,
        scratch_shapes=[pltpu.SemaphoreType.DMA((ns,)), pltpu.SemaphoreType.DMA((ns,)), pltpu.SemaphoreType.DMA((ns,))],
        compiler_params=_params(("arbitrary", "arbitrary")),
    )(qx, kx, kx_t, v_st, do, delta, lse, *halves)


def _softplus_of(z):
    return jnp.maximum(z, 0.0) + jnp.log(1.0 + jnp.exp2(jnp.abs(z) * (-LOG2_E)))


def _sb_fwd(qkv, k_st, v_t, dg):
    t = qkv.shape[0]
    p, nq = dg // LANES, t // BLK
    nh = 2 * p

    def body(q_ref, k_ref, vt_ref, o_ref, rt_ref):
        i = pl.program_id(1)
        _, strict = _key_query_masks()
        suffix = _key_triangle("suffix")
        q = q_ref[...]

        def scores(j):
            z2 = lax.dot_general(k_ref[0, j], q, NT_DIMS, preferred_element_type=F32)
            return [z2[a * BLK:(a + 1) * BLK] for a in range(2)]

        def suffix_sums(z, masked):
            out = []
            for a in range(2):
                sp = _softplus_of(z[a])
                if masked:
                    sp = jnp.where(strict, sp, 0.0)
                out.append(_tri_dot(suffix, sp, 2))
            return out

        def weights(z, cs, rest, masked):
            w, rest_new = [], []
            for a in range(2):
                wa = jnp.exp(z[a] - cs[a] - rest[a])
                if masked:
                    wa = jnp.where(strict, wa, 0.0)
                w.append(wa.astype(BF16))
                rest_new.append(rest[a] + cs[a][0:1])
            return jnp.concatenate(w, axis=0), tuple(rest_new)

        def tiles(js, carry, masked=False):
            flags = masked if isinstance(masked, tuple) else (masked,) * len(js)
            rest, acc = carry
            zs = [scores(j) for j in js]
            css = [suffix_sums(z, f) for z, f in zip(zs, flags)]
            ws = []
            for z, cs, f in zip(zs, css, flags):
                w2, rest = weights(z, cs, rest, f)
                ws.append(w2)
            for j, w2 in zip(js, ws):
                acc = acc + jnp.dot(vt_ref[0, j], w2, preferred_element_type=F32)
            return rest, acc

        zero = jnp.zeros((1, BLK), F32)
        carry = ((zero, zero), jnp.zeros((LANES, BLK), F32))
        paired = jnp.minimum(i, 1)
        carry = lax.fori_loop(0, paired, lambda _, c: tiles([i, i - 1], c, (True, False)), carry)
        carry = lax.fori_loop(0, 1 - paired, lambda _, c: tiles([i], c, True), carry)
        rest, acc = _loop_blocks(jnp.maximum(i - 1, 0), tiles, carry, descending=True, groups=(8, 4, 2, 1))
        o_ref[...] = acc
        rt_ref[0] = rest[0]
        rt_ref[1] = rest[1]

    return _pcall(
        body, name="sb_fwd", grid=(p, nq),
        out_shape=[jax.ShapeDtypeStruct((dg, t), F32), jax.ShapeDtypeStruct((nh, 1, t), F32)],
        in_specs=[pl.BlockSpec((BLK, LANES), lambda h, i: (i, 3 * p + h)), _resident((nq, 2 * BLK, LANES)),
                  _resident((nq, LANES, 2 * BLK))],
        out_specs=[pl.BlockSpec((LANES, BLK), lambda h, i: (h, i)), pl.BlockSpec((2, 1, BLK), lambda h, i: (h, 0, i))],
        compiler_params=_params(("arbitrary", "arbitrary")),
    )(qkv, k_st, v_t)


def _scatter8_copies(ins, outs, send_sems, recv_sems, local_sems):
    x, y, c = lax.axis_index("x"), lax.axis_index("y"), lax.axis_index("c")
    me = 4 * x + 2 * y + c
    local, remote = [], []
    for w in range(len(ins)):
        local.append(pltpu.make_async_copy(ins[w].at[me], outs[w].at[me], local_sems.at[w]))
        for f in range(1, 8):
            px = 1 - x if f & 4 else x
            py = 1 - y if f & 2 else y
            pc = 1 - c if f & 1 else c
            remote.append(pltpu.make_async_remote_copy(
                src_ref=ins[w].at[4 * px + 2 * py + pc], dst_ref=outs[w].at[me],
                send_sem=send_sems.at[7 * w + f - 1], recv_sem=recv_sems.at[7 * w + f - 1],
                device_id=(px, py, pc), device_id_type=MESH))
    return local, remote


def _sb_bwd(qkv, k_st, k_t, v_st, do, rtot, dg, pieces):
    t = qkv.shape[0]
    p, nq = dg // LANES, t // BLK
    ns = len(pieces)

    def body(q_ref, k_ref, kt_ref, v_ref, do_ref, rt_ref, *rest):
        piece_refs, (dq_ref, dk_ref, dv_ref), recv_refs = rest[:ns], rest[ns:ns + 3], rest[ns + 3:2 * ns + 3]
        local, remote = _scatter8_copies(piece_refs, recv_refs, *rest[2 * ns + 3:])
        i = pl.program_id(1)

        @pl.when((pl.program_id(0) == 0) & (i == 0))
        def _():
            for cp in local + remote:
                cp.start()

        @pl.when(i == 0)
        def _():
            dk_ref[...] = jnp.zeros_like(dk_ref)
            dv_ref[...] = jnp.zeros_like(dv_ref)

        in_a, _ = _head_masks()
        _, strict = _key_query_masks()
        before_m, prefix_m = _key_triangle("before"), _key_triangle("prefix")
        q, do2 = q_ref[...], do_ref[...]
        rt = (rt_ref[0], rt_ref[1])

        def products(j):
            z2 = lax.dot_general(k_ref[0, j], q, NT_DIMS, preferred_element_type=F32)
            da2 = lax.dot_general(v_ref[0, j], do2, NT_DIMS, preferred_element_type=F32)
            return [z2[a * BLK:(a + 1) * BLK] for a in range(2)], [da2[a * BLK:(a + 1) * BLK] for a in range(2)]

        def softplus_sums(z, masked):
            sp = [_softplus_of(x) for x in z]
            if masked:
                sp = [jnp.where(strict, x, 0.0) for x in sp]
            return sp, [_tri_dot(before_m, x, 2) for x in sp]

        def weight_grads(z, da, sp, pre, before, masked):
            w, g, pg, before_new = [], [], [], []
            for a in range(2):
                wa = jnp.exp(z[a] + (before[a] - rt[a]) + pre[a])
                if masked:
                    wa = jnp.where(strict, wa, 0.0)
                ga = wa * da[a]
                w.append(wa.astype(BF16))
                g.append(ga)
                pg.append(jnp.dot(prefix_m, ga.astype(BF16), preferred_element_type=F32))
                before_new.append(before[a] + pre[a][BLK - 1:BLK] + sp[a][BLK - 1:BLK])
            return jnp.concatenate(w, axis=0), g, pg, tuple(before_new)

        def dlogits(sp, g, pg, gbefore, masked):
            dz, gbefore_new = [], []
            for a in range(2):
                s_incl = gbefore[a] + pg[a]
                dza = (g[a] - s_incl) + jnp.exp2(sp[a] * (-LOG2_E)) * s_incl
                if masked:
                    dza = jnp.where(strict, dza, 0.0)
                dz.append(dza.astype(BF16))
                gbefore_new.append(s_incl[BLK - 1:BLK])
            return jnp.concatenate(dz, axis=0), tuple(gbefore_new)

        def accumulate(j, dzb, wb, dq):
            off = pl.multiple_of(j * BLK, BLK)
            dk_ref[pl.ds(off, BLK), :] += _fold_heads(jnp.dot(dzb, q, preferred_element_type=F32), in_a)
            dv_ref[pl.ds(off, BLK), :] += _fold_heads(jnp.dot(wb, do2, preferred_element_type=F32), in_a)
            return dq + jnp.dot(kt_ref[0, j], dzb, preferred_element_type=F32)

        def tiles(js, carry, masked=False):
            flags = masked if isinstance(masked, tuple) else (masked,) * len(js)
            before, gbefore, dq = carry
            prods = [products(j) for j in js]
            sums = [softplus_sums(z, f) for (z, _), f in zip(prods, flags)]
            grads = []
            for (z, da), (sp, pre), f in zip(prods, sums, flags):
                wb, g, pg, before = weight_grads(z, da, sp, pre, before, f)
                grads.append((wb, g, pg))
            for j, (sp, _), (wb, g, pg), f in zip(js, sums, grads, flags):
                dzb, gbefore = dlogits(sp, g, pg, gbefore, f)
                dq = accumulate(j, dzb, wb, dq)
            return before, gbefore, dq

        zero = jnp.zeros((1, BLK), F32)
        carry = _loop_blocks(jnp.maximum(i - 1, 0), tiles, ((zero, zero), (zero, zero), jnp.zeros((LANES, BLK), F32)),
                             groups=(3, 2, 1))
        paired = jnp.minimum(i, 1)
        carry = lax.fori_loop(0, paired, lambda _, c: tiles([i - 1, i], c, (False, True)), carry)
        carry = lax.fori_loop(0, 1 - paired, lambda _, c: tiles([i], c, True), carry)
        dq_ref[...] = carry[2]

        @pl.when((pl.program_id(0) == p - 1) & (i == nq - 1))
        def _():
            for cp in remote:
                cp.wait_recv()
            for cp in remote:
                cp.wait_send()
            for cp in local:
                cp.wait()

    acc = pl.BlockSpec((t, LANES), lambda h, i: (0, h))
    hbm = pl.BlockSpec(memory_space=pltpu.HBM)
    return _pcall(
        body, name="sb_bwd", grid=(p, nq),
        out_shape=[jax.ShapeDtypeStruct((dg, t), F32)] + [jax.ShapeDtypeStruct((t, dg), F32)] * 2
        + [jax.ShapeDtypeStruct(pc.shape, pc.dtype) for pc in pieces],
        in_specs=[pl.BlockSpec((BLK, LANES), lambda h, i: (i, 3 * p + h)), _resident((nq, 2 * BLK, LANES)),
                  _resident((nq, LANES, 2 * BLK)), _resident((nq, 2 * BLK, LANES)),
                  pl.BlockSpec((BLK, LANES), lambda h, i: (i, h)), pl.BlockSpec((2, 1, BLK), lambda h, i: (h, 0, i))] + [hbm] * ns,
        out_specs=[pl.BlockSpec((LANES, BLK), lambda h, i: (h, i)), acc, acc] + [hbm] * ns,
        scratch_shapes=[pltpu.SemaphoreType.DMA((7 * ns,)), pltpu.SemaphoreType.DMA((7 * ns,)), pltpu.SemaphoreType.DMA((ns,))],
        compiler_params=_params(("arbitrary", "arbitrary")),
    )(qkv, k_st, k_t, v_st, do, rtot, *pieces)


def _tri_constants(nh, t):
    nb = t // LANES
    r = nh * nb
    li = np.arange(LANES)
    tri_in = (li[:, None] <= li[None, :])
    ri = np.arange(r)
    same = (ri[:, None] // nb) == (ri[None, :] // nb)
    blk = same & (ri[None, :] < ri[:, None])
    blk_rev = same & (ri[None, :] > ri[:, None])
    head_rows = (np.arange(max(8, nh))[:, None] == (ri[None, :] // nb))
    as_bf16 = lambda a: jnp.asarray(a.astype(np.float32), BF16)
    return as_bf16(tri_in), as_bf16(blk), as_bf16(tri_in.T), as_bf16(blk_rev), as_bf16(head_rows)


def kernel(x, c, w_ada, b_ada, g_attn, w_in, b_fgate, g_out_fox, g_out_sb, w_out, g_mlp, w_up, conv_w, conv_b, w_down, g_final, loss_target, m_w_ada, m_b_ada, m_g_attn, m_w_in, m_b_fgate, m_g_out_fox, m_g_out_sb, m_w_out, m_g_mlp, m_w_up, m_conv_w, m_conv_b, m_w_down, m_g_final, v_w_ada, v_b_ada, v_g_attn, v_w_in, v_b_fgate, v_g_out_fox, v_g_out_sb, v_w_out, v_g_mlp, v_w_up, v_conv_w, v_conv_b, v_w_down, v_g_final):
    t, d = x.shape[1], x.shape[2]
    dg = d // 2
    nh = dg // HEAD_DIM
    n_in = 6 * dg + nh
    dff = w_down.shape[1] * 4
    dfp = -(-dff // 256) * 256
    cf = 256
    tm = _tile(t, (512, 256, 128))
    nq = t // BLK
    xi, yi, ci = lax.axis_index("x"), lax.axis_index("y"), lax.axis_index("c")
    shard = 2 * xi + yi
    me = 4 * xi + 2 * yi + ci

    x2d, tg2d = x[0], loss_target[0]

    c_all = _all_gather8(jnp.pad(c, ((0, 7), (0, 0)))).reshape(8, 8, d)[:, 0, :]
    ada_cols = w_ada.shape[2]
    b_shard = lax.dynamic_slice(b_ada, (0, shard * ada_cols), (1, ada_cols))
    sc_all, mod_shard = _ada_fwd(c_all, w_ada[0], b_shard)
    mod_all = _all_gather8(mod_shard).reshape(4, 2, 8, ada_cols)
    mod_me = lax.dynamic_index_in_dim(mod_all[:, 0], me, axis=1, keepdims=False)
    mod8 = jnp.pad(mod_me.reshape(6, d), ((0, 2), (0, 0)))

    (g_in,) = _gather_xy([w_in[0].astype(BF16)])
    later_shards = [w_out[0].astype(BF16), w_up[0].astype(BF16), w_down[0].astype(BF16), conv_w[0]]
    w_in_full = jnp.transpose(g_in, (1, 0, 2)).reshape(d, n_in)
    w_qkv = w_in_full[:, :6 * dg]
    w_f = jnp.pad(w_in_full[:, 6 * dg:].T, ((0, LANES - nh), (0, 0)))

    qkv, fl, h1 = _in_proj_fwd(x2d, mod8, g_attn, w_qkv, w_f, tm)
    tri_in, tri_blk, tri_in_rev, tri_blk_rev, head_rows = _tri_constants(nh, t)
    fl2d = fl[:nh].reshape(nh * t // LANES, LANES)
    b_rows = jnp.repeat(b_fgate[0], t // LANES)[:, None]
    f2d = _fgate_fwd(fl2d, b_rows, tri_in, tri_blk)
    frow = f2d.reshape(nh, 1, t)
    pairs = nh // 2
    qx, kx, kx_t, vf_st, vf_t, ks_st, ks_t, vs_st, vs_t = _attn_operands(qkv, frow, dg)
    o_fox, lse, g_out, g_up, g_down, g_cw = _fox_fwd(qx, kx, vf_t, dg, later_shards)
    w_out_full = g_out.reshape(2 * dg, d)
    w_up_full = jnp.transpose(g_up, (1, 0, 2)).reshape(d, 2 * dff)
    padc = ((0, 0), (0, dfp - dff))
    wg, wv = jnp.pad(w_up_full[:, :dff], padc), jnp.pad(w_up_full[:, dff:], padc)
    wd = jnp.pad(g_down.reshape(dff, d), ((0, dfp - dff), (0, 0)))
    cw_full = jnp.transpose(g_cw, (1, 0, 2)).reshape(3, 2 * dff)
    cw4 = jnp.concatenate([cw_full, conv_b], axis=0)
    cwg = jnp.pad(cw4[:, :dff], ((0, 4), (0, dfp - dff)))
    cwv = jnp.pad(cw4[:, dff:], ((0, 4), (0, dfp - dff)))
    o_sb, rtot = _sb_fwd(qkv, ks_st, vs_t, dg)
    li = np.arange(dg)
    bd = jnp.asarray((li[:, None] // HEAD_DIM == li[None, :] // HEAD_DIM).astype(np.float32), BF16)
    hsel = jnp.asarray((np.arange(LANES)[:, None] == li[None, :] // HEAD_DIM).astype(np.float32), BF16)
    x2, mix, mix_t = _attn_out_fwd(x2d, o_fox, o_sb, g_out_fox, g_out_sb, w_out_full, mod8, bd, tm)
    g_final2 = g_final[None, :]
    dx3, h2, part_f = _ffn_fwd(x2, tg2d, mod8, g_mlp, g_final2, wg, wv, cwg, cwv, wd, tm, cf)

    tm_ffn_bwd = _tile(t, (256, 128))
    dx2, dupg, dupv, act, dxg3, part_b, pcg, pcv = _ffn_bwd(x2, dx3, mod8, g_mlp, wg, wv, cwg, cwv, wd, tm_ffn_bwd, cf)
    do_fox, do_sb, delta, dxg2, part_o = _attn_out_bwd(dx2, mix, o_fox, o_sb, g_out_fox, g_out_sb, w_out_full, mod8, bd, hsel, tm)
    drow = delta[:nh].reshape(nh, 1, t)

    def col_pieces(g):
        r, cc = g.shape
        return jnp.transpose(g.reshape(2, r // 2, 4, cc // 4), (2, 0, 1, 3)).reshape(8, r // 2, cc // 4)

    def row_pieces(g):
        r, cc = g.shape
        return g.reshape(8, r // 8, cc)

    gw_out = _matmul_tn(mix_t, dxg2, "grad_w_out")
    gw_upg = _matmul_tn(h2, dupg, "grad_w_up_gate")
    gw_upv = _matmul_tn(h2, dupv, "grad_w_up_val")
    gw_up = jnp.concatenate([gw_upg[:, :dff], gw_upv[:, :dff]], axis=1)
    gw_down = _matmul_tn(act, dxg3, "grad_w_down")[:dff]
    early = (row_pieces(gw_out), col_pieces(gw_up), row_pieces(gw_down))
    early = [_to_bf16(pc, "pieces_bf16_" + nm) for pc, nm in zip(early, ("w_out", "w_up", "w_down"))]

    dq_s, dk_s, dv_s, *recv_early = _sb_bwd(qkv, ks_st, ks_t, vs_st, do_sb, rtot, dg, early)
    halves_early = [_sum_leading(rv, nm) for rv, nm in zip(recv_early, ("sum_w_out", "sum_w_up", "sum_w_down"))]
    dq_f, dft, dk_f, dv_f, dkx, *swapped_early = _fox_bwd(qx, kx, kx_t, vf_st, do_fox, drow, lse, dg, halves_early)
    f2d_shape = (nh * t // LANES, LANES)
    dfs = jnp.transpose(dkx.reshape(t, pairs, LANES)[:, :, :2], (1, 2, 0))
    dfl2d, gb8 = _fgate_bwd(fl2d, b_rows, dft.reshape(f2d_shape), dfs.reshape(f2d_shape), tri_in_rev, tri_blk_rev, head_rows)
    dfl = jnp.pad(dfl2d.reshape(nh, t), ((0, LANES - nh), (0, 0)))
    grad_x, dproj, dflb, part_i = _in_proj_bwd([dq_f, dk_f, dv_f, dq_s, dk_s, dv_s], dfl, w_qkv, w_f, x2d, dx2, mod8, g_attn, tm)

    gw_qkv = _matmul_tn(h1, dproj, "grad_w_qkv")
    gw_f = _matmul_tn(h1, dflb, "grad_w_f")
    gw_in = jnp.concatenate([gw_qkv, gw_f[:, :nh]], axis=1)

    sf = _sum_leading(part_f, "sum_part_ffn_fwd")
    sb_ = _sum_leading(part_b, "sum_part_ffn_bwd")
    so = _sum_leading(part_o, "sum_part_attn_out")
    si = _sum_leading(part_i, "sum_part_in_proj")
    scg = _sum_leading(pcg, "sum_part_conv_gate")
    scv = _sum_leading(pcv, "sum_part_conv_val")
    gb_f = gb8[:nh, 0]
    dmod = jnp.concatenate([si[0], si[1], so[0], sb_[0], sb_[1], sf[1]])
    g_conv_w = jnp.concatenate([scg[0:3, :dff], scv[0:3, :dff]], axis=1).reshape(-1)
    g_conv_b = jnp.concatenate([scg[3, :dff], scv[3, :dff]])
    loss_part = jnp.sum(sf[2])
    fields = [dmod, si[2], gb_f, so[1, :dg], so[1, dg:], sb_[2], g_conv_b, sf[0], g_conv_w, loss_part[None]]
    sizes = [int(f.shape[0]) for f in fields]
    n_pack = sum(sizes)
    lanes_pack = -(-n_pack // (8 * LANES)) * LANES
    pack = jnp.pad(jnp.concatenate(fields), (0, 8 * lanes_pack - n_pack)).reshape(8, lanes_pack)
    gathered = _all_gather8(pack)
    tot = _sum_leading(gathered.reshape(8, 8, lanes_pack), "sum_pack").reshape(-1)
    offs = np.concatenate([[0], np.cumsum(sizes)])
    take = lambda k: tot[int(offs[k]):int(offs[k + 1])]
    g_b_ada, g_g_attn, g_b_fgate, g_g_fox, g_g_sb, g_g_mlp, g_cb, g_g_final, g_cw_full, loss_v = [take(k) for k in range(10)]
    loss = loss_v[0]
    dmod_all = gathered.reshape(8, 8 * lanes_pack)[:, :6 * d]
    dmod_cols = lax.dynamic_slice(dmod_all, (0, shard * ada_cols), (8, ada_cols))
    g_w_ada = _ada_bwd(sc_all.T, dmod_cols)

    g_w_out, g_w_up, g_w_down = [s.reshape(2 * s.shape[1], s.shape[2]) for s in swapped_early]
    g_conv_w_shard = lax.dynamic_slice(g_cw_full.reshape(3, 2 * dff), (0, shard * (dff // 2)), (3, dff // 2))

    grads, deltas, new_m, new_v = {}, {}, {}, {}

    def step(name, w, g, m, v, pieces=()):
        shape = w.shape
        as2d = lambda a: a.reshape(-1, shape[-1])
        dl, nm, nv, *received = _adamw(as2d(w), as2d(g), as2d(m), as2d(v), "adamw_" + name, pieces)
        grads[name], deltas[name], new_m[name], new_v[name] = g.reshape(shape), dl.reshape(shape), nm.reshape(shape), nv.reshape(shape)
        return received

    (recv_in,) = step("w_down", w_down, g_w_down, m_w_down, v_w_down, [_to_bf16(col_pieces(gw_in), "pieces_bf16_w_in")])
    (swapped_in,) = _swap_halves([_sum_leading(recv_in, "sum_w_in")])
    g_w_in = swapped_in.reshape(2 * swapped_in.shape[1], swapped_in.shape[2])
    step("w_ada", w_ada, g_w_ada, m_w_ada, v_w_ada)
    step("w_in", w_in, g_w_in, m_w_in, v_w_in)
    step("w_out", w_out, g_w_out, m_w_out, v_w_out)
    step("w_up", w_up, g_w_up, m_w_up, v_w_up)
    step("conv_w", conv_w, g_conv_w_shard, m_conv_w, v_conv_w)

    small = [("b_ada", b_ada, g_b_ada, m_b_ada, v_b_ada), ("g_attn", g_attn, g_g_attn, m_g_attn, v_g_attn),
             ("b_fgate", b_fgate, g_b_fgate, m_b_fgate, v_b_fgate), ("g_out_fox", g_out_fox, g_g_fox, m_g_out_fox, v_g_out_fox),
             ("g_out_sb", g_out_sb, g_g_sb, m_g_out_sb, v_g_out_sb), ("g_mlp", g_mlp, g_g_mlp, m_g_mlp, v_g_mlp),
             ("conv_b", conv_b, g_cb, m_conv_b, v_conv_b), ("g_final", g_final, g_g_final, m_g_final, v_g_final)]
    ssz = [int(np.prod(s[1].shape)) for s in small]
    n_small = sum(ssz)
    lanes_small = -(-n_small // (8 * LANES)) * LANES
    packs = [jnp.pad(jnp.concatenate([s[k].reshape(-1) for s in small]), (0, 8 * lanes_small - n_small)).reshape(8, lanes_small)
             for k in (1, 2, 3, 4)]
    dl_s, nm_s, nv_s = _adamw(*packs, "adamw_small")
    so_ = np.concatenate([[0], np.cumsum(ssz)])
    for k, s in enumerate(small):
        cut = lambda a: a.reshape(-1)[int(so_[k]):int(so_[k + 1])].reshape(s[1].shape)
        grads[s[0]], deltas[s[0]], new_m[s[0]], new_v[s[0]] = s[2].reshape(s[1].shape), cut(dl_s), cut(nm_s), cut(nv_s)

    order = ["w_ada", "b_ada", "g_attn", "w_in", "b_fgate", "g_out_fox", "g_out_sb", "w_out", "g_mlp", "w_up",
             "conv_w", "conv_b", "w_down", "g_final"]
    return (loss, grad_x[None], *[grads[n] for n in order], *[deltas[n] for n in order],
            *[new_m[n] for n in order], *[new_v[n] for n in order])
```
